```python
import math
import jax, jax.numpy as jnp
from jax import lax
import numpy as np

D_MODEL = 1024
BATCH = 32
SEQ = 256
DEPTH = 2
DEC_BATCH = 8
DEC_SEQ = 2048
PAST_LEN = 512

GRID_W = 64
EPS = 1e-6
LOG_FLOOR = 1e-30
MASK_NEG = -1e30
N_BRANCH = 4
D_BRANCH = 512
SHORT_CONV_W = 3
HY_ORDER = 2
HY_POS_FREQS = 16
HY_POS_DIM = 1 + 2 * HY_POS_FREQS
HY_FILTER_HIDDEN = 64
HY_SIN_FREQ = 1.0
HY_FAST_DECAY = 0.3
HY_SLOW_DECAY = 1.5
HY_DECAY_TARGET = 1e-2
HG_HEADS = 4
HG_DK = 128
HG_DV = 128
HG_CHUNK = 16
SSD_HEADS = 8
SSD_P = 64
SSD_GROUPS = 2
SSD_N = 128
SSD_CHUNK = 64
GDN_HEADS = 4
GDN_DK = 128
GDN_DV = 128
GDN_CHUNK = 64
N_EXPERTS = 64
TOP_K = 8
N_EXPERT_GROUPS = 8
TOPK_GROUPS = 4
D_EXPERT = 256
D_SHARED = 256
ROUTED_SCALE = 2.5
MOE_BLOCK = 128

CONV_SPLITS = (D_BRANCH, D_BRANCH, D_BRANCH,
               SSD_HEADS * SSD_P, SSD_GROUPS * SSD_N, SSD_GROUPS * SSD_N,
               GDN_HEADS * GDN_DK, GDN_HEADS * GDN_DK, GDN_HEADS * GDN_DV)
D_CONV = sum(CONV_SPLITS)
PLAIN_SPLITS = (HG_HEADS * HG_DK, HG_HEADS * HG_DK, HG_HEADS * HG_DK, HG_HEADS * HG_DV, HG_HEADS * HG_DV,
                SSD_HEADS * SSD_P, SSD_HEADS, SSD_HEADS,
                GDN_HEADS * GDN_DV, GDN_HEADS, GDN_HEADS, GDN_HEADS, GDN_HEADS,
                N_BRANCH * D_MODEL)
D_IN = D_CONV + sum(PLAIN_SPLITS)

kernel_name = "bidir_hybrid_hyena_hgrn2_ssd_gdn_moe_dit"

F32 = jnp.float32


def _split(a, widths):
    cuts = [int(i) for i in np.cumsum(widths)[:-1]]
    return jnp.split(a, cuts, axis=-1)


def _rmsnorm(x, g):
    xf = x.astype(F32)
    return (xf * lax.rsqrt(jnp.mean(xf * xf, axis=-1, keepdims=True) + EPS)).astype(x.dtype) * g


def _l2norm(a):
    af = a.astype(F32)
    return af * lax.rsqrt(jnp.sum(af * af, axis=-1, keepdims=True) + EPS)


def _masked_exp(mask, diff):
    return jnp.where(mask, jnp.exp(jnp.where(mask, diff, 0.0)), 0.0)


def _short_conv(u, w, b, on_grid):
    bsz, L, ch = u.shape
    seq = u.reshape(bsz * (L // GRID_W), GRID_W, ch) if on_grid else u
    n = seq.shape[1]
    pad = SHORT_CONV_W // 2
    sp = jnp.pad(seq, ((0, 0), (pad, pad), (0, 0)))
    out = b + sp[:, 0:n] * w[0]
    for j in range(1, SHORT_CONV_W):
        out = out + sp[:, j:j + n] * w[j]
    return out.reshape(bsz, L, ch)


def _hyena_filters(L, w1, b1, w2, b2, w3):
    t = jnp.arange(L, dtype=F32) / L
    bands = jnp.arange(1, HY_POS_FREQS + 1, dtype=F32)
    ang = 2.0 * math.pi * t[:, None] * bands
    feats = jnp.concatenate([t[:, None], jnp.sin(ang), jnp.cos(ang)], axis=-1)
    hdn = jnp.sin(HY_SIN_FREQ * (feats @ w1 + b1))
    hdn = jnp.sin(HY_SIN_FREQ * (hdn @ w2 + b2))
    filt = (hdn @ w3).astype(F32).reshape(L, 2, HY_ORDER, D_BRANCH)
    max_decay = math.log(HY_DECAY_TARGET) / HY_FAST_DECAY
    min_decay = math.log(HY_DECAY_TARGET) / HY_SLOW_DECAY
    deltas = jnp.linspace(min_decay, max_decay, D_BRANCH, dtype=F32)
    window = jnp.exp(-t[:, None] * jnp.abs(deltas))
    return filt * window[:, None, None, :]


def _fft_bidir_conv(u, h_causal, h_anti, bias):
    L, ch = u.shape[1], u.shape[-1]
    two_sided = jnp.concatenate([h_causal, jnp.zeros((1, ch), F32), h_anti[:0:-1]], axis=0)
    hf = jnp.fft.rfft(two_sided, axis=0)
    uf32 = u.astype(F32)
    uf = jnp.fft.rfft(uf32, n=2 * L, axis=1)
    y = jnp.fft.irfft(uf * hf[None], n=2 * L, axis=1)[:, :L]
    return (y + uf32 * bias).astype(u.dtype)


def _hyena(v, x1, x2, p):
    L = v.shape[1]
    filt = _hyena_filters(L, p["hy_w1"], p["hy_b1"], p["hy_w2"], p["hy_b2"], p["hy_w3"])
    z = x1 * _fft_bidir_conv(v, filt[:, 0, 0], filt[:, 1, 0], p["hy_bias"][0])
    return x2 * _fft_bidir_conv(z, filt[:, 0, 1], filt[:, 1, 1], p["hy_bias"][1])


def _chunk(a, size):
    bsz, L, h = a.shape[:3]
    a = a.astype(F32).reshape((bsz, L // size, size, h) + a.shape[3:])
    return jnp.moveaxis(a, (1, 2), (0, 3))


def _unchunk(a):
    a = jnp.moveaxis(a, (0, 3), (1, 2))
    return a.reshape((a.shape[0], a.shape[1] * a.shape[2]) + a.shape[3:])


def _hgrn2_chunked(q, v, k, log_f, s0):
    qc, vc, kc, gc = (_chunk(a, HG_CHUNK) for a in (q, v, k, log_f))
    incl = jnp.tril(jnp.ones((HG_CHUNK, HG_CHUNK), bool))

    def step(s, inp):
        qi, vi, ki, gi = inp
        b = jnp.cumsum(gi, axis=2)
        pair = _masked_exp(incl[:, :, None], b[:, :, :, None] - b[:, :, None])
        attn = jnp.einsum("bhtk,bhsk,bhtsk->bhts", qi, ki, pair)
        o = jnp.einsum("bhtk,bhkv->bhtv", qi * jnp.exp(b), s) + jnp.einsum("bhts,bhsv->bhtv", attn, vi)
        b_end = b[:, :, -1:]
        s = jnp.exp(b_end)[:, :, 0, :, None] * s + jnp.einsum("bhsk,bhsv->bhkv", ki * jnp.exp(b_end - b), vi)
        return s, o

    s_fin, o = lax.scan(step, s0.astype(F32), (qc, vc, kc, gc))
    return _unchunk(o), s_fin


def _ssd_chunked(x, bm, cm, dt, log_a, s0):
    xc, bc, cc, dtc, lac = (_chunk(a, SSD_CHUNK) for a in (x, bm, cm, dt, log_a))
    incl = jnp.tril(jnp.ones((SSD_CHUNK, SSD_CHUNK), bool))

    def step(s, inp):
        xi, bi, ci, dti, lai = inp
        cum = jnp.cumsum(lai, axis=-1)
        lmat = _masked_exp(incl, cum[..., :, None] - cum[..., None, :])
        scores = jnp.einsum("bhtn,bhsn->bhts", ci, bi) * lmat
        xdt = xi * dti[..., None]
        y = jnp.einsum("bhts,bhsp->bhtp", scores, xdt) + jnp.einsum("bhtn,bhpn->bhtp", ci * jnp.exp(cum)[..., None], s)
        s = jnp.exp(cum[..., -1])[..., None, None] * s + jnp.einsum(
            "bhsp,bhsn->bhpn", xdt * jnp.exp(cum[..., -1:] - cum)[..., None], bi)
        return s, y

    s_fin, y = lax.scan(step, s0.astype(F32), (xc, bc, cc, dtc, lac))
    return _unchunk(y), s_fin


def _gdn_chunked(q, k, v, log_alpha, beta, s0):
    qc, kc, vc, gl, bc = (_chunk(a, GDN_CHUNK) for a in (q, k, v, log_alpha, beta))
    incl = jnp.tril(jnp.ones((GDN_CHUNK, GDN_CHUNK), bool))
    strict = jnp.tril(jnp.ones((GDN_CHUNK, GDN_CHUNK), bool), -1)
    g = jnp.cumsum(gl, axis=-1)
    decay = _masked_exp(incl, g[..., :, None] - g[..., None, :])
    kb = kc * bc[..., None]
    a_strict = jnp.where(strict, jnp.einsum("nbhtk,nbhsk->nbhts", kb, kc) * decay, 0.0)
    u = lax.linalg.triangular_solve(a_strict, vc * bc[..., None], left_side=True, lower=True, unit_diagonal=True)
    w = lax.linalg.triangular_solve(a_strict, kb * jnp.exp(g)[..., None], left_side=True, lower=True, unit_diagonal=True)
    qs = qc * (GDN_DK ** -0.5)
    aqk = jnp.einsum("nbhtk,nbhsk->nbhts", qs, kc) * decay

    def step(s, inp):
        qi, ki, ui, wi, gi, ai = inp
        v_new = ui - jnp.einsum("bhtk,bhkv->bhtv", wi, s)
        o = jnp.einsum("bhtk,bhkv->bhtv", qi * jnp.exp(gi)[..., None], s) + jnp.einsum("bhts,bhsv->bhtv", ai, v_new)
        g_end = gi[..., -1:]
        s = jnp.exp(g_end)[..., None] * s + jnp.einsum("bhsk,bhsv->bhkv", ki * jnp.exp(g_end - gi)[..., None], v_new)
        return s, o

    s_fin, o = lax.scan(step, s0.astype(F32), (qs, kc, u, w, g, aqk))
    return _unchunk(o), s_fin


def _scan_both_directions(scan_fn, shared, fwd_only, bwd_only, s0):
    flip = lambda a: jnp.flip(a, axis=1)
    o_f, s_f = scan_fn(*shared, *fwd_only, s0[:, 0])
    o_b, s_b = scan_fn(*map(flip, shared), *map(flip, bwd_only), s0[:, 1])
    return o_f + flip(o_b), jnp.stack([s_f, s_b], axis=1)


def _hgrn_lower_bounds(hg_lb):
    pr = jax.nn.softmax(hg_lb.astype(F32), axis=1)
    lb = jnp.cumsum(pr, axis=1) - pr[:, :1]
    return lb.reshape(2, DEPTH, HG_HEADS, HG_DK)


def _token_mixers(h, st_hg, st_ssd, st_gdn, p, lb, on_grid):
    bsz, L, _ = h.shape
    heads = lambda a, d: a.reshape(bsz, L, -1, d)
    proj = h @ p["w_in"]
    conv_out = _short_conv(proj[..., :D_CONV], p["conv_w"], p["conv_b"], on_grid)
    hy_v, hy_x1, hy_x2, ssd_x, ssd_b, ssd_c, gdn_q, gdn_k, gdn_v = _split(conv_out, CONV_SPLITS)
    (hg_q, hg_f_fwd, hg_f_bwd, hg_v, hg_g, ssd_z, ssd_dt_fwd, ssd_dt_bwd,
     gdn_g, gdn_a_fwd, gdn_a_bwd, gdn_b_fwd, gdn_b_bwd, merge) = _split(proj[..., D_CONV:], PLAIN_SPLITS)

    y_a = _hyena(hy_v, hy_x1, hy_x2, p)

    def hg_gates(f_raw, lb_dir):
        uf = heads(f_raw, HG_DK).astype(F32)
        f = lb_dir + (1.0 - lb_dir) * jax.nn.sigmoid(uf)
        log_f = jnp.log(jnp.maximum(f, LOG_FLOOR))
        return (1.0 - lb_dir) * jax.nn.sigmoid(-uf), log_f
    o_hg, s_hg = _scan_both_directions(
        _hgrn2_chunked, (heads(jax.nn.silu(hg_q), HG_DK), heads(hg_v, HG_DV)),
        hg_gates(hg_f_fwd, lb[0]), hg_gates(hg_f_bwd, lb[1]), st_hg)
    y_b = (_rmsnorm(o_hg.astype(h.dtype), p["hg_norm"]) * jax.nn.silu(heads(hg_g, HG_DV))).reshape(bsz, L, D_BRANCH)

    x_s = heads(jax.nn.silu(ssd_x), SSD_P)
    rep = SSD_HEADS // SSD_GROUPS
    bm = jnp.repeat(heads(jax.nn.silu(ssd_b), SSD_N), rep, axis=2)
    cm = jnp.repeat(heads(jax.nn.silu(ssd_c), SSD_N), rep, axis=2)
    def ssd_gates(dt_raw, d):
        dt = jax.nn.softplus(dt_raw.astype(F32) + p["ssd_dt_bias"][d])
        return dt, -jnp.exp(p["ssd_a_log"][d].astype(F32)) * dt
    y_ssd, s_ssd = _scan_both_directions(_ssd_chunked, (x_s, bm, cm), ssd_gates(ssd_dt_fwd, 0), ssd_gates(ssd_dt_bwd, 1), st_ssd)
    y_ssd = y_ssd.astype(h.dtype) + x_s * p["ssd_d"][:, None]
    gated = (y_ssd.reshape(bsz, L, D_BRANCH) * jax.nn.silu(ssd_z)).reshape(bsz, L, SSD_GROUPS, -1)
    y_c = _rmsnorm(gated, p["ssd_norm"].reshape(SSD_GROUPS, -1)).reshape(bsz, L, D_BRANCH)

    def gdn_gates(a_raw, b_raw, d):
        log_alpha = -jnp.exp(p["gdn_a_log"][d].astype(F32)) * jax.nn.softplus(a_raw.astype(F32) + p["gdn_dt_bias"][d])
        return log_alpha, jax.nn.sigmoid(b_raw.astype(F32))
    o_gdn, s_gdn = _scan_both_directions(
        _gdn_chunked,
        (_l2norm(heads(jax.nn.silu(gdn_q), GDN_DK)), _l2norm(heads(jax.nn.silu(gdn_k), GDN_DK)), heads(jax.nn.silu(gdn_v), GDN_DV)),
        gdn_gates(gdn_a_fwd, gdn_b_fwd, 0), gdn_gates(gdn_a_bwd, gdn_b_bwd, 1), st_gdn)
    y_d = (_rmsnorm(o_gdn.astype(h.dtype), p["gdn_norm"]) * jax.nn.silu(heads(gdn_g, GDN_DV))).reshape(bsz, L, D_BRANCH)

    branches = jnp.stack([y_a, y_b, y_c, y_d], axis=2)
    lifted = jnp.einsum("blnc,ncd->blnd", branches, p["w_branch"])
    gates = jax.nn.sigmoid(merge).reshape(bsz, L, N_BRANCH, D_MODEL)
    mixed = jnp.einsum("blnd,blnd->bld", gates, lifted)
    return mixed @ p["w_out"], s_hg, s_ssd, s_gdn


def _routed_experts(xt, idx, wts, w1, w3, w2):
    T, D = xt.shape
    A = T * TOP_K
    e_flat = idx.reshape(-1)
    order = jnp.argsort(e_flat)
    e_sorted = e_flat[order]
    counts = jnp.zeros((N_EXPERTS,), jnp.int32).at[e_flat].add(1)
    padded = (counts + MOE_BLOCK - 1) // MOE_BLOCK * MOE_BLOCK
    start = jnp.cumsum(counts) - counts
    ends_pad = jnp.cumsum(padded)
    start_pad = ends_pad - padded
    dest = start_pad[e_sorted] + jnp.arange(A, dtype=jnp.int32) - start[e_sorted]
    n_blocks = (A + N_EXPERTS * (MOE_BLOCK - 1) + MOE_BLOCK - 1) // MOE_BLOCK
    R = n_blocks * MOE_BLOCK
    row_token = jnp.full((R,), T, jnp.int32).at[dest].set((order // TOP_K).astype(jnp.int32))
    row_gate = jnp.zeros((R,), xt.dtype).at[dest].set(wts.reshape(-1)[order].astype(xt.dtype))
    block_expert = jnp.minimum(
        jnp.searchsorted(ends_pad, jnp.arange(n_blocks, dtype=jnp.int32) * MOE_BLOCK, side="right"), N_EXPERTS - 1)
    x_pad = jnp.concatenate([xt, jnp.zeros((1, D), xt.dtype)], axis=0)

    def block(args):
        rows, e = args
        xb = x_pad[rows]
        return (jax.nn.silu(xb @ w1[e]) * (xb @ w3[e])) @ w2[e]

    out = lax.map(block, (row_token.reshape(n_blocks, MOE_BLOCK), block_expert)).reshape(R, D)
    return jax.ops.segment_sum(out * row_gate[:, None], row_token, num_segments=T + 1)[:T]


def _moe(h, p):
    bsz, L, D = h.shape
    xt = h.reshape(-1, D)
    T = xt.shape[0]
    scores = jax.nn.sigmoid((xt @ p["router_w"]).astype(F32))
    biased = scores + p["router_bias"].astype(F32)
    grp_scores = lax.top_k(biased.reshape(T, N_EXPERT_GROUPS, -1), 2)[0].sum(-1)
    _, top_grp = lax.top_k(grp_scores, TOPK_GROUPS)
    grp_mask = jax.nn.one_hot(top_grp, N_EXPERT_GROUPS, dtype=F32).sum(1) > 0
    exp_mask = jnp.repeat(grp_mask, N_EXPERTS // N_EXPERT_GROUPS, axis=1)
    _, idx = lax.top_k(jnp.where(exp_mask, biased, MASK_NEG), TOP_K)
    w = jnp.take_along_axis(scores, idx, axis=1)
    w = w / jnp.sum(w, axis=-1, keepdims=True) * ROUTED_SCALE
    routed = _routed_experts(xt, idx, w, p["exp_w1"], p["exp_w3"], p["exp_w2"])
    shared = (jax.nn.silu(xt @ p["sh_w1"]) * (xt @ p["sh_w3"])) @ p["sh_w2"]
    return (routed + shared).reshape(bsz, L, D)


def _layer(x, cond, st_hg, st_ssd, st_gdn, p, lb, on_grid):
    mod = jax.nn.silu(cond) @ p["ada_w"] + p["ada_b"]
    sh1, sc1, g1, sh2, sc2, g2 = jnp.split(mod[:, None, :], 6, axis=-1)
    h = _rmsnorm(x, p["norm1_g"]) * (1.0 + sc1) + sh1
    mix, s_hg, s_ssd, s_gdn = _token_mixers(h, st_hg, st_ssd, st_gdn, p, lb, on_grid)
    x = x + g1 * mix
    h = _rmsnorm(x, p["norm2_g"]) * (1.0 + sc2) + sh2
    x = x + g2 * _moe(h, p)
    return x, s_hg, s_ssd, s_gdn


def setup_inputs(seed: int = 0) -> dict:
    key = jax.random.key(seed)
    ks = iter(jax.random.split(key, 48))

    def nrm(shape, scale):
        return scale * jax.random.normal(next(ks), shape, F32)

    def gain(shape):
        return 1.0 + nrm(shape, 0.02)

    def a_log(shape):
        return jnp.log(jax.random.uniform(next(ks), shape, F32, 1.0, 16.0))

    def dt_bias(shape):
        dt = jnp.exp(jax.random.uniform(next(ks), shape, F32, math.log(1e-3), math.log(1e-1)))
        return dt + jnp.log(-jnp.expm1(-dt))

    D = D_MODEL
    return {
        "x_prompt": nrm((BATCH, SEQ, D), 1.0),
        "x_sample": nrm((DEC_BATCH, DEC_SEQ, D), 1.0),
        "state_hgrn": nrm((DEC_BATCH, DEPTH, 2, HG_HEADS, HG_DK, HG_DV), 0.1),
        "state_ssd": nrm((DEC_BATCH, DEPTH, 2, SSD_HEADS, SSD_P, SSD_N), 0.1),
        "state_gdn": nrm((DEC_BATCH, DEPTH, 2, GDN_HEADS, GDN_DK, GDN_DV), 0.1),
        "c": nrm((DEC_BATCH, D), 1.0),
        "c_ctx": nrm((D,), 1.0),
        "norm1_g": gain((DEPTH, D)),
        "norm2_g": gain((DEPTH, D)),
        "ada_w": nrm((DEPTH, D, 6 * D), 0.01),
        "ada_b": nrm((DEPTH, 6 * D), 0.02),
        "w_in": nrm((DEPTH, D, D_IN), D ** -0.5),
        "conv_w": nrm((DEPTH, SHORT_CONV_W, D_CONV), SHORT_CONV_W ** -0.5),
        "conv_b": nrm((DEPTH, D_CONV), 0.02),
        "hy_w1": nrm((DEPTH, HY_POS_DIM, HY_FILTER_HIDDEN), HY_POS_DIM ** -0.5),
        "hy_b1": nrm((DEPTH, HY_FILTER_HIDDEN), 0.02),
        "hy_w2": nrm((DEPTH, HY_FILTER_HIDDEN, HY_FILTER_HIDDEN), HY_FILTER_HIDDEN ** -0.5),
        "hy_b2": nrm((DEPTH, HY_FILTER_HIDDEN), 0.02),
        "hy_w3": nrm((DEPTH, HY_FILTER_HIDDEN, 2 * HY_ORDER * D_BRANCH), 0.02 * HY_FILTER_HIDDEN ** -0.5),
        "hy_bias": nrm((DEPTH, HY_ORDER, D_BRANCH), 0.1),
        "hg_lb": nrm((2, DEPTH, HG_HEADS * HG_DK), 1.0),
        "hg_norm": gain((DEPTH, HG_DV)),
        "ssd_a_log": a_log((DEPTH, 2, SSD_HEADS)),
        "ssd_dt_bias": dt_bias((DEPTH, 2, SSD_HEADS)),
        "ssd_d": 1.0 + nrm((DEPTH, SSD_HEADS), 0.1),
        "ssd_norm": gain((DEPTH, D_BRANCH)),
        "gdn_a_log": a_log((DEPTH, 2, GDN_HEADS)),
        "gdn_dt_bias": dt_bias((DEPTH, 2, GDN_HEADS)),
        "gdn_norm": gain((DEPTH, GDN_DV)),
        "w_branch": nrm((DEPTH, N_BRANCH, D_BRANCH, D), D_BRANCH ** -0.5),
        "w_out": nrm((DEPTH, D, D), D ** -0.5),
        "router_w": nrm((DEPTH, D, N_EXPERTS), D ** -0.5),
        "router_bias": nrm((DEPTH, N_EXPERTS), 0.01),
        "exp_w1": nrm((DEPTH, N_EXPERTS, D, D_EXPERT), D ** -0.5),
        "exp_w3": nrm((DEPTH, N_EXPERTS, D, D_EXPERT), D ** -0.5),
        "exp_w2": nrm((DEPTH, N_EXPERTS, D_EXPERT, D), D_EXPERT ** -0.5),
        "sh_w1": nrm((DEPTH, D, D_SHARED), D ** -0.5),
        "sh_w3": nrm((DEPTH, D, D_SHARED), D ** -0.5),
        "sh_w2": nrm((DEPTH, D_SHARED, D), D_SHARED ** -0.5),
        "final_g": gain((D,)),
    }


def reference(x_prompt, x_sample, state_hgrn, state_ssd, state_gdn, c, c_ctx,
              norm1_g, norm2_g, ada_w, ada_b, w_in, conv_w, conv_b,
              hy_w1, hy_b1, hy_w2, hy_b2, hy_w3, hy_bias,
              hg_lb, hg_norm, ssd_a_log, ssd_dt_bias, ssd_d, ssd_norm,
              gdn_a_log, gdn_dt_bias, gdn_norm, w_branch, w_out,
              router_w, router_bias, exp_w1, exp_w3, exp_w2, sh_w1, sh_w3, sh_w2, final_g):
    lb = _hgrn_lower_bounds(hg_lb)
    layers = [dict(norm1_g=norm1_g[l], norm2_g=norm2_g[l], ada_w=ada_w[l], ada_b=ada_b[l],
                   w_in=w_in[l], conv_w=conv_w[l], conv_b=conv_b[l],
                   hy_w1=hy_w1[l], hy_b1=hy_b1[l], hy_w2=hy_w2[l], hy_b2=hy_b2[l], hy_w3=hy_w3[l], hy_bias=hy_bias[l],
                   hg_norm=hg_norm[l], ssd_a_log=ssd_a_log[l], ssd_dt_bias=ssd_dt_bias[l], ssd_d=ssd_d[l],
                   ssd_norm=ssd_norm[l], gdn_a_log=gdn_a_log[l], gdn_dt_bias=gdn_dt_bias[l], gdn_norm=gdn_norm[l],
                   w_branch=w_branch[l], w_out=w_out[l], router_w=router_w[l], router_bias=router_bias[l],
                   exp_w1=exp_w1[l], exp_w3=exp_w3[l], exp_w2=exp_w2[l],
                   sh_w1=sh_w1[l], sh_w3=sh_w3[l], sh_w2=sh_w2[l]) for l in range(DEPTH)]

    x = x_prompt
    bp = x.shape[0]
    cond_ctx = jnp.broadcast_to(c_ctx, (bp, D_MODEL))
    zero_hg = jnp.zeros((bp, 2, HG_HEADS, HG_DK, HG_DV), F32)
    zero_ssd = jnp.zeros((bp, 2, SSD_HEADS, SSD_P, SSD_N), F32)
    zero_gdn = jnp.zeros((bp, 2, GDN_HEADS, GDN_DK, GDN_DV), F32)
    hg_states, ssd_states, gdn_states = [], [], []
    for l in range(DEPTH):
        x, s_hg, s_ssd, s_gdn = _layer(x, cond_ctx, zero_hg, zero_ssd, zero_gdn, layers[l], lb[:, l], False)
        hg_states.append(s_hg)
        ssd_states.append(s_ssd)
        gdn_states.append(s_gdn)
    y_prompt = _rmsnorm(x, final_g)
    new_state_hgrn = jnp.stack(hg_states, axis=1).astype(x_prompt.dtype)
    new_state_ssd = jnp.stack(ssd_states, axis=1).astype(x_prompt.dtype)
    new_state_gdn = jnp.stack(gdn_states, axis=1).astype(x_prompt.dtype)

    x = x_sample
    for l in range(DEPTH):
        x, _, _, _ = _layer(x, c, state_hgrn[:, l], state_ssd[:, l], state_gdn[:, l], layers[l], lb[:, l], True)
    y_sample = _rmsnorm(x, final_g)
    return (y_prompt, y_sample, new_state_hgrn, new_state_ssd, new_state_gdn)
```

```python
import functools
import math

import numpy as np
import jax
import jax.numpy as jnp
from jax import lax
from jax.experimental import pallas as pl
from jax.experimental.pallas import tpu as pltpu

F32 = jnp.float32
BF16 = jnp.bfloat16
I32 = jnp.int32
HIGHEST = lax.Precision.HIGHEST

D_MODEL = 1024
GRID_W = 64
EPS = 1e-6
LOG_FLOOR = 1e-30
MASK_NEG = -1e30
N_BRANCH = 4
D_BRANCH = 512
HY_POS_FREQS = 16
HY_FILTER_HIDDEN = 64
HY_FAST_DECAY = 0.3
HY_SLOW_DECAY = 1.5
HY_DECAY_TARGET = 1e-2
HG_HEADS = 4
HG_DK = 128
HG_CHUNK = 16
SSD_HEADS = 8
SSD_P = 64
SSD_N = 128
SSD_CHUNK = 64
GDN_HEADS = 4
GDN_DK = 128
GDN_CHUNK = 64
N_EXPERTS = 64
TOP_K = 8
N_EXPERT_GROUPS = 8
TOPK_GROUPS = 4
D_EXPERT = 256
D_SHARED = 256
ROUTED_SCALE = 2.5

LANES = 128

D_CONV = 4096
C_HYV, C_HYX1, C_HYX2 = 0, 512, 1024
C_SSDX, C_SSDB, C_SSDC = 1536, 2048, 2304
C_GQ, C_GK, C_GV = 2560, 3072, 3584
P_MERGE = 4096
P_HGQ, P_HGFF, P_HGFB, P_HGV, P_HGG = 8192, 8704, 9216, 9728, 10240
P_SSDZ, P_GDNG, P_SMALL = 10752, 11264, 11776
D_PROJ = 12288
S_DT, S_GA, S_GB = 0, 16, 24

VMEM_LIMIT = 56 * 1024 * 1024


def _cparams(sem):
    return pltpu.CompilerParams(dimension_semantics=sem, vmem_limit_bytes=VMEM_LIMIT)


def _sigmoid(x):
    return 1.0 / (1.0 + jnp.exp(-x))


def _silu(x):
    return x * _sigmoid(x)


def _softplus(x):
    return jnp.maximum(x, 0.0) + jnp.log(1.0 + jnp.exp(-jnp.abs(x)))


def _dot(a, b, precision=None):
    return jnp.dot(a, b, preferred_element_type=F32, precision=precision)


def _dot_nt(a, b, precision=None):
    return lax.dot_general(a, b, (((1,), (1,)), ((), ())), preferred_element_type=F32, precision=precision)


def _dot_tn(a, b, precision=None):
    return lax.dot_general(a, b, (((0,), (0,)), ((), ())), preferred_element_type=F32, precision=precision)


def _bdot(a, b):
    return _dot(a.astype(BF16), b.astype(BF16))


def _bdot_nt(a, b):
    return _dot_nt(a.astype(BF16), b.astype(BF16))


def _bdot_tn(a, b):
    return _dot_tn(a.astype(BF16), b.astype(BF16))


def _iota(shape, dim):
    return lax.broadcasted_iota(I32, shape, dim)


def _cumsum_rows(g, reverse):
    n = g.shape[0]
    row = _iota(g.shape, 0)
    sh = 1
    while sh < n:
        if reverse:
            g = g + jnp.where(row < n - sh, pltpu.roll(g, n - sh, 0), 0.0)
        else:
            g = g + jnp.where(row >= sh, pltpu.roll(g, sh, 0), 0.0)
        sh *= 2
    return g


def _lane_pick(a, j):
    return jnp.sum(jnp.where(_iota(a.shape, 1) == j, a, 0.0), axis=1, keepdims=True)


def _row_pick(a, j):
    sel = (_iota((8, a.shape[1]), 1) == j).astype(F32)
    return _dot_nt(sel, a, precision=HIGHEST)[0:1, :]


def _tri_mask(n, reverse, strict=False):
    t = _iota((n, n), 0)
    s = _iota((n, n), 1)
    if reverse:
        return (s > t) if strict else (s >= t)
    return (s < t) if strict else (s <= t)


def _masked_exp(mask, diff):
    return jnp.where(mask, jnp.exp(jnp.where(mask, diff, 0.0)), 0.0)


def _mod_kernel(c_ref, w_ref, b_ref, o_ref):
    o_ref[...] = _dot(_silu(c_ref[...]), w_ref[...], precision=HIGHEST) + b_ref[...]


def _modulation(cond8, ada_w, ada_b):
    rows = cond8.shape[0]
    tn = 1536
    n = ada_w.shape[1]
    return pl.pallas_call(
        _mod_kernel,
        grid=(n // tn,),
        in_specs=[pl.BlockSpec((rows, D_MODEL), lambda j: (0, 0)),
                  pl.BlockSpec((D_MODEL, tn), lambda j: (0, j)),
                  pl.BlockSpec((1, tn), lambda j: (0, j))],
        out_specs=pl.BlockSpec((rows, tn), lambda j: (0, j)),
        out_shape=jax.ShapeDtypeStruct((rows, n), F32),
        compiler_params=_cparams(("arbitrary",)),
        name="adaln_mod",
    )(cond8, ada_w, ada_b.reshape(1, n))


def _inproj_kernel(x_ref, mod_ref, g_ref, w_ref, cw_ref, cb_ref, o_ref, h_scr, *, seg, n_conv_tiles):
    j = pl.program_id(1)

    @pl.when(j == 0)
    def _():
        x = x_ref[...]
        xn = x * lax.rsqrt(jnp.mean(x * x, axis=-1, keepdims=True) + EPS) * g_ref[...]
        m = mod_ref[...]
        h_scr[...] = (xn * (1.0 + m[1:2]) + m[0:1]).astype(BF16)

    y = _dot(h_scr[...], w_ref[...])

    @pl.when(j < n_conv_tiles)
    def _():
        tm = y.shape[0]
        pos = _iota(y.shape, 0) & (seg - 1)
        prev = jnp.where(pos == 0, 0.0, pltpu.roll(y, 1, 0))
        nxt = jnp.where(pos == seg - 1, 0.0, pltpu.roll(y, tm - 1, 0))
        cw = cw_ref[...]
        o_ref[...] = cb_ref[...] + prev * cw[0:1] + y * cw[1:2] + nxt * cw[2:3]

    @pl.when(j >= n_conv_tiles)
    def _():
        o_ref[...] = y


def _in_projection(x, mod, norm_g, w_bf16, conv_w, conv_b, *, tokens_per_mod, seg):
    T = x.shape[0]
    tm = min(1024, tokens_per_mod)
    tn = 1024
    n_conv_tiles = D_CONV // tn
    tiles_per_mod = tokens_per_mod // tm
    kern = functools.partial(_inproj_kernel, seg=seg, n_conv_tiles=n_conv_tiles)
    cmap = lambda i, j: (0, jnp.minimum(j, n_conv_tiles - 1))
    return pl.pallas_call(
        kern,
        grid=(T // tm, D_PROJ // tn),
        in_specs=[pl.BlockSpec((tm, D_MODEL), lambda i, j: (i, 0)),
                  pl.BlockSpec((None, 6, D_MODEL), lambda i, j: (i // tiles_per_mod, 0, 0)),
                  pl.BlockSpec((1, D_MODEL), lambda i, j: (0, 0)),
                  pl.BlockSpec((D_MODEL, tn), lambda i, j: (0, j)),
                  pl.BlockSpec((3, tn), cmap),
                  pl.BlockSpec((1, tn), cmap)],
        out_specs=pl.BlockSpec((tm, tn), lambda i, j: (i, j)),
        out_shape=jax.ShapeDtypeStruct((T, D_PROJ), F32),
        scratch_shapes=[pltpu.VMEM((tm, D_MODEL), BF16)],
        compiler_params=_cparams(("arbitrary", "arbitrary")),
        name="in_proj",
    )(x, mod, norm_g.reshape(1, D_MODEL), w_bf16, conv_w, conv_b.reshape(1, D_CONV))


def _hyfilt_kernel(w1_ref, b1_ref, w2_ref, b2_ref, w3_ref, o_ref, *, L):
    i = pl.program_id(1)
    tl = o_ref.shape[0]
    t = (_iota((tl, LANES), 0) + i * tl).astype(F32) / L
    lane = _iota((tl, LANES), 1)
    band = jnp.where(lane <= HY_POS_FREQS, lane, lane - HY_POS_FREQS).astype(F32)
    ang = 2.0 * math.pi * t * band
    feats = jnp.where(lane == 0, t,
                      jnp.where(lane <= HY_POS_FREQS, jnp.sin(ang),
                                jnp.where(lane <= 2 * HY_POS_FREQS, jnp.cos(ang), 0.0)))
    hdn = jnp.sin(_dot(feats, w1_ref[...], precision=HIGHEST) + b1_ref[...])
    hdn = jnp.sin(_dot(hdn, w2_ref[...], precision=HIGHEST) + b2_ref[...])
    filt = _dot(hdn, w3_ref[...], precision=HIGHEST)
    max_decay = math.log(HY_DECAY_TARGET) / HY_FAST_DECAY
    min_decay = math.log(HY_DECAY_TARGET) / HY_SLOW_DECAY
    n = filt.shape[1]
    ch = (_iota((tl, n), 1) & (D_BRANCH - 1)).astype(F32)
    delta = min_decay + ch * ((max_decay - min_decay) / (D_BRANCH - 1))
    tt = (_iota((tl, n), 0) + i * tl).astype(F32) / L
    o_ref[...] = filt * jnp.exp(-tt * jnp.abs(delta))


def _hyena_filters(L, w1p, b1, w2, b2, w3):
    tl = min(L, 256)
    n = 2 * D_BRANCH
    return pl.pallas_call(
        functools.partial(_hyfilt_kernel, L=L),
        grid=(2, L // tl),
        in_specs=[pl.BlockSpec((LANES, HY_FILTER_HIDDEN), lambda d, i: (0, 0)),
                  pl.BlockSpec((1, HY_FILTER_HIDDEN), lambda d, i: (0, 0)),
                  pl.BlockSpec((HY_FILTER_HIDDEN, HY_FILTER_HIDDEN), lambda d, i: (0, 0)),
                  pl.BlockSpec((1, HY_FILTER_HIDDEN), lambda d, i: (0, 0)),
                  pl.BlockSpec((HY_FILTER_HIDDEN, n), lambda d, i: (0, d))],
        out_specs=pl.BlockSpec((None, tl, n), lambda d, i: (d, i, 0)),
        out_shape=jax.ShapeDtypeStruct((2, L, n), F32),
        compiler_params=_cparams(("arbitrary", "arbitrary")),
        name="hyena_filters",
    )(w1p, b1.reshape(1, -1), w2, b2.reshape(1, -1), w3)


def _dftgen_kernel(f_ref, fs_ref, *, L, tk):
    i = pl.program_id(0)
    N = 2 * L
    k = _iota((tk, LANES), 0) + i * tk
    lane = _iota((tk, LANES), 1)
    w = 2.0 * math.pi / N
    a0 = ((k * lane) & (N - 1)).astype(F32) * w
    c0, s0 = jnp.cos(a0), jnp.sin(a0)
    a1 = ((k * lane * LANES) & (N - 1)).astype(F32) * w
    c1, s1 = jnp.cos(a1), jnp.sin(a1)
    alt = jnp.where((lane & 1) == 0, 1.0, -1.0)
    coef = jnp.where(k == 0, 1.0 / N, 2.0 / N)
    for n1 in range(L // LANES):
        c1b = c1[:, n1:n1 + 1]
        s1b = s1[:, n1:n1 + 1]
        cosb = c1b * c0 - s1b * s0
        sinb = jnp.where(k == 0, alt, -(s1b * c0 + c1b * s0))
        cols = slice(n1 * LANES, (n1 + 1) * LANES)
        f_ref[0:tk, cols] = cosb.astype(BF16)
        f_ref[tk:2 * tk, cols] = sinb.astype(BF16)
        fs_ref[0:tk, cols] = (coef * cosb).astype(BF16)
        fs_ref[tk:2 * tk, cols] = (coef * sinb).astype(BF16)


def _dft_matrices(L):
    tk = min(L, 256)
    spec = pl.BlockSpec((2 * tk, L), lambda i: (i, 0))
    return pl.pallas_call(
        functools.partial(_dftgen_kernel, L=L, tk=tk),
        grid=(L // tk,),
        in_specs=[],
        out_specs=[spec, spec],
        out_shape=[jax.ShapeDtypeStruct((2 * L, L), BF16)] * 2,
        compiler_params=_cparams(("arbitrary",)),
        name="dft_matrices",
    )()


def _dfth_kernel(f_ref, h_ref, o_ref, hp_scr, *, tk):
    i = pl.program_id(1)
    n = o_ref.shape[1]

    @pl.when(i == 0)
    def _():
        hc = h_ref[0]
        ha = jnp.where(_iota(hc.shape, 0) == 0, 0.0, h_ref[1])
        hp_scr[:, 0:n] = (hc + ha).astype(BF16)
        hp_scr[:, n:2 * n] = (hc - ha).astype(BF16)

    u = _dot(f_ref[...], hp_scr[...])
    r = _iota((2 * tk, n), 0)
    from_sum = (r < tk) | ((r == tk) & (i == 0))
    o_ref[...] = jnp.where(from_sum, u[:, 0:n], u[:, n:2 * n])


def _filter_spectrum(fmat, filt):
    L = fmat.shape[1]
    tk = min(L, 256)
    C = filt.shape[2]
    tc = 256
    return pl.pallas_call(
        functools.partial(_dfth_kernel, tk=tk),
        grid=(C // tc, L // tk),
        in_specs=[pl.BlockSpec((2 * tk, L), lambda c, i: (i, 0)),
                  pl.BlockSpec((2, L, tc), lambda c, i: (0, 0, c))],
        out_specs=pl.BlockSpec((2 * tk, tc), lambda c, i: (i, c)),
        out_shape=jax.ShapeDtypeStruct((2 * L, C), F32),
        scratch_shapes=[pltpu.VMEM((L, 2 * tc), BF16)],
        compiler_params=_cparams(("arbitrary", "arbitrary")),
        name="filter_spectrum",
    )(fmat, filt)


def _dfta_kernel(f_ref, u_ref, h_ref, y_ref, u_scr, *, tk):
    i = pl.program_id(1)

    @pl.when(i == 0)
    def _():
        u_scr[...] = u_ref[...].astype(BF16)

    uf = _dot(f_ref[...], u_scr[...])
    ur, ui = uf[0:tk], uf[tk:2 * tk]
    hr, hi = h_ref[0:tk, :], h_ref[tk:2 * tk, :]
    dc = (_iota(ur.shape, 0) == 0) & (i == 0)
    y_ref[0:tk, :] = jnp.where(dc, ur * hr, ur * hr - ui * hi).astype(BF16)
    y_ref[tk:2 * tk, :] = jnp.where(dc, ui * hi, ur * hi + ui * hr).astype(BF16)


def _dftb_kernel(g_ref, y_ref, u_ref, x_ref, b_ref, o_ref):
    y = _dot(g_ref[...], y_ref[...])
    o_ref[...] = x_ref[...] * (y + u_ref[...] * b_ref[...])


def _spectral_conv(fmat, gmat, hspec, h_col, bias, u_arr, u_col, gate_arr, gate_col):
    B, L = u_arr.shape[0], u_arr.shape[1]
    C = D_BRANCH
    tk = min(L, 256)
    y = pl.pallas_call(
        functools.partial(_dfta_kernel, tk=tk),
        grid=(B, L // tk),
        in_specs=[pl.BlockSpec((2 * tk, L), lambda b, i: (i, 0)),
                  pl.BlockSpec((None, L, C), lambda b, i: (b, 0, u_col)),
                  pl.BlockSpec((2 * tk, C), lambda b, i: (i, h_col))],
        out_specs=pl.BlockSpec((None, 2 * tk, C), lambda b, i: (b, i, 0)),
        out_shape=jax.ShapeDtypeStruct((B, 2 * L, C), BF16),
        scratch_shapes=[pltpu.VMEM((L, C), BF16)],
        compiler_params=_cparams(("arbitrary", "arbitrary")),
        name="hyena_dft_fwd",
    )(fmat, u_arr, hspec)
    tr = min(L, 256)
    return pl.pallas_call(
        _dftb_kernel,
        grid=(B, L // tr),
        in_specs=[pl.BlockSpec((tr, 2 * L), lambda b, i: (i, 0)),
                  pl.BlockSpec((None, 2 * L, C), lambda b, i: (b, 0, 0)),
                  pl.BlockSpec((None, tr, C), lambda b, i: (b, i, u_col)),
                  pl.BlockSpec((None, tr, C), lambda b, i: (b, i, gate_col)),
                  pl.BlockSpec((1, C), lambda b, i: (0, 0))],
        out_specs=pl.BlockSpec((None, tr, C), lambda b, i: (b, i, 0)),
        out_shape=jax.ShapeDtypeStruct((B, L, C), F32),
        compiler_params=_cparams(("arbitrary", "arbitrary")),
        name="hyena_dft_inv",
    )(gmat, y, u_arr, gate_arr, bias.reshape(1, C))


def _hgrn_kernel(q_ref, ff_ref, fb_ref, v_ref, g_ref, lb_ref, nrm_ref, s0_ref, y_ref, sf_ref, o_scr,
                 *, L, layer, depth):
    C = HG_CHUNK
    nc = L // C
    ridx = _iota((C, HG_DK), 0)

    def lower_bound(d):
        rows = [lb_ref[d, l:l + 1, :] for l in range(depth)]
        m = rows[0]
        for r in rows[1:]:
            m = jnp.maximum(m, r)
        es = [jnp.exp(r - m) for r in rows]
        tot = es[0]
        for e in es[1:]:
            tot = tot + e
        acc = es[0] / tot
        for e in es[1:layer + 1]:
            acc = acc + e / tot
        return acc - es[0] / tot

    for d, f_ref in ((0, ff_ref), (1, fb_ref)):
        rev = d == 1
        lb = lower_bound(d)

        def body(ci, st, rev=rev, lb=lb, f_ref=f_ref, d=d):
            c = (nc - 1 - ci) if rev else ci
            rows = pl.ds(pl.multiple_of(c * C, C), C)
            q = _silu(q_ref[rows, :])
            uf = f_ref[rows, :]
            v = v_ref[rows, :]
            f = lb + (1.0 - lb) * _sigmoid(uf)
            g = jnp.log(jnp.maximum(f, LOG_FLOOR))
            kin = (1.0 - lb) * _sigmoid(-uf)
            b = _cumsum_rows(g, rev)
            o = _bdot_nt(q * jnp.exp(b), st)
            intra = jnp.zeros((C, HG_DK), F32)
            for t in range(C):
                mask = (ridx >= t) if rev else (ridx <= t)
                pair = _masked_exp(mask, b[t:t + 1, :] - b)
                a = jnp.sum(pair * (q[t:t + 1, :] * kin), axis=1, keepdims=True)
                row = jnp.sum(a * v, axis=0, keepdims=True)
                intra = jnp.where(ridx == t, row, intra)
            o = o + intra
            b_end = b[0:1, :] if rev else b[C - 1:C, :]
            st = st * jnp.exp(b_end) + _bdot_tn(v, kin * jnp.exp(b_end - b))
            if rev:
                o_scr[rows, :] = o_scr[rows, :] + o
            else:
                o_scr[rows, :] = o
            return st

        st = lax.fori_loop(0, nc, body, s0_ref[d].T)
        sf_ref[d] = st.T

    tr = min(L, 256)

    def fin(i, carry):
        rows = pl.ds(pl.multiple_of(i * tr, tr), tr)
        o = o_scr[rows, :]
        on = o * lax.rsqrt(jnp.mean(o * o, axis=-1, keepdims=True) + EPS) * nrm_ref[...]
        y_ref[rows, :] = on * _silu(g_ref[rows, :])
        return carry

    lax.fori_loop(0, L // tr, fin, 0)


def _hgrn_scan(proj3, hg_lb, hg_norm, s0, *, layer):
    B, L = proj3.shape[0], proj3.shape[1]
    depth = hg_lb.shape[1]
    W = HG_DK
    col = lambda base: (lambda b, h: (b, 0, base // W + h))
    st_spec = pl.BlockSpec((None, 2, None, HG_DK, HG_DK), lambda b, h: (b, 0, h, 0, 0))
    return pl.pallas_call(
        functools.partial(_hgrn_kernel, L=L, layer=layer, depth=depth),
        grid=(B, HG_HEADS),
        in_specs=[pl.BlockSpec((None, L, W), col(P_HGQ)),
                  pl.BlockSpec((None, L, W), col(P_HGFF)),
                  pl.BlockSpec((None, L, W), col(P_HGFB)),
                  pl.BlockSpec((None, L, W), col(P_HGV)),
                  pl.BlockSpec((None, L, W), col(P_HGG)),
                  pl.BlockSpec((2, depth, W), lambda b, h: (0, 0, h)),
                  pl.BlockSpec((1, W), lambda b, h: (0, 0)),
                  st_spec],
        out_specs=[pl.BlockSpec((None, L, W), lambda b, h: (b, 0, h)), st_spec],
        out_shape=[jax.ShapeDtypeStruct((B, L, D_BRANCH), F32),
                   jax.ShapeDtypeStruct(s0.shape, F32)],
        scratch_shapes=[pltpu.VMEM((L, W), F32)],
        compiler_params=_cparams(("arbitrary", "arbitrary")),
        name="hgrn2_scan",
    )(proj3, proj3, proj3, proj3, proj3, hg_lb, hg_norm.reshape(1, W), s0)


def _ssd_kernel(x_ref, bm_ref, cm_ref, z_ref, sm_ref, bias_ref, alog_ref, dskip_ref, s0_ref,
                y_ref, sf_ref, o_scr, *, L):
    C = SSD_CHUNK
    nc = L // C
    hp = pl.program_id(1)
    lane_lo = _iota((C, LANES), 1) < SSD_P
    row_lo = _iota((LANES, SSD_N), 0) < SSD_P
    pick2 = lambda a0, a1: jnp.where(lane_lo, a0, a1)

    for d in (0, 1):
        rev = d == 1
        incl = _tri_mask(C, rev)
        j0 = S_DT + d * SSD_HEADS + 2 * hp

        def body(ci, st, rev=rev, incl=incl, j0=j0):
            c = (nc - 1 - ci) if rev else ci
            rows = pl.ds(pl.multiple_of(c * C, C), C)
            xs = _silu(x_ref[rows, :])
            bm = _silu(bm_ref[rows, :])
            cm = _silu(cm_ref[rows, :])
            dt_all = _softplus(sm_ref[rows, :] + bias_ref[...])
            cum_all = _cumsum_rows(-jnp.exp(alog_ref[...]) * dt_all, rev)
            gram = _bdot_nt(cm, bm)
            dts, cums, ends, ys = [], [], [], []
            for hh in (0, 1):
                dtc = _lane_pick(dt_all, j0 + hh)
                cumc = _lane_pick(cum_all, j0 + hh)
                cumr = _row_pick(cum_all, j0 + hh)
                dts.append(dtc)
                cums.append(cumc)
                ends.append(cumc[0:1, :] if rev else cumc[C - 1:C, :])
                scores = gram * _masked_exp(incl, cumc - cumr)
                head_x = jnp.where(lane_lo if hh == 0 else ~lane_lo, xs, 0.0) * dtc
                ys.append(_bdot(scores, head_x) + _bdot_nt(cm * jnp.exp(cumc), st))
            y = pick2(ys[0], ys[1])
            xdt_e = xs * pick2(dts[0] * jnp.exp(ends[0] - cums[0]), dts[1] * jnp.exp(ends[1] - cums[1]))
            st = st * jnp.where(row_lo, jnp.exp(ends[0]), jnp.exp(ends[1])) + _bdot_tn(xdt_e, bm)
            if rev:
                o_scr[rows, :] = o_scr[rows, :] + y
            else:
                o_scr[rows, :] = y
            return st

        st = lax.fori_loop(0, nc, body, s0_ref[d])
        sf_ref[d] = st

    tr = min(L, 256)

    def fin(i, carry):
        rows = pl.ds(pl.multiple_of(i * tr, tr), tr)
        y = o_scr[rows, :] + _silu(x_ref[rows, :]) * dskip_ref[...]
        y_ref[rows, :] = y * _silu(z_ref[rows, :])
        return carry

    lax.fori_loop(0, L // tr, fin, 0)


def _ssd_scan(proj3, bias_row, alog_row, ssd_d, s0):
    B, L = proj3.shape[0], proj3.shape[1]
    W = LANES
    n_pairs = SSD_HEADS // 2
    pairs_per_group = (SSD_HEADS // 2) // 2
    s0p = s0.reshape(B, 2, n_pairs, 2 * SSD_P, SSD_N)
    dskip = jnp.repeat(ssd_d, SSD_P).reshape(n_pairs, 1, W)
    col = lambda base: (lambda b, h: (b, 0, base // W + h))
    grp = lambda base: (lambda b, h: (b, 0, base // W + h // pairs_per_group))
    st_spec = pl.BlockSpec((None, 2, None, W, SSD_N), lambda b, h: (b, 0, h, 0, 0))
    y, sf = pl.pallas_call(
        functools.partial(_ssd_kernel, L=L),
        grid=(B, n_pairs),
        in_specs=[pl.BlockSpec((None, L, W), col(C_SSDX)),
                  pl.BlockSpec((None, L, W), grp(C_SSDB)),
                  pl.BlockSpec((None, L, W), grp(C_SSDC)),
                  pl.BlockSpec((None, L, W), col(P_SSDZ)),
                  pl.BlockSpec((None, L, W), lambda b, h: (b, 0, P_SMALL // W)),
                  pl.BlockSpec((1, W), lambda b, h: (0, 0)),
                  pl.BlockSpec((1, W), lambda b, h: (0, 0)),
                  pl.BlockSpec((None, 1, W), lambda b, h: (h, 0, 0)),
                  st_spec],
        out_specs=[pl.BlockSpec((None, L, W), lambda b, h: (b, 0, h)), st_spec],
        out_shape=[jax.ShapeDtypeStruct((B, L, D_BRANCH), F32),
                   jax.ShapeDtypeStruct(s0p.shape, F32)],
        scratch_shapes=[pltpu.VMEM((L, W), F32)],
        compiler_params=_cparams(("arbitrary", "arbitrary")),
        name="ssd_scan",
    )(proj3, proj3, proj3, proj3, proj3, bias_row, alog_row, dskip, s0p)
    return y, sf.reshape(s0.shape)


def _unit_tri_inverse(a):
    n = a.shape[0]
    eye = (_iota((n, n), 0) == _iota((n, n), 1)).astype(F32)
    m = -a
    p = eye + m
    sh = 2
    while sh < n:
        m = _dot(m, m, precision=HIGHEST)
        p = p + _dot(p, m, precision=HIGHEST)
        sh *= 2
    return p


def _l2norm(a):
    return a * lax.rsqrt(jnp.sum(a * a, axis=-1, keepdims=True) + EPS)


def _gdn_kernel(q_ref, k_ref, v_ref, g_ref, sm_ref, bias_ref, alog_ref, nrm_ref, s0_ref,
                y_ref, sf_ref, o_scr, *, L):
    C = GDN_CHUNK
    nc = L // C
    h = pl.program_id(1)

    for d in (0, 1):
        rev = d == 1
        incl = _tri_mask(C, rev)
        strict = _tri_mask(C, rev, strict=True)
        jg = S_GA + d * GDN_HEADS + h
        jb = S_GB + d * GDN_HEADS + h

        def body(ci, st, rev=rev, incl=incl, strict=strict, jg=jg, jb=jb):
            c = (nc - 1 - ci) if rev else ci
            rows = pl.ds(pl.multiple_of(c * C, C), C)
            q = _l2norm(_silu(q_ref[rows, :])) * (GDN_DK ** -0.5)
            k = _l2norm(_silu(k_ref[rows, :]))
            v = _silu(v_ref[rows, :])
            raw = sm_ref[rows, :]
            la_all = -jnp.exp(alog_ref[...]) * _softplus(raw + bias_ref[...])
            cum_all = _cumsum_rows(la_all, rev)
            gc = _lane_pick(cum_all, jg)
            gr = _row_pick(cum_all, jg)
            beta = _lane_pick(_sigmoid(raw), jb)
            decay = _masked_exp(incl, gc - gr)
            kb = k * beta
            a = jnp.where(strict, _bdot_nt(kb, k) * decay, 0.0)
            tinv = _unit_tri_inverse(a)
            u = _dot(tinv, v * beta, precision=HIGHEST)
            w = _dot(tinv, kb * jnp.exp(gc), precision=HIGHEST)
            aqk = _bdot_nt(q, k) * decay
            v_new = u - _bdot(w, st)
            o = _bdot(q * jnp.exp(gc), st) + _bdot(aqk, v_new)
            g_end = gc[0:1, :] if rev else gc[C - 1:C, :]
            st = st * jnp.exp(g_end) + _bdot_tn(k * jnp.exp(g_end - gc), v_new)
            if rev:
                o_scr[rows, :] = o_scr[rows, :] + o
            else:
                o_scr[rows, :] = o
            return st

        st = lax.fori_loop(0, nc, body, s0_ref[d])
        sf_ref[d] = st

    tr = min(L, 256)

    def fin(i, carry):
        rows = pl.ds(pl.multiple_of(i * tr, tr), tr)
        o = o_scr[rows, :]
        on = o * lax.rsqrt(jnp.mean(o * o, axis=-1, keepdims=True) + EPS) * nrm_ref[...]
        y_ref[rows, :] = on * _silu(g_ref[rows, :])
        return carry

    lax.fori_loop(0, L // tr, fin, 0)


def _gdn_scan(proj3, bias_row, alog_row, gdn_norm, s0):
    B, L = proj3.shape[0], proj3.shape[1]
    W = GDN_DK
    col = lambda base: (lambda b, h: (b, 0, base // W + h))
    st_spec = pl.BlockSpec((None, 2, None, W, W), lambda b, h: (b, 0, h, 0, 0))
    return pl.pallas_call(
        functools.partial(_gdn_kernel, L=L),
        grid=(B, GDN_HEADS),
        in_specs=[pl.BlockSpec((None, L, W), col(C_GQ)),
                  pl.BlockSpec((None, L, W), col(C_GK)),
                  pl.BlockSpec((None, L, W), col(C_GV)),
                  pl.BlockSpec((None, L, W), col(P_GDNG)),
                  pl.BlockSpec((None, L, W), lambda b, h: (b, 0, P_SMALL // W)),
                  pl.BlockSpec((1, W), lambda b, h: (0, 0)),
                  pl.BlockSpec((1, W), lambda b, h: (0, 0)),
                  pl.BlockSpec((1, W), lambda b, h: (0, 0)),
                  st_spec],
        out_specs=[pl.BlockSpec((None, L, W), lambda b, h: (b, 0, h)), st_spec],
        out_shape=[jax.ShapeDtypeStruct((B, L, D_BRANCH), F32),
                   jax.ShapeDtypeStruct(s0.shape, F32)],
        scratch_shapes=[pltpu.VMEM((L, W), F32)],
        compiler_params=_cparams(("arbitrary", "arbitrary")),
        name="gdn_scan",
    )(proj3, proj3, proj3, proj3, proj3, bias_row, alog_row, gdn_norm.reshape(1, W), s0)


def _merge_kernel(ya_ref, yb_ref, yc_ref, yd_ref, mg_ref, x_ref, mod_ref, sn_ref, n2_ref, wb_ref, wo_ref,
                  xo_ref, h_ref):
    yc = yc_ref[...]
    half = D_BRANCH // 2
    parts = []
    for gidx in range(2):
        seg = yc[:, gidx * half:(gidx + 1) * half]
        parts.append(seg * lax.rsqrt(jnp.mean(seg * seg, axis=-1, keepdims=True) + EPS)
                     * sn_ref[:, gidx * half:(gidx + 1) * half])
    branches = (ya_ref[...], yb_ref[...], None, yd_ref[...])
    mixed = None
    for n in range(N_BRANCH):
        gate = _sigmoid(mg_ref[:, n * D_MODEL:(n + 1) * D_MODEL])
        if n == 2:
            lifted = (_dot(parts[0].astype(BF16), wb_ref[n, 0:half, :])
                      + _dot(parts[1].astype(BF16), wb_ref[n, half:D_BRANCH, :]))
        else:
            lifted = _dot(branches[n].astype(BF16), wb_ref[n])
        mixed = gate * lifted if mixed is None else mixed + gate * lifted
    m = mod_ref[...]
    x = x_ref[...] + m[2:3] * _dot(mixed.astype(BF16), wo_ref[...])
    xo_ref[...] = x
    xn = x * lax.rsqrt(jnp.mean(x * x, axis=-1, keepdims=True) + EPS) * n2_ref[...]
    h_ref[...] = xn * (1.0 + m[4:5]) + m[3:4]


def _merge(ya, yb, yc, yd, proj, x, mod, ssd_norm, norm2_g, wb_bf16, wo_bf16, *, tokens_per_mod):
    T = x.shape[0]
    tm = 256
    tiles_per_mod = tokens_per_mod // tm
    yspec = pl.BlockSpec((tm, D_BRANCH), lambda i: (i, 0))
    xspec = pl.BlockSpec((tm, D_MODEL), lambda i: (i, 0))
    return pl.pallas_call(
        _merge_kernel,
        grid=(T // tm,),
        in_specs=[yspec, yspec, yspec, yspec,
                  pl.BlockSpec((tm, N_BRANCH * D_MODEL), lambda i: (i, P_MERGE // (N_BRANCH * D_MODEL))),
                  xspec,
                  pl.BlockSpec((None, 6, D_MODEL), lambda i: (i // tiles_per_mod, 0, 0)),
                  pl.BlockSpec((1, D_BRANCH), lambda i: (0, 0)),
                  pl.BlockSpec((1, D_MODEL), lambda i: (0, 0)),
                  pl.BlockSpec((N_BRANCH, D_BRANCH, D_MODEL), lambda i: (0, 0, 0)),
                  pl.BlockSpec((D_MODEL, D_MODEL), lambda i: (0, 0))],
        out_specs=[xspec, xspec],
        out_shape=[jax.ShapeDtypeStruct((T, D_MODEL), F32)] * 2,
        compiler_params=_cparams(("arbitrary",)),
        name="merge_outproj",
    )(ya, yb, yc, yd, proj, x, mod, ssd_norm.reshape(1, D_BRANCH), norm2_g.reshape(1, D_MODEL), wb_bf16, wo_bf16)


def _router_kernel(h_ref, rw_ref, rb_ref, idx_ref, wt_ref, rank_ref, cnt_ref, carry):
    i = pl.program_id(0)
    tm = h_ref.shape[0]
    E = N_EXPERTS
    gsz = E // N_EXPERT_GROUPS

    @pl.when(i == 0)
    def _():
        carry[...] = jnp.zeros_like(carry)

    scores = _sigmoid(_dot(h_ref[...], rw_ref[...], precision=HIGHEST))
    biased = scores + rb_ref[...]
    lane = _iota((tm, E), 1)
    grp = jnp.right_shift(lane, int(math.log2(gsz)))
    neg_inf = -jnp.inf

    gs = []
    for g in range(N_EXPERT_GROUPS):
        vals = jnp.where(grp == g, biased, neg_inf)
        m1 = jnp.max(vals, axis=1, keepdims=True)
        i1 = jnp.min(jnp.where(vals == m1, lane, E), axis=1, keepdims=True)
        m2 = jnp.max(jnp.where(lane == i1, neg_inf, vals), axis=1, keepdims=True)
        gs.append(m1 + m2)
    exp_mask = jnp.zeros((tm, E), jnp.bool_)
    for g in range(N_EXPERT_GROUPS):
        rank = jnp.zeros((tm, 1), I32)
        for g2 in range(N_EXPERT_GROUPS):
            if g2 == g:
                continue
            ahead = (gs[g2] >= gs[g]) if g2 < g else (gs[g2] > gs[g])
            rank = rank + ahead.astype(I32)
        exp_mask = exp_mask | ((grp == g) & (rank < TOPK_GROUPS))
    cur = jnp.where(exp_mask, biased, MASK_NEG)

    olane = _iota((tm, LANES), 1)
    idx_out = jnp.zeros((tm, LANES), I32)
    wt_out = jnp.zeros((tm, LANES), F32)
    onehot = jnp.zeros((tm, E), F32)
    picks = []
    wsum = jnp.zeros((tm, 1), F32)
    for kk in range(TOP_K):
        m = jnp.max(cur, axis=1, keepdims=True)
        ik = jnp.min(jnp.where(cur == m, lane, E), axis=1, keepdims=True)
        hit = lane == ik
        wk = jnp.sum(jnp.where(hit, scores, 0.0), axis=1, keepdims=True)
        cur = jnp.where(hit, neg_inf, cur)
        onehot = onehot + hit.astype(F32)
        picks.append((ik, wk))
        wsum = wsum + wk
        idx_out = jnp.where(olane == kk, ik, idx_out)
    for kk, (ik, wk) in enumerate(picks):
        wt_out = jnp.where(olane == kk, wk / wsum * ROUTED_SCALE, wt_out)

    strict_lower = (_iota((tm, tm), 1) < _iota((tm, tm), 0)).astype(BF16)
    before = _dot(strict_lower, onehot.astype(BF16)) + carry[...]
    rank_out = jnp.zeros((tm, LANES), I32)
    for kk, (ik, wk) in enumerate(picks):
        rk = jnp.sum(jnp.where(lane == ik, before, 0.0), axis=1, keepdims=True)
        rank_out = jnp.where(olane == kk, rk.astype(I32), rank_out)
    carry[...] = carry[...] + jnp.sum(onehot, axis=0, keepdims=True)

    idx_ref[...] = idx_out
    wt_ref[...] = wt_out
    rank_ref[...] = rank_out
    cnt_ref[...] = carry[...]


def _router(h2, router_w, router_b):
    T = h2.shape[0]
    tm = 256
    ospec = pl.BlockSpec((tm, LANES), lambda i: (i, 0))
    return pl.pallas_call(
        _router_kernel,
        grid=(T // tm,),
        in_specs=[pl.BlockSpec((tm, D_MODEL), lambda i: (i, 0)),
                  pl.BlockSpec((D_MODEL, N_EXPERTS), lambda i: (0, 0)),
                  pl.BlockSpec((1, N_EXPERTS), lambda i: (0, 0))],
        out_specs=[ospec, ospec, ospec, pl.BlockSpec((1, N_EXPERTS), lambda i: (0, 0))],
        out_shape=[jax.ShapeDtypeStruct((T, LANES), I32), jax.ShapeDtypeStruct((T, LANES), F32),
                   jax.ShapeDtypeStruct((T, LANES), I32), jax.ShapeDtypeStruct((1, N_EXPERTS), F32)],
        scratch_shapes=[pltpu.VMEM((1, N_EXPERTS), F32)],
        compiler_params=_cparams(("arbitrary",)),
        name="moe_router",
    )(h2, router_w, router_b.reshape(1, N_EXPERTS))


def _dispatch_kernel(dest_ref, h_ref, xs_ref, sem):
    tm = h_ref.shape[0]
    n = tm * TOP_K

    def row_copy(a):
        return pltpu.make_async_copy(h_ref.at[pl.ds(a // TOP_K, 1), :], xs_ref.at[pl.ds(dest_ref[a], 1), :], sem)

    def start(a, carry):
        row_copy(a).start()
        return carry

    def wait(a, carry):
        row_copy(a).wait()
        return carry

    lax.fori_loop(0, n, start, 0)
    lax.fori_loop(0, n, wait, 0)


def _dispatch(h2, dest_flat):
    T = h2.shape[0]
    tm = 256
    return pl.pallas_call(
        _dispatch_kernel,
        grid=(T // tm,),
        in_specs=[pl.BlockSpec((tm * TOP_K,), lambda i: (i,), memory_space=pltpu.SMEM),
                  pl.BlockSpec((tm, D_MODEL), lambda i: (i, 0))],
        out_specs=pl.BlockSpec(memory_space=pl.ANY),
        out_shape=jax.ShapeDtypeStruct((T * TOP_K, D_MODEL), F32),
        scratch_shapes=[pltpu.SemaphoreType.DMA(())],
        compiler_params=_cparams(("arbitrary",)),
        name="moe_dispatch",
    )(dest_flat, h2)


def _expert_kernel(blk_ref, exp_ref, lo_ref, hi_ref, first_ref, x_ref, w1_ref, w3_ref, w2_ref, o_ref):
    w = pl.program_id(0)
    x = x_ref[...].astype(BF16)
    a = _dot(x, w1_ref[...].astype(BF16))
    b = _dot(x, w3_ref[...].astype(BF16))
    y = _dot((_silu(a) * b).astype(BF16), w2_ref[...].astype(BF16))
    r = _iota(y.shape, 0)
    y = jnp.where((r >= lo_ref[w]) & (r < hi_ref[w]), y, 0.0)

    @pl.when(first_ref[w] == 1)
    def _():
        o_ref[...] = y

    @pl.when(first_ref[w] == 0)
    def _():
        o_ref[...] = o_ref[...] + y


def _experts(xs, work, w1, w3, w2, *, bm):
    A = xs.shape[0]
    n_work = work[0].shape[0]
    xmap = lambda w, blk, ex, lo, hi, first: (blk[w], 0)
    wmap = lambda w, blk, ex, lo, hi, first: (ex[w], 0, 0)
    return pl.pallas_call(
        _expert_kernel,
        grid_spec=pltpu.PrefetchScalarGridSpec(
            num_scalar_prefetch=5,
            grid=(n_work,),
            in_specs=[pl.BlockSpec((bm, D_MODEL), xmap),
                      pl.BlockSpec((None, D_MODEL, D_EXPERT), wmap),
                      pl.BlockSpec((None, D_MODEL, D_EXPERT), wmap),
                      pl.BlockSpec((None, D_EXPERT, D_MODEL), wmap)],
            out_specs=pl.BlockSpec((bm, D_MODEL), xmap)),
        out_shape=jax.ShapeDtypeStruct((A, D_MODEL), F32),
        compiler_params=_cparams(("arbitrary",)),
        name="moe_experts",
    )(*work, xs, w1, w3, w2)


def _combine_kernel(dest_ref, ys_ref, wt_ref, h_ref, x_ref, mod_ref, s1_ref, s3_ref, s2_ref, fg_ref,
                    o_ref, buf, sem, *, final_norm):
    tc = h_ref.shape[0]
    n = tc * TOP_K

    def row_copy(a):
        return pltpu.make_async_copy(ys_ref.at[pl.ds(dest_ref[a], 1), :],
                                     buf.at[a % TOP_K, pl.ds(a // TOP_K, 1), :], sem)

    def start(a, carry):
        row_copy(a).start()
        return carry

    def wait(a, carry):
        row_copy(a).wait()
        return carry

    lax.fori_loop(0, n, start, 0)
    hb = h_ref[...].astype(BF16)
    mid = _silu(_dot(hb, s1_ref[...])) * _dot(hb, s3_ref[...])
    y = _dot(mid.astype(BF16), s2_ref[...])
    lax.fori_loop(0, n, wait, 0)
    for kk in range(TOP_K):
        y = y + wt_ref[:, kk:kk + 1] * buf[kk]
    x = x_ref[...] + mod_ref[5:6, :] * y
    if final_norm:
        x = x * lax.rsqrt(jnp.mean(x * x, axis=-1, keepdims=True) + EPS) * fg_ref[...]
    o_ref[...] = x


def _combine(ys, dest_flat, wt, h2, x, mod, s1, s3, s2, final_g, *, tokens_per_mod, final_norm):
    T = x.shape[0]
    tc = 128
    tiles_per_mod = tokens_per_mod // tc
    xspec = pl.BlockSpec((tc, D_MODEL), lambda i: (i, 0))
    return pl.pallas_call(
        functools.partial(_combine_kernel, final_norm=final_norm),
        grid=(T // tc,),
        in_specs=[pl.BlockSpec((tc * TOP_K,), lambda i: (i,), memory_space=pltpu.SMEM),
                  pl.BlockSpec(memory_space=pl.ANY),
                  pl.BlockSpec((tc, LANES), lambda i: (i, 0)),
                  xspec, xspec,
                  pl.BlockSpec((None, 6, D_MODEL), lambda i: (i // tiles_per_mod, 0, 0)),
                  pl.BlockSpec((D_MODEL, D_SHARED), lambda i: (0, 0)),
                  pl.BlockSpec((D_MODEL, D_SHARED), lambda i: (0, 0)),
                  pl.BlockSpec((D_SHARED, D_MODEL), lambda i: (0, 0)),
                  pl.BlockSpec((1, D_MODEL), lambda i: (0, 0))],
        out_specs=xspec,
        out_shape=jax.ShapeDtypeStruct((T, D_MODEL), F32),
        scratch_shapes=[pltpu.VMEM((TOP_K, tc, D_MODEL), F32), pltpu.SemaphoreType.DMA(())],
        compiler_params=_cparams(("arbitrary",)),
        name="moe_combine",
    )(dest_flat, ys, wt, h2, x, mod, s1, s3, s2, final_g.reshape(1, D_MODEL))


def _work_list(counts, n_rows, bm):
    E = N_EXPERTS
    n_blocks = n_rows // bm
    n_work = n_blocks + E - 1
    ends = jnp.cumsum(counts)
    starts = ends - counts
    first_blk = starts // bm
    last_blk = jnp.where(counts > 0, (ends - 1) // bm, first_blk)
    n_items = jnp.where(counts > 0, last_blk - first_blk + 1, 0)
    item_end = jnp.cumsum(n_items)
    item_start = item_end - n_items
    w = jnp.arange(n_work, dtype=I32)
    ex = jnp.minimum(jnp.searchsorted(item_end, w, side="right"), E - 1).astype(I32)
    valid = w < item_end[-1]
    blk = jnp.where(valid, first_blk[ex] + (w - item_start[ex]), n_blocks - 1).astype(I32)
    lo = jnp.clip(starts[ex] - blk * bm, 0, bm)
    hi = jnp.clip(ends[ex] - blk * bm, 0, bm)
    lo = jnp.where(valid, lo, 0).astype(I32)
    hi = jnp.where(valid, hi, 0).astype(I32)
    ex = jnp.where(valid, ex, ex[jnp.maximum(item_end[-1] - 1, 0)])
    prev_blk = jnp.concatenate([jnp.full((1,), -1, I32), blk[:-1]])
    first = (blk != prev_blk).astype(I32)
    return blk, ex.astype(I32), lo, hi, first


def _moe(h2, x, mod, p, final_g, *, tokens_per_mod, final_norm):
    T = x.shape[0]
    idx, wt, rank, counts = _router(h2, p["router_w"], p["router_b"])
    counts = counts.reshape(N_EXPERTS).astype(I32)
    starts = jnp.cumsum(counts) - counts
    dest = (starts[idx[:, :TOP_K]] + rank[:, :TOP_K]).reshape(T * TOP_K)
    xs = _dispatch(h2, dest)
    bm = 512
    work = _work_list(counts, T * TOP_K, bm)
    ys = _experts(xs, work, p["exp_w1"], p["exp_w3"], p["exp_w2"], bm=bm)
    return _combine(ys, dest, wt, h2, x, mod, p["sh_w1"], p["sh_w3"], p["sh_w2"], final_g,
                    tokens_per_mod=tokens_per_mod, final_norm=final_norm)


def _reorder_w_in(w_in):
    plain = D_CONV
    pieces = [w_in[..., :plain],
              w_in[..., plain + 3616:plain + 3616 + 4096],
              w_in[..., plain:plain + 3072],
              w_in[..., plain + 3088:plain + 3600],
              w_in[..., plain + 3072:plain + 3088],
              w_in[..., plain + 3600:plain + 3616]]
    out = jnp.concatenate(pieces, axis=-1)
    pad = D_PROJ - out.shape[-1]
    return jnp.pad(out, ((0, 0), (0, 0), (0, pad))).astype(BF16)


def _small_rows(ssd_vals, gdn_vals):
    row = jnp.zeros((LANES,), F32)
    row = row.at[S_DT:S_DT + 2 * SSD_HEADS].set(ssd_vals.reshape(-1))
    row = row.at[S_GA:S_GA + 2 * GDN_HEADS].set(gdn_vals.reshape(-1))
    return row.reshape(1, LANES)


def _layer_pass(x, mod, lp, hy, st_hg, st_ssd, st_gdn, final_g, *, B, L, tokens_per_mod, seg, layer, final_norm):
    T = B * L
    proj = _in_projection(x, mod, lp["norm1_g"], lp["w_in"], lp["conv_w"], lp["conv_b"],
                          tokens_per_mod=tokens_per_mod, seg=seg)
    proj3 = proj.reshape(B, L, D_PROJ)
    fmat, gmat, hspec = hy
    cb = lambda c: c // D_BRANCH
    z = _spectral_conv(fmat, gmat, hspec, 0, lp["hy_bias"][0], proj3, cb(C_HYV), proj3, cb(C_HYX1))
    ya = _spectral_conv(fmat, gmat, hspec, 1, lp["hy_bias"][1], z, 0, proj3, cb(C_HYX2))
    yb, s_hg = _hgrn_scan(proj3, lp["hg_lb"], lp["hg_norm"], st_hg, layer=layer)
    yc, s_ssd = _ssd_scan(proj3, lp["bias_row"], lp["alog_row"], lp["ssd_d"], st_ssd)
    yd, s_gdn = _gdn_scan(proj3, lp["bias_row"], lp["alog_row"], lp["gdn_norm"], st_gdn)
    flat = lambda a: a.reshape(T, D_BRANCH)
    x, h2 = _merge(flat(ya), flat(yb), flat(yc), flat(yd), proj, x, mod, lp["ssd_norm"], lp["norm2_g"],
                   lp["w_branch"], lp["w_out"], tokens_per_mod=tokens_per_mod)
    x = _moe(h2, x, mod, lp, final_g, tokens_per_mod=tokens_per_mod, final_norm=final_norm)
    return x, s_hg, s_ssd, s_gdn


def kernel(x_prompt, x_sample, state_hgrn, state_ssd, state_gdn, c, c_ctx, norm1_g, norm2_g, ada_w, ada_b, w_in, conv_w, conv_b, hy_w1, hy_b1, hy_w2, hy_b2, hy_w3, hy_bias, hg_lb, hg_norm, ssd_a_log, ssd_dt_bias, ssd_d, ssd_norm, gdn_a_log, gdn_dt_bias, gdn_norm, w_branch, w_out, router_w, router_bias, exp_w1, exp_w3, exp_w2, sh_w1, sh_w3, sh_w2, final_g):
    depth = w_in.shape[0]
    bp, lp_len = x_prompt.shape[0], x_prompt.shape[1]
    bs, ls_len = x_sample.shape[0], x_sample.shape[1]
    D = D_MODEL

    w_in_r = _reorder_w_in(w_in)
    hy_w1p = jnp.pad(hy_w1, ((0, 0), (0, LANES - hy_w1.shape[1]), (0, 0)))
    layers = []
    for l in range(depth):
        layers.append(dict(
            norm1_g=norm1_g[l], norm2_g=norm2_g[l], w_in=w_in_r[l], conv_w=conv_w[l], conv_b=conv_b[l],
            hy_bias=hy_bias[l], hg_lb=hg_lb, hg_norm=hg_norm[l],
            bias_row=_small_rows(ssd_dt_bias[l], gdn_dt_bias[l]),
            alog_row=_small_rows(ssd_a_log[l], gdn_a_log[l]),
            ssd_d=ssd_d[l], ssd_norm=ssd_norm[l], gdn_norm=gdn_norm[l],
            w_branch=w_branch[l].astype(BF16), w_out=w_out[l].astype(BF16),
            router_w=router_w[l], router_b=router_bias[l],
            exp_w1=exp_w1[l], exp_w3=exp_w3[l], exp_w2=exp_w2[l],
            sh_w1=sh_w1[l].astype(BF16), sh_w3=sh_w3[l].astype(BF16), sh_w2=sh_w2[l].astype(BF16)))

    def hyena_setup(L):
        fmat, fs = _dft_matrices(L)
        gmat = fs.T
        specs = []
        for l in range(depth):
            filt = _hyena_filters(L, hy_w1p[l], hy_b1[l], hy_w2[l], hy_b2[l], hy_w3[l])
            specs.append(_filter_spectrum(fmat, filt))
        return fmat, gmat, specs

    cond = jnp.concatenate([c_ctx.reshape(1, D), c], axis=0)
    rows = cond.shape[0]
    rows8 = (rows + 7) // 8 * 8
    cond8 = jnp.pad(cond, ((0, rows8 - rows), (0, 0)))
    mods = [_modulation(cond8, ada_w[l], ada_b[l]).reshape(rows8, 6, D) for l in range(depth)]

    fmat, gmat, specs = hyena_setup(lp_len)
    x = x_prompt.reshape(bp * lp_len, D)
    z_hg = jnp.zeros((bp, 2, HG_HEADS, HG_DK, HG_DK), F32)
    z_ssd = jnp.zeros((bp, 2, SSD_HEADS, SSD_P, SSD_N), F32)
    z_gdn = jnp.zeros((bp, 2, GDN_HEADS, GDN_DK, GDN_DK), F32)
    hg_states, ssd_states, gdn_states = [], [], []
    for l in range(depth):
        x, s_hg, s_ssd, s_gdn = _layer_pass(
            x, mods[l][0:1], layers[l], (fmat, gmat, specs[l]), z_hg, z_ssd, z_gdn, final_g,
            B=bp, L=lp_len, tokens_per_mod=bp * lp_len, seg=lp_len, layer=l, final_norm=(l == depth - 1))
        hg_states.append(s_hg)
        ssd_states.append(s_ssd)
        gdn_states.append(s_gdn)
    y_prompt = x.reshape(bp, lp_len, D)
    new_hg = jnp.stack(hg_states, axis=1)
    new_ssd = jnp.stack(ssd_states, axis=1)
    new_gdn = jnp.stack(gdn_states, axis=1)

    fmat, gmat, specs = hyena_setup(ls_len)
    x = x_sample.reshape(bs * ls_len, D)
    for l in range(depth):
        x, _, _, _ = _layer_pass(
            x, mods[l][1:1 + bs], layers[l], (fmat, gmat, specs[l]),
            state_hgrn[:, l], state_ssd[:, l], state_gdn[:, l], final_g,
            B=bs, L=ls_len, tokens_per_mod=ls_len, seg=GRID_W, layer=l, final_norm=(l == depth - 1))
    y_sample = x.reshape(bs, ls_len, D)
    return (y_prompt, y_sample, new_hg, new_ssd, new_gdn)
```

```python
import functools
import math

import numpy as np
import jax
import jax.numpy as jnp
from jax import lax
from jax.experimental import pallas as pl
from jax.experimental.pallas import tpu as pltpu

F32 = jnp.float32
BF16 = jnp.bfloat16
I32 = jnp.int32
HIGHEST = lax.Precision.HIGHEST

D_MODEL = 1024
GRID_W = 64
EPS = 1e-6
LOG_FLOOR = 1e-30
MASK_NEG = -1e30
N_BRANCH = 4
D_BRANCH = 512
HY_POS_FREQS = 16
HY_FILTER_HIDDEN = 64
HY_FAST_DECAY = 0.3
HY_SLOW_DECAY = 1.5
HY_DECAY_TARGET = 1e-2
HG_HEADS = 4
HG_DK = 128
HG_CHUNK = 16
SSD_HEADS = 8
SSD_P = 64
SSD_N = 128
SSD_CHUNK = 64
GDN_HEADS = 4
GDN_DK = 128
GDN_CHUNK = 64
N_EXPERTS = 64
TOP_K = 8
N_EXPERT_GROUPS = 8
TOPK_GROUPS = 4
D_EXPERT = 256
D_SHARED = 256
ROUTED_SCALE = 2.5

LANES = 128

D_CONV = 4096
C_HYV, C_HYX1, C_HYX2 = 0, 512, 1024
C_SSDX, C_SSDB, C_SSDC = 1536, 2048, 2304
C_GQ, C_GK, C_GV = 2560, 3072, 3584
P_MERGE = 4096
P_HGQ, P_HGFF, P_HGFB, P_HGV, P_HGG = 8192, 8704, 9216, 9728, 10240
P_SSDZ, P_GDNG, P_SMALL = 10752, 11264, 11776
D_PROJ = 12288
S_DT, S_GA, S_GB = 0, 16, 24

VMEM_LIMIT = 56 * 1024 * 1024


def _cparams(sem):
    return pltpu.CompilerParams(dimension_semantics=sem, vmem_limit_bytes=VMEM_LIMIT)


def _sigmoid(x):
    return 1.0 / (1.0 + jnp.exp(-x))


def _silu(x):
    return x * _sigmoid(x)


def _softplus(x):
    return jnp.maximum(x, 0.0) + jnp.log(1.0 + jnp.exp(-jnp.abs(x)))


def _dot(a, b, precision=None):
    return jnp.dot(a, b, preferred_element_type=F32, precision=precision)


def _dot_nt(a, b, precision=None):
    return lax.dot_general(a, b, (((1,), (1,)), ((), ())), preferred_element_type=F32, precision=precision)


def _dot_tn(a, b, precision=None):
    return lax.dot_general(a, b, (((0,), (0,)), ((), ())), preferred_element_type=F32, precision=precision)


def _bdot(a, b):
    return _dot(a.astype(BF16), b.astype(BF16))


def _bdot_nt(a, b):
    return _dot_nt(a.astype(BF16), b.astype(BF16))


def _bdot_tn(a, b):
    return _dot_tn(a.astype(BF16), b.astype(BF16))


def _iota(shape, dim):
    return lax.broadcasted_iota(I32, shape, dim)


def _cumsum_rows(g, reverse):
    n = g.shape[0]
    row = _iota(g.shape, 0)
    sh = 1
    while sh < n:
        if reverse:
            g = g + jnp.where(row < n - sh, pltpu.roll(g, n - sh, 0), 0.0)
        else:
            g = g + jnp.where(row >= sh, pltpu.roll(g, sh, 0), 0.0)
        sh *= 2
    return g


def _lane_pick(a, j):
    return jnp.sum(jnp.where(_iota(a.shape, 1) == j, a, 0.0), axis=1, keepdims=True)


def _split3(a):
    a1 = a.astype(BF16)
    r1 = a - a1.astype(F32)
    a2 = r1.astype(BF16)
    a3 = (r1 - a2.astype(F32)).astype(BF16)
    return a1, a2, a3


def _dot3(a, b):
    a1, a2, _ = _split3(a)
    b1, b2, _ = _split3(b)
    return _dot(a1, b1) + (_dot(a1, b2) + _dot(a2, b1))


def _row_pick(a, j):
    sel = (_iota((8, a.shape[1]), 1) == j).astype(BF16)
    a1, a2, a3 = _split3(a)
    return (_dot_nt(sel, a1) + (_dot_nt(sel, a2) + _dot_nt(sel, a3)))[0:1, :]


def _tri_mask(n, reverse, strict=False):
    t = _iota((n, n), 0)
    s = _iota((n, n), 1)
    if reverse:
        return (s > t) if strict else (s >= t)
    return (s < t) if strict else (s <= t)


def _masked_exp(mask, diff):
    return jnp.where(mask, jnp.exp(jnp.where(mask, diff, 0.0)), 0.0)


def _mod_kernel(c_ref, w_ref, b_ref, o_ref):
    o_ref[...] = _dot(_silu(c_ref[...]), w_ref[...], precision=HIGHEST) + b_ref[...]


def _modulation(cond8, ada_w, ada_b):
    rows = cond8.shape[0]
    tn = 1536
    n = ada_w.shape[1]
    return pl.pallas_call(
        _mod_kernel,
        grid=(n // tn,),
        in_specs=[pl.BlockSpec((rows, D_MODEL), lambda j: (0, 0)),
                  pl.BlockSpec((D_MODEL, tn), lambda j: (0, j)),
                  pl.BlockSpec((1, tn), lambda j: (0, j))],
        out_specs=pl.BlockSpec((rows, tn), lambda j: (0, j)),
        out_shape=jax.ShapeDtypeStruct((rows, n), F32),
        compiler_params=_cparams(("arbitrary",)),
        name="adaln_mod",
    )(cond8, ada_w, ada_b.reshape(1, n))


def _inproj_kernel(x_ref, mod_ref, g_ref, w_ref, cw_ref, cb_ref, o_ref, h_scr, *, seg, n_conv_tiles):
    j = pl.program_id(1)

    @pl.when(j == 0)
    def _():
        x = x_ref[...]
        xn = x * lax.rsqrt(jnp.mean(x * x, axis=-1, keepdims=True) + EPS) * g_ref[...]
        m = mod_ref[...]
        h_scr[...] = (xn * (1.0 + m[1:2]) + m[0:1]).astype(BF16)

    y = _dot(h_scr[...], w_ref[...])

    @pl.when(j < n_conv_tiles)
    def _():
        tm = y.shape[0]
        pos = _iota(y.shape, 0) & (seg - 1)
        prev = jnp.where(pos == 0, 0.0, pltpu.roll(y, 1, 0))
        nxt = jnp.where(pos == seg - 1, 0.0, pltpu.roll(y, tm - 1, 0))
        cw = cw_ref[...]
        o_ref[...] = cb_ref[...] + prev * cw[0:1] + y * cw[1:2] + nxt * cw[2:3]

    @pl.when(j >= n_conv_tiles)
    def _():
        o_ref[...] = y


def _in_projection(x, mod, norm_g, w_bf16, conv_w, conv_b, *, tokens_per_mod, seg):
    T = x.shape[0]
    tm = min(1024, tokens_per_mod)
    tn = 1024
    n_conv_tiles = D_CONV // tn
    tiles_per_mod = tokens_per_mod // tm
    kern = functools.partial(_inproj_kernel, seg=seg, n_conv_tiles=n_conv_tiles)
    cmap = lambda i, j: (0, jnp.minimum(j, n_conv_tiles - 1))
    return pl.pallas_call(
        kern,
        grid=(T // tm, D_PROJ // tn),
        in_specs=[pl.BlockSpec((tm, D_MODEL), lambda i, j: (i, 0)),
                  pl.BlockSpec((None, 6, D_MODEL), lambda i, j: (i // tiles_per_mod, 0, 0)),
                  pl.BlockSpec((1, D_MODEL), lambda i, j: (0, 0)),
                  pl.BlockSpec((D_MODEL, tn), lambda i, j: (0, j)),
                  pl.BlockSpec((3, tn), cmap),
                  pl.BlockSpec((1, tn), cmap)],
        out_specs=pl.BlockSpec((tm, tn), lambda i, j: (i, j)),
        out_shape=jax.ShapeDtypeStruct((T, D_PROJ), F32),
        scratch_shapes=[pltpu.VMEM((tm, D_MODEL), BF16)],
        compiler_params=_cparams(("arbitrary", "arbitrary")),
        name="in_proj",
    )(x, mod, norm_g.reshape(1, D_MODEL), w_bf16, conv_w, conv_b.reshape(1, D_CONV))


def _hyfilt_kernel(w1_ref, b1_ref, w2_ref, b2_ref, w3_ref, o_ref, *, L):
    i = pl.program_id(1)
    tl = o_ref.shape[0]
    t = (_iota((tl, LANES), 0) + i * tl).astype(F32) / L
    lane = _iota((tl, LANES), 1)
    band = jnp.where(lane <= HY_POS_FREQS, lane, lane - HY_POS_FREQS).astype(F32)
    ang = 2.0 * math.pi * t * band
    feats = jnp.where(lane == 0, t,
                      jnp.where(lane <= HY_POS_FREQS, jnp.sin(ang),
                                jnp.where(lane <= 2 * HY_POS_FREQS, jnp.cos(ang), 0.0)))
    hdn = jnp.sin(_dot(feats, w1_ref[...], precision=HIGHEST) + b1_ref[...])
    hdn = jnp.sin(_dot(hdn, w2_ref[...], precision=HIGHEST) + b2_ref[...])
    filt = _dot(hdn, w3_ref[...], precision=HIGHEST)
    max_decay = math.log(HY_DECAY_TARGET) / HY_FAST_DECAY
    min_decay = math.log(HY_DECAY_TARGET) / HY_SLOW_DECAY
    n = filt.shape[1]
    ch = (_iota((tl, n), 1) & (D_BRANCH - 1)).astype(F32)
    delta = min_decay + ch * ((max_decay - min_decay) / (D_BRANCH - 1))
    tt = (_iota((tl, n), 0) + i * tl).astype(F32) / L
    o_ref[...] = filt * jnp.exp(-tt * jnp.abs(delta))


def _hyena_filters(L, w1p, b1, w2, b2, w3):
    tl = min(L, 256)
    n = 2 * D_BRANCH
    return pl.pallas_call(
        functools.partial(_hyfilt_kernel, L=L),
        grid=(2, L // tl),
        in_specs=[pl.BlockSpec((LANES, HY_FILTER_HIDDEN), lambda d, i: (0, 0)),
                  pl.BlockSpec((1, HY_FILTER_HIDDEN), lambda d, i: (0, 0)),
                  pl.BlockSpec((HY_FILTER_HIDDEN, HY_FILTER_HIDDEN), lambda d, i: (0, 0)),
                  pl.BlockSpec((1, HY_FILTER_HIDDEN), lambda d, i: (0, 0)),
                  pl.BlockSpec((HY_FILTER_HIDDEN, n), lambda d, i: (0, d))],
        out_specs=pl.BlockSpec((None, tl, n), lambda d, i: (d, i, 0)),
        out_shape=jax.ShapeDtypeStruct((2, L, n), F32),
        compiler_params=_cparams(("arbitrary", "arbitrary")),
        name="hyena_filters",
    )(w1p, b1.reshape(1, -1), w2, b2.reshape(1, -1), w3)


def _dftgen_kernel(f_ref, fs_ref, *, L, tk):
    i = pl.program_id(0)
    N = 2 * L
    k = _iota((tk, LANES), 0) + i * tk
    lane = _iota((tk, LANES), 1)
    w = 2.0 * math.pi / N
    a0 = ((k * lane) & (N - 1)).astype(F32) * w
    c0, s0 = jnp.cos(a0), jnp.sin(a0)
    a1 = ((k * lane * LANES) & (N - 1)).astype(F32) * w
    c1, s1 = jnp.cos(a1), jnp.sin(a1)
    alt = jnp.where((lane & 1) == 0, 1.0, -1.0)
    coef = jnp.where(k == 0, 1.0 / N, 2.0 / N)
    for n1 in range(L // LANES):
        c1b = c1[:, n1:n1 + 1]
        s1b = s1[:, n1:n1 + 1]
        cosb = c1b * c0 - s1b * s0
        sinb = jnp.where(k == 0, alt, -(s1b * c0 + c1b * s0))
        cols = slice(n1 * LANES, (n1 + 1) * LANES)
        f_ref[0:tk, cols] = cosb.astype(BF16)
        f_ref[tk:2 * tk, cols] = sinb.astype(BF16)
        fs_ref[0:tk, cols] = (coef * cosb).astype(BF16)
        fs_ref[tk:2 * tk, cols] = (coef * sinb).astype(BF16)


def _dft_matrices(L):
    tk = min(L, 256)
    spec = pl.BlockSpec((2 * tk, L), lambda i: (i, 0))
    return pl.pallas_call(
        functools.partial(_dftgen_kernel, L=L, tk=tk),
        grid=(L // tk,),
        in_specs=[],
        out_specs=[spec, spec],
        out_shape=[jax.ShapeDtypeStruct((2 * L, L), BF16)] * 2,
        compiler_params=_cparams(("arbitrary",)),
        name="dft_matrices",
    )()


def _dfth_kernel(f_ref, h_ref, o_ref, hp_scr, *, tk):
    i = pl.program_id(1)
    n = o_ref.shape[1]

    @pl.when(i == 0)
    def _():
        hc = h_ref[0]
        ha = jnp.where(_iota(hc.shape, 0) == 0, 0.0, h_ref[1])
        hp_scr[:, 0:n] = (hc + ha).astype(BF16)
        hp_scr[:, n:2 * n] = (hc - ha).astype(BF16)

    u = _dot(f_ref[...], hp_scr[...])
    r = _iota((2 * tk, n), 0)
    from_sum = (r < tk) | ((r == tk) & (i == 0))
    o_ref[...] = jnp.where(from_sum, u[:, 0:n], u[:, n:2 * n])


def _filter_spectrum(fmat, filt):
    L = fmat.shape[1]
    tk = min(L, 256)
    C = filt.shape[2]
    tc = 256
    return pl.pallas_call(
        functools.partial(_dfth_kernel, tk=tk),
        grid=(C // tc, L // tk),
        in_specs=[pl.BlockSpec((2 * tk, L), lambda c, i: (i, 0)),
                  pl.BlockSpec((2, L, tc), lambda c, i: (0, 0, c))],
        out_specs=pl.BlockSpec((2 * tk, tc), lambda c, i: (i, c)),
        out_shape=jax.ShapeDtypeStruct((2 * L, C), F32),
        scratch_shapes=[pltpu.VMEM((L, 2 * tc), BF16)],
        compiler_params=_cparams(("arbitrary", "arbitrary")),
        name="filter_spectrum",
    )(fmat, filt)


def _dfta_kernel(f_ref, u_ref, h_ref, y_ref, u_scr, *, tk):
    i = pl.program_id(1)

    @pl.when(i == 0)
    def _():
        u_scr[...] = u_ref[...].astype(BF16)

    uf = _dot(f_ref[...], u_scr[...])
    ur, ui = uf[0:tk], uf[tk:2 * tk]
    hr, hi = h_ref[0:tk, :], h_ref[tk:2 * tk, :]
    dc = (_iota(ur.shape, 0) == 0) & (i == 0)
    y_ref[0:tk, :] = jnp.where(dc, ur * hr, ur * hr - ui * hi).astype(BF16)
    y_ref[tk:2 * tk, :] = jnp.where(dc, ui * hi, ur * hi + ui * hr).astype(BF16)


def _dftb_kernel(g_ref, y_ref, u_ref, x_ref, b_ref, o_ref):
    y = _dot(g_ref[...], y_ref[...])
    o_ref[...] = x_ref[...] * (y + u_ref[...] * b_ref[...])


def _spectral_conv(fmat, gmat, hspec, h_col, bias, u_arr, u_col, gate_arr, gate_col):
    B, L = u_arr.shape[0], u_arr.shape[1]
    C = D_BRANCH
    tk = min(L, 256)
    y = pl.pallas_call(
        functools.partial(_dfta_kernel, tk=tk),
        grid=(B, L // tk),
        in_specs=[pl.BlockSpec((2 * tk, L), lambda b, i: (i, 0)),
                  pl.BlockSpec((None, L, C), lambda b, i: (b, 0, u_col)),
                  pl.BlockSpec((2 * tk, C), lambda b, i: (i, h_col))],
        out_specs=pl.BlockSpec((None, 2 * tk, C), lambda b, i: (b, i, 0)),
        out_shape=jax.ShapeDtypeStruct((B, 2 * L, C), BF16),
        scratch_shapes=[pltpu.VMEM((L, C), BF16)],
        compiler_params=_cparams(("arbitrary", "arbitrary")),
        name="hyena_dft_fwd",
    )(fmat, u_arr, hspec)
    tr = min(L, 256)
    return pl.pallas_call(
        _dftb_kernel,
        grid=(B, L // tr),
        in_specs=[pl.BlockSpec((tr, 2 * L), lambda b, i: (i, 0)),
                  pl.BlockSpec((None, 2 * L, C), lambda b, i: (b, 0, 0)),
                  pl.BlockSpec((None, tr, C), lambda b, i: (b, i, u_col)),
                  pl.BlockSpec((None, tr, C), lambda b, i: (b, i, gate_col)),
                  pl.BlockSpec((1, C), lambda b, i: (0, 0))],
        out_specs=pl.BlockSpec((None, tr, C), lambda b, i: (b, i, 0)),
        out_shape=jax.ShapeDtypeStruct((B, L, C), F32),
        compiler_params=_cparams(("arbitrary", "arbitrary")),
        name="hyena_dft_inv",
    )(gmat, y, u_arr, gate_arr, bias.reshape(1, C))


def _hgrn_kernel(q_ref, ff_ref, fb_ref, v_ref, g_ref, lb_ref, nrm_ref, s0_ref, y_ref, sf_ref,
                 of_scr, ob_scr, st_scr, *, L, layer, depth):
    C = HG_CHUNK
    W = HG_DK
    nc = L // C
    nh = q_ref.shape[1] // W
    ridx = _iota((C, W), 0)
    o_scrs = (of_scr, ob_scr)
    f_refs = (ff_ref, fb_ref)

    def lower_bound(d, cols):
        rows = [lb_ref[d, l:l + 1, cols] for l in range(depth)]
        m = rows[0]
        for r in rows[1:]:
            m = jnp.maximum(m, r)
        es = [jnp.exp(r - m) for r in rows]
        tot = es[0]
        for e in es[1:]:
            tot = tot + e
        acc = es[0] / tot
        for e in es[1:layer + 1]:
            acc = acc + e / tot
        return acc - es[0] / tot

    lbs = [[lower_bound(d, slice(hh * W, (hh + 1) * W)) for hh in range(nh)] for d in (0, 1)]
    for d in (0, 1):
        for hh in range(nh):
            st_scr[d, hh] = s0_ref[d, hh].T

    def step(c, d, hh):
        rev = d == 1
        rows = pl.ds(pl.multiple_of(c * C, C), C)
        cols = slice(hh * W, (hh + 1) * W)
        lb = lbs[d][hh]
        q = _silu(q_ref[rows, cols])
        uf = f_refs[d][rows, cols]
        v = v_ref[rows, cols]
        f = lb + (1.0 - lb) * _sigmoid(uf)
        g = jnp.log(jnp.maximum(f, LOG_FLOOR))
        kin = (1.0 - lb) * _sigmoid(-uf)
        b = _cumsum_rows(g, rev)
        st = st_scr[d, hh]
        o = _bdot_nt(q * jnp.exp(b), st)
        intra = jnp.zeros((C, W), F32)
        for t in range(C):
            mask = (ridx >= t) if rev else (ridx <= t)
            pair = _masked_exp(mask, b[t:t + 1, :] - b)
            a = jnp.sum(pair * (q[t:t + 1, :] * kin), axis=1, keepdims=True)
            row = jnp.sum(a * v, axis=0, keepdims=True)
            intra = jnp.where(ridx == t, row, intra)
        b_end = b[0:1, :] if rev else b[C - 1:C, :]
        st_scr[d, hh] = st * jnp.exp(b_end) + _bdot_tn(v, kin * jnp.exp(b_end - b))
        o_scrs[d][rows, cols] = o + intra

    def body(ci, carry):
        for d in (0, 1):
            for hh in range(nh):
                step((nc - 1 - ci) if d == 1 else ci, d, hh)
        return carry

    lax.fori_loop(0, nc, body, 0)
    for d in (0, 1):
        for hh in range(nh):
            sf_ref[d, hh] = st_scr[d, hh].T

    tr = min(L, 256)

    def fin(i, carry):
        rows = pl.ds(pl.multiple_of(i * tr, tr), tr)
        for hh in range(nh):
            cols = slice(hh * W, (hh + 1) * W)
            o = of_scr[rows, cols] + ob_scr[rows, cols]
            on = o * lax.rsqrt(jnp.mean(o * o, axis=-1, keepdims=True) + EPS) * nrm_ref[...]
            y_ref[rows, cols] = on * _silu(g_ref[rows, cols])
        return carry

    lax.fori_loop(0, L // tr, fin, 0)


def _hgrn_scan(proj3, hg_lb, hg_norm, s0, *, layer):
    B, L = proj3.shape[0], proj3.shape[1]
    depth = hg_lb.shape[1]
    nh = 2
    W = nh * HG_DK
    col = lambda base: (lambda b, h: (b, 0, base // W + h))
    st_spec = pl.BlockSpec((None, 2, nh, HG_DK, HG_DK), lambda b, h: (b, 0, h, 0, 0))
    return pl.pallas_call(
        functools.partial(_hgrn_kernel, L=L, layer=layer, depth=depth),
        grid=(B, HG_HEADS // nh),
        in_specs=[pl.BlockSpec((None, L, W), col(P_HGQ)),
                  pl.BlockSpec((None, L, W), col(P_HGFF)),
                  pl.BlockSpec((None, L, W), col(P_HGFB)),
                  pl.BlockSpec((None, L, W), col(P_HGV)),
                  pl.BlockSpec((None, L, W), col(P_HGG)),
                  pl.BlockSpec((2, depth, W), lambda b, h: (0, 0, h)),
                  pl.BlockSpec((1, HG_DK), lambda b, h: (0, 0)),
                  st_spec],
        out_specs=[pl.BlockSpec((None, L, W), lambda b, h: (b, 0, h)), st_spec],
        out_shape=[jax.ShapeDtypeStruct((B, L, D_BRANCH), F32),
                   jax.ShapeDtypeStruct(s0.shape, F32)],
        scratch_shapes=[pltpu.VMEM((L, W), F32), pltpu.VMEM((L, W), F32),
                        pltpu.VMEM((2, nh, HG_DK, HG_DK), F32)],
        compiler_params=_cparams(("arbitrary", "arbitrary")),
        name="hgrn2_scan",
    )(proj3, proj3, proj3, proj3, proj3, hg_lb, hg_norm.reshape(1, HG_DK), s0)


def _ssd_kernel(x_ref, bm_ref, cm_ref, z_ref, sm_ref, bias_ref, alog_ref, dskip_ref, s0_ref,
                y_ref, sf_ref, of_scr, ob_scr, st_scr, *, L):
    C = SSD_CHUNK
    W = LANES
    nc = L // C
    npairs = x_ref.shape[1] // W
    grp = pl.program_id(1)
    lane_lo = _iota((C, W), 1) < SSD_P
    row_lo = _iota((W, SSD_N), 0) < SSD_P
    pick2 = lambda a0, a1: jnp.where(lane_lo, a0, a1)
    o_scrs = (of_scr, ob_scr)
    incls = (_tri_mask(C, False), _tri_mask(C, True))
    for d in (0, 1):
        for pp in range(npairs):
            st_scr[d, pp] = s0_ref[d, pp]

    def step(c, d):
        rev = d == 1
        incl = incls[d]
        rows = pl.ds(pl.multiple_of(c * C, C), C)
        bm = _silu(bm_ref[rows, :])
        cm = _silu(cm_ref[rows, :])
        dt_all = _softplus(sm_ref[rows, :] + bias_ref[...])
        cum_all = _cumsum_rows(-jnp.exp(alog_ref[...]) * dt_all, rev)
        gram = _bdot_nt(cm, bm)
        for pp in range(npairs):
            cols = slice(pp * W, (pp + 1) * W)
            j0 = S_DT + d * SSD_HEADS + 2 * (npairs * grp + pp)
            xs = _silu(x_ref[rows, cols])
            st = st_scr[d, pp]
            dts, cums, ends, ys = [], [], [], []
            for hh in (0, 1):
                dtc = _lane_pick(dt_all, j0 + hh)
                cumc = _lane_pick(cum_all, j0 + hh)
                cumr = _row_pick(cum_all, j0 + hh)
                dts.append(dtc)
                cums.append(cumc)
                ends.append(cumc[0:1, :] if rev else cumc[C - 1:C, :])
                scores = gram * _masked_exp(incl, cumc - cumr)
                head_x = jnp.where(lane_lo if hh == 0 else ~lane_lo, xs, 0.0) * dtc
                ys.append(_bdot(scores, head_x) + _bdot_nt(cm * jnp.exp(cumc), st))
            xdt_e = xs * pick2(dts[0] * jnp.exp(ends[0] - cums[0]), dts[1] * jnp.exp(ends[1] - cums[1]))
            st_scr[d, pp] = (st * jnp.where(row_lo, jnp.exp(ends[0]), jnp.exp(ends[1]))
                             + _bdot_tn(xdt_e, bm))
            o_scrs[d][rows, cols] = pick2(ys[0], ys[1])

    def body(ci, carry):
        step(ci, 0)
        step(nc - 1 - ci, 1)
        return carry

    lax.fori_loop(0, nc, body, 0)
    for d in (0, 1):
        for pp in range(npairs):
            sf_ref[d, pp] = st_scr[d, pp]

    tr = min(L, 256)

    def fin(i, carry):
        rows = pl.ds(pl.multiple_of(i * tr, tr), tr)
        y = of_scr[rows, :] + ob_scr[rows, :] + _silu(x_ref[rows, :]) * dskip_ref[...]
        y_ref[rows, :] = y * _silu(z_ref[rows, :])
        return carry

    lax.fori_loop(0, L // tr, fin, 0)


def _ssd_scan(proj3, bias_row, alog_row, ssd_d, s0):
    B, L = proj3.shape[0], proj3.shape[1]
    n_groups = 2
    n_pairs = SSD_HEADS // 2
    ppg = n_pairs // n_groups
    W = ppg * LANES
    s0p = s0.reshape(B, 2, n_pairs, 2 * SSD_P, SSD_N)
    dskip = jnp.repeat(ssd_d, SSD_P).reshape(n_groups, 1, W)
    col = lambda base: (lambda b, g: (b, 0, base // W + g))
    grp = lambda base: (lambda b, g: (b, 0, base // LANES + g))
    st_spec = pl.BlockSpec((None, 2, ppg, LANES, SSD_N), lambda b, g: (b, 0, g, 0, 0))
    y, sf = pl.pallas_call(
        functools.partial(_ssd_kernel, L=L),
        grid=(B, n_groups),
        in_specs=[pl.BlockSpec((None, L, W), col(C_SSDX)),
                  pl.BlockSpec((None, L, LANES), grp(C_SSDB)),
                  pl.BlockSpec((None, L, LANES), grp(C_SSDC)),
                  pl.BlockSpec((None, L, W), col(P_SSDZ)),
                  pl.BlockSpec((None, L, LANES), lambda b, g: (b, 0, P_SMALL // LANES)),
                  pl.BlockSpec((1, LANES), lambda b, g: (0, 0)),
                  pl.BlockSpec((1, LANES), lambda b, g: (0, 0)),
                  pl.BlockSpec((None, 1, W), lambda b, g: (g, 0, 0)),
                  st_spec],
        out_specs=[pl.BlockSpec((None, L, W), lambda b, g: (b, 0, g)), st_spec],
        out_shape=[jax.ShapeDtypeStruct((B, L, D_BRANCH), F32),
                   jax.ShapeDtypeStruct(s0p.shape, F32)],
        scratch_shapes=[pltpu.VMEM((L, W), F32), pltpu.VMEM((L, W), F32),
                        pltpu.VMEM((2, ppg, LANES, SSD_N), F32)],
        compiler_params=_cparams(("arbitrary", "arbitrary")),
        name="ssd_scan",
    )(proj3, proj3, proj3, proj3, proj3, bias_row, alog_row, dskip, s0p)
    return y, sf.reshape(s0.shape)


def _unit_tri_inverse(a):
    n = a.shape[0]
    eye = (_iota((n, n), 0) == _iota((n, n), 1)).astype(F32)
    m = -a
    p = eye + m
    sh = 2
    while sh < n:
        m = _dot3(m, m)
        p = p + _dot3(p, m)
        sh *= 2
    return p


def _l2norm(a):
    return a * lax.rsqrt(jnp.sum(a * a, axis=-1, keepdims=True) + EPS)


def _gdn_kernel(q_ref, k_ref, v_ref, g_ref, sm_ref, bias_ref, alog_ref, nrm_ref, s0_ref,
                y_ref, sf_ref, of_scr, ob_scr, st_scr, *, L):
    C = GDN_CHUNK
    W = GDN_DK
    nc = L // C
    nh = q_ref.shape[1] // W
    hblk = pl.program_id(1)
    o_scrs = (of_scr, ob_scr)
    incls = (_tri_mask(C, False), _tri_mask(C, True))
    stricts = (_tri_mask(C, False, strict=True), _tri_mask(C, True, strict=True))
    for d in (0, 1):
        for hh in range(nh):
            st_scr[d, hh] = s0_ref[d, hh]

    def step(c, d):
        rev = d == 1
        incl, strict = incls[d], stricts[d]
        rows = pl.ds(pl.multiple_of(c * C, C), C)
        raw = sm_ref[rows, :]
        cum_all = _cumsum_rows(-jnp.exp(alog_ref[...]) * _softplus(raw + bias_ref[...]), rev)
        beta_all = _sigmoid(raw)
        for hh in range(nh):
            cols = slice(hh * W, (hh + 1) * W)
            h = nh * hblk + hh
            jg = S_GA + d * GDN_HEADS + h
            jb = S_GB + d * GDN_HEADS + h
            q = _l2norm(_silu(q_ref[rows, cols])) * (GDN_DK ** -0.5)
            k = _l2norm(_silu(k_ref[rows, cols]))
            v = _silu(v_ref[rows, cols])
            gc = _lane_pick(cum_all, jg)
            gr = _row_pick(cum_all, jg)
            beta = _lane_pick(beta_all, jb)
            decay = _masked_exp(incl, gc - gr)
            kb = k * beta
            a = jnp.where(strict, _bdot_nt(kb, k) * decay, 0.0)
            tinv = _unit_tri_inverse(a)
            u = _dot3(tinv, v * beta)
            w = _dot3(tinv, kb * jnp.exp(gc))
            aqk = _bdot_nt(q, k) * decay
            st = st_scr[d, hh]
            v_new = u - _bdot(w, st)
            o = _bdot(q * jnp.exp(gc), st) + _bdot(aqk, v_new)
            g_end = gc[0:1, :] if rev else gc[C - 1:C, :]
            st_scr[d, hh] = st * jnp.exp(g_end) + _bdot_tn(k * jnp.exp(g_end - gc), v_new)
            o_scrs[d][rows, cols] = o

    def body(ci, carry):
        step(ci, 0)
        step(nc - 1 - ci, 1)
        return carry

    lax.fori_loop(0, nc, body, 0)
    for d in (0, 1):
        for hh in range(nh):
            sf_ref[d, hh] = st_scr[d, hh]

    tr = min(L, 256)

    def fin(i, carry):
        rows = pl.ds(pl.multiple_of(i * tr, tr), tr)
        for hh in range(nh):
            cols = slice(hh * W, (hh + 1) * W)
            o = of_scr[rows, cols] + ob_scr[rows, cols]
            on = o * lax.rsqrt(jnp.mean(o * o, axis=-1, keepdims=True) + EPS) * nrm_ref[...]
            y_ref[rows, cols] = on * _silu(g_ref[rows, cols])
        return carry

    lax.fori_loop(0, L // tr, fin, 0)


def _gdn_scan(proj3, bias_row, alog_row, gdn_norm, s0):
    B, L = proj3.shape[0], proj3.shape[1]
    nh = 2
    W = nh * GDN_DK
    col = lambda base: (lambda b, h: (b, 0, base // W + h))
    st_spec = pl.BlockSpec((None, 2, nh, GDN_DK, GDN_DK), lambda b, h: (b, 0, h, 0, 0))
    return pl.pallas_call(
        functools.partial(_gdn_kernel, L=L),
        grid=(B, GDN_HEADS // nh),
        in_specs=[pl.BlockSpec((None, L, W), col(C_GQ)),
                  pl.BlockSpec((None, L, W), col(C_GK)),
                  pl.BlockSpec((None, L, W), col(C_GV)),
                  pl.BlockSpec((None, L, W), col(P_GDNG)),
                  pl.BlockSpec((None, L, LANES), lambda b, h: (b, 0, P_SMALL // LANES)),
                  pl.BlockSpec((1, LANES), lambda b, h: (0, 0)),
                  pl.BlockSpec((1, LANES), lambda b, h: (0, 0)),
                  pl.BlockSpec((1, GDN_DK), lambda b, h: (0, 0)),
                  st_spec],
        out_specs=[pl.BlockSpec((None, L, W), lambda b, h: (b, 0, h)), st_spec],
        out_shape=[jax.ShapeDtypeStruct((B, L, D_BRANCH), F32),
                   jax.ShapeDtypeStruct(s0.shape, F32)],
        scratch_shapes=[pltpu.VMEM((L, W), F32), pltpu.VMEM((L, W), F32),
                        pltpu.VMEM((2, nh, GDN_DK, GDN_DK), F32)],
        compiler_params=_cparams(("arbitrary", "arbitrary")),
        name="gdn_scan",
    )(proj3, proj3, proj3, proj3, proj3, bias_row, alog_row, gdn_norm.reshape(1, GDN_DK), s0)


def _merge_kernel(ya_ref, yb_ref, yc_ref, yd_ref, mg_ref, x_ref, mod_ref, sn_ref, n2_ref, wb_ref, wo_ref,
                  xo_ref, h_ref):
    yc = yc_ref[...]
    half = D_BRANCH // 2
    parts = []
    for gidx in range(2):
        seg = yc[:, gidx * half:(gidx + 1) * half]
        parts.append(seg * lax.rsqrt(jnp.mean(seg * seg, axis=-1, keepdims=True) + EPS)
                     * sn_ref[:, gidx * half:(gidx + 1) * half])
    branches = (ya_ref[...], yb_ref[...], None, yd_ref[...])
    mixed = None
    for n in range(N_BRANCH):
        gate = _sigmoid(mg_ref[:, n * D_MODEL:(n + 1) * D_MODEL])
        if n == 2:
            lifted = (_dot(parts[0].astype(BF16), wb_ref[n, 0:half, :])
                      + _dot(parts[1].astype(BF16), wb_ref[n, half:D_BRANCH, :]))
        else:
            lifted = _dot(branches[n].astype(BF16), wb_ref[n])
        mixed = gate * lifted if mixed is None else mixed + gate * lifted
    m = mod_ref[...]
    x = x_ref[...] + m[2:3] * _dot(mixed.astype(BF16), wo_ref[...])
    xo_ref[...] = x
    xn = x * lax.rsqrt(jnp.mean(x * x, axis=-1, keepdims=True) + EPS) * n2_ref[...]
    h_ref[...] = xn * (1.0 + m[4:5]) + m[3:4]


def _merge(ya, yb, yc, yd, proj, x, mod, ssd_norm, norm2_g, wb_bf16, wo_bf16, *, tokens_per_mod):
    T = x.shape[0]
    tm = 256
    tiles_per_mod = tokens_per_mod // tm
    yspec = pl.BlockSpec((tm, D_BRANCH), lambda i: (i, 0))
    xspec = pl.BlockSpec((tm, D_MODEL), lambda i: (i, 0))
    return pl.pallas_call(
        _merge_kernel,
        grid=(T // tm,),
        in_specs=[yspec, yspec, yspec, yspec,
                  pl.BlockSpec((tm, N_BRANCH * D_MODEL), lambda i: (i, P_MERGE // (N_BRANCH * D_MODEL))),
                  xspec,
                  pl.BlockSpec((None, 6, D_MODEL), lambda i: (i // tiles_per_mod, 0, 0)),
                  pl.BlockSpec((1, D_BRANCH), lambda i: (0, 0)),
                  pl.BlockSpec((1, D_MODEL), lambda i: (0, 0)),
                  pl.BlockSpec((N_BRANCH, D_BRANCH, D_MODEL), lambda i: (0, 0, 0)),
                  pl.BlockSpec((D_MODEL, D_MODEL), lambda i: (0, 0))],
        out_specs=[xspec, xspec],
        out_shape=[jax.ShapeDtypeStruct((T, D_MODEL), F32)] * 2,
        compiler_params=_cparams(("arbitrary",)),
        name="merge_outproj",
    )(ya, yb, yc, yd, proj, x, mod, ssd_norm.reshape(1, D_BRANCH), norm2_g.reshape(1, D_MODEL), wb_bf16, wo_bf16)


def _router_kernel(h_ref, rw_ref, rb_ref, idx_ref, wt_ref, rank_ref, cnt_ref, carry):
    i = pl.program_id(0)
    tm = h_ref.shape[0]
    E = N_EXPERTS
    gsz = E // N_EXPERT_GROUPS

    @pl.when(i == 0)
    def _():
        carry[...] = jnp.zeros_like(carry)

    scores = _sigmoid(_dot(h_ref[...], rw_ref[...], precision=HIGHEST))
    biased = scores + rb_ref[...]
    lane = _iota((tm, E), 1)
    grp = jnp.right_shift(lane, int(math.log2(gsz)))
    neg_inf = -jnp.inf

    gs = []
    for g in range(N_EXPERT_GROUPS):
        vals = jnp.where(grp == g, biased, neg_inf)
        m1 = jnp.max(vals, axis=1, keepdims=True)
        i1 = jnp.min(jnp.where(vals == m1, lane, E), axis=1, keepdims=True)
        m2 = jnp.max(jnp.where(lane == i1, neg_inf, vals), axis=1, keepdims=True)
        gs.append(m1 + m2)
    exp_mask = jnp.zeros((tm, E), jnp.bool_)
    for g in range(N_EXPERT_GROUPS):
        rank = jnp.zeros((tm, 1), I32)
        for g2 in range(N_EXPERT_GROUPS):
            if g2 == g:
                continue
            ahead = (gs[g2] >= gs[g]) if g2 < g else (gs[g2] > gs[g])
            rank = rank + ahead.astype(I32)
        exp_mask = exp_mask | ((grp == g) & (rank < TOPK_GROUPS))
    cur = jnp.where(exp_mask, biased, MASK_NEG)

    olane = _iota((tm, LANES), 1)
    idx_out = jnp.zeros((tm, LANES), I32)
    wt_out = jnp.zeros((tm, LANES), F32)
    onehot = jnp.zeros((tm, E), F32)
    picks = []
    wsum = jnp.zeros((tm, 1), F32)
    for kk in range(TOP_K):
        m = jnp.max(cur, axis=1, keepdims=True)
        ik = jnp.min(jnp.where(cur == m, lane, E), axis=1, keepdims=True)
        hit = lane == ik
        wk = jnp.sum(jnp.where(hit, scores, 0.0), axis=1, keepdims=True)
        cur = jnp.where(hit, neg_inf, cur)
        onehot = onehot + hit.astype(F32)
        picks.append((ik, wk))
        wsum = wsum + wk
        idx_out = jnp.where(olane == kk, ik, idx_out)
    for kk, (ik, wk) in enumerate(picks):
        wt_out = jnp.where(olane == kk, wk / wsum * ROUTED_SCALE, wt_out)

    strict_lower = (_iota((tm, tm), 1) < _iota((tm, tm), 0)).astype(BF16)
    before = _dot(strict_lower, onehot.astype(BF16)) + carry[...]
    rank_out = jnp.zeros((tm, LANES), I32)
    for kk, (ik, wk) in enumerate(picks):
        rk = jnp.sum(jnp.where(lane == ik, before, 0.0), axis=1, keepdims=True)
        rank_out = jnp.where(olane == kk, rk.astype(I32), rank_out)
    carry[...] = carry[...] + jnp.sum(onehot, axis=0, keepdims=True)

    idx_ref[...] = idx_out
    wt_ref[...] = wt_out
    rank_ref[...] = rank_out
    cnt_ref[...] = carry[...]


def _router(h2, router_w, router_b):
    T = h2.shape[0]
    tm = 256
    ospec = pl.BlockSpec((tm, LANES), lambda i: (i, 0))
    return pl.pallas_call(
        _router_kernel,
        grid=(T // tm,),
        in_specs=[pl.BlockSpec((tm, D_MODEL), lambda i: (i, 0)),
                  pl.BlockSpec((D_MODEL, N_EXPERTS), lambda i: (0, 0)),
                  pl.BlockSpec((1, N_EXPERTS), lambda i: (0, 0))],
        out_specs=[ospec, ospec, ospec, pl.BlockSpec((1, N_EXPERTS), lambda i: (0, 0))],
        out_shape=[jax.ShapeDtypeStruct((T, LANES), I32), jax.ShapeDtypeStruct((T, LANES), F32),
                   jax.ShapeDtypeStruct((T, LANES), I32), jax.ShapeDtypeStruct((1, N_EXPERTS), F32)],
        scratch_shapes=[pltpu.VMEM((1, N_EXPERTS), F32)],
        compiler_params=_cparams(("arbitrary",)),
        name="moe_router",
    )(h2, router_w, router_b.reshape(1, N_EXPERTS))


def _dispatch_kernel(dest_ref, h_ref, xs_ref, sem):
    tm = h_ref.shape[0]

    def row_copies(t):
        src = h_ref.at[pl.ds(t, 1), :]
        return [pltpu.make_async_copy(src, xs_ref.at[pl.ds(dest_ref[t * TOP_K + kk], 1), :], sem)
                for kk in range(TOP_K)]

    def start(t, carry):
        for cp in row_copies(t):
            cp.start()
        return carry

    def wait(t, carry):
        for cp in row_copies(t):
            cp.wait()
        return carry

    lax.fori_loop(0, tm, start, 0)
    lax.fori_loop(0, tm, wait, 0)


def _dispatch(h2, dest_flat):
    T = h2.shape[0]
    tm = 256
    return pl.pallas_call(
        _dispatch_kernel,
        grid=(T // tm,),
        in_specs=[pl.BlockSpec((tm * TOP_K,), lambda i: (i,), memory_space=pltpu.SMEM),
                  pl.BlockSpec((tm, D_MODEL), lambda i: (i, 0))],
        out_specs=pl.BlockSpec(memory_space=pl.ANY),
        out_shape=jax.ShapeDtypeStruct((T * TOP_K, D_MODEL), F32),
        scratch_shapes=[pltpu.SemaphoreType.DMA(())],
        compiler_params=_cparams(("arbitrary",)),
        name="moe_dispatch",
    )(dest_flat, h2)


def _expert_kernel(blk_ref, exp_ref, lo_ref, hi_ref, first_ref, x_ref, w1_ref, w3_ref, w2_ref, o_ref):
    w = pl.program_id(0)
    x = x_ref[...].astype(BF16)
    a = _dot(x, w1_ref[...].astype(BF16))
    b = _dot(x, w3_ref[...].astype(BF16))
    y = _dot((_silu(a) * b).astype(BF16), w2_ref[...].astype(BF16))
    r = _iota(y.shape, 0)
    y = jnp.where((r >= lo_ref[w]) & (r < hi_ref[w]), y, 0.0)

    @pl.when(first_ref[w] == 1)
    def _():
        o_ref[...] = y

    @pl.when(first_ref[w] == 0)
    def _():
        o_ref[...] = o_ref[...] + y


def _experts(xs, work, w1, w3, w2, *, bm):
    A = xs.shape[0]
    n_work = work[0].shape[0]
    xmap = lambda w, blk, ex, lo, hi, first: (blk[w], 0)
    wmap = lambda w, blk, ex, lo, hi, first: (ex[w], 0, 0)
    return pl.pallas_call(
        _expert_kernel,
        grid_spec=pltpu.PrefetchScalarGridSpec(
            num_scalar_prefetch=5,
            grid=(n_work,),
            in_specs=[pl.BlockSpec((bm, D_MODEL), xmap),
                      pl.BlockSpec((None, D_MODEL, D_EXPERT), wmap),
                      pl.BlockSpec((None, D_MODEL, D_EXPERT), wmap),
                      pl.BlockSpec((None, D_EXPERT, D_MODEL), wmap)],
            out_specs=pl.BlockSpec((bm, D_MODEL), xmap)),
        out_shape=jax.ShapeDtypeStruct((A, D_MODEL), F32),
        compiler_params=_cparams(("arbitrary",)),
        name="moe_experts",
    )(*work, xs, w1, w3, w2)


def _combine_kernel(dest_ref, ys_ref, wt_ref, h_ref, x_ref, mod_ref, s1_ref, s3_ref, s2_ref, fg_ref,
                    o_ref, buf, sem, *, final_norm):
    tc = h_ref.shape[0]

    def row_copies(t):
        return [pltpu.make_async_copy(ys_ref.at[pl.ds(dest_ref[t * TOP_K + kk], 1), :],
                                      buf.at[kk, pl.ds(t, 1), :], sem)
                for kk in range(TOP_K)]

    def start(t, carry):
        for cp in row_copies(t):
            cp.start()
        return carry

    def wait(t, carry):
        for cp in row_copies(t):
            cp.wait()
        return carry

    lax.fori_loop(0, tc, start, 0)
    hb = h_ref[...].astype(BF16)
    mid = _silu(_dot(hb, s1_ref[...])) * _dot(hb, s3_ref[...])
    y = _dot(mid.astype(BF16), s2_ref[...])
    lax.fori_loop(0, tc, wait, 0)
    for kk in range(TOP_K):
        y = y + wt_ref[:, kk:kk + 1] * buf[kk]
    x = x_ref[...] + mod_ref[5:6, :] * y
    if final_norm:
        x = x * lax.rsqrt(jnp.mean(x * x, axis=-1, keepdims=True) + EPS) * fg_ref[...]
    o_ref[...] = x


def _combine(ys, dest_flat, wt, h2, x, mod, s1, s3, s2, final_g, *, tokens_per_mod, final_norm):
    T = x.shape[0]
    tc = 128
    tiles_per_mod = tokens_per_mod // tc
    xspec = pl.BlockSpec((tc, D_MODEL), lambda i: (i, 0))
    return pl.pallas_call(
        functools.partial(_combine_kernel, final_norm=final_norm),
        grid=(T // tc,),
        in_specs=[pl.BlockSpec((tc * TOP_K,), lambda i: (i,), memory_space=pltpu.SMEM),
                  pl.BlockSpec(memory_space=pl.ANY),
                  pl.BlockSpec((tc, LANES), lambda i: (i, 0)),
                  xspec, xspec,
                  pl.BlockSpec((None, 6, D_MODEL), lambda i: (i // tiles_per_mod, 0, 0)),
                  pl.BlockSpec((D_MODEL, D_SHARED), lambda i: (0, 0)),
                  pl.BlockSpec((D_MODEL, D_SHARED), lambda i: (0, 0)),
                  pl.BlockSpec((D_SHARED, D_MODEL), lambda i: (0, 0)),
                  pl.BlockSpec((1, D_MODEL), lambda i: (0, 0))],
        out_specs=xspec,
        out_shape=jax.ShapeDtypeStruct((T, D_MODEL), F32),
        scratch_shapes=[pltpu.VMEM((TOP_K, tc, D_MODEL), F32), pltpu.SemaphoreType.DMA(())],
        compiler_params=_cparams(("arbitrary",)),
        name="moe_combine",
    )(dest_flat, ys, wt, h2, x, mod, s1, s3, s2, final_g.reshape(1, D_MODEL))


def _work_list(counts, n_rows, bm):
    E = N_EXPERTS
    n_blocks = n_rows // bm
    n_work = n_blocks + E - 1
    ends = jnp.cumsum(counts)
    starts = ends - counts
    first_blk = starts // bm
    last_blk = jnp.where(counts > 0, (ends - 1) // bm, first_blk)
    n_items = jnp.where(counts > 0, last_blk - first_blk + 1, 0)
    item_end = jnp.cumsum(n_items)
    item_start = item_end - n_items
    w = jnp.arange(n_work, dtype=I32)
    ex = jnp.minimum(jnp.searchsorted(item_end, w, side="right"), E - 1).astype(I32)
    valid = w < item_end[-1]
    blk = jnp.where(valid, first_blk[ex] + (w - item_start[ex]), n_blocks - 1).astype(I32)
    lo = jnp.clip(starts[ex] - blk * bm, 0, bm)
    hi = jnp.clip(ends[ex] - blk * bm, 0, bm)
    lo = jnp.where(valid, lo, 0).astype(I32)
    hi = jnp.where(valid, hi, 0).astype(I32)
    ex = jnp.where(valid, ex, ex[jnp.maximum(item_end[-1] - 1, 0)])
    prev_blk = jnp.concatenate([jnp.full((1,), -1, I32), blk[:-1]])
    first = (blk != prev_blk).astype(I32)
    return blk, ex.astype(I32), lo, hi, first


def _moe(h2, x, mod, p, final_g, *, tokens_per_mod, final_norm):
    T = x.shape[0]
    idx, wt, rank, counts = _router(h2, p["router_w"], p["router_b"])
    counts = counts.reshape(N_EXPERTS).astype(I32)
    starts = jnp.cumsum(counts) - counts
    dest = (starts[idx[:, :TOP_K]] + rank[:, :TOP_K]).reshape(T * TOP_K)
    xs = _dispatch(h2, dest)
    bm = 512
    work = _work_list(counts, T * TOP_K, bm)
    ys = _experts(xs, work, p["exp_w1"], p["exp_w3"], p["exp_w2"], bm=bm)
    return _combine(ys, dest, wt, h2, x, mod, p["sh_w1"], p["sh_w3"], p["sh_w2"], final_g,
                    tokens_per_mod=tokens_per_mod, final_norm=final_norm)


def _reorder_w_in(w_in):
    plain = D_CONV
    pieces = [w_in[..., :plain],
              w_in[..., plain + 3616:plain + 3616 + 4096],
              w_in[..., plain:plain + 3072],
              w_in[..., plain + 3088:plain + 3600],
              w_in[..., plain + 3072:plain + 3088],
              w_in[..., plain + 3600:plain + 3616]]
    out = jnp.concatenate(pieces, axis=-1)
    pad = D_PROJ - out.shape[-1]
    return jnp.pad(out, ((0, 0), (0, 0), (0, pad))).astype(BF16)


def _small_rows(ssd_vals, gdn_vals):
    row = jnp.zeros((LANES,), F32)
    row = row.at[S_DT:S_DT + 2 * SSD_HEADS].set(ssd_vals.reshape(-1))
    row = row.at[S_GA:S_GA + 2 * GDN_HEADS].set(gdn_vals.reshape(-1))
    return row.reshape(1, LANES)


def _layer_pass(x, mod, lp, hy, st_hg, st_ssd, st_gdn, final_g, *, B, L, tokens_per_mod, seg, layer, final_norm):
    T = B * L
    proj = _in_projection(x, mod, lp["norm1_g"], lp["w_in"], lp["conv_w"], lp["conv_b"],
                          tokens_per_mod=tokens_per_mod, seg=seg)
    proj3 = proj.reshape(B, L, D_PROJ)
    fmat, gmat, hspec = hy
    cb = lambda c: c // D_BRANCH
    z = _spectral_conv(fmat, gmat, hspec, 0, lp["hy_bias"][0], proj3, cb(C_HYV), proj3, cb(C_HYX1))
    ya = _spectral_conv(fmat, gmat, hspec, 1, lp["hy_bias"][1], z, 0, proj3, cb(C_HYX2))
    yb, s_hg = _hgrn_scan(proj3, lp["hg_lb"], lp["hg_norm"], st_hg, layer=layer)
    yc, s_ssd = _ssd_scan(proj3, lp["bias_row"], lp["alog_row"], lp["ssd_d"], st_ssd)
    yd, s_gdn = _gdn_scan(proj3, lp["bias_row"], lp["alog_row"], lp["gdn_norm"], st_gdn)
    flat = lambda a: a.reshape(T, D_BRANCH)
    x, h2 = _merge(flat(ya), flat(yb), flat(yc), flat(yd), proj, x, mod, lp["ssd_norm"], lp["norm2_g"],
                   lp["w_branch"], lp["w_out"], tokens_per_mod=tokens_per_mod)
    x = _moe(h2, x, mod, lp, final_g, tokens_per_mod=tokens_per_mod, final_norm=final_norm)
    return x, s_hg, s_ssd, s_gdn


def kernel(x_prompt, x_sample, state_hgrn, state_ssd, state_gdn, c, c_ctx, norm1_g, norm2_g, ada_w, ada_b, w_in, conv_w, conv_b, hy_w1, hy_b1, hy_w2, hy_b2, hy_w3, hy_bias, hg_lb, hg_norm, ssd_a_log, ssd_dt_bias, ssd_d, ssd_norm, gdn_a_log, gdn_dt_bias, gdn_norm, w_branch, w_out, router_w, router_bias, exp_w1, exp_w3, exp_w2, sh_w1, sh_w3, sh_w2, final_g):
    depth = w_in.shape[0]
    bp, lp_len = x_prompt.shape[0], x_prompt.shape[1]
    bs, ls_len = x_sample.shape[0], x_sample.shape[1]
    D = D_MODEL

    w_in_r = _reorder_w_in(w_in)
    hy_w1p = jnp.pad(hy_w1, ((0, 0), (0, LANES - hy_w1.shape[1]), (0, 0)))
    layers = []
    for l in range(depth):
        layers.append(dict(
            norm1_g=norm1_g[l], norm2_g=norm2_g[l], w_in=w_in_r[l], conv_w=conv_w[l], conv_b=conv_b[l],
            hy_bias=hy_bias[l], hg_lb=hg_lb, hg_norm=hg_norm[l],
            bias_row=_small_rows(ssd_dt_bias[l], gdn_dt_bias[l]),
            alog_row=_small_rows(ssd_a_log[l], gdn_a_log[l]),
            ssd_d=ssd_d[l], ssd_norm=ssd_norm[l], gdn_norm=gdn_norm[l],
            w_branch=w_branch[l].astype(BF16), w_out=w_out[l].astype(BF16),
            router_w=router_w[l], router_b=router_bias[l],
            exp_w1=exp_w1[l], exp_w3=exp_w3[l], exp_w2=exp_w2[l],
            sh_w1=sh_w1[l].astype(BF16), sh_w3=sh_w3[l].astype(BF16), sh_w2=sh_w2[l].astype(BF16)))

    def hyena_setup(L):
        fmat, fs = _dft_matrices(L)
        gmat = fs.T
        specs = []
        for l in range(depth):
            filt = _hyena_filters(L, hy_w1p[l], hy_b1[l], hy_w2[l], hy_b2[l], hy_w3[l])
            specs.append(_filter_spectrum(fmat, filt))
        return fmat, gmat, specs

    cond = jnp.concatenate([c_ctx.reshape(1, D), c], axis=0)
    rows = cond.shape[0]
    rows8 = (rows + 7) // 8 * 8
    cond8 = jnp.pad(cond, ((0, rows8 - rows), (0, 0)))
    mods = [_modulation(cond8, ada_w[l], ada_b[l]).reshape(rows8, 6, D) for l in range(depth)]

    fmat, gmat, specs = hyena_setup(lp_len)
    x = x_prompt.reshape(bp * lp_len, D)
    z_hg = jnp.zeros((bp, 2, HG_HEADS, HG_DK, HG_DK), F32)
    z_ssd = jnp.zeros((bp, 2, SSD_HEADS, SSD_P, SSD_N), F32)
    z_gdn = jnp.zeros((bp, 2, GDN_HEADS, GDN_DK, GDN_DK), F32)
    hg_states, ssd_states, gdn_states = [], [], []
    for l in range(depth):
        x, s_hg, s_ssd, s_gdn = _layer_pass(
            x, mods[l][0:1], layers[l], (fmat, gmat, specs[l]), z_hg, z_ssd, z_gdn, final_g,
            B=bp, L=lp_len, tokens_per_mod=bp * lp_len, seg=lp_len, layer=l, final_norm=(l == depth - 1))
        hg_states.append(s_hg)
        ssd_states.append(s_ssd)
        gdn_states.append(s_gdn)
    y_prompt = x.reshape(bp, lp_len, D)
    new_hg = jnp.stack(hg_states, axis=1)
    new_ssd = jnp.stack(ssd_states, axis=1)
    new_gdn = jnp.stack(gdn_states, axis=1)

    fmat, gmat, specs = hyena_setup(ls_len)
    x = x_sample.reshape(bs * ls_len, D)
    for l in range(depth):
        x, _, _, _ = _layer_pass(
            x, mods[l][1:1 + bs], layers[l], (fmat, gmat, specs[l]),
            state_hgrn[:, l], state_ssd[:, l], state_gdn[:, l], final_g,
            B=bs, L=ls_len, tokens_per_mod=ls_len, seg=GRID_W, layer=l, final_norm=(l == depth - 1))
    y_sample = x.reshape(bs, ls_len, D)
    return (y_prompt, y_sample, new_hg, new_ssd, new_gdn)
```

```python
import functools
import math

import numpy as np
import jax
import jax.numpy as jnp
from jax import lax
from jax.experimental import pallas as pl
from jax.experimental.pallas import tpu as pltpu

F32 = jnp.float32
BF16 = jnp.bfloat16
I32 = jnp.int32
HIGHEST = lax.Precision.HIGHEST

D_MODEL = 1024
GRID_W = 64
EPS = 1e-6
LOG_FLOOR = 1e-30
MASK_NEG = -1e30
N_BRANCH = 4
D_BRANCH = 512
HY_POS_FREQS = 16
HY_FILTER_HIDDEN = 64
HY_FAST_DECAY = 0.3
HY_SLOW_DECAY = 1.5
HY_DECAY_TARGET = 1e-2
HG_HEADS = 4
HG_DK = 128
HG_CHUNK = 16
SSD_HEADS = 8
SSD_P = 64
SSD_N = 128
SSD_CHUNK = 64
GDN_HEADS = 4
GDN_DK = 128
GDN_CHUNK = 64
N_EXPERTS = 64
TOP_K = 8
N_EXPERT_GROUPS = 8
TOPK_GROUPS = 4
D_EXPERT = 256
D_SHARED = 256
ROUTED_SCALE = 2.5

LANES = 128

D_CONV = 4096
C_HYV, C_HYX1, C_HYX2 = 0, 512, 1024
C_SSDX, C_SSDB, C_SSDC = 1536, 2048, 2304
C_GQ, C_GK, C_GV = 2560, 3072, 3584
P_MERGE = 4096
P_HGQ, P_HGFF, P_HGFB, P_HGV, P_HGG = 8192, 8704, 9216, 9728, 10240
P_SSDZ, P_GDNG, P_SMALL = 10752, 11264, 11776
D_PROJ = 12288
S_DT, S_GA, S_GB = 0, 16, 24

VMEM_LIMIT = 56 * 1024 * 1024


def _cparams(sem):
    return pltpu.CompilerParams(dimension_semantics=sem, vmem_limit_bytes=VMEM_LIMIT)


def _sigmoid(x):
    return 1.0 / (1.0 + jnp.exp(-x))


def _silu(x):
    return x * _sigmoid(x)


def _softplus(x):
    return jnp.maximum(x, 0.0) + jnp.log(1.0 + jnp.exp(-jnp.abs(x)))


def _dot(a, b, precision=None):
    return jnp.dot(a, b, preferred_element_type=F32, precision=precision)


def _dot_nt(a, b, precision=None):
    return lax.dot_general(a, b, (((1,), (1,)), ((), ())), preferred_element_type=F32, precision=precision)


def _dot_tn(a, b, precision=None):
    return lax.dot_general(a, b, (((0,), (0,)), ((), ())), preferred_element_type=F32, precision=precision)


def _bdot(a, b):
    return _dot(a.astype(BF16), b.astype(BF16))


def _bdot_nt(a, b):
    return _dot_nt(a.astype(BF16), b.astype(BF16))


def _bdot_tn(a, b):
    return _dot_tn(a.astype(BF16), b.astype(BF16))


def _iota(shape, dim):
    return lax.broadcasted_iota(I32, shape, dim)


def _cumsum_rows(g, reverse):
    n = g.shape[0]
    row = _iota(g.shape, 0)
    sh = 1
    while sh < n:
        if reverse:
            g = g + jnp.where(row < n - sh, pltpu.roll(g, n - sh, 0), 0.0)
        else:
            g = g + jnp.where(row >= sh, pltpu.roll(g, sh, 0), 0.0)
        sh *= 2
    return g


def _lane_pick(a, j):
    return jnp.sum(jnp.where(_iota(a.shape, 1) == j, a, 0.0), axis=1, keepdims=True)


def _split3(a):
    a1 = a.astype(BF16)
    r1 = a - a1.astype(F32)
    a2 = r1.astype(BF16)
    a3 = (r1 - a2.astype(F32)).astype(BF16)
    return a1, a2, a3


def _dot3(a, b):
    a1, a2, _ = _split3(a)
    b1, b2, _ = _split3(b)
    return _dot(a1, b1) + (_dot(a1, b2) + _dot(a2, b1))


def _row_pick(a, j):
    sel = (_iota((8, a.shape[1]), 1) == j).astype(BF16)
    a1, a2, a3 = _split3(a)
    return (_dot_nt(sel, a1) + (_dot_nt(sel, a2) + _dot_nt(sel, a3)))[0:1, :]


def _tri_mask(n, reverse, strict=False):
    t = _iota((n, n), 0)
    s = _iota((n, n), 1)
    if reverse:
        return (s > t) if strict else (s >= t)
    return (s < t) if strict else (s <= t)


def _masked_exp(mask, diff):
    return jnp.where(mask, jnp.exp(jnp.where(mask, diff, 0.0)), 0.0)


def _mod_kernel(c_ref, w_ref, b_ref, o_ref):
    o_ref[...] = _dot(_silu(c_ref[...]), w_ref[...], precision=HIGHEST) + b_ref[...]


def _modulation(cond8, ada_w, ada_b):
    rows = cond8.shape[0]
    tn = 1536
    n = ada_w.shape[1]
    return pl.pallas_call(
        _mod_kernel,
        grid=(n // tn,),
        in_specs=[pl.BlockSpec((rows, D_MODEL), lambda j: (0, 0)),
                  pl.BlockSpec((D_MODEL, tn), lambda j: (0, j)),
                  pl.BlockSpec((1, tn), lambda j: (0, j))],
        out_specs=pl.BlockSpec((rows, tn), lambda j: (0, j)),
        out_shape=jax.ShapeDtypeStruct((rows, n), F32),
        compiler_params=_cparams(("arbitrary",)),
        name="adaln_mod",
    )(cond8, ada_w, ada_b.reshape(1, n))


def _inproj_kernel(x_ref, mod_ref, g_ref, w_ref, cw_ref, cb_ref, o_ref, h_scr, *, seg, n_conv_tiles):
    j = pl.program_id(1)

    @pl.when(j == 0)
    def _():
        x = x_ref[...]
        xn = x * lax.rsqrt(jnp.mean(x * x, axis=-1, keepdims=True) + EPS) * g_ref[...]
        m = mod_ref[...]
        h_scr[...] = (xn * (1.0 + m[1:2]) + m[0:1]).astype(BF16)

    y = _dot(h_scr[...], w_ref[...])

    @pl.when(j < n_conv_tiles)
    def _():
        tm = y.shape[0]
        pos = _iota(y.shape, 0) & (seg - 1)
        prev = jnp.where(pos == 0, 0.0, pltpu.roll(y, 1, 0))
        nxt = jnp.where(pos == seg - 1, 0.0, pltpu.roll(y, tm - 1, 0))
        cw = cw_ref[...]
        o_ref[...] = cb_ref[...] + prev * cw[0:1] + y * cw[1:2] + nxt * cw[2:3]

    @pl.when(j >= n_conv_tiles)
    def _():
        o_ref[...] = y


def _in_projection(x, mod, norm_g, w_bf16, conv_w, conv_b, *, tokens_per_mod, seg):
    T = x.shape[0]
    tm = min(1024, tokens_per_mod)
    tn = 1024
    n_conv_tiles = D_CONV // tn
    tiles_per_mod = tokens_per_mod // tm
    kern = functools.partial(_inproj_kernel, seg=seg, n_conv_tiles=n_conv_tiles)
    cmap = lambda i, j: (0, jnp.minimum(j, n_conv_tiles - 1))
    return pl.pallas_call(
        kern,
        grid=(T // tm, D_PROJ // tn),
        in_specs=[pl.BlockSpec((tm, D_MODEL), lambda i, j: (i, 0)),
                  pl.BlockSpec((None, 6, D_MODEL), lambda i, j: (i // tiles_per_mod, 0, 0)),
                  pl.BlockSpec((1, D_MODEL), lambda i, j: (0, 0)),
                  pl.BlockSpec((D_MODEL, tn), lambda i, j: (0, j)),
                  pl.BlockSpec((3, tn), cmap),
                  pl.BlockSpec((1, tn), cmap)],
        out_specs=pl.BlockSpec((tm, tn), lambda i, j: (i, j)),
        out_shape=jax.ShapeDtypeStruct((T, D_PROJ), F32),
        scratch_shapes=[pltpu.VMEM((tm, D_MODEL), BF16)],
        compiler_params=_cparams(("arbitrary", "arbitrary")),
        name="in_proj",
    )(x, mod, norm_g.reshape(1, D_MODEL), w_bf16, conv_w, conv_b.reshape(1, D_CONV))


def _hyfilt_kernel(w1_ref, b1_ref, w2_ref, b2_ref, w3_ref, o_ref, *, L):
    i = pl.program_id(1)
    tl = o_ref.shape[0]
    t = (_iota((tl, LANES), 0) + i * tl).astype(F32) / L
    lane = _iota((tl, LANES), 1)
    band = jnp.where(lane <= HY_POS_FREQS, lane, lane - HY_POS_FREQS).astype(F32)
    ang = 2.0 * math.pi * t * band
    feats = jnp.where(lane == 0, t,
                      jnp.where(lane <= HY_POS_FREQS, jnp.sin(ang),
                                jnp.where(lane <= 2 * HY_POS_FREQS, jnp.cos(ang), 0.0)))
    hdn = jnp.sin(_dot(feats, w1_ref[...], precision=HIGHEST) + b1_ref[...])
    hdn = jnp.sin(_dot(hdn, w2_ref[...], precision=HIGHEST) + b2_ref[...])
    filt = _dot(hdn, w3_ref[...], precision=HIGHEST)
    max_decay = math.log(HY_DECAY_TARGET) / HY_FAST_DECAY
    min_decay = math.log(HY_DECAY_TARGET) / HY_SLOW_DECAY
    n = filt.shape[1]
    ch = (_iota((tl, n), 1) & (D_BRANCH - 1)).astype(F32)
    delta = min_decay + ch * ((max_decay - min_decay) / (D_BRANCH - 1))
    tt = (_iota((tl, n), 0) + i * tl).astype(F32) / L
    o_ref[...] = filt * jnp.exp(-tt * jnp.abs(delta))


def _hyena_filters(L, w1p, b1, w2, b2, w3):
    tl = min(L, 256)
    n = 2 * D_BRANCH
    return pl.pallas_call(
        functools.partial(_hyfilt_kernel, L=L),
        grid=(2, L // tl),
        in_specs=[pl.BlockSpec((LANES, HY_FILTER_HIDDEN), lambda d, i: (0, 0)),
                  pl.BlockSpec((1, HY_FILTER_HIDDEN), lambda d, i: (0, 0)),
                  pl.BlockSpec((HY_FILTER_HIDDEN, HY_FILTER_HIDDEN), lambda d, i: (0, 0)),
                  pl.BlockSpec((1, HY_FILTER_HIDDEN), lambda d, i: (0, 0)),
                  pl.BlockSpec((HY_FILTER_HIDDEN, n), lambda d, i: (0, d))],
        out_specs=pl.BlockSpec((None, tl, n), lambda d, i: (d, i, 0)),
        out_shape=jax.ShapeDtypeStruct((2, L, n), F32),
        compiler_params=_cparams(("arbitrary", "arbitrary")),
        name="hyena_filters",
    )(w1p, b1.reshape(1, -1), w2, b2.reshape(1, -1), w3)


def _dftgen_kernel(f_ref, fs_ref, *, L, tk):
    i = pl.program_id(0)
    N = 2 * L
    k = _iota((tk, LANES), 0) + i * tk
    lane = _iota((tk, LANES), 1)
    w = 2.0 * math.pi / N
    a0 = ((k * lane) & (N - 1)).astype(F32) * w
    c0, s0 = jnp.cos(a0), jnp.sin(a0)
    a1 = ((k * lane * LANES) & (N - 1)).astype(F32) * w
    c1, s1 = jnp.cos(a1), jnp.sin(a1)
    alt = jnp.where((lane & 1) == 0, 1.0, -1.0)
    coef = jnp.where(k == 0, 1.0 / N, 2.0 / N)
    for n1 in range(L // LANES):
        c1b = c1[:, n1:n1 + 1]
        s1b = s1[:, n1:n1 + 1]
        cosb = c1b * c0 - s1b * s0
        sinb = jnp.where(k == 0, alt, -(s1b * c0 + c1b * s0))
        cols = slice(n1 * LANES, (n1 + 1) * LANES)
        f_ref[0:tk, cols] = cosb.astype(BF16)
        f_ref[tk:2 * tk, cols] = sinb.astype(BF16)
        fs_ref[0:tk, cols] = (coef * cosb).astype(BF16)
        fs_ref[tk:2 * tk, cols] = (coef * sinb).astype(BF16)


def _dft_matrices(L):
    tk = min(L, 256)
    spec = pl.BlockSpec((2 * tk, L), lambda i: (i, 0))
    return pl.pallas_call(
        functools.partial(_dftgen_kernel, L=L, tk=tk),
        grid=(L // tk,),
        in_specs=[],
        out_specs=[spec, spec],
        out_shape=[jax.ShapeDtypeStruct((2 * L, L), BF16)] * 2,
        compiler_params=_cparams(("arbitrary",)),
        name="dft_matrices",
    )()


def _dfth_kernel(f_ref, h_ref, o_ref, hp_scr, *, tk):
    i = pl.program_id(1)
    n = o_ref.shape[1]

    @pl.when(i == 0)
    def _():
        hc = h_ref[0]
        ha = jnp.where(_iota(hc.shape, 0) == 0, 0.0, h_ref[1])
        hp_scr[:, 0:n] = (hc + ha).astype(BF16)
        hp_scr[:, n:2 * n] = (hc - ha).astype(BF16)

    u = _dot(f_ref[...], hp_scr[...])
    r = _iota((2 * tk, n), 0)
    from_sum = (r < tk) | ((r == tk) & (i == 0))
    o_ref[...] = jnp.where(from_sum, u[:, 0:n], u[:, n:2 * n])


def _filter_spectrum(fmat, filt):
    L = fmat.shape[1]
    tk = min(L, 256)
    C = filt.shape[2]
    tc = 256
    return pl.pallas_call(
        functools.partial(_dfth_kernel, tk=tk),
        grid=(C // tc, L // tk),
        in_specs=[pl.BlockSpec((2 * tk, L), lambda c, i: (i, 0)),
                  pl.BlockSpec((2, L, tc), lambda c, i: (0, 0, c))],
        out_specs=pl.BlockSpec((2 * tk, tc), lambda c, i: (i, c)),
        out_shape=jax.ShapeDtypeStruct((2 * L, C), F32),
        scratch_shapes=[pltpu.VMEM((L, 2 * tc), BF16)],
        compiler_params=_cparams(("arbitrary", "arbitrary")),
        name="filter_spectrum",
    )(fmat, filt)


def _dfta_kernel(f_ref, u_ref, h_ref, y_ref, u_scr, *, tk):
    i = pl.program_id(1)

    @pl.when(i == 0)
    def _():
        u_scr[...] = u_ref[...].astype(BF16)

    uf = _dot(f_ref[...], u_scr[...])
    ur, ui = uf[0:tk], uf[tk:2 * tk]
    hr, hi = h_ref[0:tk, :], h_ref[tk:2 * tk, :]
    dc = (_iota(ur.shape, 0) == 0) & (i == 0)
    y_ref[0:tk, :] = jnp.where(dc, ur * hr, ur * hr - ui * hi).astype(BF16)
    y_ref[tk:2 * tk, :] = jnp.where(dc, ui * hi, ur * hi + ui * hr).astype(BF16)


def _dftb_kernel(g_ref, y_ref, u_ref, x_ref, b_ref, o_ref):
    y = _dot(g_ref[...], y_ref[...])
    o_ref[...] = x_ref[...] * (y + u_ref[...] * b_ref[...])


def _spectral_conv(fmat, gmat, hspec, h_col, bias, u_arr, u_col, gate_arr, gate_col):
    B, L = u_arr.shape[0], u_arr.shape[1]
    C = D_BRANCH
    tk = min(L, 256)
    y = pl.pallas_call(
        functools.partial(_dfta_kernel, tk=tk),
        grid=(B, L // tk),
        in_specs=[pl.BlockSpec((2 * tk, L), lambda b, i: (i, 0)),
                  pl.BlockSpec((None, L, C), lambda b, i: (b, 0, u_col)),
                  pl.BlockSpec((2 * tk, C), lambda b, i: (i, h_col))],
        out_specs=pl.BlockSpec((None, 2 * tk, C), lambda b, i: (b, i, 0)),
        out_shape=jax.ShapeDtypeStruct((B, 2 * L, C), BF16),
        scratch_shapes=[pltpu.VMEM((L, C), BF16)],
        compiler_params=_cparams(("arbitrary", "arbitrary")),
        name="hyena_dft_fwd",
    )(fmat, u_arr, hspec)
    tr = min(L, 256)
    return pl.pallas_call(
        _dftb_kernel,
        grid=(B, L // tr),
        in_specs=[pl.BlockSpec((tr, 2 * L), lambda b, i: (i, 0)),
                  pl.BlockSpec((None, 2 * L, C), lambda b, i: (b, 0, 0)),
                  pl.BlockSpec((None, tr, C), lambda b, i: (b, i, u_col)),
                  pl.BlockSpec((None, tr, C), lambda b, i: (b, i, gate_col)),
                  pl.BlockSpec((1, C), lambda b, i: (0, 0))],
        out_specs=pl.BlockSpec((None, tr, C), lambda b, i: (b, i, 0)),
        out_shape=jax.ShapeDtypeStruct((B, L, C), F32),
        compiler_params=_cparams(("arbitrary", "arbitrary")),
        name="hyena_dft_inv",
    )(gmat, y, u_arr, gate_arr, bias.reshape(1, C))


def _hgrn_kernel(q_ref, ff_ref, fb_ref, v_ref, g_ref, lb_ref, nrm_ref, s0_ref, y_ref, sf_ref,
                 of_scr, ob_scr, st_scr, *, L, layer, depth):
    C = HG_CHUNK
    W = HG_DK
    nc = L // C
    nh = q_ref.shape[1] // W
    ridx = _iota((C, W), 0)
    o_scrs = (of_scr, ob_scr)
    f_refs = (ff_ref, fb_ref)

    def lower_bound(d, cols):
        rows = [lb_ref[d, l:l + 1, cols] for l in range(depth)]
        m = rows[0]
        for r in rows[1:]:
            m = jnp.maximum(m, r)
        es = [jnp.exp(r - m) for r in rows]
        tot = es[0]
        for e in es[1:]:
            tot = tot + e
        acc = es[0] / tot
        for e in es[1:layer + 1]:
            acc = acc + e / tot
        return acc - es[0] / tot

    lbs = [[lower_bound(d, slice(hh * W, (hh + 1) * W)) for hh in range(nh)] for d in (0, 1)]
    for d in (0, 1):
        for hh in range(nh):
            st_scr[d, hh] = s0_ref[d, hh].T

    def step(c, d, hh):
        rev = d == 1
        rows = pl.ds(pl.multiple_of(c * C, C), C)
        cols = slice(hh * W, (hh + 1) * W)
        lb = lbs[d][hh]
        q = _silu(q_ref[rows, cols])
        uf = f_refs[d][rows, cols]
        v = v_ref[rows, cols]
        f = lb + (1.0 - lb) * _sigmoid(uf)
        g = jnp.log(jnp.maximum(f, LOG_FLOOR))
        kin = (1.0 - lb) * _sigmoid(-uf)
        b = _cumsum_rows(g, rev)
        st = st_scr[d, hh]
        o = _bdot_nt(q * jnp.exp(b), st)
        intra = jnp.zeros((C, W), F32)
        for t in range(C):
            mask = (ridx >= t) if rev else (ridx <= t)
            pair = _masked_exp(mask, b[t:t + 1, :] - b)
            a = jnp.sum(pair * (q[t:t + 1, :] * kin), axis=1, keepdims=True)
            row = jnp.sum(a * v, axis=0, keepdims=True)
            intra = jnp.where(ridx == t, row, intra)
        b_end = b[0:1, :] if rev else b[C - 1:C, :]
        st_scr[d, hh] = st * jnp.exp(b_end) + _bdot_tn(v, kin * jnp.exp(b_end - b))
        o_scrs[d][rows, cols] = o + intra

    def body(ci, carry):
        for d in (0, 1):
            for hh in range(nh):
                step((nc - 1 - ci) if d == 1 else ci, d, hh)
        return carry

    lax.fori_loop(0, nc, body, 0)
    for d in (0, 1):
        for hh in range(nh):
            sf_ref[d, hh] = st_scr[d, hh].T

    tr = min(L, 256)

    def fin(i, carry):
        rows = pl.ds(pl.multiple_of(i * tr, tr), tr)
        for hh in range(nh):
            cols = slice(hh * W, (hh + 1) * W)
            o = of_scr[rows, cols] + ob_scr[rows, cols]
            on = o * lax.rsqrt(jnp.mean(o * o, axis=-1, keepdims=True) + EPS) * nrm_ref[...]
            y_ref[rows, cols] = on * _silu(g_ref[rows, cols])
        return carry

    lax.fori_loop(0, L // tr, fin, 0)


def _hgrn_scan(proj3, hg_lb, hg_norm, s0, *, layer):
    B, L = proj3.shape[0], proj3.shape[1]
    depth = hg_lb.shape[1]
    nh = 2
    W = nh * HG_DK
    col = lambda base: (lambda b, h: (b, 0, base // W + h))
    st_spec = pl.BlockSpec((None, 2, nh, HG_DK, HG_DK), lambda b, h: (b, 0, h, 0, 0))
    return pl.pallas_call(
        functools.partial(_hgrn_kernel, L=L, layer=layer, depth=depth),
        grid=(B, HG_HEADS // nh),
        in_specs=[pl.BlockSpec((None, L, W), col(P_HGQ)),
                  pl.BlockSpec((None, L, W), col(P_HGFF)),
                  pl.BlockSpec((None, L, W), col(P_HGFB)),
                  pl.BlockSpec((None, L, W), col(P_HGV)),
                  pl.BlockSpec((None, L, W), col(P_HGG)),
                  pl.BlockSpec((2, depth, W), lambda b, h: (0, 0, h)),
                  pl.BlockSpec((1, HG_DK), lambda b, h: (0, 0)),
                  st_spec],
        out_specs=[pl.BlockSpec((None, L, W), lambda b, h: (b, 0, h)), st_spec],
        out_shape=[jax.ShapeDtypeStruct((B, L, D_BRANCH), F32),
                   jax.ShapeDtypeStruct(s0.shape, F32)],
        scratch_shapes=[pltpu.VMEM((L, W), F32), pltpu.VMEM((L, W), F32),
                        pltpu.VMEM((2, nh, HG_DK, HG_DK), F32)],
        compiler_params=_cparams(("arbitrary", "arbitrary")),
        name="hgrn2_scan",
    )(proj3, proj3, proj3, proj3, proj3, hg_lb, hg_norm.reshape(1, HG_DK), s0)


def _ssd_kernel(x_ref, bm_ref, cm_ref, z_ref, sm_ref, bias_ref, alog_ref, dskip_ref, s0_ref,
                y_ref, sf_ref, of_scr, ob_scr, st_scr, *, L):
    C = SSD_CHUNK
    W = LANES
    nc = L // C
    npairs = x_ref.shape[1] // W
    grp = pl.program_id(1)
    lane_lo = _iota((C, W), 1) < SSD_P
    row_lo = _iota((W, SSD_N), 0) < SSD_P
    pick2 = lambda a0, a1: jnp.where(lane_lo, a0, a1)
    o_scrs = (of_scr, ob_scr)
    incls = (_tri_mask(C, False), _tri_mask(C, True))
    for d in (0, 1):
        for pp in range(npairs):
            st_scr[d, pp] = s0_ref[d, pp]

    heads = [(d, pp, hh) for d in (0, 1) for pp in range(npairs) for hh in (0, 1)]
    pairs = [(d, pp) for d in (0, 1) for pp in range(npairs)]

    def body(ci, carry):
        rows_d, bm, cm, dt_all, cum_all, gram = [], [], [], [], [], []
        for d in (0, 1):
            c = (nc - 1 - ci) if d == 1 else ci
            rows = pl.ds(pl.multiple_of(c * C, C), C)
            rows_d.append(rows)
            bm.append(_silu(bm_ref[rows, :]))
            cm.append(_silu(cm_ref[rows, :]))
            dt_all.append(_softplus(sm_ref[rows, :] + bias_ref[...]))
            cum_all.append(_cumsum_rows(-jnp.exp(alog_ref[...]) * dt_all[d], d == 1))
        lane_of = lambda u: S_DT + u[0] * SSD_HEADS + 2 * (npairs * grp + u[1]) + u[2]
        cumr = {u: _row_pick(cum_all[u[0]], lane_of(u)) for u in heads}
        for d in (0, 1):
            gram.append(_bdot_nt(cm[d], bm[d]))
        dtc = {u: _lane_pick(dt_all[u[0]], lane_of(u)) for u in heads}
        cumc = {u: _lane_pick(cum_all[u[0]], lane_of(u)) for u in heads}
        end = {u: (cumc[u][0:1, :] if u[0] == 1 else cumc[u][C - 1:C, :]) for u in heads}
        xs = {pr: _silu(x_ref[rows_d[pr[0]], slice(pr[1] * W, (pr[1] + 1) * W)]) for pr in pairs}
        st = {pr: st_scr[pr[0], pr[1]] for pr in pairs}
        y_state = {u: _bdot_nt(cm[u[0]] * jnp.exp(cumc[u]), st[u[:2]]) for u in heads}
        scores = {u: gram[u[0]] * _masked_exp(incls[u[0]], cumc[u] - cumr[u]) for u in heads}
        y_intra = {}
        for u in heads:
            head_x = jnp.where(lane_lo if u[2] == 0 else ~lane_lo, xs[u[:2]], 0.0) * dtc[u]
            y_intra[u] = _bdot(scores[u], head_x)
        for pr in pairs:
            u0, u1 = pr + (0,), pr + (1,)
            xdt_e = xs[pr] * pick2(dtc[u0] * jnp.exp(end[u0] - cumc[u0]), dtc[u1] * jnp.exp(end[u1] - cumc[u1]))
            st_scr[pr[0], pr[1]] = (st[pr] * jnp.where(row_lo, jnp.exp(end[u0]), jnp.exp(end[u1]))
                                    + _bdot_tn(xdt_e, bm[pr[0]]))
            o_scrs[pr[0]][rows_d[pr[0]], slice(pr[1] * W, (pr[1] + 1) * W)] = pick2(
                y_intra[u0] + y_state[u0], y_intra[u1] + y_state[u1])
        return carry

    lax.fori_loop(0, nc, body, 0)
    for d in (0, 1):
        for pp in range(npairs):
            sf_ref[d, pp] = st_scr[d, pp]

    tr = min(L, 256)

    def fin(i, carry):
        rows = pl.ds(pl.multiple_of(i * tr, tr), tr)
        y = of_scr[rows, :] + ob_scr[rows, :] + _silu(x_ref[rows, :]) * dskip_ref[...]
        y_ref[rows, :] = y * _silu(z_ref[rows, :])
        return carry

    lax.fori_loop(0, L // tr, fin, 0)


def _ssd_scan(proj3, bias_row, alog_row, ssd_d, s0):
    B, L = proj3.shape[0], proj3.shape[1]
    n_groups = 2
    n_pairs = SSD_HEADS // 2
    ppg = n_pairs // n_groups
    W = ppg * LANES
    s0p = s0.reshape(B, 2, n_pairs, 2 * SSD_P, SSD_N)
    dskip = jnp.repeat(ssd_d, SSD_P).reshape(n_groups, 1, W)
    col = lambda base: (lambda b, g: (b, 0, base // W + g))
    grp = lambda base: (lambda b, g: (b, 0, base // LANES + g))
    st_spec = pl.BlockSpec((None, 2, ppg, LANES, SSD_N), lambda b, g: (b, 0, g, 0, 0))
    y, sf = pl.pallas_call(
        functools.partial(_ssd_kernel, L=L),
        grid=(B, n_groups),
        in_specs=[pl.BlockSpec((None, L, W), col(C_SSDX)),
                  pl.BlockSpec((None, L, LANES), grp(C_SSDB)),
                  pl.BlockSpec((None, L, LANES), grp(C_SSDC)),
                  pl.BlockSpec((None, L, W), col(P_SSDZ)),
                  pl.BlockSpec((None, L, LANES), lambda b, g: (b, 0, P_SMALL // LANES)),
                  pl.BlockSpec((1, LANES), lambda b, g: (0, 0)),
                  pl.BlockSpec((1, LANES), lambda b, g: (0, 0)),
                  pl.BlockSpec((None, 1, W), lambda b, g: (g, 0, 0)),
                  st_spec],
        out_specs=[pl.BlockSpec((None, L, W), lambda b, g: (b, 0, g)), st_spec],
        out_shape=[jax.ShapeDtypeStruct((B, L, D_BRANCH), F32),
                   jax.ShapeDtypeStruct(s0p.shape, F32)],
        scratch_shapes=[pltpu.VMEM((L, W), F32), pltpu.VMEM((L, W), F32),
                        pltpu.VMEM((2, ppg, LANES, SSD_N), F32)],
        compiler_params=_cparams(("arbitrary", "arbitrary")),
        name="ssd_scan",
    )(proj3, proj3, proj3, proj3, proj3, bias_row, alog_row, dskip, s0p)
    return y, sf.reshape(s0.shape)


def _l2norm(a):
    return a * lax.rsqrt(jnp.sum(a * a, axis=-1, keepdims=True) + EPS)


def _gdn_kernel(q_ref, k_ref, v_ref, g_ref, sm_ref, bias_ref, alog_ref, nrm_ref, s0_ref,
                y_ref, sf_ref, of_scr, ob_scr, st_scr, *, L):
    C = GDN_CHUNK
    W = GDN_DK
    nc = L // C
    nh = q_ref.shape[1] // W
    hblk = pl.program_id(1)
    o_scrs = (of_scr, ob_scr)
    incls = (_tri_mask(C, False), _tri_mask(C, True))
    stricts = (_tri_mask(C, False, strict=True), _tri_mask(C, True, strict=True))
    for d in (0, 1):
        for hh in range(nh):
            st_scr[d, hh] = s0_ref[d, hh]

    units = [(d, hh) for d in (0, 1) for hh in range(nh)]
    eye = (_iota((C, C), 0) == _iota((C, C), 1)).astype(F32)

    def body(ci, carry):
        rows_d, cum_d, beta_d = [], [], []
        for d in (0, 1):
            c = (nc - 1 - ci) if d == 1 else ci
            rows = pl.ds(pl.multiple_of(c * C, C), C)
            raw = sm_ref[rows, :]
            rows_d.append(rows)
            cum_d.append(_cumsum_rows(-jnp.exp(alog_ref[...]) * _softplus(raw + bias_ref[...]), d == 1))
            beta_d.append(_sigmoid(raw))
        q, k, v, gc, grow, beta = {}, {}, {}, {}, {}, {}
        for u in units:
            d, hh = u
            cols = slice(hh * W, (hh + 1) * W)
            h = nh * hblk + hh
            q[u] = _l2norm(_silu(q_ref[rows_d[d], cols])) * (GDN_DK ** -0.5)
            k[u] = _l2norm(_silu(k_ref[rows_d[d], cols]))
            v[u] = _silu(v_ref[rows_d[d], cols])
            gc[u] = _lane_pick(cum_d[d], S_GA + d * GDN_HEADS + h)
            beta[u] = _lane_pick(beta_d[d], S_GB + d * GDN_HEADS + h)
        for u in units:
            grow[u] = _row_pick(cum_d[u[0]], S_GA + u[0] * GDN_HEADS + nh * hblk + u[1])
        decay = {u: _masked_exp(incls[u[0]], gc[u] - grow[u]) for u in units}
        kb = {u: k[u] * beta[u] for u in units}
        m = {u: -jnp.where(stricts[u[0]], _bdot_nt(kb[u], k[u]) * decay[u], 0.0) for u in units}
        aqk = {u: _bdot_nt(q[u], k[u]) * decay[u] for u in units}
        p = {u: eye + m[u] for u in units}
        sh = 2
        while sh < C:
            m = {u: _dot3(m[u], m[u]) for u in units}
            p = {u: p[u] + _dot3(p[u], m[u]) for u in units}
            sh *= 2
        uu = {u: _dot3(p[u], v[u] * beta[u]) for u in units}
        ww = {u: _dot3(p[u], kb[u] * jnp.exp(gc[u])) for u in units}
        st = {u: st_scr[u[0], u[1]] for u in units}
        v_new = {u: uu[u] - _bdot(ww[u], st[u]) for u in units}
        o = {u: _bdot(q[u] * jnp.exp(gc[u]), st[u]) + _bdot(aqk[u], v_new[u]) for u in units}
        for u in units:
            d, hh = u
            g_end = gc[u][0:1, :] if d == 1 else gc[u][C - 1:C, :]
            st_scr[d, hh] = st[u] * jnp.exp(g_end) + _bdot_tn(k[u] * jnp.exp(g_end - gc[u]), v_new[u])
            o_scrs[d][rows_d[d], slice(hh * W, (hh + 1) * W)] = o[u]
        return carry

    lax.fori_loop(0, nc, body, 0)
    for d in (0, 1):
        for hh in range(nh):
            sf_ref[d, hh] = st_scr[d, hh]

    tr = min(L, 256)

    def fin(i, carry):
        rows = pl.ds(pl.multiple_of(i * tr, tr), tr)
        for hh in range(nh):
            cols = slice(hh * W, (hh + 1) * W)
            o = of_scr[rows, cols] + ob_scr[rows, cols]
            on = o * lax.rsqrt(jnp.mean(o * o, axis=-1, keepdims=True) + EPS) * nrm_ref[...]
            y_ref[rows, cols] = on * _silu(g_ref[rows, cols])
        return carry

    lax.fori_loop(0, L // tr, fin, 0)


def _gdn_scan(proj3, bias_row, alog_row, gdn_norm, s0):
    B, L = proj3.shape[0], proj3.shape[1]
    nh = 2
    W = nh * GDN_DK
    col = lambda base: (lambda b, h: (b, 0, base // W + h))
    st_spec = pl.BlockSpec((None, 2, nh, GDN_DK, GDN_DK), lambda b, h: (b, 0, h, 0, 0))
    return pl.pallas_call(
        functools.partial(_gdn_kernel, L=L),
        grid=(B, GDN_HEADS // nh),
        in_specs=[pl.BlockSpec((None, L, W), col(C_GQ)),
                  pl.BlockSpec((None, L, W), col(C_GK)),
                  pl.BlockSpec((None, L, W), col(C_GV)),
                  pl.BlockSpec((None, L, W), col(P_GDNG)),
                  pl.BlockSpec((None, L, LANES), lambda b, h: (b, 0, P_SMALL // LANES)),
                  pl.BlockSpec((1, LANES), lambda b, h: (0, 0)),
                  pl.BlockSpec((1, LANES), lambda b, h: (0, 0)),
                  pl.BlockSpec((1, GDN_DK), lambda b, h: (0, 0)),
                  st_spec],
        out_specs=[pl.BlockSpec((None, L, W), lambda b, h: (b, 0, h)), st_spec],
        out_shape=[jax.ShapeDtypeStruct((B, L, D_BRANCH), F32),
                   jax.ShapeDtypeStruct(s0.shape, F32)],
        scratch_shapes=[pltpu.VMEM((L, W), F32), pltpu.VMEM((L, W), F32),
                        pltpu.VMEM((2, nh, GDN_DK, GDN_DK), F32)],
        compiler_params=_cparams(("arbitrary", "arbitrary")),
        name="gdn_scan",
    )(proj3, proj3, proj3, proj3, proj3, bias_row, alog_row, gdn_norm.reshape(1, GDN_DK), s0)


def _merge_kernel(ya_ref, yb_ref, yc_ref, yd_ref, mg_ref, x_ref, mod_ref, sn_ref, n2_ref, wb_ref, wo_ref,
                  xo_ref, h_ref):
    yc = yc_ref[...]
    half = D_BRANCH // 2
    parts = []
    for gidx in range(2):
        seg = yc[:, gidx * half:(gidx + 1) * half]
        parts.append(seg * lax.rsqrt(jnp.mean(seg * seg, axis=-1, keepdims=True) + EPS)
                     * sn_ref[:, gidx * half:(gidx + 1) * half])
    branches = (ya_ref[...], yb_ref[...], None, yd_ref[...])
    mixed = None
    for n in range(N_BRANCH):
        gate = _sigmoid(mg_ref[:, n * D_MODEL:(n + 1) * D_MODEL])
        if n == 2:
            lifted = (_dot(parts[0].astype(BF16), wb_ref[n, 0:half, :])
                      + _dot(parts[1].astype(BF16), wb_ref[n, half:D_BRANCH, :]))
        else:
            lifted = _dot(branches[n].astype(BF16), wb_ref[n])
        mixed = gate * lifted if mixed is None else mixed + gate * lifted
    m = mod_ref[...]
    x = x_ref[...] + m[2:3] * _dot(mixed.astype(BF16), wo_ref[...])
    xo_ref[...] = x
    xn = x * lax.rsqrt(jnp.mean(x * x, axis=-1, keepdims=True) + EPS) * n2_ref[...]
    h_ref[...] = xn * (1.0 + m[4:5]) + m[3:4]


def _merge(ya, yb, yc, yd, proj, x, mod, ssd_norm, norm2_g, wb_bf16, wo_bf16, *, tokens_per_mod):
    T = x.shape[0]
    tm = 256
    tiles_per_mod = tokens_per_mod // tm
    yspec = pl.BlockSpec((tm, D_BRANCH), lambda i: (i, 0))
    xspec = pl.BlockSpec((tm, D_MODEL), lambda i: (i, 0))
    return pl.pallas_call(
        _merge_kernel,
        grid=(T // tm,),
        in_specs=[yspec, yspec, yspec, yspec,
                  pl.BlockSpec((tm, N_BRANCH * D_MODEL), lambda i: (i, P_MERGE // (N_BRANCH * D_MODEL))),
                  xspec,
                  pl.BlockSpec((None, 6, D_MODEL), lambda i: (i // tiles_per_mod, 0, 0)),
                  pl.BlockSpec((1, D_BRANCH), lambda i: (0, 0)),
                  pl.BlockSpec((1, D_MODEL), lambda i: (0, 0)),
                  pl.BlockSpec((N_BRANCH, D_BRANCH, D_MODEL), lambda i: (0, 0, 0)),
                  pl.BlockSpec((D_MODEL, D_MODEL), lambda i: (0, 0))],
        out_specs=[xspec, xspec],
        out_shape=[jax.ShapeDtypeStruct((T, D_MODEL), F32)] * 2,
        compiler_params=_cparams(("arbitrary",)),
        name="merge_outproj",
    )(ya, yb, yc, yd, proj, x, mod, ssd_norm.reshape(1, D_BRANCH), norm2_g.reshape(1, D_MODEL), wb_bf16, wo_bf16)


def _router_kernel(h_ref, rw_ref, rb_ref, idx_ref, wt_ref, rank_ref, cnt_ref, carry):
    i = pl.program_id(0)
    tm = h_ref.shape[0]
    E = N_EXPERTS
    gsz = E // N_EXPERT_GROUPS

    @pl.when(i == 0)
    def _():
        carry[...] = jnp.zeros_like(carry)

    scores = _sigmoid(_dot(h_ref[...], rw_ref[...], precision=HIGHEST))
    biased = scores + rb_ref[...]
    lane = _iota((tm, E), 1)
    grp = jnp.right_shift(lane, int(math.log2(gsz)))
    neg_inf = -jnp.inf

    gs = []
    for g in range(N_EXPERT_GROUPS):
        vals = jnp.where(grp == g, biased, neg_inf)
        m1 = jnp.max(vals, axis=1, keepdims=True)
        i1 = jnp.min(jnp.where(vals == m1, lane, E), axis=1, keepdims=True)
        m2 = jnp.max(jnp.where(lane == i1, neg_inf, vals), axis=1, keepdims=True)
        gs.append(m1 + m2)
    exp_mask = jnp.zeros((tm, E), jnp.bool_)
    for g in range(N_EXPERT_GROUPS):
        rank = jnp.zeros((tm, 1), I32)
        for g2 in range(N_EXPERT_GROUPS):
            if g2 == g:
                continue
            ahead = (gs[g2] >= gs[g]) if g2 < g else (gs[g2] > gs[g])
            rank = rank + ahead.astype(I32)
        exp_mask = exp_mask | ((grp == g) & (rank < TOPK_GROUPS))
    cur = jnp.where(exp_mask, biased, MASK_NEG)

    olane = _iota((tm, LANES), 1)
    idx_out = jnp.zeros((tm, LANES), I32)
    wt_out = jnp.zeros((tm, LANES), F32)
    onehot = jnp.zeros((tm, E), F32)
    picks = []
    wsum = jnp.zeros((tm, 1), F32)
    for kk in range(TOP_K):
        m = jnp.max(cur, axis=1, keepdims=True)
        ik = jnp.min(jnp.where(cur == m, lane, E), axis=1, keepdims=True)
        hit = lane == ik
        wk = jnp.sum(jnp.where(hit, scores, 0.0), axis=1, keepdims=True)
        cur = jnp.where(hit, neg_inf, cur)
        onehot = onehot + hit.astype(F32)
        picks.append((ik, wk))
        wsum = wsum + wk
        idx_out = jnp.where(olane == kk, ik, idx_out)
    for kk, (ik, wk) in enumerate(picks):
        wt_out = jnp.where(olane == kk, wk / wsum * ROUTED_SCALE, wt_out)

    strict_lower = (_iota((tm, tm), 1) < _iota((tm, tm), 0)).astype(BF16)
    before = _dot(strict_lower, onehot.astype(BF16)) + carry[...]
    rank_out = jnp.zeros((tm, LANES), I32)
    for kk, (ik, wk) in enumerate(picks):
        rk = jnp.sum(jnp.where(lane == ik, before, 0.0), axis=1, keepdims=True)
        rank_out = jnp.where(olane == kk, rk.astype(I32), rank_out)
    carry[...] = carry[...] + jnp.sum(onehot, axis=0, keepdims=True)

    idx_ref[...] = idx_out
    wt_ref[...] = wt_out
    rank_ref[...] = rank_out
    cnt_ref[...] = carry[...]


def _router(h2, router_w, router_b):
    T = h2.shape[0]
    tm = 256
    ospec = pl.BlockSpec((tm, LANES), lambda i: (i, 0))
    return pl.pallas_call(
        _router_kernel,
        grid=(T // tm,),
        in_specs=[pl.BlockSpec((tm, D_MODEL), lambda i: (i, 0)),
                  pl.BlockSpec((D_MODEL, N_EXPERTS), lambda i: (0, 0)),
                  pl.BlockSpec((1, N_EXPERTS), lambda i: (0, 0))],
        out_specs=[ospec, ospec, ospec, pl.BlockSpec((1, N_EXPERTS), lambda i: (0, 0))],
        out_shape=[jax.ShapeDtypeStruct((T, LANES), I32), jax.ShapeDtypeStruct((T, LANES), F32),
                   jax.ShapeDtypeStruct((T, LANES), I32), jax.ShapeDtypeStruct((1, N_EXPERTS), F32)],
        scratch_shapes=[pltpu.VMEM((1, N_EXPERTS), F32)],
        compiler_params=_cparams(("arbitrary",)),
        name="moe_router",
    )(h2, router_w, router_b.reshape(1, N_EXPERTS))


def _dispatch_kernel(dest_ref, h_ref, xs_ref, sem):
    tm = h_ref.shape[0]

    def row_copies(t):
        src = h_ref.at[pl.ds(t, 1), :]
        return [pltpu.make_async_copy(src, xs_ref.at[pl.ds(dest_ref[t * TOP_K + kk], 1), :], sem)
                for kk in range(TOP_K)]

    def start(t, carry):
        for cp in row_copies(t):
            cp.start()
        return carry

    def wait(t, carry):
        for cp in row_copies(t):
            cp.wait()
        return carry

    lax.fori_loop(0, tm, start, 0)
    lax.fori_loop(0, tm, wait, 0)


def _dispatch(h2, dest_flat):
    T = h2.shape[0]
    tm = 256
    return pl.pallas_call(
        _dispatch_kernel,
        grid=(T // tm,),
        in_specs=[pl.BlockSpec((tm * TOP_K,), lambda i: (i,), memory_space=pltpu.SMEM),
                  pl.BlockSpec((tm, D_MODEL), lambda i: (i, 0))],
        out_specs=pl.BlockSpec(memory_space=pl.ANY),
        out_shape=jax.ShapeDtypeStruct((T * TOP_K, D_MODEL), F32),
        scratch_shapes=[pltpu.SemaphoreType.DMA(())],
        compiler_params=_cparams(("arbitrary",)),
        name="moe_dispatch",
    )(dest_flat, h2)


def _expert_kernel(blk_ref, exp_ref, lo_ref, hi_ref, first_ref, x_ref, w1_ref, w3_ref, w2_ref, o_ref):
    w = pl.program_id(0)
    x = x_ref[...].astype(BF16)
    a = _dot(x, w1_ref[...].astype(BF16))
    b = _dot(x, w3_ref[...].astype(BF16))
    y = _dot((_silu(a) * b).astype(BF16), w2_ref[...].astype(BF16))
    r = _iota(y.shape, 0)
    y = jnp.where((r >= lo_ref[w]) & (r < hi_ref[w]), y, 0.0)

    @pl.when(first_ref[w] == 1)
    def _():
        o_ref[...] = y

    @pl.when(first_ref[w] == 0)
    def _():
        o_ref[...] = o_ref[...] + y


def _experts(xs, work, w1, w3, w2, *, bm):
    A = xs.shape[0]
    n_work = work[0].shape[0]
    xmap = lambda w, blk, ex, lo, hi, first: (blk[w], 0)
    wmap = lambda w, blk, ex, lo, hi, first: (ex[w], 0, 0)
    return pl.pallas_call(
        _expert_kernel,
        grid_spec=pltpu.PrefetchScalarGridSpec(
            num_scalar_prefetch=5,
            grid=(n_work,),
            in_specs=[pl.BlockSpec((bm, D_MODEL), xmap),
                      pl.BlockSpec((None, D_MODEL, D_EXPERT), wmap),
                      pl.BlockSpec((None, D_MODEL, D_EXPERT), wmap),
                      pl.BlockSpec((None, D_EXPERT, D_MODEL), wmap)],
            out_specs=pl.BlockSpec((bm, D_MODEL), xmap)),
        out_shape=jax.ShapeDtypeStruct((A, D_MODEL), F32),
        compiler_params=_cparams(("arbitrary",)),
        name="moe_experts",
    )(*work, xs, w1, w3, w2)


def _combine_kernel(dest_ref, ys_ref, wt_ref, h_ref, x_ref, mod_ref, s1_ref, s3_ref, s2_ref, fg_ref,
                    o_ref, buf, sem, *, final_norm):
    tc = h_ref.shape[0]

    def row_copies(t):
        return [pltpu.make_async_copy(ys_ref.at[pl.ds(dest_ref[t * TOP_K + kk], 1), :],
                                      buf.at[kk, pl.ds(t, 1), :], sem)
                for kk in range(TOP_K)]

    def start(t, carry):
        for cp in row_copies(t):
            cp.start()
        return carry

    def wait(t, carry):
        for cp in row_copies(t):
            cp.wait()
        return carry

    lax.fori_loop(0, tc, start, 0)
    hb = h_ref[...].astype(BF16)
    mid = _silu(_dot(hb, s1_ref[...])) * _dot(hb, s3_ref[...])
    y = _dot(mid.astype(BF16), s2_ref[...])
    lax.fori_loop(0, tc, wait, 0)
    for kk in range(TOP_K):
        y = y + wt_ref[:, kk:kk + 1] * buf[kk]
    x = x_ref[...] + mod_ref[5:6, :] * y
    if final_norm:
        x = x * lax.rsqrt(jnp.mean(x * x, axis=-1, keepdims=True) + EPS) * fg_ref[...]
    o_ref[...] = x


def _combine(ys, dest_flat, wt, h2, x, mod, s1, s3, s2, final_g, *, tokens_per_mod, final_norm):
    T = x.shape[0]
    tc = 128
    tiles_per_mod = tokens_per_mod // tc
    xspec = pl.BlockSpec((tc, D_MODEL), lambda i: (i, 0))
    return pl.pallas_call(
        functools.partial(_combine_kernel, final_norm=final_norm),
        grid=(T // tc,),
        in_specs=[pl.BlockSpec((tc * TOP_K,), lambda i: (i,), memory_space=pltpu.SMEM),
                  pl.BlockSpec(memory_space=pl.ANY),
                  pl.BlockSpec((tc, LANES), lambda i: (i, 0)),
                  xspec, xspec,
                  pl.BlockSpec((None, 6, D_MODEL), lambda i: (i // tiles_per_mod, 0, 0)),
                  pl.BlockSpec((D_MODEL, D_SHARED), lambda i: (0, 0)),
                  pl.BlockSpec((D_MODEL, D_SHARED), lambda i: (0, 0)),
                  pl.BlockSpec((D_SHARED, D_MODEL), lambda i: (0, 0)),
                  pl.BlockSpec((1, D_MODEL), lambda i: (0, 0))],
        out_specs=xspec,
        out_shape=jax.ShapeDtypeStruct((T, D_MODEL), F32),
        scratch_shapes=[pltpu.VMEM((TOP_K, tc, D_MODEL), F32), pltpu.SemaphoreType.DMA(())],
        compiler_params=_cparams(("arbitrary",)),
        name="moe_combine",
    )(dest_flat, ys, wt, h2, x, mod, s1, s3, s2, final_g.reshape(1, D_MODEL))


def _work_list(counts, n_rows, bm):
    E = N_EXPERTS
    n_blocks = n_rows // bm
    n_work = n_blocks + E - 1
    ends = jnp.cumsum(counts)
    starts = ends - counts
    first_blk = starts // bm
    last_blk = jnp.where(counts > 0, (ends - 1) // bm, first_blk)
    n_items = jnp.where(counts > 0, last_blk - first_blk + 1, 0)
    item_end = jnp.cumsum(n_items)
    item_start = item_end - n_items
    w = jnp.arange(n_work, dtype=I32)
    ex = jnp.minimum(jnp.searchsorted(item_end, w, side="right"), E - 1).astype(I32)
    valid = w < item_end[-1]
    blk = jnp.where(valid, first_blk[ex] + (w - item_start[ex]), n_blocks - 1).astype(I32)
    lo = jnp.clip(starts[ex] - blk * bm, 0, bm)
    hi = jnp.clip(ends[ex] - blk * bm, 0, bm)
    lo = jnp.where(valid, lo, 0).astype(I32)
    hi = jnp.where(valid, hi, 0).astype(I32)
    ex = jnp.where(valid, ex, ex[jnp.maximum(item_end[-1] - 1, 0)])
    prev_blk = jnp.concatenate([jnp.full((1,), -1, I32), blk[:-1]])
    first = (blk != prev_blk).astype(I32)
    return blk, ex.astype(I32), lo, hi, first


def _moe(h2, x, mod, p, final_g, *, tokens_per_mod, final_norm):
    T = x.shape[0]
    idx, wt, rank, counts = _router(h2, p["router_w"], p["router_b"])
    counts = counts.reshape(N_EXPERTS).astype(I32)
    starts = jnp.cumsum(counts) - counts
    dest = (starts[idx[:, :TOP_K]] + rank[:, :TOP_K]).reshape(T * TOP_K)
    xs = _dispatch(h2, dest)
    bm = 512
    work = _work_list(counts, T * TOP_K, bm)
    ys = _experts(xs, work, p["exp_w1"], p["exp_w3"], p["exp_w2"], bm=bm)
    return _combine(ys, dest, wt, h2, x, mod, p["sh_w1"], p["sh_w3"], p["sh_w2"], final_g,
                    tokens_per_mod=tokens_per_mod, final_norm=final_norm)


def _reorder_w_in(w_in):
    plain = D_CONV
    pieces = [w_in[..., :plain],
              w_in[..., plain + 3616:plain + 3616 + 4096],
              w_in[..., plain:plain + 3072],
              w_in[..., plain + 3088:plain + 3600],
              w_in[..., plain + 3072:plain + 3088],
              w_in[..., plain + 3600:plain + 3616]]
    out = jnp.concatenate(pieces, axis=-1)
    pad = D_PROJ - out.shape[-1]
    return jnp.pad(out, ((0, 0), (0, 0), (0, pad))).astype(BF16)


def _small_rows(ssd_vals, gdn_vals):
    row = jnp.zeros((LANES,), F32)
    row = row.at[S_DT:S_DT + 2 * SSD_HEADS].set(ssd_vals.reshape(-1))
    row = row.at[S_GA:S_GA + 2 * GDN_HEADS].set(gdn_vals.reshape(-1))
    return row.reshape(1, LANES)


def _layer_pass(x, mod, lp, hy, st_hg, st_ssd, st_gdn, final_g, *, B, L, tokens_per_mod, seg, layer, final_norm):
    T = B * L
    proj = _in_projection(x, mod, lp["norm1_g"], lp["w_in"], lp["conv_w"], lp["conv_b"],
                          tokens_per_mod=tokens_per_mod, seg=seg)
    proj3 = proj.reshape(B, L, D_PROJ)
    fmat, gmat, hspec = hy
    cb = lambda c: c // D_BRANCH
    z = _spectral_conv(fmat, gmat, hspec, 0, lp["hy_bias"][0], proj3, cb(C_HYV), proj3, cb(C_HYX1))
    ya = _spectral_conv(fmat, gmat, hspec, 1, lp["hy_bias"][1], z, 0, proj3, cb(C_HYX2))
    yb, s_hg = _hgrn_scan(proj3, lp["hg_lb"], lp["hg_norm"], st_hg, layer=layer)
    yc, s_ssd = _ssd_scan(proj3, lp["bias_row"], lp["alog_row"], lp["ssd_d"], st_ssd)
    yd, s_gdn = _gdn_scan(proj3, lp["bias_row"], lp["alog_row"], lp["gdn_norm"], st_gdn)
    flat = lambda a: a.reshape(T, D_BRANCH)
    x, h2 = _merge(flat(ya), flat(yb), flat(yc), flat(yd), proj, x, mod, lp["ssd_norm"], lp["norm2_g"],
                   lp["w_branch"], lp["w_out"], tokens_per_mod=tokens_per_mod)
    x = _moe(h2, x, mod, lp, final_g, tokens_per_mod=tokens_per_mod, final_norm=final_norm)
    return x, s_hg, s_ssd, s_gdn


def kernel(x_prompt, x_sample, state_hgrn, state_ssd, state_gdn, c, c_ctx, norm1_g, norm2_g, ada_w, ada_b, w_in, conv_w, conv_b, hy_w1, hy_b1, hy_w2, hy_b2, hy_w3, hy_bias, hg_lb, hg_norm, ssd_a_log, ssd_dt_bias, ssd_d, ssd_norm, gdn_a_log, gdn_dt_bias, gdn_norm, w_branch, w_out, router_w, router_bias, exp_w1, exp_w3, exp_w2, sh_w1, sh_w3, sh_w2, final_g):
    depth = w_in.shape[0]
    bp, lp_len = x_prompt.shape[0], x_prompt.shape[1]
    bs, ls_len = x_sample.shape[0], x_sample.shape[1]
    D = D_MODEL

    w_in_r = _reorder_w_in(w_in)
    hy_w1p = jnp.pad(hy_w1, ((0, 0), (0, LANES - hy_w1.shape[1]), (0, 0)))
    layers = []
    for l in range(depth):
        layers.append(dict(
            norm1_g=norm1_g[l], norm2_g=norm2_g[l], w_in=w_in_r[l], conv_w=conv_w[l], conv_b=conv_b[l],
            hy_bias=hy_bias[l], hg_lb=hg_lb, hg_norm=hg_norm[l],
            bias_row=_small_rows(ssd_dt_bias[l], gdn_dt_bias[l]),
            alog_row=_small_rows(ssd_a_log[l], gdn_a_log[l]),
            ssd_d=ssd_d[l], ssd_norm=ssd_norm[l], gdn_norm=gdn_norm[l],
            w_branch=w_branch[l].astype(BF16), w_out=w_out[l].astype(BF16),
            router_w=router_w[l], router_b=router_bias[l],
            exp_w1=exp_w1[l], exp_w3=exp_w3[l], exp_w2=exp_w2[l],
            sh_w1=sh_w1[l].astype(BF16), sh_w3=sh_w3[l].astype(BF16), sh_w2=sh_w2[l].astype(BF16)))

    def hyena_setup(L):
        fmat, fs = _dft_matrices(L)
        gmat = fs.T
        specs = []
        for l in range(depth):
            filt = _hyena_filters(L, hy_w1p[l], hy_b1[l], hy_w2[l], hy_b2[l], hy_w3[l])
            specs.append(_filter_spectrum(fmat, filt))
        return fmat, gmat, specs

    cond = jnp.concatenate([c_ctx.reshape(1, D), c], axis=0)
    rows = cond.shape[0]
    rows8 = (rows + 7) // 8 * 8
    cond8 = jnp.pad(cond, ((0, rows8 - rows), (0, 0)))
    mods = [_modulation(cond8, ada_w[l], ada_b[l]).reshape(rows8, 6, D) for l in range(depth)]

    fmat, gmat, specs = hyena_setup(lp_len)
    x = x_prompt.reshape(bp * lp_len, D)
    z_hg = jnp.zeros((bp, 2, HG_HEADS, HG_DK, HG_DK), F32)
    z_ssd = jnp.zeros((bp, 2, SSD_HEADS, SSD_P, SSD_N), F32)
    z_gdn = jnp.zeros((bp, 2, GDN_HEADS, GDN_DK, GDN_DK), F32)
    hg_states, ssd_states, gdn_states = [], [], []
    for l in range(depth):
        x, s_hg, s_ssd, s_gdn = _layer_pass(
            x, mods[l][0:1], layers[l], (fmat, gmat, specs[l]), z_hg, z_ssd, z_gdn, final_g,
            B=bp, L=lp_len, tokens_per_mod=bp * lp_len, seg=lp_len, layer=l, final_norm=(l == depth - 1))
        hg_states.append(s_hg)
        ssd_states.append(s_ssd)
        gdn_states.append(s_gdn)
    y_prompt = x.reshape(bp, lp_len, D)
    new_hg = jnp.stack(hg_states, axis=1)
    new_ssd = jnp.stack(ssd_states, axis=1)
    new_gdn = jnp.stack(gdn_states, axis=1)

    fmat, gmat, specs = hyena_setup(ls_len)
    x = x_sample.reshape(bs * ls_len, D)
    for l in range(depth):
        x, _, _, _ = _layer_pass(
            x, mods[l][1:1 + bs], layers[l], (fmat, gmat, specs[l]),
            state_hgrn[:, l], state_ssd[:, l], state_gdn[:, l], final_g,
            B=bs, L=ls_len, tokens_per_mod=ls_len, seg=GRID_W, layer=l, final_norm=(l == depth - 1))
    y_sample = x.reshape(bs, ls_len, D)
    return (y_prompt, y_sample, new_hg, new_ssd, new_gdn)
```

```python
import functools
import math

import numpy as np
import jax
import jax.numpy as jnp
from jax import lax
from jax.experimental import pallas as pl
from jax.experimental.pallas import tpu as pltpu

F32 = jnp.float32
BF16 = jnp.bfloat16
I32 = jnp.int32
HIGHEST = lax.Precision.HIGHEST

D_MODEL = 1024
GRID_W = 64
EPS = 1e-6
LOG_FLOOR = 1e-30
MASK_NEG = -1e30
N_BRANCH = 4
D_BRANCH = 512
HY_POS_FREQS = 16
HY_FILTER_HIDDEN = 64
HY_FAST_DECAY = 0.3
HY_SLOW_DECAY = 1.5
HY_DECAY_TARGET = 1e-2
HG_HEADS = 4
HG_DK = 128
HG_CHUNK = 16
SSD_HEADS = 8
SSD_P = 64
SSD_N = 128
SSD_CHUNK = 64
GDN_HEADS = 4
GDN_DK = 128
GDN_CHUNK = 64
N_EXPERTS = 64
TOP_K = 8
N_EXPERT_GROUPS = 8
TOPK_GROUPS = 4
D_EXPERT = 256
D_SHARED = 256
ROUTED_SCALE = 2.5

LANES = 128

D_CONV = 4096
C_HYV, C_HYX1, C_HYX2 = 0, 512, 1024
C_SSDX, C_SSDB, C_SSDC = 1536, 2048, 2304
C_GQ, C_GK, C_GV = 2560, 3072, 3584
P_MERGE = 4096
P_HGQ, P_HGFF, P_HGFB, P_HGV, P_HGG = 8192, 8704, 9216, 9728, 10240
P_SSDZ, P_GDNG, P_SMALL = 10752, 11264, 11776
D_PROJ = 12288
S_DT, S_GA, S_GB = 0, 16, 24

VMEM_LIMIT = 56 * 1024 * 1024


def _cparams(sem):
    return pltpu.CompilerParams(dimension_semantics=sem, vmem_limit_bytes=VMEM_LIMIT)


def _sigmoid(x):
    return 1.0 / (1.0 + jnp.exp(-x))


def _silu(x):
    return x * _sigmoid(x)


def _softplus(x):
    return jnp.maximum(x, 0.0) + jnp.log(1.0 + jnp.exp(-jnp.abs(x)))


def _dot(a, b, precision=None):
    return jnp.dot(a, b, preferred_element_type=F32, precision=precision)


def _dot_nt(a, b, precision=None):
    return lax.dot_general(a, b, (((1,), (1,)), ((), ())), preferred_element_type=F32, precision=precision)


def _dot_tn(a, b, precision=None):
    return lax.dot_general(a, b, (((0,), (0,)), ((), ())), preferred_element_type=F32, precision=precision)


def _bdot(a, b):
    return _dot(a.astype(BF16), b.astype(BF16))


def _bdot_nt(a, b):
    return _dot_nt(a.astype(BF16), b.astype(BF16))


def _bdot_tn(a, b):
    return _dot_tn(a.astype(BF16), b.astype(BF16))


def _iota(shape, dim):
    return lax.broadcasted_iota(I32, shape, dim)


def _cumsum_rows(g, reverse):
    n = g.shape[0]
    row = _iota(g.shape, 0)
    sh = 1
    while sh < n:
        if reverse:
            g = g + jnp.where(row < n - sh, pltpu.roll(g, n - sh, 0), 0.0)
        else:
            g = g + jnp.where(row >= sh, pltpu.roll(g, sh, 0), 0.0)
        sh *= 2
    return g


def _lane_pick(a, j):
    return jnp.sum(jnp.where(_iota(a.shape, 1) == j, a, 0.0), axis=1, keepdims=True)


def _split3(a):
    a1 = a.astype(BF16)
    r1 = a - a1.astype(F32)
    a2 = r1.astype(BF16)
    a3 = (r1 - a2.astype(F32)).astype(BF16)
    return a1, a2, a3


def _dot3(a, b):
    a1, a2, _ = _split3(a)
    b1, b2, _ = _split3(b)
    return _dot(a1, b1) + (_dot(a1, b2) + _dot(a2, b1))


def _row_pick(a, j):
    sel = (_iota((8, a.shape[1]), 1) == j).astype(BF16)
    a1, a2, a3 = _split3(a)
    return (_dot_nt(sel, a1) + (_dot_nt(sel, a2) + _dot_nt(sel, a3)))[0:1, :]


def _tri_mask(n, reverse, strict=False):
    t = _iota((n, n), 0)
    s = _iota((n, n), 1)
    if reverse:
        return (s > t) if strict else (s >= t)
    return (s < t) if strict else (s <= t)


def _masked_exp(mask, diff):
    return jnp.where(mask, jnp.exp(jnp.where(mask, diff, 0.0)), 0.0)


def _mod_kernel(c_ref, w_ref, b_ref, o_ref):
    o_ref[...] = _dot(_silu(c_ref[...]), w_ref[...], precision=HIGHEST) + b_ref[...]


def _modulation(cond8, ada_w, ada_b):
    rows = cond8.shape[0]
    tn = 1536
    n = ada_w.shape[1]
    return pl.pallas_call(
        _mod_kernel,
        grid=(n // tn,),
        in_specs=[pl.BlockSpec((rows, D_MODEL), lambda j: (0, 0)),
                  pl.BlockSpec((D_MODEL, tn), lambda j: (0, j)),
                  pl.BlockSpec((1, tn), lambda j: (0, j))],
        out_specs=pl.BlockSpec((rows, tn), lambda j: (0, j)),
        out_shape=jax.ShapeDtypeStruct((rows, n), F32),
        compiler_params=_cparams(("arbitrary",)),
        name="adaln_mod",
    )(cond8, ada_w, ada_b.reshape(1, n))


def _inproj_kernel(x_ref, mod_ref, g_ref, w_ref, cw_ref, cb_ref, o_ref, h_scr, *, seg, n_conv_tiles):
    j = pl.program_id(1)

    @pl.when(j == 0)
    def _():
        x = x_ref[...]
        xn = x * lax.rsqrt(jnp.mean(x * x, axis=-1, keepdims=True) + EPS) * g_ref[...]
        m = mod_ref[...]
        h_scr[...] = (xn * (1.0 + m[1:2]) + m[0:1]).astype(BF16)

    y = _dot(h_scr[...], w_ref[...])

    @pl.when(j < n_conv_tiles)
    def _():
        tm = y.shape[0]
        pos = _iota(y.shape, 0) & (seg - 1)
        prev = jnp.where(pos == 0, 0.0, pltpu.roll(y, 1, 0))
        nxt = jnp.where(pos == seg - 1, 0.0, pltpu.roll(y, tm - 1, 0))
        cw = cw_ref[...]
        o_ref[...] = cb_ref[...] + prev * cw[0:1] + y * cw[1:2] + nxt * cw[2:3]

    @pl.when(j >= n_conv_tiles)
    def _():
        o_ref[...] = y


def _in_projection(x, mod, norm_g, w_bf16, conv_w, conv_b, *, tokens_per_mod, seg):
    T = x.shape[0]
    tm = min(1024, tokens_per_mod)
    tn = 1024
    n_conv_tiles = D_CONV // tn
    tiles_per_mod = tokens_per_mod // tm
    kern = functools.partial(_inproj_kernel, seg=seg, n_conv_tiles=n_conv_tiles)
    cmap = lambda i, j: (0, jnp.minimum(j, n_conv_tiles - 1))
    return pl.pallas_call(
        kern,
        grid=(T // tm, D_PROJ // tn),
        in_specs=[pl.BlockSpec((tm, D_MODEL), lambda i, j: (i, 0)),
                  pl.BlockSpec((None, 6, D_MODEL), lambda i, j: (i // tiles_per_mod, 0, 0)),
                  pl.BlockSpec((1, D_MODEL), lambda i, j: (0, 0)),
                  pl.BlockSpec((D_MODEL, tn), lambda i, j: (0, j)),
                  pl.BlockSpec((3, tn), cmap),
                  pl.BlockSpec((1, tn), cmap)],
        out_specs=pl.BlockSpec((tm, tn), lambda i, j: (i, j)),
        out_shape=jax.ShapeDtypeStruct((T, D_PROJ), F32),
        scratch_shapes=[pltpu.VMEM((tm, D_MODEL), BF16)],
        compiler_params=_cparams(("arbitrary", "arbitrary")),
        name="in_proj",
    )(x, mod, norm_g.reshape(1, D_MODEL), w_bf16, conv_w, conv_b.reshape(1, D_CONV))


def _hyfilt_kernel(w1_ref, b1_ref, w2_ref, b2_ref, w3_ref, o_ref, *, L):
    i = pl.program_id(1)
    tl = o_ref.shape[0]
    t = (_iota((tl, LANES), 0) + i * tl).astype(F32) / L
    lane = _iota((tl, LANES), 1)
    band = jnp.where(lane <= HY_POS_FREQS, lane, lane - HY_POS_FREQS).astype(F32)
    ang = 2.0 * math.pi * t * band
    feats = jnp.where(lane == 0, t,
                      jnp.where(lane <= HY_POS_FREQS, jnp.sin(ang),
                                jnp.where(lane <= 2 * HY_POS_FREQS, jnp.cos(ang), 0.0)))
    hdn = jnp.sin(_dot(feats, w1_ref[...], precision=HIGHEST) + b1_ref[...])
    hdn = jnp.sin(_dot(hdn, w2_ref[...], precision=HIGHEST) + b2_ref[...])
    filt = _dot(hdn, w3_ref[...], precision=HIGHEST)
    max_decay = math.log(HY_DECAY_TARGET) / HY_FAST_DECAY
    min_decay = math.log(HY_DECAY_TARGET) / HY_SLOW_DECAY
    n = filt.shape[1]
    ch = (_iota((tl, n), 1) & (D_BRANCH - 1)).astype(F32)
    delta = min_decay + ch * ((max_decay - min_decay) / (D_BRANCH - 1))
    tt = (_iota((tl, n), 0) + i * tl).astype(F32) / L
    o_ref[...] = filt * jnp.exp(-tt * jnp.abs(delta))


def _hyena_filters(L, w1p, b1, w2, b2, w3):
    tl = min(L, 256)
    n = 2 * D_BRANCH
    return pl.pallas_call(
        functools.partial(_hyfilt_kernel, L=L),
        grid=(2, L // tl),
        in_specs=[pl.BlockSpec((LANES, HY_FILTER_HIDDEN), lambda d, i: (0, 0)),
                  pl.BlockSpec((1, HY_FILTER_HIDDEN), lambda d, i: (0, 0)),
                  pl.BlockSpec((HY_FILTER_HIDDEN, HY_FILTER_HIDDEN), lambda d, i: (0, 0)),
                  pl.BlockSpec((1, HY_FILTER_HIDDEN), lambda d, i: (0, 0)),
                  pl.BlockSpec((HY_FILTER_HIDDEN, n), lambda d, i: (0, d))],
        out_specs=pl.BlockSpec((None, tl, n), lambda d, i: (d, i, 0)),
        out_shape=jax.ShapeDtypeStruct((2, L, n), F32),
        compiler_params=_cparams(("arbitrary", "arbitrary")),
        name="hyena_filters",
    )(w1p, b1.reshape(1, -1), w2, b2.reshape(1, -1), w3)


def _dftgen_kernel(f_ref, fs_ref, *, L, tk):
    i = pl.program_id(0)
    N = 2 * L
    k = _iota((tk, LANES), 0) + i * tk
    lane = _iota((tk, LANES), 1)
    w = 2.0 * math.pi / N
    a0 = ((k * lane) & (N - 1)).astype(F32) * w
    c0, s0 = jnp.cos(a0), jnp.sin(a0)
    a1 = ((k * lane * LANES) & (N - 1)).astype(F32) * w
    c1, s1 = jnp.cos(a1), jnp.sin(a1)
    alt = jnp.where((lane & 1) == 0, 1.0, -1.0)
    coef = jnp.where(k == 0, 1.0 / N, 2.0 / N)
    for n1 in range(L // LANES):
        c1b = c1[:, n1:n1 + 1]
        s1b = s1[:, n1:n1 + 1]
        cosb = c1b * c0 - s1b * s0
        sinb = jnp.where(k == 0, alt, -(s1b * c0 + c1b * s0))
        cols = slice(n1 * LANES, (n1 + 1) * LANES)
        f_ref[0:tk, cols] = cosb.astype(BF16)
        f_ref[tk:2 * tk, cols] = sinb.astype(BF16)
        fs_ref[0:tk, cols] = (coef * cosb).astype(BF16)
        fs_ref[tk:2 * tk, cols] = (coef * sinb).astype(BF16)


def _dft_matrices(L):
    tk = min(L, 256)
    spec = pl.BlockSpec((2 * tk, L), lambda i: (i, 0))
    return pl.pallas_call(
        functools.partial(_dftgen_kernel, L=L, tk=tk),
        grid=(L // tk,),
        in_specs=[],
        out_specs=[spec, spec],
        out_shape=[jax.ShapeDtypeStruct((2 * L, L), BF16)] * 2,
        compiler_params=_cparams(("arbitrary",)),
        name="dft_matrices",
    )()


def _dfth_kernel(f_ref, h_ref, o_ref, hp_scr, *, tk):
    i = pl.program_id(1)
    n = o_ref.shape[1]

    @pl.when(i == 0)
    def _():
        hc = h_ref[0]
        ha = jnp.where(_iota(hc.shape, 0) == 0, 0.0, h_ref[1])
        hp_scr[:, 0:n] = (hc + ha).astype(BF16)
        hp_scr[:, n:2 * n] = (hc - ha).astype(BF16)

    u = _dot(f_ref[...], hp_scr[...])
    r = _iota((2 * tk, n), 0)
    from_sum = (r < tk) | ((r == tk) & (i == 0))
    o_ref[...] = jnp.where(from_sum, u[:, 0:n], u[:, n:2 * n])


def _filter_spectrum(fmat, filt):
    L = fmat.shape[1]
    tk = min(L, 256)
    C = filt.shape[2]
    tc = 256
    return pl.pallas_call(
        functools.partial(_dfth_kernel, tk=tk),
        grid=(C // tc, L // tk),
        in_specs=[pl.BlockSpec((2 * tk, L), lambda c, i: (i, 0)),
                  pl.BlockSpec((2, L, tc), lambda c, i: (0, 0, c))],
        out_specs=pl.BlockSpec((2 * tk, tc), lambda c, i: (i, c)),
        out_shape=jax.ShapeDtypeStruct((2 * L, C), F32),
        scratch_shapes=[pltpu.VMEM((L, 2 * tc), BF16)],
        compiler_params=_cparams(("arbitrary", "arbitrary")),
        name="filter_spectrum",
    )(fmat, filt)


def _dfta_kernel(f_ref, u_ref, h_ref, y_ref, u_scr, *, tk):
    i = pl.program_id(1)

    @pl.when(i == 0)
    def _():
        u_scr[...] = u_ref[...].astype(BF16)

    uf = _dot(f_ref[...], u_scr[...])
    ur, ui = uf[0:tk], uf[tk:2 * tk]
    hr, hi = h_ref[0:tk, :], h_ref[tk:2 * tk, :]
    dc = (_iota(ur.shape, 0) == 0) & (i == 0)
    y_ref[0:tk, :] = jnp.where(dc, ur * hr, ur * hr - ui * hi).astype(BF16)
    y_ref[tk:2 * tk, :] = jnp.where(dc, ui * hi, ur * hi + ui * hr).astype(BF16)


def _dftb_kernel(g_ref, y_ref, u_ref, x_ref, b_ref, o_ref):
    y = _dot(g_ref[...], y_ref[...])
    o_ref[...] = x_ref[...] * (y + u_ref[...] * b_ref[...])


def _spectral_conv(fmat, gmat, hspec, h_col, bias, u_arr, u_col, gate_arr, gate_col):
    B, L = u_arr.shape[0], u_arr.shape[1]
    C = D_BRANCH
    tk = min(L, 256)
    y = pl.pallas_call(
        functools.partial(_dfta_kernel, tk=tk),
        grid=(B, L // tk),
        in_specs=[pl.BlockSpec((2 * tk, L), lambda b, i: (i, 0)),
                  pl.BlockSpec((None, L, C), lambda b, i: (b, 0, u_col)),
                  pl.BlockSpec((2 * tk, C), lambda b, i: (i, h_col))],
        out_specs=pl.BlockSpec((None, 2 * tk, C), lambda b, i: (b, i, 0)),
        out_shape=jax.ShapeDtypeStruct((B, 2 * L, C), BF16),
        scratch_shapes=[pltpu.VMEM((L, C), BF16)],
        compiler_params=_cparams(("arbitrary", "arbitrary")),
        name="hyena_dft_fwd",
    )(fmat, u_arr, hspec)
    tr = min(L, 256)
    return pl.pallas_call(
        _dftb_kernel,
        grid=(B, L // tr),
        in_specs=[pl.BlockSpec((tr, 2 * L), lambda b, i: (i, 0)),
                  pl.BlockSpec((None, 2 * L, C), lambda b, i: (b, 0, 0)),
                  pl.BlockSpec((None, tr, C), lambda b, i: (b, i, u_col)),
                  pl.BlockSpec((None, tr, C), lambda b, i: (b, i, gate_col)),
                  pl.BlockSpec((1, C), lambda b, i: (0, 0))],
        out_specs=pl.BlockSpec((None, tr, C), lambda b, i: (b, i, 0)),
        out_shape=jax.ShapeDtypeStruct((B, L, C), F32),
        compiler_params=_cparams(("arbitrary", "arbitrary")),
        name="hyena_dft_inv",
    )(gmat, y, u_arr, gate_arr, bias.reshape(1, C))


def _hgrn_kernel(q_ref, ff_ref, fb_ref, v_ref, g_ref, lb_ref, nrm_ref, s0_ref, y_ref, sf_ref,
                 of_scr, ob_scr, st_scr, *, L, layer, depth):
    C = HG_CHUNK
    W = HG_DK
    nc = L // C
    nh = q_ref.shape[1] // W
    ridx = _iota((C, W), 0)
    o_scrs = (of_scr, ob_scr)
    f_refs = (ff_ref, fb_ref)

    def lower_bound(d, cols):
        rows = [lb_ref[d, l:l + 1, cols] for l in range(depth)]
        m = rows[0]
        for r in rows[1:]:
            m = jnp.maximum(m, r)
        es = [jnp.exp(r - m) for r in rows]
        tot = es[0]
        for e in es[1:]:
            tot = tot + e
        acc = es[0] / tot
        for e in es[1:layer + 1]:
            acc = acc + e / tot
        return acc - es[0] / tot

    lbs = [[lower_bound(d, slice(hh * W, (hh + 1) * W)) for hh in range(nh)] for d in (0, 1)]
    for d in (0, 1):
        for hh in range(nh):
            st_scr[d, hh] = s0_ref[d, hh].T

    def step(c, d, hh):
        rev = d == 1
        rows = pl.ds(pl.multiple_of(c * C, C), C)
        cols = slice(hh * W, (hh + 1) * W)
        lb = lbs[d][hh]
        q = _silu(q_ref[rows, cols])
        uf = f_refs[d][rows, cols]
        v = v_ref[rows, cols]
        f = lb + (1.0 - lb) * _sigmoid(uf)
        g = jnp.log(jnp.maximum(f, LOG_FLOOR))
        kin = (1.0 - lb) * _sigmoid(-uf)
        b = _cumsum_rows(g, rev)
        st = st_scr[d, hh]
        o = _bdot_nt(q * jnp.exp(b), st)
        intra = jnp.zeros((C, W), F32)
        for t in range(C):
            mask = (ridx >= t) if rev else (ridx <= t)
            pair = _masked_exp(mask, b[t:t + 1, :] - b)
            a = jnp.sum(pair * (q[t:t + 1, :] * kin), axis=1, keepdims=True)
            row = jnp.sum(a * v, axis=0, keepdims=True)
            intra = jnp.where(ridx == t, row, intra)
        b_end = b[0:1, :] if rev else b[C - 1:C, :]
        st_scr[d, hh] = st * jnp.exp(b_end) + _bdot_tn(v, kin * jnp.exp(b_end - b))
        o_scrs[d][rows, cols] = o + intra

    def body(ci, carry):
        for d in (0, 1):
            for hh in range(nh):
                step((nc - 1 - ci) if d == 1 else ci, d, hh)
        return carry

    lax.fori_loop(0, nc, body, 0)
    for d in (0, 1):
        for hh in range(nh):
            sf_ref[d, hh] = st_scr[d, hh].T

    tr = min(L, 256)

    def fin(i, carry):
        rows = pl.ds(pl.multiple_of(i * tr, tr), tr)
        for hh in range(nh):
            cols = slice(hh * W, (hh + 1) * W)
            o = of_scr[rows, cols] + ob_scr[rows, cols]
            on = o * lax.rsqrt(jnp.mean(o * o, axis=-1, keepdims=True) + EPS) * nrm_ref[...]
            y_ref[rows, cols] = on * _silu(g_ref[rows, cols])
        return carry

    lax.fori_loop(0, L // tr, fin, 0)


def _hgrn_scan(proj3, hg_lb, hg_norm, s0, *, layer):
    B, L = proj3.shape[0], proj3.shape[1]
    depth = hg_lb.shape[1]
    nh = 2
    W = nh * HG_DK
    col = lambda base: (lambda b, h: (b, 0, base // W + h))
    st_spec = pl.BlockSpec((None, 2, nh, HG_DK, HG_DK), lambda b, h: (b, 0, h, 0, 0))
    return pl.pallas_call(
        functools.partial(_hgrn_kernel, L=L, layer=layer, depth=depth),
        grid=(B, HG_HEADS // nh),
        in_specs=[pl.BlockSpec((None, L, W), col(P_HGQ)),
                  pl.BlockSpec((None, L, W), col(P_HGFF)),
                  pl.BlockSpec((None, L, W), col(P_HGFB)),
                  pl.BlockSpec((None, L, W), col(P_HGV)),
                  pl.BlockSpec((None, L, W), col(P_HGG)),
                  pl.BlockSpec((2, depth, W), lambda b, h: (0, 0, h)),
                  pl.BlockSpec((1, HG_DK), lambda b, h: (0, 0)),
                  st_spec],
        out_specs=[pl.BlockSpec((None, L, W), lambda b, h: (b, 0, h)), st_spec],
        out_shape=[jax.ShapeDtypeStruct((B, L, D_BRANCH), F32),
                   jax.ShapeDtypeStruct(s0.shape, F32)],
        scratch_shapes=[pltpu.VMEM((L, W), F32), pltpu.VMEM((L, W), F32),
                        pltpu.VMEM((2, nh, HG_DK, HG_DK), F32)],
        compiler_params=_cparams(("arbitrary", "arbitrary")),
        name="hgrn2_scan",
    )(proj3, proj3, proj3, proj3, proj3, hg_lb, hg_norm.reshape(1, HG_DK), s0)


def _ssd_kernel(x_ref, bm_ref, cm_ref, z_ref, sm_ref, bias_ref, alog_ref, dskip_ref, s0_ref,
                y_ref, sf_ref, of_scr, ob_scr, st_scr, *, L):
    C = SSD_CHUNK
    W = LANES
    nc = L // C
    npairs = x_ref.shape[1] // W
    grp = pl.program_id(1)
    lane_lo = _iota((C, W), 1) < SSD_P
    row_lo = _iota((W, SSD_N), 0) < SSD_P
    pick2 = lambda a0, a1: jnp.where(lane_lo, a0, a1)
    o_scrs = (of_scr, ob_scr)
    incls = (_tri_mask(C, False), _tri_mask(C, True))
    for d in (0, 1):
        for pp in range(npairs):
            st_scr[d, pp] = s0_ref[d, pp]

    heads = [(d, pp, hh) for d in (0, 1) for pp in range(npairs) for hh in (0, 1)]
    pairs = [(d, pp) for d in (0, 1) for pp in range(npairs)]

    def body(ci, carry):
        rows_d, bm, cm, dt_all, cum_all, gram = [], [], [], [], [], []
        for d in (0, 1):
            c = (nc - 1 - ci) if d == 1 else ci
            rows = pl.ds(pl.multiple_of(c * C, C), C)
            rows_d.append(rows)
            bm.append(_silu(bm_ref[rows, :]))
            cm.append(_silu(cm_ref[rows, :]))
            dt_all.append(_softplus(sm_ref[rows, :] + bias_ref[...]))
            cum_all.append(_cumsum_rows(-jnp.exp(alog_ref[...]) * dt_all[d], d == 1))
        lane_of = lambda u: S_DT + u[0] * SSD_HEADS + 2 * (npairs * grp + u[1]) + u[2]
        cumr = {u: _row_pick(cum_all[u[0]], lane_of(u)) for u in heads}
        for d in (0, 1):
            gram.append(_bdot_nt(cm[d], bm[d]))
        dtc = {u: _lane_pick(dt_all[u[0]], lane_of(u)) for u in heads}
        cumc = {u: _lane_pick(cum_all[u[0]], lane_of(u)) for u in heads}
        end = {u: (cumc[u][0:1, :] if u[0] == 1 else cumc[u][C - 1:C, :]) for u in heads}
        xs = {pr: _silu(x_ref[rows_d[pr[0]], slice(pr[1] * W, (pr[1] + 1) * W)]) for pr in pairs}
        st = {pr: st_scr[pr[0], pr[1]] for pr in pairs}
        y_state = {u: _bdot_nt(cm[u[0]] * jnp.exp(cumc[u]), st[u[:2]]) for u in heads}
        scores = {u: gram[u[0]] * _masked_exp(incls[u[0]], cumc[u] - cumr[u]) for u in heads}
        y_intra = {}
        for u in heads:
            head_x = jnp.where(lane_lo if u[2] == 0 else ~lane_lo, xs[u[:2]], 0.0) * dtc[u]
            y_intra[u] = _bdot(scores[u], head_x)
        for pr in pairs:
            u0, u1 = pr + (0,), pr + (1,)
            xdt_e = xs[pr] * pick2(dtc[u0] * jnp.exp(end[u0] - cumc[u0]), dtc[u1] * jnp.exp(end[u1] - cumc[u1]))
            st_scr[pr[0], pr[1]] = (st[pr] * jnp.where(row_lo, jnp.exp(end[u0]), jnp.exp(end[u1]))
                                    + _bdot_tn(xdt_e, bm[pr[0]]))
            o_scrs[pr[0]][rows_d[pr[0]], slice(pr[1] * W, (pr[1] + 1) * W)] = pick2(
                y_intra[u0] + y_state[u0], y_intra[u1] + y_state[u1])
        return carry

    lax.fori_loop(0, nc, body, 0)
    for d in (0, 1):
        for pp in range(npairs):
            sf_ref[d, pp] = st_scr[d, pp]

    tr = min(L, 256)

    def fin(i, carry):
        rows = pl.ds(pl.multiple_of(i * tr, tr), tr)
        y = of_scr[rows, :] + ob_scr[rows, :] + _silu(x_ref[rows, :]) * dskip_ref[...]
        y_ref[rows, :] = y * _silu(z_ref[rows, :])
        return carry

    lax.fori_loop(0, L // tr, fin, 0)


def _ssd_scan(proj3, bias_row, alog_row, ssd_d, s0):
    B, L = proj3.shape[0], proj3.shape[1]
    n_groups = 2
    n_pairs = SSD_HEADS // 2
    ppg = n_pairs // n_groups
    W = ppg * LANES
    s0p = s0.reshape(B, 2, n_pairs, 2 * SSD_P, SSD_N)
    dskip = jnp.repeat(ssd_d, SSD_P).reshape(n_groups, 1, W)
    col = lambda base: (lambda b, g: (b, 0, base // W + g))
    grp = lambda base: (lambda b, g: (b, 0, base // LANES + g))
    st_spec = pl.BlockSpec((None, 2, ppg, LANES, SSD_N), lambda b, g: (b, 0, g, 0, 0))
    y, sf = pl.pallas_call(
        functools.partial(_ssd_kernel, L=L),
        grid=(B, n_groups),
        in_specs=[pl.BlockSpec((None, L, W), col(C_SSDX)),
                  pl.BlockSpec((None, L, LANES), grp(C_SSDB)),
                  pl.BlockSpec((None, L, LANES), grp(C_SSDC)),
                  pl.BlockSpec((None, L, W), col(P_SSDZ)),
                  pl.BlockSpec((None, L, LANES), lambda b, g: (b, 0, P_SMALL // LANES)),
                  pl.BlockSpec((1, LANES), lambda b, g: (0, 0)),
                  pl.BlockSpec((1, LANES), lambda b, g: (0, 0)),
                  pl.BlockSpec((None, 1, W), lambda b, g: (g, 0, 0)),
                  st_spec],
        out_specs=[pl.BlockSpec((None, L, W), lambda b, g: (b, 0, g)), st_spec],
        out_shape=[jax.ShapeDtypeStruct((B, L, D_BRANCH), F32),
                   jax.ShapeDtypeStruct(s0p.shape, F32)],
        scratch_shapes=[pltpu.VMEM((L, W), F32), pltpu.VMEM((L, W), F32),
                        pltpu.VMEM((2, ppg, LANES, SSD_N), F32)],
        compiler_params=_cparams(("arbitrary", "arbitrary")),
        name="ssd_scan",
    )(proj3, proj3, proj3, proj3, proj3, bias_row, alog_row, dskip, s0p)
    return y, sf.reshape(s0.shape)


def _l2norm(a):
    return a * lax.rsqrt(jnp.sum(a * a, axis=-1, keepdims=True) + EPS)


def _gdn_kernel(q_ref, k_ref, v_ref, g_ref, sm_ref, bias_ref, alog_ref, nrm_ref, s0_ref,
                y_ref, sf_ref, of_scr, ob_scr, st_scr, *, L):
    C = GDN_CHUNK
    W = GDN_DK
    nc = L // C
    nh = q_ref.shape[1] // W
    hblk = pl.program_id(1)
    o_scrs = (of_scr, ob_scr)
    incls = (_tri_mask(C, False), _tri_mask(C, True))
    stricts = (_tri_mask(C, False, strict=True), _tri_mask(C, True, strict=True))
    for d in (0, 1):
        for hh in range(nh):
            st_scr[d, hh] = s0_ref[d, hh]

    units = [(d, hh) for d in (0, 1) for hh in range(nh)]
    eye = (_iota((C, C), 0) == _iota((C, C), 1)).astype(F32)

    def body(ci, carry):
        rows_d, cum_d, beta_d = [], [], []
        for d in (0, 1):
            c = (nc - 1 - ci) if d == 1 else ci
            rows = pl.ds(pl.multiple_of(c * C, C), C)
            raw = sm_ref[rows, :]
            rows_d.append(rows)
            cum_d.append(_cumsum_rows(-jnp.exp(alog_ref[...]) * _softplus(raw + bias_ref[...]), d == 1))
            beta_d.append(_sigmoid(raw))
        q, k, v, gc, grow, beta = {}, {}, {}, {}, {}, {}
        for u in units:
            d, hh = u
            cols = slice(hh * W, (hh + 1) * W)
            h = nh * hblk + hh
            q[u] = _l2norm(_silu(q_ref[rows_d[d], cols])) * (GDN_DK ** -0.5)
            k[u] = _l2norm(_silu(k_ref[rows_d[d], cols]))
            v[u] = _silu(v_ref[rows_d[d], cols])
            gc[u] = _lane_pick(cum_d[d], S_GA + d * GDN_HEADS + h)
            beta[u] = _lane_pick(beta_d[d], S_GB + d * GDN_HEADS + h)
        for u in units:
            grow[u] = _row_pick(cum_d[u[0]], S_GA + u[0] * GDN_HEADS + nh * hblk + u[1])
        decay = {u: _masked_exp(incls[u[0]], gc[u] - grow[u]) for u in units}
        kb = {u: k[u] * beta[u] for u in units}
        m = {u: -jnp.where(stricts[u[0]], _bdot_nt(kb[u], k[u]) * decay[u], 0.0) for u in units}
        aqk = {u: _bdot_nt(q[u], k[u]) * decay[u] for u in units}
        p = {u: eye + m[u] for u in units}
        sh = 2
        while sh < C:
            m = {u: _dot3(m[u], m[u]) for u in units}
            p = {u: p[u] + _dot3(p[u], m[u]) for u in units}
            sh *= 2
        uu = {u: _dot3(p[u], v[u] * beta[u]) for u in units}
        ww = {u: _dot3(p[u], kb[u] * jnp.exp(gc[u])) for u in units}
        st = {u: st_scr[u[0], u[1]] for u in units}
        v_new = {u: uu[u] - _bdot(ww[u], st[u]) for u in units}
        o = {u: _bdot(q[u] * jnp.exp(gc[u]), st[u]) + _bdot(aqk[u], v_new[u]) for u in units}
        for u in units:
            d, hh = u
            g_end = gc[u][0:1, :] if d == 1 else gc[u][C - 1:C, :]
            st_scr[d, hh] = st[u] * jnp.exp(g_end) + _bdot_tn(k[u] * jnp.exp(g_end - gc[u]), v_new[u])
            o_scrs[d][rows_d[d], slice(hh * W, (hh + 1) * W)] = o[u]
        return carry

    lax.fori_loop(0, nc, body, 0)
    for d in (0, 1):
        for hh in range(nh):
            sf_ref[d, hh] = st_scr[d, hh]

    tr = min(L, 256)

    def fin(i, carry):
        rows = pl.ds(pl.multiple_of(i * tr, tr), tr)
        for hh in range(nh):
            cols = slice(hh * W, (hh + 1) * W)
            o = of_scr[rows, cols] + ob_scr[rows, cols]
            on = o * lax.rsqrt(jnp.mean(o * o, axis=-1, keepdims=True) + EPS) * nrm_ref[...]
            y_ref[rows, cols] = on * _silu(g_ref[rows, cols])
        return carry

    lax.fori_loop(0, L // tr, fin, 0)


def _gdn_scan(proj3, bias_row, alog_row, gdn_norm, s0):
    B, L = proj3.shape[0], proj3.shape[1]
    nh = GDN_HEADS
    W = nh * GDN_DK
    col = lambda base: (lambda b, h: (b, 0, base // W + h))
    st_spec = pl.BlockSpec((None, 2, nh, GDN_DK, GDN_DK), lambda b, h: (b, 0, h, 0, 0))
    once = pl.Buffered(1)
    return pl.pallas_call(
        functools.partial(_gdn_kernel, L=L),
        grid=(B, GDN_HEADS // nh),
        in_specs=[pl.BlockSpec((None, L, W), col(C_GQ), pipeline_mode=once),
                  pl.BlockSpec((None, L, W), col(C_GK), pipeline_mode=once),
                  pl.BlockSpec((None, L, W), col(C_GV), pipeline_mode=once),
                  pl.BlockSpec((None, L, W), col(P_GDNG), pipeline_mode=once),
                  pl.BlockSpec((None, L, LANES), lambda b, h: (b, 0, P_SMALL // LANES), pipeline_mode=once),
                  pl.BlockSpec((1, LANES), lambda b, h: (0, 0)),
                  pl.BlockSpec((1, LANES), lambda b, h: (0, 0)),
                  pl.BlockSpec((1, GDN_DK), lambda b, h: (0, 0)),
                  st_spec],
        out_specs=[pl.BlockSpec((None, L, W), lambda b, h: (b, 0, h)), st_spec],
        out_shape=[jax.ShapeDtypeStruct((B, L, D_BRANCH), F32),
                   jax.ShapeDtypeStruct(s0.shape, F32)],
        scratch_shapes=[pltpu.VMEM((L, W), F32), pltpu.VMEM((L, W), F32),
                        pltpu.VMEM((2, nh, GDN_DK, GDN_DK), F32)],
        compiler_params=_cparams(("arbitrary", "arbitrary")),
        name="gdn_scan",
    )(proj3, proj3, proj3, proj3, proj3, bias_row, alog_row, gdn_norm.reshape(1, GDN_DK), s0)


def _merge_kernel(ya_ref, yb_ref, yc_ref, yd_ref, mg_ref, x_ref, mod_ref, sn_ref, n2_ref, wb_ref, wo_ref,
                  xo_ref, h_ref):
    yc = yc_ref[...]
    half = D_BRANCH // 2
    parts = []
    for gidx in range(2):
        seg = yc[:, gidx * half:(gidx + 1) * half]
        parts.append(seg * lax.rsqrt(jnp.mean(seg * seg, axis=-1, keepdims=True) + EPS)
                     * sn_ref[:, gidx * half:(gidx + 1) * half])
    branches = (ya_ref[...], yb_ref[...], None, yd_ref[...])
    mixed = None
    for n in range(N_BRANCH):
        gate = _sigmoid(mg_ref[:, n * D_MODEL:(n + 1) * D_MODEL])
        if n == 2:
            lifted = (_dot(parts[0].astype(BF16), wb_ref[n, 0:half, :])
                      + _dot(parts[1].astype(BF16), wb_ref[n, half:D_BRANCH, :]))
        else:
            lifted = _dot(branches[n].astype(BF16), wb_ref[n])
        mixed = gate * lifted if mixed is None else mixed + gate * lifted
    m = mod_ref[...]
    x = x_ref[...] + m[2:3] * _dot(mixed.astype(BF16), wo_ref[...])
    xo_ref[...] = x
    xn = x * lax.rsqrt(jnp.mean(x * x, axis=-1, keepdims=True) + EPS) * n2_ref[...]
    h_ref[...] = xn * (1.0 + m[4:5]) + m[3:4]


def _merge(ya, yb, yc, yd, proj, x, mod, ssd_norm, norm2_g, wb_bf16, wo_bf16, *, tokens_per_mod):
    T = x.shape[0]
    tm = 256
    tiles_per_mod = tokens_per_mod // tm
    yspec = pl.BlockSpec((tm, D_BRANCH), lambda i: (i, 0))
    xspec = pl.BlockSpec((tm, D_MODEL), lambda i: (i, 0))
    return pl.pallas_call(
        _merge_kernel,
        grid=(T // tm,),
        in_specs=[yspec, yspec, yspec, yspec,
                  pl.BlockSpec((tm, N_BRANCH * D_MODEL), lambda i: (i, P_MERGE // (N_BRANCH * D_MODEL))),
                  xspec,
                  pl.BlockSpec((None, 6, D_MODEL), lambda i: (i // tiles_per_mod, 0, 0)),
                  pl.BlockSpec((1, D_BRANCH), lambda i: (0, 0)),
                  pl.BlockSpec((1, D_MODEL), lambda i: (0, 0)),
                  pl.BlockSpec((N_BRANCH, D_BRANCH, D_MODEL), lambda i: (0, 0, 0)),
                  pl.BlockSpec((D_MODEL, D_MODEL), lambda i: (0, 0))],
        out_specs=[xspec, xspec],
        out_shape=[jax.ShapeDtypeStruct((T, D_MODEL), F32)] * 2,
        compiler_params=_cparams(("arbitrary",)),
        name="merge_outproj",
    )(ya, yb, yc, yd, proj, x, mod, ssd_norm.reshape(1, D_BRANCH), norm2_g.reshape(1, D_MODEL), wb_bf16, wo_bf16)


def _router_kernel(h_ref, rw_ref, rb_ref, idx_ref, wt_ref, rank_ref, cnt_ref, carry):
    i = pl.program_id(0)
    tm = h_ref.shape[0]
    E = N_EXPERTS
    gsz = E // N_EXPERT_GROUPS
    neg_inf = -jnp.inf

    @pl.when(i == 0)
    def _():
        carry[...] = jnp.zeros_like(carry)

    scores = _sigmoid(_dot_nt(rw_ref[...], h_ref[...], precision=HIGHEST))
    biased = scores + rb_ref[...]
    eidx = _iota((E, tm), 0)
    ridx = _iota((gsz, tm), 0)

    slabs = [biased[g * gsz:(g + 1) * gsz, :] for g in range(N_EXPERT_GROUPS)]
    gs = []
    for v in slabs:
        m1 = jnp.max(v, axis=0, keepdims=True)
        i1 = jnp.min(jnp.where(v == m1, ridx, gsz), axis=0, keepdims=True)
        m2 = jnp.max(jnp.where(ridx == i1, neg_inf, v), axis=0, keepdims=True)
        gs.append(m1 + m2)
    masked = []
    for g in range(N_EXPERT_GROUPS):
        rank = jnp.zeros((1, tm), I32)
        for g2 in range(N_EXPERT_GROUPS):
            if g2 == g:
                continue
            ahead = (gs[g2] >= gs[g]) if g2 < g else (gs[g2] > gs[g])
            rank = rank + ahead.astype(I32)
        masked.append(jnp.where(rank < TOPK_GROUPS, slabs[g], MASK_NEG))
    cur = jnp.concatenate(masked, axis=0)

    krow = _iota((TOP_K, tm), 0)
    idx_out = jnp.zeros((TOP_K, tm), I32)
    onehot = jnp.zeros((E, tm), F32)
    picks = []
    wsum = jnp.zeros((1, tm), F32)
    for kk in range(TOP_K):
        m = jnp.max(cur, axis=0, keepdims=True)
        ik = jnp.min(jnp.where(cur == m, eidx, E), axis=0, keepdims=True)
        hit = eidx == ik
        wk = jnp.sum(jnp.where(hit, scores, 0.0), axis=0, keepdims=True)
        cur = jnp.where(hit, neg_inf, cur)
        onehot = onehot + hit.astype(F32)
        picks.append((ik, wk))
        wsum = wsum + wk
        idx_out = jnp.where(krow == kk, ik, idx_out)
    wt_out = jnp.zeros((TOP_K, tm), F32)
    for kk, (ik, wk) in enumerate(picks):
        wt_out = jnp.where(krow == kk, wk / wsum * ROUTED_SCALE, wt_out)

    earlier = (_iota((tm, tm), 0) < _iota((tm, tm), 1)).astype(BF16)
    before = _dot(onehot.astype(BF16), earlier) + carry[:, 0:1]
    rank_out = jnp.zeros((TOP_K, tm), I32)
    for kk, (ik, wk) in enumerate(picks):
        rk = jnp.sum(jnp.where(eidx == ik, before, 0.0), axis=0, keepdims=True)
        rank_out = jnp.where(krow == kk, rk.astype(I32), rank_out)
    carry[...] = carry[...] + jnp.sum(onehot, axis=1, keepdims=True)

    idx_ref[...] = idx_out
    wt_ref[...] = wt_out
    rank_ref[...] = rank_out
    cnt_ref[...] = carry[...]


def _router(h2, router_w, router_b):
    T = h2.shape[0]
    tm = min(512, T)
    ospec = pl.BlockSpec((TOP_K, tm), lambda i: (0, i))
    cspec = pl.BlockSpec((N_EXPERTS, LANES), lambda i: (0, 0))
    idx, wt, rank, cnt = pl.pallas_call(
        _router_kernel,
        grid=(T // tm,),
        in_specs=[pl.BlockSpec((tm, D_MODEL), lambda i: (i, 0)),
                  pl.BlockSpec((N_EXPERTS, D_MODEL), lambda i: (0, 0)),
                  pl.BlockSpec((N_EXPERTS, 1), lambda i: (0, 0))],
        out_specs=[ospec, ospec, ospec, cspec],
        out_shape=[jax.ShapeDtypeStruct((TOP_K, T), I32), jax.ShapeDtypeStruct((TOP_K, T), F32),
                   jax.ShapeDtypeStruct((TOP_K, T), I32), jax.ShapeDtypeStruct((N_EXPERTS, LANES), F32)],
        scratch_shapes=[pltpu.VMEM((N_EXPERTS, LANES), F32)],
        compiler_params=_cparams(("arbitrary",)),
        name="moe_router",
    )(h2, router_w.T, router_b.reshape(N_EXPERTS, 1))
    return idx, wt, rank, cnt[:, 0].astype(I32)


def _dispatch_kernel(dest_ref, h_ref, xs_ref, sem):
    tm = h_ref.shape[0]

    def row_copies(t):
        src = h_ref.at[pl.ds(t, 1), :]
        return [pltpu.make_async_copy(src, xs_ref.at[pl.ds(dest_ref[t * TOP_K + kk], 1), :], sem)
                for kk in range(TOP_K)]

    def start(t, carry):
        for cp in row_copies(t):
            cp.start()
        return carry

    def wait(t, carry):
        for cp in row_copies(t):
            cp.wait()
        return carry

    lax.fori_loop(0, tm, start, 0)
    lax.fori_loop(0, tm, wait, 0)


def _dispatch(h2, dest_flat):
    T = h2.shape[0]
    tm = 256
    return pl.pallas_call(
        _dispatch_kernel,
        grid=(T // tm,),
        in_specs=[pl.BlockSpec((tm * TOP_K,), lambda i: (i,), memory_space=pltpu.SMEM),
                  pl.BlockSpec((tm, D_MODEL), lambda i: (i, 0))],
        out_specs=pl.BlockSpec(memory_space=pl.ANY),
        out_shape=jax.ShapeDtypeStruct((T * TOP_K, D_MODEL), F32),
        scratch_shapes=[pltpu.SemaphoreType.DMA(())],
        compiler_params=_cparams(("arbitrary",)),
        name="moe_dispatch",
    )(dest_flat, h2)


def _expert_kernel(blk_ref, exp_ref, lo_ref, hi_ref, first_ref, x_ref, w1_ref, w3_ref, w2_ref, o_ref):
    w = pl.program_id(0)
    x = x_ref[...].astype(BF16)
    a = _dot(x, w1_ref[...].astype(BF16))
    b = _dot(x, w3_ref[...].astype(BF16))
    y = _dot((_silu(a) * b).astype(BF16), w2_ref[...].astype(BF16))
    r = _iota(y.shape, 0)
    y = jnp.where((r >= lo_ref[w]) & (r < hi_ref[w]), y, 0.0)

    @pl.when(first_ref[w] == 1)
    def _():
        o_ref[...] = y

    @pl.when(first_ref[w] == 0)
    def _():
        o_ref[...] = o_ref[...] + y


def _experts(xs, work, w1, w3, w2, *, bm):
    A = xs.shape[0]
    n_work = work[0].shape[0]
    xmap = lambda w, blk, ex, lo, hi, first: (blk[w], 0)
    wmap = lambda w, blk, ex, lo, hi, first: (ex[w], 0, 0)
    return pl.pallas_call(
        _expert_kernel,
        grid_spec=pltpu.PrefetchScalarGridSpec(
            num_scalar_prefetch=5,
            grid=(n_work,),
            in_specs=[pl.BlockSpec((bm, D_MODEL), xmap),
                      pl.BlockSpec((None, D_MODEL, D_EXPERT), wmap),
                      pl.BlockSpec((None, D_MODEL, D_EXPERT), wmap),
                      pl.BlockSpec((None, D_EXPERT, D_MODEL), wmap)],
            out_specs=pl.BlockSpec((bm, D_MODEL), xmap)),
        out_shape=jax.ShapeDtypeStruct((A, D_MODEL), F32),
        compiler_params=_cparams(("arbitrary",)),
        name="moe_experts",
    )(*work, xs, w1, w3, w2)


def _combine_kernel(dest_ref, ys_ref, wt_ref, h_ref, x_ref, mod_ref, s1_ref, s3_ref, s2_ref, fg_ref,
                    o_ref, buf, sem, *, final_norm):
    tc = h_ref.shape[0]

    def row_copies(t):
        return [pltpu.make_async_copy(ys_ref.at[pl.ds(dest_ref[t * TOP_K + kk], 1), :],
                                      buf.at[kk, pl.ds(t, 1), :], sem)
                for kk in range(TOP_K)]

    def start(t, carry):
        for cp in row_copies(t):
            cp.start()
        return carry

    def wait(t, carry):
        for cp in row_copies(t):
            cp.wait()
        return carry

    lax.fori_loop(0, tc, start, 0)
    hb = h_ref[...].astype(BF16)
    mid = _silu(_dot(hb, s1_ref[...])) * _dot(hb, s3_ref[...])
    y = _dot(mid.astype(BF16), s2_ref[...])
    lax.fori_loop(0, tc, wait, 0)
    for kk in range(TOP_K):
        y = y + wt_ref[:, kk:kk + 1] * buf[kk]
    x = x_ref[...] + mod_ref[5:6, :] * y
    if final_norm:
        x = x * lax.rsqrt(jnp.mean(x * x, axis=-1, keepdims=True) + EPS) * fg_ref[...]
    o_ref[...] = x


def _combine(ys, dest_flat, wt, h2, x, mod, s1, s3, s2, final_g, *, tokens_per_mod, final_norm):
    T = x.shape[0]
    tc = 128
    tiles_per_mod = tokens_per_mod // tc
    xspec = pl.BlockSpec((tc, D_MODEL), lambda i: (i, 0))
    return pl.pallas_call(
        functools.partial(_combine_kernel, final_norm=final_norm),
        grid=(T // tc,),
        in_specs=[pl.BlockSpec((tc * TOP_K,), lambda i: (i,), memory_space=pltpu.SMEM),
                  pl.BlockSpec(memory_space=pl.ANY),
                  pl.BlockSpec((tc, LANES), lambda i: (i, 0)),
                  xspec, xspec,
                  pl.BlockSpec((None, 6, D_MODEL), lambda i: (i // tiles_per_mod, 0, 0)),
                  pl.BlockSpec((D_MODEL, D_SHARED), lambda i: (0, 0)),
                  pl.BlockSpec((D_MODEL, D_SHARED), lambda i: (0, 0)),
                  pl.BlockSpec((D_SHARED, D_MODEL), lambda i: (0, 0)),
                  pl.BlockSpec((1, D_MODEL), lambda i: (0, 0))],
        out_specs=xspec,
        out_shape=jax.ShapeDtypeStruct((T, D_MODEL), F32),
        scratch_shapes=[pltpu.VMEM((TOP_K, tc, D_MODEL), F32), pltpu.SemaphoreType.DMA(())],
        compiler_params=_cparams(("arbitrary",)),
        name="moe_combine",
    )(dest_flat, ys, wt, h2, x, mod, s1, s3, s2, final_g.reshape(1, D_MODEL))


def _work_list(counts, n_rows, bm):
    E = N_EXPERTS
    n_blocks = n_rows // bm
    n_work = n_blocks + E - 1
    ends = jnp.cumsum(counts)
    starts = ends - counts
    first_blk = starts // bm
    last_blk = jnp.where(counts > 0, (ends - 1) // bm, first_blk)
    n_items = jnp.where(counts > 0, last_blk - first_blk + 1, 0)
    item_end = jnp.cumsum(n_items)
    item_start = item_end - n_items
    w = jnp.arange(n_work, dtype=I32)
    ex = jnp.minimum(jnp.searchsorted(item_end, w, side="right"), E - 1).astype(I32)
    valid = w < item_end[-1]
    blk = jnp.where(valid, first_blk[ex] + (w - item_start[ex]), n_blocks - 1).astype(I32)
    lo = jnp.clip(starts[ex] - blk * bm, 0, bm)
    hi = jnp.clip(ends[ex] - blk * bm, 0, bm)
    lo = jnp.where(valid, lo, 0).astype(I32)
    hi = jnp.where(valid, hi, 0).astype(I32)
    ex = jnp.where(valid, ex, ex[jnp.maximum(item_end[-1] - 1, 0)])
    prev_blk = jnp.concatenate([jnp.full((1,), -1, I32), blk[:-1]])
    first = (blk != prev_blk).astype(I32)
    return blk, ex.astype(I32), lo, hi, first


def _moe(h2, x, mod, p, final_g, *, tokens_per_mod, final_norm):
    T = x.shape[0]
    idx, wt, rank, counts = _router(h2, p["router_w"], p["router_b"])
    starts = jnp.cumsum(counts) - counts
    dest = (starts[idx] + rank).T.reshape(T * TOP_K)
    wt = jnp.pad(wt.T, ((0, 0), (0, LANES - TOP_K)))
    xs = _dispatch(h2, dest)
    bm = 512
    work = _work_list(counts, T * TOP_K, bm)
    ys = _experts(xs, work, p["exp_w1"], p["exp_w3"], p["exp_w2"], bm=bm)
    return _combine(ys, dest, wt, h2, x, mod, p["sh_w1"], p["sh_w3"], p["sh_w2"], final_g,
                    tokens_per_mod=tokens_per_mod, final_norm=final_norm)


def _reorder_w_in(w_in):
    plain = D_CONV
    pieces = [w_in[..., :plain],
              w_in[..., plain + 3616:plain + 3616 + 4096],
              w_in[..., plain:plain + 3072],
              w_in[..., plain + 3088:plain + 3600],
              w_in[..., plain + 3072:plain + 3088],
              w_in[..., plain + 3600:plain + 3616]]
    out = jnp.concatenate(pieces, axis=-1)
    pad = D_PROJ - out.shape[-1]
    return jnp.pad(out, ((0, 0), (0, 0), (0, pad))).astype(BF16)


def _small_rows(ssd_vals, gdn_vals):
    row = jnp.zeros((LANES,), F32)
    row = row.at[S_DT:S_DT + 2 * SSD_HEADS].set(ssd_vals.reshape(-1))
    row = row.at[S_GA:S_GA + 2 * GDN_HEADS].set(gdn_vals.reshape(-1))
    return row.reshape(1, LANES)


def _layer_pass(x, mod, lp, hy, st_hg, st_ssd, st_gdn, final_g, *, B, L, tokens_per_mod, seg, layer, final_norm):
    T = B * L
    proj = _in_projection(x, mod, lp["norm1_g"], lp["w_in"], lp["conv_w"], lp["conv_b"],
                          tokens_per_mod=tokens_per_mod, seg=seg)
    proj3 = proj.reshape(B, L, D_PROJ)
    fmat, gmat, hspec = hy
    cb = lambda c: c // D_BRANCH
    z = _spectral_conv(fmat, gmat, hspec, 0, lp["hy_bias"][0], proj3, cb(C_HYV), proj3, cb(C_HYX1))
    ya = _spectral_conv(fmat, gmat, hspec, 1, lp["hy_bias"][1], z, 0, proj3, cb(C_HYX2))
    yb, s_hg = _hgrn_scan(proj3, lp["hg_lb"], lp["hg_norm"], st_hg, layer=layer)
    yc, s_ssd = _ssd_scan(proj3, lp["bias_row"], lp["alog_row"], lp["ssd_d"], st_ssd)
    yd, s_gdn = _gdn_scan(proj3, lp["bias_row"], lp["alog_row"], lp["gdn_norm"], st_gdn)
    flat = lambda a: a.reshape(T, D_BRANCH)
    x, h2 = _merge(flat(ya), flat(yb), flat(yc), flat(yd), proj, x, mod, lp["ssd_norm"], lp["norm2_g"],
                   lp["w_branch"], lp["w_out"], tokens_per_mod=tokens_per_mod)
    x = _moe(h2, x, mod, lp, final_g, tokens_per_mod=tokens_per_mod, final_norm=final_norm)
    return x, s_hg, s_ssd, s_gdn


def kernel(x_prompt, x_sample, state_hgrn, state_ssd, state_gdn, c, c_ctx, norm1_g, norm2_g, ada_w, ada_b, w_in, conv_w, conv_b, hy_w1, hy_b1, hy_w2, hy_b2, hy_w3, hy_bias, hg_lb, hg_norm, ssd_a_log, ssd_dt_bias, ssd_d, ssd_norm, gdn_a_log, gdn_dt_bias, gdn_norm, w_branch, w_out, router_w, router_bias, exp_w1, exp_w3, exp_w2, sh_w1, sh_w3, sh_w2, final_g):
    depth = w_in.shape[0]
    bp, lp_len = x_prompt.shape[0], x_prompt.shape[1]
    bs, ls_len = x_sample.shape[0], x_sample.shape[1]
    D = D_MODEL

    w_in_r = _reorder_w_in(w_in)
    hy_w1p = jnp.pad(hy_w1, ((0, 0), (0, LANES - hy_w1.shape[1]), (0, 0)))
    layers = []
    for l in range(depth):
        layers.append(dict(
            norm1_g=norm1_g[l], norm2_g=norm2_g[l], w_in=w_in_r[l], conv_w=conv_w[l], conv_b=conv_b[l],
            hy_bias=hy_bias[l], hg_lb=hg_lb, hg_norm=hg_norm[l],
            bias_row=_small_rows(ssd_dt_bias[l], gdn_dt_bias[l]),
            alog_row=_small_rows(ssd_a_log[l], gdn_a_log[l]),
            ssd_d=ssd_d[l], ssd_norm=ssd_norm[l], gdn_norm=gdn_norm[l],
            w_branch=w_branch[l].astype(BF16), w_out=w_out[l].astype(BF16),
            router_w=router_w[l], router_b=router_bias[l],
            exp_w1=exp_w1[l], exp_w3=exp_w3[l], exp_w2=exp_w2[l],
            sh_w1=sh_w1[l].astype(BF16), sh_w3=sh_w3[l].astype(BF16), sh_w2=sh_w2[l].astype(BF16)))

    def hyena_setup(L):
        fmat, fs = _dft_matrices(L)
        gmat = fs.T
        specs = []
        for l in range(depth):
            filt = _hyena_filters(L, hy_w1p[l], hy_b1[l], hy_w2[l], hy_b2[l], hy_w3[l])
            specs.append(_filter_spectrum(fmat, filt))
        return fmat, gmat, specs

    cond = jnp.concatenate([c_ctx.reshape(1, D), c], axis=0)
    rows = cond.shape[0]
    rows8 = (rows + 7) // 8 * 8
    cond8 = jnp.pad(cond, ((0, rows8 - rows), (0, 0)))
    mods = [_modulation(cond8, ada_w[l], ada_b[l]).reshape(rows8, 6, D) for l in range(depth)]

    fmat, gmat, specs = hyena_setup(lp_len)
    x = x_prompt.reshape(bp * lp_len, D)
    z_hg = jnp.zeros((bp, 2, HG_HEADS, HG_DK, HG_DK), F32)
    z_ssd = jnp.zeros((bp, 2, SSD_HEADS, SSD_P, SSD_N), F32)
    z_gdn = jnp.zeros((bp, 2, GDN_HEADS, GDN_DK, GDN_DK), F32)
    hg_states, ssd_states, gdn_states = [], [], []
    for l in range(depth):
        x, s_hg, s_ssd, s_gdn = _layer_pass(
            x, mods[l][0:1], layers[l], (fmat, gmat, specs[l]), z_hg, z_ssd, z_gdn, final_g,
            B=bp, L=lp_len, tokens_per_mod=bp * lp_len, seg=lp_len, layer=l, final_norm=(l == depth - 1))
        hg_states.append(s_hg)
        ssd_states.append(s_ssd)
        gdn_states.append(s_gdn)
    y_prompt = x.reshape(bp, lp_len, D)
    new_hg = jnp.stack(hg_states, axis=1)
    new_ssd = jnp.stack(ssd_states, axis=1)
    new_gdn = jnp.stack(gdn_states, axis=1)

    fmat, gmat, specs = hyena_setup(ls_len)
    x = x_sample.reshape(bs * ls_len, D)
    for l in range(depth):
        x, _, _, _ = _layer_pass(
            x, mods[l][1:1 + bs], layers[l], (fmat, gmat, specs[l]),
            state_hgrn[:, l], state_ssd[:, l], state_gdn[:, l], final_g,
            B=bs, L=ls_len, tokens_per_mod=ls_len, seg=GRID_W, layer=l, final_norm=(l == depth - 1))
    y_sample = x.reshape(bs, ls_len, D)
    return (y_prompt, y_sample, new_hg, new_ssd, new_gdn)
```

```python
import functools
import math

import numpy as np
import jax
import jax.numpy as jnp
from jax import lax
from jax.experimental import pallas as pl
from jax.experimental.pallas import tpu as pltpu

F32 = jnp.float32
BF16 = jnp.bfloat16
I32 = jnp.int32
HIGHEST = lax.Precision.HIGHEST

D_MODEL = 1024
GRID_W = 64
EPS = 1e-6
LOG_FLOOR = 1e-30
MASK_NEG = -1e30
N_BRANCH = 4
D_BRANCH = 512
HY_POS_FREQS = 16
HY_FILTER_HIDDEN = 64
HY_FAST_DECAY = 0.3
HY_SLOW_DECAY = 1.5
HY_DECAY_TARGET = 1e-2
HG_HEADS = 4
HG_DK = 128
HG_CHUNK = 16
SSD_HEADS = 8
SSD_P = 64
SSD_N = 128
SSD_CHUNK = 64
GDN_HEADS = 4
GDN_DK = 128
GDN_CHUNK = 64
N_EXPERTS = 64
TOP_K = 8
N_EXPERT_GROUPS = 8
TOPK_GROUPS = 4
D_EXPERT = 256
D_SHARED = 256
ROUTED_SCALE = 2.5

LANES = 128

D_CONV = 4096
C_HYV, C_HYX1, C_HYX2 = 0, 512, 1024
C_SSDX, C_SSDB, C_SSDC = 1536, 2048, 2304
C_GQ, C_GK, C_GV = 2560, 3072, 3584
P_MERGE = 4096
P_HGQ, P_HGFF, P_HGFB, P_HGV, P_HGG = 8192, 8704, 9216, 9728, 10240
P_SSDZ, P_GDNG, P_SMALL = 10752, 11264, 11776
D_PROJ = 12288
S_DT, S_GA, S_GB = 0, 16, 24

VMEM_LIMIT = 56 * 1024 * 1024


def _cparams(sem):
    return pltpu.CompilerParams(dimension_semantics=sem, vmem_limit_bytes=VMEM_LIMIT)


def _sigmoid(x):
    return 1.0 / (1.0 + jnp.exp(-x))


def _silu(x):
    return x * _sigmoid(x)


def _softplus(x):
    return jnp.maximum(x, 0.0) + jnp.log(1.0 + jnp.exp(-jnp.abs(x)))


def _dot(a, b, precision=None):
    return jnp.dot(a, b, preferred_element_type=F32, precision=precision)


def _dot_nt(a, b, precision=None):
    return lax.dot_general(a, b, (((1,), (1,)), ((), ())), preferred_element_type=F32, precision=precision)


def _dot_tn(a, b, precision=None):
    return lax.dot_general(a, b, (((0,), (0,)), ((), ())), preferred_element_type=F32, precision=precision)


def _bdot(a, b):
    return _dot(a.astype(BF16), b.astype(BF16))


def _bdot_nt(a, b):
    return _dot_nt(a.astype(BF16), b.astype(BF16))


def _bdot_tn(a, b):
    return _dot_tn(a.astype(BF16), b.astype(BF16))


def _iota(shape, dim):
    return lax.broadcasted_iota(I32, shape, dim)


def _cumsum_rows(g, reverse):
    n = g.shape[0]
    row = _iota(g.shape, 0)
    sh = 1
    while sh < n:
        if reverse:
            g = g + jnp.where(row < n - sh, pltpu.roll(g, n - sh, 0), 0.0)
        else:
            g = g + jnp.where(row >= sh, pltpu.roll(g, sh, 0), 0.0)
        sh *= 2
    return g


def _lane_pick(a, j):
    return jnp.sum(jnp.where(_iota(a.shape, 1) == j, a, 0.0), axis=1, keepdims=True)


def _split3(a):
    a1 = a.astype(BF16)
    r1 = a - a1.astype(F32)
    a2 = r1.astype(BF16)
    a3 = (r1 - a2.astype(F32)).astype(BF16)
    return a1, a2, a3


def _dot3(a, b):
    a1, a2, _ = _split3(a)
    b1, b2, _ = _split3(b)
    return _dot(a1, b1) + (_dot(a1, b2) + _dot(a2, b1))


def _dot3_nt(a, b):
    a1, a2, _ = _split3(a)
    b1, b2, _ = _split3(b)
    return _dot_nt(a1, b1) + (_dot_nt(a1, b2) + _dot_nt(a2, b1))


def _row_pick(a, j):
    sel = (_iota((8, a.shape[1]), 1) == j).astype(BF16)
    a1, a2, a3 = _split3(a)
    return (_dot_nt(sel, a1) + (_dot_nt(sel, a2) + _dot_nt(sel, a3)))[0:1, :]


def _tri_mask(n, reverse, strict=False):
    t = _iota((n, n), 0)
    s = _iota((n, n), 1)
    if reverse:
        return (s > t) if strict else (s >= t)
    return (s < t) if strict else (s <= t)


def _masked_exp(mask, diff):
    return jnp.where(mask, jnp.exp(jnp.where(mask, diff, 0.0)), 0.0)


def _mod_kernel(c_ref, w_ref, b_ref, o_ref):
    o_ref[...] = _dot(_silu(c_ref[...]), w_ref[...], precision=HIGHEST) + b_ref[...]


def _modulation(cond8, ada_w, ada_b):
    rows = cond8.shape[0]
    tn = 1536
    n = ada_w.shape[1]
    return pl.pallas_call(
        _mod_kernel,
        grid=(n // tn,),
        in_specs=[pl.BlockSpec((rows, D_MODEL), lambda j: (0, 0)),
                  pl.BlockSpec((D_MODEL, tn), lambda j: (0, j)),
                  pl.BlockSpec((1, tn), lambda j: (0, j))],
        out_specs=pl.BlockSpec((rows, tn), lambda j: (0, j)),
        out_shape=jax.ShapeDtypeStruct((rows, n), F32),
        compiler_params=_cparams(("arbitrary",)),
        name="adaln_mod",
    )(cond8, ada_w, ada_b.reshape(1, n))


def _inproj_kernel(x_ref, mod_ref, g_ref, w_ref, cw_ref, cb_ref, o_ref, h_scr, *, seg, n_conv_tiles):
    j = pl.program_id(1)

    @pl.when(j == 0)
    def _():
        x = x_ref[...]
        xn = x * lax.rsqrt(jnp.mean(x * x, axis=-1, keepdims=True) + EPS) * g_ref[...]
        m = mod_ref[...]
        h_scr[...] = (xn * (1.0 + m[1:2]) + m[0:1]).astype(BF16)

    y = _dot(h_scr[...], w_ref[...])

    @pl.when(j < n_conv_tiles)
    def _():
        tm = y.shape[0]
        pos = _iota(y.shape, 0) & (seg - 1)
        prev = jnp.where(pos == 0, 0.0, pltpu.roll(y, 1, 0))
        nxt = jnp.where(pos == seg - 1, 0.0, pltpu.roll(y, tm - 1, 0))
        cw = cw_ref[...]
        o_ref[...] = cb_ref[...] + prev * cw[0:1] + y * cw[1:2] + nxt * cw[2:3]

    @pl.when(j >= n_conv_tiles)
    def _():
        o_ref[...] = y


def _in_projection(x, mod, norm_g, w_bf16, conv_w, conv_b, *, tokens_per_mod, seg):
    T = x.shape[0]
    tm = min(1024, tokens_per_mod)
    tn = 1024
    n_conv_tiles = D_CONV // tn
    tiles_per_mod = tokens_per_mod // tm
    kern = functools.partial(_inproj_kernel, seg=seg, n_conv_tiles=n_conv_tiles)
    cmap = lambda i, j: (0, jnp.minimum(j, n_conv_tiles - 1))
    return pl.pallas_call(
        kern,
        grid=(T // tm, D_PROJ // tn),
        in_specs=[pl.BlockSpec((tm, D_MODEL), lambda i, j: (i, 0)),
                  pl.BlockSpec((None, 6, D_MODEL), lambda i, j: (i // tiles_per_mod, 0, 0)),
                  pl.BlockSpec((1, D_MODEL), lambda i, j: (0, 0)),
                  pl.BlockSpec((D_MODEL, tn), lambda i, j: (0, j)),
                  pl.BlockSpec((3, tn), cmap),
                  pl.BlockSpec((1, tn), cmap)],
        out_specs=pl.BlockSpec((tm, tn), lambda i, j: (i, j)),
        out_shape=jax.ShapeDtypeStruct((T, D_PROJ), F32),
        scratch_shapes=[pltpu.VMEM((tm, D_MODEL), BF16)],
        compiler_params=_cparams(("arbitrary", "arbitrary")),
        name="in_proj",
    )(x, mod, norm_g.reshape(1, D_MODEL), w_bf16, conv_w, conv_b.reshape(1, D_CONV))


def _hyfilt_kernel(w1_ref, b1_ref, w2_ref, b2_ref, w3_ref, o_ref, *, L):
    i = pl.program_id(1)
    tl = o_ref.shape[0]
    t = (_iota((tl, LANES), 0) + i * tl).astype(F32) / L
    lane = _iota((tl, LANES), 1)
    band = jnp.where(lane <= HY_POS_FREQS, lane, lane - HY_POS_FREQS).astype(F32)
    ang = 2.0 * math.pi * t * band
    feats = jnp.where(lane == 0, t,
                      jnp.where(lane <= HY_POS_FREQS, jnp.sin(ang),
                                jnp.where(lane <= 2 * HY_POS_FREQS, jnp.cos(ang), 0.0)))
    hdn = jnp.sin(_dot(feats, w1_ref[...], precision=HIGHEST) + b1_ref[...])
    hdn = jnp.sin(_dot(hdn, w2_ref[...], precision=HIGHEST) + b2_ref[...])
    filt = _dot(hdn, w3_ref[...], precision=HIGHEST)
    max_decay = math.log(HY_DECAY_TARGET) / HY_FAST_DECAY
    min_decay = math.log(HY_DECAY_TARGET) / HY_SLOW_DECAY
    n = filt.shape[1]
    ch = (_iota((tl, n), 1) & (D_BRANCH - 1)).astype(F32)
    delta = min_decay + ch * ((max_decay - min_decay) / (D_BRANCH - 1))
    tt = (_iota((tl, n), 0) + i * tl).astype(F32) / L
    o_ref[...] = filt * jnp.exp(-tt * jnp.abs(delta))


def _hyena_filters(L, w1p, b1, w2, b2, w3):
    tl = min(L, 256)
    n = 2 * D_BRANCH
    return pl.pallas_call(
        functools.partial(_hyfilt_kernel, L=L),
        grid=(2, L // tl),
        in_specs=[pl.BlockSpec((LANES, HY_FILTER_HIDDEN), lambda d, i: (0, 0)),
                  pl.BlockSpec((1, HY_FILTER_HIDDEN), lambda d, i: (0, 0)),
                  pl.BlockSpec((HY_FILTER_HIDDEN, HY_FILTER_HIDDEN), lambda d, i: (0, 0)),
                  pl.BlockSpec((1, HY_FILTER_HIDDEN), lambda d, i: (0, 0)),
                  pl.BlockSpec((HY_FILTER_HIDDEN, n), lambda d, i: (0, d))],
        out_specs=pl.BlockSpec((None, tl, n), lambda d, i: (d, i, 0)),
        out_shape=jax.ShapeDtypeStruct((2, L, n), F32),
        compiler_params=_cparams(("arbitrary", "arbitrary")),
        name="hyena_filters",
    )(w1p, b1.reshape(1, -1), w2, b2.reshape(1, -1), w3)


def _dftgen_kernel(f_ref, fs_ref, *, L, tk):
    i = pl.program_id(0)
    N = 2 * L
    k = _iota((tk, LANES), 0) + i * tk
    lane = _iota((tk, LANES), 1)
    w = 2.0 * math.pi / N
    a0 = ((k * lane) & (N - 1)).astype(F32) * w
    c0, s0 = jnp.cos(a0), jnp.sin(a0)
    a1 = ((k * lane * LANES) & (N - 1)).astype(F32) * w
    c1, s1 = jnp.cos(a1), jnp.sin(a1)
    alt = jnp.where((lane & 1) == 0, 1.0, -1.0)
    coef = jnp.where(k == 0, 1.0 / N, 2.0 / N)
    for n1 in range(L // LANES):
        c1b = c1[:, n1:n1 + 1]
        s1b = s1[:, n1:n1 + 1]
        cosb = c1b * c0 - s1b * s0
        sinb = jnp.where(k == 0, alt, -(s1b * c0 + c1b * s0))
        cols = slice(n1 * LANES, (n1 + 1) * LANES)
        f_ref[0:tk, cols] = cosb.astype(BF16)
        f_ref[tk:2 * tk, cols] = sinb.astype(BF16)
        fs_ref[0:tk, cols] = (coef * cosb).astype(BF16)
        fs_ref[tk:2 * tk, cols] = (coef * sinb).astype(BF16)


def _dft_matrices(L):
    tk = min(L, 256)
    spec = pl.BlockSpec((2 * tk, L), lambda i: (i, 0))
    return pl.pallas_call(
        functools.partial(_dftgen_kernel, L=L, tk=tk),
        grid=(L // tk,),
        in_specs=[],
        out_specs=[spec, spec],
        out_shape=[jax.ShapeDtypeStruct((2 * L, L), BF16)] * 2,
        compiler_params=_cparams(("arbitrary",)),
        name="dft_matrices",
    )()


def _dfth_kernel(f_ref, h_ref, o_ref, hp_scr, *, tk):
    i = pl.program_id(1)
    n = o_ref.shape[1]

    @pl.when(i == 0)
    def _():
        hc = h_ref[0]
        ha = jnp.where(_iota(hc.shape, 0) == 0, 0.0, h_ref[1])
        hp_scr[:, 0:n] = (hc + ha).astype(BF16)
        hp_scr[:, n:2 * n] = (hc - ha).astype(BF16)

    u = _dot(f_ref[...], hp_scr[...])
    r = _iota((2 * tk, n), 0)
    from_sum = (r < tk) | ((r == tk) & (i == 0))
    o_ref[...] = jnp.where(from_sum, u[:, 0:n], u[:, n:2 * n])


def _filter_spectrum(fmat, filt):
    L = fmat.shape[1]
    tk = min(L, 256)
    C = filt.shape[2]
    tc = 256
    return pl.pallas_call(
        functools.partial(_dfth_kernel, tk=tk),
        grid=(C // tc, L // tk),
        in_specs=[pl.BlockSpec((2 * tk, L), lambda c, i: (i, 0)),
                  pl.BlockSpec((2, L, tc), lambda c, i: (0, 0, c))],
        out_specs=pl.BlockSpec((2 * tk, tc), lambda c, i: (i, c)),
        out_shape=jax.ShapeDtypeStruct((2 * L, C), F32),
        scratch_shapes=[pltpu.VMEM((L, 2 * tc), BF16)],
        compiler_params=_cparams(("arbitrary", "arbitrary")),
        name="filter_spectrum",
    )(fmat, filt)


def _dfta_kernel(f_ref, u_ref, h_ref, y_ref, u_scr, *, tk):
    i = pl.program_id(1)

    @pl.when(i == 0)
    def _():
        u_scr[...] = u_ref[...].astype(BF16)

    uf = _dot(f_ref[...], u_scr[...])
    ur, ui = uf[0:tk], uf[tk:2 * tk]
    hr, hi = h_ref[0:tk, :], h_ref[tk:2 * tk, :]
    dc = (_iota(ur.shape, 0) == 0) & (i == 0)
    y_ref[0:tk, :] = jnp.where(dc, ur * hr, ur * hr - ui * hi).astype(BF16)
    y_ref[tk:2 * tk, :] = jnp.where(dc, ui * hi, ur * hi + ui * hr).astype(BF16)


def _dftb_kernel(g_ref, y_ref, u_ref, x_ref, b_ref, o_ref):
    y = _dot(g_ref[...], y_ref[...])
    o_ref[...] = x_ref[...] * (y + u_ref[...] * b_ref[...])


def _spectral_conv(fmat, gmat, hspec, h_col, bias, u_arr, u_col, gate_arr, gate_col):
    B, L = u_arr.shape[0], u_arr.shape[1]
    C = D_BRANCH
    tk = min(L, 256)
    y = pl.pallas_call(
        functools.partial(_dfta_kernel, tk=tk),
        grid=(B, L // tk),
        in_specs=[pl.BlockSpec((2 * tk, L), lambda b, i: (i, 0)),
                  pl.BlockSpec((None, L, C), lambda b, i: (b, 0, u_col)),
                  pl.BlockSpec((2 * tk, C), lambda b, i: (i, h_col))],
        out_specs=pl.BlockSpec((None, 2 * tk, C), lambda b, i: (b, i, 0)),
        out_shape=jax.ShapeDtypeStruct((B, 2 * L, C), BF16),
        scratch_shapes=[pltpu.VMEM((L, C), BF16)],
        compiler_params=_cparams(("arbitrary", "arbitrary")),
        name="hyena_dft_fwd",
    )(fmat, u_arr, hspec)
    tr = min(L, 256)
    return pl.pallas_call(
        _dftb_kernel,
        grid=(B, L // tr),
        in_specs=[pl.BlockSpec((tr, 2 * L), lambda b, i: (i, 0)),
                  pl.BlockSpec((None, 2 * L, C), lambda b, i: (b, 0, 0)),
                  pl.BlockSpec((None, tr, C), lambda b, i: (b, i, u_col)),
                  pl.BlockSpec((None, tr, C), lambda b, i: (b, i, gate_col)),
                  pl.BlockSpec((1, C), lambda b, i: (0, 0))],
        out_specs=pl.BlockSpec((None, tr, C), lambda b, i: (b, i, 0)),
        out_shape=jax.ShapeDtypeStruct((B, L, C), F32),
        compiler_params=_cparams(("arbitrary", "arbitrary")),
        name="hyena_dft_inv",
    )(gmat, y, u_arr, gate_arr, bias.reshape(1, C))


def _hgrn_kernel(q_ref, ff_ref, fb_ref, v_ref, g_ref, lb_ref, nrm_ref, s0_ref, y_ref, sf_ref,
                 of_scr, ob_scr, st_scr, *, L, layer, depth):
    C = HG_CHUNK
    W = HG_DK
    nc = L // C
    nh = q_ref.shape[1] // W
    ridx = _iota((C, W), 0)
    lidx = _iota((C, W), 1)
    o_scrs = (of_scr, ob_scr)
    f_refs = (ff_ref, fb_ref)

    def lower_bound(d, cols):
        rows = [lb_ref[d, l:l + 1, cols] for l in range(depth)]
        m = rows[0]
        for r in rows[1:]:
            m = jnp.maximum(m, r)
        es = [jnp.exp(r - m) for r in rows]
        tot = es[0]
        for e in es[1:]:
            tot = tot + e
        acc = es[0] / tot
        for e in es[1:layer + 1]:
            acc = acc + e / tot
        return acc - es[0] / tot

    lbs = [[lower_bound(d, slice(hh * W, (hh + 1) * W)) for hh in range(nh)] for d in (0, 1)]
    for d in (0, 1):
        for hh in range(nh):
            st_scr[d, hh] = s0_ref[d, hh].T

    half = C // 2
    hi_rows = ridx >= half
    local = ridx & (half - 1)
    lane_of_col = jnp.where(hi_rows, half, 0)

    units = [(d, hh) for d in (0, 1) for hh in range(nh)]

    def body(ci, carry):
        rows_d = []
        for d in (0, 1):
            c = (nc - 1 - ci) if d == 1 else ci
            rows_d.append(pl.ds(pl.multiple_of(c * C, C), C))
        q, v, kin, b, lhs, rhs = {}, {}, {}, {}, {}, {}
        for u in units:
            d, hh = u
            rev = d == 1
            cols = slice(hh * W, (hh + 1) * W)
            lb = lbs[d][hh]
            q[u] = _silu(q_ref[rows_d[d], cols])
            uf = f_refs[d][rows_d[d], cols]
            v[u] = v_ref[rows_d[d], cols]
            f = lb + (1.0 - lb) * _sigmoid(uf)
            kin[u] = (1.0 - lb) * _sigmoid(-uf)
            b[u] = _cumsum_rows(jnp.log(jnp.maximum(f, LOG_FLOOR)), rev)
            lhs_rows = ~hi_rows if rev else hi_rows
            mid = b[u][half:half + 1, :] if rev else b[u][half - 1:half, :]
            e = jnp.exp(jnp.where(lhs_rows, b[u] - mid, mid - b[u]))
            lhs[u] = jnp.where(lhs_rows, q[u] * e, 0.0)
            rhs[u] = jnp.where(lhs_rows, 0.0, kin[u] * e)
        st = {u: st_scr[u[0], u[1]] for u in units}
        b_end = {u: (b[u][0:1, :] if u[0] == 1 else b[u][C - 1:C, :]) for u in units}
        attn = {u: _dot3_nt(lhs[u], rhs[u]) for u in units}
        o_state = {u: _bdot_nt(q[u] * jnp.exp(b[u]), st[u]) for u in units}
        upd = {u: _bdot_tn(v[u], kin[u] * jnp.exp(b_end[u] - b[u])) for u in units}
        acc = {}
        for u in units:
            rev = u[0] == 1
            a = jnp.zeros((C, W), F32)
            for s in range(half):
                ref = jnp.where(hi_rows, b[u][half + s:half + s + 1, :], b[u][s:s + 1, :])
                kref = jnp.where(hi_rows, kin[u][half + s:half + s + 1, :], kin[u][s:s + 1, :])
                mask = (local <= s) if rev else (local >= s)
                col = jnp.sum(_masked_exp(mask, b[u] - ref) * (q[u] * kref), axis=1, keepdims=True)
                a = jnp.where(lidx == lane_of_col + s, col, a)
            acc[u] = a
        o_intra = {u: _bdot(attn[u] + acc[u][:, 0:C], v[u]) for u in units}
        for u in units:
            d, hh = u
            st_scr[d, hh] = st[u] * jnp.exp(b_end[u]) + upd[u]
            o_scrs[d][rows_d[d], slice(hh * W, (hh + 1) * W)] = o_state[u] + o_intra[u]
        return carry

    lax.fori_loop(0, nc, body, 0)
    for d in (0, 1):
        for hh in range(nh):
            sf_ref[d, hh] = st_scr[d, hh].T

    tr = min(L, 256)

    def fin(i, carry):
        rows = pl.ds(pl.multiple_of(i * tr, tr), tr)
        for hh in range(nh):
            cols = slice(hh * W, (hh + 1) * W)
            o = of_scr[rows, cols] + ob_scr[rows, cols]
            on = o * lax.rsqrt(jnp.mean(o * o, axis=-1, keepdims=True) + EPS) * nrm_ref[...]
            y_ref[rows, cols] = on * _silu(g_ref[rows, cols])
        return carry

    lax.fori_loop(0, L // tr, fin, 0)


def _hgrn_scan(proj3, hg_lb, hg_norm, s0, *, layer):
    B, L = proj3.shape[0], proj3.shape[1]
    depth = hg_lb.shape[1]
    nh = HG_HEADS
    W = nh * HG_DK
    col = lambda base: (lambda b, h: (b, 0, base // W + h))
    st_spec = pl.BlockSpec((None, 2, nh, HG_DK, HG_DK), lambda b, h: (b, 0, h, 0, 0))
    once = pl.Buffered(1)
    return pl.pallas_call(
        functools.partial(_hgrn_kernel, L=L, layer=layer, depth=depth),
        grid=(B, HG_HEADS // nh),
        in_specs=[pl.BlockSpec((None, L, W), col(P_HGQ), pipeline_mode=once),
                  pl.BlockSpec((None, L, W), col(P_HGFF), pipeline_mode=once),
                  pl.BlockSpec((None, L, W), col(P_HGFB), pipeline_mode=once),
                  pl.BlockSpec((None, L, W), col(P_HGV), pipeline_mode=once),
                  pl.BlockSpec((None, L, W), col(P_HGG), pipeline_mode=once),
                  pl.BlockSpec((2, depth, W), lambda b, h: (0, 0, h)),
                  pl.BlockSpec((1, HG_DK), lambda b, h: (0, 0)),
                  st_spec],
        out_specs=[pl.BlockSpec((None, L, W), lambda b, h: (b, 0, h)), st_spec],
        out_shape=[jax.ShapeDtypeStruct((B, L, D_BRANCH), F32),
                   jax.ShapeDtypeStruct(s0.shape, F32)],
        scratch_shapes=[pltpu.VMEM((L, W), F32), pltpu.VMEM((L, W), F32),
                        pltpu.VMEM((2, nh, HG_DK, HG_DK), F32)],
        compiler_params=_cparams(("arbitrary", "arbitrary")),
        name="hgrn2_scan",
    )(proj3, proj3, proj3, proj3, proj3, hg_lb, hg_norm.reshape(1, HG_DK), s0)


def _ssd_kernel(x_ref, bm_ref, cm_ref, z_ref, sm_ref, bias_ref, alog_ref, dskip_ref, s0_ref,
                y_ref, sf_ref, of_scr, ob_scr, st_scr, *, L):
    C = SSD_CHUNK
    W = LANES
    nc = L // C
    npairs = x_ref.shape[1] // W
    grp = pl.program_id(1)
    lane_lo = _iota((C, W), 1) < SSD_P
    row_lo = _iota((W, SSD_N), 0) < SSD_P
    pick2 = lambda a0, a1: jnp.where(lane_lo, a0, a1)
    o_scrs = (of_scr, ob_scr)
    incls = (_tri_mask(C, False), _tri_mask(C, True))
    for d in (0, 1):
        for pp in range(npairs):
            st_scr[d, pp] = s0_ref[d, pp]

    heads = [(d, pp, hh) for d in (0, 1) for pp in range(npairs) for hh in (0, 1)]
    pairs = [(d, pp) for d in (0, 1) for pp in range(npairs)]

    def body(ci, carry):
        rows_d, bm, cm, dt_all, cum_all, gram = [], [], [], [], [], []
        for d in (0, 1):
            c = (nc - 1 - ci) if d == 1 else ci
            rows = pl.ds(pl.multiple_of(c * C, C), C)
            rows_d.append(rows)
            bm.append(_silu(bm_ref[rows, :]))
            cm.append(_silu(cm_ref[rows, :]))
            dt_all.append(_softplus(sm_ref[rows, :] + bias_ref[...]))
            cum_all.append(_cumsum_rows(-jnp.exp(alog_ref[...]) * dt_all[d], d == 1))
        lane_of = lambda u: S_DT + u[0] * SSD_HEADS + 2 * (npairs * grp + u[1]) + u[2]
        cumr = {u: _row_pick(cum_all[u[0]], lane_of(u)) for u in heads}
        for d in (0, 1):
            gram.append(_bdot_nt(cm[d], bm[d]))
        dtc = {u: _lane_pick(dt_all[u[0]], lane_of(u)) for u in heads}
        cumc = {u: _lane_pick(cum_all[u[0]], lane_of(u)) for u in heads}
        end = {u: (cumc[u][0:1, :] if u[0] == 1 else cumc[u][C - 1:C, :]) for u in heads}
        xs = {pr: _silu(x_ref[rows_d[pr[0]], slice(pr[1] * W, (pr[1] + 1) * W)]) for pr in pairs}
        st = {pr: st_scr[pr[0], pr[1]] for pr in pairs}
        y_state = {u: _bdot_nt(cm[u[0]] * jnp.exp(cumc[u]), st[u[:2]]) for u in heads}
        scores = {u: gram[u[0]] * _masked_exp(incls[u[0]], cumc[u] - cumr[u]) for u in heads}
        y_intra = {}
        for u in heads:
            head_x = jnp.where(lane_lo if u[2] == 0 else ~lane_lo, xs[u[:2]], 0.0) * dtc[u]
            y_intra[u] = _bdot(scores[u], head_x)
        for pr in pairs:
            u0, u1 = pr + (0,), pr + (1,)
            xdt_e = xs[pr] * pick2(dtc[u0] * jnp.exp(end[u0] - cumc[u0]), dtc[u1] * jnp.exp(end[u1] - cumc[u1]))
            st_scr[pr[0], pr[1]] = (st[pr] * jnp.where(row_lo, jnp.exp(end[u0]), jnp.exp(end[u1]))
                                    + _bdot_tn(xdt_e, bm[pr[0]]))
            o_scrs[pr[0]][rows_d[pr[0]], slice(pr[1] * W, (pr[1] + 1) * W)] = pick2(
                y_intra[u0] + y_state[u0], y_intra[u1] + y_state[u1])
        return carry

    lax.fori_loop(0, nc, body, 0)
    for d in (0, 1):
        for pp in range(npairs):
            sf_ref[d, pp] = st_scr[d, pp]

    tr = min(L, 256)

    def fin(i, carry):
        rows = pl.ds(pl.multiple_of(i * tr, tr), tr)
        y = of_scr[rows, :] + ob_scr[rows, :] + _silu(x_ref[rows, :]) * dskip_ref[...]
        y_ref[rows, :] = y * _silu(z_ref[rows, :])
        return carry

    lax.fori_loop(0, L // tr, fin, 0)


def _ssd_scan(proj3, bias_row, alog_row, ssd_d, s0):
    B, L = proj3.shape[0], proj3.shape[1]
    n_groups = 2
    n_pairs = SSD_HEADS // 2
    ppg = n_pairs // n_groups
    W = ppg * LANES
    s0p = s0.reshape(B, 2, n_pairs, 2 * SSD_P, SSD_N)
    dskip = jnp.repeat(ssd_d, SSD_P).reshape(n_groups, 1, W)
    col = lambda base: (lambda b, g: (b, 0, base // W + g))
    grp = lambda base: (lambda b, g: (b, 0, base // LANES + g))
    st_spec = pl.BlockSpec((None, 2, ppg, LANES, SSD_N), lambda b, g: (b, 0, g, 0, 0))
    y, sf = pl.pallas_call(
        functools.partial(_ssd_kernel, L=L),
        grid=(B, n_groups),
        in_specs=[pl.BlockSpec((None, L, W), col(C_SSDX)),
                  pl.BlockSpec((None, L, LANES), grp(C_SSDB)),
                  pl.BlockSpec((None, L, LANES), grp(C_SSDC)),
                  pl.BlockSpec((None, L, W), col(P_SSDZ)),
                  pl.BlockSpec((None, L, LANES), lambda b, g: (b, 0, P_SMALL // LANES)),
                  pl.BlockSpec((1, LANES), lambda b, g: (0, 0)),
                  pl.BlockSpec((1, LANES), lambda b, g: (0, 0)),
                  pl.BlockSpec((None, 1, W), lambda b, g: (g, 0, 0)),
                  st_spec],
        out_specs=[pl.BlockSpec((None, L, W), lambda b, g: (b, 0, g)), st_spec],
        out_shape=[jax.ShapeDtypeStruct((B, L, D_BRANCH), F32),
                   jax.ShapeDtypeStruct(s0p.shape, F32)],
        scratch_shapes=[pltpu.VMEM((L, W), F32), pltpu.VMEM((L, W), F32),
                        pltpu.VMEM((2, ppg, LANES, SSD_N), F32)],
        compiler_params=_cparams(("arbitrary", "arbitrary")),
        name="ssd_scan",
    )(proj3, proj3, proj3, proj3, proj3, bias_row, alog_row, dskip, s0p)
    return y, sf.reshape(s0.shape)


def _l2norm(a):
    return a * lax.rsqrt(jnp.sum(a * a, axis=-1, keepdims=True) + EPS)


def _gdn_kernel(q_ref, k_ref, v_ref, g_ref, sm_ref, bias_ref, alog_ref, nrm_ref, s0_ref,
                y_ref, sf_ref, of_scr, ob_scr, st_scr, *, L):
    C = GDN_CHUNK
    W = GDN_DK
    nc = L // C
    nh = q_ref.shape[1] // W
    hblk = pl.program_id(1)
    o_scrs = (of_scr, ob_scr)
    incls = (_tri_mask(C, False), _tri_mask(C, True))
    stricts = (_tri_mask(C, False, strict=True), _tri_mask(C, True, strict=True))
    for d in (0, 1):
        for hh in range(nh):
            st_scr[d, hh] = s0_ref[d, hh]

    units = [(d, hh) for d in (0, 1) for hh in range(nh)]
    eye = (_iota((C, C), 0) == _iota((C, C), 1)).astype(F32)

    def body(ci, carry):
        rows_d, cum_d, beta_d = [], [], []
        for d in (0, 1):
            c = (nc - 1 - ci) if d == 1 else ci
            rows = pl.ds(pl.multiple_of(c * C, C), C)
            raw = sm_ref[rows, :]
            rows_d.append(rows)
            cum_d.append(_cumsum_rows(-jnp.exp(alog_ref[...]) * _softplus(raw + bias_ref[...]), d == 1))
            beta_d.append(_sigmoid(raw))
        q, k, v, gc, grow, beta = {}, {}, {}, {}, {}, {}
        for u in units:
            d, hh = u
            cols = slice(hh * W, (hh + 1) * W)
            h = nh * hblk + hh
            q[u] = _l2norm(_silu(q_ref[rows_d[d], cols])) * (GDN_DK ** -0.5)
            k[u] = _l2norm(_silu(k_ref[rows_d[d], cols]))
            v[u] = _silu(v_ref[rows_d[d], cols])
            gc[u] = _lane_pick(cum_d[d], S_GA + d * GDN_HEADS + h)
            beta[u] = _lane_pick(beta_d[d], S_GB + d * GDN_HEADS + h)
        for u in units:
            grow[u] = _row_pick(cum_d[u[0]], S_GA + u[0] * GDN_HEADS + nh * hblk + u[1])
        decay = {u: _masked_exp(incls[u[0]], gc[u] - grow[u]) for u in units}
        kb = {u: k[u] * beta[u] for u in units}
        m = {u: -jnp.where(stricts[u[0]], _bdot_nt(kb[u], k[u]) * decay[u], 0.0) for u in units}
        aqk = {u: _bdot_nt(q[u], k[u]) * decay[u] for u in units}
        p = {u: eye + m[u] for u in units}
        sh = 2
        while sh < C:
            m = {u: _dot3(m[u], m[u]) for u in units}
            p = {u: p[u] + _dot3(p[u], m[u]) for u in units}
            sh *= 2
        uu = {u: _dot3(p[u], v[u] * beta[u]) for u in units}
        ww = {u: _dot3(p[u], kb[u] * jnp.exp(gc[u])) for u in units}
        st = {u: st_scr[u[0], u[1]] for u in units}
        v_new = {u: uu[u] - _bdot(ww[u], st[u]) for u in units}
        o = {u: _bdot(q[u] * jnp.exp(gc[u]), st[u]) + _bdot(aqk[u], v_new[u]) for u in units}
        for u in units:
            d, hh = u
            g_end = gc[u][0:1, :] if d == 1 else gc[u][C - 1:C, :]
            st_scr[d, hh] = st[u] * jnp.exp(g_end) + _bdot_tn(k[u] * jnp.exp(g_end - gc[u]), v_new[u])
            o_scrs[d][rows_d[d], slice(hh * W, (hh + 1) * W)] = o[u]
        return carry

    lax.fori_loop(0, nc, body, 0)
    for d in (0, 1):
        for hh in range(nh):
            sf_ref[d, hh] = st_scr[d, hh]

    tr = min(L, 256)

    def fin(i, carry):
        rows = pl.ds(pl.multiple_of(i * tr, tr), tr)
        for hh in range(nh):
            cols = slice(hh * W, (hh + 1) * W)
            o = of_scr[rows, cols] + ob_scr[rows, cols]
            on = o * lax.rsqrt(jnp.mean(o * o, axis=-1, keepdims=True) + EPS) * nrm_ref[...]
            y_ref[rows, cols] = on * _silu(g_ref[rows, cols])
        return carry

    lax.fori_loop(0, L // tr, fin, 0)


def _gdn_scan(proj3, bias_row, alog_row, gdn_norm, s0):
    B, L = proj3.shape[0], proj3.shape[1]
    nh = GDN_HEADS
    W = nh * GDN_DK
    col = lambda base: (lambda b, h: (b, 0, base // W + h))
    st_spec = pl.BlockSpec((None, 2, nh, GDN_DK, GDN_DK), lambda b, h: (b, 0, h, 0, 0))
    once = pl.Buffered(1)
    return pl.pallas_call(
        functools.partial(_gdn_kernel, L=L),
        grid=(B, GDN_HEADS // nh),
        in_specs=[pl.BlockSpec((None, L, W), col(C_GQ), pipeline_mode=once),
                  pl.BlockSpec((None, L, W), col(C_GK), pipeline_mode=once),
                  pl.BlockSpec((None, L, W), col(C_GV), pipeline_mode=once),
                  pl.BlockSpec((None, L, W), col(P_GDNG), pipeline_mode=once),
                  pl.BlockSpec((None, L, LANES), lambda b, h: (b, 0, P_SMALL // LANES), pipeline_mode=once),
                  pl.BlockSpec((1, LANES), lambda b, h: (0, 0)),
                  pl.BlockSpec((1, LANES), lambda b, h: (0, 0)),
                  pl.BlockSpec((1, GDN_DK), lambda b, h: (0, 0)),
                  st_spec],
        out_specs=[pl.BlockSpec((None, L, W), lambda b, h: (b, 0, h)), st_spec],
        out_shape=[jax.ShapeDtypeStruct((B, L, D_BRANCH), F32),
                   jax.ShapeDtypeStruct(s0.shape, F32)],
        scratch_shapes=[pltpu.VMEM((L, W), F32), pltpu.VMEM((L, W), F32),
                        pltpu.VMEM((2, nh, GDN_DK, GDN_DK), F32)],
        compiler_params=_cparams(("arbitrary", "arbitrary")),
        name="gdn_scan",
    )(proj3, proj3, proj3, proj3, proj3, bias_row, alog_row, gdn_norm.reshape(1, GDN_DK), s0)


def _merge_kernel(ya_ref, yb_ref, yc_ref, yd_ref, mg_ref, x_ref, mod_ref, sn_ref, n2_ref, wb_ref, wo_ref,
                  xo_ref, h_ref):
    yc = yc_ref[...]
    half = D_BRANCH // 2
    parts = []
    for gidx in range(2):
        seg = yc[:, gidx * half:(gidx + 1) * half]
        parts.append(seg * lax.rsqrt(jnp.mean(seg * seg, axis=-1, keepdims=True) + EPS)
                     * sn_ref[:, gidx * half:(gidx + 1) * half])
    branches = (ya_ref[...], yb_ref[...], None, yd_ref[...])
    mixed = None
    for n in range(N_BRANCH):
        gate = _sigmoid(mg_ref[:, n * D_MODEL:(n + 1) * D_MODEL])
        if n == 2:
            lifted = (_dot(parts[0].astype(BF16), wb_ref[n, 0:half, :])
                      + _dot(parts[1].astype(BF16), wb_ref[n, half:D_BRANCH, :]))
        else:
            lifted = _dot(branches[n].astype(BF16), wb_ref[n])
        mixed = gate * lifted if mixed is None else mixed + gate * lifted
    m = mod_ref[...]
    x = x_ref[...] + m[2:3] * _dot(mixed.astype(BF16), wo_ref[...])
    xo_ref[...] = x
    xn = x * lax.rsqrt(jnp.mean(x * x, axis=-1, keepdims=True) + EPS) * n2_ref[...]
    h_ref[...] = xn * (1.0 + m[4:5]) + m[3:4]


def _merge(ya, yb, yc, yd, proj, x, mod, ssd_norm, norm2_g, wb_bf16, wo_bf16, *, tokens_per_mod):
    T = x.shape[0]
    tm = 256
    tiles_per_mod = tokens_per_mod // tm
    yspec = pl.BlockSpec((tm, D_BRANCH), lambda i: (i, 0))
    xspec = pl.BlockSpec((tm, D_MODEL), lambda i: (i, 0))
    return pl.pallas_call(
        _merge_kernel,
        grid=(T // tm,),
        in_specs=[yspec, yspec, yspec, yspec,
                  pl.BlockSpec((tm, N_BRANCH * D_MODEL), lambda i: (i, P_MERGE // (N_BRANCH * D_MODEL))),
                  xspec,
                  pl.BlockSpec((None, 6, D_MODEL), lambda i: (i // tiles_per_mod, 0, 0)),
                  pl.BlockSpec((1, D_BRANCH), lambda i: (0, 0)),
                  pl.BlockSpec((1, D_MODEL), lambda i: (0, 0)),
                  pl.BlockSpec((N_BRANCH, D_BRANCH, D_MODEL), lambda i: (0, 0, 0)),
                  pl.BlockSpec((D_MODEL, D_MODEL), lambda i: (0, 0))],
        out_specs=[xspec, xspec],
        out_shape=[jax.ShapeDtypeStruct((T, D_MODEL), F32)] * 2,
        compiler_params=_cparams(("arbitrary",)),
        name="merge_outproj",
    )(ya, yb, yc, yd, proj, x, mod, ssd_norm.reshape(1, D_BRANCH), norm2_g.reshape(1, D_MODEL), wb_bf16, wo_bf16)


def _router_kernel(h_ref, rw_ref, rb_ref, idx_ref, wt_ref, rank_ref, cnt_ref, carry):
    i = pl.program_id(0)
    tm = h_ref.shape[0]
    E = N_EXPERTS
    gsz = E // N_EXPERT_GROUPS
    neg_inf = -jnp.inf

    @pl.when(i == 0)
    def _():
        carry[...] = jnp.zeros_like(carry)

    scores = _sigmoid(_dot_nt(rw_ref[...], h_ref[...], precision=HIGHEST))
    biased = scores + rb_ref[...]
    eidx = _iota((E, tm), 0)
    ridx = _iota((gsz, tm), 0)

    slabs = [biased[g * gsz:(g + 1) * gsz, :] for g in range(N_EXPERT_GROUPS)]
    gs = []
    for v in slabs:
        m1 = jnp.max(v, axis=0, keepdims=True)
        i1 = jnp.min(jnp.where(v == m1, ridx, gsz), axis=0, keepdims=True)
        m2 = jnp.max(jnp.where(ridx == i1, neg_inf, v), axis=0, keepdims=True)
        gs.append(m1 + m2)
    masked = []
    for g in range(N_EXPERT_GROUPS):
        rank = jnp.zeros((1, tm), I32)
        for g2 in range(N_EXPERT_GROUPS):
            if g2 == g:
                continue
            ahead = (gs[g2] >= gs[g]) if g2 < g else (gs[g2] > gs[g])
            rank = rank + ahead.astype(I32)
        masked.append(jnp.where(rank < TOPK_GROUPS, slabs[g], MASK_NEG))
    cur = jnp.concatenate(masked, axis=0)

    krow = _iota((TOP_K, tm), 0)
    idx_out = jnp.zeros((TOP_K, tm), I32)
    onehot = jnp.zeros((E, tm), F32)
    picks = []
    wsum = jnp.zeros((1, tm), F32)
    for kk in range(TOP_K):
        m = jnp.max(cur, axis=0, keepdims=True)
        ik = jnp.min(jnp.where(cur == m, eidx, E), axis=0, keepdims=True)
        hit = eidx == ik
        wk = jnp.sum(jnp.where(hit, scores, 0.0), axis=0, keepdims=True)
        cur = jnp.where(hit, neg_inf, cur)
        onehot = onehot + hit.astype(F32)
        picks.append((ik, wk))
        wsum = wsum + wk
        idx_out = jnp.where(krow == kk, ik, idx_out)
    wt_out = jnp.zeros((TOP_K, tm), F32)
    for kk, (ik, wk) in enumerate(picks):
        wt_out = jnp.where(krow == kk, wk / wsum * ROUTED_SCALE, wt_out)

    earlier = (_iota((tm, tm), 0) < _iota((tm, tm), 1)).astype(BF16)
    before = _dot(onehot.astype(BF16), earlier) + carry[:, 0:1]
    rank_out = jnp.zeros((TOP_K, tm), I32)
    for kk, (ik, wk) in enumerate(picks):
        rk = jnp.sum(jnp.where(eidx == ik, before, 0.0), axis=0, keepdims=True)
        rank_out = jnp.where(krow == kk, rk.astype(I32), rank_out)
    carry[...] = carry[...] + jnp.sum(onehot, axis=1, keepdims=True)

    idx_ref[...] = idx_out
    wt_ref[...] = wt_out
    rank_ref[...] = rank_out
    cnt_ref[...] = carry[...]


def _router(h2, router_w, router_b):
    T = h2.shape[0]
    tm = min(512, T)
    ospec = pl.BlockSpec((TOP_K, tm), lambda i: (0, i))
    cspec = pl.BlockSpec((N_EXPERTS, LANES), lambda i: (0, 0))
    idx, wt, rank, cnt = pl.pallas_call(
        _router_kernel,
        grid=(T // tm,),
        in_specs=[pl.BlockSpec((tm, D_MODEL), lambda i: (i, 0)),
                  pl.BlockSpec((N_EXPERTS, D_MODEL), lambda i: (0, 0)),
                  pl.BlockSpec((N_EXPERTS, 1), lambda i: (0, 0))],
        out_specs=[ospec, ospec, ospec, cspec],
        out_shape=[jax.ShapeDtypeStruct((TOP_K, T), I32), jax.ShapeDtypeStruct((TOP_K, T), F32),
                   jax.ShapeDtypeStruct((TOP_K, T), I32), jax.ShapeDtypeStruct((N_EXPERTS, LANES), F32)],
        scratch_shapes=[pltpu.VMEM((N_EXPERTS, LANES), F32)],
        compiler_params=_cparams(("arbitrary",)),
        name="moe_router",
    )(h2, router_w.T, router_b.reshape(N_EXPERTS, 1))
    return idx, wt, rank, cnt[:, 0].astype(I32)


def _dest_kernel(idx_ref, rank_ref, starts_ref, o_ref):
    idx = idx_ref[...]
    eidx = _iota((N_EXPERTS, idx.shape[1]), 0)
    krow = _iota(idx.shape, 0)
    out = jnp.zeros(idx.shape, I32)
    for kk in range(TOP_K):
        seg = jnp.sum(jnp.where(eidx == idx[kk:kk + 1, :], starts_ref[...], 0), axis=0, keepdims=True)
        out = jnp.where(krow == kk, seg, out)
    o_ref[...] = out + rank_ref[...]


def _dest_rows(idx, rank, starts):
    T = idx.shape[1]
    tm = min(2048, T)
    spec = pl.BlockSpec((TOP_K, tm), lambda i: (0, i))
    return pl.pallas_call(
        _dest_kernel,
        grid=(T // tm,),
        in_specs=[spec, spec, pl.BlockSpec((N_EXPERTS, 1), lambda i: (0, 0))],
        out_specs=spec,
        out_shape=jax.ShapeDtypeStruct((TOP_K, T), I32),
        compiler_params=_cparams(("arbitrary",)),
        name="moe_dest_rows",
    )(idx, rank, starts.reshape(N_EXPERTS, 1))


def _dispatch_kernel(dest_ref, h_ref, xs_ref, sem):
    tm = h_ref.shape[0]

    def row_copies(t):
        src = h_ref.at[pl.ds(t, 1), :]
        return [pltpu.make_async_copy(src, xs_ref.at[pl.ds(dest_ref[t * TOP_K + kk], 1), :], sem)
                for kk in range(TOP_K)]

    def start(t, carry):
        for cp in row_copies(t):
            cp.start()
        return carry

    def wait(t, carry):
        for cp in row_copies(t):
            cp.wait()
        return carry

    lax.fori_loop(0, tm, start, 0)
    lax.fori_loop(0, tm, wait, 0)


def _dispatch(h2, dest_flat):
    T = h2.shape[0]
    tm = 256
    return pl.pallas_call(
        _dispatch_kernel,
        grid=(T // tm,),
        in_specs=[pl.BlockSpec((tm * TOP_K,), lambda i: (i,), memory_space=pltpu.SMEM),
                  pl.BlockSpec((tm, D_MODEL), lambda i: (i, 0))],
        out_specs=pl.BlockSpec(memory_space=pl.ANY),
        out_shape=jax.ShapeDtypeStruct((T * TOP_K, D_MODEL), F32),
        scratch_shapes=[pltpu.SemaphoreType.DMA(())],
        compiler_params=_cparams(("arbitrary",)),
        name="moe_dispatch",
    )(dest_flat, h2)


def _expert_kernel(blk_ref, exp_ref, lo_ref, hi_ref, first_ref, x_ref, w1_ref, w3_ref, w2_ref, o_ref):
    w = pl.program_id(0)
    x = x_ref[...].astype(BF16)
    a = _dot(x, w1_ref[...].astype(BF16))
    b = _dot(x, w3_ref[...].astype(BF16))
    y = _dot((_silu(a) * b).astype(BF16), w2_ref[...].astype(BF16))
    r = _iota(y.shape, 0)
    y = jnp.where((r >= lo_ref[w]) & (r < hi_ref[w]), y, 0.0)

    @pl.when(first_ref[w] == 1)
    def _():
        o_ref[...] = y

    @pl.when(first_ref[w] == 0)
    def _():
        o_ref[...] = o_ref[...] + y


def _experts(xs, work, w1, w3, w2, *, bm):
    A = xs.shape[0]
    n_work = work[0].shape[0]
    xmap = lambda w, blk, ex, lo, hi, first: (blk[w], 0)
    wmap = lambda w, blk, ex, lo, hi, first: (ex[w], 0, 0)
    return pl.pallas_call(
        _expert_kernel,
        grid_spec=pltpu.PrefetchScalarGridSpec(
            num_scalar_prefetch=5,
            grid=(n_work,),
            in_specs=[pl.BlockSpec((bm, D_MODEL), xmap),
                      pl.BlockSpec((None, D_MODEL, D_EXPERT), wmap),
                      pl.BlockSpec((None, D_MODEL, D_EXPERT), wmap),
                      pl.BlockSpec((None, D_EXPERT, D_MODEL), wmap)],
            out_specs=pl.BlockSpec((bm, D_MODEL), xmap)),
        out_shape=jax.ShapeDtypeStruct((A, D_MODEL), F32),
        compiler_params=_cparams(("arbitrary",)),
        name="moe_experts",
    )(*work, xs, w1, w3, w2)


def _combine_kernel(dest_ref, ys_ref, wt_ref, h_ref, x_ref, mod_ref, s1_ref, s3_ref, s2_ref, fg_ref,
                    o_ref, buf, sem, *, final_norm):
    tc = h_ref.shape[0]

    def row_copies(t):
        return [pltpu.make_async_copy(ys_ref.at[pl.ds(dest_ref[t * TOP_K + kk], 1), :],
                                      buf.at[kk, pl.ds(t, 1), :], sem)
                for kk in range(TOP_K)]

    def start(t, carry):
        for cp in row_copies(t):
            cp.start()
        return carry

    def wait(t, carry):
        for cp in row_copies(t):
            cp.wait()
        return carry

    lax.fori_loop(0, tc, start, 0)
    hb = h_ref[...].astype(BF16)
    mid = _silu(_dot(hb, s1_ref[...])) * _dot(hb, s3_ref[...])
    y = _dot(mid.astype(BF16), s2_ref[...])
    lax.fori_loop(0, tc, wait, 0)
    for kk in range(TOP_K):
        y = y + wt_ref[:, kk:kk + 1] * buf[kk]
    x = x_ref[...] + mod_ref[5:6, :] * y
    if final_norm:
        x = x * lax.rsqrt(jnp.mean(x * x, axis=-1, keepdims=True) + EPS) * fg_ref[...]
    o_ref[...] = x


def _combine(ys, dest_flat, wt, h2, x, mod, s1, s3, s2, final_g, *, tokens_per_mod, final_norm):
    T = x.shape[0]
    tc = 128
    tiles_per_mod = tokens_per_mod // tc
    xspec = pl.BlockSpec((tc, D_MODEL), lambda i: (i, 0))
    return pl.pallas_call(
        functools.partial(_combine_kernel, final_norm=final_norm),
        grid=(T // tc,),
        in_specs=[pl.BlockSpec((tc * TOP_K,), lambda i: (i,), memory_space=pltpu.SMEM),
                  pl.BlockSpec(memory_space=pl.ANY),
                  pl.BlockSpec((tc, LANES), lambda i: (i, 0)),
                  xspec, xspec,
                  pl.BlockSpec((None, 6, D_MODEL), lambda i: (i // tiles_per_mod, 0, 0)),
                  pl.BlockSpec((D_MODEL, D_SHARED), lambda i: (0, 0)),
                  pl.BlockSpec((D_MODEL, D_SHARED), lambda i: (0, 0)),
                  pl.BlockSpec((D_SHARED, D_MODEL), lambda i: (0, 0)),
                  pl.BlockSpec((1, D_MODEL), lambda i: (0, 0))],
        out_specs=xspec,
        out_shape=jax.ShapeDtypeStruct((T, D_MODEL), F32),
        scratch_shapes=[pltpu.VMEM((TOP_K, tc, D_MODEL), F32), pltpu.SemaphoreType.DMA(())],
        compiler_params=_cparams(("arbitrary",)),
        name="moe_combine",
    )(dest_flat, ys, wt, h2, x, mod, s1, s3, s2, final_g.reshape(1, D_MODEL))


def _work_list(counts, n_rows, bm):
    E = N_EXPERTS
    n_blocks = n_rows // bm
    n_work = n_blocks + E - 1
    ends = jnp.cumsum(counts)
    starts = ends - counts
    first_blk = starts // bm
    last_blk = jnp.where(counts > 0, (ends - 1) // bm, first_blk)
    n_items = jnp.where(counts > 0, last_blk - first_blk + 1, 0)
    item_end = jnp.cumsum(n_items)
    item_start = item_end - n_items
    w = jnp.arange(n_work, dtype=I32)
    ex = jnp.minimum(jnp.searchsorted(item_end, w, side="right"), E - 1).astype(I32)
    valid = w < item_end[-1]
    blk = jnp.where(valid, first_blk[ex] + (w - item_start[ex]), n_blocks - 1).astype(I32)
    lo = jnp.clip(starts[ex] - blk * bm, 0, bm)
    hi = jnp.clip(ends[ex] - blk * bm, 0, bm)
    lo = jnp.where(valid, lo, 0).astype(I32)
    hi = jnp.where(valid, hi, 0).astype(I32)
    ex = jnp.where(valid, ex, ex[jnp.maximum(item_end[-1] - 1, 0)])
    prev_blk = jnp.concatenate([jnp.full((1,), -1, I32), blk[:-1]])
    first = (blk != prev_blk).astype(I32)
    return blk, ex.astype(I32), lo, hi, first


def _moe(h2, x, mod, p, final_g, *, tokens_per_mod, final_norm):
    T = x.shape[0]
    idx, wt, rank, counts = _router(h2, p["router_w"], p["router_b"])
    starts = jnp.cumsum(counts) - counts
    dest = _dest_rows(idx, rank, starts).T.reshape(T * TOP_K)
    wt = jnp.pad(wt.T, ((0, 0), (0, LANES - TOP_K)))
    xs = _dispatch(h2, dest)
    bm = 512
    work = _work_list(counts, T * TOP_K, bm)
    ys = _experts(xs, work, p["exp_w1"], p["exp_w3"], p["exp_w2"], bm=bm)
    return _combine(ys, dest, wt, h2, x, mod, p["sh_w1"], p["sh_w3"], p["sh_w2"], final_g,
                    tokens_per_mod=tokens_per_mod, final_norm=final_norm)


def _reorder_w_in(w_in):
    plain = D_CONV
    pieces = [w_in[..., :plain],
              w_in[..., plain + 3616:plain + 3616 + 4096],
              w_in[..., plain:plain + 3072],
              w_in[..., plain + 3088:plain + 3600],
              w_in[..., plain + 3072:plain + 3088],
              w_in[..., plain + 3600:plain + 3616]]
    out = jnp.concatenate(pieces, axis=-1)
    pad = D_PROJ - out.shape[-1]
    return jnp.pad(out, ((0, 0), (0, 0), (0, pad))).astype(BF16)


def _small_rows(ssd_vals, gdn_vals):
    row = jnp.zeros((LANES,), F32)
    row = row.at[S_DT:S_DT + 2 * SSD_HEADS].set(ssd_vals.reshape(-1))
    row = row.at[S_GA:S_GA + 2 * GDN_HEADS].set(gdn_vals.reshape(-1))
    return row.reshape(1, LANES)


def _layer_pass(x, mod, lp, hy, st_hg, st_ssd, st_gdn, final_g, *, B, L, tokens_per_mod, seg, layer, final_norm):
    T = B * L
    proj = _in_projection(x, mod, lp["norm1_g"], lp["w_in"], lp["conv_w"], lp["conv_b"],
                          tokens_per_mod=tokens_per_mod, seg=seg)
    proj3 = proj.reshape(B, L, D_PROJ)
    fmat, gmat, hspec = hy
    cb = lambda c: c // D_BRANCH
    z = _spectral_conv(fmat, gmat, hspec, 0, lp["hy_bias"][0], proj3, cb(C_HYV), proj3, cb(C_HYX1))
    ya = _spectral_conv(fmat, gmat, hspec, 1, lp["hy_bias"][1], z, 0, proj3, cb(C_HYX2))
    yb, s_hg = _hgrn_scan(proj3, lp["hg_lb"], lp["hg_norm"], st_hg, layer=layer)
    yc, s_ssd = _ssd_scan(proj3, lp["bias_row"], lp["alog_row"], lp["ssd_d"], st_ssd)
    yd, s_gdn = _gdn_scan(proj3, lp["bias_row"], lp["alog_row"], lp["gdn_norm"], st_gdn)
    flat = lambda a: a.reshape(T, D_BRANCH)
    x, h2 = _merge(flat(ya), flat(yb), flat(yc), flat(yd), proj, x, mod, lp["ssd_norm"], lp["norm2_g"],
                   lp["w_branch"], lp["w_out"], tokens_per_mod=tokens_per_mod)
    x = _moe(h2, x, mod, lp, final_g, tokens_per_mod=tokens_per_mod, final_norm=final_norm)
    return x, s_hg, s_ssd, s_gdn


def kernel(x_prompt, x_sample, state_hgrn, state_ssd, state_gdn, c, c_ctx, norm1_g, norm2_g, ada_w, ada_b, w_in, conv_w, conv_b, hy_w1, hy_b1, hy_w2, hy_b2, hy_w3, hy_bias, hg_lb, hg_norm, ssd_a_log, ssd_dt_bias, ssd_d, ssd_norm, gdn_a_log, gdn_dt_bias, gdn_norm, w_branch, w_out, router_w, router_bias, exp_w1, exp_w3, exp_w2, sh_w1, sh_w3, sh_w2, final_g):
    depth = w_in.shape[0]
    bp, lp_len = x_prompt.shape[0], x_prompt.shape[1]
    bs, ls_len = x_sample.shape[0], x_sample.shape[1]
    D = D_MODEL

    w_in_r = _reorder_w_in(w_in)
    hy_w1p = jnp.pad(hy_w1, ((0, 0), (0, LANES - hy_w1.shape[1]), (0, 0)))
    layers = []
    for l in range(depth):
        layers.append(dict(
            norm1_g=norm1_g[l], norm2_g=norm2_g[l], w_in=w_in_r[l], conv_w=conv_w[l], conv_b=conv_b[l],
            hy_bias=hy_bias[l], hg_lb=hg_lb, hg_norm=hg_norm[l],
            bias_row=_small_rows(ssd_dt_bias[l], gdn_dt_bias[l]),
            alog_row=_small_rows(ssd_a_log[l], gdn_a_log[l]),
            ssd_d=ssd_d[l], ssd_norm=ssd_norm[l], gdn_norm=gdn_norm[l],
            w_branch=w_branch[l].astype(BF16), w_out=w_out[l].astype(BF16),
            router_w=router_w[l], router_b=router_bias[l],
            exp_w1=exp_w1[l], exp_w3=exp_w3[l], exp_w2=exp_w2[l],
            sh_w1=sh_w1[l].astype(BF16), sh_w3=sh_w3[l].astype(BF16), sh_w2=sh_w2[l].astype(BF16)))

    def hyena_setup(L):
        fmat, fs = _dft_matrices(L)
        gmat = fs.T
        specs = []
        for l in range(depth):
            filt = _hyena_filters(L, hy_w1p[l], hy_b1[l], hy_w2[l], hy_b2[l], hy_w3[l])
            specs.append(_filter_spectrum(fmat, filt))
        return fmat, gmat, specs

    cond = jnp.concatenate([c_ctx.reshape(1, D), c], axis=0)
    rows = cond.shape[0]
    rows8 = (rows + 7) // 8 * 8
    cond8 = jnp.pad(cond, ((0, rows8 - rows), (0, 0)))
    mods = [_modulation(cond8, ada_w[l], ada_b[l]).reshape(rows8, 6, D) for l in range(depth)]

    fmat, gmat, specs = hyena_setup(lp_len)
    x = x_prompt.reshape(bp * lp_len, D)
    z_hg = jnp.zeros((bp, 2, HG_HEADS, HG_DK, HG_DK), F32)
    z_ssd = jnp.zeros((bp, 2, SSD_HEADS, SSD_P, SSD_N), F32)
    z_gdn = jnp.zeros((bp, 2, GDN_HEADS, GDN_DK, GDN_DK), F32)
    hg_states, ssd_states, gdn_states = [], [], []
    for l in range(depth):
        x, s_hg, s_ssd, s_gdn = _layer_pass(
            x, mods[l][0:1], layers[l], (fmat, gmat, specs[l]), z_hg, z_ssd, z_gdn, final_g,
            B=bp, L=lp_len, tokens_per_mod=bp * lp_len, seg=lp_len, layer=l, final_norm=(l == depth - 1))
        hg_states.append(s_hg)
        ssd_states.append(s_ssd)
        gdn_states.append(s_gdn)
    y_prompt = x.reshape(bp, lp_len, D)
    new_hg = jnp.stack(hg_states, axis=1)
    new_ssd = jnp.stack(ssd_states, axis=1)
    new_gdn = jnp.stack(gdn_states, axis=1)

    fmat, gmat, specs = hyena_setup(ls_len)
    x = x_sample.reshape(bs * ls_len, D)
    for l in range(depth):
        x, _, _, _ = _layer_pass(
            x, mods[l][1:1 + bs], layers[l], (fmat, gmat, specs[l]),
            state_hgrn[:, l], state_ssd[:, l], state_gdn[:, l], final_g,
            B=bs, L=ls_len, tokens_per_mod=ls_len, seg=GRID_W, layer=l, final_norm=(l == depth - 1))
    y_sample = x.reshape(bs, ls_len, D)
    return (y_prompt, y_sample, new_hg, new_ssd, new_gdn)
```

```python
import functools
import math

import numpy as np
import jax
import jax.numpy as jnp
from jax import lax
from jax.experimental import pallas as pl
from jax.experimental.pallas import tpu as pltpu

F32 = jnp.float32
BF16 = jnp.bfloat16
I32 = jnp.int32
HIGHEST = lax.Precision.HIGHEST

D_MODEL = 1024
GRID_W = 64
EPS = 1e-6
LOG_FLOOR = 1e-30
MASK_NEG = -1e30
N_BRANCH = 4
D_BRANCH = 512
HY_POS_FREQS = 16
HY_FILTER_HIDDEN = 64
HY_FAST_DECAY = 0.3
HY_SLOW_DECAY = 1.5
HY_DECAY_TARGET = 1e-2
HG_HEADS = 4
HG_DK = 128
HG_CHUNK = 16
SSD_HEADS = 8
SSD_P = 64
SSD_N = 128
SSD_CHUNK = 64
GDN_HEADS = 4
GDN_DK = 128
GDN_CHUNK = 64
N_EXPERTS = 64
TOP_K = 8
N_EXPERT_GROUPS = 8
TOPK_GROUPS = 4
D_EXPERT = 256
D_SHARED = 256
ROUTED_SCALE = 2.5

LANES = 128
RUN_ROWS = 8

D_CONV = 4096
C_HYV, C_HYX1, C_HYX2 = 0, 512, 1024
C_SSDX, C_SSDB, C_SSDC = 1536, 2048, 2304
C_GQ, C_GK, C_GV = 2560, 3072, 3584
P_MERGE = 4096
P_HGQ, P_HGFF, P_HGFB, P_HGV, P_HGG = 8192, 8704, 9216, 9728, 10240
P_SSDZ, P_GDNG, P_SMALL = 10752, 11264, 11776
D_PROJ = 12288
S_DT, S_GA, S_GB = 0, 16, 24

VMEM_LIMIT = 56 * 1024 * 1024


def _cparams(sem):
    return pltpu.CompilerParams(dimension_semantics=sem, vmem_limit_bytes=VMEM_LIMIT)


def _sigmoid(x):
    return 1.0 / (1.0 + jnp.exp(-x))


def _silu(x):
    return x * _sigmoid(x)


def _softplus(x):
    return jnp.maximum(x, 0.0) + jnp.log(1.0 + jnp.exp(-jnp.abs(x)))


def _dot(a, b, precision=None):
    return jnp.dot(a, b, preferred_element_type=F32, precision=precision)


def _dot_nt(a, b, precision=None):
    return lax.dot_general(a, b, (((1,), (1,)), ((), ())), preferred_element_type=F32, precision=precision)


def _dot_tn(a, b, precision=None):
    return lax.dot_general(a, b, (((0,), (0,)), ((), ())), preferred_element_type=F32, precision=precision)


def _bdot(a, b):
    return _dot(a.astype(BF16), b.astype(BF16))


def _bdot_nt(a, b):
    return _dot_nt(a.astype(BF16), b.astype(BF16))


def _bdot_tn(a, b):
    return _dot_tn(a.astype(BF16), b.astype(BF16))


def _iota(shape, dim):
    return lax.broadcasted_iota(I32, shape, dim)


def _cumsum_rows(g, reverse):
    n = g.shape[0]
    row = _iota(g.shape, 0)
    sh = 1
    while sh < n:
        if reverse:
            g = g + jnp.where(row < n - sh, pltpu.roll(g, n - sh, 0), 0.0)
        else:
            g = g + jnp.where(row >= sh, pltpu.roll(g, sh, 0), 0.0)
        sh *= 2
    return g


def _lane_pick(a, j):
    return jnp.sum(jnp.where(_iota(a.shape, 1) == j, a, 0.0), axis=1, keepdims=True)


def _split3(a):
    a1 = a.astype(BF16)
    r1 = a - a1.astype(F32)
    a2 = r1.astype(BF16)
    a3 = (r1 - a2.astype(F32)).astype(BF16)
    return a1, a2, a3


def _dot3(a, b):
    a1, a2, _ = _split3(a)
    b1, b2, _ = _split3(b)
    return _dot(a1, b1) + (_dot(a1, b2) + _dot(a2, b1))


def _row_pick(a, j):
    sel = (_iota((8, a.shape[1]), 1) == j).astype(BF16)
    a1, a2, a3 = _split3(a)
    return (_dot_nt(sel, a1) + (_dot_nt(sel, a2) + _dot_nt(sel, a3)))[0:1, :]


def _tri_mask(n, reverse, strict=False):
    t = _iota((n, n), 0)
    s = _iota((n, n), 1)
    if reverse:
        return (s > t) if strict else (s >= t)
    return (s < t) if strict else (s <= t)


def _masked_exp(mask, diff):
    return jnp.where(mask, jnp.exp(jnp.where(mask, diff, 0.0)), 0.0)


def _mod_kernel(c_ref, w_ref, b_ref, o_ref):
    o_ref[...] = _dot(_silu(c_ref[...]), w_ref[...], precision=HIGHEST) + b_ref[...]


def _modulation(cond8, ada_w, ada_b):
    rows = cond8.shape[0]
    tn = 1536
    n = ada_w.shape[1]
    return pl.pallas_call(
        _mod_kernel,
        grid=(n // tn,),
        in_specs=[pl.BlockSpec((rows, D_MODEL), lambda j: (0, 0)),
                  pl.BlockSpec((D_MODEL, tn), lambda j: (0, j)),
                  pl.BlockSpec((1, tn), lambda j: (0, j))],
        out_specs=pl.BlockSpec((rows, tn), lambda j: (0, j)),
        out_shape=jax.ShapeDtypeStruct((rows, n), F32),
        compiler_params=_cparams(("arbitrary",)),
        name="adaln_mod",
    )(cond8, ada_w, ada_b.reshape(1, n))


def _inproj_kernel(x_ref, mod_ref, g_ref, w_ref, cw_ref, cb_ref, o_ref, h_scr, *, seg, n_conv_tiles):
    j = pl.program_id(1)

    @pl.when(j == 0)
    def _():
        x = x_ref[...]
        xn = x * lax.rsqrt(jnp.mean(x * x, axis=-1, keepdims=True) + EPS) * g_ref[...]
        m = mod_ref[...]
        h_scr[...] = (xn * (1.0 + m[1:2]) + m[0:1]).astype(BF16)

    y = _dot(h_scr[...], w_ref[...])

    @pl.when(j < n_conv_tiles)
    def _():
        tm = y.shape[0]
        pos = _iota(y.shape, 0) & (seg - 1)
        prev = jnp.where(pos == 0, 0.0, pltpu.roll(y, 1, 0))
        nxt = jnp.where(pos == seg - 1, 0.0, pltpu.roll(y, tm - 1, 0))
        cw = cw_ref[...]
        o_ref[...] = cb_ref[...] + prev * cw[0:1] + y * cw[1:2] + nxt * cw[2:3]

    @pl.when(j >= n_conv_tiles)
    def _():
        o_ref[...] = y


def _in_projection(x, mod, norm_g, w_bf16, conv_w, conv_b, *, tokens_per_mod, seg):
    T = x.shape[0]
    tm = min(1024, tokens_per_mod)
    tn = 1024
    n_conv_tiles = D_CONV // tn
    tiles_per_mod = tokens_per_mod // tm
    kern = functools.partial(_inproj_kernel, seg=seg, n_conv_tiles=n_conv_tiles)
    cmap = lambda i, j: (0, jnp.minimum(j, n_conv_tiles - 1))
    return pl.pallas_call(
        kern,
        grid=(T // tm, D_PROJ // tn),
        in_specs=[pl.BlockSpec((tm, D_MODEL), lambda i, j: (i, 0)),
                  pl.BlockSpec((None, 6, D_MODEL), lambda i, j: (i // tiles_per_mod, 0, 0)),
                  pl.BlockSpec((1, D_MODEL), lambda i, j: (0, 0)),
                  pl.BlockSpec((D_MODEL, tn), lambda i, j: (0, j)),
                  pl.BlockSpec((3, tn), cmap),
                  pl.BlockSpec((1, tn), cmap)],
        out_specs=pl.BlockSpec((tm, tn), lambda i, j: (i, j)),
        out_shape=jax.ShapeDtypeStruct((T, D_PROJ), F32),
        scratch_shapes=[pltpu.VMEM((tm, D_MODEL), BF16)],
        compiler_params=_cparams(("arbitrary", "arbitrary")),
        name="in_proj",
    )(x, mod, norm_g.reshape(1, D_MODEL), w_bf16, conv_w, conv_b.reshape(1, D_CONV))


def _hyfilt_kernel(w1_ref, b1_ref, w2_ref, b2_ref, w3_ref, o_ref, *, L):
    i = pl.program_id(1)
    tl = o_ref.shape[0]
    t = (_iota((tl, LANES), 0) + i * tl).astype(F32) / L
    lane = _iota((tl, LANES), 1)
    band = jnp.where(lane <= HY_POS_FREQS, lane, lane - HY_POS_FREQS).astype(F32)
    ang = 2.0 * math.pi * t * band
    feats = jnp.where(lane == 0, t,
                      jnp.where(lane <= HY_POS_FREQS, jnp.sin(ang),
                                jnp.where(lane <= 2 * HY_POS_FREQS, jnp.cos(ang), 0.0)))
    hdn = jnp.sin(_dot(feats, w1_ref[...], precision=HIGHEST) + b1_ref[...])
    hdn = jnp.sin(_dot(hdn, w2_ref[...], precision=HIGHEST) + b2_ref[...])
    filt = _dot(hdn, w3_ref[...], precision=HIGHEST)
    max_decay = math.log(HY_DECAY_TARGET) / HY_FAST_DECAY
    min_decay = math.log(HY_DECAY_TARGET) / HY_SLOW_DECAY
    n = filt.shape[1]
    ch = (_iota((tl, n), 1) & (D_BRANCH - 1)).astype(F32)
    delta = min_decay + ch * ((max_decay - min_decay) / (D_BRANCH - 1))
    tt = (_iota((tl, n), 0) + i * tl).astype(F32) / L
    o_ref[...] = filt * jnp.exp(-tt * jnp.abs(delta))


def _hyena_filters(L, w1p, b1, w2, b2, w3):
    tl = min(L, 256)
    n = 2 * D_BRANCH
    return pl.pallas_call(
        functools.partial(_hyfilt_kernel, L=L),
        grid=(2, L // tl),
        in_specs=[pl.BlockSpec((LANES, HY_FILTER_HIDDEN), lambda d, i: (0, 0)),
                  pl.BlockSpec((1, HY_FILTER_HIDDEN), lambda d, i: (0, 0)),
                  pl.BlockSpec((HY_FILTER_HIDDEN, HY_FILTER_HIDDEN), lambda d, i: (0, 0)),
                  pl.BlockSpec((1, HY_FILTER_HIDDEN), lambda d, i: (0, 0)),
                  pl.BlockSpec((HY_FILTER_HIDDEN, n), lambda d, i: (0, d))],
        out_specs=pl.BlockSpec((None, tl, n), lambda d, i: (d, i, 0)),
        out_shape=jax.ShapeDtypeStruct((2, L, n), F32),
        compiler_params=_cparams(("arbitrary", "arbitrary")),
        name="hyena_filters",
    )(w1p, b1.reshape(1, -1), w2, b2.reshape(1, -1), w3)


def _dftgen_kernel(f_ref, fs_ref, *, L, tk):
    i = pl.program_id(0)
    N = 2 * L
    k = _iota((tk, LANES), 0) + i * tk
    lane = _iota((tk, LANES), 1)
    w = 2.0 * math.pi / N
    a0 = ((k * lane) & (N - 1)).astype(F32) * w
    c0, s0 = jnp.cos(a0), jnp.sin(a0)
    a1 = ((k * lane * LANES) & (N - 1)).astype(F32) * w
    c1, s1 = jnp.cos(a1), jnp.sin(a1)
    alt = jnp.where((lane & 1) == 0, 1.0, -1.0)
    coef = jnp.where(k == 0, 1.0 / N, 2.0 / N)
    for n1 in range(L // LANES):
        c1b = c1[:, n1:n1 + 1]
        s1b = s1[:, n1:n1 + 1]
        cosb = c1b * c0 - s1b * s0
        sinb = jnp.where(k == 0, alt, -(s1b * c0 + c1b * s0))
        cols = slice(n1 * LANES, (n1 + 1) * LANES)
        f_ref[0:tk, cols] = cosb.astype(BF16)
        f_ref[tk:2 * tk, cols] = sinb.astype(BF16)
        fs_ref[0:tk, cols] = (coef * cosb).astype(BF16)
        fs_ref[tk:2 * tk, cols] = (coef * sinb).astype(BF16)


def _dft_matrices(L):
    tk = min(L, 256)
    spec = pl.BlockSpec((2 * tk, L), lambda i: (i, 0))
    return pl.pallas_call(
        functools.partial(_dftgen_kernel, L=L, tk=tk),
        grid=(L // tk,),
        in_specs=[],
        out_specs=[spec, spec],
        out_shape=[jax.ShapeDtypeStruct((2 * L, L), BF16)] * 2,
        compiler_params=_cparams(("arbitrary",)),
        name="dft_matrices",
    )()


def _dfth_kernel(f_ref, h_ref, o_ref, hp_scr, *, tk):
    i = pl.program_id(1)
    n = o_ref.shape[1]

    @pl.when(i == 0)
    def _():
        hc = h_ref[0]
        ha = jnp.where(_iota(hc.shape, 0) == 0, 0.0, h_ref[1])
        hp_scr[:, 0:n] = (hc + ha).astype(BF16)
        hp_scr[:, n:2 * n] = (hc - ha).astype(BF16)

    u = _dot(f_ref[...], hp_scr[...])
    r = _iota((2 * tk, n), 0)
    from_sum = (r < tk) | ((r == tk) & (i == 0))
    o_ref[...] = jnp.where(from_sum, u[:, 0:n], u[:, n:2 * n])


def _filter_spectrum(fmat, filt):
    L = fmat.shape[1]
    tk = min(L, 256)
    C = filt.shape[2]
    tc = 256
    return pl.pallas_call(
        functools.partial(_dfth_kernel, tk=tk),
        grid=(C // tc, L // tk),
        in_specs=[pl.BlockSpec((2 * tk, L), lambda c, i: (i, 0)),
                  pl.BlockSpec((2, L, tc), lambda c, i: (0, 0, c))],
        out_specs=pl.BlockSpec((2 * tk, tc), lambda c, i: (i, c)),
        out_shape=jax.ShapeDtypeStruct((2 * L, C), F32),
        scratch_shapes=[pltpu.VMEM((L, 2 * tc), BF16)],
        compiler_params=_cparams(("arbitrary", "arbitrary")),
        name="filter_spectrum",
    )(fmat, filt)


def _dfta_kernel(f_ref, u_ref, h_ref, y_ref, u_scr, *, tk):
    i = pl.program_id(1)

    @pl.when(i == 0)
    def _():
        u_scr[...] = u_ref[...].astype(BF16)

    uf = _dot(f_ref[...], u_scr[...])
    ur, ui = uf[0:tk], uf[tk:2 * tk]
    hr, hi = h_ref[0:tk, :], h_ref[tk:2 * tk, :]
    dc = (_iota(ur.shape, 0) == 0) & (i == 0)
    y_ref[0:tk, :] = jnp.where(dc, ur * hr, ur * hr - ui * hi).astype(BF16)
    y_ref[tk:2 * tk, :] = jnp.where(dc, ui * hi, ur * hi + ui * hr).astype(BF16)


def _dftb_kernel(g_ref, y_ref, u_ref, x_ref, b_ref, o_ref):
    y = _dot(g_ref[...], y_ref[...])
    o_ref[...] = x_ref[...] * (y + u_ref[...] * b_ref[...])


def _spectral_conv(fmat, gmat, hspec, h_col, bias, u_arr, u_col, gate_arr, gate_col):
    B, L = u_arr.shape[0], u_arr.shape[1]
    C = D_BRANCH
    tk = min(L, 256)
    y = pl.pallas_call(
        functools.partial(_dfta_kernel, tk=tk),
        grid=(B, L // tk),
        in_specs=[pl.BlockSpec((2 * tk, L), lambda b, i: (i, 0)),
                  pl.BlockSpec((None, L, C), lambda b, i: (b, 0, u_col)),
                  pl.BlockSpec((2 * tk, C), lambda b, i: (i, h_col))],
        out_specs=pl.BlockSpec((None, 2 * tk, C), lambda b, i: (b, i, 0)),
        out_shape=jax.ShapeDtypeStruct((B, 2 * L, C), BF16),
        scratch_shapes=[pltpu.VMEM((L, C), BF16)],
        compiler_params=_cparams(("arbitrary", "arbitrary")),
        name="hyena_dft_fwd",
    )(fmat, u_arr, hspec)
    tr = min(L, 256)
    return pl.pallas_call(
        _dftb_kernel,
        grid=(B, L // tr),
        in_specs=[pl.BlockSpec((tr, 2 * L), lambda b, i: (i, 0)),
                  pl.BlockSpec((None, 2 * L, C), lambda b, i: (b, 0, 0)),
                  pl.BlockSpec((None, tr, C), lambda b, i: (b, i, u_col)),
                  pl.BlockSpec((None, tr, C), lambda b, i: (b, i, gate_col)),
                  pl.BlockSpec((1, C), lambda b, i: (0, 0))],
        out_specs=pl.BlockSpec((None, tr, C), lambda b, i: (b, i, 0)),
        out_shape=jax.ShapeDtypeStruct((B, L, C), F32),
        compiler_params=_cparams(("arbitrary", "arbitrary")),
        name="hyena_dft_inv",
    )(gmat, y, u_arr, gate_arr, bias.reshape(1, C))


def _hgrn_kernel(q_ref, ff_ref, fb_ref, v_ref, g_ref, lb_ref, nrm_ref, s0_ref, y_ref, sf_ref,
                 of_scr, ob_scr, st_scr, *, L, layer, depth):
    C = HG_CHUNK
    W = HG_DK
    nc = L // C
    nh = q_ref.shape[1] // W
    ridx = _iota((C, W), 0)
    o_scrs = (of_scr, ob_scr)
    f_refs = (ff_ref, fb_ref)

    def lower_bound(d, cols):
        rows = [lb_ref[d, l:l + 1, cols] for l in range(depth)]
        m = rows[0]
        for r in rows[1:]:
            m = jnp.maximum(m, r)
        es = [jnp.exp(r - m) for r in rows]
        tot = es[0]
        for e in es[1:]:
            tot = tot + e
        acc = es[0] / tot
        for e in es[1:layer + 1]:
            acc = acc + e / tot
        return acc - es[0] / tot

    lbs = [[lower_bound(d, slice(hh * W, (hh + 1) * W)) for hh in range(nh)] for d in (0, 1)]
    for d in (0, 1):
        for hh in range(nh):
            st_scr[d, hh] = s0_ref[d, hh].T

    def step(c, d, hh):
        rev = d == 1
        rows = pl.ds(pl.multiple_of(c * C, C), C)
        cols = slice(hh * W, (hh + 1) * W)
        lb = lbs[d][hh]
        q = _silu(q_ref[rows, cols])
        uf = f_refs[d][rows, cols]
        v = v_ref[rows, cols]
        f = lb + (1.0 - lb) * _sigmoid(uf)
        g = jnp.log(jnp.maximum(f, LOG_FLOOR))
        kin = (1.0 - lb) * _sigmoid(-uf)
        b = _cumsum_rows(g, rev)
        st = st_scr[d, hh]
        o = _bdot_nt(q * jnp.exp(b), st)
        intra = jnp.zeros((C, W), F32)
        for t in range(C):
            mask = (ridx >= t) if rev else (ridx <= t)
            pair = _masked_exp(mask, b[t:t + 1, :] - b)
            a = jnp.sum(pair * (q[t:t + 1, :] * kin), axis=1, keepdims=True)
            row = jnp.sum(a * v, axis=0, keepdims=True)
            intra = jnp.where(ridx == t, row, intra)
        b_end = b[0:1, :] if rev else b[C - 1:C, :]
        st_scr[d, hh] = st * jnp.exp(b_end) + _bdot_tn(v, kin * jnp.exp(b_end - b))
        o_scrs[d][rows, cols] = o + intra

    def body(ci, carry):
        for d in (0, 1):
            for hh in range(nh):
                step((nc - 1 - ci) if d == 1 else ci, d, hh)
        return carry

    lax.fori_loop(0, nc, body, 0)
    for d in (0, 1):
        for hh in range(nh):
            sf_ref[d, hh] = st_scr[d, hh].T

    tr = min(L, 256)

    def fin(i, carry):
        rows = pl.ds(pl.multiple_of(i * tr, tr), tr)
        for hh in range(nh):
            cols = slice(hh * W, (hh + 1) * W)
            o = of_scr[rows, cols] + ob_scr[rows, cols]
            on = o * lax.rsqrt(jnp.mean(o * o, axis=-1, keepdims=True) + EPS) * nrm_ref[...]
            y_ref[rows, cols] = on * _silu(g_ref[rows, cols])
        return carry

    lax.fori_loop(0, L // tr, fin, 0)


def _hgrn_scan(proj3, hg_lb, hg_norm, s0, *, layer):
    B, L = proj3.shape[0], proj3.shape[1]
    depth = hg_lb.shape[1]
    nh = 2
    W = nh * HG_DK
    col = lambda base: (lambda b, h: (b, 0, base // W + h))
    st_spec = pl.BlockSpec((None, 2, nh, HG_DK, HG_DK), lambda b, h: (b, 0, h, 0, 0))
    return pl.pallas_call(
        functools.partial(_hgrn_kernel, L=L, layer=layer, depth=depth),
        grid=(B, HG_HEADS // nh),
        in_specs=[pl.BlockSpec((None, L, W), col(P_HGQ)),
                  pl.BlockSpec((None, L, W), col(P_HGFF)),
                  pl.BlockSpec((None, L, W), col(P_HGFB)),
                  pl.BlockSpec((None, L, W), col(P_HGV)),
                  pl.BlockSpec((None, L, W), col(P_HGG)),
                  pl.BlockSpec((2, depth, W), lambda b, h: (0, 0, h)),
                  pl.BlockSpec((1, HG_DK), lambda b, h: (0, 0)),
                  st_spec],
        out_specs=[pl.BlockSpec((None, L, W), lambda b, h: (b, 0, h)), st_spec],
        out_shape=[jax.ShapeDtypeStruct((B, L, D_BRANCH), F32),
                   jax.ShapeDtypeStruct(s0.shape, F32)],
        scratch_shapes=[pltpu.VMEM((L, W), F32), pltpu.VMEM((L, W), F32),
                        pltpu.VMEM((2, nh, HG_DK, HG_DK), F32)],
        compiler_params=_cparams(("arbitrary", "arbitrary")),
        name="hgrn2_scan",
    )(proj3, proj3, proj3, proj3, proj3, hg_lb, hg_norm.reshape(1, HG_DK), s0)


def _ssd_kernel(x_ref, bm_ref, cm_ref, z_ref, sm_ref, bias_ref, alog_ref, dskip_ref, s0_ref,
                y_ref, sf_ref, of_scr, ob_scr, st_scr, *, L):
    C = SSD_CHUNK
    W = LANES
    nc = L // C
    npairs = x_ref.shape[1] // W
    grp = pl.program_id(1)
    lane_lo = _iota((C, W), 1) < SSD_P
    row_lo = _iota((W, SSD_N), 0) < SSD_P
    pick2 = lambda a0, a1: jnp.where(lane_lo, a0, a1)
    o_scrs = (of_scr, ob_scr)
    incls = (_tri_mask(C, False), _tri_mask(C, True))
    for d in (0, 1):
        for pp in range(npairs):
            st_scr[d, pp] = s0_ref[d, pp]

    heads = [(d, pp, hh) for d in (0, 1) for pp in range(npairs) for hh in (0, 1)]
    pairs = [(d, pp) for d in (0, 1) for pp in range(npairs)]

    def body(ci, carry):
        rows_d, bm, cm, dt_all, cum_all, gram = [], [], [], [], [], []
        for d in (0, 1):
            c = (nc - 1 - ci) if d == 1 else ci
            rows = pl.ds(pl.multiple_of(c * C, C), C)
            rows_d.append(rows)
            bm.append(_silu(bm_ref[rows, :]))
            cm.append(_silu(cm_ref[rows, :]))
            dt_all.append(_softplus(sm_ref[rows, :] + bias_ref[...]))
            cum_all.append(_cumsum_rows(-jnp.exp(alog_ref[...]) * dt_all[d], d == 1))
        lane_of = lambda u: S_DT + u[0] * SSD_HEADS + 2 * (npairs * grp + u[1]) + u[2]
        cumr = {u: _row_pick(cum_all[u[0]], lane_of(u)) for u in heads}
        for d in (0, 1):
            gram.append(_bdot_nt(cm[d], bm[d]))
        dtc = {u: _lane_pick(dt_all[u[0]], lane_of(u)) for u in heads}
        cumc = {u: _lane_pick(cum_all[u[0]], lane_of(u)) for u in heads}
        end = {u: (cumc[u][0:1, :] if u[0] == 1 else cumc[u][C - 1:C, :]) for u in heads}
        xs = {pr: _silu(x_ref[rows_d[pr[0]], slice(pr[1] * W, (pr[1] + 1) * W)]) for pr in pairs}
        st = {pr: st_scr[pr[0], pr[1]] for pr in pairs}
        y_state = {u: _bdot_nt(cm[u[0]] * jnp.exp(cumc[u]), st[u[:2]]) for u in heads}
        scores = {u: gram[u[0]] * _masked_exp(incls[u[0]], cumc[u] - cumr[u]) for u in heads}
        y_intra = {}
        for u in heads:
            head_x = jnp.where(lane_lo if u[2] == 0 else ~lane_lo, xs[u[:2]], 0.0) * dtc[u]
            y_intra[u] = _bdot(scores[u], head_x)
        for pr in pairs:
            u0, u1 = pr + (0,), pr + (1,)
            xdt_e = xs[pr] * pick2(dtc[u0] * jnp.exp(end[u0] - cumc[u0]), dtc[u1] * jnp.exp(end[u1] - cumc[u1]))
            st_scr[pr[0], pr[1]] = (st[pr] * jnp.where(row_lo, jnp.exp(end[u0]), jnp.exp(end[u1]))
                                    + _bdot_tn(xdt_e, bm[pr[0]]))
            o_scrs[pr[0]][rows_d[pr[0]], slice(pr[1] * W, (pr[1] + 1) * W)] = pick2(
                y_intra[u0] + y_state[u0], y_intra[u1] + y_state[u1])
        return carry

    lax.fori_loop(0, nc, body, 0)
    for d in (0, 1):
        for pp in range(npairs):
            sf_ref[d, pp] = st_scr[d, pp]

    tr = min(L, 256)

    def fin(i, carry):
        rows = pl.ds(pl.multiple_of(i * tr, tr), tr)
        y = of_scr[rows, :] + ob_scr[rows, :] + _silu(x_ref[rows, :]) * dskip_ref[...]
        y_ref[rows, :] = y * _silu(z_ref[rows, :])
        return carry

    lax.fori_loop(0, L // tr, fin, 0)


def _ssd_scan(proj3, bias_row, alog_row, ssd_d, s0):
    B, L = proj3.shape[0], proj3.shape[1]
    n_groups = 2
    n_pairs = SSD_HEADS // 2
    ppg = n_pairs // n_groups
    W = ppg * LANES
    s0p = s0.reshape(B, 2, n_pairs, 2 * SSD_P, SSD_N)
    dskip = jnp.repeat(ssd_d, SSD_P).reshape(n_groups, 1, W)
    col = lambda base: (lambda b, g: (b, 0, base // W + g))
    grp = lambda base: (lambda b, g: (b, 0, base // LANES + g))
    st_spec = pl.BlockSpec((None, 2, ppg, LANES, SSD_N), lambda b, g: (b, 0, g, 0, 0))
    y, sf = pl.pallas_call(
        functools.partial(_ssd_kernel, L=L),
        grid=(B, n_groups),
        in_specs=[pl.BlockSpec((None, L, W), col(C_SSDX)),
                  pl.BlockSpec((None, L, LANES), grp(C_SSDB)),
                  pl.BlockSpec((None, L, LANES), grp(C_SSDC)),
                  pl.BlockSpec((None, L, W), col(P_SSDZ)),
                  pl.BlockSpec((None, L, LANES), lambda b, g: (b, 0, P_SMALL // LANES)),
                  pl.BlockSpec((1, LANES), lambda b, g: (0, 0)),
                  pl.BlockSpec((1, LANES), lambda b, g: (0, 0)),
                  pl.BlockSpec((None, 1, W), lambda b, g: (g, 0, 0)),
                  st_spec],
        out_specs=[pl.BlockSpec((None, L, W), lambda b, g: (b, 0, g)), st_spec],
        out_shape=[jax.ShapeDtypeStruct((B, L, D_BRANCH), F32),
                   jax.ShapeDtypeStruct(s0p.shape, F32)],
        scratch_shapes=[pltpu.VMEM((L, W), F32), pltpu.VMEM((L, W), F32),
                        pltpu.VMEM((2, ppg, LANES, SSD_N), F32)],
        compiler_params=_cparams(("arbitrary", "arbitrary")),
        name="ssd_scan",
    )(proj3, proj3, proj3, proj3, proj3, bias_row, alog_row, dskip, s0p)
    return y, sf.reshape(s0.shape)


def _l2norm(a):
    return a * lax.rsqrt(jnp.sum(a * a, axis=-1, keepdims=True) + EPS)


def _gdn_kernel(q_ref, k_ref, v_ref, g_ref, sm_ref, bias_ref, alog_ref, nrm_ref, s0_ref,
                y_ref, sf_ref, of_scr, ob_scr, st_scr, *, L):
    C = GDN_CHUNK
    W = GDN_DK
    nc = L // C
    nh = q_ref.shape[1] // W
    hblk = pl.program_id(1)
    o_scrs = (of_scr, ob_scr)
    incls = (_tri_mask(C, False), _tri_mask(C, True))
    stricts = (_tri_mask(C, False, strict=True), _tri_mask(C, True, strict=True))
    for d in (0, 1):
        for hh in range(nh):
            st_scr[d, hh] = s0_ref[d, hh]

    units = [(d, hh) for d in (0, 1) for hh in range(nh)]
    eye = (_iota((C, C), 0) == _iota((C, C), 1)).astype(F32)

    def body(ci, carry):
        rows_d, cum_d, beta_d = [], [], []
        for d in (0, 1):
            c = (nc - 1 - ci) if d == 1 else ci
            rows = pl.ds(pl.multiple_of(c * C, C), C)
            raw = sm_ref[rows, :]
            rows_d.append(rows)
            cum_d.append(_cumsum_rows(-jnp.exp(alog_ref[...]) * _softplus(raw + bias_ref[...]), d == 1))
            beta_d.append(_sigmoid(raw))
        q, k, v, gc, grow, beta = {}, {}, {}, {}, {}, {}
        for u in units:
            d, hh = u
            cols = slice(hh * W, (hh + 1) * W)
            h = nh * hblk + hh
            q[u] = _l2norm(_silu(q_ref[rows_d[d], cols])) * (GDN_DK ** -0.5)
            k[u] = _l2norm(_silu(k_ref[rows_d[d], cols]))
            v[u] = _silu(v_ref[rows_d[d], cols])
            gc[u] = _lane_pick(cum_d[d], S_GA + d * GDN_HEADS + h)
            beta[u] = _lane_pick(beta_d[d], S_GB + d * GDN_HEADS + h)
        for u in units:
            grow[u] = _row_pick(cum_d[u[0]], S_GA + u[0] * GDN_HEADS + nh * hblk + u[1])
        decay = {u: _masked_exp(incls[u[0]], gc[u] - grow[u]) for u in units}
        kb = {u: k[u] * beta[u] for u in units}
        m = {u: -jnp.where(stricts[u[0]], _bdot_nt(kb[u], k[u]) * decay[u], 0.0) for u in units}
        aqk = {u: _bdot_nt(q[u], k[u]) * decay[u] for u in units}
        p = {u: eye + m[u] for u in units}
        sh = 2
        while sh < C:
            m = {u: _dot3(m[u], m[u]) for u in units}
            p = {u: p[u] + _dot3(p[u], m[u]) for u in units}
            sh *= 2
        uu = {u: _dot3(p[u], v[u] * beta[u]) for u in units}
        ww = {u: _dot3(p[u], kb[u] * jnp.exp(gc[u])) for u in units}
        st = {u: st_scr[u[0], u[1]] for u in units}
        v_new = {u: uu[u] - _bdot(ww[u], st[u]) for u in units}
        o = {u: _bdot(q[u] * jnp.exp(gc[u]), st[u]) + _bdot(aqk[u], v_new[u]) for u in units}
        for u in units:
            d, hh = u
            g_end = gc[u][0:1, :] if d == 1 else gc[u][C - 1:C, :]
            st_scr[d, hh] = st[u] * jnp.exp(g_end) + _bdot_tn(k[u] * jnp.exp(g_end - gc[u]), v_new[u])
            o_scrs[d][rows_d[d], slice(hh * W, (hh + 1) * W)] = o[u]
        return carry

    lax.fori_loop(0, nc, body, 0)
    for d in (0, 1):
        for hh in range(nh):
            sf_ref[d, hh] = st_scr[d, hh]

    tr = min(L, 256)

    def fin(i, carry):
        rows = pl.ds(pl.multiple_of(i * tr, tr), tr)
        for hh in range(nh):
            cols = slice(hh * W, (hh + 1) * W)
            o = of_scr[rows, cols] + ob_scr[rows, cols]
            on = o * lax.rsqrt(jnp.mean(o * o, axis=-1, keepdims=True) + EPS) * nrm_ref[...]
            y_ref[rows, cols] = on * _silu(g_ref[rows, cols])
        return carry

    lax.fori_loop(0, L // tr, fin, 0)


def _gdn_scan(proj3, bias_row, alog_row, gdn_norm, s0):
    B, L = proj3.shape[0], proj3.shape[1]
    nh = GDN_HEADS
    W = nh * GDN_DK
    col = lambda base: (lambda b, h: (b, 0, base // W + h))
    st_spec = pl.BlockSpec((None, 2, nh, GDN_DK, GDN_DK), lambda b, h: (b, 0, h, 0, 0))
    once = pl.Buffered(1)
    return pl.pallas_call(
        functools.partial(_gdn_kernel, L=L),
        grid=(B, GDN_HEADS // nh),
        in_specs=[pl.BlockSpec((None, L, W), col(C_GQ), pipeline_mode=once),
                  pl.BlockSpec((None, L, W), col(C_GK), pipeline_mode=once),
                  pl.BlockSpec((None, L, W), col(C_GV), pipeline_mode=once),
                  pl.BlockSpec((None, L, W), col(P_GDNG), pipeline_mode=once),
                  pl.BlockSpec((None, L, LANES), lambda b, h: (b, 0, P_SMALL // LANES), pipeline_mode=once),
                  pl.BlockSpec((1, LANES), lambda b, h: (0, 0)),
                  pl.BlockSpec((1, LANES), lambda b, h: (0, 0)),
                  pl.BlockSpec((1, GDN_DK), lambda b, h: (0, 0)),
                  st_spec],
        out_specs=[pl.BlockSpec((None, L, W), lambda b, h: (b, 0, h)), st_spec],
        out_shape=[jax.ShapeDtypeStruct((B, L, D_BRANCH), F32),
                   jax.ShapeDtypeStruct(s0.shape, F32)],
        scratch_shapes=[pltpu.VMEM((L, W), F32), pltpu.VMEM((L, W), F32),
                        pltpu.VMEM((2, nh, GDN_DK, GDN_DK), F32)],
        compiler_params=_cparams(("arbitrary", "arbitrary")),
        name="gdn_scan",
    )(proj3, proj3, proj3, proj3, proj3, bias_row, alog_row, gdn_norm.reshape(1, GDN_DK), s0)


def _merge_kernel(ya_ref, yb_ref, yc_ref, yd_ref, mg_ref, x_ref, mod_ref, sn_ref, n2_ref, wb_ref, wo_ref,
                  xo_ref, h_ref):
    yc = yc_ref[...]
    half = D_BRANCH // 2
    parts = []
    for gidx in range(2):
        seg = yc[:, gidx * half:(gidx + 1) * half]
        parts.append(seg * lax.rsqrt(jnp.mean(seg * seg, axis=-1, keepdims=True) + EPS)
                     * sn_ref[:, gidx * half:(gidx + 1) * half])
    branches = (ya_ref[...], yb_ref[...], None, yd_ref[...])
    mixed = None
    for n in range(N_BRANCH):
        gate = _sigmoid(mg_ref[:, n * D_MODEL:(n + 1) * D_MODEL])
        if n == 2:
            lifted = (_dot(parts[0].astype(BF16), wb_ref[n, 0:half, :])
                      + _dot(parts[1].astype(BF16), wb_ref[n, half:D_BRANCH, :]))
        else:
            lifted = _dot(branches[n].astype(BF16), wb_ref[n])
        mixed = gate * lifted if mixed is None else mixed + gate * lifted
    m = mod_ref[...]
    x = x_ref[...] + m[2:3] * _dot(mixed.astype(BF16), wo_ref[...])
    xo_ref[...] = x
    xn = x * lax.rsqrt(jnp.mean(x * x, axis=-1, keepdims=True) + EPS) * n2_ref[...]
    h_ref[...] = xn * (1.0 + m[4:5]) + m[3:4]


def _merge(ya, yb, yc, yd, proj, x, mod, ssd_norm, norm2_g, wb_bf16, wo_bf16, *, tokens_per_mod):
    T = x.shape[0]
    tm = 256
    tiles_per_mod = tokens_per_mod // tm
    yspec = pl.BlockSpec((tm, D_BRANCH), lambda i: (i, 0))
    xspec = pl.BlockSpec((tm, D_MODEL), lambda i: (i, 0))
    return pl.pallas_call(
        _merge_kernel,
        grid=(T // tm,),
        in_specs=[yspec, yspec, yspec, yspec,
                  pl.BlockSpec((tm, N_BRANCH * D_MODEL), lambda i: (i, P_MERGE // (N_BRANCH * D_MODEL))),
                  xspec,
                  pl.BlockSpec((None, 6, D_MODEL), lambda i: (i // tiles_per_mod, 0, 0)),
                  pl.BlockSpec((1, D_BRANCH), lambda i: (0, 0)),
                  pl.BlockSpec((1, D_MODEL), lambda i: (0, 0)),
                  pl.BlockSpec((N_BRANCH, D_BRANCH, D_MODEL), lambda i: (0, 0, 0)),
                  pl.BlockSpec((D_MODEL, D_MODEL), lambda i: (0, 0))],
        out_specs=[xspec, xspec],
        out_shape=[jax.ShapeDtypeStruct((T, D_MODEL), F32)] * 2,
        compiler_params=_cparams(("arbitrary",)),
        name="merge_outproj",
    )(ya, yb, yc, yd, proj, x, mod, ssd_norm.reshape(1, D_BRANCH), norm2_g.reshape(1, D_MODEL), wb_bf16, wo_bf16)


def _router_kernel(h_ref, rw_ref, rb_ref, wt_ref, lpos_ref, tlen_ref, tcar_ref, cnt_ref, carry):
    i = pl.program_id(0)
    tm = h_ref.shape[0]
    E = N_EXPERTS
    gsz = E // N_EXPERT_GROUPS
    neg_inf = -jnp.inf

    @pl.when(i == 0)
    def _():
        carry[...] = jnp.zeros_like(carry)

    scores = _sigmoid(_dot_nt(rw_ref[...], h_ref[...], precision=HIGHEST))
    biased = scores + rb_ref[...]
    eidx = _iota((E, tm), 0)
    ridx = _iota((gsz, tm), 0)

    slabs = [biased[g * gsz:(g + 1) * gsz, :] for g in range(N_EXPERT_GROUPS)]
    gs = []
    for v in slabs:
        m1 = jnp.max(v, axis=0, keepdims=True)
        i1 = jnp.min(jnp.where(v == m1, ridx, gsz), axis=0, keepdims=True)
        m2 = jnp.max(jnp.where(ridx == i1, neg_inf, v), axis=0, keepdims=True)
        gs.append(m1 + m2)
    masked = []
    for g in range(N_EXPERT_GROUPS):
        rank = jnp.zeros((1, tm), I32)
        for g2 in range(N_EXPERT_GROUPS):
            if g2 == g:
                continue
            ahead = (gs[g2] >= gs[g]) if g2 < g else (gs[g2] > gs[g])
            rank = rank + ahead.astype(I32)
        masked.append(jnp.where(rank < TOPK_GROUPS, slabs[g], MASK_NEG))
    cur = jnp.concatenate(masked, axis=0)

    krow = _iota((TOP_K, tm), 0)
    onehot = jnp.zeros((E, tm), F32)
    picks = []
    wsum = jnp.zeros((1, tm), F32)
    for kk in range(TOP_K):
        m = jnp.max(cur, axis=0, keepdims=True)
        ik = jnp.min(jnp.where(cur == m, eidx, E), axis=0, keepdims=True)
        hit = eidx == ik
        wk = jnp.sum(jnp.where(hit, scores, 0.0), axis=0, keepdims=True)
        cur = jnp.where(hit, neg_inf, cur)
        onehot = onehot + hit.astype(F32)
        picks.append((ik, wk))
        wsum = wsum + wk
    wt_out = jnp.zeros((TOP_K, tm), F32)
    for kk, (ik, wk) in enumerate(picks):
        wt_out = jnp.where(krow == kk, wk / wsum * ROUTED_SCALE, wt_out)

    earlier = (_iota((tm, tm), 0) < _iota((tm, tm), 1)).astype(BF16)
    before = _dot(onehot.astype(BF16), earlier)
    tile_cnt = (jnp.sum(onehot, axis=1, keepdims=True) + jnp.zeros((E, LANES), F32)).astype(I32)
    run_len = ((tile_cnt + (RUN_ROWS - 1)) // RUN_ROWS * RUN_ROWS).astype(F32)
    run_start = _cumsum_rows(run_len, False) - run_len
    local = before + run_start[:, 0:1]
    lpos_out = jnp.zeros((TOP_K, tm), I32)
    for kk, (ik, wk) in enumerate(picks):
        lp = jnp.sum(jnp.where(eidx == ik, local, 0.0), axis=0, keepdims=True)
        lpos_out = jnp.where(krow == kk, lp.astype(I32), lpos_out)
    tcar_ref[...] = carry[...].astype(I32)
    tlen_ref[...] = run_len.astype(I32)
    carry[...] = carry[...] + run_len

    wt_ref[...] = wt_out
    lpos_ref[...] = lpos_out
    cnt_ref[...] = carry[...]


def _router(h2, router_w, router_b, *, tm):
    T = h2.shape[0]
    nt = T // tm
    ospec = pl.BlockSpec((TOP_K, tm), lambda i: (0, i))
    tspec = pl.BlockSpec((N_EXPERTS, LANES), lambda i: (0, i))
    cspec = pl.BlockSpec((N_EXPERTS, LANES), lambda i: (0, 0))
    wt, lpos, tlen, tcar, cnt = pl.pallas_call(
        _router_kernel,
        grid=(nt,),
        in_specs=[pl.BlockSpec((tm, D_MODEL), lambda i: (i, 0)),
                  pl.BlockSpec((N_EXPERTS, D_MODEL), lambda i: (0, 0)),
                  pl.BlockSpec((N_EXPERTS, 1), lambda i: (0, 0))],
        out_specs=[ospec, ospec, tspec, tspec, cspec],
        out_shape=[jax.ShapeDtypeStruct((TOP_K, T), F32), jax.ShapeDtypeStruct((TOP_K, T), I32),
                   jax.ShapeDtypeStruct((N_EXPERTS, nt * LANES), I32),
                   jax.ShapeDtypeStruct((N_EXPERTS, nt * LANES), I32),
                   jax.ShapeDtypeStruct((N_EXPERTS, LANES), F32)],
        scratch_shapes=[pltpu.VMEM((N_EXPERTS, LANES), F32)],
        compiler_params=_cparams(("arbitrary",)),
        name="moe_router",
    )(h2, router_w.T, router_b.reshape(N_EXPERTS, 1))
    return wt, lpos, tlen[:, ::LANES].T, tcar[:, ::LANES].T, cnt[:, 0].astype(I32)


def _dispatch_kernel(len_ref, lst_ref, gdst_ref, lpos_ref, h_ref, xs_ref, xloc, sem):
    i = pl.program_id(0)
    tm = h_ref.shape[0]
    nloc = xloc.shape[0]
    R = RUN_ROWS
    lpos = lpos_ref[...]
    hb = h_ref[...].astype(BF16)
    rb = 256
    for blk in range(nloc // rb):
        r = _iota((rb, tm), 0) + blk * rb
        hit = r == lpos[0:1, :]
        for kk in range(1, TOP_K):
            hit = hit | (r == lpos[kk:kk + 1, :])
        xloc[blk * rb:(blk + 1) * rb, :] = _dot(hit.astype(BF16), hb)

    def piece(src_row, dst_row):
        return pltpu.make_async_copy(xloc.at[pl.ds(pl.multiple_of(src_row, R), R), :],
                                     xs_ref.at[pl.ds(pl.multiple_of(dst_row, R), R), :], sem)

    def per_expert(e, issued):
        pieces = len_ref[i, e] // R
        ls = lst_ref[i, e]
        gd = gdst_ref[i, e]

        def issue(j, c):
            piece(ls + j * R, gd + j * R).start()
            return c

        lax.fori_loop(0, pieces, issue, 0)
        return issued + pieces

    total = lax.fori_loop(0, N_EXPERTS, per_expert, 0)

    def wait(j, c):
        piece(0, 0).wait()
        return c

    lax.fori_loop(0, total, wait, 0)


def _dispatch(h2, lpos, tile_len, tile_lstart, tile_gdst, *, tm):
    T = h2.shape[0]
    nt = T // tm
    n_rows = T * TOP_K + N_EXPERTS * nt * RUN_ROWS
    nloc = tm * TOP_K + N_EXPERTS * RUN_ROWS
    return pl.pallas_call(
        _dispatch_kernel,
        grid_spec=pltpu.PrefetchScalarGridSpec(
            num_scalar_prefetch=3,
            grid=(nt,),
            in_specs=[pl.BlockSpec((TOP_K, tm), lambda i, n, l, g: (0, i)),
                      pl.BlockSpec((tm, D_MODEL), lambda i, n, l, g: (i, 0))],
            out_specs=pl.BlockSpec(memory_space=pl.ANY),
            scratch_shapes=[pltpu.VMEM((nloc, D_MODEL), F32), pltpu.SemaphoreType.DMA(())]),
        out_shape=jax.ShapeDtypeStruct((n_rows, D_MODEL), F32),
        compiler_params=_cparams(("arbitrary",)),
        name="moe_dispatch",
    )(tile_len, tile_lstart, tile_gdst, lpos, h2)


def _expert_kernel(blk_ref, exp_ref, lo_ref, hi_ref, first_ref, x_ref, w1_ref, w3_ref, w2_ref, o_ref):
    w = pl.program_id(0)
    x = x_ref[...].astype(BF16)
    a = _dot(x, w1_ref[...].astype(BF16))
    b = _dot(x, w3_ref[...].astype(BF16))
    y = _dot((_silu(a) * b).astype(BF16), w2_ref[...].astype(BF16))
    r = _iota(y.shape, 0)
    y = jnp.where((r >= lo_ref[w]) & (r < hi_ref[w]), y, 0.0)

    @pl.when(first_ref[w] == 1)
    def _():
        o_ref[...] = y

    @pl.when(first_ref[w] == 0)
    def _():
        o_ref[...] = o_ref[...] + y


def _experts(xs, work, w1, w3, w2, *, bm):
    A = xs.shape[0]
    n_work = work[0].shape[0]
    xmap = lambda w, blk, ex, lo, hi, first: (blk[w], 0)
    wmap = lambda w, blk, ex, lo, hi, first: (ex[w], 0, 0)
    return pl.pallas_call(
        _expert_kernel,
        grid_spec=pltpu.PrefetchScalarGridSpec(
            num_scalar_prefetch=5,
            grid=(n_work,),
            in_specs=[pl.BlockSpec((bm, D_MODEL), xmap),
                      pl.BlockSpec((None, D_MODEL, D_EXPERT), wmap),
                      pl.BlockSpec((None, D_MODEL, D_EXPERT), wmap),
                      pl.BlockSpec((None, D_EXPERT, D_MODEL), wmap)],
            out_specs=pl.BlockSpec((bm, D_MODEL), xmap)),
        out_shape=jax.ShapeDtypeStruct((A, D_MODEL), F32),
        compiler_params=_cparams(("arbitrary",)),
        name="moe_experts",
    )(*work, xs, w1, w3, w2)


def _combine_kernel(len_ref, lst_ref, gdst_ref, lpos_ref, lpost_ref, wt_ref, ys_ref, h_ref, x_ref, mod_ref,
                    s1_ref, s3_ref, s2_ref, fg_ref, o_ref, yloc, sem, *, final_norm):
    i = pl.program_id(0)
    tm = h_ref.shape[0]
    nloc = yloc.shape[0]
    R = RUN_ROWS

    @pl.when(i == 0)
    def _():
        yloc[...] = jnp.zeros_like(yloc)

    def piece(src_row, dst_row):
        return pltpu.make_async_copy(ys_ref.at[pl.ds(pl.multiple_of(src_row, R), R), :],
                                     yloc.at[pl.ds(pl.multiple_of(dst_row, R), R), :], sem)

    def per_expert(e, issued):
        pieces = len_ref[i, e] // R
        ls = lst_ref[i, e]
        gd = gdst_ref[i, e]

        def issue(j, c):
            piece(gd + j * R, ls + j * R).start()
            return c

        lax.fori_loop(0, pieces, issue, 0)
        return issued + pieces

    total = lax.fori_loop(0, N_EXPERTS, per_expert, 0)

    hb = h_ref[...].astype(BF16)
    mid = _silu(_dot(hb, s1_ref[...])) * _dot(hb, s3_ref[...])
    y = _dot(mid.astype(BF16), s2_ref[...])

    def wait(j, c):
        piece(0, 0).wait()
        return c

    lax.fori_loop(0, total, wait, 0)

    lpos = lpos_ref[...]
    wt = wt_ref[...]
    rb = 256
    for blk in range(nloc // rb):
        r = _iota((rb, tm), 0) + blk * rb
        pw = jnp.zeros((rb, tm), F32)
        for kk in range(TOP_K):
            pw = pw + jnp.where(r == lpos[kk:kk + 1, :], wt[kk:kk + 1, :], 0.0)
        wrow = jnp.sum(pw, axis=1, keepdims=True)
        ysc = yloc[blk * rb:(blk + 1) * rb, :] * wrow
        hi = ysc.astype(BF16)
        lo = (ysc - hi.astype(F32)).astype(BF16)
        c = _iota((tm, rb), 1) + blk * rb
        hit = c == lpost_ref[:, 0:1]
        for kk in range(1, TOP_K):
            hit = hit | (c == lpost_ref[:, kk:kk + 1])
        pt = hit.astype(BF16)
        y = y + (_dot(pt, hi) + _dot(pt, lo))
    x = x_ref[...] + mod_ref[5:6, :] * y
    if final_norm:
        x = x * lax.rsqrt(jnp.mean(x * x, axis=-1, keepdims=True) + EPS) * fg_ref[...]
    o_ref[...] = x


def _combine(ys, lpos, lpos_t, wt, tile_len, tile_lstart, tile_gdst, h2, x, mod, s1, s3, s2, final_g,
             *, tm, tokens_per_mod, final_norm):
    T = x.shape[0]
    tiles_per_mod = tokens_per_mod // tm
    nloc = tm * TOP_K + N_EXPERTS * RUN_ROWS
    xspec = pl.BlockSpec((tm, D_MODEL), lambda i, n, l, g: (i, 0))
    kspec = pl.BlockSpec((TOP_K, tm), lambda i, n, l, g: (0, i))
    const = lambda i, n, l, g: (0, 0)
    return pl.pallas_call(
        functools.partial(_combine_kernel, final_norm=final_norm),
        grid_spec=pltpu.PrefetchScalarGridSpec(
            num_scalar_prefetch=3,
            grid=(T // tm,),
            in_specs=[kspec,
                      pl.BlockSpec((tm, LANES), lambda i, n, l, g: (i, 0)),
                      kspec,
                      pl.BlockSpec(memory_space=pl.ANY),
                      xspec, xspec,
                      pl.BlockSpec((None, 6, D_MODEL), lambda i, n, l, g: (i // tiles_per_mod, 0, 0)),
                      pl.BlockSpec((D_MODEL, D_SHARED), const),
                      pl.BlockSpec((D_MODEL, D_SHARED), const),
                      pl.BlockSpec((D_SHARED, D_MODEL), const),
                      pl.BlockSpec((1, D_MODEL), const)],
            out_specs=xspec,
            scratch_shapes=[pltpu.VMEM((nloc, D_MODEL), F32), pltpu.SemaphoreType.DMA(())]),
        out_shape=jax.ShapeDtypeStruct((T, D_MODEL), F32),
        compiler_params=_cparams(("arbitrary",)),
        name="moe_combine",
    )(tile_len, tile_lstart, tile_gdst, lpos, lpos_t, wt, ys, h2, x, mod, s1, s3, s2, final_g.reshape(1, D_MODEL))


def _work_list(counts, starts, n_rows, bm):
    E = N_EXPERTS
    n_blocks = n_rows // bm
    n_work = n_blocks + E - 1
    ends = starts + counts
    first_blk = starts // bm
    last_blk = jnp.where(counts > 0, (ends - 1) // bm, first_blk)
    n_items = jnp.where(counts > 0, last_blk - first_blk + 1, 0)
    item_end = jnp.cumsum(n_items)
    item_start = item_end - n_items
    w = jnp.arange(n_work, dtype=I32)
    ex = jnp.minimum(jnp.searchsorted(item_end, w, side="right"), E - 1).astype(I32)
    valid = w < item_end[-1]
    blk = first_blk[ex] + (w - item_start[ex])
    blk = jnp.where(valid, blk, blk[jnp.maximum(item_end[-1] - 1, 0)]).astype(I32)
    lo = jnp.clip(starts[ex] - blk * bm, 0, bm)
    hi = jnp.clip(ends[ex] - blk * bm, 0, bm)
    lo = jnp.where(valid, lo, 0).astype(I32)
    hi = jnp.where(valid, hi, 0).astype(I32)
    ex = jnp.where(valid, ex, ex[jnp.maximum(item_end[-1] - 1, 0)])
    prev_blk = jnp.concatenate([jnp.full((1,), -1, I32), blk[:-1]])
    first = (blk != prev_blk).astype(I32)
    return blk, ex.astype(I32), lo, hi, first


def _moe(h2, x, mod, p, final_g, *, tokens_per_mod, final_norm):
    T = x.shape[0]
    tm = 256
    wt, lpos, tile_len, tile_car, seg_len = _router(h2, p["router_w"], p["router_b"], tm=tm)
    starts = jnp.cumsum(seg_len) - seg_len
    tile_lstart = jnp.cumsum(tile_len, axis=1) - tile_len
    tile_gdst = starts[None, :] + tile_car
    xs = _dispatch(h2, lpos, tile_len, tile_lstart, tile_gdst, tm=tm)
    bm = 512
    work = _work_list(seg_len, starts, xs.shape[0], bm)
    ys = _experts(xs, work, p["exp_w1"], p["exp_w3"], p["exp_w2"], bm=bm)
    lpos_t = jnp.pad(lpos.T, ((0, 0), (0, LANES - TOP_K)))
    return _combine(ys, lpos, lpos_t, wt, tile_len, tile_lstart, tile_gdst, h2, x, mod,
                    p["sh_w1"], p["sh_w3"], p["sh_w2"], final_g,
                    tm=tm, tokens_per_mod=tokens_per_mod, final_norm=final_norm)


def _reorder_w_in(w_in):
    plain = D_CONV
    pieces = [w_in[..., :plain],
              w_in[..., plain + 3616:plain + 3616 + 4096],
              w_in[..., plain:plain + 3072],
              w_in[..., plain + 3088:plain + 3600],
              w_in[..., plain + 3072:plain + 3088],
              w_in[..., plain + 3600:plain + 3616]]
    out = jnp.concatenate(pieces, axis=-1)
    pad = D_PROJ - out.shape[-1]
    return jnp.pad(out, ((0, 0), (0, 0), (0, pad))).astype(BF16)


def _small_rows(ssd_vals, gdn_vals):
    row = jnp.zeros((LANES,), F32)
    row = row.at[S_DT:S_DT + 2 * SSD_HEADS].set(ssd_vals.reshape(-1))
    row = row.at[S_GA:S_GA + 2 * GDN_HEADS].set(gdn_vals.reshape(-1))
    return row.reshape(1, LANES)


def _layer_pass(x, mod, lp, hy, st_hg, st_ssd, st_gdn, final_g, *, B, L, tokens_per_mod, seg, layer, final_norm):
    T = B * L
    proj = _in_projection(x, mod, lp["norm1_g"], lp["w_in"], lp["conv_w"], lp["conv_b"],
                          tokens_per_mod=tokens_per_mod, seg=seg)
    proj3 = proj.reshape(B, L, D_PROJ)
    fmat, gmat, hspec = hy
    cb = lambda c: c // D_BRANCH
    z = _spectral_conv(fmat, gmat, hspec, 0, lp["hy_bias"][0], proj3, cb(C_HYV), proj3, cb(C_HYX1))
    ya = _spectral_conv(fmat, gmat, hspec, 1, lp["hy_bias"][1], z, 0, proj3, cb(C_HYX2))
    yb, s_hg = _hgrn_scan(proj3, lp["hg_lb"], lp["hg_norm"], st_hg, layer=layer)
    yc, s_ssd = _ssd_scan(proj3, lp["bias_row"], lp["alog_row"], lp["ssd_d"], st_ssd)
    yd, s_gdn = _gdn_scan(proj3, lp["bias_row"], lp["alog_row"], lp["gdn_norm"], st_gdn)
    flat = lambda a: a.reshape(T, D_BRANCH)
    x, h2 = _merge(flat(ya), flat(yb), flat(yc), flat(yd), proj, x, mod, lp["ssd_norm"], lp["norm2_g"],
                   lp["w_branch"], lp["w_out"], tokens_per_mod=tokens_per_mod)
    x = _moe(h2, x, mod, lp, final_g, tokens_per_mod=tokens_per_mod, final_norm=final_norm)
    return x, s_hg, s_ssd, s_gdn


def kernel(x_prompt, x_sample, state_hgrn, state_ssd, state_gdn, c, c_ctx, norm1_g, norm2_g, ada_w, ada_b, w_in, conv_w, conv_b, hy_w1, hy_b1, hy_w2, hy_b2, hy_w3, hy_bias, hg_lb, hg_norm, ssd_a_log, ssd_dt_bias, ssd_d, ssd_norm, gdn_a_log, gdn_dt_bias, gdn_norm, w_branch, w_out, router_w, router_bias, exp_w1, exp_w3, exp_w2, sh_w1, sh_w3, sh_w2, final_g):
    depth = w_in.shape[0]
    bp, lp_len = x_prompt.shape[0], x_prompt.shape[1]
    bs, ls_len = x_sample.shape[0], x_sample.shape[1]
    D = D_MODEL

    w_in_r = _reorder_w_in(w_in)
    hy_w1p = jnp.pad(hy_w1, ((0, 0), (0, LANES - hy_w1.shape[1]), (0, 0)))
    layers = []
    for l in range(depth):
        layers.append(dict(
            norm1_g=norm1_g[l], norm2_g=norm2_g[l], w_in=w_in_r[l], conv_w=conv_w[l], conv_b=conv_b[l],
            hy_bias=hy_bias[l], hg_lb=hg_lb, hg_norm=hg_norm[l],
            bias_row=_small_rows(ssd_dt_bias[l], gdn_dt_bias[l]),
            alog_row=_small_rows(ssd_a_log[l], gdn_a_log[l]),
            ssd_d=ssd_d[l], ssd_norm=ssd_norm[l], gdn_norm=gdn_norm[l],
            w_branch=w_branch[l].astype(BF16), w_out=w_out[l].astype(BF16),
            router_w=router_w[l], router_b=router_bias[l],
            exp_w1=exp_w1[l], exp_w3=exp_w3[l], exp_w2=exp_w2[l],
            sh_w1=sh_w1[l].astype(BF16), sh_w3=sh_w3[l].astype(BF16), sh_w2=sh_w2[l].astype(BF16)))

    def hyena_setup(L):
        fmat, fs = _dft_matrices(L)
        gmat = fs.T
        specs = []
        for l in range(depth):
            filt = _hyena_filters(L, hy_w1p[l], hy_b1[l], hy_w2[l], hy_b2[l], hy_w3[l])
            specs.append(_filter_spectrum(fmat, filt))
        return fmat, gmat, specs

    cond = jnp.concatenate([c_ctx.reshape(1, D), c], axis=0)
    rows = cond.shape[0]
    rows8 = (rows + 7) // 8 * 8
    cond8 = jnp.pad(cond, ((0, rows8 - rows), (0, 0)))
    mods = [_modulation(cond8, ada_w[l], ada_b[l]).reshape(rows8, 6, D) for l in range(depth)]

    fmat, gmat, specs = hyena_setup(lp_len)
    x = x_prompt.reshape(bp * lp_len, D)
    z_hg = jnp.zeros((bp, 2, HG_HEADS, HG_DK, HG_DK), F32)
    z_ssd = jnp.zeros((bp, 2, SSD_HEADS, SSD_P, SSD_N), F32)
    z_gdn = jnp.zeros((bp, 2, GDN_HEADS, GDN_DK, GDN_DK), F32)
    hg_states, ssd_states, gdn_states = [], [], []
    for l in range(depth):
        x, s_hg, s_ssd, s_gdn = _layer_pass(
            x, mods[l][0:1], layers[l], (fmat, gmat, specs[l]), z_hg, z_ssd, z_gdn, final_g,
            B=bp, L=lp_len, tokens_per_mod=bp * lp_len, seg=lp_len, layer=l, final_norm=(l == depth - 1))
        hg_states.append(s_hg)
        ssd_states.append(s_ssd)
        gdn_states.append(s_gdn)
    y_prompt = x.reshape(bp, lp_len, D)
    new_hg = jnp.stack(hg_states, axis=1)
    new_ssd = jnp.stack(ssd_states, axis=1)
    new_gdn = jnp.stack(gdn_states, axis=1)

    fmat, gmat, specs = hyena_setup(ls_len)
    x = x_sample.reshape(bs * ls_len, D)
    for l in range(depth):
        x, _, _, _ = _layer_pass(
            x, mods[l][1:1 + bs], layers[l], (fmat, gmat, specs[l]),
            state_hgrn[:, l], state_ssd[:, l], state_gdn[:, l], final_g,
            B=bs, L=ls_len, tokens_per_mod=ls_len, seg=GRID_W, layer=l, final_norm=(l == depth - 1))
    y_sample = x.reshape(bs, ls_len, D)
    return (y_prompt, y_sample, new_hg, new_ssd, new_gdn)
```

```python
import functools
import math

import numpy as np
import jax
import jax.numpy as jnp
from jax import lax
from jax.experimental import pallas as pl
from jax.experimental.pallas import tpu as pltpu

F32 = jnp.float32
BF16 = jnp.bfloat16
I32 = jnp.int32
HIGHEST = lax.Precision.HIGHEST

D_MODEL = 1024
GRID_W = 64
EPS = 1e-6
LOG_FLOOR = 1e-30
MASK_NEG = -1e30
N_BRANCH = 4
D_BRANCH = 512
HY_POS_FREQS = 16
HY_FILTER_HIDDEN = 64
HY_FAST_DECAY = 0.3
HY_SLOW_DECAY = 1.5
HY_DECAY_TARGET = 1e-2
HG_HEADS = 4
HG_DK = 128
HG_CHUNK = 16
SSD_HEADS = 8
SSD_P = 64
SSD_N = 128
SSD_CHUNK = 64
GDN_HEADS = 4
GDN_DK = 128
GDN_CHUNK = 64
N_EXPERTS = 64
TOP_K = 8
N_EXPERT_GROUPS = 8
TOPK_GROUPS = 4
D_EXPERT = 256
D_SHARED = 256
ROUTED_SCALE = 2.5

LANES = 128
RUN_ROWS = 8

D_CONV = 4096
C_HYV, C_HYX1, C_HYX2 = 0, 512, 1024
C_SSDX, C_SSDB, C_SSDC = 1536, 2048, 2304
C_GQ, C_GK, C_GV = 2560, 3072, 3584
P_MERGE = 4096
P_HGQ, P_HGFF, P_HGFB, P_HGV, P_HGG = 8192, 8704, 9216, 9728, 10240
P_SSDZ, P_GDNG, P_SMALL = 10752, 11264, 11776
D_PROJ = 12288
S_DT, S_GA, S_GB = 0, 16, 24

VMEM_LIMIT = 56 * 1024 * 1024


def _cparams(sem):
    return pltpu.CompilerParams(dimension_semantics=sem, vmem_limit_bytes=VMEM_LIMIT)


def _sigmoid(x):
    return 1.0 / (1.0 + jnp.exp(-x))


def _silu(x):
    return x * _sigmoid(x)


def _softplus(x):
    return jnp.maximum(x, 0.0) + jnp.log(1.0 + jnp.exp(-jnp.abs(x)))


def _dot(a, b, precision=None):
    return jnp.dot(a, b, preferred_element_type=F32, precision=precision)


def _dot_nt(a, b, precision=None):
    return lax.dot_general(a, b, (((1,), (1,)), ((), ())), preferred_element_type=F32, precision=precision)


def _dot_tn(a, b, precision=None):
    return lax.dot_general(a, b, (((0,), (0,)), ((), ())), preferred_element_type=F32, precision=precision)


def _bdot(a, b):
    return _dot(a.astype(BF16), b.astype(BF16))


def _bdot_nt(a, b):
    return _dot_nt(a.astype(BF16), b.astype(BF16))


def _bdot_tn(a, b):
    return _dot_tn(a.astype(BF16), b.astype(BF16))


def _iota(shape, dim):
    return lax.broadcasted_iota(I32, shape, dim)


def _cumsum_rows(g, reverse):
    n = g.shape[0]
    row = _iota(g.shape, 0)
    sh = 1
    while sh < n:
        if reverse:
            g = g + jnp.where(row < n - sh, pltpu.roll(g, n - sh, 0), 0.0)
        else:
            g = g + jnp.where(row >= sh, pltpu.roll(g, sh, 0), 0.0)
        sh *= 2
    return g


def _lane_pick(a, j):
    return jnp.sum(jnp.where(_iota(a.shape, 1) == j, a, 0.0), axis=1, keepdims=True)


def _split3(a):
    a1 = a.astype(BF16)
    r1 = a - a1.astype(F32)
    a2 = r1.astype(BF16)
    a3 = (r1 - a2.astype(F32)).astype(BF16)
    return a1, a2, a3


def _dot3(a, b):
    a1, a2, _ = _split3(a)
    b1, b2, _ = _split3(b)
    return _dot(a1, b1) + (_dot(a1, b2) + _dot(a2, b1))


def _row_pick(a, j):
    sel = (_iota((8, a.shape[1]), 1) == j).astype(BF16)
    a1, a2, a3 = _split3(a)
    return (_dot_nt(sel, a1) + (_dot_nt(sel, a2) + _dot_nt(sel, a3)))[0:1, :]


def _tri_mask(n, reverse, strict=False):
    t = _iota((n, n), 0)
    s = _iota((n, n), 1)
    if reverse:
        return (s > t) if strict else (s >= t)
    return (s < t) if strict else (s <= t)


def _masked_exp(mask, diff):
    return jnp.where(mask, jnp.exp(jnp.where(mask, diff, 0.0)), 0.0)


def _mod_kernel(c_ref, w_ref, b_ref, o_ref):
    o_ref[...] = _dot(_silu(c_ref[...]), w_ref[...], precision=HIGHEST) + b_ref[...]


def _modulation(cond8, ada_w, ada_b):
    rows = cond8.shape[0]
    tn = 1536
    n = ada_w.shape[1]
    return pl.pallas_call(
        _mod_kernel,
        grid=(n // tn,),
        in_specs=[pl.BlockSpec((rows, D_MODEL), lambda j: (0, 0)),
                  pl.BlockSpec((D_MODEL, tn), lambda j: (0, j)),
                  pl.BlockSpec((1, tn), lambda j: (0, j))],
        out_specs=pl.BlockSpec((rows, tn), lambda j: (0, j)),
        out_shape=jax.ShapeDtypeStruct((rows, n), F32),
        compiler_params=_cparams(("arbitrary",)),
        name="adaln_mod",
    )(cond8, ada_w, ada_b.reshape(1, n))


def _inproj_kernel(x_ref, mod_ref, g_ref, w_ref, cw_ref, cb_ref, o_ref, h_scr, *, seg, n_conv_tiles):
    j = pl.program_id(1)

    @pl.when(j == 0)
    def _():
        x = x_ref[...]
        xn = x * lax.rsqrt(jnp.mean(x * x, axis=-1, keepdims=True) + EPS) * g_ref[...]
        m = mod_ref[...]
        h_scr[...] = (xn * (1.0 + m[1:2]) + m[0:1]).astype(BF16)

    y = _dot(h_scr[...], w_ref[...])

    @pl.when(j < n_conv_tiles)
    def _():
        tm = y.shape[0]
        pos = _iota(y.shape, 0) & (seg - 1)
        prev = jnp.where(pos == 0, 0.0, pltpu.roll(y, 1, 0))
        nxt = jnp.where(pos == seg - 1, 0.0, pltpu.roll(y, tm - 1, 0))
        cw = cw_ref[...]
        o_ref[...] = cb_ref[...] + prev * cw[0:1] + y * cw[1:2] + nxt * cw[2:3]

    @pl.when(j >= n_conv_tiles)
    def _():
        o_ref[...] = y


def _in_projection(x, mod, norm_g, w_bf16, conv_w, conv_b, *, tokens_per_mod, seg):
    T = x.shape[0]
    tm = min(1024, tokens_per_mod)
    tn = 1024
    n_conv_tiles = D_CONV // tn
    tiles_per_mod = tokens_per_mod // tm
    kern = functools.partial(_inproj_kernel, seg=seg, n_conv_tiles=n_conv_tiles)
    cmap = lambda i, j: (0, jnp.minimum(j, n_conv_tiles - 1))
    return pl.pallas_call(
        kern,
        grid=(T // tm, D_PROJ // tn),
        in_specs=[pl.BlockSpec((tm, D_MODEL), lambda i, j: (i, 0)),
                  pl.BlockSpec((None, 6, D_MODEL), lambda i, j: (i // tiles_per_mod, 0, 0)),
                  pl.BlockSpec((1, D_MODEL), lambda i, j: (0, 0)),
                  pl.BlockSpec((D_MODEL, tn), lambda i, j: (0, j)),
                  pl.BlockSpec((3, tn), cmap),
                  pl.BlockSpec((1, tn), cmap)],
        out_specs=pl.BlockSpec((tm, tn), lambda i, j: (i, j)),
        out_shape=jax.ShapeDtypeStruct((T, D_PROJ), F32),
        scratch_shapes=[pltpu.VMEM((tm, D_MODEL), BF16)],
        compiler_params=_cparams(("arbitrary", "arbitrary")),
        name="in_proj",
    )(x, mod, norm_g.reshape(1, D_MODEL), w_bf16, conv_w, conv_b.reshape(1, D_CONV))


def _hyfilt_kernel(w1_ref, b1_ref, w2_ref, b2_ref, w3_ref, o_ref, *, L):
    i = pl.program_id(1)
    tl = o_ref.shape[0]
    t = (_iota((tl, LANES), 0) + i * tl).astype(F32) / L
    lane = _iota((tl, LANES), 1)
    band = jnp.where(lane <= HY_POS_FREQS, lane, lane - HY_POS_FREQS).astype(F32)
    ang = 2.0 * math.pi * t * band
    feats = jnp.where(lane == 0, t,
                      jnp.where(lane <= HY_POS_FREQS, jnp.sin(ang),
                                jnp.where(lane <= 2 * HY_POS_FREQS, jnp.cos(ang), 0.0)))
    hdn = jnp.sin(_dot(feats, w1_ref[...], precision=HIGHEST) + b1_ref[...])
    hdn = jnp.sin(_dot(hdn, w2_ref[...], precision=HIGHEST) + b2_ref[...])
    filt = _dot(hdn, w3_ref[...], precision=HIGHEST)
    max_decay = math.log(HY_DECAY_TARGET) / HY_FAST_DECAY
    min_decay = math.log(HY_DECAY_TARGET) / HY_SLOW_DECAY
    n = filt.shape[1]
    ch = (_iota((tl, n), 1) & (D_BRANCH - 1)).astype(F32)
    delta = min_decay + ch * ((max_decay - min_decay) / (D_BRANCH - 1))
    tt = (_iota((tl, n), 0) + i * tl).astype(F32) / L
    o_ref[...] = filt * jnp.exp(-tt * jnp.abs(delta))


def _hyena_filters(L, w1p, b1, w2, b2, w3):
    tl = min(L, 256)
    n = 2 * D_BRANCH
    return pl.pallas_call(
        functools.partial(_hyfilt_kernel, L=L),
        grid=(2, L // tl),
        in_specs=[pl.BlockSpec((LANES, HY_FILTER_HIDDEN), lambda d, i: (0, 0)),
                  pl.BlockSpec((1, HY_FILTER_HIDDEN), lambda d, i: (0, 0)),
                  pl.BlockSpec((HY_FILTER_HIDDEN, HY_FILTER_HIDDEN), lambda d, i: (0, 0)),
                  pl.BlockSpec((1, HY_FILTER_HIDDEN), lambda d, i: (0, 0)),
                  pl.BlockSpec((HY_FILTER_HIDDEN, n), lambda d, i: (0, d))],
        out_specs=pl.BlockSpec((None, tl, n), lambda d, i: (d, i, 0)),
        out_shape=jax.ShapeDtypeStruct((2, L, n), F32),
        compiler_params=_cparams(("arbitrary", "arbitrary")),
        name="hyena_filters",
    )(w1p, b1.reshape(1, -1), w2, b2.reshape(1, -1), w3)


def _dftgen_kernel(f_ref, fs_ref, *, L, tk):
    i = pl.program_id(0)
    N = 2 * L
    k = _iota((tk, LANES), 0) + i * tk
    lane = _iota((tk, LANES), 1)
    w = 2.0 * math.pi / N
    a0 = ((k * lane) & (N - 1)).astype(F32) * w
    c0, s0 = jnp.cos(a0), jnp.sin(a0)
    a1 = ((k * lane * LANES) & (N - 1)).astype(F32) * w
    c1, s1 = jnp.cos(a1), jnp.sin(a1)
    alt = jnp.where((lane & 1) == 0, 1.0, -1.0)
    coef = jnp.where(k == 0, 1.0 / N, 2.0 / N)
    for n1 in range(L // LANES):
        c1b = c1[:, n1:n1 + 1]
        s1b = s1[:, n1:n1 + 1]
        cosb = c1b * c0 - s1b * s0
        sinb = jnp.where(k == 0, alt, -(s1b * c0 + c1b * s0))
        cols = slice(n1 * LANES, (n1 + 1) * LANES)
        f_ref[0:tk, cols] = cosb.astype(BF16)
        f_ref[tk:2 * tk, cols] = sinb.astype(BF16)
        fs_ref[0:tk, cols] = (coef * cosb).astype(BF16)
        fs_ref[tk:2 * tk, cols] = (coef * sinb).astype(BF16)


def _dft_matrices(L):
    tk = min(L, 256)
    spec = pl.BlockSpec((2 * tk, L), lambda i: (i, 0))
    return pl.pallas_call(
        functools.partial(_dftgen_kernel, L=L, tk=tk),
        grid=(L // tk,),
        in_specs=[],
        out_specs=[spec, spec],
        out_shape=[jax.ShapeDtypeStruct((2 * L, L), BF16)] * 2,
        compiler_params=_cparams(("arbitrary",)),
        name="dft_matrices",
    )()


def _dfth_kernel(f_ref, h_ref, o_ref, hp_scr, *, tk):
    i = pl.program_id(1)
    n = o_ref.shape[1]

    @pl.when(i == 0)
    def _():
        hc = h_ref[0]
        ha = jnp.where(_iota(hc.shape, 0) == 0, 0.0, h_ref[1])
        hp_scr[:, 0:n] = (hc + ha).astype(BF16)
        hp_scr[:, n:2 * n] = (hc - ha).astype(BF16)

    u = _dot(f_ref[...], hp_scr[...])
    r = _iota((2 * tk, n), 0)
    from_sum = (r < tk) | ((r == tk) & (i == 0))
    o_ref[...] = jnp.where(from_sum, u[:, 0:n], u[:, n:2 * n])


def _filter_spectrum(fmat, filt):
    L = fmat.shape[1]
    tk = min(L, 256)
    C = filt.shape[2]
    tc = 256
    return pl.pallas_call(
        functools.partial(_dfth_kernel, tk=tk),
        grid=(C // tc, L // tk),
        in_specs=[pl.BlockSpec((2 * tk, L), lambda c, i: (i, 0)),
                  pl.BlockSpec((2, L, tc), lambda c, i: (0, 0, c))],
        out_specs=pl.BlockSpec((2 * tk, tc), lambda c, i: (i, c)),
        out_shape=jax.ShapeDtypeStruct((2 * L, C), F32),
        scratch_shapes=[pltpu.VMEM((L, 2 * tc), BF16)],
        compiler_params=_cparams(("arbitrary", "arbitrary")),
        name="filter_spectrum",
    )(fmat, filt)


def _dfta_kernel(f_ref, u_ref, h_ref, y_ref, u_scr, *, tk):
    i = pl.program_id(1)

    @pl.when(i == 0)
    def _():
        u_scr[...] = u_ref[...].astype(BF16)

    uf = _dot(f_ref[...], u_scr[...])
    ur, ui = uf[0:tk], uf[tk:2 * tk]
    hr, hi = h_ref[0:tk, :], h_ref[tk:2 * tk, :]
    dc = (_iota(ur.shape, 0) == 0) & (i == 0)
    y_ref[0:tk, :] = jnp.where(dc, ur * hr, ur * hr - ui * hi).astype(BF16)
    y_ref[tk:2 * tk, :] = jnp.where(dc, ui * hi, ur * hi + ui * hr).astype(BF16)


def _dftb_kernel(g_ref, y_ref, u_ref, x_ref, b_ref, o_ref):
    y = _dot(g_ref[...], y_ref[...])
    o_ref[...] = x_ref[...] * (y + u_ref[...] * b_ref[...])


def _spectral_conv(fmat, gmat, hspec, h_col, bias, u_arr, u_col, gate_arr, gate_col):
    B, L = u_arr.shape[0], u_arr.shape[1]
    C = D_BRANCH
    tk = min(L, 256)
    y = pl.pallas_call(
        functools.partial(_dfta_kernel, tk=tk),
        grid=(B, L // tk),
        in_specs=[pl.BlockSpec((2 * tk, L), lambda b, i: (i, 0)),
                  pl.BlockSpec((None, L, C), lambda b, i: (b, 0, u_col)),
                  pl.BlockSpec((2 * tk, C), lambda b, i: (i, h_col))],
        out_specs=pl.BlockSpec((None, 2 * tk, C), lambda b, i: (b, i, 0)),
        out_shape=jax.ShapeDtypeStruct((B, 2 * L, C), BF16),
        scratch_shapes=[pltpu.VMEM((L, C), BF16)],
        compiler_params=_cparams(("arbitrary", "arbitrary")),
        name="hyena_dft_fwd",
    )(fmat, u_arr, hspec)
    tr = min(L, 256)
    return pl.pallas_call(
        _dftb_kernel,
        grid=(B, L // tr),
        in_specs=[pl.BlockSpec((tr, 2 * L), lambda b, i: (i, 0)),
                  pl.BlockSpec((None, 2 * L, C), lambda b, i: (b, 0, 0)),
                  pl.BlockSpec((None, tr, C), lambda b, i: (b, i, u_col)),
                  pl.BlockSpec((None, tr, C), lambda b, i: (b, i, gate_col)),
                  pl.BlockSpec((1, C), lambda b, i: (0, 0))],
        out_specs=pl.BlockSpec((None, tr, C), lambda b, i: (b, i, 0)),
        out_shape=jax.ShapeDtypeStruct((B, L, C), F32),
        compiler_params=_cparams(("arbitrary", "arbitrary")),
        name="hyena_dft_inv",
    )(gmat, y, u_arr, gate_arr, bias.reshape(1, C))


def _hgrn_kernel(q_ref, ff_ref, fb_ref, v_ref, g_ref, lb_ref, nrm_ref, s0_ref, y_ref, sf_ref,
                 of_scr, ob_scr, st_scr, *, L, layer, depth):
    C = HG_CHUNK
    W = HG_DK
    nc = L // C
    nh = q_ref.shape[1] // W
    ridx = _iota((C, W), 0)
    o_scrs = (of_scr, ob_scr)
    f_refs = (ff_ref, fb_ref)

    def lower_bound(d, cols):
        rows = [lb_ref[d, l:l + 1, cols] for l in range(depth)]
        m = rows[0]
        for r in rows[1:]:
            m = jnp.maximum(m, r)
        es = [jnp.exp(r - m) for r in rows]
        tot = es[0]
        for e in es[1:]:
            tot = tot + e
        acc = es[0] / tot
        for e in es[1:layer + 1]:
            acc = acc + e / tot
        return acc - es[0] / tot

    lbs = [[lower_bound(d, slice(hh * W, (hh + 1) * W)) for hh in range(nh)] for d in (0, 1)]
    for d in (0, 1):
        for hh in range(nh):
            st_scr[d, hh] = s0_ref[d, hh].T

    def step(c, d, hh):
        rev = d == 1
        rows = pl.ds(pl.multiple_of(c * C, C), C)
        cols = slice(hh * W, (hh + 1) * W)
        lb = lbs[d][hh]
        q = _silu(q_ref[rows, cols])
        uf = f_refs[d][rows, cols]
        v = v_ref[rows, cols]
        f = lb + (1.0 - lb) * _sigmoid(uf)
        g = jnp.log(jnp.maximum(f, LOG_FLOOR))
        kin = (1.0 - lb) * _sigmoid(-uf)
        b = _cumsum_rows(g, rev)
        st = st_scr[d, hh]
        o = _bdot_nt(q * jnp.exp(b), st)
        intra = jnp.zeros((C, W), F32)
        for t in range(C):
            mask = (ridx >= t) if rev else (ridx <= t)
            pair = _masked_exp(mask, b[t:t + 1, :] - b)
            a = jnp.sum(pair * (q[t:t + 1, :] * kin), axis=1, keepdims=True)
            row = jnp.sum(a * v, axis=0, keepdims=True)
            intra = jnp.where(ridx == t, row, intra)
        b_end = b[0:1, :] if rev else b[C - 1:C, :]
        st_scr[d, hh] = st * jnp.exp(b_end) + _bdot_tn(v, kin * jnp.exp(b_end - b))
        o_scrs[d][rows, cols] = o + intra

    def body(ci, carry):
        for d in (0, 1):
            for hh in range(nh):
                step((nc - 1 - ci) if d == 1 else ci, d, hh)
        return carry

    lax.fori_loop(0, nc, body, 0)
    for d in (0, 1):
        for hh in range(nh):
            sf_ref[d, hh] = st_scr[d, hh].T

    tr = min(L, 256)

    def fin(i, carry):
        rows = pl.ds(pl.multiple_of(i * tr, tr), tr)
        for hh in range(nh):
            cols = slice(hh * W, (hh + 1) * W)
            o = of_scr[rows, cols] + ob_scr[rows, cols]
            on = o * lax.rsqrt(jnp.mean(o * o, axis=-1, keepdims=True) + EPS) * nrm_ref[...]
            y_ref[rows, cols] = on * _silu(g_ref[rows, cols])
        return carry

    lax.fori_loop(0, L // tr, fin, 0)


def _hgrn_scan(proj3, hg_lb, hg_norm, s0, *, layer):
    B, L = proj3.shape[0], proj3.shape[1]
    depth = hg_lb.shape[1]
    nh = 2
    W = nh * HG_DK
    col = lambda base: (lambda b, h: (b, 0, base // W + h))
    st_spec = pl.BlockSpec((None, 2, nh, HG_DK, HG_DK), lambda b, h: (b, 0, h, 0, 0))
    return pl.pallas_call(
        functools.partial(_hgrn_kernel, L=L, layer=layer, depth=depth),
        grid=(B, HG_HEADS // nh),
        in_specs=[pl.BlockSpec((None, L, W), col(P_HGQ)),
                  pl.BlockSpec((None, L, W), col(P_HGFF)),
                  pl.BlockSpec((None, L, W), col(P_HGFB)),
                  pl.BlockSpec((None, L, W), col(P_HGV)),
                  pl.BlockSpec((None, L, W), col(P_HGG)),
                  pl.BlockSpec((2, depth, W), lambda b, h: (0, 0, h)),
                  pl.BlockSpec((1, HG_DK), lambda b, h: (0, 0)),
                  st_spec],
        out_specs=[pl.BlockSpec((None, L, W), lambda b, h: (b, 0, h)), st_spec],
        out_shape=[jax.ShapeDtypeStruct((B, L, D_BRANCH), F32),
                   jax.ShapeDtypeStruct(s0.shape, F32)],
        scratch_shapes=[pltpu.VMEM((L, W), F32), pltpu.VMEM((L, W), F32),
                        pltpu.VMEM((2, nh, HG_DK, HG_DK), F32)],
        compiler_params=_cparams(("arbitrary", "arbitrary")),
        name="hgrn2_scan",
    )(proj3, proj3, proj3, proj3, proj3, hg_lb, hg_norm.reshape(1, HG_DK), s0)


def _ssd_kernel(x_ref, bm_ref, cm_ref, z_ref, sm_ref, bias_ref, alog_ref, dskip_ref, s0_ref,
                y_ref, sf_ref, of_scr, ob_scr, st_scr, *, L):
    C = SSD_CHUNK
    W = LANES
    nc = L // C
    npairs = x_ref.shape[1] // W
    grp = pl.program_id(1)
    lane_lo = _iota((C, W), 1) < SSD_P
    row_lo = _iota((W, SSD_N), 0) < SSD_P
    pick2 = lambda a0, a1: jnp.where(lane_lo, a0, a1)
    o_scrs = (of_scr, ob_scr)
    incls = (_tri_mask(C, False), _tri_mask(C, True))
    for d in (0, 1):
        for pp in range(npairs):
            st_scr[d, pp] = s0_ref[d, pp]

    heads = [(d, pp, hh) for d in (0, 1) for pp in range(npairs) for hh in (0, 1)]
    pairs = [(d, pp) for d in (0, 1) for pp in range(npairs)]

    def body(ci, carry):
        rows_d, bm, cm, dt_all, cum_all, gram = [], [], [], [], [], []
        for d in (0, 1):
            c = (nc - 1 - ci) if d == 1 else ci
            rows = pl.ds(pl.multiple_of(c * C, C), C)
            rows_d.append(rows)
            bm.append(_silu(bm_ref[rows, :]))
            cm.append(_silu(cm_ref[rows, :]))
            dt_all.append(_softplus(sm_ref[rows, :] + bias_ref[...]))
            cum_all.append(_cumsum_rows(-jnp.exp(alog_ref[...]) * dt_all[d], d == 1))
        lane_of = lambda u: S_DT + u[0] * SSD_HEADS + 2 * (npairs * grp + u[1]) + u[2]
        cumr = {u: _row_pick(cum_all[u[0]], lane_of(u)) for u in heads}
        for d in (0, 1):
            gram.append(_bdot_nt(cm[d], bm[d]))
        dtc = {u: _lane_pick(dt_all[u[0]], lane_of(u)) for u in heads}
        cumc = {u: _lane_pick(cum_all[u[0]], lane_of(u)) for u in heads}
        end = {u: (cumc[u][0:1, :] if u[0] == 1 else cumc[u][C - 1:C, :]) for u in heads}
        xs = {pr: _silu(x_ref[rows_d[pr[0]], slice(pr[1] * W, (pr[1] + 1) * W)]) for pr in pairs}
        st = {pr: st_scr[pr[0], pr[1]] for pr in pairs}
        y_state = {u: _bdot_nt(cm[u[0]] * jnp.exp(cumc[u]), st[u[:2]]) for u in heads}
        scores = {u: gram[u[0]] * _masked_exp(incls[u[0]], cumc[u] - cumr[u]) for u in heads}
        y_intra = {}
        for u in heads:
            head_x = jnp.where(lane_lo if u[2] == 0 else ~lane_lo, xs[u[:2]], 0.0) * dtc[u]
            y_intra[u] = _bdot(scores[u], head_x)
        for pr in pairs:
            u0, u1 = pr + (0,), pr + (1,)
            xdt_e = xs[pr] * pick2(dtc[u0] * jnp.exp(end[u0] - cumc[u0]), dtc[u1] * jnp.exp(end[u1] - cumc[u1]))
            st_scr[pr[0], pr[1]] = (st[pr] * jnp.where(row_lo, jnp.exp(end[u0]), jnp.exp(end[u1]))
                                    + _bdot_tn(xdt_e, bm[pr[0]]))
            o_scrs[pr[0]][rows_d[pr[0]], slice(pr[1] * W, (pr[1] + 1) * W)] = pick2(
                y_intra[u0] + y_state[u0], y_intra[u1] + y_state[u1])
        return carry

    lax.fori_loop(0, nc, body, 0)
    for d in (0, 1):
        for pp in range(npairs):
            sf_ref[d, pp] = st_scr[d, pp]

    tr = min(L, 256)

    def fin(i, carry):
        rows = pl.ds(pl.multiple_of(i * tr, tr), tr)
        y = of_scr[rows, :] + ob_scr[rows, :] + _silu(x_ref[rows, :]) * dskip_ref[...]
        y_ref[rows, :] = y * _silu(z_ref[rows, :])
        return carry

    lax.fori_loop(0, L // tr, fin, 0)


def _ssd_scan(proj3, bias_row, alog_row, ssd_d, s0):
    B, L = proj3.shape[0], proj3.shape[1]
    n_groups = 2
    n_pairs = SSD_HEADS // 2
    ppg = n_pairs // n_groups
    W = ppg * LANES
    s0p = s0.reshape(B, 2, n_pairs, 2 * SSD_P, SSD_N)
    dskip = jnp.repeat(ssd_d, SSD_P).reshape(n_groups, 1, W)
    col = lambda base: (lambda b, g: (b, 0, base // W + g))
    grp = lambda base: (lambda b, g: (b, 0, base // LANES + g))
    st_spec = pl.BlockSpec((None, 2, ppg, LANES, SSD_N), lambda b, g: (b, 0, g, 0, 0))
    y, sf = pl.pallas_call(
        functools.partial(_ssd_kernel, L=L),
        grid=(B, n_groups),
        in_specs=[pl.BlockSpec((None, L, W), col(C_SSDX)),
                  pl.BlockSpec((None, L, LANES), grp(C_SSDB)),
                  pl.BlockSpec((None, L, LANES), grp(C_SSDC)),
                  pl.BlockSpec((None, L, W), col(P_SSDZ)),
                  pl.BlockSpec((None, L, LANES), lambda b, g: (b, 0, P_SMALL // LANES)),
                  pl.BlockSpec((1, LANES), lambda b, g: (0, 0)),
                  pl.BlockSpec((1, LANES), lambda b, g: (0, 0)),
                  pl.BlockSpec((None, 1, W), lambda b, g: (g, 0, 0)),
                  st_spec],
        out_specs=[pl.BlockSpec((None, L, W), lambda b, g: (b, 0, g)), st_spec],
        out_shape=[jax.ShapeDtypeStruct((B, L, D_BRANCH), F32),
                   jax.ShapeDtypeStruct(s0p.shape, F32)],
        scratch_shapes=[pltpu.VMEM((L, W), F32), pltpu.VMEM((L, W), F32),
                        pltpu.VMEM((2, ppg, LANES, SSD_N), F32)],
        compiler_params=_cparams(("arbitrary", "arbitrary")),
        name="ssd_scan",
    )(proj3, proj3, proj3, proj3, proj3, bias_row, alog_row, dskip, s0p)
    return y, sf.reshape(s0.shape)


def _l2norm(a):
    return a * lax.rsqrt(jnp.sum(a * a, axis=-1, keepdims=True) + EPS)


def _gdn_kernel(q_ref, k_ref, v_ref, g_ref, sm_ref, bias_ref, alog_ref, nrm_ref, s0_ref,
                y_ref, sf_ref, of_scr, ob_scr, st_scr, *, L):
    C = GDN_CHUNK
    W = GDN_DK
    nc = L // C
    nh = q_ref.shape[1] // W
    hblk = pl.program_id(1)
    o_scrs = (of_scr, ob_scr)
    incls = (_tri_mask(C, False), _tri_mask(C, True))
    stricts = (_tri_mask(C, False, strict=True), _tri_mask(C, True, strict=True))
    for d in (0, 1):
        for hh in range(nh):
            st_scr[d, hh] = s0_ref[d, hh]

    units = [(d, hh) for d in (0, 1) for hh in range(nh)]
    eye = (_iota((C, C), 0) == _iota((C, C), 1)).astype(F32)

    def body(ci, carry):
        rows_d, cum_d, beta_d = [], [], []
        for d in (0, 1):
            c = (nc - 1 - ci) if d == 1 else ci
            rows = pl.ds(pl.multiple_of(c * C, C), C)
            raw = sm_ref[rows, :]
            rows_d.append(rows)
            cum_d.append(_cumsum_rows(-jnp.exp(alog_ref[...]) * _softplus(raw + bias_ref[...]), d == 1))
            beta_d.append(_sigmoid(raw))
        q, k, v, gc, grow, beta = {}, {}, {}, {}, {}, {}
        for u in units:
            d, hh = u
            cols = slice(hh * W, (hh + 1) * W)
            h = nh * hblk + hh
            q[u] = _l2norm(_silu(q_ref[rows_d[d], cols])) * (GDN_DK ** -0.5)
            k[u] = _l2norm(_silu(k_ref[rows_d[d], cols]))
            v[u] = _silu(v_ref[rows_d[d], cols])
            gc[u] = _lane_pick(cum_d[d], S_GA + d * GDN_HEADS + h)
            beta[u] = _lane_pick(beta_d[d], S_GB + d * GDN_HEADS + h)
        for u in units:
            grow[u] = _row_pick(cum_d[u[0]], S_GA + u[0] * GDN_HEADS + nh * hblk + u[1])
        decay = {u: _masked_exp(incls[u[0]], gc[u] - grow[u]) for u in units}
        kb = {u: k[u] * beta[u] for u in units}
        m = {u: -jnp.where(stricts[u[0]], _bdot_nt(kb[u], k[u]) * decay[u], 0.0) for u in units}
        aqk = {u: _bdot_nt(q[u], k[u]) * decay[u] for u in units}
        p = {u: eye + m[u] for u in units}
        sh = 2
        while sh < C:
            m = {u: _dot3(m[u], m[u]) for u in units}
            p = {u: p[u] + _dot3(p[u], m[u]) for u in units}
            sh *= 2
        uu = {u: _dot3(p[u], v[u] * beta[u]) for u in units}
        ww = {u: _dot3(p[u], kb[u] * jnp.exp(gc[u])) for u in units}
        st = {u: st_scr[u[0], u[1]] for u in units}
        v_new = {u: uu[u] - _bdot(ww[u], st[u]) for u in units}
        o = {u: _bdot(q[u] * jnp.exp(gc[u]), st[u]) + _bdot(aqk[u], v_new[u]) for u in units}
        for u in units:
            d, hh = u
            g_end = gc[u][0:1, :] if d == 1 else gc[u][C - 1:C, :]
            st_scr[d, hh] = st[u] * jnp.exp(g_end) + _bdot_tn(k[u] * jnp.exp(g_end - gc[u]), v_new[u])
            o_scrs[d][rows_d[d], slice(hh * W, (hh + 1) * W)] = o[u]
        return carry

    lax.fori_loop(0, nc, body, 0)
    for d in (0, 1):
        for hh in range(nh):
            sf_ref[d, hh] = st_scr[d, hh]

    tr = min(L, 256)

    def fin(i, carry):
        rows = pl.ds(pl.multiple_of(i * tr, tr), tr)
        for hh in range(nh):
            cols = slice(hh * W, (hh + 1) * W)
            o = of_scr[rows, cols] + ob_scr[rows, cols]
            on = o * lax.rsqrt(jnp.mean(o * o, axis=-1, keepdims=True) + EPS) * nrm_ref[...]
            y_ref[rows, cols] = on * _silu(g_ref[rows, cols])
        return carry

    lax.fori_loop(0, L // tr, fin, 0)


def _gdn_scan(proj3, bias_row, alog_row, gdn_norm, s0):
    B, L = proj3.shape[0], proj3.shape[1]
    nh = GDN_HEADS
    W = nh * GDN_DK
    col = lambda base: (lambda b, h: (b, 0, base // W + h))
    st_spec = pl.BlockSpec((None, 2, nh, GDN_DK, GDN_DK), lambda b, h: (b, 0, h, 0, 0))
    once = pl.Buffered(1)
    return pl.pallas_call(
        functools.partial(_gdn_kernel, L=L),
        grid=(B, GDN_HEADS // nh),
        in_specs=[pl.BlockSpec((None, L, W), col(C_GQ), pipeline_mode=once),
                  pl.BlockSpec((None, L, W), col(C_GK), pipeline_mode=once),
                  pl.BlockSpec((None, L, W), col(C_GV), pipeline_mode=once),
                  pl.BlockSpec((None, L, W), col(P_GDNG), pipeline_mode=once),
                  pl.BlockSpec((None, L, LANES), lambda b, h: (b, 0, P_SMALL // LANES), pipeline_mode=once),
                  pl.BlockSpec((1, LANES), lambda b, h: (0, 0)),
                  pl.BlockSpec((1, LANES), lambda b, h: (0, 0)),
                  pl.BlockSpec((1, GDN_DK), lambda b, h: (0, 0)),
                  st_spec],
        out_specs=[pl.BlockSpec((None, L, W), lambda b, h: (b, 0, h)), st_spec],
        out_shape=[jax.ShapeDtypeStruct((B, L, D_BRANCH), F32),
                   jax.ShapeDtypeStruct(s0.shape, F32)],
        scratch_shapes=[pltpu.VMEM((L, W), F32), pltpu.VMEM((L, W), F32),
                        pltpu.VMEM((2, nh, GDN_DK, GDN_DK), F32)],
        compiler_params=_cparams(("arbitrary", "arbitrary")),
        name="gdn_scan",
    )(proj3, proj3, proj3, proj3, proj3, bias_row, alog_row, gdn_norm.reshape(1, GDN_DK), s0)


def _merge_kernel(ya_ref, yb_ref, yc_ref, yd_ref, mg_ref, x_ref, mod_ref, sn_ref, n2_ref, wb_ref, wo_ref,
                  xo_ref, h_ref):
    yc = yc_ref[...]
    half = D_BRANCH // 2
    parts = []
    for gidx in range(2):
        seg = yc[:, gidx * half:(gidx + 1) * half]
        parts.append(seg * lax.rsqrt(jnp.mean(seg * seg, axis=-1, keepdims=True) + EPS)
                     * sn_ref[:, gidx * half:(gidx + 1) * half])
    branches = (ya_ref[...], yb_ref[...], None, yd_ref[...])
    mixed = None
    for n in range(N_BRANCH):
        gate = _sigmoid(mg_ref[:, n * D_MODEL:(n + 1) * D_MODEL])
        if n == 2:
            lifted = (_dot(parts[0].astype(BF16), wb_ref[n, 0:half, :])
                      + _dot(parts[1].astype(BF16), wb_ref[n, half:D_BRANCH, :]))
        else:
            lifted = _dot(branches[n].astype(BF16), wb_ref[n])
        mixed = gate * lifted if mixed is None else mixed + gate * lifted
    m = mod_ref[...]
    x = x_ref[...] + m[2:3] * _dot(mixed.astype(BF16), wo_ref[...])
    xo_ref[...] = x
    xn = x * lax.rsqrt(jnp.mean(x * x, axis=-1, keepdims=True) + EPS) * n2_ref[...]
    h_ref[...] = xn * (1.0 + m[4:5]) + m[3:4]


def _merge(ya, yb, yc, yd, proj, x, mod, ssd_norm, norm2_g, wb_bf16, wo_bf16, *, tokens_per_mod):
    T = x.shape[0]
    tm = 256
    tiles_per_mod = tokens_per_mod // tm
    yspec = pl.BlockSpec((tm, D_BRANCH), lambda i: (i, 0))
    xspec = pl.BlockSpec((tm, D_MODEL), lambda i: (i, 0))
    return pl.pallas_call(
        _merge_kernel,
        grid=(T // tm,),
        in_specs=[yspec, yspec, yspec, yspec,
                  pl.BlockSpec((tm, N_BRANCH * D_MODEL), lambda i: (i, P_MERGE // (N_BRANCH * D_MODEL))),
                  xspec,
                  pl.BlockSpec((None, 6, D_MODEL), lambda i: (i // tiles_per_mod, 0, 0)),
                  pl.BlockSpec((1, D_BRANCH), lambda i: (0, 0)),
                  pl.BlockSpec((1, D_MODEL), lambda i: (0, 0)),
                  pl.BlockSpec((N_BRANCH, D_BRANCH, D_MODEL), lambda i: (0, 0, 0)),
                  pl.BlockSpec((D_MODEL, D_MODEL), lambda i: (0, 0))],
        out_specs=[xspec, xspec],
        out_shape=[jax.ShapeDtypeStruct((T, D_MODEL), F32)] * 2,
        compiler_params=_cparams(("arbitrary",)),
        name="merge_outproj",
    )(ya, yb, yc, yd, proj, x, mod, ssd_norm.reshape(1, D_BRANCH), norm2_g.reshape(1, D_MODEL), wb_bf16, wo_bf16)


def _router_kernel(h_ref, rw_ref, rb_ref, wt_ref, lpos_ref, tlen_ref, tstart_ref, tcar_ref, carry):
    i = pl.program_id(0)
    tm = h_ref.shape[0]
    E = N_EXPERTS
    gsz = E // N_EXPERT_GROUPS
    neg_inf = -jnp.inf

    @pl.when(i == 0)
    def _():
        carry[...] = jnp.zeros_like(carry)

    scores = _sigmoid(_dot_nt(rw_ref[...], h_ref[...], precision=HIGHEST))
    biased = scores + rb_ref[...]
    eidx = _iota((E, tm), 0)
    ridx = _iota((gsz, tm), 0)

    slabs = [biased[g * gsz:(g + 1) * gsz, :] for g in range(N_EXPERT_GROUPS)]
    gs = []
    for v in slabs:
        m1 = jnp.max(v, axis=0, keepdims=True)
        i1 = jnp.min(jnp.where(v == m1, ridx, gsz), axis=0, keepdims=True)
        m2 = jnp.max(jnp.where(ridx == i1, neg_inf, v), axis=0, keepdims=True)
        gs.append(m1 + m2)
    masked = []
    for g in range(N_EXPERT_GROUPS):
        rank = jnp.zeros((1, tm), I32)
        for g2 in range(N_EXPERT_GROUPS):
            if g2 == g:
                continue
            ahead = (gs[g2] >= gs[g]) if g2 < g else (gs[g2] > gs[g])
            rank = rank + ahead.astype(I32)
        masked.append(jnp.where(rank < TOPK_GROUPS, slabs[g], MASK_NEG))
    cur = jnp.concatenate(masked, axis=0)

    krow = _iota((TOP_K, tm), 0)
    onehot = jnp.zeros((E, tm), F32)
    picks = []
    wsum = jnp.zeros((1, tm), F32)
    for kk in range(TOP_K):
        m = jnp.max(cur, axis=0, keepdims=True)
        ik = jnp.min(jnp.where(cur == m, eidx, E), axis=0, keepdims=True)
        hit = eidx == ik
        wk = jnp.sum(jnp.where(hit, scores, 0.0), axis=0, keepdims=True)
        cur = jnp.where(hit, neg_inf, cur)
        onehot = onehot + hit.astype(F32)
        picks.append((ik, wk))
        wsum = wsum + wk
    wt_out = jnp.zeros((TOP_K, tm), F32)
    for kk, (ik, wk) in enumerate(picks):
        wt_out = jnp.where(krow == kk, wk / wsum * ROUTED_SCALE, wt_out)

    earlier = (_iota((tm, tm), 0) < _iota((tm, tm), 1)).astype(BF16)
    before = _dot(onehot.astype(BF16), earlier)
    pad = lambda c: (c.astype(I32) + (RUN_ROWS - 1)) // RUN_ROWS * RUN_ROWS
    run_len = pad(jnp.sum(onehot, axis=1, keepdims=True) + jnp.zeros((E, LANES), F32)).astype(F32)
    run_start = _cumsum_rows(run_len, False) - run_len
    local = before + run_start[:, 0:1]
    lpos_out = jnp.zeros((TOP_K, tm), I32)
    for kk, (ik, wk) in enumerate(picks):
        lp = jnp.sum(jnp.where(eidx == ik, local, 0.0), axis=0, keepdims=True)
        lpos_out = jnp.where(krow == kk, lp.astype(I32), lpos_out)
    wt_ref[...] = wt_out
    lpos_ref[...] = lpos_out

    onehot_l = jnp.concatenate([onehot, jnp.zeros((LANES - E, tm), F32)], axis=0).astype(BF16)
    len_row = pad(_dot_nt(jnp.ones((8, tm), BF16), onehot_l)).astype(F32)
    lanes_before = (_iota((LANES, LANES), 0) < _iota((LANES, LANES), 1)).astype(BF16)
    tlen_ref[...] = len_row.astype(I32)
    tstart_ref[...] = _dot(len_row.astype(BF16), lanes_before).astype(I32)
    tcar_ref[...] = carry[...].astype(I32)
    carry[...] = carry[...] + len_row


def _router(h2, router_w, router_b, *, tm):
    T = h2.shape[0]
    nt = T // tm
    ospec = pl.BlockSpec((TOP_K, tm), lambda i: (0, i))
    tspec = pl.BlockSpec((None, 8, LANES), lambda i: (i, 0, 0))
    tshape = jax.ShapeDtypeStruct((nt, 8, LANES), I32)
    wt, lpos, tlen, tstart, tcar = pl.pallas_call(
        _router_kernel,
        grid=(nt,),
        in_specs=[pl.BlockSpec((tm, D_MODEL), lambda i: (i, 0)),
                  pl.BlockSpec((N_EXPERTS, D_MODEL), lambda i: (0, 0)),
                  pl.BlockSpec((N_EXPERTS, 1), lambda i: (0, 0))],
        out_specs=[ospec, ospec, tspec, tspec, tspec],
        out_shape=[jax.ShapeDtypeStruct((TOP_K, T), F32), jax.ShapeDtypeStruct((TOP_K, T), I32),
                   tshape, tshape, tshape],
        scratch_shapes=[pltpu.VMEM((8, LANES), F32)],
        compiler_params=_cparams(("arbitrary",)),
        name="moe_router",
    )(h2, router_w.T, router_b.reshape(N_EXPERTS, 1))
    table = lambda a: a[:, 0, :N_EXPERTS]
    return wt, lpos, table(tlen), table(tstart), table(tcar)


def _dispatch_kernel(len_ref, lst_ref, car_ref, seg_ref, lpos_ref, h_ref, xs_ref, xloc, sem):
    i = pl.program_id(0)
    tm = h_ref.shape[0]
    nloc = xloc.shape[0]
    R = RUN_ROWS
    lpos = lpos_ref[...]
    hb = h_ref[...].astype(BF16)
    rb = 256
    for blk in range(nloc // rb):
        r = _iota((rb, tm), 0) + blk * rb
        hit = r == lpos[0:1, :]
        for kk in range(1, TOP_K):
            hit = hit | (r == lpos[kk:kk + 1, :])
        xloc[blk * rb:(blk + 1) * rb, :] = _dot(hit.astype(BF16), hb)

    def piece(src_row, dst_row):
        return pltpu.make_async_copy(xloc.at[pl.ds(pl.multiple_of(src_row, R), R), :],
                                     xs_ref.at[pl.ds(pl.multiple_of(dst_row, R), R), :], sem)

    def per_expert(e, issued):
        pieces = len_ref[i, e] // R
        ls = lst_ref[i, e]
        gd = seg_ref[e] + car_ref[i, e]

        def issue(j, c):
            piece(ls + j * R, gd + j * R).start()
            return c

        lax.fori_loop(0, pieces, issue, 0)
        return issued + pieces

    total = lax.fori_loop(0, N_EXPERTS, per_expert, 0)

    def wait(j, c):
        piece(0, 0).wait()
        return c

    lax.fori_loop(0, total, wait, 0)


def _dispatch(h2, lpos, tile_len, tile_lstart, tile_car, seg_start, *, tm):
    T = h2.shape[0]
    nt = T // tm
    n_rows = T * TOP_K + N_EXPERTS * nt * RUN_ROWS
    nloc = tm * TOP_K + N_EXPERTS * RUN_ROWS
    return pl.pallas_call(
        _dispatch_kernel,
        grid_spec=pltpu.PrefetchScalarGridSpec(
            num_scalar_prefetch=4,
            grid=(nt,),
            in_specs=[pl.BlockSpec((TOP_K, tm), lambda i, *_: (0, i)),
                      pl.BlockSpec((tm, D_MODEL), lambda i, *_: (i, 0))],
            out_specs=pl.BlockSpec(memory_space=pl.ANY),
            scratch_shapes=[pltpu.VMEM((nloc, D_MODEL), F32), pltpu.SemaphoreType.DMA(())]),
        out_shape=jax.ShapeDtypeStruct((n_rows, D_MODEL), F32),
        compiler_params=_cparams(("arbitrary",)),
        name="moe_dispatch",
    )(tile_len, tile_lstart, tile_car, seg_start, lpos, h2)


def _expert_kernel(blk_ref, exp_ref, lo_ref, hi_ref, first_ref, x_ref, w1_ref, w3_ref, w2_ref, o_ref):
    w = pl.program_id(0)
    x = x_ref[...].astype(BF16)
    a = _dot(x, w1_ref[...].astype(BF16))
    b = _dot(x, w3_ref[...].astype(BF16))
    y = _dot((_silu(a) * b).astype(BF16), w2_ref[...].astype(BF16))
    r = _iota(y.shape, 0)
    y = jnp.where((r >= lo_ref[w]) & (r < hi_ref[w]), y, 0.0)

    @pl.when(first_ref[w] == 1)
    def _():
        o_ref[...] = y

    @pl.when(first_ref[w] == 0)
    def _():
        o_ref[...] = o_ref[...] + y


def _experts(xs, work, w1, w3, w2, *, bm):
    A = xs.shape[0]
    n_work = work[0].shape[0]
    xmap = lambda w, blk, ex, lo, hi, first: (blk[w], 0)
    wmap = lambda w, blk, ex, lo, hi, first: (ex[w], 0, 0)
    return pl.pallas_call(
        _expert_kernel,
        grid_spec=pltpu.PrefetchScalarGridSpec(
            num_scalar_prefetch=5,
            grid=(n_work,),
            in_specs=[pl.BlockSpec((bm, D_MODEL), xmap),
                      pl.BlockSpec((None, D_MODEL, D_EXPERT), wmap),
                      pl.BlockSpec((None, D_MODEL, D_EXPERT), wmap),
                      pl.BlockSpec((None, D_EXPERT, D_MODEL), wmap)],
            out_specs=pl.BlockSpec((bm, D_MODEL), xmap)),
        out_shape=jax.ShapeDtypeStruct((A, D_MODEL), F32),
        compiler_params=_cparams(("arbitrary",)),
        name="moe_experts",
    )(*work, xs, w1, w3, w2)


def _combine_kernel(len_ref, lst_ref, car_ref, seg_ref, lpos_ref, lpost_ref, wt_ref, ys_ref, h_ref, x_ref, mod_ref,
                    s1_ref, s3_ref, s2_ref, fg_ref, o_ref, yloc, sem, *, final_norm):
    i = pl.program_id(0)
    tm = h_ref.shape[0]
    nloc = yloc.shape[0]
    R = RUN_ROWS

    @pl.when(i == 0)
    def _():
        yloc[...] = jnp.zeros_like(yloc)

    def piece(src_row, dst_row):
        return pltpu.make_async_copy(ys_ref.at[pl.ds(pl.multiple_of(src_row, R), R), :],
                                     yloc.at[pl.ds(pl.multiple_of(dst_row, R), R), :], sem)

    def per_expert(e, issued):
        pieces = len_ref[i, e] // R
        ls = lst_ref[i, e]
        gd = seg_ref[e] + car_ref[i, e]

        def issue(j, c):
            piece(gd + j * R, ls + j * R).start()
            return c

        lax.fori_loop(0, pieces, issue, 0)
        return issued + pieces

    total = lax.fori_loop(0, N_EXPERTS, per_expert, 0)

    hb = h_ref[...].astype(BF16)
    mid = _silu(_dot(hb, s1_ref[...])) * _dot(hb, s3_ref[...])
    y = _dot(mid.astype(BF16), s2_ref[...])

    def wait(j, c):
        piece(0, 0).wait()
        return c

    lax.fori_loop(0, total, wait, 0)

    lpos = lpos_ref[...]
    wt = wt_ref[...]
    rb = 256
    for blk in range(nloc // rb):
        r = _iota((rb, tm), 0) + blk * rb
        pw = jnp.zeros((rb, tm), F32)
        for kk in range(TOP_K):
            pw = pw + jnp.where(r == lpos[kk:kk + 1, :], wt[kk:kk + 1, :], 0.0)
        wrow = jnp.sum(pw, axis=1, keepdims=True)
        ysc = yloc[blk * rb:(blk + 1) * rb, :] * wrow
        hi = ysc.astype(BF16)
        lo = (ysc - hi.astype(F32)).astype(BF16)
        c = _iota((tm, rb), 1) + blk * rb
        hit = c == lpost_ref[:, 0:1]
        for kk in range(1, TOP_K):
            hit = hit | (c == lpost_ref[:, kk:kk + 1])
        pt = hit.astype(BF16)
        y = y + (_dot(pt, hi) + _dot(pt, lo))
    x = x_ref[...] + mod_ref[5:6, :] * y
    if final_norm:
        x = x * lax.rsqrt(jnp.mean(x * x, axis=-1, keepdims=True) + EPS) * fg_ref[...]
    o_ref[...] = x


def _combine(ys, lpos, lpos_t, wt, tile_len, tile_lstart, tile_car, seg_start, h2, x, mod, s1, s3, s2, final_g,
             *, tm, tokens_per_mod, final_norm):
    T = x.shape[0]
    tiles_per_mod = tokens_per_mod // tm
    nloc = tm * TOP_K + N_EXPERTS * RUN_ROWS
    xspec = pl.BlockSpec((tm, D_MODEL), lambda i, *_: (i, 0))
    kspec = pl.BlockSpec((TOP_K, tm), lambda i, *_: (0, i))
    const = lambda i, *_: (0, 0)
    return pl.pallas_call(
        functools.partial(_combine_kernel, final_norm=final_norm),
        grid_spec=pltpu.PrefetchScalarGridSpec(
            num_scalar_prefetch=4,
            grid=(T // tm,),
            in_specs=[kspec,
                      pl.BlockSpec((tm, LANES), lambda i, *_: (i, 0)),
                      kspec,
                      pl.BlockSpec(memory_space=pl.ANY),
                      xspec, xspec,
                      pl.BlockSpec((None, 6, D_MODEL), lambda i, *_: (i // tiles_per_mod, 0, 0)),
                      pl.BlockSpec((D_MODEL, D_SHARED), const),
                      pl.BlockSpec((D_MODEL, D_SHARED), const),
                      pl.BlockSpec((D_SHARED, D_MODEL), const),
                      pl.BlockSpec((1, D_MODEL), const)],
            out_specs=xspec,
            scratch_shapes=[pltpu.VMEM((nloc, D_MODEL), F32), pltpu.SemaphoreType.DMA(())]),
        out_shape=jax.ShapeDtypeStruct((T, D_MODEL), F32),
        compiler_params=_cparams(("arbitrary",)),
        name="moe_combine",
    )(tile_len, tile_lstart, tile_car, seg_start, lpos, lpos_t, wt, ys, h2, x, mod, s1, s3, s2, final_g.reshape(1, D_MODEL))


def _work_list(counts, starts, n_rows, bm):
    E = N_EXPERTS
    n_blocks = n_rows // bm
    n_work = n_blocks + E - 1
    ends = starts + counts
    first_blk = starts // bm
    last_blk = jnp.where(counts > 0, (ends - 1) // bm, first_blk)
    n_items = jnp.where(counts > 0, last_blk - first_blk + 1, 0)
    item_end = jnp.cumsum(n_items)
    item_start = item_end - n_items
    w = jnp.arange(n_work, dtype=I32)
    ex = jnp.minimum(jnp.searchsorted(item_end, w, side="right"), E - 1).astype(I32)
    valid = w < item_end[-1]
    blk = first_blk[ex] + (w - item_start[ex])
    blk = jnp.where(valid, blk, blk[jnp.maximum(item_end[-1] - 1, 0)]).astype(I32)
    lo = jnp.clip(starts[ex] - blk * bm, 0, bm)
    hi = jnp.clip(ends[ex] - blk * bm, 0, bm)
    lo = jnp.where(valid, lo, 0).astype(I32)
    hi = jnp.where(valid, hi, 0).astype(I32)
    ex = jnp.where(valid, ex, ex[jnp.maximum(item_end[-1] - 1, 0)])
    prev_blk = jnp.concatenate([jnp.full((1,), -1, I32), blk[:-1]])
    first = (blk != prev_blk).astype(I32)
    return blk, ex.astype(I32), lo, hi, first


def _moe(h2, x, mod, p, final_g, *, tokens_per_mod, final_norm):
    T = x.shape[0]
    tm = 256
    wt, lpos, tile_len, tile_lstart, tile_car = _router(h2, p["router_w"], p["router_b"], tm=tm)
    seg_len = tile_car[-1] + tile_len[-1]
    seg_start = jnp.cumsum(seg_len) - seg_len
    xs = _dispatch(h2, lpos, tile_len, tile_lstart, tile_car, seg_start, tm=tm)
    bm = 512
    work = _work_list(seg_len, seg_start, xs.shape[0], bm)
    ys = _experts(xs, work, p["exp_w1"], p["exp_w3"], p["exp_w2"], bm=bm)
    lpos_t = jnp.pad(lpos.T, ((0, 0), (0, LANES - TOP_K)))
    return _combine(ys, lpos, lpos_t, wt, tile_len, tile_lstart, tile_car, seg_start, h2, x, mod,
                    p["sh_w1"], p["sh_w3"], p["sh_w2"], final_g,
                    tm=tm, tokens_per_mod=tokens_per_mod, final_norm=final_norm)


def _reorder_w_in(w_in):
    plain = D_CONV
    pieces = [w_in[..., :plain],
              w_in[..., plain + 3616:plain + 3616 + 4096],
              w_in[..., plain:plain + 3072],
              w_in[..., plain + 3088:plain + 3600],
              w_in[..., plain + 3072:plain + 3088],
              w_in[..., plain + 3600:plain + 3616]]
    out = jnp.concatenate(pieces, axis=-1)
    pad = D_PROJ - out.shape[-1]
    return jnp.pad(out, ((0, 0), (0, 0), (0, pad))).astype(BF16)


def _small_rows(ssd_vals, gdn_vals):
    row = jnp.zeros((LANES,), F32)
    row = row.at[S_DT:S_DT + 2 * SSD_HEADS].set(ssd_vals.reshape(-1))
    row = row.at[S_GA:S_GA + 2 * GDN_HEADS].set(gdn_vals.reshape(-1))
    return row.reshape(1, LANES)


def _layer_pass(x, mod, lp, hy, st_hg, st_ssd, st_gdn, final_g, *, B, L, tokens_per_mod, seg, layer, final_norm):
    T = B * L
    proj = _in_projection(x, mod, lp["norm1_g"], lp["w_in"], lp["conv_w"], lp["conv_b"],
                          tokens_per_mod=tokens_per_mod, seg=seg)
    proj3 = proj.reshape(B, L, D_PROJ)
    fmat, gmat, hspec = hy
    cb = lambda c: c // D_BRANCH
    z = _spectral_conv(fmat, gmat, hspec, 0, lp["hy_bias"][0], proj3, cb(C_HYV), proj3, cb(C_HYX1))
    ya = _spectral_conv(fmat, gmat, hspec, 1, lp["hy_bias"][1], z, 0, proj3, cb(C_HYX2))
    yb, s_hg = _hgrn_scan(proj3, lp["hg_lb"], lp["hg_norm"], st_hg, layer=layer)
    yc, s_ssd = _ssd_scan(proj3, lp["bias_row"], lp["alog_row"], lp["ssd_d"], st_ssd)
    yd, s_gdn = _gdn_scan(proj3, lp["bias_row"], lp["alog_row"], lp["gdn_norm"], st_gdn)
    flat = lambda a: a.reshape(T, D_BRANCH)
    x, h2 = _merge(flat(ya), flat(yb), flat(yc), flat(yd), proj, x, mod, lp["ssd_norm"], lp["norm2_g"],
                   lp["w_branch"], lp["w_out"], tokens_per_mod=tokens_per_mod)
    x = _moe(h2, x, mod, lp, final_g, tokens_per_mod=tokens_per_mod, final_norm=final_norm)
    return x, s_hg, s_ssd, s_gdn


def kernel(x_prompt, x_sample, state_hgrn, state_ssd, state_gdn, c, c_ctx, norm1_g, norm2_g, ada_w, ada_b, w_in, conv_w, conv_b, hy_w1, hy_b1, hy_w2, hy_b2, hy_w3, hy_bias, hg_lb, hg_norm, ssd_a_log, ssd_dt_bias, ssd_d, ssd_norm, gdn_a_log, gdn_dt_bias, gdn_norm, w_branch, w_out, router_w, router_bias, exp_w1, exp_w3, exp_w2, sh_w1, sh_w3, sh_w2, final_g):
    depth = w_in.shape[0]
    bp, lp_len = x_prompt.shape[0], x_prompt.shape[1]
    bs, ls_len = x_sample.shape[0], x_sample.shape[1]
    D = D_MODEL

    w_in_r = _reorder_w_in(w_in)
    hy_w1p = jnp.pad(hy_w1, ((0, 0), (0, LANES - hy_w1.shape[1]), (0, 0)))
    layers = []
    for l in range(depth):
        layers.append(dict(
            norm1_g=norm1_g[l], norm2_g=norm2_g[l], w_in=w_in_r[l], conv_w=conv_w[l], conv_b=conv_b[l],
            hy_bias=hy_bias[l], hg_lb=hg_lb, hg_norm=hg_norm[l],
            bias_row=_small_rows(ssd_dt_bias[l], gdn_dt_bias[l]),
            alog_row=_small_rows(ssd_a_log[l], gdn_a_log[l]),
            ssd_d=ssd_d[l], ssd_norm=ssd_norm[l], gdn_norm=gdn_norm[l],
            w_branch=w_branch[l].astype(BF16), w_out=w_out[l].astype(BF16),
            router_w=router_w[l], router_b=router_bias[l],
            exp_w1=exp_w1[l], exp_w3=exp_w3[l], exp_w2=exp_w2[l],
            sh_w1=sh_w1[l].astype(BF16), sh_w3=sh_w3[l].astype(BF16), sh_w2=sh_w2[l].astype(BF16)))

    def hyena_setup(L):
        fmat, fs = _dft_matrices(L)
        gmat = fs.T
        specs = []
        for l in range(depth):
            filt = _hyena_filters(L, hy_w1p[l], hy_b1[l], hy_w2[l], hy_b2[l], hy_w3[l])
            specs.append(_filter_spectrum(fmat, filt))
        return fmat, gmat, specs

    cond = jnp.concatenate([c_ctx.reshape(1, D), c], axis=0)
    rows = cond.shape[0]
    rows8 = (rows + 7) // 8 * 8
    cond8 = jnp.pad(cond, ((0, rows8 - rows), (0, 0)))
    mods = [_modulation(cond8, ada_w[l], ada_b[l]).reshape(rows8, 6, D) for l in range(depth)]

    fmat, gmat, specs = hyena_setup(lp_len)
    x = x_prompt.reshape(bp * lp_len, D)
    z_hg = jnp.zeros((bp, 2, HG_HEADS, HG_DK, HG_DK), F32)
    z_ssd = jnp.zeros((bp, 2, SSD_HEADS, SSD_P, SSD_N), F32)
    z_gdn = jnp.zeros((bp, 2, GDN_HEADS, GDN_DK, GDN_DK), F32)
    hg_states, ssd_states, gdn_states = [], [], []
    for l in range(depth):
        x, s_hg, s_ssd, s_gdn = _layer_pass(
            x, mods[l][0:1], layers[l], (fmat, gmat, specs[l]), z_hg, z_ssd, z_gdn, final_g,
            B=bp, L=lp_len, tokens_per_mod=bp * lp_len, seg=lp_len, layer=l, final_norm=(l == depth - 1))
        hg_states.append(s_hg)
        ssd_states.append(s_ssd)
        gdn_states.append(s_gdn)
    y_prompt = x.reshape(bp, lp_len, D)
    new_hg = jnp.stack(hg_states, axis=1)
    new_ssd = jnp.stack(ssd_states, axis=1)
    new_gdn = jnp.stack(gdn_states, axis=1)

    fmat, gmat, specs = hyena_setup(ls_len)
    x = x_sample.reshape(bs * ls_len, D)
    for l in range(depth):
        x, _, _, _ = _layer_pass(
            x, mods[l][1:1 + bs], layers[l], (fmat, gmat, specs[l]),
            state_hgrn[:, l], state_ssd[:, l], state_gdn[:, l], final_g,
            B=bs, L=ls_len, tokens_per_mod=ls_len, seg=GRID_W, layer=l, final_norm=(l == depth - 1))
    y_sample = x.reshape(bs, ls_len, D)
    return (y_prompt, y_sample, new_hg, new_ssd, new_gdn)
```

```python
import functools
import math

import numpy as np
import jax
import jax.numpy as jnp
from jax import lax
from jax.experimental import pallas as pl
from jax.experimental.pallas import tpu as pltpu

F32 = jnp.float32
BF16 = jnp.bfloat16
I32 = jnp.int32
HIGHEST = lax.Precision.HIGHEST

D_MODEL = 1024
GRID_W = 64
EPS = 1e-6
LOG_FLOOR = 1e-30
MASK_NEG = -1e30
N_BRANCH = 4
D_BRANCH = 512
HY_POS_FREQS = 16
HY_FILTER_HIDDEN = 64
HY_FAST_DECAY = 0.3
HY_SLOW_DECAY = 1.5
HY_DECAY_TARGET = 1e-2
HG_HEADS = 4
HG_DK = 128
HG_CHUNK = 16
SSD_HEADS = 8
SSD_P = 64
SSD_N = 128
SSD_CHUNK = 64
GDN_HEADS = 4
GDN_DK = 128
GDN_CHUNK = 64
N_EXPERTS = 64
TOP_K = 8
N_EXPERT_GROUPS = 8
TOPK_GROUPS = 4
D_EXPERT = 256
D_SHARED = 256
ROUTED_SCALE = 2.5

LANES = 128
RUN_ROWS = 8

D_CONV = 4096
C_HYV, C_HYX1, C_HYX2 = 0, 512, 1024
C_SSDX, C_SSDB, C_SSDC = 1536, 2048, 2304
C_GQ, C_GK, C_GV = 2560, 3072, 3584
P_MERGE = 4096
P_HGQ, P_HGFF, P_HGFB, P_HGV, P_HGG = 8192, 8704, 9216, 9728, 10240
P_SSDZ, P_GDNG, P_SMALL = 10752, 11264, 11776
D_PROJ = 12288
S_DT, S_GA, S_GB = 0, 16, 24

VMEM_LIMIT = 56 * 1024 * 1024


def _cparams(sem):
    return pltpu.CompilerParams(dimension_semantics=sem, vmem_limit_bytes=VMEM_LIMIT)


def _sigmoid(x):
    return 1.0 / (1.0 + jnp.exp(-x))


def _silu(x):
    return x * _sigmoid(x)


def _softplus(x):
    return jnp.maximum(x, 0.0) + jnp.log(1.0 + jnp.exp(-jnp.abs(x)))


def _dot(a, b, precision=None):
    return jnp.dot(a, b, preferred_element_type=F32, precision=precision)


def _dot_nt(a, b, precision=None):
    return lax.dot_general(a, b, (((1,), (1,)), ((), ())), preferred_element_type=F32, precision=precision)


def _dot_tn(a, b, precision=None):
    return lax.dot_general(a, b, (((0,), (0,)), ((), ())), preferred_element_type=F32, precision=precision)


def _bdot(a, b):
    return _dot(a.astype(BF16), b.astype(BF16))


def _bdot_nt(a, b):
    return _dot_nt(a.astype(BF16), b.astype(BF16))


def _bdot_tn(a, b):
    return _dot_tn(a.astype(BF16), b.astype(BF16))


def _iota(shape, dim):
    return lax.broadcasted_iota(I32, shape, dim)


def _cumsum_rows(g, reverse):
    n = g.shape[0]
    row = _iota(g.shape, 0)
    sh = 1
    while sh < n:
        if reverse:
            g = g + jnp.where(row < n - sh, pltpu.roll(g, n - sh, 0), 0.0)
        else:
            g = g + jnp.where(row >= sh, pltpu.roll(g, sh, 0), 0.0)
        sh *= 2
    return g


def _lane_pick(a, j):
    return jnp.sum(jnp.where(_iota(a.shape, 1) == j, a, 0.0), axis=1, keepdims=True)


def _split3(a):
    a1 = a.astype(BF16)
    r1 = a - a1.astype(F32)
    a2 = r1.astype(BF16)
    a3 = (r1 - a2.astype(F32)).astype(BF16)
    return a1, a2, a3


def _dot3(a, b):
    a1, a2, _ = _split3(a)
    b1, b2, _ = _split3(b)
    return _dot(a1, b1) + (_dot(a1, b2) + _dot(a2, b1))


def _row_pick(a, j):
    sel = (_iota((8, a.shape[1]), 1) == j).astype(BF16)
    a1, a2, a3 = _split3(a)
    return (_dot_nt(sel, a1) + (_dot_nt(sel, a2) + _dot_nt(sel, a3)))[0:1, :]


def _tri_mask(n, reverse, strict=False):
    t = _iota((n, n), 0)
    s = _iota((n, n), 1)
    if reverse:
        return (s > t) if strict else (s >= t)
    return (s < t) if strict else (s <= t)


def _masked_exp(mask, diff):
    return jnp.where(mask, jnp.exp(jnp.where(mask, diff, 0.0)), 0.0)


def _mod_kernel(c_ref, w_ref, b_ref, o_ref):
    o_ref[...] = _dot(_silu(c_ref[...]), w_ref[...], precision=HIGHEST) + b_ref[...]


def _modulation(cond8, ada_w, ada_b, *, layer):
    rows = cond8.shape[0]
    tn = 1536
    n = ada_w.shape[2]
    return pl.pallas_call(
        _mod_kernel,
        grid=(n // tn,),
        in_specs=[pl.BlockSpec((rows, D_MODEL), lambda j: (0, 0)),
                  pl.BlockSpec((None, D_MODEL, tn), lambda j: (layer, 0, j)),
                  pl.BlockSpec((1, tn), lambda j: (0, j))],
        out_specs=pl.BlockSpec((rows, tn), lambda j: (0, j)),
        out_shape=jax.ShapeDtypeStruct((rows, n), F32),
        compiler_params=_cparams(("arbitrary",)),
        name="adaln_mod",
    )(cond8, ada_w, ada_b.reshape(1, n))


def _inproj_kernel(x_ref, mod_ref, g_ref, w_ref, cw_ref, cb_ref, o_ref, h_scr, *, seg, n_conv_tiles):
    j = pl.program_id(1)

    @pl.when(j == 0)
    def _():
        x = x_ref[...]
        xn = x * lax.rsqrt(jnp.mean(x * x, axis=-1, keepdims=True) + EPS) * g_ref[...]
        m = mod_ref[...]
        h_scr[...] = (xn * (1.0 + m[1:2]) + m[0:1]).astype(BF16)

    y = _dot(h_scr[...], w_ref[...])

    @pl.when(j < n_conv_tiles)
    def _():
        tm = y.shape[0]
        pos = _iota(y.shape, 0) & (seg - 1)
        prev = jnp.where(pos == 0, 0.0, pltpu.roll(y, 1, 0))
        nxt = jnp.where(pos == seg - 1, 0.0, pltpu.roll(y, tm - 1, 0))
        cw = cw_ref[...]
        o_ref[...] = cb_ref[...] + prev * cw[0:1] + y * cw[1:2] + nxt * cw[2:3]

    @pl.when(j >= n_conv_tiles)
    def _():
        o_ref[...] = y


def _in_projection(x, mod, norm_g, w_bf16, conv_w, conv_b, *, layer, tokens_per_mod, seg):
    T = x.shape[0]
    tm = min(1024, tokens_per_mod)
    tn = 1024
    n_conv_tiles = D_CONV // tn
    tiles_per_mod = tokens_per_mod // tm
    kern = functools.partial(_inproj_kernel, seg=seg, n_conv_tiles=n_conv_tiles)
    cmap = lambda i, j: (0, jnp.minimum(j, n_conv_tiles - 1))
    return pl.pallas_call(
        kern,
        grid=(T // tm, D_PROJ // tn),
        in_specs=[pl.BlockSpec((tm, D_MODEL), lambda i, j: (i, 0)),
                  pl.BlockSpec((None, 6, D_MODEL), lambda i, j: (i // tiles_per_mod, 0, 0)),
                  pl.BlockSpec((1, D_MODEL), lambda i, j: (0, 0)),
                  pl.BlockSpec((None, D_MODEL, tn), lambda i, j: (layer, 0, j)),
                  pl.BlockSpec((3, tn), cmap),
                  pl.BlockSpec((1, tn), cmap)],
        out_specs=pl.BlockSpec((tm, tn), lambda i, j: (i, j)),
        out_shape=jax.ShapeDtypeStruct((T, D_PROJ), F32),
        scratch_shapes=[pltpu.VMEM((tm, D_MODEL), BF16)],
        compiler_params=_cparams(("arbitrary", "arbitrary")),
        name="in_proj",
    )(x, mod, norm_g.reshape(1, D_MODEL), w_bf16, conv_w, conv_b.reshape(1, D_CONV))


def _hyfilt_kernel(w1_ref, b1_ref, w2_ref, b2_ref, w3_ref, o_ref, *, L):
    i = pl.program_id(1)
    tl = o_ref.shape[0]
    t = (_iota((tl, LANES), 0) + i * tl).astype(F32) / L
    lane = _iota((tl, LANES), 1)
    band = jnp.where(lane <= HY_POS_FREQS, lane, lane - HY_POS_FREQS).astype(F32)
    ang = 2.0 * math.pi * t * band
    feats = jnp.where(lane == 0, t,
                      jnp.where(lane <= HY_POS_FREQS, jnp.sin(ang),
                                jnp.where(lane <= 2 * HY_POS_FREQS, jnp.cos(ang), 0.0)))
    hdn = jnp.sin(_dot(feats, w1_ref[...], precision=HIGHEST) + b1_ref[...])
    hdn = jnp.sin(_dot(hdn, w2_ref[...], precision=HIGHEST) + b2_ref[...])
    filt = _dot(hdn, w3_ref[...], precision=HIGHEST)
    max_decay = math.log(HY_DECAY_TARGET) / HY_FAST_DECAY
    min_decay = math.log(HY_DECAY_TARGET) / HY_SLOW_DECAY
    n = filt.shape[1]
    ch = (_iota((tl, n), 1) & (D_BRANCH - 1)).astype(F32)
    delta = min_decay + ch * ((max_decay - min_decay) / (D_BRANCH - 1))
    tt = (_iota((tl, n), 0) + i * tl).astype(F32) / L
    o_ref[...] = filt * jnp.exp(-tt * jnp.abs(delta))


def _hyena_filters(L, w1p, b1, w2, b2, w3):
    tl = min(L, 256)
    n = 2 * D_BRANCH
    return pl.pallas_call(
        functools.partial(_hyfilt_kernel, L=L),
        grid=(2, L // tl),
        in_specs=[pl.BlockSpec((LANES, HY_FILTER_HIDDEN), lambda d, i: (0, 0)),
                  pl.BlockSpec((1, HY_FILTER_HIDDEN), lambda d, i: (0, 0)),
                  pl.BlockSpec((HY_FILTER_HIDDEN, HY_FILTER_HIDDEN), lambda d, i: (0, 0)),
                  pl.BlockSpec((1, HY_FILTER_HIDDEN), lambda d, i: (0, 0)),
                  pl.BlockSpec((HY_FILTER_HIDDEN, n), lambda d, i: (0, d))],
        out_specs=pl.BlockSpec((None, tl, n), lambda d, i: (d, i, 0)),
        out_shape=jax.ShapeDtypeStruct((2, L, n), F32),
        compiler_params=_cparams(("arbitrary", "arbitrary")),
        name="hyena_filters",
    )(w1p, b1.reshape(1, -1), w2, b2.reshape(1, -1), w3)


def _dftgen_kernel(f_ref, fs_ref, *, L, tk):
    i = pl.program_id(0)
    N = 2 * L
    k = _iota((tk, LANES), 0) + i * tk
    lane = _iota((tk, LANES), 1)
    w = 2.0 * math.pi / N
    a0 = ((k * lane) & (N - 1)).astype(F32) * w
    c0, s0 = jnp.cos(a0), jnp.sin(a0)
    a1 = ((k * lane * LANES) & (N - 1)).astype(F32) * w
    c1, s1 = jnp.cos(a1), jnp.sin(a1)
    alt = jnp.where((lane & 1) == 0, 1.0, -1.0)
    coef = jnp.where(k == 0, 1.0 / N, 2.0 / N)
    for n1 in range(L // LANES):
        c1b = c1[:, n1:n1 + 1]
        s1b = s1[:, n1:n1 + 1]
        cosb = c1b * c0 - s1b * s0
        sinb = jnp.where(k == 0, alt, -(s1b * c0 + c1b * s0))
        cols = slice(n1 * LANES, (n1 + 1) * LANES)
        f_ref[0:tk, cols] = cosb.astype(BF16)
        f_ref[tk:2 * tk, cols] = sinb.astype(BF16)
        fs_ref[0:tk, cols] = (coef * cosb).astype(BF16)
        fs_ref[tk:2 * tk, cols] = (coef * sinb).astype(BF16)


def _dft_matrices(L):
    tk = min(L, 256)
    spec = pl.BlockSpec((2 * tk, L), lambda i: (i, 0))
    return pl.pallas_call(
        functools.partial(_dftgen_kernel, L=L, tk=tk),
        grid=(L // tk,),
        in_specs=[],
        out_specs=[spec, spec],
        out_shape=[jax.ShapeDtypeStruct((2 * L, L), BF16)] * 2,
        compiler_params=_cparams(("arbitrary",)),
        name="dft_matrices",
    )()


def _dfth_kernel(f_ref, h_ref, o_ref, hp_scr, *, tk):
    i = pl.program_id(1)
    n = o_ref.shape[1]

    @pl.when(i == 0)
    def _():
        hc = h_ref[0]
        ha = jnp.where(_iota(hc.shape, 0) == 0, 0.0, h_ref[1])
        hp_scr[:, 0:n] = (hc + ha).astype(BF16)
        hp_scr[:, n:2 * n] = (hc - ha).astype(BF16)

    u = _dot(f_ref[...], hp_scr[...])
    r = _iota((2 * tk, n), 0)
    from_sum = (r < tk) | ((r == tk) & (i == 0))
    o_ref[...] = jnp.where(from_sum, u[:, 0:n], u[:, n:2 * n])


def _filter_spectrum(fmat, filt):
    L = fmat.shape[1]
    tk = min(L, 256)
    C = filt.shape[2]
    tc = 256
    return pl.pallas_call(
        functools.partial(_dfth_kernel, tk=tk),
        grid=(C // tc, L // tk),
        in_specs=[pl.BlockSpec((2 * tk, L), lambda c, i: (i, 0)),
                  pl.BlockSpec((2, L, tc), lambda c, i: (0, 0, c))],
        out_specs=pl.BlockSpec((2 * tk, tc), lambda c, i: (i, c)),
        out_shape=jax.ShapeDtypeStruct((2 * L, C), F32),
        scratch_shapes=[pltpu.VMEM((L, 2 * tc), BF16)],
        compiler_params=_cparams(("arbitrary", "arbitrary")),
        name="filter_spectrum",
    )(fmat, filt)


def _dfta_kernel(f_ref, u_ref, h_ref, y_ref, u_scr, *, tk):
    i = pl.program_id(1)

    @pl.when(i == 0)
    def _():
        u_scr[...] = u_ref[...].astype(BF16)

    uf = _dot(f_ref[...], u_scr[...])
    ur, ui = uf[0:tk], uf[tk:2 * tk]
    hr, hi = h_ref[0:tk, :], h_ref[tk:2 * tk, :]
    dc = (_iota(ur.shape, 0) == 0) & (i == 0)
    y_ref[0:tk, :] = jnp.where(dc, ur * hr, ur * hr - ui * hi).astype(BF16)
    y_ref[tk:2 * tk, :] = jnp.where(dc, ui * hi, ur * hi + ui * hr).astype(BF16)


def _dftb_kernel(g_ref, y_ref, u_ref, x_ref, b_ref, o_ref):
    y = _dot(g_ref[...], y_ref[...])
    o_ref[...] = x_ref[...] * (y + u_ref[...] * b_ref[...])


def _spectral_conv(fmat, gmat, hspec, h_col, bias, u_arr, u_col, gate_arr, gate_col):
    B, L = u_arr.shape[0], u_arr.shape[1]
    C = D_BRANCH
    tk = min(L, 256)
    y = pl.pallas_call(
        functools.partial(_dfta_kernel, tk=tk),
        grid=(B, L // tk),
        in_specs=[pl.BlockSpec((2 * tk, L), lambda b, i: (i, 0)),
                  pl.BlockSpec((None, L, C), lambda b, i: (b, 0, u_col)),
                  pl.BlockSpec((2 * tk, C), lambda b, i: (i, h_col))],
        out_specs=pl.BlockSpec((None, 2 * tk, C), lambda b, i: (b, i, 0)),
        out_shape=jax.ShapeDtypeStruct((B, 2 * L, C), BF16),
        scratch_shapes=[pltpu.VMEM((L, C), BF16)],
        compiler_params=_cparams(("arbitrary", "arbitrary")),
        name="hyena_dft_fwd",
    )(fmat, u_arr, hspec)
    tr = min(L, 256)
    return pl.pallas_call(
        _dftb_kernel,
        grid=(B, L // tr),
        in_specs=[pl.BlockSpec((tr, 2 * L), lambda b, i: (i, 0)),
                  pl.BlockSpec((None, 2 * L, C), lambda b, i: (b, 0, 0)),
                  pl.BlockSpec((None, tr, C), lambda b, i: (b, i, u_col)),
                  pl.BlockSpec((None, tr, C), lambda b, i: (b, i, gate_col)),
                  pl.BlockSpec((1, C), lambda b, i: (0, 0))],
        out_specs=pl.BlockSpec((None, tr, C), lambda b, i: (b, i, 0)),
        out_shape=jax.ShapeDtypeStruct((B, L, C), F32),
        compiler_params=_cparams(("arbitrary", "arbitrary")),
        name="hyena_dft_inv",
    )(gmat, y, u_arr, gate_arr, bias.reshape(1, C))


def _hgrn_kernel(q_ref, ff_ref, fb_ref, v_ref, g_ref, lb_ref, nrm_ref, s0_ref, y_ref, sf_ref,
                 of_scr, ob_scr, st_scr, *, L, layer, depth):
    C = HG_CHUNK
    W = HG_DK
    nc = L // C
    nh = q_ref.shape[1] // W
    ridx = _iota((C, W), 0)
    o_scrs = (of_scr, ob_scr)
    f_refs = (ff_ref, fb_ref)

    def lower_bound(d, cols):
        rows = [lb_ref[d, l:l + 1, cols] for l in range(depth)]
        m = rows[0]
        for r in rows[1:]:
            m = jnp.maximum(m, r)
        es = [jnp.exp(r - m) for r in rows]
        tot = es[0]
        for e in es[1:]:
            tot = tot + e
        acc = es[0] / tot
        for e in es[1:layer + 1]:
            acc = acc + e / tot
        return acc - es[0] / tot

    lbs = [[lower_bound(d, slice(hh * W, (hh + 1) * W)) for hh in range(nh)] for d in (0, 1)]
    for d in (0, 1):
        for hh in range(nh):
            st_scr[d, hh] = s0_ref[d, hh].T

    def step(c, d, hh):
        rev = d == 1
        rows = pl.ds(pl.multiple_of(c * C, C), C)
        cols = slice(hh * W, (hh + 1) * W)
        lb = lbs[d][hh]
        q = _silu(q_ref[rows, cols])
        uf = f_refs[d][rows, cols]
        v = v_ref[rows, cols]
        f = lb + (1.0 - lb) * _sigmoid(uf)
        g = jnp.log(jnp.maximum(f, LOG_FLOOR))
        kin = (1.0 - lb) * _sigmoid(-uf)
        b = _cumsum_rows(g, rev)
        st = st_scr[d, hh]
        o = _bdot_nt(q * jnp.exp(b), st)
        intra = jnp.zeros((C, W), F32)
        for t in range(C):
            mask = (ridx >= t) if rev else (ridx <= t)
            pair = _masked_exp(mask, b[t:t + 1, :] - b)
            a = jnp.sum(pair * (q[t:t + 1, :] * kin), axis=1, keepdims=True)
            row = jnp.sum(a * v, axis=0, keepdims=True)
            intra = jnp.where(ridx == t, row, intra)
        b_end = b[0:1, :] if rev else b[C - 1:C, :]
        st_scr[d, hh] = st * jnp.exp(b_end) + _bdot_tn(v, kin * jnp.exp(b_end - b))
        o_scrs[d][rows, cols] = o + intra

    def body(ci, carry):
        for d in (0, 1):
            for hh in range(nh):
                step((nc - 1 - ci) if d == 1 else ci, d, hh)
        return carry

    lax.fori_loop(0, nc, body, 0)
    for d in (0, 1):
        for hh in range(nh):
            sf_ref[d, hh] = st_scr[d, hh].T

    tr = min(L, 256)

    def fin(i, carry):
        rows = pl.ds(pl.multiple_of(i * tr, tr), tr)
        for hh in range(nh):
            cols = slice(hh * W, (hh + 1) * W)
            o = of_scr[rows, cols] + ob_scr[rows, cols]
            on = o * lax.rsqrt(jnp.mean(o * o, axis=-1, keepdims=True) + EPS) * nrm_ref[...]
            y_ref[rows, cols] = on * _silu(g_ref[rows, cols])
        return carry

    lax.fori_loop(0, L // tr, fin, 0)


def _hgrn_scan(proj3, hg_lb, hg_norm, s0, *, layer):
    B, L = proj3.shape[0], proj3.shape[1]
    depth = hg_lb.shape[1]
    nh = 2
    W = nh * HG_DK
    col = lambda base: (lambda b, h: (b, 0, base // W + h))
    st_spec = pl.BlockSpec((None, 2, nh, HG_DK, HG_DK), lambda b, h: (b, 0, h, 0, 0))
    return pl.pallas_call(
        functools.partial(_hgrn_kernel, L=L, layer=layer, depth=depth),
        grid=(B, HG_HEADS // nh),
        in_specs=[pl.BlockSpec((None, L, W), col(P_HGQ)),
                  pl.BlockSpec((None, L, W), col(P_HGFF)),
                  pl.BlockSpec((None, L, W), col(P_HGFB)),
                  pl.BlockSpec((None, L, W), col(P_HGV)),
                  pl.BlockSpec((None, L, W), col(P_HGG)),
                  pl.BlockSpec((2, depth, W), lambda b, h: (0, 0, h)),
                  pl.BlockSpec((1, HG_DK), lambda b, h: (0, 0)),
                  st_spec],
        out_specs=[pl.BlockSpec((None, L, W), lambda b, h: (b, 0, h)), st_spec],
        out_shape=[jax.ShapeDtypeStruct((B, L, D_BRANCH), F32),
                   jax.ShapeDtypeStruct(s0.shape, F32)],
        scratch_shapes=[pltpu.VMEM((L, W), F32), pltpu.VMEM((L, W), F32),
                        pltpu.VMEM((2, nh, HG_DK, HG_DK), F32)],
        compiler_params=_cparams(("arbitrary", "arbitrary")),
        name="hgrn2_scan",
    )(proj3, proj3, proj3, proj3, proj3, hg_lb, hg_norm.reshape(1, HG_DK), s0)


def _ssd_kernel(x_ref, bm_ref, cm_ref, z_ref, sm_ref, bias_ref, alog_ref, dskip_ref, s0_ref,
                y_ref, sf_ref, of_scr, ob_scr, st_scr, *, L):
    C = SSD_CHUNK
    W = LANES
    nc = L // C
    npairs = x_ref.shape[1] // W
    grp = pl.program_id(1)
    lane_lo = _iota((C, W), 1) < SSD_P
    row_lo = _iota((W, SSD_N), 0) < SSD_P
    pick2 = lambda a0, a1: jnp.where(lane_lo, a0, a1)
    o_scrs = (of_scr, ob_scr)
    incls = (_tri_mask(C, False), _tri_mask(C, True))
    for d in (0, 1):
        for pp in range(npairs):
            st_scr[d, pp] = s0_ref[d, pp]

    heads = [(d, pp, hh) for d in (0, 1) for pp in range(npairs) for hh in (0, 1)]
    pairs = [(d, pp) for d in (0, 1) for pp in range(npairs)]

    def body(ci, carry):
        rows_d, bm, cm, dt_all, cum_all, gram = [], [], [], [], [], []
        for d in (0, 1):
            c = (nc - 1 - ci) if d == 1 else ci
            rows = pl.ds(pl.multiple_of(c * C, C), C)
            rows_d.append(rows)
            bm.append(_silu(bm_ref[rows, :]))
            cm.append(_silu(cm_ref[rows, :]))
            dt_all.append(_softplus(sm_ref[rows, :] + bias_ref[...]))
            cum_all.append(_cumsum_rows(-jnp.exp(alog_ref[...]) * dt_all[d], d == 1))
        lane_of = lambda u: S_DT + u[0] * SSD_HEADS + 2 * (npairs * grp + u[1]) + u[2]
        cumr = {u: _row_pick(cum_all[u[0]], lane_of(u)) for u in heads}
        for d in (0, 1):
            gram.append(_bdot_nt(cm[d], bm[d]))
        dtc = {u: _lane_pick(dt_all[u[0]], lane_of(u)) for u in heads}
        cumc = {u: _lane_pick(cum_all[u[0]], lane_of(u)) for u in heads}
        end = {u: (cumc[u][0:1, :] if u[0] == 1 else cumc[u][C - 1:C, :]) for u in heads}
        xs = {pr: _silu(x_ref[rows_d[pr[0]], slice(pr[1] * W, (pr[1] + 1) * W)]) for pr in pairs}
        st = {pr: st_scr[pr[0], pr[1]] for pr in pairs}
        y_state = {u: _bdot_nt(cm[u[0]] * jnp.exp(cumc[u]), st[u[:2]]) for u in heads}
        scores = {u: gram[u[0]] * _masked_exp(incls[u[0]], cumc[u] - cumr[u]) for u in heads}
        y_intra = {}
        for u in heads:
            head_x = jnp.where(lane_lo if u[2] == 0 else ~lane_lo, xs[u[:2]], 0.0) * dtc[u]
            y_intra[u] = _bdot(scores[u], head_x)
        for pr in pairs:
            u0, u1 = pr + (0,), pr + (1,)
            xdt_e = xs[pr] * pick2(dtc[u0] * jnp.exp(end[u0] - cumc[u0]), dtc[u1] * jnp.exp(end[u1] - cumc[u1]))
            st_scr[pr[0], pr[1]] = (st[pr] * jnp.where(row_lo, jnp.exp(end[u0]), jnp.exp(end[u1]))
                                    + _bdot_tn(xdt_e, bm[pr[0]]))
            o_scrs[pr[0]][rows_d[pr[0]], slice(pr[1] * W, (pr[1] + 1) * W)] = pick2(
                y_intra[u0] + y_state[u0], y_intra[u1] + y_state[u1])
        return carry

    lax.fori_loop(0, nc, body, 0)
    for d in (0, 1):
        for pp in range(npairs):
            sf_ref[d, pp] = st_scr[d, pp]

    tr = min(L, 256)

    def fin(i, carry):
        rows = pl.ds(pl.multiple_of(i * tr, tr), tr)
        y = of_scr[rows, :] + ob_scr[rows, :] + _silu(x_ref[rows, :]) * dskip_ref[...]
        y_ref[rows, :] = y * _silu(z_ref[rows, :])
        return carry

    lax.fori_loop(0, L // tr, fin, 0)


def _ssd_scan(proj3, bias_row, alog_row, ssd_d, s0):
    B, L = proj3.shape[0], proj3.shape[1]
    n_groups = 2
    n_pairs = SSD_HEADS // 2
    ppg = n_pairs // n_groups
    W = ppg * LANES
    s0p = s0.reshape(B, 2, n_pairs, 2 * SSD_P, SSD_N)
    dskip = jnp.repeat(ssd_d, SSD_P).reshape(n_groups, 1, W)
    col = lambda base: (lambda b, g: (b, 0, base // W + g))
    grp = lambda base: (lambda b, g: (b, 0, base // LANES + g))
    st_spec = pl.BlockSpec((None, 2, ppg, LANES, SSD_N), lambda b, g: (b, 0, g, 0, 0))
    y, sf = pl.pallas_call(
        functools.partial(_ssd_kernel, L=L),
        grid=(B, n_groups),
        in_specs=[pl.BlockSpec((None, L, W), col(C_SSDX)),
                  pl.BlockSpec((None, L, LANES), grp(C_SSDB)),
                  pl.BlockSpec((None, L, LANES), grp(C_SSDC)),
                  pl.BlockSpec((None, L, W), col(P_SSDZ)),
                  pl.BlockSpec((None, L, LANES), lambda b, g: (b, 0, P_SMALL // LANES)),
                  pl.BlockSpec((1, LANES), lambda b, g: (0, 0)),
                  pl.BlockSpec((1, LANES), lambda b, g: (0, 0)),
                  pl.BlockSpec((None, 1, W), lambda b, g: (g, 0, 0)),
                  st_spec],
        out_specs=[pl.BlockSpec((None, L, W), lambda b, g: (b, 0, g)), st_spec],
        out_shape=[jax.ShapeDtypeStruct((B, L, D_BRANCH), F32),
                   jax.ShapeDtypeStruct(s0p.shape, F32)],
        scratch_shapes=[pltpu.VMEM((L, W), F32), pltpu.VMEM((L, W), F32),
                        pltpu.VMEM((2, ppg, LANES, SSD_N), F32)],
        compiler_params=_cparams(("arbitrary", "arbitrary")),
        name="ssd_scan",
    )(proj3, proj3, proj3, proj3, proj3, bias_row, alog_row, dskip, s0p)
    return y, sf.reshape(s0.shape)


def _l2norm(a):
    return a * lax.rsqrt(jnp.sum(a * a, axis=-1, keepdims=True) + EPS)


def _gdn_kernel(q_ref, k_ref, v_ref, g_ref, sm_ref, bias_ref, alog_ref, nrm_ref, s0_ref,
                y_ref, sf_ref, of_scr, ob_scr, st_scr, *, L):
    C = GDN_CHUNK
    W = GDN_DK
    nc = L // C
    nh = q_ref.shape[1] // W
    hblk = pl.program_id(1)
    o_scrs = (of_scr, ob_scr)
    incls = (_tri_mask(C, False), _tri_mask(C, True))
    stricts = (_tri_mask(C, False, strict=True), _tri_mask(C, True, strict=True))
    for d in (0, 1):
        for hh in range(nh):
            st_scr[d, hh] = s0_ref[d, hh]

    units = [(d, hh) for d in (0, 1) for hh in range(nh)]
    eye = (_iota((C, C), 0) == _iota((C, C), 1)).astype(F32)

    def body(ci, carry):
        rows_d, cum_d, beta_d = [], [], []
        for d in (0, 1):
            c = (nc - 1 - ci) if d == 1 else ci
            rows = pl.ds(pl.multiple_of(c * C, C), C)
            raw = sm_ref[rows, :]
            rows_d.append(rows)
            cum_d.append(_cumsum_rows(-jnp.exp(alog_ref[...]) * _softplus(raw + bias_ref[...]), d == 1))
            beta_d.append(_sigmoid(raw))
        q, k, v, gc, grow, beta = {}, {}, {}, {}, {}, {}
        for u in units:
            d, hh = u
            cols = slice(hh * W, (hh + 1) * W)
            h = nh * hblk + hh
            q[u] = _l2norm(_silu(q_ref[rows_d[d], cols])) * (GDN_DK ** -0.5)
            k[u] = _l2norm(_silu(k_ref[rows_d[d], cols]))
            v[u] = _silu(v_ref[rows_d[d], cols])
            gc[u] = _lane_pick(cum_d[d], S_GA + d * GDN_HEADS + h)
            beta[u] = _lane_pick(beta_d[d], S_GB + d * GDN_HEADS + h)
        for u in units:
            grow[u] = _row_pick(cum_d[u[0]], S_GA + u[0] * GDN_HEADS + nh * hblk + u[1])
        decay = {u: _masked_exp(incls[u[0]], gc[u] - grow[u]) for u in units}
        kb = {u: k[u] * beta[u] for u in units}
        m = {u: -jnp.where(stricts[u[0]], _bdot_nt(kb[u], k[u]) * decay[u], 0.0) for u in units}
        aqk = {u: _bdot_nt(q[u], k[u]) * decay[u] for u in units}
        p = {u: eye + m[u] for u in units}
        sh = 2
        while sh < C:
            m = {u: _dot3(m[u], m[u]) for u in units}
            p = {u: p[u] + _dot3(p[u], m[u]) for u in units}
            sh *= 2
        uu = {u: _dot3(p[u], v[u] * beta[u]) for u in units}
        ww = {u: _dot3(p[u], kb[u] * jnp.exp(gc[u])) for u in units}
        st = {u: st_scr[u[0], u[1]] for u in units}
        v_new = {u: uu[u] - _bdot(ww[u], st[u]) for u in units}
        o = {u: _bdot(q[u] * jnp.exp(gc[u]), st[u]) + _bdot(aqk[u], v_new[u]) for u in units}
        for u in units:
            d, hh = u
            g_end = gc[u][0:1, :] if d == 1 else gc[u][C - 1:C, :]
            st_scr[d, hh] = st[u] * jnp.exp(g_end) + _bdot_tn(k[u] * jnp.exp(g_end - gc[u]), v_new[u])
            o_scrs[d][rows_d[d], slice(hh * W, (hh + 1) * W)] = o[u]
        return carry

    lax.fori_loop(0, nc, body, 0)
    for d in (0, 1):
        for hh in range(nh):
            sf_ref[d, hh] = st_scr[d, hh]

    tr = min(L, 256)

    def fin(i, carry):
        rows = pl.ds(pl.multiple_of(i * tr, tr), tr)
        for hh in range(nh):
            cols = slice(hh * W, (hh + 1) * W)
            o = of_scr[rows, cols] + ob_scr[rows, cols]
            on = o * lax.rsqrt(jnp.mean(o * o, axis=-1, keepdims=True) + EPS) * nrm_ref[...]
            y_ref[rows, cols] = on * _silu(g_ref[rows, cols])
        return carry

    lax.fori_loop(0, L // tr, fin, 0)


def _gdn_scan(proj3, bias_row, alog_row, gdn_norm, s0):
    B, L = proj3.shape[0], proj3.shape[1]
    nh = GDN_HEADS
    W = nh * GDN_DK
    col = lambda base: (lambda b, h: (b, 0, base // W + h))
    st_spec = pl.BlockSpec((None, 2, nh, GDN_DK, GDN_DK), lambda b, h: (b, 0, h, 0, 0))
    once = pl.Buffered(1)
    return pl.pallas_call(
        functools.partial(_gdn_kernel, L=L),
        grid=(B, GDN_HEADS // nh),
        in_specs=[pl.BlockSpec((None, L, W), col(C_GQ), pipeline_mode=once),
                  pl.BlockSpec((None, L, W), col(C_GK), pipeline_mode=once),
                  pl.BlockSpec((None, L, W), col(C_GV), pipeline_mode=once),
                  pl.BlockSpec((None, L, W), col(P_GDNG), pipeline_mode=once),
                  pl.BlockSpec((None, L, LANES), lambda b, h: (b, 0, P_SMALL // LANES), pipeline_mode=once),
                  pl.BlockSpec((1, LANES), lambda b, h: (0, 0)),
                  pl.BlockSpec((1, LANES), lambda b, h: (0, 0)),
                  pl.BlockSpec((1, GDN_DK), lambda b, h: (0, 0)),
                  st_spec],
        out_specs=[pl.BlockSpec((None, L, W), lambda b, h: (b, 0, h)), st_spec],
        out_shape=[jax.ShapeDtypeStruct((B, L, D_BRANCH), F32),
                   jax.ShapeDtypeStruct(s0.shape, F32)],
        scratch_shapes=[pltpu.VMEM((L, W), F32), pltpu.VMEM((L, W), F32),
                        pltpu.VMEM((2, nh, GDN_DK, GDN_DK), F32)],
        compiler_params=_cparams(("arbitrary", "arbitrary")),
        name="gdn_scan",
    )(proj3, proj3, proj3, proj3, proj3, bias_row, alog_row, gdn_norm.reshape(1, GDN_DK), s0)


def _merge_kernel(ya_ref, yb_ref, yc_ref, yd_ref, mg_ref, x_ref, mod_ref, sn_ref, n2_ref, wb_ref, wo_ref,
                  xo_ref, h_ref):
    yc = yc_ref[...]
    half = D_BRANCH // 2
    parts = []
    for gidx in range(2):
        seg = yc[:, gidx * half:(gidx + 1) * half]
        parts.append(seg * lax.rsqrt(jnp.mean(seg * seg, axis=-1, keepdims=True) + EPS)
                     * sn_ref[:, gidx * half:(gidx + 1) * half])
    branches = (ya_ref[...], yb_ref[...], None, yd_ref[...])
    mixed = None
    for n in range(N_BRANCH):
        gate = _sigmoid(mg_ref[:, n * D_MODEL:(n + 1) * D_MODEL])
        if n == 2:
            lifted = (_dot(parts[0].astype(BF16), wb_ref[n, 0:half, :])
                      + _dot(parts[1].astype(BF16), wb_ref[n, half:D_BRANCH, :]))
        else:
            lifted = _dot(branches[n].astype(BF16), wb_ref[n])
        mixed = gate * lifted if mixed is None else mixed + gate * lifted
    m = mod_ref[...]
    x = x_ref[...] + m[2:3] * _dot(mixed.astype(BF16), wo_ref[...])
    xo_ref[...] = x
    xn = x * lax.rsqrt(jnp.mean(x * x, axis=-1, keepdims=True) + EPS) * n2_ref[...]
    h_ref[...] = xn * (1.0 + m[4:5]) + m[3:4]


def _merge(ya, yb, yc, yd, proj, x, mod, ssd_norm, norm2_g, wb_bf16, wo_bf16, *, tokens_per_mod):
    T = x.shape[0]
    tm = 256
    tiles_per_mod = tokens_per_mod // tm
    yspec = pl.BlockSpec((tm, D_BRANCH), lambda i: (i, 0))
    xspec = pl.BlockSpec((tm, D_MODEL), lambda i: (i, 0))
    return pl.pallas_call(
        _merge_kernel,
        grid=(T // tm,),
        in_specs=[yspec, yspec, yspec, yspec,
                  pl.BlockSpec((tm, N_BRANCH * D_MODEL), lambda i: (i, P_MERGE // (N_BRANCH * D_MODEL))),
                  xspec,
                  pl.BlockSpec((None, 6, D_MODEL), lambda i: (i // tiles_per_mod, 0, 0)),
                  pl.BlockSpec((1, D_BRANCH), lambda i: (0, 0)),
                  pl.BlockSpec((1, D_MODEL), lambda i: (0, 0)),
                  pl.BlockSpec((N_BRANCH, D_BRANCH, D_MODEL), lambda i: (0, 0, 0)),
                  pl.BlockSpec((D_MODEL, D_MODEL), lambda i: (0, 0))],
        out_specs=[xspec, xspec],
        out_shape=[jax.ShapeDtypeStruct((T, D_MODEL), F32)] * 2,
        compiler_params=_cparams(("arbitrary",)),
        name="merge_outproj",
    )(ya, yb, yc, yd, proj, x, mod, ssd_norm.reshape(1, D_BRANCH), norm2_g.reshape(1, D_MODEL), wb_bf16, wo_bf16)


def _router_kernel(h_ref, rw_ref, rb_ref, wt_ref, lpos_ref, tlen_ref, tstart_ref, tcar_ref, carry):
    i = pl.program_id(0)
    tm = h_ref.shape[0]
    E = N_EXPERTS
    gsz = E // N_EXPERT_GROUPS
    neg_inf = -jnp.inf

    @pl.when(i == 0)
    def _():
        carry[...] = jnp.zeros_like(carry)

    scores = _sigmoid(_dot_nt(rw_ref[...], h_ref[...], precision=HIGHEST))
    biased = scores + rb_ref[...]
    eidx = _iota((E, tm), 0)
    ridx = _iota((gsz, tm), 0)

    slabs = [biased[g * gsz:(g + 1) * gsz, :] for g in range(N_EXPERT_GROUPS)]
    gs = []
    for v in slabs:
        m1 = jnp.max(v, axis=0, keepdims=True)
        i1 = jnp.min(jnp.where(v == m1, ridx, gsz), axis=0, keepdims=True)
        m2 = jnp.max(jnp.where(ridx == i1, neg_inf, v), axis=0, keepdims=True)
        gs.append(m1 + m2)
    masked = []
    for g in range(N_EXPERT_GROUPS):
        rank = jnp.zeros((1, tm), I32)
        for g2 in range(N_EXPERT_GROUPS):
            if g2 == g:
                continue
            ahead = (gs[g2] >= gs[g]) if g2 < g else (gs[g2] > gs[g])
            rank = rank + ahead.astype(I32)
        masked.append(jnp.where(rank < TOPK_GROUPS, slabs[g], MASK_NEG))
    cur = jnp.concatenate(masked, axis=0)

    krow = _iota((TOP_K, tm), 0)
    onehot = jnp.zeros((E, tm), F32)
    picks = []
    wsum = jnp.zeros((1, tm), F32)
    for kk in range(TOP_K):
        m = jnp.max(cur, axis=0, keepdims=True)
        ik = jnp.min(jnp.where(cur == m, eidx, E), axis=0, keepdims=True)
        hit = eidx == ik
        wk = jnp.sum(jnp.where(hit, scores, 0.0), axis=0, keepdims=True)
        cur = jnp.where(hit, neg_inf, cur)
        onehot = onehot + hit.astype(F32)
        picks.append((ik, wk))
        wsum = wsum + wk
    wt_out = jnp.zeros((TOP_K, tm), F32)
    for kk, (ik, wk) in enumerate(picks):
        wt_out = jnp.where(krow == kk, wk / wsum * ROUTED_SCALE, wt_out)

    earlier = (_iota((tm, tm), 0) < _iota((tm, tm), 1)).astype(BF16)
    before = _dot(onehot.astype(BF16), earlier)
    pad = lambda c: (c.astype(I32) + (RUN_ROWS - 1)) // RUN_ROWS * RUN_ROWS
    run_len = pad(jnp.sum(onehot, axis=1, keepdims=True) + jnp.zeros((E, LANES), F32)).astype(F32)
    run_start = _cumsum_rows(run_len, False) - run_len
    local = before + run_start[:, 0:1]
    lpos_out = jnp.zeros((TOP_K, tm), I32)
    for kk, (ik, wk) in enumerate(picks):
        lp = jnp.sum(jnp.where(eidx == ik, local, 0.0), axis=0, keepdims=True)
        lpos_out = jnp.where(krow == kk, lp.astype(I32), lpos_out)
    wt_ref[...] = wt_out
    lpos_ref[...] = lpos_out

    onehot_l = jnp.concatenate([onehot, jnp.zeros((LANES - E, tm), F32)], axis=0).astype(BF16)
    len_row = pad(_dot_nt(jnp.ones((8, tm), BF16), onehot_l)).astype(F32)
    lanes_before = (_iota((LANES, LANES), 0) < _iota((LANES, LANES), 1)).astype(BF16)
    tlen_ref[...] = len_row.astype(I32)
    tstart_ref[...] = _dot(len_row.astype(BF16), lanes_before).astype(I32)
    tcar_ref[...] = carry[...].astype(I32)
    carry[...] = carry[...] + len_row


def _router(h2, router_w, router_b, *, tm):
    T = h2.shape[0]
    nt = T // tm
    ospec = pl.BlockSpec((TOP_K, tm), lambda i: (0, i))
    tspec = pl.BlockSpec((None, 8, LANES), lambda i: (i, 0, 0))
    tshape = jax.ShapeDtypeStruct((nt, 8, LANES), I32)
    wt, lpos, tlen, tstart, tcar = pl.pallas_call(
        _router_kernel,
        grid=(nt,),
        in_specs=[pl.BlockSpec((tm, D_MODEL), lambda i: (i, 0)),
                  pl.BlockSpec((N_EXPERTS, D_MODEL), lambda i: (0, 0)),
                  pl.BlockSpec((N_EXPERTS, 1), lambda i: (0, 0))],
        out_specs=[ospec, ospec, tspec, tspec, tspec],
        out_shape=[jax.ShapeDtypeStruct((TOP_K, T), F32), jax.ShapeDtypeStruct((TOP_K, T), I32),
                   tshape, tshape, tshape],
        scratch_shapes=[pltpu.VMEM((8, LANES), F32)],
        compiler_params=_cparams(("arbitrary",)),
        name="moe_router",
    )(h2, router_w.T, router_b.reshape(N_EXPERTS, 1))
    table = lambda a: a[:, 0, :N_EXPERTS]
    return wt, lpos, table(tlen), table(tstart), table(tcar)


def _dispatch_kernel(len_ref, lst_ref, car_ref, seg_ref, lpos_ref, h_ref, xs_ref, xloc, sem):
    i = pl.program_id(0)
    tm = h_ref.shape[0]
    nloc = xloc.shape[0]
    R = RUN_ROWS
    lpos = lpos_ref[...]
    hb = h_ref[...].astype(BF16)
    rb = 256
    for blk in range(nloc // rb):
        r = _iota((rb, tm), 0) + blk * rb
        hit = r == lpos[0:1, :]
        for kk in range(1, TOP_K):
            hit = hit | (r == lpos[kk:kk + 1, :])
        xloc[blk * rb:(blk + 1) * rb, :] = _dot(hit.astype(BF16), hb)

    def piece(src_row, dst_row):
        return pltpu.make_async_copy(xloc.at[pl.ds(pl.multiple_of(src_row, R), R), :],
                                     xs_ref.at[pl.ds(pl.multiple_of(dst_row, R), R), :], sem)

    def per_expert(e, issued):
        pieces = len_ref[i, e] // R
        ls = lst_ref[i, e]
        gd = seg_ref[e] + car_ref[i, e]

        def issue(j, c):
            piece(ls + j * R, gd + j * R).start()
            return c

        lax.fori_loop(0, pieces, issue, 0)
        return issued + pieces

    total = lax.fori_loop(0, N_EXPERTS, per_expert, 0)

    def wait(j, c):
        piece(0, 0).wait()
        return c

    lax.fori_loop(0, total, wait, 0)


def _dispatch(h2, lpos, tile_len, tile_lstart, tile_car, seg_start, *, tm):
    T = h2.shape[0]
    nt = T // tm
    n_rows = T * TOP_K + N_EXPERTS * nt * RUN_ROWS
    nloc = tm * TOP_K + N_EXPERTS * RUN_ROWS
    return pl.pallas_call(
        _dispatch_kernel,
        grid_spec=pltpu.PrefetchScalarGridSpec(
            num_scalar_prefetch=4,
            grid=(nt,),
            in_specs=[pl.BlockSpec((TOP_K, tm), lambda i, *_: (0, i)),
                      pl.BlockSpec((tm, D_MODEL), lambda i, *_: (i, 0))],
            out_specs=pl.BlockSpec(memory_space=pl.ANY),
            scratch_shapes=[pltpu.VMEM((nloc, D_MODEL), F32), pltpu.SemaphoreType.DMA(())]),
        out_shape=jax.ShapeDtypeStruct((n_rows, D_MODEL), F32),
        compiler_params=_cparams(("arbitrary",)),
        name="moe_dispatch",
    )(tile_len, tile_lstart, tile_car, seg_start, lpos, h2)


def _expert_kernel(blk_ref, exp_ref, lo_ref, hi_ref, first_ref, x_ref, w1_ref, w3_ref, w2_ref, o_ref):
    w = pl.program_id(0)
    x = x_ref[...].astype(BF16)
    a = _dot(x, w1_ref[...].astype(BF16))
    b = _dot(x, w3_ref[...].astype(BF16))
    y = _dot((_silu(a) * b).astype(BF16), w2_ref[...].astype(BF16))
    r = _iota(y.shape, 0)
    y = jnp.where((r >= lo_ref[w]) & (r < hi_ref[w]), y, 0.0)

    @pl.when(first_ref[w] == 1)
    def _():
        o_ref[...] = y

    @pl.when(first_ref[w] == 0)
    def _():
        o_ref[...] = o_ref[...] + y


def _experts(xs, work, w1, w3, w2, *, layer, bm):
    A = xs.shape[0]
    n_work = work[0].shape[0]
    xmap = lambda w, blk, ex, lo, hi, first: (blk[w], 0)
    wmap = lambda w, blk, ex, lo, hi, first: (layer, ex[w], 0, 0)
    return pl.pallas_call(
        _expert_kernel,
        grid_spec=pltpu.PrefetchScalarGridSpec(
            num_scalar_prefetch=5,
            grid=(n_work,),
            in_specs=[pl.BlockSpec((bm, D_MODEL), xmap),
                      pl.BlockSpec((None, None, D_MODEL, D_EXPERT), wmap),
                      pl.BlockSpec((None, None, D_MODEL, D_EXPERT), wmap),
                      pl.BlockSpec((None, None, D_EXPERT, D_MODEL), wmap)],
            out_specs=pl.BlockSpec((bm, D_MODEL), xmap)),
        out_shape=jax.ShapeDtypeStruct((A, D_MODEL), F32),
        compiler_params=_cparams(("arbitrary",)),
        name="moe_experts",
    )(*work, xs, w1, w3, w2)


def _combine_kernel(len_ref, lst_ref, car_ref, seg_ref, lpos_ref, lpost_ref, wt_ref, ys_ref, h_ref, x_ref, mod_ref,
                    s1_ref, s3_ref, s2_ref, fg_ref, o_ref, yloc, sem, *, final_norm):
    i = pl.program_id(0)
    tm = h_ref.shape[0]
    nloc = yloc.shape[0]
    R = RUN_ROWS

    @pl.when(i == 0)
    def _():
        yloc[...] = jnp.zeros_like(yloc)

    def piece(src_row, dst_row):
        return pltpu.make_async_copy(ys_ref.at[pl.ds(pl.multiple_of(src_row, R), R), :],
                                     yloc.at[pl.ds(pl.multiple_of(dst_row, R), R), :], sem)

    def per_expert(e, issued):
        pieces = len_ref[i, e] // R
        ls = lst_ref[i, e]
        gd = seg_ref[e] + car_ref[i, e]

        def issue(j, c):
            piece(gd + j * R, ls + j * R).start()
            return c

        lax.fori_loop(0, pieces, issue, 0)
        return issued + pieces

    total = lax.fori_loop(0, N_EXPERTS, per_expert, 0)

    hb = h_ref[...].astype(BF16)
    mid = _silu(_dot(hb, s1_ref[...])) * _dot(hb, s3_ref[...])
    y = _dot(mid.astype(BF16), s2_ref[...])

    def wait(j, c):
        piece(0, 0).wait()
        return c

    lax.fori_loop(0, total, wait, 0)

    lpos = lpos_ref[...]
    wt = wt_ref[...]
    rb = 256
    for blk in range(nloc // rb):
        r = _iota((rb, tm), 0) + blk * rb
        pw = jnp.zeros((rb, tm), F32)
        for kk in range(TOP_K):
            pw = pw + jnp.where(r == lpos[kk:kk + 1, :], wt[kk:kk + 1, :], 0.0)
        wrow = jnp.sum(pw, axis=1, keepdims=True)
        ysc = yloc[blk * rb:(blk + 1) * rb, :] * wrow
        hi = ysc.astype(BF16)
        lo = (ysc - hi.astype(F32)).astype(BF16)
        c = _iota((tm, rb), 1) + blk * rb
        hit = c == lpost_ref[:, 0:1]
        for kk in range(1, TOP_K):
            hit = hit | (c == lpost_ref[:, kk:kk + 1])
        pt = hit.astype(BF16)
        y = y + (_dot(pt, hi) + _dot(pt, lo))
    x = x_ref[...] + mod_ref[5:6, :] * y
    if final_norm:
        x = x * lax.rsqrt(jnp.mean(x * x, axis=-1, keepdims=True) + EPS) * fg_ref[...]
    o_ref[...] = x


def _combine(ys, lpos, lpos_t, wt, tile_len, tile_lstart, tile_car, seg_start, h2, x, mod, s1, s3, s2, final_g,
             *, tm, tokens_per_mod, final_norm):
    T = x.shape[0]
    tiles_per_mod = tokens_per_mod // tm
    nloc = tm * TOP_K + N_EXPERTS * RUN_ROWS
    xspec = pl.BlockSpec((tm, D_MODEL), lambda i, *_: (i, 0))
    kspec = pl.BlockSpec((TOP_K, tm), lambda i, *_: (0, i))
    const = lambda i, *_: (0, 0)
    return pl.pallas_call(
        functools.partial(_combine_kernel, final_norm=final_norm),
        grid_spec=pltpu.PrefetchScalarGridSpec(
            num_scalar_prefetch=4,
            grid=(T // tm,),
            in_specs=[kspec,
                      pl.BlockSpec((tm, LANES), lambda i, *_: (i, 0)),
                      kspec,
                      pl.BlockSpec(memory_space=pl.ANY),
                      xspec, xspec,
                      pl.BlockSpec((None, 6, D_MODEL), lambda i, *_: (i // tiles_per_mod, 0, 0)),
                      pl.BlockSpec((D_MODEL, D_SHARED), const),
                      pl.BlockSpec((D_MODEL, D_SHARED), const),
                      pl.BlockSpec((D_SHARED, D_MODEL), const),
                      pl.BlockSpec((1, D_MODEL), const)],
            out_specs=xspec,
            scratch_shapes=[pltpu.VMEM((nloc, D_MODEL), F32), pltpu.SemaphoreType.DMA(())]),
        out_shape=jax.ShapeDtypeStruct((T, D_MODEL), F32),
        compiler_params=_cparams(("arbitrary",)),
        name="moe_combine",
    )(tile_len, tile_lstart, tile_car, seg_start, lpos, lpos_t, wt, ys, h2, x, mod, s1, s3, s2, final_g.reshape(1, D_MODEL))


def _work_list(counts, starts, n_rows, bm):
    E = N_EXPERTS
    n_blocks = n_rows // bm
    n_work = n_blocks + E - 1
    ends = starts + counts
    first_blk = starts // bm
    last_blk = jnp.where(counts > 0, (ends - 1) // bm, first_blk)
    n_items = jnp.where(counts > 0, last_blk - first_blk + 1, 0)
    item_end = jnp.cumsum(n_items)
    item_start = item_end - n_items
    w = jnp.arange(n_work, dtype=I32)
    ex = jnp.minimum(jnp.sum((item_end[None, :] <= w[:, None]).astype(I32), axis=1), E - 1)
    valid = w < item_end[-1]
    blk = first_blk[ex] + (w - item_start[ex])
    blk = jnp.where(valid, blk, blk[jnp.maximum(item_end[-1] - 1, 0)]).astype(I32)
    lo = jnp.clip(starts[ex] - blk * bm, 0, bm)
    hi = jnp.clip(ends[ex] - blk * bm, 0, bm)
    lo = jnp.where(valid, lo, 0).astype(I32)
    hi = jnp.where(valid, hi, 0).astype(I32)
    ex = jnp.where(valid, ex, ex[jnp.maximum(item_end[-1] - 1, 0)])
    prev_blk = jnp.concatenate([jnp.full((1,), -1, I32), blk[:-1]])
    first = (blk != prev_blk).astype(I32)
    return blk, ex.astype(I32), lo, hi, first


def _moe(h2, x, mod, p, final_g, *, layer, tokens_per_mod, final_norm):
    T = x.shape[0]
    tm = 256
    wt, lpos, tile_len, tile_lstart, tile_car = _router(h2, p["router_w"], p["router_b"], tm=tm)
    seg_len = tile_car[-1] + tile_len[-1]
    seg_start = jnp.cumsum(seg_len) - seg_len
    xs = _dispatch(h2, lpos, tile_len, tile_lstart, tile_car, seg_start, tm=tm)
    bm = 512
    work = _work_list(seg_len, seg_start, xs.shape[0], bm)
    ys = _experts(xs, work, p["exp_w1"], p["exp_w3"], p["exp_w2"], layer=layer, bm=bm)
    lpos_t = jnp.pad(lpos.T, ((0, 0), (0, LANES - TOP_K)))
    return _combine(ys, lpos, lpos_t, wt, tile_len, tile_lstart, tile_car, seg_start, h2, x, mod,
                    p["sh_w1"], p["sh_w3"], p["sh_w2"], final_g,
                    tm=tm, tokens_per_mod=tokens_per_mod, final_norm=final_norm)


def _reorder_w_in(w_in):
    plain = D_CONV
    pieces = [w_in[..., :plain],
              w_in[..., plain + 3616:plain + 3616 + 4096],
              w_in[..., plain:plain + 3072],
              w_in[..., plain + 3088:plain + 3600],
              w_in[..., plain + 3072:plain + 3088],
              w_in[..., plain + 3600:plain + 3616]]
    out = jnp.concatenate(pieces, axis=-1)
    pad = D_PROJ - out.shape[-1]
    return jnp.pad(out, ((0, 0), (0, 0), (0, pad))).astype(BF16)


def _small_rows(ssd_vals, gdn_vals):
    row = jnp.zeros((LANES,), F32)
    row = row.at[S_DT:S_DT + 2 * SSD_HEADS].set(ssd_vals.reshape(-1))
    row = row.at[S_GA:S_GA + 2 * GDN_HEADS].set(gdn_vals.reshape(-1))
    return row.reshape(1, LANES)


def _layer_pass(x, mod, lp, hy, st_hg, st_ssd, st_gdn, final_g, *, B, L, tokens_per_mod, seg, layer, final_norm):
    T = B * L
    proj = _in_projection(x, mod, lp["norm1_g"], lp["w_in"], lp["conv_w"], lp["conv_b"],
                          layer=layer, tokens_per_mod=tokens_per_mod, seg=seg)
    proj3 = proj.reshape(B, L, D_PROJ)
    fmat, gmat, hspec = hy
    cb = lambda c: c // D_BRANCH
    z = _spectral_conv(fmat, gmat, hspec, 0, lp["hy_bias"][0], proj3, cb(C_HYV), proj3, cb(C_HYX1))
    ya = _spectral_conv(fmat, gmat, hspec, 1, lp["hy_bias"][1], z, 0, proj3, cb(C_HYX2))
    yb, s_hg = _hgrn_scan(proj3, lp["hg_lb"], lp["hg_norm"], st_hg, layer=layer)
    yc, s_ssd = _ssd_scan(proj3, lp["bias_row"], lp["alog_row"], lp["ssd_d"], st_ssd)
    yd, s_gdn = _gdn_scan(proj3, lp["bias_row"], lp["alog_row"], lp["gdn_norm"], st_gdn)
    flat = lambda a: a.reshape(T, D_BRANCH)
    x, h2 = _merge(flat(ya), flat(yb), flat(yc), flat(yd), proj, x, mod, lp["ssd_norm"], lp["norm2_g"],
                   lp["w_branch"], lp["w_out"], tokens_per_mod=tokens_per_mod)
    x = _moe(h2, x, mod, lp, final_g, layer=layer, tokens_per_mod=tokens_per_mod, final_norm=final_norm)
    return x, s_hg, s_ssd, s_gdn


def kernel(x_prompt, x_sample, state_hgrn, state_ssd, state_gdn, c, c_ctx, norm1_g, norm2_g, ada_w, ada_b, w_in, conv_w, conv_b, hy_w1, hy_b1, hy_w2, hy_b2, hy_w3, hy_bias, hg_lb, hg_norm, ssd_a_log, ssd_dt_bias, ssd_d, ssd_norm, gdn_a_log, gdn_dt_bias, gdn_norm, w_branch, w_out, router_w, router_bias, exp_w1, exp_w3, exp_w2, sh_w1, sh_w3, sh_w2, final_g):
    depth = w_in.shape[0]
    bp, lp_len = x_prompt.shape[0], x_prompt.shape[1]
    bs, ls_len = x_sample.shape[0], x_sample.shape[1]
    D = D_MODEL

    w_in_r = _reorder_w_in(w_in)
    hy_w1p = jnp.pad(hy_w1, ((0, 0), (0, LANES - hy_w1.shape[1]), (0, 0)))
    layers = []
    for l in range(depth):
        layers.append(dict(
            norm1_g=norm1_g[l], norm2_g=norm2_g[l], w_in=w_in_r, conv_w=conv_w[l], conv_b=conv_b[l],
            hy_bias=hy_bias[l], hg_lb=hg_lb, hg_norm=hg_norm[l],
            bias_row=_small_rows(ssd_dt_bias[l], gdn_dt_bias[l]),
            alog_row=_small_rows(ssd_a_log[l], gdn_a_log[l]),
            ssd_d=ssd_d[l], ssd_norm=ssd_norm[l], gdn_norm=gdn_norm[l],
            w_branch=w_branch[l].astype(BF16), w_out=w_out[l].astype(BF16),
            router_w=router_w[l], router_b=router_bias[l],
            exp_w1=exp_w1, exp_w3=exp_w3, exp_w2=exp_w2,
            sh_w1=sh_w1[l].astype(BF16), sh_w3=sh_w3[l].astype(BF16), sh_w2=sh_w2[l].astype(BF16)))

    def hyena_setup(L):
        fmat, fs = _dft_matrices(L)
        gmat = fs.T
        specs = []
        for l in range(depth):
            filt = _hyena_filters(L, hy_w1p[l], hy_b1[l], hy_w2[l], hy_b2[l], hy_w3[l])
            specs.append(_filter_spectrum(fmat, filt))
        return fmat, gmat, specs

    cond = jnp.concatenate([c_ctx.reshape(1, D), c], axis=0)
    rows = cond.shape[0]
    rows8 = (rows + 7) // 8 * 8
    cond8 = jnp.pad(cond, ((0, rows8 - rows), (0, 0)))
    mods = [_modulation(cond8, ada_w, ada_b[l], layer=l).reshape(rows8, 6, D) for l in range(depth)]

    fmat, gmat, specs = hyena_setup(lp_len)
    x = x_prompt.reshape(bp * lp_len, D)
    z_hg = jnp.zeros((bp, 2, HG_HEADS, HG_DK, HG_DK), F32)
    z_ssd = jnp.zeros((bp, 2, SSD_HEADS, SSD_P, SSD_N), F32)
    z_gdn = jnp.zeros((bp, 2, GDN_HEADS, GDN_DK, GDN_DK), F32)
    hg_states, ssd_states, gdn_states = [], [], []
    for l in range(depth):
        x, s_hg, s_ssd, s_gdn = _layer_pass(
            x, mods[l][0:1], layers[l], (fmat, gmat, specs[l]), z_hg, z_ssd, z_gdn, final_g,
            B=bp, L=lp_len, tokens_per_mod=bp * lp_len, seg=lp_len, layer=l, final_norm=(l == depth - 1))
        hg_states.append(s_hg)
        ssd_states.append(s_ssd)
        gdn_states.append(s_gdn)
    y_prompt = x.reshape(bp, lp_len, D)
    new_hg = jnp.stack(hg_states, axis=1)
    new_ssd = jnp.stack(ssd_states, axis=1)
    new_gdn = jnp.stack(gdn_states, axis=1)

    fmat, gmat, specs = hyena_setup(ls_len)
    x = x_sample.reshape(bs * ls_len, D)
    for l in range(depth):
        x, _, _, _ = _layer_pass(
            x, mods[l][1:1 + bs], layers[l], (fmat, gmat, specs[l]),
            state_hgrn[:, l], state_ssd[:, l], state_gdn[:, l], final_g,
            B=bs, L=ls_len, tokens_per_mod=ls_len, seg=GRID_W, layer=l, final_norm=(l == depth - 1))
    y_sample = x.reshape(bs, ls_len, D)
    return (y_prompt, y_sample, new_hg, new_ssd, new_gdn)
```

```python
import functools
import math

import numpy as np
import jax
import jax.numpy as jnp
from jax import lax
from jax.experimental import pallas as pl
from jax.experimental.pallas import tpu as pltpu

F32 = jnp.float32
BF16 = jnp.bfloat16
I32 = jnp.int32
HIGHEST = lax.Precision.HIGHEST

D_MODEL = 1024
GRID_W = 64
EPS = 1e-6
LOG_FLOOR = 1e-30
MASK_NEG = -1e30
N_BRANCH = 4
D_BRANCH = 512
HY_POS_FREQS = 16
HY_FILTER_HIDDEN = 64
HY_FAST_DECAY = 0.3
HY_SLOW_DECAY = 1.5
HY_DECAY_TARGET = 1e-2
HG_HEADS = 4
HG_DK = 128
HG_CHUNK = 16
SSD_HEADS = 8
SSD_P = 64
SSD_N = 128
SSD_CHUNK = 64
GDN_HEADS = 4
GDN_DK = 128
GDN_CHUNK = 64
N_EXPERTS = 64
TOP_K = 8
N_EXPERT_GROUPS = 8
TOPK_GROUPS = 4
D_EXPERT = 256
D_SHARED = 256
ROUTED_SCALE = 2.5

LANES = 128
RUN_ROWS = 8

D_CONV = 4096
C_HYV, C_HYX1, C_HYX2 = 0, 512, 1024
C_SSDX, C_SSDB, C_SSDC = 1536, 2048, 2304
C_GQ, C_GK, C_GV = 2560, 3072, 3584
P_MERGE = 4096
P_HGQ, P_HGFF, P_HGFB, P_HGV, P_HGG = 8192, 8704, 9216, 9728, 10240
P_SSDZ, P_GDNG, P_SMALL = 10752, 11264, 11776
D_PROJ = 12288
S_DT, S_GA, S_GB = 0, 16, 24

VMEM_LIMIT = 56 * 1024 * 1024


def _cparams(sem):
    return pltpu.CompilerParams(dimension_semantics=sem, vmem_limit_bytes=VMEM_LIMIT)


def _sigmoid(x):
    return 1.0 / (1.0 + jnp.exp(-x))


def _silu(x):
    return x * _sigmoid(x)


def _softplus(x):
    return jnp.maximum(x, 0.0) + jnp.log(1.0 + jnp.exp(-jnp.abs(x)))


def _dot(a, b, precision=None):
    return jnp.dot(a, b, preferred_element_type=F32, precision=precision)


def _dot_nt(a, b, precision=None):
    return lax.dot_general(a, b, (((1,), (1,)), ((), ())), preferred_element_type=F32, precision=precision)


def _dot_tn(a, b, precision=None):
    return lax.dot_general(a, b, (((0,), (0,)), ((), ())), preferred_element_type=F32, precision=precision)


def _bdot(a, b):
    return _dot(a.astype(BF16), b.astype(BF16))


def _bdot_nt(a, b):
    return _dot_nt(a.astype(BF16), b.astype(BF16))


def _bdot_tn(a, b):
    return _dot_tn(a.astype(BF16), b.astype(BF16))


def _iota(shape, dim):
    return lax.broadcasted_iota(I32, shape, dim)


def _cumsum_rows(g, reverse):
    n = g.shape[0]
    row = _iota(g.shape, 0)
    sh = 1
    while sh < n:
        if reverse:
            g = g + jnp.where(row < n - sh, pltpu.roll(g, n - sh, 0), 0.0)
        else:
            g = g + jnp.where(row >= sh, pltpu.roll(g, sh, 0), 0.0)
        sh *= 2
    return g


def _lane_pick(a, j):
    return jnp.sum(jnp.where(_iota(a.shape, 1) == j, a, 0.0), axis=1, keepdims=True)


def _split3(a):
    a1 = a.astype(BF16)
    r1 = a - a1.astype(F32)
    a2 = r1.astype(BF16)
    a3 = (r1 - a2.astype(F32)).astype(BF16)
    return a1, a2, a3


def _dot3(a, b):
    a1, a2, _ = _split3(a)
    b1, b2, _ = _split3(b)
    return _dot(a1, b1) + (_dot(a1, b2) + _dot(a2, b1))


def _row_pick(a, j):
    sel = (_iota((8, a.shape[1]), 1) == j).astype(BF16)
    a1, a2, a3 = _split3(a)
    return (_dot_nt(sel, a1) + (_dot_nt(sel, a2) + _dot_nt(sel, a3)))[0:1, :]


def _tri_mask(n, reverse, strict=False):
    t = _iota((n, n), 0)
    s = _iota((n, n), 1)
    if reverse:
        return (s > t) if strict else (s >= t)
    return (s < t) if strict else (s <= t)


def _masked_exp(mask, diff):
    return jnp.where(mask, jnp.exp(jnp.where(mask, diff, 0.0)), 0.0)


def _mod_kernel(c_ref, w_ref, b_ref, o_ref):
    o_ref[...] = _dot(_silu(c_ref[...]), w_ref[...], precision=HIGHEST) + b_ref[...]


def _modulation(cond8, ada_w, ada_b, *, layer):
    rows = cond8.shape[0]
    tn = 1536
    n = ada_w.shape[2]
    return pl.pallas_call(
        _mod_kernel,
        grid=(n // tn,),
        in_specs=[pl.BlockSpec((rows, D_MODEL), lambda j: (0, 0)),
                  pl.BlockSpec((None, D_MODEL, tn), lambda j: (layer, 0, j)),
                  pl.BlockSpec((1, tn), lambda j: (0, j))],
        out_specs=pl.BlockSpec((rows, tn), lambda j: (0, j)),
        out_shape=jax.ShapeDtypeStruct((rows, n), F32),
        compiler_params=_cparams(("arbitrary",)),
        name="adaln_mod",
    )(cond8, ada_w, ada_b.reshape(1, n))


def _inproj_kernel(x_ref, mod_ref, g_ref, w_ref, cw_ref, cb_ref, o_ref, h_scr, *, seg, n_conv_tiles):
    j = pl.program_id(1)

    @pl.when(j == 0)
    def _():
        x = x_ref[...]
        xn = x * lax.rsqrt(jnp.mean(x * x, axis=-1, keepdims=True) + EPS) * g_ref[...]
        m = mod_ref[...]
        h_scr[...] = (xn * (1.0 + m[1:2]) + m[0:1]).astype(BF16)

    y = _dot(h_scr[...], w_ref[...])

    @pl.when(j < n_conv_tiles)
    def _():
        tm = y.shape[0]
        pos = _iota(y.shape, 0) & (seg - 1)
        prev = jnp.where(pos == 0, 0.0, pltpu.roll(y, 1, 0))
        nxt = jnp.where(pos == seg - 1, 0.0, pltpu.roll(y, tm - 1, 0))
        cw = cw_ref[...]
        o_ref[...] = cb_ref[...] + prev * cw[0:1] + y * cw[1:2] + nxt * cw[2:3]

    @pl.when(j >= n_conv_tiles)
    def _():
        o_ref[...] = y


def _in_projection(x, mod, norm_g, w_bf16, conv_w, conv_b, *, layer, tokens_per_mod, seg):
    T = x.shape[0]
    tm = min(1024, tokens_per_mod)
    tn = 1024
    n_conv_tiles = D_CONV // tn
    tiles_per_mod = tokens_per_mod // tm
    kern = functools.partial(_inproj_kernel, seg=seg, n_conv_tiles=n_conv_tiles)
    cmap = lambda i, j: (0, jnp.minimum(j, n_conv_tiles - 1))
    return pl.pallas_call(
        kern,
        grid=(T // tm, D_PROJ // tn),
        in_specs=[pl.BlockSpec((tm, D_MODEL), lambda i, j: (i, 0)),
                  pl.BlockSpec((None, 6, D_MODEL), lambda i, j: (i // tiles_per_mod, 0, 0)),
                  pl.BlockSpec((1, D_MODEL), lambda i, j: (0, 0)),
                  pl.BlockSpec((None, D_MODEL, tn), lambda i, j: (layer, 0, j)),
                  pl.BlockSpec((3, tn), cmap),
                  pl.BlockSpec((1, tn), cmap)],
        out_specs=pl.BlockSpec((tm, tn), lambda i, j: (i, j)),
        out_shape=jax.ShapeDtypeStruct((T, D_PROJ), F32),
        scratch_shapes=[pltpu.VMEM((tm, D_MODEL), BF16)],
        compiler_params=_cparams(("arbitrary", "arbitrary")),
        name="in_proj",
    )(x, mod, norm_g.reshape(1, D_MODEL), w_bf16, conv_w, conv_b.reshape(1, D_CONV))


def _hyfilt_kernel(w1_ref, b1_ref, w2_ref, b2_ref, w3_ref, o_ref, *, L):
    i = pl.program_id(1)
    tl = o_ref.shape[0]
    t = (_iota((tl, LANES), 0) + i * tl).astype(F32) / L
    lane = _iota((tl, LANES), 1)
    band = jnp.where(lane <= HY_POS_FREQS, lane, lane - HY_POS_FREQS).astype(F32)
    ang = 2.0 * math.pi * t * band
    feats = jnp.where(lane == 0, t,
                      jnp.where(lane <= HY_POS_FREQS, jnp.sin(ang),
                                jnp.where(lane <= 2 * HY_POS_FREQS, jnp.cos(ang), 0.0)))
    hdn = jnp.sin(_dot(feats, w1_ref[...], precision=HIGHEST) + b1_ref[...])
    hdn = jnp.sin(_dot(hdn, w2_ref[...], precision=HIGHEST) + b2_ref[...])
    filt = _dot(hdn, w3_ref[...], precision=HIGHEST)
    max_decay = math.log(HY_DECAY_TARGET) / HY_FAST_DECAY
    min_decay = math.log(HY_DECAY_TARGET) / HY_SLOW_DECAY
    n = filt.shape[1]
    ch = (_iota((tl, n), 1) & (D_BRANCH - 1)).astype(F32)
    delta = min_decay + ch * ((max_decay - min_decay) / (D_BRANCH - 1))
    tt = (_iota((tl, n), 0) + i * tl).astype(F32) / L
    o_ref[...] = filt * jnp.exp(-tt * jnp.abs(delta))


def _hyena_filters(L, w1p, b1, w2, b2, w3):
    tl = min(L, 256)
    n = 2 * D_BRANCH
    return pl.pallas_call(
        functools.partial(_hyfilt_kernel, L=L),
        grid=(2, L // tl),
        in_specs=[pl.BlockSpec((LANES, HY_FILTER_HIDDEN), lambda d, i: (0, 0)),
                  pl.BlockSpec((1, HY_FILTER_HIDDEN), lambda d, i: (0, 0)),
                  pl.BlockSpec((HY_FILTER_HIDDEN, HY_FILTER_HIDDEN), lambda d, i: (0, 0)),
                  pl.BlockSpec((1, HY_FILTER_HIDDEN), lambda d, i: (0, 0)),
                  pl.BlockSpec((HY_FILTER_HIDDEN, n), lambda d, i: (0, d))],
        out_specs=pl.BlockSpec((None, tl, n), lambda d, i: (d, i, 0)),
        out_shape=jax.ShapeDtypeStruct((2, L, n), F32),
        compiler_params=_cparams(("arbitrary", "arbitrary")),
        name="hyena_filters",
    )(w1p, b1.reshape(1, -1), w2, b2.reshape(1, -1), w3)


def _dftgen_kernel(f_ref, fs_ref, *, L, tk):
    i = pl.program_id(0)
    N = 2 * L
    k = _iota((tk, LANES), 0) + i * tk
    lane = _iota((tk, LANES), 1)
    w = 2.0 * math.pi / N
    a0 = ((k * lane) & (N - 1)).astype(F32) * w
    c0, s0 = jnp.cos(a0), jnp.sin(a0)
    a1 = ((k * lane * LANES) & (N - 1)).astype(F32) * w
    c1, s1 = jnp.cos(a1), jnp.sin(a1)
    alt = jnp.where((lane & 1) == 0, 1.0, -1.0)
    coef = jnp.where(k == 0, 1.0 / N, 2.0 / N)
    for n1 in range(L // LANES):
        c1b = c1[:, n1:n1 + 1]
        s1b = s1[:, n1:n1 + 1]
        cosb = c1b * c0 - s1b * s0
        sinb = jnp.where(k == 0, alt, -(s1b * c0 + c1b * s0))
        cols = slice(n1 * LANES, (n1 + 1) * LANES)
        f_ref[0:tk, cols] = cosb.astype(BF16)
        f_ref[tk:2 * tk, cols] = sinb.astype(BF16)
        fs_ref[0:tk, cols] = (coef * cosb).astype(BF16)
        fs_ref[tk:2 * tk, cols] = (coef * sinb).astype(BF16)


def _dft_matrices(L):
    tk = min(L, 256)
    spec = pl.BlockSpec((2 * tk, L), lambda i: (i, 0))
    return pl.pallas_call(
        functools.partial(_dftgen_kernel, L=L, tk=tk),
        grid=(L // tk,),
        in_specs=[],
        out_specs=[spec, spec],
        out_shape=[jax.ShapeDtypeStruct((2 * L, L), BF16)] * 2,
        compiler_params=_cparams(("arbitrary",)),
        name="dft_matrices",
    )()


def _dfth_kernel(f_ref, h_ref, o_ref, hp_scr, *, tk):
    i = pl.program_id(1)
    n = o_ref.shape[1]

    @pl.when(i == 0)
    def _():
        hc = h_ref[0]
        ha = jnp.where(_iota(hc.shape, 0) == 0, 0.0, h_ref[1])
        hp_scr[:, 0:n] = (hc + ha).astype(BF16)
        hp_scr[:, n:2 * n] = (hc - ha).astype(BF16)

    u = _dot(f_ref[...], hp_scr[...])
    r = _iota((2 * tk, n), 0)
    from_sum = (r < tk) | ((r == tk) & (i == 0))
    o_ref[...] = jnp.where(from_sum, u[:, 0:n], u[:, n:2 * n])


def _filter_spectrum(fmat, filt):
    L = fmat.shape[1]
    tk = min(L, 256)
    C = filt.shape[2]
    tc = 256
    return pl.pallas_call(
        functools.partial(_dfth_kernel, tk=tk),
        grid=(C // tc, L // tk),
        in_specs=[pl.BlockSpec((2 * tk, L), lambda c, i: (i, 0)),
                  pl.BlockSpec((2, L, tc), lambda c, i: (0, 0, c))],
        out_specs=pl.BlockSpec((2 * tk, tc), lambda c, i: (i, c)),
        out_shape=jax.ShapeDtypeStruct((2 * L, C), F32),
        scratch_shapes=[pltpu.VMEM((L, 2 * tc), BF16)],
        compiler_params=_cparams(("arbitrary", "arbitrary")),
        name="filter_spectrum",
    )(fmat, filt)


def _dfta_kernel(f_ref, u_ref, h_ref, y_ref, u_scr, *, tk):
    i = pl.program_id(1)

    @pl.when(i == 0)
    def _():
        u_scr[...] = u_ref[...].astype(BF16)

    uf = _dot(f_ref[...], u_scr[...])
    ur, ui = uf[0:tk], uf[tk:2 * tk]
    hr, hi = h_ref[0:tk, :], h_ref[tk:2 * tk, :]
    dc = (_iota(ur.shape, 0) == 0) & (i == 0)
    y_ref[0:tk, :] = jnp.where(dc, ur * hr, ur * hr - ui * hi).astype(BF16)
    y_ref[tk:2 * tk, :] = jnp.where(dc, ui * hi, ur * hi + ui * hr).astype(BF16)


def _dftb_kernel(g_ref, y_ref, u_ref, x_ref, b_ref, o_ref):
    y = _dot(g_ref[...], y_ref[...])
    o_ref[...] = x_ref[...] * (y + u_ref[...] * b_ref[...])


def _spectral_conv(fmat, gmat, hspec, h_col, bias, u_arr, u_col, gate_arr, gate_col):
    B, L = u_arr.shape[0], u_arr.shape[1]
    C = D_BRANCH
    tk = min(L, 256)
    y = pl.pallas_call(
        functools.partial(_dfta_kernel, tk=tk),
        grid=(B, L // tk),
        in_specs=[pl.BlockSpec((2 * tk, L), lambda b, i: (i, 0)),
                  pl.BlockSpec((None, L, C), lambda b, i: (b, 0, u_col)),
                  pl.BlockSpec((2 * tk, C), lambda b, i: (i, h_col))],
        out_specs=pl.BlockSpec((None, 2 * tk, C), lambda b, i: (b, i, 0)),
        out_shape=jax.ShapeDtypeStruct((B, 2 * L, C), BF16),
        scratch_shapes=[pltpu.VMEM((L, C), BF16)],
        compiler_params=_cparams(("arbitrary", "arbitrary")),
        name="hyena_dft_fwd",
    )(fmat, u_arr, hspec)
    tr = min(L, 256)
    return pl.pallas_call(
        _dftb_kernel,
        grid=(B, L // tr),
        in_specs=[pl.BlockSpec((tr, 2 * L), lambda b, i: (i, 0)),
                  pl.BlockSpec((None, 2 * L, C), lambda b, i: (b, 0, 0)),
                  pl.BlockSpec((None, tr, C), lambda b, i: (b, i, u_col)),
                  pl.BlockSpec((None, tr, C), lambda b, i: (b, i, gate_col)),
                  pl.BlockSpec((1, C), lambda b, i: (0, 0))],
        out_specs=pl.BlockSpec((None, tr, C), lambda b, i: (b, i, 0)),
        out_shape=jax.ShapeDtypeStruct((B, L, C), F32),
        compiler_params=_cparams(("arbitrary", "arbitrary")),
        name="hyena_dft_inv",
    )(gmat, y, u_arr, gate_arr, bias.reshape(1, C))


def _hgrn_kernel(q_ref, ff_ref, fb_ref, v_ref, g_ref, lb_ref, nrm_ref, s0_ref, y_ref, sf_ref,
                 of_scr, ob_scr, st_scr, *, L, layer, depth):
    C = HG_CHUNK
    W = HG_DK
    nc = L // C
    nh = q_ref.shape[1] // W
    ridx = _iota((C, W), 0)
    o_scrs = (of_scr, ob_scr)
    f_refs = (ff_ref, fb_ref)

    def lower_bound(d, cols):
        rows = [lb_ref[d, l:l + 1, cols] for l in range(depth)]
        m = rows[0]
        for r in rows[1:]:
            m = jnp.maximum(m, r)
        es = [jnp.exp(r - m) for r in rows]
        tot = es[0]
        for e in es[1:]:
            tot = tot + e
        acc = es[0] / tot
        for e in es[1:layer + 1]:
            acc = acc + e / tot
        return acc - es[0] / tot

    lbs = [[lower_bound(d, slice(hh * W, (hh + 1) * W)) for hh in range(nh)] for d in (0, 1)]
    for d in (0, 1):
        for hh in range(nh):
            st_scr[d, hh] = s0_ref[d, hh].T

    def step(c, d, hh):
        rev = d == 1
        rows = pl.ds(pl.multiple_of(c * C, C), C)
        cols = slice(hh * W, (hh + 1) * W)
        lb = lbs[d][hh]
        q = _silu(q_ref[rows, cols])
        uf = f_refs[d][rows, cols]
        v = v_ref[rows, cols]
        f = lb + (1.0 - lb) * _sigmoid(uf)
        g = jnp.log(jnp.maximum(f, LOG_FLOOR))
        kin = (1.0 - lb) * _sigmoid(-uf)
        b = _cumsum_rows(g, rev)
        st = st_scr[d, hh]
        o = _bdot_nt(q * jnp.exp(b), st)
        intra = jnp.zeros((C, W), F32)
        for t in range(C):
            mask = (ridx >= t) if rev else (ridx <= t)
            pair = _masked_exp(mask, b[t:t + 1, :] - b)
            a = jnp.sum(pair * (q[t:t + 1, :] * kin), axis=1, keepdims=True)
            row = jnp.sum(a * v, axis=0, keepdims=True)
            intra = jnp.where(ridx == t, row, intra)
        b_end = b[0:1, :] if rev else b[C - 1:C, :]
        st_scr[d, hh] = st * jnp.exp(b_end) + _bdot_tn(v, kin * jnp.exp(b_end - b))
        o_scrs[d][rows, cols] = o + intra

    def body(ci, carry):
        for d in (0, 1):
            for hh in range(nh):
                step((nc - 1 - ci) if d == 1 else ci, d, hh)
        return carry

    lax.fori_loop(0, nc, body, 0)
    for d in (0, 1):
        for hh in range(nh):
            sf_ref[d, hh] = st_scr[d, hh].T

    tr = min(L, 256)

    def fin(i, carry):
        rows = pl.ds(pl.multiple_of(i * tr, tr), tr)
        for hh in range(nh):
            cols = slice(hh * W, (hh + 1) * W)
            o = of_scr[rows, cols] + ob_scr[rows, cols]
            on = o * lax.rsqrt(jnp.mean(o * o, axis=-1, keepdims=True) + EPS) * nrm_ref[...]
            y_ref[rows, cols] = on * _silu(g_ref[rows, cols])
        return carry

    lax.fori_loop(0, L // tr, fin, 0)


def _hgrn_scan(proj3, hg_lb, hg_norm, s0, *, layer):
    B, L = proj3.shape[0], proj3.shape[1]
    depth = hg_lb.shape[1]
    nh = 2
    W = nh * HG_DK
    col = lambda base: (lambda b, h: (b, 0, base // W + h))
    st_spec = pl.BlockSpec((None, 2, nh, HG_DK, HG_DK), lambda b, h: (b, 0, h, 0, 0))
    return pl.pallas_call(
        functools.partial(_hgrn_kernel, L=L, layer=layer, depth=depth),
        grid=(B, HG_HEADS // nh),
        in_specs=[pl.BlockSpec((None, L, W), col(P_HGQ)),
                  pl.BlockSpec((None, L, W), col(P_HGFF)),
                  pl.BlockSpec((None, L, W), col(P_HGFB)),
                  pl.BlockSpec((None, L, W), col(P_HGV)),
                  pl.BlockSpec((None, L, W), col(P_HGG)),
                  pl.BlockSpec((2, depth, W), lambda b, h: (0, 0, h)),
                  pl.BlockSpec((1, HG_DK), lambda b, h: (0, 0)),
                  st_spec],
        out_specs=[pl.BlockSpec((None, L, W), lambda b, h: (b, 0, h)), st_spec],
        out_shape=[jax.ShapeDtypeStruct((B, L, D_BRANCH), F32),
                   jax.ShapeDtypeStruct(s0.shape, F32)],
        scratch_shapes=[pltpu.VMEM((L, W), F32), pltpu.VMEM((L, W), F32),
                        pltpu.VMEM((2, nh, HG_DK, HG_DK), F32)],
        compiler_params=_cparams(("arbitrary", "arbitrary")),
        name="hgrn2_scan",
    )(proj3, proj3, proj3, proj3, proj3, hg_lb, hg_norm.reshape(1, HG_DK), s0)


def _ssd_kernel(x_ref, bm_ref, cm_ref, z_ref, sm_ref, bias_ref, alog_ref, dskip_ref, s0_ref,
                y_ref, sf_ref, of_scr, ob_scr, st_scr, *, L):
    C = SSD_CHUNK
    W = LANES
    nc = L // C
    npairs = x_ref.shape[1] // W
    grp = pl.program_id(1)
    lane_lo = _iota((C, W), 1) < SSD_P
    row_lo = _iota((W, SSD_N), 0) < SSD_P
    pick2 = lambda a0, a1: jnp.where(lane_lo, a0, a1)
    o_scrs = (of_scr, ob_scr)
    incls = (_tri_mask(C, False), _tri_mask(C, True))
    for d in (0, 1):
        for pp in range(npairs):
            st_scr[d, pp] = s0_ref[d, pp]

    heads = [(d, pp, hh) for d in (0, 1) for pp in range(npairs) for hh in (0, 1)]
    pairs = [(d, pp) for d in (0, 1) for pp in range(npairs)]

    def body(ci, carry):
        rows_d, bm, cm, dt_all, cum_all, gram = [], [], [], [], [], []
        for d in (0, 1):
            c = (nc - 1 - ci) if d == 1 else ci
            rows = pl.ds(pl.multiple_of(c * C, C), C)
            rows_d.append(rows)
            bm.append(_silu(bm_ref[rows, :]))
            cm.append(_silu(cm_ref[rows, :]))
            dt_all.append(_softplus(sm_ref[rows, :] + bias_ref[...]))
            cum_all.append(_cumsum_rows(-jnp.exp(alog_ref[...]) * dt_all[d], d == 1))
        lane_of = lambda u: S_DT + u[0] * SSD_HEADS + 2 * (npairs * grp + u[1]) + u[2]
        cumr = {u: _row_pick(cum_all[u[0]], lane_of(u)) for u in heads}
        for d in (0, 1):
            gram.append(_bdot_nt(cm[d], bm[d]))
        dtc = {u: _lane_pick(dt_all[u[0]], lane_of(u)) for u in heads}
        cumc = {u: _lane_pick(cum_all[u[0]], lane_of(u)) for u in heads}
        end = {u: (cumc[u][0:1, :] if u[0] == 1 else cumc[u][C - 1:C, :]) for u in heads}
        xs = {pr: _silu(x_ref[rows_d[pr[0]], slice(pr[1] * W, (pr[1] + 1) * W)]) for pr in pairs}
        st = {pr: st_scr[pr[0], pr[1]] for pr in pairs}
        y_state = {u: _bdot_nt(cm[u[0]] * jnp.exp(cumc[u]), st[u[:2]]) for u in heads}
        scores = {u: gram[u[0]] * _masked_exp(incls[u[0]], cumc[u] - cumr[u]) for u in heads}
        y_intra = {}
        for u in heads:
            head_x = jnp.where(lane_lo if u[2] == 0 else ~lane_lo, xs[u[:2]], 0.0) * dtc[u]
            y_intra[u] = _bdot(scores[u], head_x)
        for pr in pairs:
            u0, u1 = pr + (0,), pr + (1,)
            xdt_e = xs[pr] * pick2(dtc[u0] * jnp.exp(end[u0] - cumc[u0]), dtc[u1] * jnp.exp(end[u1] - cumc[u1]))
            st_scr[pr[0], pr[1]] = (st[pr] * jnp.where(row_lo, jnp.exp(end[u0]), jnp.exp(end[u1]))
                                    + _bdot_tn(xdt_e, bm[pr[0]]))
            o_scrs[pr[0]][rows_d[pr[0]], slice(pr[1] * W, (pr[1] + 1) * W)] = pick2(
                y_intra[u0] + y_state[u0], y_intra[u1] + y_state[u1])
        return carry

    lax.fori_loop(0, nc, body, 0)
    for d in (0, 1):
        for pp in range(npairs):
            sf_ref[d, pp] = st_scr[d, pp]

    tr = min(L, 256)

    def fin(i, carry):
        rows = pl.ds(pl.multiple_of(i * tr, tr), tr)
        y = of_scr[rows, :] + ob_scr[rows, :] + _silu(x_ref[rows, :]) * dskip_ref[...]
        y_ref[rows, :] = y * _silu(z_ref[rows, :])
        return carry

    lax.fori_loop(0, L // tr, fin, 0)


def _ssd_scan(proj3, bias_row, alog_row, ssd_d, s0):
    B, L = proj3.shape[0], proj3.shape[1]
    n_groups = 2
    n_pairs = SSD_HEADS // 2
    ppg = n_pairs // n_groups
    W = ppg * LANES
    s0p = s0.reshape(B, 2, n_pairs, 2 * SSD_P, SSD_N)
    dskip = jnp.repeat(ssd_d, SSD_P).reshape(n_groups, 1, W)
    col = lambda base: (lambda b, g: (b, 0, base // W + g))
    grp = lambda base: (lambda b, g: (b, 0, base // LANES + g))
    st_spec = pl.BlockSpec((None, 2, ppg, LANES, SSD_N), lambda b, g: (b, 0, g, 0, 0))
    y, sf = pl.pallas_call(
        functools.partial(_ssd_kernel, L=L),
        grid=(B, n_groups),
        in_specs=[pl.BlockSpec((None, L, W), col(C_SSDX)),
                  pl.BlockSpec((None, L, LANES), grp(C_SSDB)),
                  pl.BlockSpec((None, L, LANES), grp(C_SSDC)),
                  pl.BlockSpec((None, L, W), col(P_SSDZ)),
                  pl.BlockSpec((None, L, LANES), lambda b, g: (b, 0, P_SMALL // LANES)),
                  pl.BlockSpec((1, LANES), lambda b, g: (0, 0)),
                  pl.BlockSpec((1, LANES), lambda b, g: (0, 0)),
                  pl.BlockSpec((None, 1, W), lambda b, g: (g, 0, 0)),
                  st_spec],
        out_specs=[pl.BlockSpec((None, L, W), lambda b, g: (b, 0, g)), st_spec],
        out_shape=[jax.ShapeDtypeStruct((B, L, D_BRANCH), F32),
                   jax.ShapeDtypeStruct(s0p.shape, F32)],
        scratch_shapes=[pltpu.VMEM((L, W), F32), pltpu.VMEM((L, W), F32),
                        pltpu.VMEM((2, ppg, LANES, SSD_N), F32)],
        compiler_params=_cparams(("arbitrary", "arbitrary")),
        name="ssd_scan",
    )(proj3, proj3, proj3, proj3, proj3, bias_row, alog_row, dskip, s0p)
    return y, sf.reshape(s0.shape)


def _l2norm(a):
    return a * lax.rsqrt(jnp.sum(a * a, axis=-1, keepdims=True) + EPS)


def _gdn_kernel(q_ref, k_ref, v_ref, g_ref, sm_ref, bias_ref, alog_ref, nrm_ref, s0_ref,
                y_ref, sf_ref, of_scr, ob_scr, st_scr, *, L):
    C = GDN_CHUNK
    W = GDN_DK
    nc = L // C
    nh = q_ref.shape[1] // W
    hblk = pl.program_id(1)
    o_scrs = (of_scr, ob_scr)
    incls = (_tri_mask(C, False), _tri_mask(C, True))
    stricts = (_tri_mask(C, False, strict=True), _tri_mask(C, True, strict=True))
    for d in (0, 1):
        for hh in range(nh):
            st_scr[d, hh] = s0_ref[d, hh]

    units = [(d, hh) for d in (0, 1) for hh in range(nh)]
    eye = (_iota((C, C), 0) == _iota((C, C), 1)).astype(F32)

    def body(ci, carry):
        rows_d, cum_d, beta_d = [], [], []
        for d in (0, 1):
            c = (nc - 1 - ci) if d == 1 else ci
            rows = pl.ds(pl.multiple_of(c * C, C), C)
            raw = sm_ref[rows, :]
            rows_d.append(rows)
            cum_d.append(_cumsum_rows(-jnp.exp(alog_ref[...]) * _softplus(raw + bias_ref[...]), d == 1))
            beta_d.append(_sigmoid(raw))
        q, k, v, gc, grow, beta = {}, {}, {}, {}, {}, {}
        for u in units:
            d, hh = u
            cols = slice(hh * W, (hh + 1) * W)
            h = nh * hblk + hh
            q[u] = _l2norm(_silu(q_ref[rows_d[d], cols])) * (GDN_DK ** -0.5)
            k[u] = _l2norm(_silu(k_ref[rows_d[d], cols]))
            v[u] = _silu(v_ref[rows_d[d], cols])
            gc[u] = _lane_pick(cum_d[d], S_GA + d * GDN_HEADS + h)
            beta[u] = _lane_pick(beta_d[d], S_GB + d * GDN_HEADS + h)
        for u in units:
            grow[u] = _row_pick(cum_d[u[0]], S_GA + u[0] * GDN_HEADS + nh * hblk + u[1])
        decay = {u: _masked_exp(incls[u[0]], gc[u] - grow[u]) for u in units}
        kb = {u: k[u] * beta[u] for u in units}
        m = {u: -jnp.where(stricts[u[0]], _bdot_nt(kb[u], k[u]) * decay[u], 0.0) for u in units}
        aqk = {u: _bdot_nt(q[u], k[u]) * decay[u] for u in units}
        p = {u: eye + m[u] for u in units}
        sh = 2
        while sh < C:
            m = {u: _dot3(m[u], m[u]) for u in units}
            p = {u: p[u] + _dot3(p[u], m[u]) for u in units}
            sh *= 2
        uu = {u: _dot3(p[u], v[u] * beta[u]) for u in units}
        ww = {u: _dot3(p[u], kb[u] * jnp.exp(gc[u])) for u in units}
        st = {u: st_scr[u[0], u[1]] for u in units}
        v_new = {u: uu[u] - _bdot(ww[u], st[u]) for u in units}
        o = {u: _bdot(q[u] * jnp.exp(gc[u]), st[u]) + _bdot(aqk[u], v_new[u]) for u in units}
        for u in units:
            d, hh = u
            g_end = gc[u][0:1, :] if d == 1 else gc[u][C - 1:C, :]
            st_scr[d, hh] = st[u] * jnp.exp(g_end) + _bdot_tn(k[u] * jnp.exp(g_end - gc[u]), v_new[u])
            o_scrs[d][rows_d[d], slice(hh * W, (hh + 1) * W)] = o[u]
        return carry

    lax.fori_loop(0, nc, body, 0)
    for d in (0, 1):
        for hh in range(nh):
            sf_ref[d, hh] = st_scr[d, hh]

    tr = min(L, 256)

    def fin(i, carry):
        rows = pl.ds(pl.multiple_of(i * tr, tr), tr)
        for hh in range(nh):
            cols = slice(hh * W, (hh + 1) * W)
            o = of_scr[rows, cols] + ob_scr[rows, cols]
            on = o * lax.rsqrt(jnp.mean(o * o, axis=-1, keepdims=True) + EPS) * nrm_ref[...]
            y_ref[rows, cols] = on * _silu(g_ref[rows, cols])
        return carry

    lax.fori_loop(0, L // tr, fin, 0)


def _gdn_scan(proj3, bias_row, alog_row, gdn_norm, s0):
    B, L = proj3.shape[0], proj3.shape[1]
    nh = GDN_HEADS
    W = nh * GDN_DK
    col = lambda base: (lambda b, h: (b, 0, base // W + h))
    st_spec = pl.BlockSpec((None, 2, nh, GDN_DK, GDN_DK), lambda b, h: (b, 0, h, 0, 0))
    once = pl.Buffered(1)
    return pl.pallas_call(
        functools.partial(_gdn_kernel, L=L),
        grid=(B, GDN_HEADS // nh),
        in_specs=[pl.BlockSpec((None, L, W), col(C_GQ), pipeline_mode=once),
                  pl.BlockSpec((None, L, W), col(C_GK), pipeline_mode=once),
                  pl.BlockSpec((None, L, W), col(C_GV), pipeline_mode=once),
                  pl.BlockSpec((None, L, W), col(P_GDNG), pipeline_mode=once),
                  pl.BlockSpec((None, L, LANES), lambda b, h: (b, 0, P_SMALL // LANES), pipeline_mode=once),
                  pl.BlockSpec((1, LANES), lambda b, h: (0, 0)),
                  pl.BlockSpec((1, LANES), lambda b, h: (0, 0)),
                  pl.BlockSpec((1, GDN_DK), lambda b, h: (0, 0)),
                  st_spec],
        out_specs=[pl.BlockSpec((None, L, W), lambda b, h: (b, 0, h)), st_spec],
        out_shape=[jax.ShapeDtypeStruct((B, L, D_BRANCH), F32),
                   jax.ShapeDtypeStruct(s0.shape, F32)],
        scratch_shapes=[pltpu.VMEM((L, W), F32), pltpu.VMEM((L, W), F32),
                        pltpu.VMEM((2, nh, GDN_DK, GDN_DK), F32)],
        compiler_params=_cparams(("arbitrary", "arbitrary")),
        name="gdn_scan",
    )(proj3, proj3, proj3, proj3, proj3, bias_row, alog_row, gdn_norm.reshape(1, GDN_DK), s0)


def _merge_kernel(ya_ref, yb_ref, yc_ref, yd_ref, mg_ref, x_ref, mod_ref, sn_ref, n2_ref, wb_ref, wo_ref,
                  xo_ref, h_ref):
    yc = yc_ref[...]
    half = D_BRANCH // 2
    parts = []
    for gidx in range(2):
        seg = yc[:, gidx * half:(gidx + 1) * half]
        parts.append(seg * lax.rsqrt(jnp.mean(seg * seg, axis=-1, keepdims=True) + EPS)
                     * sn_ref[:, gidx * half:(gidx + 1) * half])
    branches = (ya_ref[...], yb_ref[...], None, yd_ref[...])
    mixed = None
    for n in range(N_BRANCH):
        gate = _sigmoid(mg_ref[:, n * D_MODEL:(n + 1) * D_MODEL])
        if n == 2:
            lifted = (_dot(parts[0].astype(BF16), wb_ref[n, 0:half, :])
                      + _dot(parts[1].astype(BF16), wb_ref[n, half:D_BRANCH, :]))
        else:
            lifted = _dot(branches[n].astype(BF16), wb_ref[n])
        mixed = gate * lifted if mixed is None else mixed + gate * lifted
    m = mod_ref[...]
    x = x_ref[...] + m[2:3] * _dot(mixed.astype(BF16), wo_ref[...])
    xo_ref[...] = x
    xn = x * lax.rsqrt(jnp.mean(x * x, axis=-1, keepdims=True) + EPS) * n2_ref[...]
    h_ref[...] = xn * (1.0 + m[4:5]) + m[3:4]


def _merge(ya, yb, yc, yd, proj, x, mod, ssd_norm, norm2_g, wb_bf16, wo_bf16, *, tokens_per_mod):
    T = x.shape[0]
    tm = 256
    tiles_per_mod = tokens_per_mod // tm
    yspec = pl.BlockSpec((tm, D_BRANCH), lambda i: (i, 0))
    xspec = pl.BlockSpec((tm, D_MODEL), lambda i: (i, 0))
    return pl.pallas_call(
        _merge_kernel,
        grid=(T // tm,),
        in_specs=[yspec, yspec, yspec, yspec,
                  pl.BlockSpec((tm, N_BRANCH * D_MODEL), lambda i: (i, P_MERGE // (N_BRANCH * D_MODEL))),
                  xspec,
                  pl.BlockSpec((None, 6, D_MODEL), lambda i: (i // tiles_per_mod, 0, 0)),
                  pl.BlockSpec((1, D_BRANCH), lambda i: (0, 0)),
                  pl.BlockSpec((1, D_MODEL), lambda i: (0, 0)),
                  pl.BlockSpec((N_BRANCH, D_BRANCH, D_MODEL), lambda i: (0, 0, 0)),
                  pl.BlockSpec((D_MODEL, D_MODEL), lambda i: (0, 0))],
        out_specs=[xspec, xspec],
        out_shape=[jax.ShapeDtypeStruct((T, D_MODEL), F32)] * 2,
        compiler_params=_cparams(("arbitrary",)),
        name="merge_outproj",
    )(ya, yb, yc, yd, proj, x, mod, ssd_norm.reshape(1, D_BRANCH), norm2_g.reshape(1, D_MODEL), wb_bf16, wo_bf16)


def _router_kernel(h_ref, rw_ref, rb_ref, wt_ref, lpos_ref, tlen_ref, tstart_ref, tcar_ref, carry):
    i = pl.program_id(0)
    tm = h_ref.shape[0]
    E = N_EXPERTS
    gsz = E // N_EXPERT_GROUPS
    neg_inf = -jnp.inf

    @pl.when(i == 0)
    def _():
        carry[...] = jnp.zeros_like(carry)

    scores = _sigmoid(_dot_nt(rw_ref[...], h_ref[...], precision=HIGHEST))
    biased = scores + rb_ref[...]
    eidx = _iota((E, tm), 0)
    ridx = _iota((gsz, tm), 0)

    slabs = [biased[g * gsz:(g + 1) * gsz, :] for g in range(N_EXPERT_GROUPS)]
    gs = []
    for v in slabs:
        m1 = jnp.max(v, axis=0, keepdims=True)
        i1 = jnp.min(jnp.where(v == m1, ridx, gsz), axis=0, keepdims=True)
        m2 = jnp.max(jnp.where(ridx == i1, neg_inf, v), axis=0, keepdims=True)
        gs.append(m1 + m2)
    masked = []
    for g in range(N_EXPERT_GROUPS):
        rank = jnp.zeros((1, tm), I32)
        for g2 in range(N_EXPERT_GROUPS):
            if g2 == g:
                continue
            ahead = (gs[g2] >= gs[g]) if g2 < g else (gs[g2] > gs[g])
            rank = rank + ahead.astype(I32)
        masked.append(jnp.where(rank < TOPK_GROUPS, slabs[g], MASK_NEG))
    cur = jnp.concatenate(masked, axis=0)

    krow = _iota((TOP_K, tm), 0)
    onehot = jnp.zeros((E, tm), F32)
    picks = []
    wsum = jnp.zeros((1, tm), F32)
    for kk in range(TOP_K):
        m = jnp.max(cur, axis=0, keepdims=True)
        ik = jnp.min(jnp.where(cur == m, eidx, E), axis=0, keepdims=True)
        hit = eidx == ik
        wk = jnp.sum(jnp.where(hit, scores, 0.0), axis=0, keepdims=True)
        cur = jnp.where(hit, neg_inf, cur)
        onehot = onehot + hit.astype(F32)
        picks.append((ik, wk))
        wsum = wsum + wk
    wt_out = jnp.zeros((TOP_K, tm), F32)
    for kk, (ik, wk) in enumerate(picks):
        wt_out = jnp.where(krow == kk, wk / wsum * ROUTED_SCALE, wt_out)

    earlier = (_iota((tm, tm), 0) < _iota((tm, tm), 1)).astype(BF16)
    before = _dot(onehot.astype(BF16), earlier)
    pad = lambda c: (c.astype(I32) + (RUN_ROWS - 1)) // RUN_ROWS * RUN_ROWS
    run_len = pad(jnp.sum(onehot, axis=1, keepdims=True) + jnp.zeros((E, LANES), F32)).astype(F32)
    run_start = _cumsum_rows(run_len, False) - run_len
    local = before + run_start[:, 0:1]
    lpos_out = jnp.zeros((TOP_K, tm), I32)
    for kk, (ik, wk) in enumerate(picks):
        lp = jnp.sum(jnp.where(eidx == ik, local, 0.0), axis=0, keepdims=True)
        lpos_out = jnp.where(krow == kk, lp.astype(I32), lpos_out)
    wt_ref[...] = wt_out
    lpos_ref[...] = lpos_out

    onehot_l = jnp.concatenate([onehot, jnp.zeros((LANES - E, tm), F32)], axis=0).astype(BF16)
    len_row = pad(_dot_nt(jnp.ones((8, tm), BF16), onehot_l)).astype(F32)
    lanes_before = (_iota((LANES, LANES), 0) < _iota((LANES, LANES), 1)).astype(BF16)
    tlen_ref[...] = len_row.astype(I32)
    tstart_ref[...] = _dot(len_row.astype(BF16), lanes_before).astype(I32)
    tcar_ref[...] = carry[...].astype(I32)
    carry[...] = carry[...] + len_row


def _router(h2, router_w, router_b, *, tm):
    T = h2.shape[0]
    nt = T // tm
    ospec = pl.BlockSpec((TOP_K, tm), lambda i: (0, i))
    tspec = pl.BlockSpec((None, 8, LANES), lambda i: (i, 0, 0))
    tshape = jax.ShapeDtypeStruct((nt, 8, LANES), I32)
    wt, lpos, tlen, tstart, tcar = pl.pallas_call(
        _router_kernel,
        grid=(nt,),
        in_specs=[pl.BlockSpec((tm, D_MODEL), lambda i: (i, 0)),
                  pl.BlockSpec((N_EXPERTS, D_MODEL), lambda i: (0, 0)),
                  pl.BlockSpec((N_EXPERTS, 1), lambda i: (0, 0))],
        out_specs=[ospec, ospec, tspec, tspec, tspec],
        out_shape=[jax.ShapeDtypeStruct((TOP_K, T), F32), jax.ShapeDtypeStruct((TOP_K, T), I32),
                   tshape, tshape, tshape],
        scratch_shapes=[pltpu.VMEM((8, LANES), F32)],
        compiler_params=_cparams(("arbitrary",)),
        name="moe_router",
    )(h2, router_w.T, router_b.reshape(N_EXPERTS, 1))
    table = lambda a: a[:, 0, :N_EXPERTS]
    return wt, lpos, table(tlen), table(tstart), table(tcar)


def _piece_count(len_ref, tile):
    return lax.fori_loop(0, N_EXPERTS, lambda e, n: n + len_ref[tile, e] // RUN_ROWS, 0)


def _dispatch_kernel(len_ref, lst_ref, car_ref, seg_ref, lpos_ref, h_ref, xs_ref, xloc, sem):
    i = pl.program_id(0)
    nt = pl.num_programs(0)
    slot = i % 2
    tm = h_ref.shape[0]
    nloc = xloc.shape[1]
    R = RUN_ROWS
    lpos = lpos_ref[...]
    hb = h_ref[...].astype(BF16)
    rb = 256
    for blk in range(nloc // rb):
        r = _iota((rb, tm), 0) + blk * rb
        hit = r == lpos[0:1, :]
        for kk in range(1, TOP_K):
            hit = hit | (r == lpos[kk:kk + 1, :])
        xloc[slot, blk * rb:(blk + 1) * rb, :] = _dot(hit.astype(BF16), hb)

    def piece(s, src_row, dst_row):
        return pltpu.make_async_copy(xloc.at[s, pl.ds(pl.multiple_of(src_row, R), R), :],
                                     xs_ref.at[pl.ds(pl.multiple_of(dst_row, R), R), :], sem.at[s])

    def wait_tile(tile, s):
        def wait(j, c):
            piece(s, 0, 0).wait()
            return c

        lax.fori_loop(0, _piece_count(len_ref, tile), wait, 0)

    @pl.when(i > 0)
    def _():
        wait_tile(i - 1, 1 - slot)

    def per_expert(e, c):
        ls = lst_ref[i, e]
        gd = seg_ref[e] + car_ref[i, e]

        def issue(j, c2):
            piece(slot, ls + j * R, gd + j * R).start()
            return c2

        return lax.fori_loop(0, len_ref[i, e] // R, issue, c)

    lax.fori_loop(0, N_EXPERTS, per_expert, 0)

    @pl.when(i == nt - 1)
    def _():
        wait_tile(i, slot)


def _dispatch(h2, lpos, tile_len, tile_lstart, tile_car, seg_start, *, tm):
    T = h2.shape[0]
    nt = T // tm
    n_rows = T * TOP_K + N_EXPERTS * nt * RUN_ROWS
    nloc = tm * TOP_K + N_EXPERTS * RUN_ROWS
    return pl.pallas_call(
        _dispatch_kernel,
        grid_spec=pltpu.PrefetchScalarGridSpec(
            num_scalar_prefetch=4,
            grid=(nt,),
            in_specs=[pl.BlockSpec((TOP_K, tm), lambda i, *_: (0, i)),
                      pl.BlockSpec((tm, D_MODEL), lambda i, *_: (i, 0))],
            out_specs=pl.BlockSpec(memory_space=pl.ANY),
            scratch_shapes=[pltpu.VMEM((2, nloc, D_MODEL), F32), pltpu.SemaphoreType.DMA((2,))]),
        out_shape=jax.ShapeDtypeStruct((n_rows, D_MODEL), F32),
        compiler_params=_cparams(("arbitrary",)),
        name="moe_dispatch",
    )(tile_len, tile_lstart, tile_car, seg_start, lpos, h2)


def _expert_kernel(blk_ref, exp_ref, lo_ref, hi_ref, first_ref, x_ref, w1_ref, w3_ref, w2_ref, o_ref):
    w = pl.program_id(0)
    x = x_ref[...].astype(BF16)
    a = _dot(x, w1_ref[...].astype(BF16))
    b = _dot(x, w3_ref[...].astype(BF16))
    y = _dot((_silu(a) * b).astype(BF16), w2_ref[...].astype(BF16))
    r = _iota(y.shape, 0)
    y = jnp.where((r >= lo_ref[w]) & (r < hi_ref[w]), y, 0.0)

    @pl.when(first_ref[w] == 1)
    def _():
        o_ref[...] = y

    @pl.when(first_ref[w] == 0)
    def _():
        o_ref[...] = o_ref[...] + y


def _experts(xs, work, w1, w3, w2, *, layer, bm):
    A = xs.shape[0]
    n_work = work[0].shape[0]
    xmap = lambda w, blk, ex, lo, hi, first: (blk[w], 0)
    wmap = lambda w, blk, ex, lo, hi, first: (layer, ex[w], 0, 0)
    return pl.pallas_call(
        _expert_kernel,
        grid_spec=pltpu.PrefetchScalarGridSpec(
            num_scalar_prefetch=5,
            grid=(n_work,),
            in_specs=[pl.BlockSpec((bm, D_MODEL), xmap),
                      pl.BlockSpec((None, None, D_MODEL, D_EXPERT), wmap),
                      pl.BlockSpec((None, None, D_MODEL, D_EXPERT), wmap),
                      pl.BlockSpec((None, None, D_EXPERT, D_MODEL), wmap)],
            out_specs=pl.BlockSpec((bm, D_MODEL), xmap)),
        out_shape=jax.ShapeDtypeStruct((A, D_MODEL), F32),
        compiler_params=_cparams(("arbitrary",)),
        name="moe_experts",
    )(*work, xs, w1, w3, w2)


def _combine_kernel(len_ref, lst_ref, car_ref, seg_ref, lpos_ref, lpost_ref, wt_ref, ys_ref, h_ref, x_ref, mod_ref,
                    s1_ref, s3_ref, s2_ref, fg_ref, o_ref, yloc, sem, *, final_norm):
    i = pl.program_id(0)
    nt = pl.num_programs(0)
    slot = i % 2
    tm = h_ref.shape[0]
    nloc = yloc.shape[1]
    R = RUN_ROWS

    def piece(s, src_row, dst_row):
        return pltpu.make_async_copy(ys_ref.at[pl.ds(pl.multiple_of(src_row, R), R), :],
                                     yloc.at[s, pl.ds(pl.multiple_of(dst_row, R), R), :], sem.at[s])

    def fetch_tile(tile, s):
        def per_expert(e, c):
            ls = lst_ref[tile, e]
            gd = seg_ref[e] + car_ref[tile, e]

            def issue(j, c2):
                piece(s, gd + j * R, ls + j * R).start()
                return c2

            return lax.fori_loop(0, len_ref[tile, e] // R, issue, c)

        lax.fori_loop(0, N_EXPERTS, per_expert, 0)

    @pl.when(i == 0)
    def _():
        yloc[...] = jnp.zeros_like(yloc)
        fetch_tile(0, 0)

    @pl.when(i + 1 < nt)
    def _():
        fetch_tile(i + 1, 1 - slot)

    hb = h_ref[...].astype(BF16)
    mid = _silu(_dot(hb, s1_ref[...])) * _dot(hb, s3_ref[...])
    y = _dot(mid.astype(BF16), s2_ref[...])

    def wait(j, c):
        piece(slot, 0, 0).wait()
        return c

    lax.fori_loop(0, _piece_count(len_ref, i), wait, 0)

    lpos = lpos_ref[...]
    wt = wt_ref[...]
    rb = 256
    for blk in range(nloc // rb):
        r = _iota((rb, tm), 0) + blk * rb
        pw = jnp.zeros((rb, tm), F32)
        for kk in range(TOP_K):
            pw = pw + jnp.where(r == lpos[kk:kk + 1, :], wt[kk:kk + 1, :], 0.0)
        wrow = jnp.sum(pw, axis=1, keepdims=True)
        ysc = yloc[slot, blk * rb:(blk + 1) * rb, :] * wrow
        hi = ysc.astype(BF16)
        lo = (ysc - hi.astype(F32)).astype(BF16)
        c = _iota((tm, rb), 1) + blk * rb
        hit = c == lpost_ref[:, 0:1]
        for kk in range(1, TOP_K):
            hit = hit | (c == lpost_ref[:, kk:kk + 1])
        pt = hit.astype(BF16)
        y = y + (_dot(pt, hi) + _dot(pt, lo))
    x = x_ref[...] + mod_ref[5:6, :] * y
    if final_norm:
        x = x * lax.rsqrt(jnp.mean(x * x, axis=-1, keepdims=True) + EPS) * fg_ref[...]
    o_ref[...] = x


def _combine(ys, lpos, lpos_t, wt, tile_len, tile_lstart, tile_car, seg_start, h2, x, mod, s1, s3, s2, final_g,
             *, tm, tokens_per_mod, final_norm):
    T = x.shape[0]
    tiles_per_mod = tokens_per_mod // tm
    nloc = tm * TOP_K + N_EXPERTS * RUN_ROWS
    xspec = pl.BlockSpec((tm, D_MODEL), lambda i, *_: (i, 0))
    kspec = pl.BlockSpec((TOP_K, tm), lambda i, *_: (0, i))
    const = lambda i, *_: (0, 0)
    return pl.pallas_call(
        functools.partial(_combine_kernel, final_norm=final_norm),
        grid_spec=pltpu.PrefetchScalarGridSpec(
            num_scalar_prefetch=4,
            grid=(T // tm,),
            in_specs=[kspec,
                      pl.BlockSpec((tm, LANES), lambda i, *_: (i, 0)),
                      kspec,
                      pl.BlockSpec(memory_space=pl.ANY),
                      xspec, xspec,
                      pl.BlockSpec((None, 6, D_MODEL), lambda i, *_: (i // tiles_per_mod, 0, 0)),
                      pl.BlockSpec((D_MODEL, D_SHARED), const),
                      pl.BlockSpec((D_MODEL, D_SHARED), const),
                      pl.BlockSpec((D_SHARED, D_MODEL), const),
                      pl.BlockSpec((1, D_MODEL), const)],
            out_specs=xspec,
            scratch_shapes=[pltpu.VMEM((2, nloc, D_MODEL), F32), pltpu.SemaphoreType.DMA((2,))]),
        out_shape=jax.ShapeDtypeStruct((T, D_MODEL), F32),
        compiler_params=_cparams(("arbitrary",)),
        name="moe_combine",
    )(tile_len, tile_lstart, tile_car, seg_start, lpos, lpos_t, wt, ys, h2, x, mod, s1, s3, s2, final_g.reshape(1, D_MODEL))


def _work_list(counts, starts, n_rows, bm):
    E = N_EXPERTS
    n_blocks = n_rows // bm
    n_work = n_blocks + E - 1
    ends = starts + counts
    first_blk = starts // bm
    last_blk = jnp.where(counts > 0, (ends - 1) // bm, first_blk)
    n_items = jnp.where(counts > 0, last_blk - first_blk + 1, 0)
    item_end = jnp.cumsum(n_items)
    item_start = item_end - n_items
    w = jnp.arange(n_work, dtype=I32)
    ex = jnp.minimum(jnp.sum((item_end[None, :] <= w[:, None]).astype(I32), axis=1), E - 1)
    valid = w < item_end[-1]
    blk = first_blk[ex] + (w - item_start[ex])
    blk = jnp.where(valid, blk, blk[jnp.maximum(item_end[-1] - 1, 0)]).astype(I32)
    lo = jnp.clip(starts[ex] - blk * bm, 0, bm)
    hi = jnp.clip(ends[ex] - blk * bm, 0, bm)
    lo = jnp.where(valid, lo, 0).astype(I32)
    hi = jnp.where(valid, hi, 0).astype(I32)
    ex = jnp.where(valid, ex, ex[jnp.maximum(item_end[-1] - 1, 0)])
    prev_blk = jnp.concatenate([jnp.full((1,), -1, I32), blk[:-1]])
    first = (blk != prev_blk).astype(I32)
    return blk, ex.astype(I32), lo, hi, first


def _moe(h2, x, mod, p, final_g, *, layer, tokens_per_mod, final_norm):
    T = x.shape[0]
    tm = 256
    wt, lpos, tile_len, tile_lstart, tile_car = _router(h2, p["router_w"], p["router_b"], tm=tm)
    seg_len = tile_car[-1] + tile_len[-1]
    seg_start = jnp.cumsum(seg_len) - seg_len
    xs = _dispatch(h2, lpos, tile_len, tile_lstart, tile_car, seg_start, tm=tm)
    bm = 512
    work = _work_list(seg_len, seg_start, xs.shape[0], bm)
    ys = _experts(xs, work, p["exp_w1"], p["exp_w3"], p["exp_w2"], layer=layer, bm=bm)
    lpos_t = jnp.pad(lpos.T, ((0, 0), (0, LANES - TOP_K)))
    return _combine(ys, lpos, lpos_t, wt, tile_len, tile_lstart, tile_car, seg_start, h2, x, mod,
                    p["sh_w1"], p["sh_w3"], p["sh_w2"], final_g,
                    tm=tm, tokens_per_mod=tokens_per_mod, final_norm=final_norm)


def _reorder_w_in(w_in):
    plain = D_CONV
    pieces = [w_in[..., :plain],
              w_in[..., plain + 3616:plain + 3616 + 4096],
              w_in[..., plain:plain + 3072],
              w_in[..., plain + 3088:plain + 3600],
              w_in[..., plain + 3072:plain + 3088],
              w_in[..., plain + 3600:plain + 3616]]
    out = jnp.concatenate(pieces, axis=-1)
    pad = D_PROJ - out.shape[-1]
    return jnp.pad(out, ((0, 0), (0, 0), (0, pad))).astype(BF16)


def _small_rows(ssd_vals, gdn_vals):
    row = jnp.zeros((LANES,), F32)
    row = row.at[S_DT:S_DT + 2 * SSD_HEADS].set(ssd_vals.reshape(-1))
    row = row.at[S_GA:S_GA + 2 * GDN_HEADS].set(gdn_vals.reshape(-1))
    return row.reshape(1, LANES)


def _layer_pass(x, mod, lp, hy, st_hg, st_ssd, st_gdn, final_g, *, B, L, tokens_per_mod, seg, layer, final_norm):
    T = B * L
    proj = _in_projection(x, mod, lp["norm1_g"], lp["w_in"], lp["conv_w"], lp["conv_b"],
                          layer=layer, tokens_per_mod=tokens_per_mod, seg=seg)
    proj3 = proj.reshape(B, L, D_PROJ)
    fmat, gmat, hspec = hy
    cb = lambda c: c // D_BRANCH
    z = _spectral_conv(fmat, gmat, hspec, 0, lp["hy_bias"][0], proj3, cb(C_HYV), proj3, cb(C_HYX1))
    ya = _spectral_conv(fmat, gmat, hspec, 1, lp["hy_bias"][1], z, 0, proj3, cb(C_HYX2))
    yb, s_hg = _hgrn_scan(proj3, lp["hg_lb"], lp["hg_norm"], st_hg, layer=layer)
    yc, s_ssd = _ssd_scan(proj3, lp["bias_row"], lp["alog_row"], lp["ssd_d"], st_ssd)
    yd, s_gdn = _gdn_scan(proj3, lp["bias_row"], lp["alog_row"], lp["gdn_norm"], st_gdn)
    flat = lambda a: a.reshape(T, D_BRANCH)
    x, h2 = _merge(flat(ya), flat(yb), flat(yc), flat(yd), proj, x, mod, lp["ssd_norm"], lp["norm2_g"],
                   lp["w_branch"], lp["w_out"], tokens_per_mod=tokens_per_mod)
    x = _moe(h2, x, mod, lp, final_g, layer=layer, tokens_per_mod=tokens_per_mod, final_norm=final_norm)
    return x, s_hg, s_ssd, s_gdn


def kernel(x_prompt, x_sample, state_hgrn, state_ssd, state_gdn, c, c_ctx, norm1_g, norm2_g, ada_w, ada_b, w_in, conv_w, conv_b, hy_w1, hy_b1, hy_w2, hy_b2, hy_w3, hy_bias, hg_lb, hg_norm, ssd_a_log, ssd_dt_bias, ssd_d, ssd_norm, gdn_a_log, gdn_dt_bias, gdn_norm, w_branch, w_out, router_w, router_bias, exp_w1, exp_w3, exp_w2, sh_w1, sh_w3, sh_w2, final_g):
    depth = w_in.shape[0]
    bp, lp_len = x_prompt.shape[0], x_prompt.shape[1]
    bs, ls_len = x_sample.shape[0], x_sample.shape[1]
    D = D_MODEL

    w_in_r = _reorder_w_in(w_in)
    hy_w1p = jnp.pad(hy_w1, ((0, 0), (0, LANES - hy_w1.shape[1]), (0, 0)))
    layers = []
    for l in range(depth):
        layers.append(dict(
            norm1_g=norm1_g[l], norm2_g=norm2_g[l], w_in=w_in_r, conv_w=conv_w[l], conv_b=conv_b[l],
            hy_bias=hy_bias[l], hg_lb=hg_lb, hg_norm=hg_norm[l],
            bias_row=_small_rows(ssd_dt_bias[l], gdn_dt_bias[l]),
            alog_row=_small_rows(ssd_a_log[l], gdn_a_log[l]),
            ssd_d=ssd_d[l], ssd_norm=ssd_norm[l], gdn_norm=gdn_norm[l],
            w_branch=w_branch[l].astype(BF16), w_out=w_out[l].astype(BF16),
            router_w=router_w[l], router_b=router_bias[l],
            exp_w1=exp_w1, exp_w3=exp_w3, exp_w2=exp_w2,
            sh_w1=sh_w1[l].astype(BF16), sh_w3=sh_w3[l].astype(BF16), sh_w2=sh_w2[l].astype(BF16)))

    def hyena_setup(L):
        fmat, fs = _dft_matrices(L)
        gmat = fs.T
        specs = []
        for l in range(depth):
            filt = _hyena_filters(L, hy_w1p[l], hy_b1[l], hy_w2[l], hy_b2[l], hy_w3[l])
            specs.append(_filter_spectrum(fmat, filt))
        return fmat, gmat, specs

    cond = jnp.concatenate([c_ctx.reshape(1, D), c], axis=0)
    rows = cond.shape[0]
    rows8 = (rows + 7) // 8 * 8
    cond8 = jnp.pad(cond, ((0, rows8 - rows), (0, 0)))
    mods = [_modulation(cond8, ada_w, ada_b[l], layer=l).reshape(rows8, 6, D) for l in range(depth)]

    fmat, gmat, specs = hyena_setup(lp_len)
    x = x_prompt.reshape(bp * lp_len, D)
    z_hg = jnp.zeros((bp, 2, HG_HEADS, HG_DK, HG_DK), F32)
    z_ssd = jnp.zeros((bp, 2, SSD_HEADS, SSD_P, SSD_N), F32)
    z_gdn = jnp.zeros((bp, 2, GDN_HEADS, GDN_DK, GDN_DK), F32)
    hg_states, ssd_states, gdn_states = [], [], []
    for l in range(depth):
        x, s_hg, s_ssd, s_gdn = _layer_pass(
            x, mods[l][0:1], layers[l], (fmat, gmat, specs[l]), z_hg, z_ssd, z_gdn, final_g,
            B=bp, L=lp_len, tokens_per_mod=bp * lp_len, seg=lp_len, layer=l, final_norm=(l == depth - 1))
        hg_states.append(s_hg)
        ssd_states.append(s_ssd)
        gdn_states.append(s_gdn)
    y_prompt = x.reshape(bp, lp_len, D)
    new_hg = jnp.stack(hg_states, axis=1)
    new_ssd = jnp.stack(ssd_states, axis=1)
    new_gdn = jnp.stack(gdn_states, axis=1)

    fmat, gmat, specs = hyena_setup(ls_len)
    x = x_sample.reshape(bs * ls_len, D)
    for l in range(depth):
        x, _, _, _ = _layer_pass(
            x, mods[l][1:1 + bs], layers[l], (fmat, gmat, specs[l]),
            state_hgrn[:, l], state_ssd[:, l], state_gdn[:, l], final_g,
            B=bs, L=ls_len, tokens_per_mod=ls_len, seg=GRID_W, layer=l, final_norm=(l == depth - 1))
    y_sample = x.reshape(bs, ls_len, D)
    return (y_prompt, y_sample, new_hg, new_ssd, new_gdn)
```

```python
import functools
import math

import numpy as np
import jax
import jax.numpy as jnp
from jax import lax
from jax.experimental import pallas as pl
from jax.experimental.pallas import tpu as pltpu

F32 = jnp.float32
BF16 = jnp.bfloat16
I32 = jnp.int32
HIGHEST = lax.Precision.HIGHEST

D_MODEL = 1024
GRID_W = 64
EPS = 1e-6
LOG_FLOOR = 1e-30
MASK_NEG = -1e30
N_BRANCH = 4
D_BRANCH = 512
HY_POS_FREQS = 16
HY_FILTER_HIDDEN = 64
HY_FAST_DECAY = 0.3
HY_SLOW_DECAY = 1.5
HY_DECAY_TARGET = 1e-2
HG_HEADS = 4
HG_DK = 128
HG_CHUNK = 16
SSD_HEADS = 8
SSD_P = 64
SSD_N = 128
SSD_CHUNK = 64
GDN_HEADS = 4
GDN_DK = 128
GDN_CHUNK = 64
N_EXPERTS = 64
TOP_K = 8
N_EXPERT_GROUPS = 8
TOPK_GROUPS = 4
D_EXPERT = 256
D_SHARED = 256
ROUTED_SCALE = 2.5

LANES = 128
RUN_ROWS = 8

D_CONV = 4096
C_HYV, C_HYX1, C_HYX2 = 0, 512, 1024
C_SSDX, C_SSDB, C_SSDC = 1536, 2048, 2304
C_GQ, C_GK, C_GV = 2560, 3072, 3584
P_MERGE = 4096
P_HGQ, P_HGFF, P_HGFB, P_HGV, P_HGG = 8192, 8704, 9216, 9728, 10240
P_SSDZ, P_GDNG, P_SMALL = 10752, 11264, 11776
D_PROJ = 12288
S_DT, S_GA, S_GB = 0, 16, 24

VMEM_LIMIT = 56 * 1024 * 1024


def _cparams(sem):
    return pltpu.CompilerParams(dimension_semantics=sem, vmem_limit_bytes=VMEM_LIMIT)


def _sigmoid(x):
    return 1.0 / (1.0 + jnp.exp(-x))


def _silu(x):
    return x * _sigmoid(x)


def _softplus(x):
    return jnp.maximum(x, 0.0) + jnp.log(1.0 + jnp.exp(-jnp.abs(x)))


def _dot(a, b, precision=None):
    return jnp.dot(a, b, preferred_element_type=F32, precision=precision)


def _dot_nt(a, b, precision=None):
    return lax.dot_general(a, b, (((1,), (1,)), ((), ())), preferred_element_type=F32, precision=precision)


def _dot_tn(a, b, precision=None):
    return lax.dot_general(a, b, (((0,), (0,)), ((), ())), preferred_element_type=F32, precision=precision)


def _bdot(a, b):
    return _dot(a.astype(BF16), b.astype(BF16))


def _bdot_nt(a, b):
    return _dot_nt(a.astype(BF16), b.astype(BF16))


def _bdot_tn(a, b):
    return _dot_tn(a.astype(BF16), b.astype(BF16))


def _iota(shape, dim):
    return lax.broadcasted_iota(I32, shape, dim)


def _cumsum_rows(g, reverse):
    n = g.shape[0]
    row = _iota(g.shape, 0)
    sh = 1
    while sh < n:
        if reverse:
            g = g + jnp.where(row < n - sh, pltpu.roll(g, n - sh, 0), 0.0)
        else:
            g = g + jnp.where(row >= sh, pltpu.roll(g, sh, 0), 0.0)
        sh *= 2
    return g


def _lane_pick(a, j):
    return jnp.sum(jnp.where(_iota(a.shape, 1) == j, a, 0.0), axis=1, keepdims=True)


def _split3(a):
    a1 = a.astype(BF16)
    r1 = a - a1.astype(F32)
    a2 = r1.astype(BF16)
    a3 = (r1 - a2.astype(F32)).astype(BF16)
    return a1, a2, a3


def _dot3(a, b):
    a1, a2, _ = _split3(a)
    b1, b2, _ = _split3(b)
    return _dot(a1, b1) + (_dot(a1, b2) + _dot(a2, b1))


def _row_pick(a, j):
    sel = (_iota((8, a.shape[1]), 1) == j).astype(BF16)
    a1, a2, a3 = _split3(a)
    return (_dot_nt(sel, a1) + (_dot_nt(sel, a2) + _dot_nt(sel, a3)))[0:1, :]


def _tri_mask(n, reverse, strict=False):
    t = _iota((n, n), 0)
    s = _iota((n, n), 1)
    if reverse:
        return (s > t) if strict else (s >= t)
    return (s < t) if strict else (s <= t)


def _masked_exp(mask, diff):
    return jnp.where(mask, jnp.exp(jnp.where(mask, diff, 0.0)), 0.0)


def _mod_kernel(c_ref, w_ref, b_ref, o_ref):
    o_ref[...] = _dot(_silu(c_ref[...]), w_ref[...], precision=HIGHEST) + b_ref[...]


def _modulation(cond8, ada_w, ada_b, *, layer):
    rows = cond8.shape[0]
    tn = 1536
    n = ada_w.shape[2]
    return pl.pallas_call(
        _mod_kernel,
        grid=(n // tn,),
        in_specs=[pl.BlockSpec((rows, D_MODEL), lambda j: (0, 0)),
                  pl.BlockSpec((None, D_MODEL, tn), lambda j: (layer, 0, j)),
                  pl.BlockSpec((1, tn), lambda j: (0, j))],
        out_specs=pl.BlockSpec((rows, tn), lambda j: (0, j)),
        out_shape=jax.ShapeDtypeStruct((rows, n), F32),
        compiler_params=_cparams(("arbitrary",)),
        name="adaln_mod",
    )(cond8, ada_w, ada_b.reshape(1, n))


def _inproj_kernel(x_ref, mod_ref, g_ref, w_ref, cw_ref, cb_ref, o_ref, h_scr, *, seg, n_conv_tiles):
    j = pl.program_id(1)

    @pl.when(j == 0)
    def _():
        x = x_ref[...]
        xn = x * lax.rsqrt(jnp.mean(x * x, axis=-1, keepdims=True) + EPS) * g_ref[...]
        m = mod_ref[...]
        h_scr[...] = (xn * (1.0 + m[1:2]) + m[0:1]).astype(BF16)

    y = _dot(h_scr[...], w_ref[...])

    @pl.when(j < n_conv_tiles)
    def _():
        tm = y.shape[0]
        pos = _iota(y.shape, 0) & (seg - 1)
        prev = jnp.where(pos == 0, 0.0, pltpu.roll(y, 1, 0))
        nxt = jnp.where(pos == seg - 1, 0.0, pltpu.roll(y, tm - 1, 0))
        cw = cw_ref[...]
        o_ref[...] = cb_ref[...] + prev * cw[0:1] + y * cw[1:2] + nxt * cw[2:3]

    @pl.when(j >= n_conv_tiles)
    def _():
        o_ref[...] = y


def _in_projection(x, mod, norm_g, w_bf16, conv_w, conv_b, *, layer, tokens_per_mod, seg):
    T = x.shape[0]
    tm = min(1024, tokens_per_mod)
    tn = 1024
    n_conv_tiles = D_CONV // tn
    tiles_per_mod = tokens_per_mod // tm
    kern = functools.partial(_inproj_kernel, seg=seg, n_conv_tiles=n_conv_tiles)
    cmap = lambda i, j: (0, jnp.minimum(j, n_conv_tiles - 1))
    return pl.pallas_call(
        kern,
        grid=(T // tm, D_PROJ // tn),
        in_specs=[pl.BlockSpec((tm, D_MODEL), lambda i, j: (i, 0)),
                  pl.BlockSpec((None, 6, D_MODEL), lambda i, j: (i // tiles_per_mod, 0, 0)),
                  pl.BlockSpec((1, D_MODEL), lambda i, j: (0, 0)),
                  pl.BlockSpec((None, D_MODEL, tn), lambda i, j: (layer, 0, j)),
                  pl.BlockSpec((3, tn), cmap),
                  pl.BlockSpec((1, tn), cmap)],
        out_specs=pl.BlockSpec((tm, tn), lambda i, j: (i, j)),
        out_shape=jax.ShapeDtypeStruct((T, D_PROJ), F32),
        scratch_shapes=[pltpu.VMEM((tm, D_MODEL), BF16)],
        compiler_params=_cparams(("arbitrary", "arbitrary")),
        name="in_proj",
    )(x, mod, norm_g.reshape(1, D_MODEL), w_bf16, conv_w, conv_b.reshape(1, D_CONV))


def _hyfilt_kernel(w1_ref, b1_ref, w2_ref, b2_ref, w3_ref, o_ref, *, L):
    i = pl.program_id(1)
    tl = o_ref.shape[0]
    t = (_iota((tl, LANES), 0) + i * tl).astype(F32) / L
    lane = _iota((tl, LANES), 1)
    band = jnp.where(lane <= HY_POS_FREQS, lane, lane - HY_POS_FREQS).astype(F32)
    ang = 2.0 * math.pi * t * band
    feats = jnp.where(lane == 0, t,
                      jnp.where(lane <= HY_POS_FREQS, jnp.sin(ang),
                                jnp.where(lane <= 2 * HY_POS_FREQS, jnp.cos(ang), 0.0)))
    hdn = jnp.sin(_dot(feats, w1_ref[...], precision=HIGHEST) + b1_ref[...])
    hdn = jnp.sin(_dot(hdn, w2_ref[...], precision=HIGHEST) + b2_ref[...])
    filt = _dot(hdn, w3_ref[...], precision=HIGHEST)
    max_decay = math.log(HY_DECAY_TARGET) / HY_FAST_DECAY
    min_decay = math.log(HY_DECAY_TARGET) / HY_SLOW_DECAY
    n = filt.shape[1]
    ch = (_iota((tl, n), 1) & (D_BRANCH - 1)).astype(F32)
    delta = min_decay + ch * ((max_decay - min_decay) / (D_BRANCH - 1))
    tt = (_iota((tl, n), 0) + i * tl).astype(F32) / L
    o_ref[...] = filt * jnp.exp(-tt * jnp.abs(delta))


def _hyena_filters(L, w1p, b1, w2, b2, w3):
    tl = min(L, 256)
    n = 2 * D_BRANCH
    return pl.pallas_call(
        functools.partial(_hyfilt_kernel, L=L),
        grid=(2, L // tl),
        in_specs=[pl.BlockSpec((LANES, HY_FILTER_HIDDEN), lambda d, i: (0, 0)),
                  pl.BlockSpec((1, HY_FILTER_HIDDEN), lambda d, i: (0, 0)),
                  pl.BlockSpec((HY_FILTER_HIDDEN, HY_FILTER_HIDDEN), lambda d, i: (0, 0)),
                  pl.BlockSpec((1, HY_FILTER_HIDDEN), lambda d, i: (0, 0)),
                  pl.BlockSpec((HY_FILTER_HIDDEN, n), lambda d, i: (0, d))],
        out_specs=pl.BlockSpec((None, tl, n), lambda d, i: (d, i, 0)),
        out_shape=jax.ShapeDtypeStruct((2, L, n), F32),
        compiler_params=_cparams(("arbitrary", "arbitrary")),
        name="hyena_filters",
    )(w1p, b1.reshape(1, -1), w2, b2.reshape(1, -1), w3)


def _dftgen_kernel(f_ref, fs_ref, *, L, tk):
    i = pl.program_id(0)
    N = 2 * L
    k = _iota((tk, LANES), 0) + i * tk
    lane = _iota((tk, LANES), 1)
    w = 2.0 * math.pi / N
    a0 = ((k * lane) & (N - 1)).astype(F32) * w
    c0, s0 = jnp.cos(a0), jnp.sin(a0)
    a1 = ((k * lane * LANES) & (N - 1)).astype(F32) * w
    c1, s1 = jnp.cos(a1), jnp.sin(a1)
    alt = jnp.where((lane & 1) == 0, 1.0, -1.0)
    coef = jnp.where(k == 0, 1.0 / N, 2.0 / N)
    for n1 in range(L // LANES):
        c1b = c1[:, n1:n1 + 1]
        s1b = s1[:, n1:n1 + 1]
        cosb = c1b * c0 - s1b * s0
        sinb = jnp.where(k == 0, alt, -(s1b * c0 + c1b * s0))
        cols = slice(n1 * LANES, (n1 + 1) * LANES)
        f_ref[0:tk, cols] = cosb.astype(BF16)
        f_ref[tk:2 * tk, cols] = sinb.astype(BF16)
        fs_ref[0:tk, cols] = (coef * cosb).astype(BF16)
        fs_ref[tk:2 * tk, cols] = (coef * sinb).astype(BF16)


def _dft_matrices(L):
    tk = min(L, 256)
    spec = pl.BlockSpec((2 * tk, L), lambda i: (i, 0))
    return pl.pallas_call(
        functools.partial(_dftgen_kernel, L=L, tk=tk),
        grid=(L // tk,),
        in_specs=[],
        out_specs=[spec, spec],
        out_shape=[jax.ShapeDtypeStruct((2 * L, L), BF16)] * 2,
        compiler_params=_cparams(("arbitrary",)),
        name="dft_matrices",
    )()


def _dfth_kernel(f_ref, h_ref, o_ref, hp_scr, *, tk):
    i = pl.program_id(1)
    n = o_ref.shape[1]

    @pl.when(i == 0)
    def _():
        hc = h_ref[0]
        ha = jnp.where(_iota(hc.shape, 0) == 0, 0.0, h_ref[1])
        hp_scr[:, 0:n] = (hc + ha).astype(BF16)
        hp_scr[:, n:2 * n] = (hc - ha).astype(BF16)

    u = _dot(f_ref[...], hp_scr[...])
    r = _iota((2 * tk, n), 0)
    from_sum = (r < tk) | ((r == tk) & (i == 0))
    o_ref[...] = jnp.where(from_sum, u[:, 0:n], u[:, n:2 * n])


def _filter_spectrum(fmat, filt):
    L = fmat.shape[1]
    tk = min(L, 256)
    C = filt.shape[2]
    tc = 256
    return pl.pallas_call(
        functools.partial(_dfth_kernel, tk=tk),
        grid=(C // tc, L // tk),
        in_specs=[pl.BlockSpec((2 * tk, L), lambda c, i: (i, 0)),
                  pl.BlockSpec((2, L, tc), lambda c, i: (0, 0, c))],
        out_specs=pl.BlockSpec((2 * tk, tc), lambda c, i: (i, c)),
        out_shape=jax.ShapeDtypeStruct((2 * L, C), F32),
        scratch_shapes=[pltpu.VMEM((L, 2 * tc), BF16)],
        compiler_params=_cparams(("arbitrary", "arbitrary")),
        name="filter_spectrum",
    )(fmat, filt)


def _dfta_kernel(f_ref, u_ref, h_ref, y_ref, u_scr, *, tk):
    i = pl.program_id(1)

    @pl.when(i == 0)
    def _():
        u_scr[...] = u_ref[...].astype(BF16)

    uf = _dot(f_ref[...], u_scr[...])
    ur, ui = uf[0:tk], uf[tk:2 * tk]
    hr, hi = h_ref[0:tk, :], h_ref[tk:2 * tk, :]
    dc = (_iota(ur.shape, 0) == 0) & (i == 0)
    y_ref[0:tk, :] = jnp.where(dc, ur * hr, ur * hr - ui * hi).astype(BF16)
    y_ref[tk:2 * tk, :] = jnp.where(dc, ui * hi, ur * hi + ui * hr).astype(BF16)


def _dftb_kernel(g_ref, y_ref, u_ref, x_ref, b_ref, o_ref):
    y = _dot(g_ref[...], y_ref[...])
    o_ref[...] = x_ref[...] * (y + u_ref[...] * b_ref[...])


def _spectral_conv(fmat, gmat, hspec, h_col, bias, u_arr, u_col, gate_arr, gate_col):
    B, L = u_arr.shape[0], u_arr.shape[1]
    C = D_BRANCH
    tk = min(L, 256)
    y = pl.pallas_call(
        functools.partial(_dfta_kernel, tk=tk),
        grid=(B, L // tk),
        in_specs=[pl.BlockSpec((2 * tk, L), lambda b, i: (i, 0)),
                  pl.BlockSpec((None, L, C), lambda b, i: (b, 0, u_col)),
                  pl.BlockSpec((2 * tk, C), lambda b, i: (i, h_col))],
        out_specs=pl.BlockSpec((None, 2 * tk, C), lambda b, i: (b, i, 0)),
        out_shape=jax.ShapeDtypeStruct((B, 2 * L, C), BF16),
        scratch_shapes=[pltpu.VMEM((L, C), BF16)],
        compiler_params=_cparams(("arbitrary", "arbitrary")),
        name="hyena_dft_fwd",
    )(fmat, u_arr, hspec)
    tr = min(L, 256)
    return pl.pallas_call(
        _dftb_kernel,
        grid=(B, L // tr),
        in_specs=[pl.BlockSpec((tr, 2 * L), lambda b, i: (i, 0)),
                  pl.BlockSpec((None, 2 * L, C), lambda b, i: (b, 0, 0)),
                  pl.BlockSpec((None, tr, C), lambda b, i: (b, i, u_col)),
                  pl.BlockSpec((None, tr, C), lambda b, i: (b, i, gate_col)),
                  pl.BlockSpec((1, C), lambda b, i: (0, 0))],
        out_specs=pl.BlockSpec((None, tr, C), lambda b, i: (b, i, 0)),
        out_shape=jax.ShapeDtypeStruct((B, L, C), F32),
        compiler_params=_cparams(("arbitrary", "arbitrary")),
        name="hyena_dft_inv",
    )(gmat, y, u_arr, gate_arr, bias.reshape(1, C))


def _hgrn_kernel(q_ref, ff_ref, fb_ref, v_ref, g_ref, lb_ref, nrm_ref, s0_ref, y_ref, sf_ref,
                 of_scr, ob_scr, st_scr, *, L, layer, depth):
    C = HG_CHUNK
    W = HG_DK
    nc = L // C
    nh = q_ref.shape[1] // W
    ridx = _iota((C, W), 0)
    o_scrs = (of_scr, ob_scr)
    f_refs = (ff_ref, fb_ref)

    def lower_bound(d, cols):
        rows = [lb_ref[d, l:l + 1, cols] for l in range(depth)]
        m = rows[0]
        for r in rows[1:]:
            m = jnp.maximum(m, r)
        es = [jnp.exp(r - m) for r in rows]
        tot = es[0]
        for e in es[1:]:
            tot = tot + e
        acc = es[0] / tot
        for e in es[1:layer + 1]:
            acc = acc + e / tot
        return acc - es[0] / tot

    lbs = [[lower_bound(d, slice(hh * W, (hh + 1) * W)) for hh in range(nh)] for d in (0, 1)]
    for d in (0, 1):
        for hh in range(nh):
            st_scr[d, hh] = s0_ref[d, hh].T

    def step(c, d, hh):
        rev = d == 1
        rows = pl.ds(pl.multiple_of(c * C, C), C)
        cols = slice(hh * W, (hh + 1) * W)
        lb = lbs[d][hh]
        q = _silu(q_ref[rows, cols])
        uf = f_refs[d][rows, cols]
        v = v_ref[rows, cols]
        f = lb + (1.0 - lb) * _sigmoid(uf)
        g = jnp.log(jnp.maximum(f, LOG_FLOOR))
        kin = (1.0 - lb) * _sigmoid(-uf)
        b = _cumsum_rows(g, rev)
        st = st_scr[d, hh]
        o = _bdot_nt(q * jnp.exp(b), st)
        intra = jnp.zeros((C, W), F32)
        for t in range(C):
            mask = (ridx >= t) if rev else (ridx <= t)
            pair = _masked_exp(mask, b[t:t + 1, :] - b)
            a = jnp.sum(pair * (q[t:t + 1, :] * kin), axis=1, keepdims=True)
            row = jnp.sum(a * v, axis=0, keepdims=True)
            intra = jnp.where(ridx == t, row, intra)
        b_end = b[0:1, :] if rev else b[C - 1:C, :]
        st_scr[d, hh] = st * jnp.exp(b_end) + _bdot_tn(v, kin * jnp.exp(b_end - b))
        o_scrs[d][rows, cols] = o + intra

    def body(ci, carry):
        for d in (0, 1):
            for hh in range(nh):
                step((nc - 1 - ci) if d == 1 else ci, d, hh)
        return carry

    lax.fori_loop(0, nc, body, 0)
    for d in (0, 1):
        for hh in range(nh):
            sf_ref[d, hh] = st_scr[d, hh].T

    tr = min(L, 256)

    def fin(i, carry):
        rows = pl.ds(pl.multiple_of(i * tr, tr), tr)
        for hh in range(nh):
            cols = slice(hh * W, (hh + 1) * W)
            o = of_scr[rows, cols] + ob_scr[rows, cols]
            on = o * lax.rsqrt(jnp.mean(o * o, axis=-1, keepdims=True) + EPS) * nrm_ref[...]
            y_ref[rows, cols] = on * _silu(g_ref[rows, cols])
        return carry

    lax.fori_loop(0, L // tr, fin, 0)


def _hgrn_scan(proj3, hg_lb, hg_norm, s0, *, layer):
    B, L = proj3.shape[0], proj3.shape[1]
    depth = hg_lb.shape[1]
    nh = 2
    W = nh * HG_DK
    col = lambda base: (lambda b, h: (b, 0, base // W + h))
    st_spec = pl.BlockSpec((None, 2, nh, HG_DK, HG_DK), lambda b, h: (b, 0, h, 0, 0))
    return pl.pallas_call(
        functools.partial(_hgrn_kernel, L=L, layer=layer, depth=depth),
        grid=(B, HG_HEADS // nh),
        in_specs=[pl.BlockSpec((None, L, W), col(P_HGQ)),
                  pl.BlockSpec((None, L, W), col(P_HGFF)),
                  pl.BlockSpec((None, L, W), col(P_HGFB)),
                  pl.BlockSpec((None, L, W), col(P_HGV)),
                  pl.BlockSpec((None, L, W), col(P_HGG)),
                  pl.BlockSpec((2, depth, W), lambda b, h: (0, 0, h)),
                  pl.BlockSpec((1, HG_DK), lambda b, h: (0, 0)),
                  st_spec],
        out_specs=[pl.BlockSpec((None, L, W), lambda b, h: (b, 0, h)), st_spec],
        out_shape=[jax.ShapeDtypeStruct((B, L, D_BRANCH), F32),
                   jax.ShapeDtypeStruct(s0.shape, F32)],
        scratch_shapes=[pltpu.VMEM((L, W), F32), pltpu.VMEM((L, W), F32),
                        pltpu.VMEM((2, nh, HG_DK, HG_DK), F32)],
        compiler_params=_cparams(("arbitrary", "arbitrary")),
        name="hgrn2_scan",
    )(proj3, proj3, proj3, proj3, proj3, hg_lb, hg_norm.reshape(1, HG_DK), s0)


def _ssd_kernel(x_ref, bm_ref, cm_ref, z_ref, sm_ref, bias_ref, alog_ref, dskip_ref, s0_ref,
                y_ref, sf_ref, of_scr, ob_scr, st_scr, *, L):
    C = SSD_CHUNK
    W = LANES
    nc = L // C
    npairs = x_ref.shape[1] // W
    grp = pl.program_id(1)
    lane_lo = _iota((C, W), 1) < SSD_P
    row_lo = _iota((W, SSD_N), 0) < SSD_P
    pick2 = lambda a0, a1: jnp.where(lane_lo, a0, a1)
    o_scrs = (of_scr, ob_scr)
    incls = (_tri_mask(C, False), _tri_mask(C, True))
    for d in (0, 1):
        for pp in range(npairs):
            st_scr[d, pp] = s0_ref[d, pp]

    heads = [(d, pp, hh) for d in (0, 1) for pp in range(npairs) for hh in (0, 1)]
    pairs = [(d, pp) for d in (0, 1) for pp in range(npairs)]

    def body(ci, carry):
        rows_d, bm, cm, dt_all, cum_all, gram = [], [], [], [], [], []
        for d in (0, 1):
            c = (nc - 1 - ci) if d == 1 else ci
            rows = pl.ds(pl.multiple_of(c * C, C), C)
            rows_d.append(rows)
            bm.append(_silu(bm_ref[rows, :]))
            cm.append(_silu(cm_ref[rows, :]))
            dt_all.append(_softplus(sm_ref[rows, :] + bias_ref[...]))
            cum_all.append(_cumsum_rows(-jnp.exp(alog_ref[...]) * dt_all[d], d == 1))
        lane_of = lambda u: S_DT + u[0] * SSD_HEADS + 2 * (npairs * grp + u[1]) + u[2]
        cumr = {u: _row_pick(cum_all[u[0]], lane_of(u)) for u in heads}
        for d in (0, 1):
            gram.append(_bdot_nt(cm[d], bm[d]))
        dtc = {u: _lane_pick(dt_all[u[0]], lane_of(u)) for u in heads}
        cumc = {u: _lane_pick(cum_all[u[0]], lane_of(u)) for u in heads}
        end = {u: (cumc[u][0:1, :] if u[0] == 1 else cumc[u][C - 1:C, :]) for u in heads}
        xs = {pr: _silu(x_ref[rows_d[pr[0]], slice(pr[1] * W, (pr[1] + 1) * W)]) for pr in pairs}
        st = {pr: st_scr[pr[0], pr[1]] for pr in pairs}
        y_state = {u: _bdot_nt(cm[u[0]] * jnp.exp(cumc[u]), st[u[:2]]) for u in heads}
        scores = {u: gram[u[0]] * _masked_exp(incls[u[0]], cumc[u] - cumr[u]) for u in heads}
        y_intra = {}
        for u in heads:
            head_x = jnp.where(lane_lo if u[2] == 0 else ~lane_lo, xs[u[:2]], 0.0) * dtc[u]
            y_intra[u] = _bdot(scores[u], head_x)
        for pr in pairs:
            u0, u1 = pr + (0,), pr + (1,)
            xdt_e = xs[pr] * pick2(dtc[u0] * jnp.exp(end[u0] - cumc[u0]), dtc[u1] * jnp.exp(end[u1] - cumc[u1]))
            st_scr[pr[0], pr[1]] = (st[pr] * jnp.where(row_lo, jnp.exp(end[u0]), jnp.exp(end[u1]))
                                    + _bdot_tn(xdt_e, bm[pr[0]]))
            o_scrs[pr[0]][rows_d[pr[0]], slice(pr[1] * W, (pr[1] + 1) * W)] = pick2(
                y_intra[u0] + y_state[u0], y_intra[u1] + y_state[u1])
        return carry

    lax.fori_loop(0, nc, body, 0)
    for d in (0, 1):
        for pp in range(npairs):
            sf_ref[d, pp] = st_scr[d, pp]

    tr = min(L, 256)

    def fin(i, carry):
        rows = pl.ds(pl.multiple_of(i * tr, tr), tr)
        y = of_scr[rows, :] + ob_scr[rows, :] + _silu(x_ref[rows, :]) * dskip_ref[...]
        y_ref[rows, :] = y * _silu(z_ref[rows, :])
        return carry

    lax.fori_loop(0, L // tr, fin, 0)


def _ssd_scan(proj3, bias_row, alog_row, ssd_d, s0):
    B, L = proj3.shape[0], proj3.shape[1]
    n_groups = 2
    n_pairs = SSD_HEADS // 2
    ppg = n_pairs // n_groups
    W = ppg * LANES
    s0p = s0.reshape(B, 2, n_pairs, 2 * SSD_P, SSD_N)
    dskip = jnp.repeat(ssd_d, SSD_P).reshape(n_groups, 1, W)
    col = lambda base: (lambda b, g: (b, 0, base // W + g))
    grp = lambda base: (lambda b, g: (b, 0, base // LANES + g))
    st_spec = pl.BlockSpec((None, 2, ppg, LANES, SSD_N), lambda b, g: (b, 0, g, 0, 0))
    y, sf = pl.pallas_call(
        functools.partial(_ssd_kernel, L=L),
        grid=(B, n_groups),
        in_specs=[pl.BlockSpec((None, L, W), col(C_SSDX)),
                  pl.BlockSpec((None, L, LANES), grp(C_SSDB)),
                  pl.BlockSpec((None, L, LANES), grp(C_SSDC)),
                  pl.BlockSpec((None, L, W), col(P_SSDZ)),
                  pl.BlockSpec((None, L, LANES), lambda b, g: (b, 0, P_SMALL // LANES)),
                  pl.BlockSpec((1, LANES), lambda b, g: (0, 0)),
                  pl.BlockSpec((1, LANES), lambda b, g: (0, 0)),
                  pl.BlockSpec((None, 1, W), lambda b, g: (g, 0, 0)),
                  st_spec],
        out_specs=[pl.BlockSpec((None, L, W), lambda b, g: (b, 0, g)), st_spec],
        out_shape=[jax.ShapeDtypeStruct((B, L, D_BRANCH), F32),
                   jax.ShapeDtypeStruct(s0p.shape, F32)],
        scratch_shapes=[pltpu.VMEM((L, W), F32), pltpu.VMEM((L, W), F32),
                        pltpu.VMEM((2, ppg, LANES, SSD_N), F32)],
        compiler_params=_cparams(("arbitrary", "arbitrary")),
        name="ssd_scan",
    )(proj3, proj3, proj3, proj3, proj3, bias_row, alog_row, dskip, s0p)
    return y, sf.reshape(s0.shape)


def _l2norm(a):
    return a * lax.rsqrt(jnp.sum(a * a, axis=-1, keepdims=True) + EPS)


def _gdn_kernel(q_ref, k_ref, v_ref, g_ref, sm_ref, bias_ref, alog_ref, nrm_ref, s0_ref,
                y_ref, sf_ref, of_scr, ob_scr, st_scr, *, L):
    C = GDN_CHUNK
    W = GDN_DK
    nc = L // C
    nh = q_ref.shape[1] // W
    hblk = pl.program_id(1)
    o_scrs = (of_scr, ob_scr)
    incls = (_tri_mask(C, False), _tri_mask(C, True))
    stricts = (_tri_mask(C, False, strict=True), _tri_mask(C, True, strict=True))
    for d in (0, 1):
        for hh in range(nh):
            st_scr[d, hh] = s0_ref[d, hh]

    units = [(d, hh) for d in (0, 1) for hh in range(nh)]
    eye = (_iota((C, C), 0) == _iota((C, C), 1)).astype(F32)

    def body(ci, carry):
        rows_d, cum_d, beta_d = [], [], []
        for d in (0, 1):
            c = (nc - 1 - ci) if d == 1 else ci
            rows = pl.ds(pl.multiple_of(c * C, C), C)
            raw = sm_ref[rows, :]
            rows_d.append(rows)
            cum_d.append(_cumsum_rows(-jnp.exp(alog_ref[...]) * _softplus(raw + bias_ref[...]), d == 1))
            beta_d.append(_sigmoid(raw))
        q, k, v, gc, grow, beta = {}, {}, {}, {}, {}, {}
        for u in units:
            d, hh = u
            cols = slice(hh * W, (hh + 1) * W)
            h = nh * hblk + hh
            q[u] = _l2norm(_silu(q_ref[rows_d[d], cols])) * (GDN_DK ** -0.5)
            k[u] = _l2norm(_silu(k_ref[rows_d[d], cols]))
            v[u] = _silu(v_ref[rows_d[d], cols])
            gc[u] = _lane_pick(cum_d[d], S_GA + d * GDN_HEADS + h)
            beta[u] = _lane_pick(beta_d[d], S_GB + d * GDN_HEADS + h)
        for u in units:
            grow[u] = _row_pick(cum_d[u[0]], S_GA + u[0] * GDN_HEADS + nh * hblk + u[1])
        decay = {u: _masked_exp(incls[u[0]], gc[u] - grow[u]) for u in units}
        kb = {u: k[u] * beta[u] for u in units}
        m = {u: -jnp.where(stricts[u[0]], _bdot_nt(kb[u], k[u]) * decay[u], 0.0) for u in units}
        aqk = {u: _bdot_nt(q[u], k[u]) * decay[u] for u in units}
        p = {u: eye + m[u] for u in units}
        sh = 2
        while sh < C:
            m = {u: _dot3(m[u], m[u]) for u in units}
            p = {u: p[u] + _dot3(p[u], m[u]) for u in units}
            sh *= 2
        uu = {u: _dot3(p[u], v[u] * beta[u]) for u in units}
        ww = {u: _dot3(p[u], kb[u] * jnp.exp(gc[u])) for u in units}
        st = {u: st_scr[u[0], u[1]] for u in units}
        v_new = {u: uu[u] - _bdot(ww[u], st[u]) for u in units}
        o = {u: _bdot(q[u] * jnp.exp(gc[u]), st[u]) + _bdot(aqk[u], v_new[u]) for u in units}
        for u in units:
            d, hh = u
            g_end = gc[u][0:1, :] if d == 1 else gc[u][C - 1:C, :]
            st_scr[d, hh] = st[u] * jnp.exp(g_end) + _bdot_tn(k[u] * jnp.exp(g_end - gc[u]), v_new[u])
            o_scrs[d][rows_d[d], slice(hh * W, (hh + 1) * W)] = o[u]
        return carry

    lax.fori_loop(0, nc, body, 0)
    for d in (0, 1):
        for hh in range(nh):
            sf_ref[d, hh] = st_scr[d, hh]

    tr = min(L, 256)

    def fin(i, carry):
        rows = pl.ds(pl.multiple_of(i * tr, tr), tr)
        for hh in range(nh):
            cols = slice(hh * W, (hh + 1) * W)
            o = of_scr[rows, cols] + ob_scr[rows, cols]
            on = o * lax.rsqrt(jnp.mean(o * o, axis=-1, keepdims=True) + EPS) * nrm_ref[...]
            y_ref[rows, cols] = on * _silu(g_ref[rows, cols])
        return carry

    lax.fori_loop(0, L // tr, fin, 0)


def _gdn_scan(proj3, bias_row, alog_row, gdn_norm, s0):
    B, L = proj3.shape[0], proj3.shape[1]
    nh = GDN_HEADS
    W = nh * GDN_DK
    col = lambda base: (lambda b, h: (b, 0, base // W + h))
    st_spec = pl.BlockSpec((None, 2, nh, GDN_DK, GDN_DK), lambda b, h: (b, 0, h, 0, 0))
    once = pl.Buffered(1)
    return pl.pallas_call(
        functools.partial(_gdn_kernel, L=L),
        grid=(B, GDN_HEADS // nh),
        in_specs=[pl.BlockSpec((None, L, W), col(C_GQ), pipeline_mode=once),
                  pl.BlockSpec((None, L, W), col(C_GK), pipeline_mode=once),
                  pl.BlockSpec((None, L, W), col(C_GV), pipeline_mode=once),
                  pl.BlockSpec((None, L, W), col(P_GDNG), pipeline_mode=once),
                  pl.BlockSpec((None, L, LANES), lambda b, h: (b, 0, P_SMALL // LANES), pipeline_mode=once),
                  pl.BlockSpec((1, LANES), lambda b, h: (0, 0)),
                  pl.BlockSpec((1, LANES), lambda b, h: (0, 0)),
                  pl.BlockSpec((1, GDN_DK), lambda b, h: (0, 0)),
                  st_spec],
        out_specs=[pl.BlockSpec((None, L, W), lambda b, h: (b, 0, h)), st_spec],
        out_shape=[jax.ShapeDtypeStruct((B, L, D_BRANCH), F32),
                   jax.ShapeDtypeStruct(s0.shape, F32)],
        scratch_shapes=[pltpu.VMEM((L, W), F32), pltpu.VMEM((L, W), F32),
                        pltpu.VMEM((2, nh, GDN_DK, GDN_DK), F32)],
        compiler_params=_cparams(("arbitrary", "arbitrary")),
        name="gdn_scan",
    )(proj3, proj3, proj3, proj3, proj3, bias_row, alog_row, gdn_norm.reshape(1, GDN_DK), s0)


def _merge_kernel(ya_ref, yb_ref, yc_ref, yd_ref, mg_ref, x_ref, mod_ref, sn_ref, n2_ref, wb_ref, wo_ref,
                  xo_ref, h_ref):
    yc = yc_ref[...]
    half = D_BRANCH // 2
    parts = []
    for gidx in range(2):
        seg = yc[:, gidx * half:(gidx + 1) * half]
        parts.append(seg * lax.rsqrt(jnp.mean(seg * seg, axis=-1, keepdims=True) + EPS)
                     * sn_ref[:, gidx * half:(gidx + 1) * half])
    branches = (ya_ref[...], yb_ref[...], None, yd_ref[...])
    mixed = None
    for n in range(N_BRANCH):
        gate = _sigmoid(mg_ref[:, n * D_MODEL:(n + 1) * D_MODEL])
        if n == 2:
            lifted = (_dot(parts[0].astype(BF16), wb_ref[n, 0:half, :])
                      + _dot(parts[1].astype(BF16), wb_ref[n, half:D_BRANCH, :]))
        else:
            lifted = _dot(branches[n].astype(BF16), wb_ref[n])
        mixed = gate * lifted if mixed is None else mixed + gate * lifted
    m = mod_ref[...]
    x = x_ref[...] + m[2:3] * _dot(mixed.astype(BF16), wo_ref[...])
    xo_ref[...] = x
    xn = x * lax.rsqrt(jnp.mean(x * x, axis=-1, keepdims=True) + EPS) * n2_ref[...]
    h_ref[...] = xn * (1.0 + m[4:5]) + m[3:4]


def _merge(ya, yb, yc, yd, proj, x, mod, ssd_norm, norm2_g, wb_bf16, wo_bf16, *, tokens_per_mod):
    T = x.shape[0]
    tm = 256
    tiles_per_mod = tokens_per_mod // tm
    yspec = pl.BlockSpec((tm, D_BRANCH), lambda i: (i, 0))
    xspec = pl.BlockSpec((tm, D_MODEL), lambda i: (i, 0))
    return pl.pallas_call(
        _merge_kernel,
        grid=(T // tm,),
        in_specs=[yspec, yspec, yspec, yspec,
                  pl.BlockSpec((tm, N_BRANCH * D_MODEL), lambda i: (i, P_MERGE // (N_BRANCH * D_MODEL))),
                  xspec,
                  pl.BlockSpec((None, 6, D_MODEL), lambda i: (i // tiles_per_mod, 0, 0)),
                  pl.BlockSpec((1, D_BRANCH), lambda i: (0, 0)),
                  pl.BlockSpec((1, D_MODEL), lambda i: (0, 0)),
                  pl.BlockSpec((N_BRANCH, D_BRANCH, D_MODEL), lambda i: (0, 0, 0)),
                  pl.BlockSpec((D_MODEL, D_MODEL), lambda i: (0, 0))],
        out_specs=[xspec, xspec],
        out_shape=[jax.ShapeDtypeStruct((T, D_MODEL), F32)] * 2,
        compiler_params=_cparams(("arbitrary",)),
        name="merge_outproj",
    )(ya, yb, yc, yd, proj, x, mod, ssd_norm.reshape(1, D_BRANCH), norm2_g.reshape(1, D_MODEL), wb_bf16, wo_bf16)


def _router_kernel(h_ref, rw_ref, rb_ref, wt_ref, lpos_ref, tlen_ref, tstart_ref, tcar_ref, carry):
    i = pl.program_id(0)
    tm = h_ref.shape[0]
    E = N_EXPERTS
    gsz = E // N_EXPERT_GROUPS
    neg_inf = -jnp.inf

    @pl.when(i == 0)
    def _():
        carry[...] = jnp.zeros_like(carry)

    scores = _sigmoid(_dot_nt(rw_ref[...], h_ref[...], precision=HIGHEST))
    biased = scores + rb_ref[...]
    eidx = _iota((E, tm), 0)
    ridx = _iota((gsz, tm), 0)

    slabs = [biased[g * gsz:(g + 1) * gsz, :] for g in range(N_EXPERT_GROUPS)]
    gs = []
    for v in slabs:
        m1 = jnp.max(v, axis=0, keepdims=True)
        i1 = jnp.min(jnp.where(v == m1, ridx, gsz), axis=0, keepdims=True)
        m2 = jnp.max(jnp.where(ridx == i1, neg_inf, v), axis=0, keepdims=True)
        gs.append(m1 + m2)
    masked = []
    for g in range(N_EXPERT_GROUPS):
        rank = jnp.zeros((1, tm), I32)
        for g2 in range(N_EXPERT_GROUPS):
            if g2 == g:
                continue
            ahead = (gs[g2] >= gs[g]) if g2 < g else (gs[g2] > gs[g])
            rank = rank + ahead.astype(I32)
        masked.append(jnp.where(rank < TOPK_GROUPS, slabs[g], MASK_NEG))
    cur = jnp.concatenate(masked, axis=0)

    krow = _iota((TOP_K, tm), 0)
    onehot = jnp.zeros((E, tm), F32)
    picks = []
    wsum = jnp.zeros((1, tm), F32)
    for kk in range(TOP_K):
        m = jnp.max(cur, axis=0, keepdims=True)
        ik = jnp.min(jnp.where(cur == m, eidx, E), axis=0, keepdims=True)
        hit = eidx == ik
        wk = jnp.sum(jnp.where(hit, scores, 0.0), axis=0, keepdims=True)
        cur = jnp.where(hit, neg_inf, cur)
        onehot = onehot + hit.astype(F32)
        picks.append((ik, wk))
        wsum = wsum + wk
    wt_out = jnp.zeros((TOP_K, tm), F32)
    for kk, (ik, wk) in enumerate(picks):
        wt_out = jnp.where(krow == kk, wk / wsum * ROUTED_SCALE, wt_out)

    earlier = (_iota((tm, tm), 0) < _iota((tm, tm), 1)).astype(BF16)
    before = _dot(onehot.astype(BF16), earlier)
    pad = lambda c: (c.astype(I32) + (RUN_ROWS - 1)) // RUN_ROWS * RUN_ROWS
    run_len = pad(jnp.sum(onehot, axis=1, keepdims=True) + jnp.zeros((E, LANES), F32)).astype(F32)
    run_start = _cumsum_rows(run_len, False) - run_len
    local = before + run_start[:, 0:1]
    lpos_out = jnp.zeros((TOP_K, tm), I32)
    for kk, (ik, wk) in enumerate(picks):
        lp = jnp.sum(jnp.where(eidx == ik, local, 0.0), axis=0, keepdims=True)
        lpos_out = jnp.where(krow == kk, lp.astype(I32), lpos_out)
    wt_ref[...] = wt_out
    lpos_ref[...] = lpos_out

    onehot_l = jnp.concatenate([onehot, jnp.zeros((LANES - E, tm), F32)], axis=0).astype(BF16)
    len_row = pad(_dot_nt(jnp.ones((8, tm), BF16), onehot_l)).astype(F32)
    lanes_before = (_iota((LANES, LANES), 0) < _iota((LANES, LANES), 1)).astype(BF16)
    tlen_ref[...] = len_row.astype(I32)
    tstart_ref[...] = _dot(len_row.astype(BF16), lanes_before).astype(I32)
    tcar_ref[...] = carry[...].astype(I32)
    carry[...] = carry[...] + len_row


def _router(h2, router_w, router_b, *, tm):
    T = h2.shape[0]
    nt = T // tm
    ospec = pl.BlockSpec((TOP_K, tm), lambda i: (0, i))
    tspec = pl.BlockSpec((None, 8, LANES), lambda i: (i, 0, 0))
    tshape = jax.ShapeDtypeStruct((nt, 8, LANES), I32)
    wt, lpos, tlen, tstart, tcar = pl.pallas_call(
        _router_kernel,
        grid=(nt,),
        in_specs=[pl.BlockSpec((tm, D_MODEL), lambda i: (i, 0)),
                  pl.BlockSpec((N_EXPERTS, D_MODEL), lambda i: (0, 0)),
                  pl.BlockSpec((N_EXPERTS, 1), lambda i: (0, 0))],
        out_specs=[ospec, ospec, tspec, tspec, tspec],
        out_shape=[jax.ShapeDtypeStruct((TOP_K, T), F32), jax.ShapeDtypeStruct((TOP_K, T), I32),
                   tshape, tshape, tshape],
        scratch_shapes=[pltpu.VMEM((8, LANES), F32)],
        compiler_params=_cparams(("arbitrary",)),
        name="moe_router",
    )(h2, router_w.T, router_b.reshape(N_EXPERTS, 1))
    table = lambda a: a[:, 0, :N_EXPERTS]
    return wt, lpos, table(tlen), table(tstart), table(tcar)


def _dispatch_kernel(len_ref, lst_ref, car_ref, seg_ref, lpos_ref, h_ref, xs_ref, xloc, sem):
    i = pl.program_id(0)
    tm = h_ref.shape[0]
    nloc = xloc.shape[0]
    R = RUN_ROWS
    lpos = lpos_ref[...]
    hb = h_ref[...].astype(BF16)
    rb = 256
    for blk in range(nloc // rb):
        r = _iota((rb, tm), 0) + blk * rb
        hit = r == lpos[0:1, :]
        for kk in range(1, TOP_K):
            hit = hit | (r == lpos[kk:kk + 1, :])
        xloc[blk * rb:(blk + 1) * rb, :] = _dot(hit.astype(BF16), hb)

    def piece(src_row, dst_row):
        return pltpu.make_async_copy(xloc.at[pl.ds(pl.multiple_of(src_row, R), R), :],
                                     xs_ref.at[pl.ds(pl.multiple_of(dst_row, R), R), :], sem)

    def per_expert(e, issued):
        pieces = len_ref[i, e] // R
        ls = lst_ref[i, e]
        gd = seg_ref[e] + car_ref[i, e]

        def issue(j, c):
            piece(ls + j * R, gd + j * R).start()
            return c

        lax.fori_loop(0, pieces, issue, 0)
        return issued + pieces

    total = lax.fori_loop(0, N_EXPERTS, per_expert, 0)

    def wait(j, c):
        piece(0, 0).wait()
        return c

    lax.fori_loop(0, total, wait, 0)


def _dispatch(h2, lpos, tile_len, tile_lstart, tile_car, seg_start, *, tm):
    T = h2.shape[0]
    nt = T // tm
    n_rows = T * TOP_K + N_EXPERTS * nt * RUN_ROWS
    nloc = tm * TOP_K + N_EXPERTS * RUN_ROWS
    return pl.pallas_call(
        _dispatch_kernel,
        grid_spec=pltpu.PrefetchScalarGridSpec(
            num_scalar_prefetch=4,
            grid=(nt,),
            in_specs=[pl.BlockSpec((TOP_K, tm), lambda i, *_: (0, i)),
                      pl.BlockSpec((tm, D_MODEL), lambda i, *_: (i, 0))],
            out_specs=pl.BlockSpec(memory_space=pl.ANY),
            scratch_shapes=[pltpu.VMEM((nloc, D_MODEL), F32), pltpu.SemaphoreType.DMA(())]),
        out_shape=jax.ShapeDtypeStruct((n_rows, D_MODEL), F32),
        compiler_params=_cparams(("arbitrary",)),
        name="moe_dispatch",
    )(tile_len, tile_lstart, tile_car, seg_start, lpos, h2)


def _expert_kernel(blk_ref, exp_ref, lo_ref, hi_ref, first_ref, x_ref, w1_ref, w3_ref, w2_ref, o_ref):
    w = pl.program_id(0)
    x = x_ref[...].astype(BF16)
    a = _dot(x, w1_ref[...].astype(BF16))
    b = _dot(x, w3_ref[...].astype(BF16))
    y = _dot((_silu(a) * b).astype(BF16), w2_ref[...].astype(BF16))
    r = _iota(y.shape, 0)
    y = jnp.where((r >= lo_ref[w]) & (r < hi_ref[w]), y, 0.0)

    @pl.when(first_ref[w] == 1)
    def _():
        o_ref[...] = y

    @pl.when(first_ref[w] == 0)
    def _():
        o_ref[...] = o_ref[...] + y


def _experts(xs, work, w1, w3, w2, *, layer, bm):
    A = xs.shape[0]
    n_work = work[0].shape[0]
    xmap = lambda w, blk, ex, lo, hi, first: (blk[w], 0)
    wmap = lambda w, blk, ex, lo, hi, first: (layer, ex[w], 0, 0)
    return pl.pallas_call(
        _expert_kernel,
        grid_spec=pltpu.PrefetchScalarGridSpec(
            num_scalar_prefetch=5,
            grid=(n_work,),
            in_specs=[pl.BlockSpec((bm, D_MODEL), xmap),
                      pl.BlockSpec((None, None, D_MODEL, D_EXPERT), wmap),
                      pl.BlockSpec((None, None, D_MODEL, D_EXPERT), wmap),
                      pl.BlockSpec((None, None, D_EXPERT, D_MODEL), wmap)],
            out_specs=pl.BlockSpec((bm, D_MODEL), xmap)),
        out_shape=jax.ShapeDtypeStruct((A, D_MODEL), F32),
        compiler_params=_cparams(("arbitrary",)),
        name="moe_experts",
    )(*work, xs, w1, w3, w2)


def _combine_kernel(len_ref, lst_ref, car_ref, seg_ref, lpos_ref, wt_ref, ys_ref, h_ref, x_ref, mod_ref,
                    s1_ref, s3_ref, s2_ref, fg_ref, o_ref, yloc, sem, *, final_norm):
    i = pl.program_id(0)
    tm = h_ref.shape[0]
    nloc = yloc.shape[0]
    R = RUN_ROWS

    @pl.when(i == 0)
    def _():
        yloc[...] = jnp.zeros_like(yloc)

    def piece(src_row, dst_row):
        return pltpu.make_async_copy(ys_ref.at[pl.ds(pl.multiple_of(src_row, R), R), :],
                                     yloc.at[pl.ds(pl.multiple_of(dst_row, R), R), :], sem)

    def per_expert(e, issued):
        pieces = len_ref[i, e] // R
        ls = lst_ref[i, e]
        gd = seg_ref[e] + car_ref[i, e]

        def issue(j, c):
            piece(gd + j * R, ls + j * R).start()
            return c

        lax.fori_loop(0, pieces, issue, 0)
        return issued + pieces

    total = lax.fori_loop(0, N_EXPERTS, per_expert, 0)

    hb = h_ref[...].astype(BF16)
    mid = _silu(_dot(hb, s1_ref[...])) * _dot(hb, s3_ref[...])
    y = _dot(mid.astype(BF16), s2_ref[...])

    def wait(j, c):
        piece(0, 0).wait()
        return c

    lax.fori_loop(0, total, wait, 0)

    lpos = lpos_ref[...]
    wt = wt_ref[...]
    used_rows = lst_ref[i, N_EXPERTS - 1] + len_ref[i, N_EXPERTS - 1]
    rb = 256

    def block_sum(blk):
        r = _iota((rb, tm), 0) + blk * rb
        pw = jnp.zeros((rb, tm), F32)
        for kk in range(TOP_K):
            pw = jnp.where(r == lpos[kk:kk + 1, :], wt[kk:kk + 1, :], pw)
        ysc = yloc[blk * rb:(blk + 1) * rb, :] * jnp.sum(pw, axis=1, keepdims=True)
        hi = ysc.astype(BF16)
        lo = (ysc - hi.astype(F32)).astype(BF16)
        owner = (pw != 0.0).astype(BF16)
        return _dot_tn(owner, hi) + _dot_tn(owner, lo)

    for blk in range(nloc // rb):
        if (blk + 1) * rb <= tm * TOP_K:
            y = y + block_sum(blk)
        else:
            y = y + lax.cond(used_rows > blk * rb, functools.partial(block_sum, blk),
                             lambda: jnp.zeros((tm, D_MODEL), F32))
    x = x_ref[...] + mod_ref[5:6, :] * y
    if final_norm:
        x = x * lax.rsqrt(jnp.mean(x * x, axis=-1, keepdims=True) + EPS) * fg_ref[...]
    o_ref[...] = x


def _combine(ys, lpos, wt, tile_len, tile_lstart, tile_car, seg_start, h2, x, mod, s1, s3, s2, final_g,
             *, tm, tokens_per_mod, final_norm):
    T = x.shape[0]
    tiles_per_mod = tokens_per_mod // tm
    nloc = tm * TOP_K + N_EXPERTS * RUN_ROWS
    xspec = pl.BlockSpec((tm, D_MODEL), lambda i, *_: (i, 0))
    kspec = pl.BlockSpec((TOP_K, tm), lambda i, *_: (0, i))
    const = lambda i, *_: (0, 0)
    return pl.pallas_call(
        functools.partial(_combine_kernel, final_norm=final_norm),
        grid_spec=pltpu.PrefetchScalarGridSpec(
            num_scalar_prefetch=4,
            grid=(T // tm,),
            in_specs=[kspec, kspec,
                      pl.BlockSpec(memory_space=pl.ANY),
                      xspec, xspec,
                      pl.BlockSpec((None, 6, D_MODEL), lambda i, *_: (i // tiles_per_mod, 0, 0)),
                      pl.BlockSpec((D_MODEL, D_SHARED), const),
                      pl.BlockSpec((D_MODEL, D_SHARED), const),
                      pl.BlockSpec((D_SHARED, D_MODEL), const),
                      pl.BlockSpec((1, D_MODEL), const)],
            out_specs=xspec,
            scratch_shapes=[pltpu.VMEM((nloc, D_MODEL), F32), pltpu.SemaphoreType.DMA(())]),
        out_shape=jax.ShapeDtypeStruct((T, D_MODEL), F32),
        compiler_params=_cparams(("arbitrary",)),
        name="moe_combine",
    )(tile_len, tile_lstart, tile_car, seg_start, lpos, wt, ys, h2, x, mod, s1, s3, s2, final_g.reshape(1, D_MODEL))


def _work_list(counts, starts, n_rows, bm):
    E = N_EXPERTS
    n_blocks = n_rows // bm
    n_work = n_blocks + E - 1
    ends = starts + counts
    first_blk = starts // bm
    last_blk = jnp.where(counts > 0, (ends - 1) // bm, first_blk)
    n_items = jnp.where(counts > 0, last_blk - first_blk + 1, 0)
    item_end = jnp.cumsum(n_items)
    item_start = item_end - n_items
    w = jnp.arange(n_work, dtype=I32)
    ex = jnp.minimum(jnp.sum((item_end[None, :] <= w[:, None]).astype(I32), axis=1), E - 1)
    valid = w < item_end[-1]
    blk = first_blk[ex] + (w - item_start[ex])
    blk = jnp.where(valid, blk, blk[jnp.maximum(item_end[-1] - 1, 0)]).astype(I32)
    lo = jnp.clip(starts[ex] - blk * bm, 0, bm)
    hi = jnp.clip(ends[ex] - blk * bm, 0, bm)
    lo = jnp.where(valid, lo, 0).astype(I32)
    hi = jnp.where(valid, hi, 0).astype(I32)
    ex = jnp.where(valid, ex, ex[jnp.maximum(item_end[-1] - 1, 0)])
    prev_blk = jnp.concatenate([jnp.full((1,), -1, I32), blk[:-1]])
    first = (blk != prev_blk).astype(I32)
    return blk, ex.astype(I32), lo, hi, first


def _moe(h2, x, mod, p, final_g, *, layer, tokens_per_mod, final_norm):
    T = x.shape[0]
    tm = 256
    wt, lpos, tile_len, tile_lstart, tile_car = _router(h2, p["router_w"], p["router_b"], tm=tm)
    seg_len = tile_car[-1] + tile_len[-1]
    seg_start = jnp.cumsum(seg_len) - seg_len
    xs = _dispatch(h2, lpos, tile_len, tile_lstart, tile_car, seg_start, tm=tm)
    bm = 512
    work = _work_list(seg_len, seg_start, xs.shape[0], bm)
    ys = _experts(xs, work, p["exp_w1"], p["exp_w3"], p["exp_w2"], layer=layer, bm=bm)
    return _combine(ys, lpos, wt, tile_len, tile_lstart, tile_car, seg_start, h2, x, mod,
                    p["sh_w1"], p["sh_w3"], p["sh_w2"], final_g,
                    tm=tm, tokens_per_mod=tokens_per_mod, final_norm=final_norm)


def _reorder_w_in(w_in):
    plain = D_CONV
    pieces = [w_in[..., :plain],
              w_in[..., plain + 3616:plain + 3616 + 4096],
              w_in[..., plain:plain + 3072],
              w_in[..., plain + 3088:plain + 3600],
              w_in[..., plain + 3072:plain + 3088],
              w_in[..., plain + 3600:plain + 3616]]
    out = jnp.concatenate(pieces, axis=-1)
    pad = D_PROJ - out.shape[-1]
    return jnp.pad(out, ((0, 0), (0, 0), (0, pad))).astype(BF16)


def _small_rows(ssd_vals, gdn_vals):
    row = jnp.zeros((LANES,), F32)
    row = row.at[S_DT:S_DT + 2 * SSD_HEADS].set(ssd_vals.reshape(-1))
    row = row.at[S_GA:S_GA + 2 * GDN_HEADS].set(gdn_vals.reshape(-1))
    return row.reshape(1, LANES)


def _layer_pass(x, mod, lp, hy, st_hg, st_ssd, st_gdn, final_g, *, B, L, tokens_per_mod, seg, layer, final_norm):
    T = B * L
    proj = _in_projection(x, mod, lp["norm1_g"], lp["w_in"], lp["conv_w"], lp["conv_b"],
                          layer=layer, tokens_per_mod=tokens_per_mod, seg=seg)
    proj3 = proj.reshape(B, L, D_PROJ)
    fmat, gmat, hspec = hy
    cb = lambda c: c // D_BRANCH
    z = _spectral_conv(fmat, gmat, hspec, 0, lp["hy_bias"][0], proj3, cb(C_HYV), proj3, cb(C_HYX1))
    ya = _spectral_conv(fmat, gmat, hspec, 1, lp["hy_bias"][1], z, 0, proj3, cb(C_HYX2))
    yb, s_hg = _hgrn_scan(proj3, lp["hg_lb"], lp["hg_norm"], st_hg, layer=layer)
    yc, s_ssd = _ssd_scan(proj3, lp["bias_row"], lp["alog_row"], lp["ssd_d"], st_ssd)
    yd, s_gdn = _gdn_scan(proj3, lp["bias_row"], lp["alog_row"], lp["gdn_norm"], st_gdn)
    flat = lambda a: a.reshape(T, D_BRANCH)
    x, h2 = _merge(flat(ya), flat(yb), flat(yc), flat(yd), proj, x, mod, lp["ssd_norm"], lp["norm2_g"],
                   lp["w_branch"], lp["w_out"], tokens_per_mod=tokens_per_mod)
    x = _moe(h2, x, mod, lp, final_g, layer=layer, tokens_per_mod=tokens_per_mod, final_norm=final_norm)
    return x, s_hg, s_ssd, s_gdn


def kernel(x_prompt, x_sample, state_hgrn, state_ssd, state_gdn, c, c_ctx, norm1_g, norm2_g, ada_w, ada_b, w_in, conv_w, conv_b, hy_w1, hy_b1, hy_w2, hy_b2, hy_w3, hy_bias, hg_lb, hg_norm, ssd_a_log, ssd_dt_bias, ssd_d, ssd_norm, gdn_a_log, gdn_dt_bias, gdn_norm, w_branch, w_out, router_w, router_bias, exp_w1, exp_w3, exp_w2, sh_w1, sh_w3, sh_w2, final_g):
    depth = w_in.shape[0]
    bp, lp_len = x_prompt.shape[0], x_prompt.shape[1]
    bs, ls_len = x_sample.shape[0], x_sample.shape[1]
    D = D_MODEL

    w_in_r = _reorder_w_in(w_in)
    hy_w1p = jnp.pad(hy_w1, ((0, 0), (0, LANES - hy_w1.shape[1]), (0, 0)))
    layers = []
    for l in range(depth):
        layers.append(dict(
            norm1_g=norm1_g[l], norm2_g=norm2_g[l], w_in=w_in_r, conv_w=conv_w[l], conv_b=conv_b[l],
            hy_bias=hy_bias[l], hg_lb=hg_lb, hg_norm=hg_norm[l],
            bias_row=_small_rows(ssd_dt_bias[l], gdn_dt_bias[l]),
            alog_row=_small_rows(ssd_a_log[l], gdn_a_log[l]),
            ssd_d=ssd_d[l], ssd_norm=ssd_norm[l], gdn_norm=gdn_norm[l],
            w_branch=w_branch[l].astype(BF16), w_out=w_out[l].astype(BF16),
            router_w=router_w[l], router_b=router_bias[l],
            exp_w1=exp_w1, exp_w3=exp_w3, exp_w2=exp_w2,
            sh_w1=sh_w1[l].astype(BF16), sh_w3=sh_w3[l].astype(BF16), sh_w2=sh_w2[l].astype(BF16)))

    def hyena_setup(L):
        fmat, fs = _dft_matrices(L)
        gmat = fs.T
        specs = []
        for l in range(depth):
            filt = _hyena_filters(L, hy_w1p[l], hy_b1[l], hy_w2[l], hy_b2[l], hy_w3[l])
            specs.append(_filter_spectrum(fmat, filt))
        return fmat, gmat, specs

    cond = jnp.concatenate([c_ctx.reshape(1, D), c], axis=0)
    rows = cond.shape[0]
    rows8 = (rows + 7) // 8 * 8
    cond8 = jnp.pad(cond, ((0, rows8 - rows), (0, 0)))
    mods = [_modulation(cond8, ada_w, ada_b[l], layer=l).reshape(rows8, 6, D) for l in range(depth)]

    fmat, gmat, specs = hyena_setup(lp_len)
    x = x_prompt.reshape(bp * lp_len, D)
    z_hg = jnp.zeros((bp, 2, HG_HEADS, HG_DK, HG_DK), F32)
    z_ssd = jnp.zeros((bp, 2, SSD_HEADS, SSD_P, SSD_N), F32)
    z_gdn = jnp.zeros((bp, 2, GDN_HEADS, GDN_DK, GDN_DK), F32)
    hg_states, ssd_states, gdn_states = [], [], []
    for l in range(depth):
        x, s_hg, s_ssd, s_gdn = _layer_pass(
            x, mods[l][0:1], layers[l], (fmat, gmat, specs[l]), z_hg, z_ssd, z_gdn, final_g,
            B=bp, L=lp_len, tokens_per_mod=bp * lp_len, seg=lp_len, layer=l, final_norm=(l == depth - 1))
        hg_states.append(s_hg)
        ssd_states.append(s_ssd)
        gdn_states.append(s_gdn)
    y_prompt = x.reshape(bp, lp_len, D)
    new_hg = jnp.stack(hg_states, axis=1)
    new_ssd = jnp.stack(ssd_states, axis=1)
    new_gdn = jnp.stack(gdn_states, axis=1)

    fmat, gmat, specs = hyena_setup(ls_len)
    x = x_sample.reshape(bs * ls_len, D)
    for l in range(depth):
        x, _, _, _ = _layer_pass(
            x, mods[l][1:1 + bs], layers[l], (fmat, gmat, specs[l]),
            state_hgrn[:, l], state_ssd[:, l], state_gdn[:, l], final_g,
            B=bs, L=ls_len, tokens_per_mod=ls_len, seg=GRID_W, layer=l, final_norm=(l == depth - 1))
    y_sample = x.reshape(bs, ls_len, D)
    return (y_prompt, y_sample, new_hg, new_ssd, new_gdn)
```

```python
import functools
import math

import numpy as np
import jax
import jax.numpy as jnp
from jax import lax
from jax.experimental import pallas as pl
from jax.experimental.pallas import tpu as pltpu

F32 = jnp.float32
BF16 = jnp.bfloat16
I32 = jnp.int32
HIGHEST = lax.Precision.HIGHEST

D_MODEL = 1024
GRID_W = 64
EPS = 1e-6
LOG_FLOOR = 1e-30
MASK_NEG = -1e30
N_BRANCH = 4
D_BRANCH = 512
HY_POS_FREQS = 16
HY_FILTER_HIDDEN = 64
HY_FAST_DECAY = 0.3
HY_SLOW_DECAY = 1.5
HY_DECAY_TARGET = 1e-2
HG_HEADS = 4
HG_DK = 128
HG_CHUNK = 16
SSD_HEADS = 8
SSD_P = 64
SSD_N = 128
SSD_CHUNK = 64
GDN_HEADS = 4
GDN_DK = 128
GDN_CHUNK = 64
N_EXPERTS = 64
TOP_K = 8
N_EXPERT_GROUPS = 8
TOPK_GROUPS = 4
D_EXPERT = 256
D_SHARED = 256
ROUTED_SCALE = 2.5

LANES = 128
RUN_ROWS = 8

D_CONV = 4096
C_HYV, C_HYX1, C_HYX2 = 0, 512, 1024
C_SSDX, C_SSDB, C_SSDC = 1536, 2048, 2304
C_GQ, C_GK, C_GV = 2560, 3072, 3584
P_MERGE = 4096
P_HGQ, P_HGFF, P_HGFB, P_HGV, P_HGG = 8192, 8704, 9216, 9728, 10240
P_SSDZ, P_GDNG, P_SMALL = 10752, 11264, 11776
D_PROJ = 12288
S_DT, S_GA, S_GB = 0, 16, 24

VMEM_LIMIT = 56 * 1024 * 1024


def _cparams(sem):
    return pltpu.CompilerParams(dimension_semantics=sem, vmem_limit_bytes=VMEM_LIMIT)


def _sigmoid(x):
    return 1.0 / (1.0 + jnp.exp(-x))


def _silu(x):
    return x * _sigmoid(x)


def _softplus(x):
    return jnp.maximum(x, 0.0) + jnp.log(1.0 + jnp.exp(-jnp.abs(x)))


def _dot(a, b, precision=None):
    return jnp.dot(a, b, preferred_element_type=F32, precision=precision)


def _dot_nt(a, b, precision=None):
    return lax.dot_general(a, b, (((1,), (1,)), ((), ())), preferred_element_type=F32, precision=precision)


def _dot_tn(a, b, precision=None):
    return lax.dot_general(a, b, (((0,), (0,)), ((), ())), preferred_element_type=F32, precision=precision)


def _bdot(a, b):
    return _dot(a.astype(BF16), b.astype(BF16))


def _bdot_nt(a, b):
    return _dot_nt(a.astype(BF16), b.astype(BF16))


def _bdot_tn(a, b):
    return _dot_tn(a.astype(BF16), b.astype(BF16))


def _iota(shape, dim):
    return lax.broadcasted_iota(I32, shape, dim)


def _cumsum_rows(g, reverse):
    n = g.shape[0]
    row = _iota(g.shape, 0)
    sh = 1
    while sh < n:
        if reverse:
            g = g + jnp.where(row < n - sh, pltpu.roll(g, n - sh, 0), 0.0)
        else:
            g = g + jnp.where(row >= sh, pltpu.roll(g, sh, 0), 0.0)
        sh *= 2
    return g


def _lane_pick(a, j):
    return jnp.sum(jnp.where(_iota(a.shape, 1) == j, a, 0.0), axis=1, keepdims=True)


def _split3(a):
    a1 = a.astype(BF16)
    r1 = a - a1.astype(F32)
    a2 = r1.astype(BF16)
    a3 = (r1 - a2.astype(F32)).astype(BF16)
    return a1, a2, a3


def _dot3(a, b):
    a1, a2, _ = _split3(a)
    b1, b2, _ = _split3(b)
    return _dot(a1, b1) + (_dot(a1, b2) + _dot(a2, b1))


def _row_pick(a, j):
    sel = (_iota((8, a.shape[1]), 1) == j).astype(BF16)
    a1, a2, a3 = _split3(a)
    return (_dot_nt(sel, a1) + (_dot_nt(sel, a2) + _dot_nt(sel, a3)))[0:1, :]


def _tri_mask(n, reverse, strict=False):
    t = _iota((n, n), 0)
    s = _iota((n, n), 1)
    if reverse:
        return (s > t) if strict else (s >= t)
    return (s < t) if strict else (s <= t)


def _masked_exp(mask, diff):
    return jnp.where(mask, jnp.exp(jnp.where(mask, diff, 0.0)), 0.0)


def _mod_kernel(c_ref, w_ref, b_ref, o_ref):
    o_ref[...] = _dot(_silu(c_ref[...]), w_ref[...], precision=HIGHEST) + b_ref[...]


def _modulation(cond8, ada_w, ada_b, *, layer):
    rows = cond8.shape[0]
    tn = 1536
    n = ada_w.shape[2]
    return pl.pallas_call(
        _mod_kernel,
        grid=(n // tn,),
        in_specs=[pl.BlockSpec((rows, D_MODEL), lambda j: (0, 0)),
                  pl.BlockSpec((None, D_MODEL, tn), lambda j: (layer, 0, j)),
                  pl.BlockSpec((1, tn), lambda j: (0, j))],
        out_specs=pl.BlockSpec((rows, tn), lambda j: (0, j)),
        out_shape=jax.ShapeDtypeStruct((rows, n), F32),
        compiler_params=_cparams(("arbitrary",)),
        name="adaln_mod",
    )(cond8, ada_w, ada_b.reshape(1, n))


def _inproj_kernel(x_ref, mod_ref, g_ref, w_ref, cw_ref, cb_ref, o_ref, h_scr, *, seg, n_conv_tiles):
    j = pl.program_id(1)

    @pl.when(j == 0)
    def _():
        x = x_ref[...]
        xn = x * lax.rsqrt(jnp.mean(x * x, axis=-1, keepdims=True) + EPS) * g_ref[...]
        m = mod_ref[...]
        h_scr[...] = (xn * (1.0 + m[1:2]) + m[0:1]).astype(BF16)

    y = _dot(h_scr[...], w_ref[...])

    @pl.when(j < n_conv_tiles)
    def _():
        tm = y.shape[0]
        pos = _iota(y.shape, 0) & (seg - 1)
        prev = jnp.where(pos == 0, 0.0, pltpu.roll(y, 1, 0))
        nxt = jnp.where(pos == seg - 1, 0.0, pltpu.roll(y, tm - 1, 0))
        cw = cw_ref[...]
        o_ref[...] = cb_ref[...] + prev * cw[0:1] + y * cw[1:2] + nxt * cw[2:3]

    @pl.when(j >= n_conv_tiles)
    def _():
        o_ref[...] = y


def _in_projection(x, mod, norm_g, w_bf16, conv_w, conv_b, *, layer, tokens_per_mod, seg):
    T = x.shape[0]
    tm = min(2048, tokens_per_mod)
    tn = 1024
    n_conv_tiles = D_CONV // tn
    tiles_per_mod = tokens_per_mod // tm
    kern = functools.partial(_inproj_kernel, seg=seg, n_conv_tiles=n_conv_tiles)
    cmap = lambda i, j: (0, jnp.minimum(j, n_conv_tiles - 1))
    return pl.pallas_call(
        kern,
        grid=(T // tm, D_PROJ // tn),
        in_specs=[pl.BlockSpec((tm, D_MODEL), lambda i, j: (i, 0)),
                  pl.BlockSpec((None, 6, D_MODEL), lambda i, j: (i // tiles_per_mod, 0, 0)),
                  pl.BlockSpec((1, D_MODEL), lambda i, j: (0, 0)),
                  pl.BlockSpec((None, D_MODEL, tn), lambda i, j: (layer, 0, j)),
                  pl.BlockSpec((3, tn), cmap),
                  pl.BlockSpec((1, tn), cmap)],
        out_specs=pl.BlockSpec((tm, tn), lambda i, j: (i, j)),
        out_shape=jax.ShapeDtypeStruct((T, D_PROJ), F32),
        scratch_shapes=[pltpu.VMEM((tm, D_MODEL), BF16)],
        compiler_params=_cparams(("arbitrary", "arbitrary")),
        name="in_proj",
    )(x, mod, norm_g.reshape(1, D_MODEL), w_bf16, conv_w, conv_b.reshape(1, D_CONV))


def _hyfilt_kernel(w1_ref, b1_ref, w2_ref, b2_ref, w3_ref, o_ref, *, L):
    i = pl.program_id(1)
    tl = o_ref.shape[0]
    t = (_iota((tl, LANES), 0) + i * tl).astype(F32) / L
    lane = _iota((tl, LANES), 1)
    band = jnp.where(lane <= HY_POS_FREQS, lane, lane - HY_POS_FREQS).astype(F32)
    ang = 2.0 * math.pi * t * band
    feats = jnp.where(lane == 0, t,
                      jnp.where(lane <= HY_POS_FREQS, jnp.sin(ang),
                                jnp.where(lane <= 2 * HY_POS_FREQS, jnp.cos(ang), 0.0)))
    hdn = jnp.sin(_dot(feats, w1_ref[...], precision=HIGHEST) + b1_ref[...])
    hdn = jnp.sin(_dot(hdn, w2_ref[...], precision=HIGHEST) + b2_ref[...])
    filt = _dot(hdn, w3_ref[...], precision=HIGHEST)
    max_decay = math.log(HY_DECAY_TARGET) / HY_FAST_DECAY
    min_decay = math.log(HY_DECAY_TARGET) / HY_SLOW_DECAY
    n = filt.shape[1]
    ch = (_iota((tl, n), 1) & (D_BRANCH - 1)).astype(F32)
    delta = min_decay + ch * ((max_decay - min_decay) / (D_BRANCH - 1))
    tt = (_iota((tl, n), 0) + i * tl).astype(F32) / L
    o_ref[...] = filt * jnp.exp(-tt * jnp.abs(delta))


def _hyena_filters(L, w1p, b1, w2, b2, w3):
    tl = min(L, 256)
    n = 2 * D_BRANCH
    return pl.pallas_call(
        functools.partial(_hyfilt_kernel, L=L),
        grid=(2, L // tl),
        in_specs=[pl.BlockSpec((LANES, HY_FILTER_HIDDEN), lambda d, i: (0, 0)),
                  pl.BlockSpec((1, HY_FILTER_HIDDEN), lambda d, i: (0, 0)),
                  pl.BlockSpec((HY_FILTER_HIDDEN, HY_FILTER_HIDDEN), lambda d, i: (0, 0)),
                  pl.BlockSpec((1, HY_FILTER_HIDDEN), lambda d, i: (0, 0)),
                  pl.BlockSpec((HY_FILTER_HIDDEN, n), lambda d, i: (0, d))],
        out_specs=pl.BlockSpec((None, tl, n), lambda d, i: (d, i, 0)),
        out_shape=jax.ShapeDtypeStruct((2, L, n), F32),
        compiler_params=_cparams(("arbitrary", "arbitrary")),
        name="hyena_filters",
    )(w1p, b1.reshape(1, -1), w2, b2.reshape(1, -1), w3)


def _dftgen_kernel(f_ref, fs_ref, *, L, tk):
    i = pl.program_id(0)
    N = 2 * L
    k = _iota((tk, LANES), 0) + i * tk
    lane = _iota((tk, LANES), 1)
    w = 2.0 * math.pi / N
    a0 = ((k * lane) & (N - 1)).astype(F32) * w
    c0, s0 = jnp.cos(a0), jnp.sin(a0)
    a1 = ((k * lane * LANES) & (N - 1)).astype(F32) * w
    c1, s1 = jnp.cos(a1), jnp.sin(a1)
    alt = jnp.where((lane & 1) == 0, 1.0, -1.0)
    coef = jnp.where(k == 0, 1.0 / N, 2.0 / N)
    for n1 in range(L // LANES):
        c1b = c1[:, n1:n1 + 1]
        s1b = s1[:, n1:n1 + 1]
        cosb = c1b * c0 - s1b * s0
        sinb = jnp.where(k == 0, alt, -(s1b * c0 + c1b * s0))
        cols = slice(n1 * LANES, (n1 + 1) * LANES)
        f_ref[0:tk, cols] = cosb.astype(BF16)
        f_ref[tk:2 * tk, cols] = sinb.astype(BF16)
        fs_ref[0:tk, cols] = (coef * cosb).astype(BF16)
        fs_ref[tk:2 * tk, cols] = (coef * sinb).astype(BF16)


def _dft_matrices(L):
    tk = min(L, 256)
    spec = pl.BlockSpec((2 * tk, L), lambda i: (i, 0))
    return pl.pallas_call(
        functools.partial(_dftgen_kernel, L=L, tk=tk),
        grid=(L // tk,),
        in_specs=[],
        out_specs=[spec, spec],
        out_shape=[jax.ShapeDtypeStruct((2 * L, L), BF16)] * 2,
        compiler_params=_cparams(("arbitrary",)),
        name="dft_matrices",
    )()


def _dfth_kernel(f_ref, h_ref, o_ref, hp_scr, *, tk):
    i = pl.program_id(1)
    n = o_ref.shape[1]

    @pl.when(i == 0)
    def _():
        hc = h_ref[0]
        ha = jnp.where(_iota(hc.shape, 0) == 0, 0.0, h_ref[1])
        hp_scr[:, 0:n] = (hc + ha).astype(BF16)
        hp_scr[:, n:2 * n] = (hc - ha).astype(BF16)

    u = _dot(f_ref[...], hp_scr[...])
    r = _iota((2 * tk, n), 0)
    from_sum = (r < tk) | ((r == tk) & (i == 0))
    o_ref[...] = jnp.where(from_sum, u[:, 0:n], u[:, n:2 * n])


def _filter_spectrum(fmat, filt):
    L = fmat.shape[1]
    tk = min(L, 256)
    C = filt.shape[2]
    tc = 256
    return pl.pallas_call(
        functools.partial(_dfth_kernel, tk=tk),
        grid=(C // tc, L // tk),
        in_specs=[pl.BlockSpec((2 * tk, L), lambda c, i: (i, 0)),
                  pl.BlockSpec((2, L, tc), lambda c, i: (0, 0, c))],
        out_specs=pl.BlockSpec((2 * tk, tc), lambda c, i: (i, c)),
        out_shape=jax.ShapeDtypeStruct((2 * L, C), F32),
        scratch_shapes=[pltpu.VMEM((L, 2 * tc), BF16)],
        compiler_params=_cparams(("arbitrary", "arbitrary")),
        name="filter_spectrum",
    )(fmat, filt)


def _dfta_kernel(f_ref, u_ref, h_ref, y_ref, u_scr, *, tk):
    i = pl.program_id(1)

    @pl.when(i == 0)
    def _():
        u_scr[...] = u_ref[...].astype(BF16)

    uf = _dot(f_ref[...], u_scr[...])
    ur, ui = uf[0:tk], uf[tk:2 * tk]
    hr, hi = h_ref[0:tk, :], h_ref[tk:2 * tk, :]
    dc = (_iota(ur.shape, 0) == 0) & (i == 0)
    y_ref[0:tk, :] = jnp.where(dc, ur * hr, ur * hr - ui * hi).astype(BF16)
    y_ref[tk:2 * tk, :] = jnp.where(dc, ui * hi, ur * hi + ui * hr).astype(BF16)


def _dftb_kernel(g_ref, y_ref, u_ref, x_ref, b_ref, o_ref):
    y = _dot(g_ref[...], y_ref[...])
    o_ref[...] = x_ref[...] * (y + u_ref[...] * b_ref[...])


def _spectral_conv(fmat, gmat, hspec, h_col, bias, u_arr, u_col, gate_arr, gate_col):
    B, L = u_arr.shape[0], u_arr.shape[1]
    C = D_BRANCH
    tk = min(L, 256)
    y = pl.pallas_call(
        functools.partial(_dfta_kernel, tk=tk),
        grid=(B, L // tk),
        in_specs=[pl.BlockSpec((2 * tk, L), lambda b, i: (i, 0)),
                  pl.BlockSpec((None, L, C), lambda b, i: (b, 0, u_col)),
                  pl.BlockSpec((2 * tk, C), lambda b, i: (i, h_col))],
        out_specs=pl.BlockSpec((None, 2 * tk, C), lambda b, i: (b, i, 0)),
        out_shape=jax.ShapeDtypeStruct((B, 2 * L, C), BF16),
        scratch_shapes=[pltpu.VMEM((L, C), BF16)],
        compiler_params=_cparams(("arbitrary", "arbitrary")),
        name="hyena_dft_fwd",
    )(fmat, u_arr, hspec)
    tr = min(L, 256)
    return pl.pallas_call(
        _dftb_kernel,
        grid=(B, L // tr),
        in_specs=[pl.BlockSpec((tr, 2 * L), lambda b, i: (i, 0)),
                  pl.BlockSpec((None, 2 * L, C), lambda b, i: (b, 0, 0)),
                  pl.BlockSpec((None, tr, C), lambda b, i: (b, i, u_col)),
                  pl.BlockSpec((None, tr, C), lambda b, i: (b, i, gate_col)),
                  pl.BlockSpec((1, C), lambda b, i: (0, 0))],
        out_specs=pl.BlockSpec((None, tr, C), lambda b, i: (b, i, 0)),
        out_shape=jax.ShapeDtypeStruct((B, L, C), F32),
        compiler_params=_cparams(("arbitrary", "arbitrary")),
        name="hyena_dft_inv",
    )(gmat, y, u_arr, gate_arr, bias.reshape(1, C))


def _hgrn_kernel(q_ref, ff_ref, fb_ref, v_ref, g_ref, lb_ref, nrm_ref, s0_ref, y_ref, sf_ref,
                 of_scr, ob_scr, st_scr, *, L, layer, depth):
    C = HG_CHUNK
    W = HG_DK
    nc = L // C
    nh = q_ref.shape[1] // W
    ridx = _iota((C, W), 0)
    o_scrs = (of_scr, ob_scr)
    f_refs = (ff_ref, fb_ref)

    def lower_bound(d, cols):
        rows = [lb_ref[d, l:l + 1, cols] for l in range(depth)]
        m = rows[0]
        for r in rows[1:]:
            m = jnp.maximum(m, r)
        es = [jnp.exp(r - m) for r in rows]
        tot = es[0]
        for e in es[1:]:
            tot = tot + e
        acc = es[0] / tot
        for e in es[1:layer + 1]:
            acc = acc + e / tot
        return acc - es[0] / tot

    lbs = [[lower_bound(d, slice(hh * W, (hh + 1) * W)) for hh in range(nh)] for d in (0, 1)]
    for d in (0, 1):
        for hh in range(nh):
            st_scr[d, hh] = s0_ref[d, hh].T

    def step(c, d, hh):
        rev = d == 1
        rows = pl.ds(pl.multiple_of(c * C, C), C)
        cols = slice(hh * W, (hh + 1) * W)
        lb = lbs[d][hh]
        q = _silu(q_ref[rows, cols])
        uf = f_refs[d][rows, cols]
        v = v_ref[rows, cols]
        f = lb + (1.0 - lb) * _sigmoid(uf)
        g = jnp.log(jnp.maximum(f, LOG_FLOOR))
        kin = (1.0 - lb) * _sigmoid(-uf)
        b = _cumsum_rows(g, rev)
        st = st_scr[d, hh]
        o = _bdot_nt(q * jnp.exp(b), st)
        intra = jnp.zeros((C, W), F32)
        for t in range(C):
            mask = (ridx >= t) if rev else (ridx <= t)
            pair = _masked_exp(mask, b[t:t + 1, :] - b)
            a = jnp.sum(pair * (q[t:t + 1, :] * kin), axis=1, keepdims=True)
            row = jnp.sum(a * v, axis=0, keepdims=True)
            intra = jnp.where(ridx == t, row, intra)
        b_end = b[0:1, :] if rev else b[C - 1:C, :]
        st_scr[d, hh] = st * jnp.exp(b_end) + _bdot_tn(v, kin * jnp.exp(b_end - b))
        o_scrs[d][rows, cols] = o + intra

    def body(ci, carry):
        for d in (0, 1):
            for hh in range(nh):
                step((nc - 1 - ci) if d == 1 else ci, d, hh)
        return carry

    lax.fori_loop(0, nc, body, 0)
    for d in (0, 1):
        for hh in range(nh):
            sf_ref[d, hh] = st_scr[d, hh].T

    tr = min(L, 256)

    def fin(i, carry):
        rows = pl.ds(pl.multiple_of(i * tr, tr), tr)
        for hh in range(nh):
            cols = slice(hh * W, (hh + 1) * W)
            o = of_scr[rows, cols] + ob_scr[rows, cols]
            on = o * lax.rsqrt(jnp.mean(o * o, axis=-1, keepdims=True) + EPS) * nrm_ref[...]
            y_ref[rows, cols] = on * _silu(g_ref[rows, cols])
        return carry

    lax.fori_loop(0, L // tr, fin, 0)


def _hgrn_scan(proj3, hg_lb, hg_norm, s0, *, layer):
    B, L = proj3.shape[0], proj3.shape[1]
    depth = hg_lb.shape[1]
    nh = 2
    W = nh * HG_DK
    col = lambda base: (lambda b, h: (b, 0, base // W + h))
    st_spec = pl.BlockSpec((None, 2, nh, HG_DK, HG_DK), lambda b, h: (b, 0, h, 0, 0))
    return pl.pallas_call(
        functools.partial(_hgrn_kernel, L=L, layer=layer, depth=depth),
        grid=(B, HG_HEADS // nh),
        in_specs=[pl.BlockSpec((None, L, W), col(P_HGQ)),
                  pl.BlockSpec((None, L, W), col(P_HGFF)),
                  pl.BlockSpec((None, L, W), col(P_HGFB)),
                  pl.BlockSpec((None, L, W), col(P_HGV)),
                  pl.BlockSpec((None, L, W), col(P_HGG)),
                  pl.BlockSpec((2, depth, W), lambda b, h: (0, 0, h)),
                  pl.BlockSpec((1, HG_DK), lambda b, h: (0, 0)),
                  st_spec],
        out_specs=[pl.BlockSpec((None, L, W), lambda b, h: (b, 0, h)), st_spec],
        out_shape=[jax.ShapeDtypeStruct((B, L, D_BRANCH), F32),
                   jax.ShapeDtypeStruct(s0.shape, F32)],
        scratch_shapes=[pltpu.VMEM((L, W), F32), pltpu.VMEM((L, W), F32),
                        pltpu.VMEM((2, nh, HG_DK, HG_DK), F32)],
        compiler_params=_cparams(("arbitrary", "arbitrary")),
        name="hgrn2_scan",
    )(proj3, proj3, proj3, proj3, proj3, hg_lb, hg_norm.reshape(1, HG_DK), s0)


def _ssd_kernel(x_ref, bm_ref, cm_ref, z_ref, sm_ref, bias_ref, alog_ref, dskip_ref, s0_ref,
                y_ref, sf_ref, of_scr, ob_scr, st_scr, *, L):
    C = SSD_CHUNK
    W = LANES
    nc = L // C
    npairs = x_ref.shape[1] // W
    grp = pl.program_id(1)
    lane_lo = _iota((C, W), 1) < SSD_P
    row_lo = _iota((W, SSD_N), 0) < SSD_P
    pick2 = lambda a0, a1: jnp.where(lane_lo, a0, a1)
    o_scrs = (of_scr, ob_scr)
    incls = (_tri_mask(C, False), _tri_mask(C, True))
    for d in (0, 1):
        for pp in range(npairs):
            st_scr[d, pp] = s0_ref[d, pp]

    heads = [(d, pp, hh) for d in (0, 1) for pp in range(npairs) for hh in (0, 1)]
    pairs = [(d, pp) for d in (0, 1) for pp in range(npairs)]

    def body(ci, carry):
        rows_d, bm, cm, dt_all, cum_all, gram = [], [], [], [], [], []
        for d in (0, 1):
            c = (nc - 1 - ci) if d == 1 else ci
            rows = pl.ds(pl.multiple_of(c * C, C), C)
            rows_d.append(rows)
            bm.append(_silu(bm_ref[rows, :]))
            cm.append(_silu(cm_ref[rows, :]))
            dt_all.append(_softplus(sm_ref[rows, :] + bias_ref[...]))
            cum_all.append(_cumsum_rows(-jnp.exp(alog_ref[...]) * dt_all[d], d == 1))
        lane_of = lambda u: S_DT + u[0] * SSD_HEADS + 2 * (npairs * grp + u[1]) + u[2]
        cumr = {u: _row_pick(cum_all[u[0]], lane_of(u)) for u in heads}
        for d in (0, 1):
            gram.append(_bdot_nt(cm[d], bm[d]))
        dtc = {u: _lane_pick(dt_all[u[0]], lane_of(u)) for u in heads}
        cumc = {u: _lane_pick(cum_all[u[0]], lane_of(u)) for u in heads}
        end = {u: (cumc[u][0:1, :] if u[0] == 1 else cumc[u][C - 1:C, :]) for u in heads}
        xs = {pr: _silu(x_ref[rows_d[pr[0]], slice(pr[1] * W, (pr[1] + 1) * W)]) for pr in pairs}
        st = {pr: st_scr[pr[0], pr[1]] for pr in pairs}
        y_state = {u: _bdot_nt(cm[u[0]] * jnp.exp(cumc[u]), st[u[:2]]) for u in heads}
        scores = {u: gram[u[0]] * _masked_exp(incls[u[0]], cumc[u] - cumr[u]) for u in heads}
        y_intra = {}
        for u in heads:
            head_x = jnp.where(lane_lo if u[2] == 0 else ~lane_lo, xs[u[:2]], 0.0) * dtc[u]
            y_intra[u] = _bdot(scores[u], head_x)
        for pr in pairs:
            u0, u1 = pr + (0,), pr + (1,)
            xdt_e = xs[pr] * pick2(dtc[u0] * jnp.exp(end[u0] - cumc[u0]), dtc[u1] * jnp.exp(end[u1] - cumc[u1]))
            st_scr[pr[0], pr[1]] = (st[pr] * jnp.where(row_lo, jnp.exp(end[u0]), jnp.exp(end[u1]))
                                    + _bdot_tn(xdt_e, bm[pr[0]]))
            o_scrs[pr[0]][rows_d[pr[0]], slice(pr[1] * W, (pr[1] + 1) * W)] = pick2(
                y_intra[u0] + y_state[u0], y_intra[u1] + y_state[u1])
        return carry

    lax.fori_loop(0, nc, body, 0)
    for d in (0, 1):
        for pp in range(npairs):
            sf_ref[d, pp] = st_scr[d, pp]

    tr = min(L, 256)

    def fin(i, carry):
        rows = pl.ds(pl.multiple_of(i * tr, tr), tr)
        y = of_scr[rows, :] + ob_scr[rows, :] + _silu(x_ref[rows, :]) * dskip_ref[...]
        y_ref[rows, :] = y * _silu(z_ref[rows, :])
        return carry

    lax.fori_loop(0, L // tr, fin, 0)


def _ssd_scan(proj3, bias_row, alog_row, ssd_d, s0):
    B, L = proj3.shape[0], proj3.shape[1]
    n_groups = 2
    n_pairs = SSD_HEADS // 2
    ppg = n_pairs // n_groups
    W = ppg * LANES
    s0p = s0.reshape(B, 2, n_pairs, 2 * SSD_P, SSD_N)
    dskip = jnp.repeat(ssd_d, SSD_P).reshape(n_groups, 1, W)
    col = lambda base: (lambda b, g: (b, 0, base // W + g))
    grp = lambda base: (lambda b, g: (b, 0, base // LANES + g))
    st_spec = pl.BlockSpec((None, 2, ppg, LANES, SSD_N), lambda b, g: (b, 0, g, 0, 0))
    y, sf = pl.pallas_call(
        functools.partial(_ssd_kernel, L=L),
        grid=(B, n_groups),
        in_specs=[pl.BlockSpec((None, L, W), col(C_SSDX)),
                  pl.BlockSpec((None, L, LANES), grp(C_SSDB)),
                  pl.BlockSpec((None, L, LANES), grp(C_SSDC)),
                  pl.BlockSpec((None, L, W), col(P_SSDZ)),
                  pl.BlockSpec((None, L, LANES), lambda b, g: (b, 0, P_SMALL // LANES)),
                  pl.BlockSpec((1, LANES), lambda b, g: (0, 0)),
                  pl.BlockSpec((1, LANES), lambda b, g: (0, 0)),
                  pl.BlockSpec((None, 1, W), lambda b, g: (g, 0, 0)),
                  st_spec],
        out_specs=[pl.BlockSpec((None, L, W), lambda b, g: (b, 0, g)), st_spec],
        out_shape=[jax.ShapeDtypeStruct((B, L, D_BRANCH), F32),
                   jax.ShapeDtypeStruct(s0p.shape, F32)],
        scratch_shapes=[pltpu.VMEM((L, W), F32), pltpu.VMEM((L, W), F32),
                        pltpu.VMEM((2, ppg, LANES, SSD_N), F32)],
        compiler_params=_cparams(("arbitrary", "arbitrary")),
        name="ssd_scan",
    )(proj3, proj3, proj3, proj3, proj3, bias_row, alog_row, dskip, s0p)
    return y, sf.reshape(s0.shape)


def _l2norm(a):
    return a * lax.rsqrt(jnp.sum(a * a, axis=-1, keepdims=True) + EPS)


def _gdn_kernel(q_ref, k_ref, v_ref, g_ref, sm_ref, bias_ref, alog_ref, nrm_ref, s0_ref,
                y_ref, sf_ref, of_scr, ob_scr, st_scr, *, L):
    C = GDN_CHUNK
    W = GDN_DK
    nc = L // C
    nh = q_ref.shape[1] // W
    hblk = pl.program_id(1)
    o_scrs = (of_scr, ob_scr)
    incls = (_tri_mask(C, False), _tri_mask(C, True))
    stricts = (_tri_mask(C, False, strict=True), _tri_mask(C, True, strict=True))
    for d in (0, 1):
        for hh in range(nh):
            st_scr[d, hh] = s0_ref[d, hh]

    units = [(d, hh) for d in (0, 1) for hh in range(nh)]
    eye = (_iota((C, C), 0) == _iota((C, C), 1)).astype(F32)

    def body(ci, carry):
        rows_d, cum_d, beta_d = [], [], []
        for d in (0, 1):
            c = (nc - 1 - ci) if d == 1 else ci
            rows = pl.ds(pl.multiple_of(c * C, C), C)
            raw = sm_ref[rows, :]
            rows_d.append(rows)
            cum_d.append(_cumsum_rows(-jnp.exp(alog_ref[...]) * _softplus(raw + bias_ref[...]), d == 1))
            beta_d.append(_sigmoid(raw))
        q, k, v, gc, grow, beta = {}, {}, {}, {}, {}, {}
        for u in units:
            d, hh = u
            cols = slice(hh * W, (hh + 1) * W)
            h = nh * hblk + hh
            q[u] = _l2norm(_silu(q_ref[rows_d[d], cols])) * (GDN_DK ** -0.5)
            k[u] = _l2norm(_silu(k_ref[rows_d[d], cols]))
            v[u] = _silu(v_ref[rows_d[d], cols])
            gc[u] = _lane_pick(cum_d[d], S_GA + d * GDN_HEADS + h)
            beta[u] = _lane_pick(beta_d[d], S_GB + d * GDN_HEADS + h)
        for u in units:
            grow[u] = _row_pick(cum_d[u[0]], S_GA + u[0] * GDN_HEADS + nh * hblk + u[1])
        decay = {u: _masked_exp(incls[u[0]], gc[u] - grow[u]) for u in units}
        kb = {u: k[u] * beta[u] for u in units}
        m = {u: -jnp.where(stricts[u[0]], _bdot_nt(kb[u], k[u]) * decay[u], 0.0) for u in units}
        aqk = {u: _bdot_nt(q[u], k[u]) * decay[u] for u in units}
        p = {u: eye + m[u] for u in units}
        m = {u: _dot3(m[u], m[u]) for u in units}
        sh = 4
        while sh < C:
            both = {u: _dot3(jnp.concatenate([m[u], p[u]], axis=0), m[u]) for u in units}
            p = {u: p[u] + both[u][C:2 * C] for u in units}
            m = {u: both[u][0:C] for u in units}
            sh *= 2
        p = {u: p[u] + _dot3(p[u], m[u]) for u in units}
        rhs = {u: jnp.concatenate([v[u] * beta[u], kb[u] * jnp.exp(gc[u])], axis=1) for u in units}
        uw = {u: _dot3(p[u], rhs[u]) for u in units}
        uu = {u: uw[u][:, 0:W] for u in units}
        ww = {u: uw[u][:, W:2 * W] for u in units}
        st = {u: st_scr[u[0], u[1]] for u in units}
        v_new = {u: uu[u] - _bdot(ww[u], st[u]) for u in units}
        o = {u: _bdot(q[u] * jnp.exp(gc[u]), st[u]) + _bdot(aqk[u], v_new[u]) for u in units}
        for u in units:
            d, hh = u
            g_end = gc[u][0:1, :] if d == 1 else gc[u][C - 1:C, :]
            st_scr[d, hh] = st[u] * jnp.exp(g_end) + _bdot_tn(k[u] * jnp.exp(g_end - gc[u]), v_new[u])
            o_scrs[d][rows_d[d], slice(hh * W, (hh + 1) * W)] = o[u]
        return carry

    lax.fori_loop(0, nc, body, 0)
    for d in (0, 1):
        for hh in range(nh):
            sf_ref[d, hh] = st_scr[d, hh]

    tr = min(L, 256)

    def fin(i, carry):
        rows = pl.ds(pl.multiple_of(i * tr, tr), tr)
        for hh in range(nh):
            cols = slice(hh * W, (hh + 1) * W)
            o = of_scr[rows, cols] + ob_scr[rows, cols]
            on = o * lax.rsqrt(jnp.mean(o * o, axis=-1, keepdims=True) + EPS) * nrm_ref[...]
            y_ref[rows, cols] = on * _silu(g_ref[rows, cols])
        return carry

    lax.fori_loop(0, L // tr, fin, 0)


def _gdn_scan(proj3, bias_row, alog_row, gdn_norm, s0):
    B, L = proj3.shape[0], proj3.shape[1]
    nh = GDN_HEADS
    W = nh * GDN_DK
    col = lambda base: (lambda b, h: (b, 0, base // W + h))
    st_spec = pl.BlockSpec((None, 2, nh, GDN_DK, GDN_DK), lambda b, h: (b, 0, h, 0, 0))
    once = pl.Buffered(1)
    return pl.pallas_call(
        functools.partial(_gdn_kernel, L=L),
        grid=(B, GDN_HEADS // nh),
        in_specs=[pl.BlockSpec((None, L, W), col(C_GQ), pipeline_mode=once),
                  pl.BlockSpec((None, L, W), col(C_GK), pipeline_mode=once),
                  pl.BlockSpec((None, L, W), col(C_GV), pipeline_mode=once),
                  pl.BlockSpec((None, L, W), col(P_GDNG), pipeline_mode=once),
                  pl.BlockSpec((None, L, LANES), lambda b, h: (b, 0, P_SMALL // LANES), pipeline_mode=once),
                  pl.BlockSpec((1, LANES), lambda b, h: (0, 0)),
                  pl.BlockSpec((1, LANES), lambda b, h: (0, 0)),
                  pl.BlockSpec((1, GDN_DK), lambda b, h: (0, 0)),
                  st_spec],
        out_specs=[pl.BlockSpec((None, L, W), lambda b, h: (b, 0, h)), st_spec],
        out_shape=[jax.ShapeDtypeStruct((B, L, D_BRANCH), F32),
                   jax.ShapeDtypeStruct(s0.shape, F32)],
        scratch_shapes=[pltpu.VMEM((L, W), F32), pltpu.VMEM((L, W), F32),
                        pltpu.VMEM((2, nh, GDN_DK, GDN_DK), F32)],
        compiler_params=_cparams(("arbitrary", "arbitrary")),
        name="gdn_scan",
    )(proj3, proj3, proj3, proj3, proj3, bias_row, alog_row, gdn_norm.reshape(1, GDN_DK), s0)


def _merge_kernel(ya_ref, yb_ref, yc_ref, yd_ref, mg_ref, x_ref, mod_ref, sn_ref, n2_ref, wb_ref, wo_ref,
                  xo_ref, h_ref):
    yc = yc_ref[...]
    half = D_BRANCH // 2
    parts = []
    for gidx in range(2):
        seg = yc[:, gidx * half:(gidx + 1) * half]
        parts.append(seg * lax.rsqrt(jnp.mean(seg * seg, axis=-1, keepdims=True) + EPS)
                     * sn_ref[:, gidx * half:(gidx + 1) * half])
    branches = (ya_ref[...], yb_ref[...], None, yd_ref[...])
    mixed = None
    for n in range(N_BRANCH):
        gate = _sigmoid(mg_ref[:, n * D_MODEL:(n + 1) * D_MODEL])
        if n == 2:
            lifted = (_dot(parts[0].astype(BF16), wb_ref[n, 0:half, :])
                      + _dot(parts[1].astype(BF16), wb_ref[n, half:D_BRANCH, :]))
        else:
            lifted = _dot(branches[n].astype(BF16), wb_ref[n])
        mixed = gate * lifted if mixed is None else mixed + gate * lifted
    m = mod_ref[...]
    x = x_ref[...] + m[2:3] * _dot(mixed.astype(BF16), wo_ref[...])
    xo_ref[...] = x
    xn = x * lax.rsqrt(jnp.mean(x * x, axis=-1, keepdims=True) + EPS) * n2_ref[...]
    h_ref[...] = xn * (1.0 + m[4:5]) + m[3:4]


def _merge(ya, yb, yc, yd, proj, x, mod, ssd_norm, norm2_g, wb_bf16, wo_bf16, *, tokens_per_mod):
    T = x.shape[0]
    tm = min(512, tokens_per_mod)
    tiles_per_mod = tokens_per_mod // tm
    yspec = pl.BlockSpec((tm, D_BRANCH), lambda i: (i, 0))
    xspec = pl.BlockSpec((tm, D_MODEL), lambda i: (i, 0))
    return pl.pallas_call(
        _merge_kernel,
        grid=(T // tm,),
        in_specs=[yspec, yspec, yspec, yspec,
                  pl.BlockSpec((tm, N_BRANCH * D_MODEL), lambda i: (i, P_MERGE // (N_BRANCH * D_MODEL))),
                  xspec,
                  pl.BlockSpec((None, 6, D_MODEL), lambda i: (i // tiles_per_mod, 0, 0)),
                  pl.BlockSpec((1, D_BRANCH), lambda i: (0, 0)),
                  pl.BlockSpec((1, D_MODEL), lambda i: (0, 0)),
                  pl.BlockSpec((N_BRANCH, D_BRANCH, D_MODEL), lambda i: (0, 0, 0)),
                  pl.BlockSpec((D_MODEL, D_MODEL), lambda i: (0, 0))],
        out_specs=[xspec, xspec],
        out_shape=[jax.ShapeDtypeStruct((T, D_MODEL), F32)] * 2,
        compiler_params=_cparams(("arbitrary",)),
        name="merge_outproj",
    )(ya, yb, yc, yd, proj, x, mod, ssd_norm.reshape(1, D_BRANCH), norm2_g.reshape(1, D_MODEL), wb_bf16, wo_bf16)


def _router_kernel(h_ref, rw_ref, rb_ref, wt_ref, lpos_ref, tlen_ref, tstart_ref, tcar_ref, carry):
    i = pl.program_id(0)
    tm = h_ref.shape[0]
    E = N_EXPERTS
    gsz = E // N_EXPERT_GROUPS
    neg_inf = -jnp.inf

    @pl.when(i == 0)
    def _():
        carry[...] = jnp.zeros_like(carry)

    scores = _sigmoid(_dot_nt(rw_ref[...], h_ref[...], precision=HIGHEST))
    biased = scores + rb_ref[...]
    eidx = _iota((E, tm), 0)
    ridx = _iota((gsz, tm), 0)

    slabs = [biased[g * gsz:(g + 1) * gsz, :] for g in range(N_EXPERT_GROUPS)]
    gs = []
    for v in slabs:
        m1 = jnp.max(v, axis=0, keepdims=True)
        i1 = jnp.min(jnp.where(v == m1, ridx, gsz), axis=0, keepdims=True)
        m2 = jnp.max(jnp.where(ridx == i1, neg_inf, v), axis=0, keepdims=True)
        gs.append(m1 + m2)
    masked = []
    for g in range(N_EXPERT_GROUPS):
        rank = jnp.zeros((1, tm), I32)
        for g2 in range(N_EXPERT_GROUPS):
            if g2 == g:
                continue
            ahead = (gs[g2] >= gs[g]) if g2 < g else (gs[g2] > gs[g])
            rank = rank + ahead.astype(I32)
        masked.append(jnp.where(rank < TOPK_GROUPS, slabs[g], MASK_NEG))
    cur = jnp.concatenate(masked, axis=0)

    krow = _iota((TOP_K, tm), 0)
    onehot = jnp.zeros((E, tm), F32)
    picks = []
    wsum = jnp.zeros((1, tm), F32)
    for kk in range(TOP_K):
        m = jnp.max(cur, axis=0, keepdims=True)
        ik = jnp.min(jnp.where(cur == m, eidx, E), axis=0, keepdims=True)
        hit = eidx == ik
        wk = jnp.sum(jnp.where(hit, scores, 0.0), axis=0, keepdims=True)
        cur = jnp.where(hit, neg_inf, cur)
        onehot = onehot + hit.astype(F32)
        picks.append((ik, wk))
        wsum = wsum + wk
    wt_out = jnp.zeros((TOP_K, tm), F32)
    for kk, (ik, wk) in enumerate(picks):
        wt_out = jnp.where(krow == kk, wk / wsum * ROUTED_SCALE, wt_out)

    earlier = (_iota((tm, tm), 0) < _iota((tm, tm), 1)).astype(BF16)
    before = _dot(onehot.astype(BF16), earlier)
    pad = lambda c: (c.astype(I32) + (RUN_ROWS - 1)) // RUN_ROWS * RUN_ROWS
    run_len = pad(jnp.sum(onehot, axis=1, keepdims=True) + jnp.zeros((E, LANES), F32)).astype(F32)
    run_start = _cumsum_rows(run_len, False) - run_len
    local = before + run_start[:, 0:1]
    lpos_out = jnp.zeros((TOP_K, tm), I32)
    for kk, (ik, wk) in enumerate(picks):
        lp = jnp.sum(jnp.where(eidx == ik, local, 0.0), axis=0, keepdims=True)
        lpos_out = jnp.where(krow == kk, lp.astype(I32), lpos_out)
    wt_ref[...] = wt_out
    lpos_ref[...] = lpos_out

    onehot_l = jnp.concatenate([onehot, jnp.zeros((LANES - E, tm), F32)], axis=0).astype(BF16)
    len_row = pad(_dot_nt(jnp.ones((8, tm), BF16), onehot_l)).astype(F32)
    lanes_before = (_iota((LANES, LANES), 0) < _iota((LANES, LANES), 1)).astype(BF16)
    tlen_ref[...] = len_row.astype(I32)
    tstart_ref[...] = _dot(len_row.astype(BF16), lanes_before).astype(I32)
    tcar_ref[...] = carry[...].astype(I32)
    carry[...] = carry[...] + len_row


def _router(h2, router_w, router_b, *, tm):
    T = h2.shape[0]
    nt = T // tm
    ospec = pl.BlockSpec((TOP_K, tm), lambda i: (0, i))
    tspec = pl.BlockSpec((None, 8, LANES), lambda i: (i, 0, 0))
    tshape = jax.ShapeDtypeStruct((nt, 8, LANES), I32)
    wt, lpos, tlen, tstart, tcar = pl.pallas_call(
        _router_kernel,
        grid=(nt,),
        in_specs=[pl.BlockSpec((tm, D_MODEL), lambda i: (i, 0)),
                  pl.BlockSpec((N_EXPERTS, D_MODEL), lambda i: (0, 0)),
                  pl.BlockSpec((N_EXPERTS, 1), lambda i: (0, 0))],
        out_specs=[ospec, ospec, tspec, tspec, tspec],
        out_shape=[jax.ShapeDtypeStruct((TOP_K, T), F32), jax.ShapeDtypeStruct((TOP_K, T), I32),
                   tshape, tshape, tshape],
        scratch_shapes=[pltpu.VMEM((8, LANES), F32)],
        compiler_params=_cparams(("arbitrary",)),
        name="moe_router",
    )(h2, router_w.T, router_b.reshape(N_EXPERTS, 1))
    table = lambda a: a[:, 0, :N_EXPERTS]
    return wt, lpos, table(tlen), table(tstart), table(tcar)


def _dispatch_kernel(len_ref, lst_ref, car_ref, seg_ref, lpos_ref, h_ref, xs_ref, xloc, sem):
    i = pl.program_id(0)
    tm = h_ref.shape[0]
    nloc = xloc.shape[0]
    R = RUN_ROWS
    lpos = lpos_ref[...]
    hb = h_ref[...].astype(BF16)
    rb = 256
    for blk in range(nloc // rb):
        r = _iota((rb, tm), 0) + blk * rb
        hit = r == lpos[0:1, :]
        for kk in range(1, TOP_K):
            hit = hit | (r == lpos[kk:kk + 1, :])
        xloc[blk * rb:(blk + 1) * rb, :] = _dot(hit.astype(BF16), hb)

    def piece(src_row, dst_row):
        return pltpu.make_async_copy(xloc.at[pl.ds(pl.multiple_of(src_row, R), R), :],
                                     xs_ref.at[pl.ds(pl.multiple_of(dst_row, R), R), :], sem)

    def per_expert(e, issued):
        pieces = len_ref[i, e] // R
        ls = lst_ref[i, e]
        gd = seg_ref[e] + car_ref[i, e]

        def issue(j, c):
            piece(ls + j * R, gd + j * R).start()
            return c

        lax.fori_loop(0, pieces, issue, 0)
        return issued + pieces

    total = lax.fori_loop(0, N_EXPERTS, per_expert, 0)

    def wait(j, c):
        piece(0, 0).wait()
        return c

    lax.fori_loop(0, total, wait, 0)


def _dispatch(h2, lpos, tile_len, tile_lstart, tile_car, seg_start, *, tm):
    T = h2.shape[0]
    nt = T // tm
    n_rows = T * TOP_K + N_EXPERTS * nt * RUN_ROWS
    nloc = tm * TOP_K + N_EXPERTS * RUN_ROWS
    return pl.pallas_call(
        _dispatch_kernel,
        grid_spec=pltpu.PrefetchScalarGridSpec(
            num_scalar_prefetch=4,
            grid=(nt,),
            in_specs=[pl.BlockSpec((TOP_K, tm), lambda i, *_: (0, i)),
                      pl.BlockSpec((tm, D_MODEL), lambda i, *_: (i, 0))],
            out_specs=pl.BlockSpec(memory_space=pl.ANY),
            scratch_shapes=[pltpu.VMEM((nloc, D_MODEL), F32), pltpu.SemaphoreType.DMA(())]),
        out_shape=jax.ShapeDtypeStruct((n_rows, D_MODEL), F32),
        compiler_params=_cparams(("arbitrary",)),
        name="moe_dispatch",
    )(tile_len, tile_lstart, tile_car, seg_start, lpos, h2)


def _expert_kernel(blk_ref, exp_ref, lo_ref, hi_ref, first_ref, x_ref, w1_ref, w3_ref, w2_ref, o_ref):
    w = pl.program_id(0)
    x = x_ref[...].astype(BF16)
    a = _dot(x, w1_ref[...].astype(BF16))
    b = _dot(x, w3_ref[...].astype(BF16))
    y = _dot((_silu(a) * b).astype(BF16), w2_ref[...].astype(BF16))
    r = _iota(y.shape, 0)
    y = jnp.where((r >= lo_ref[w]) & (r < hi_ref[w]), y, 0.0)

    @pl.when(first_ref[w] == 1)
    def _():
        o_ref[...] = y

    @pl.when(first_ref[w] == 0)
    def _():
        o_ref[...] = o_ref[...] + y


def _experts(xs, work, w1, w3, w2, *, layer, bm):
    A = xs.shape[0]
    n_work = work[0].shape[0]
    xmap = lambda w, blk, ex, lo, hi, first: (blk[w], 0)
    wmap = lambda w, blk, ex, lo, hi, first: (layer, ex[w], 0, 0)
    return pl.pallas_call(
        _expert_kernel,
        grid_spec=pltpu.PrefetchScalarGridSpec(
            num_scalar_prefetch=5,
            grid=(n_work,),
            in_specs=[pl.BlockSpec((bm, D_MODEL), xmap),
                      pl.BlockSpec((None, None, D_MODEL, D_EXPERT), wmap),
                      pl.BlockSpec((None, None, D_MODEL, D_EXPERT), wmap),
                      pl.BlockSpec((None, None, D_EXPERT, D_MODEL), wmap)],
            out_specs=pl.BlockSpec((bm, D_MODEL), xmap)),
        out_shape=jax.ShapeDtypeStruct((A, D_MODEL), F32),
        compiler_params=_cparams(("arbitrary",)),
        name="moe_experts",
    )(*work, xs, w1, w3, w2)


def _combine_kernel(len_ref, lst_ref, car_ref, seg_ref, lpos_ref, wt_ref, ys_ref, h_ref, x_ref, mod_ref,
                    s1_ref, s3_ref, s2_ref, fg_ref, o_ref, yloc, sem, *, final_norm):
    i = pl.program_id(0)
    tm = h_ref.shape[0]
    nloc = yloc.shape[0]
    R = RUN_ROWS

    @pl.when(i == 0)
    def _():
        yloc[...] = jnp.zeros_like(yloc)

    def piece(src_row, dst_row):
        return pltpu.make_async_copy(ys_ref.at[pl.ds(pl.multiple_of(src_row, R), R), :],
                                     yloc.at[pl.ds(pl.multiple_of(dst_row, R), R), :], sem)

    def per_expert(e, issued):
        pieces = len_ref[i, e] // R
        ls = lst_ref[i, e]
        gd = seg_ref[e] + car_ref[i, e]

        def issue(j, c):
            piece(gd + j * R, ls + j * R).start()
            return c

        lax.fori_loop(0, pieces, issue, 0)
        return issued + pieces

    total = lax.fori_loop(0, N_EXPERTS, per_expert, 0)

    hb = h_ref[...].astype(BF16)
    mid = _silu(_dot(hb, s1_ref[...])) * _dot(hb, s3_ref[...])
    y = _dot(mid.astype(BF16), s2_ref[...])

    def wait(j, c):
        piece(0, 0).wait()
        return c

    lax.fori_loop(0, total, wait, 0)

    lpos = lpos_ref[...]
    wt = wt_ref[...]
    used_rows = lst_ref[i, N_EXPERTS - 1] + len_ref[i, N_EXPERTS - 1]
    rb = 256

    def block_sum(blk):
        r = _iota((rb, tm), 0) + blk * rb
        pw = jnp.zeros((rb, tm), F32)
        for kk in range(TOP_K):
            pw = jnp.where(r == lpos[kk:kk + 1, :], wt[kk:kk + 1, :], pw)
        ysc = yloc[blk * rb:(blk + 1) * rb, :] * jnp.sum(pw, axis=1, keepdims=True)
        hi = ysc.astype(BF16)
        lo = (ysc - hi.astype(F32)).astype(BF16)
        owner = (pw != 0.0).astype(BF16)
        return _dot_tn(owner, hi) + _dot_tn(owner, lo)

    for blk in range(nloc // rb):
        if (blk + 1) * rb <= tm * TOP_K:
            y = y + block_sum(blk)
        else:
            y = y + lax.cond(used_rows > blk * rb, functools.partial(block_sum, blk),
                             lambda: jnp.zeros((tm, D_MODEL), F32))
    x = x_ref[...] + mod_ref[5:6, :] * y
    if final_norm:
        x = x * lax.rsqrt(jnp.mean(x * x, axis=-1, keepdims=True) + EPS) * fg_ref[...]
    o_ref[...] = x


def _combine(ys, lpos, wt, tile_len, tile_lstart, tile_car, seg_start, h2, x, mod, s1, s3, s2, final_g,
             *, tm, tokens_per_mod, final_norm):
    T = x.shape[0]
    tiles_per_mod = tokens_per_mod // tm
    nloc = tm * TOP_K + N_EXPERTS * RUN_ROWS
    xspec = pl.BlockSpec((tm, D_MODEL), lambda i, *_: (i, 0))
    kspec = pl.BlockSpec((TOP_K, tm), lambda i, *_: (0, i))
    const = lambda i, *_: (0, 0)
    return pl.pallas_call(
        functools.partial(_combine_kernel, final_norm=final_norm),
        grid_spec=pltpu.PrefetchScalarGridSpec(
            num_scalar_prefetch=4,
            grid=(T // tm,),
            in_specs=[kspec, kspec,
                      pl.BlockSpec(memory_space=pl.ANY),
                      xspec, xspec,
                      pl.BlockSpec((None, 6, D_MODEL), lambda i, *_: (i // tiles_per_mod, 0, 0)),
                      pl.BlockSpec((D_MODEL, D_SHARED), const),
                      pl.BlockSpec((D_MODEL, D_SHARED), const),
                      pl.BlockSpec((D_SHARED, D_MODEL), const),
                      pl.BlockSpec((1, D_MODEL), const)],
            out_specs=xspec,
            scratch_shapes=[pltpu.VMEM((nloc, D_MODEL), F32), pltpu.SemaphoreType.DMA(())]),
        out_shape=jax.ShapeDtypeStruct((T, D_MODEL), F32),
        compiler_params=_cparams(("arbitrary",)),
        name="moe_combine",
    )(tile_len, tile_lstart, tile_car, seg_start, lpos, wt, ys, h2, x, mod, s1, s3, s2, final_g.reshape(1, D_MODEL))


def _work_list(counts, starts, n_rows, bm):
    E = N_EXPERTS
    n_blocks = n_rows // bm
    n_work = n_blocks + E - 1
    ends = starts + counts
    first_blk = starts // bm
    last_blk = jnp.where(counts > 0, (ends - 1) // bm, first_blk)
    n_items = jnp.where(counts > 0, last_blk - first_blk + 1, 0)
    item_end = jnp.cumsum(n_items)
    item_start = item_end - n_items
    w = jnp.arange(n_work, dtype=I32)
    ex = jnp.minimum(jnp.sum((item_end[None, :] <= w[:, None]).astype(I32), axis=1), E - 1)
    valid = w < item_end[-1]
    blk = first_blk[ex] + (w - item_start[ex])
    blk = jnp.where(valid, blk, blk[jnp.maximum(item_end[-1] - 1, 0)]).astype(I32)
    lo = jnp.clip(starts[ex] - blk * bm, 0, bm)
    hi = jnp.clip(ends[ex] - blk * bm, 0, bm)
    lo = jnp.where(valid, lo, 0).astype(I32)
    hi = jnp.where(valid, hi, 0).astype(I32)
    ex = jnp.where(valid, ex, ex[jnp.maximum(item_end[-1] - 1, 0)])
    prev_blk = jnp.concatenate([jnp.full((1,), -1, I32), blk[:-1]])
    first = (blk != prev_blk).astype(I32)
    return blk, ex.astype(I32), lo, hi, first


def _moe(h2, x, mod, p, final_g, *, layer, tokens_per_mod, final_norm):
    T = x.shape[0]
    tm = 256
    wt, lpos, tile_len, tile_lstart, tile_car = _router(h2, p["router_w"], p["router_b"], tm=tm)
    seg_len = tile_car[-1] + tile_len[-1]
    seg_start = jnp.cumsum(seg_len) - seg_len
    xs = _dispatch(h2, lpos, tile_len, tile_lstart, tile_car, seg_start, tm=tm)
    bm = 512
    work = _work_list(seg_len, seg_start, xs.shape[0], bm)
    ys = _experts(xs, work, p["exp_w1"], p["exp_w3"], p["exp_w2"], layer=layer, bm=bm)
    return _combine(ys, lpos, wt, tile_len, tile_lstart, tile_car, seg_start, h2, x, mod,
                    p["sh_w1"], p["sh_w3"], p["sh_w2"], final_g,
                    tm=tm, tokens_per_mod=tokens_per_mod, final_norm=final_norm)


def _reorder_w_in(w_in):
    plain = D_CONV
    pieces = [w_in[..., :plain],
              w_in[..., plain + 3616:plain + 3616 + 4096],
              w_in[..., plain:plain + 3072],
              w_in[..., plain + 3088:plain + 3600],
              w_in[..., plain + 3072:plain + 3088],
              w_in[..., plain + 3600:plain + 3616]]
    out = jnp.concatenate(pieces, axis=-1)
    pad = D_PROJ - out.shape[-1]
    return jnp.pad(out, ((0, 0), (0, 0), (0, pad))).astype(BF16)


def _small_rows(ssd_vals, gdn_vals):
    row = jnp.zeros((LANES,), F32)
    row = row.at[S_DT:S_DT + 2 * SSD_HEADS].set(ssd_vals.reshape(-1))
    row = row.at[S_GA:S_GA + 2 * GDN_HEADS].set(gdn_vals.reshape(-1))
    return row.reshape(1, LANES)


def _layer_pass(x, mod, lp, hy, st_hg, st_ssd, st_gdn, final_g, *, B, L, tokens_per_mod, seg, layer, final_norm):
    T = B * L
    proj = _in_projection(x, mod, lp["norm1_g"], lp["w_in"], lp["conv_w"], lp["conv_b"],
                          layer=layer, tokens_per_mod=tokens_per_mod, seg=seg)
    proj3 = proj.reshape(B, L, D_PROJ)
    fmat, gmat, hspec = hy
    cb = lambda c: c // D_BRANCH
    z = _spectral_conv(fmat, gmat, hspec, 0, lp["hy_bias"][0], proj3, cb(C_HYV), proj3, cb(C_HYX1))
    ya = _spectral_conv(fmat, gmat, hspec, 1, lp["hy_bias"][1], z, 0, proj3, cb(C_HYX2))
    yb, s_hg = _hgrn_scan(proj3, lp["hg_lb"], lp["hg_norm"], st_hg, layer=layer)
    yc, s_ssd = _ssd_scan(proj3, lp["bias_row"], lp["alog_row"], lp["ssd_d"], st_ssd)
    yd, s_gdn = _gdn_scan(proj3, lp["bias_row"], lp["alog_row"], lp["gdn_norm"], st_gdn)
    flat = lambda a: a.reshape(T, D_BRANCH)
    x, h2 = _merge(flat(ya), flat(yb), flat(yc), flat(yd), proj, x, mod, lp["ssd_norm"], lp["norm2_g"],
                   lp["w_branch"], lp["w_out"], tokens_per_mod=tokens_per_mod)
    x = _moe(h2, x, mod, lp, final_g, layer=layer, tokens_per_mod=tokens_per_mod, final_norm=final_norm)
    return x, s_hg, s_ssd, s_gdn


def kernel(x_prompt, x_sample, state_hgrn, state_ssd, state_gdn, c, c_ctx, norm1_g, norm2_g, ada_w, ada_b, w_in, conv_w, conv_b, hy_w1, hy_b1, hy_w2, hy_b2, hy_w3, hy_bias, hg_lb, hg_norm, ssd_a_log, ssd_dt_bias, ssd_d, ssd_norm, gdn_a_log, gdn_dt_bias, gdn_norm, w_branch, w_out, router_w, router_bias, exp_w1, exp_w3, exp_w2, sh_w1, sh_w3, sh_w2, final_g):
    depth = w_in.shape[0]
    bp, lp_len = x_prompt.shape[0], x_prompt.shape[1]
    bs, ls_len = x_sample.shape[0], x_sample.shape[1]
    D = D_MODEL

    w_in_r = _reorder_w_in(w_in)
    hy_w1p = jnp.pad(hy_w1, ((0, 0), (0, LANES - hy_w1.shape[1]), (0, 0)))
    layers = []
    for l in range(depth):
        layers.append(dict(
            norm1_g=norm1_g[l], norm2_g=norm2_g[l], w_in=w_in_r, conv_w=conv_w[l], conv_b=conv_b[l],
            hy_bias=hy_bias[l], hg_lb=hg_lb, hg_norm=hg_norm[l],
            bias_row=_small_rows(ssd_dt_bias[l], gdn_dt_bias[l]),
            alog_row=_small_rows(ssd_a_log[l], gdn_a_log[l]),
            ssd_d=ssd_d[l], ssd_norm=ssd_norm[l], gdn_norm=gdn_norm[l],
            w_branch=w_branch[l].astype(BF16), w_out=w_out[l].astype(BF16),
            router_w=router_w[l], router_b=router_bias[l],
            exp_w1=exp_w1, exp_w3=exp_w3, exp_w2=exp_w2,
            sh_w1=sh_w1[l].astype(BF16), sh_w3=sh_w3[l].astype(BF16), sh_w2=sh_w2[l].astype(BF16)))

    def hyena_setup(L):
        fmat, fs = _dft_matrices(L)
        gmat = fs.T
        specs = []
        for l in range(depth):
            filt = _hyena_filters(L, hy_w1p[l], hy_b1[l], hy_w2[l], hy_b2[l], hy_w3[l])
            specs.append(_filter_spectrum(fmat, filt))
        return fmat, gmat, specs

    cond = jnp.concatenate([c_ctx.reshape(1, D), c], axis=0)
    rows = cond.shape[0]
    rows8 = (rows + 7) // 8 * 8
    cond8 = jnp.pad(cond, ((0, rows8 - rows), (0, 0)))
    mods = [_modulation(cond8, ada_w, ada_b[l], layer=l).reshape(rows8, 6, D) for l in range(depth)]

    fmat, gmat, specs = hyena_setup(lp_len)
    x = x_prompt.reshape(bp * lp_len, D)
    z_hg = jnp.zeros((bp, 2, HG_HEADS, HG_DK, HG_DK), F32)
    z_ssd = jnp.zeros((bp, 2, SSD_HEADS, SSD_P, SSD_N), F32)
    z_gdn = jnp.zeros((bp, 2, GDN_HEADS, GDN_DK, GDN_DK), F32)
    hg_states, ssd_states, gdn_states = [], [], []
    for l in range(depth):
        x, s_hg, s_ssd, s_gdn = _layer_pass(
            x, mods[l][0:1], layers[l], (fmat, gmat, specs[l]), z_hg, z_ssd, z_gdn, final_g,
            B=bp, L=lp_len, tokens_per_mod=bp * lp_len, seg=lp_len, layer=l, final_norm=(l == depth - 1))
        hg_states.append(s_hg)
        ssd_states.append(s_ssd)
        gdn_states.append(s_gdn)
    y_prompt = x.reshape(bp, lp_len, D)
    new_hg = jnp.stack(hg_states, axis=1)
    new_ssd = jnp.stack(ssd_states, axis=1)
    new_gdn = jnp.stack(gdn_states, axis=1)

    fmat, gmat, specs = hyena_setup(ls_len)
    x = x_sample.reshape(bs * ls_len, D)
    for l in range(depth):
        x, _, _, _ = _layer_pass(
            x, mods[l][1:1 + bs], layers[l], (fmat, gmat, specs[l]),
            state_hgrn[:, l], state_ssd[:, l], state_gdn[:, l], final_g,
            B=bs, L=ls_len, tokens_per_mod=ls_len, seg=GRID_W, layer=l, final_norm=(l == depth - 1))
    y_sample = x.reshape(bs, ls_len, D)
    return (y_prompt, y_sample, new_hg, new_ssd, new_gdn)
```

```python
import functools
import math

import numpy as np
import jax
import jax.numpy as jnp
from jax import lax
from jax.experimental import pallas as pl
from jax.experimental.pallas import tpu as pltpu

F32 = jnp.float32
BF16 = jnp.bfloat16
I32 = jnp.int32
HIGHEST = lax.Precision.HIGHEST

D_MODEL = 1024
GRID_W = 64
EPS = 1e-6
LOG_FLOOR = 1e-30
MASK_NEG = -1e30
N_BRANCH = 4
D_BRANCH = 512
HY_POS_FREQS = 16
HY_FILTER_HIDDEN = 64
HY_FAST_DECAY = 0.3
HY_SLOW_DECAY = 1.5
HY_DECAY_TARGET = 1e-2
HG_HEADS = 4
HG_DK = 128
HG_CHUNK = 16
SSD_HEADS = 8
SSD_P = 64
SSD_N = 128
SSD_CHUNK = 64
GDN_HEADS = 4
GDN_DK = 128
GDN_CHUNK = 64
N_EXPERTS = 64
TOP_K = 8
N_EXPERT_GROUPS = 8
TOPK_GROUPS = 4
D_EXPERT = 256
D_SHARED = 256
ROUTED_SCALE = 2.5

LANES = 128
RUN_ROWS = 8

D_CONV = 4096
C_HYV, C_HYX1, C_HYX2 = 0, 512, 1024
C_SSDX, C_SSDB, C_SSDC = 1536, 2048, 2304
C_GQ, C_GK, C_GV = 2560, 3072, 3584
P_MERGE = 4096
P_HGQ, P_HGFF, P_HGFB, P_HGV, P_HGG = 8192, 8704, 9216, 9728, 10240
P_SSDZ, P_GDNG, P_SMALL = 10752, 11264, 11776
D_PROJ = 12288
S_DT, S_GA, S_GB = 0, 16, 24

VMEM_LIMIT = 56 * 1024 * 1024


def _cparams(sem):
    return pltpu.CompilerParams(dimension_semantics=sem, vmem_limit_bytes=VMEM_LIMIT)


def _sigmoid(x):
    return 1.0 / (1.0 + jnp.exp(-x))


def _silu(x):
    return x * _sigmoid(x)


def _softplus(x):
    return jnp.maximum(x, 0.0) + jnp.log(1.0 + jnp.exp(-jnp.abs(x)))


def _dot(a, b, precision=None):
    return jnp.dot(a, b, preferred_element_type=F32, precision=precision)


def _dot_nt(a, b, precision=None):
    return lax.dot_general(a, b, (((1,), (1,)), ((), ())), preferred_element_type=F32, precision=precision)


def _dot_tn(a, b, precision=None):
    return lax.dot_general(a, b, (((0,), (0,)), ((), ())), preferred_element_type=F32, precision=precision)


def _bdot(a, b):
    return _dot(a.astype(BF16), b.astype(BF16))


def _bdot_nt(a, b):
    return _dot_nt(a.astype(BF16), b.astype(BF16))


def _bdot_tn(a, b):
    return _dot_tn(a.astype(BF16), b.astype(BF16))


def _iota(shape, dim):
    return lax.broadcasted_iota(I32, shape, dim)


def _cumsum_rows(g, reverse):
    n = g.shape[0]
    row = _iota(g.shape, 0)
    sh = 1
    while sh < n:
        if reverse:
            g = g + jnp.where(row < n - sh, pltpu.roll(g, n - sh, 0), 0.0)
        else:
            g = g + jnp.where(row >= sh, pltpu.roll(g, sh, 0), 0.0)
        sh *= 2
    return g


def _lane_pick(a, j):
    return jnp.sum(jnp.where(_iota(a.shape, 1) == j, a, 0.0), axis=1, keepdims=True)


def _split3(a):
    a1 = a.astype(BF16)
    r1 = a - a1.astype(F32)
    a2 = r1.astype(BF16)
    a3 = (r1 - a2.astype(F32)).astype(BF16)
    return a1, a2, a3


def _dot3(a, b):
    a1, a2, _ = _split3(a)
    b1, b2, _ = _split3(b)
    return _dot(a1, b1) + (_dot(a1, b2) + _dot(a2, b1))


def _row_pick(a, j):
    sel = (_iota((8, a.shape[1]), 1) == j).astype(BF16)
    a1, a2, a3 = _split3(a)
    return (_dot_nt(sel, a1) + (_dot_nt(sel, a2) + _dot_nt(sel, a3)))[0:1, :]


def _tri_mask(n, reverse, strict=False):
    t = _iota((n, n), 0)
    s = _iota((n, n), 1)
    if reverse:
        return (s > t) if strict else (s >= t)
    return (s < t) if strict else (s <= t)


def _masked_exp(mask, diff):
    return jnp.where(mask, jnp.exp(jnp.where(mask, diff, 0.0)), 0.0)


def _mod_kernel(c_ref, w_ref, b_ref, o_ref):
    o_ref[...] = _dot(_silu(c_ref[...]), w_ref[...], precision=HIGHEST) + b_ref[...]


def _modulation(cond8, ada_w, ada_b, *, layer):
    rows = cond8.shape[0]
    tn = 1536
    n = ada_w.shape[2]
    return pl.pallas_call(
        _mod_kernel,
        grid=(n // tn,),
        in_specs=[pl.BlockSpec((rows, D_MODEL), lambda j: (0, 0)),
                  pl.BlockSpec((None, D_MODEL, tn), lambda j: (layer, 0, j)),
                  pl.BlockSpec((1, tn), lambda j: (0, j))],
        out_specs=pl.BlockSpec((rows, tn), lambda j: (0, j)),
        out_shape=jax.ShapeDtypeStruct((rows, n), F32),
        compiler_params=_cparams(("arbitrary",)),
        name="adaln_mod",
    )(cond8, ada_w, ada_b.reshape(1, n))


def _inproj_kernel(x_ref, mod_ref, g_ref, w_ref, cw_ref, cb_ref, o_ref, h_scr, *, seg, n_conv_tiles):
    j = pl.program_id(1)

    @pl.when(j == 0)
    def _():
        x = x_ref[...]
        xn = x * lax.rsqrt(jnp.mean(x * x, axis=-1, keepdims=True) + EPS) * g_ref[...]
        m = mod_ref[...]
        h_scr[...] = (xn * (1.0 + m[1:2]) + m[0:1]).astype(BF16)

    y = _dot(h_scr[...], w_ref[...])

    @pl.when(j < n_conv_tiles)
    def _():
        tm = y.shape[0]
        pos = _iota(y.shape, 0) & (seg - 1)
        prev = jnp.where(pos == 0, 0.0, pltpu.roll(y, 1, 0))
        nxt = jnp.where(pos == seg - 1, 0.0, pltpu.roll(y, tm - 1, 0))
        cw = cw_ref[...]
        o_ref[...] = cb_ref[...] + prev * cw[0:1] + y * cw[1:2] + nxt * cw[2:3]

    @pl.when(j >= n_conv_tiles)
    def _():
        o_ref[...] = y


def _in_projection(x, mod, norm_g, w_bf16, conv_w, conv_b, *, layer, tokens_per_mod, seg):
    T = x.shape[0]
    tm = min(2048, tokens_per_mod)
    tn = 1024
    n_conv_tiles = D_CONV // tn
    tiles_per_mod = tokens_per_mod // tm
    kern = functools.partial(_inproj_kernel, seg=seg, n_conv_tiles=n_conv_tiles)
    cmap = lambda i, j: (0, jnp.minimum(j, n_conv_tiles - 1))
    return pl.pallas_call(
        kern,
        grid=(T // tm, D_PROJ // tn),
        in_specs=[pl.BlockSpec((tm, D_MODEL), lambda i, j: (i, 0)),
                  pl.BlockSpec((None, 6, D_MODEL), lambda i, j: (i // tiles_per_mod, 0, 0)),
                  pl.BlockSpec((1, D_MODEL), lambda i, j: (0, 0)),
                  pl.BlockSpec((None, D_MODEL, tn), lambda i, j: (layer, 0, j)),
                  pl.BlockSpec((3, tn), cmap),
                  pl.BlockSpec((1, tn), cmap)],
        out_specs=pl.BlockSpec((tm, tn), lambda i, j: (i, j)),
        out_shape=jax.ShapeDtypeStruct((T, D_PROJ), F32),
        scratch_shapes=[pltpu.VMEM((tm, D_MODEL), BF16)],
        compiler_params=_cparams(("arbitrary", "arbitrary")),
        name="in_proj",
    )(x, mod, norm_g.reshape(1, D_MODEL), w_bf16, conv_w, conv_b.reshape(1, D_CONV))


def _hyfilt_kernel(w1_ref, b1_ref, w2_ref, b2_ref, w3_ref, o_ref, *, L):
    i = pl.program_id(1)
    tl = o_ref.shape[0]
    t = (_iota((tl, LANES), 0) + i * tl).astype(F32) / L
    lane = _iota((tl, LANES), 1)
    band = jnp.where(lane <= HY_POS_FREQS, lane, lane - HY_POS_FREQS).astype(F32)
    ang = 2.0 * math.pi * t * band
    feats = jnp.where(lane == 0, t,
                      jnp.where(lane <= HY_POS_FREQS, jnp.sin(ang),
                                jnp.where(lane <= 2 * HY_POS_FREQS, jnp.cos(ang), 0.0)))
    hdn = jnp.sin(_dot(feats, w1_ref[...], precision=HIGHEST) + b1_ref[...])
    hdn = jnp.sin(_dot(hdn, w2_ref[...], precision=HIGHEST) + b2_ref[...])
    filt = _dot(hdn, w3_ref[...], precision=HIGHEST)
    max_decay = math.log(HY_DECAY_TARGET) / HY_FAST_DECAY
    min_decay = math.log(HY_DECAY_TARGET) / HY_SLOW_DECAY
    n = filt.shape[1]
    ch = (_iota((tl, n), 1) & (D_BRANCH - 1)).astype(F32)
    delta = min_decay + ch * ((max_decay - min_decay) / (D_BRANCH - 1))
    tt = (_iota((tl, n), 0) + i * tl).astype(F32) / L
    o_ref[...] = filt * jnp.exp(-tt * jnp.abs(delta))


def _hyena_filters(L, w1p, b1, w2, b2, w3):
    tl = min(L, 256)
    n = 2 * D_BRANCH
    return pl.pallas_call(
        functools.partial(_hyfilt_kernel, L=L),
        grid=(2, L // tl),
        in_specs=[pl.BlockSpec((LANES, HY_FILTER_HIDDEN), lambda d, i: (0, 0)),
                  pl.BlockSpec((1, HY_FILTER_HIDDEN), lambda d, i: (0, 0)),
                  pl.BlockSpec((HY_FILTER_HIDDEN, HY_FILTER_HIDDEN), lambda d, i: (0, 0)),
                  pl.BlockSpec((1, HY_FILTER_HIDDEN), lambda d, i: (0, 0)),
                  pl.BlockSpec((HY_FILTER_HIDDEN, n), lambda d, i: (0, d))],
        out_specs=pl.BlockSpec((None, tl, n), lambda d, i: (d, i, 0)),
        out_shape=jax.ShapeDtypeStruct((2, L, n), F32),
        compiler_params=_cparams(("arbitrary", "arbitrary")),
        name="hyena_filters",
    )(w1p, b1.reshape(1, -1), w2, b2.reshape(1, -1), w3)


def _dftgen_kernel(f_ref, fs_ref, *, L, tk):
    i = pl.program_id(0)
    N = 2 * L
    k = _iota((tk, LANES), 0) + i * tk
    lane = _iota((tk, LANES), 1)
    w = 2.0 * math.pi / N
    a0 = ((k * lane) & (N - 1)).astype(F32) * w
    c0, s0 = jnp.cos(a0), jnp.sin(a0)
    a1 = ((k * lane * LANES) & (N - 1)).astype(F32) * w
    c1, s1 = jnp.cos(a1), jnp.sin(a1)
    alt = jnp.where((lane & 1) == 0, 1.0, -1.0)
    coef = jnp.where(k == 0, 1.0 / N, 2.0 / N)
    for n1 in range(L // LANES):
        c1b = c1[:, n1:n1 + 1]
        s1b = s1[:, n1:n1 + 1]
        cosb = c1b * c0 - s1b * s0
        sinb = jnp.where(k == 0, alt, -(s1b * c0 + c1b * s0))
        cols = slice(n1 * LANES, (n1 + 1) * LANES)
        f_ref[0:tk, cols] = cosb.astype(BF16)
        f_ref[tk:2 * tk, cols] = sinb.astype(BF16)
        fs_ref[0:tk, cols] = (coef * cosb).astype(BF16)
        fs_ref[tk:2 * tk, cols] = (coef * sinb).astype(BF16)


def _dft_matrices(L):
    tk = min(L, 256)
    spec = pl.BlockSpec((2 * tk, L), lambda i: (i, 0))
    return pl.pallas_call(
        functools.partial(_dftgen_kernel, L=L, tk=tk),
        grid=(L // tk,),
        in_specs=[],
        out_specs=[spec, spec],
        out_shape=[jax.ShapeDtypeStruct((2 * L, L), BF16)] * 2,
        compiler_params=_cparams(("arbitrary",)),
        name="dft_matrices",
    )()


def _dfth_kernel(f_ref, h_ref, o_ref, hp_scr, *, tk):
    i = pl.program_id(1)
    n = o_ref.shape[1]

    @pl.when(i == 0)
    def _():
        hc = h_ref[0]
        ha = jnp.where(_iota(hc.shape, 0) == 0, 0.0, h_ref[1])
        hp_scr[:, 0:n] = (hc + ha).astype(BF16)
        hp_scr[:, n:2 * n] = (hc - ha).astype(BF16)

    u = _dot(f_ref[...], hp_scr[...])
    r = _iota((2 * tk, n), 0)
    from_sum = (r < tk) | ((r == tk) & (i == 0))
    o_ref[...] = jnp.where(from_sum, u[:, 0:n], u[:, n:2 * n])


def _filter_spectrum(fmat, filt):
    L = fmat.shape[1]
    tk = min(L, 256)
    C = filt.shape[2]
    tc = 256
    return pl.pallas_call(
        functools.partial(_dfth_kernel, tk=tk),
        grid=(C // tc, L // tk),
        in_specs=[pl.BlockSpec((2 * tk, L), lambda c, i: (i, 0)),
                  pl.BlockSpec((2, L, tc), lambda c, i: (0, 0, c))],
        out_specs=pl.BlockSpec((2 * tk, tc), lambda c, i: (i, c)),
        out_shape=jax.ShapeDtypeStruct((2 * L, C), F32),
        scratch_shapes=[pltpu.VMEM((L, 2 * tc), BF16)],
        compiler_params=_cparams(("arbitrary", "arbitrary")),
        name="filter_spectrum",
    )(fmat, filt)


def _dfta_kernel(f_ref, u_ref, h_ref, y_ref, u_scr, *, tk):
    i = pl.program_id(1)

    @pl.when(i == 0)
    def _():
        u_scr[...] = u_ref[...].astype(BF16)

    uf = _dot(f_ref[...], u_scr[...])
    ur, ui = uf[0:tk], uf[tk:2 * tk]
    hr, hi = h_ref[0:tk, :], h_ref[tk:2 * tk, :]
    dc = (_iota(ur.shape, 0) == 0) & (i == 0)
    y_ref[0:tk, :] = jnp.where(dc, ur * hr, ur * hr - ui * hi).astype(BF16)
    y_ref[tk:2 * tk, :] = jnp.where(dc, ui * hi, ur * hi + ui * hr).astype(BF16)


def _dftb_kernel(g_ref, y_ref, u_ref, x_ref, b_ref, o_ref):
    y = _dot(g_ref[...], y_ref[...])
    o_ref[...] = x_ref[...] * (y + u_ref[...] * b_ref[...])


def _spectral_conv(fmat, gmat, hspec, h_col, bias, u_arr, u_col, gate_arr, gate_col):
    B, L = u_arr.shape[0], u_arr.shape[1]
    C = D_BRANCH
    tk = min(L, 256)
    y = pl.pallas_call(
        functools.partial(_dfta_kernel, tk=tk),
        grid=(B, L // tk),
        in_specs=[pl.BlockSpec((2 * tk, L), lambda b, i: (i, 0)),
                  pl.BlockSpec((None, L, C), lambda b, i: (b, 0, u_col)),
                  pl.BlockSpec((2 * tk, C), lambda b, i: (i, h_col))],
        out_specs=pl.BlockSpec((None, 2 * tk, C), lambda b, i: (b, i, 0)),
        out_shape=jax.ShapeDtypeStruct((B, 2 * L, C), BF16),
        scratch_shapes=[pltpu.VMEM((L, C), BF16)],
        compiler_params=_cparams(("arbitrary", "arbitrary")),
        name="hyena_dft_fwd",
    )(fmat, u_arr, hspec)
    tr = min(L, 256)
    return pl.pallas_call(
        _dftb_kernel,
        grid=(B, L // tr),
        in_specs=[pl.BlockSpec((tr, 2 * L), lambda b, i: (i, 0)),
                  pl.BlockSpec((None, 2 * L, C), lambda b, i: (b, 0, 0)),
                  pl.BlockSpec((None, tr, C), lambda b, i: (b, i, u_col)),
                  pl.BlockSpec((None, tr, C), lambda b, i: (b, i, gate_col)),
                  pl.BlockSpec((1, C), lambda b, i: (0, 0))],
        out_specs=pl.BlockSpec((None, tr, C), lambda b, i: (b, i, 0)),
        out_shape=jax.ShapeDtypeStruct((B, L, C), F32),
        compiler_params=_cparams(("arbitrary", "arbitrary")),
        name="hyena_dft_inv",
    )(gmat, y, u_arr, gate_arr, bias.reshape(1, C))


def _hgrn_kernel(q_ref, ff_ref, fb_ref, v_ref, g_ref, lb_ref, nrm_ref, s0_ref, y_ref, sf_ref,
                 of_scr, ob_scr, st_scr, *, L, layer, depth):
    C = HG_CHUNK
    W = HG_DK
    nc = L // C
    nh = q_ref.shape[1] // W
    ridx = _iota((C, W), 0)
    o_scrs = (of_scr, ob_scr)
    f_refs = (ff_ref, fb_ref)

    def lower_bound(d, cols):
        rows = [lb_ref[d, l:l + 1, cols] for l in range(depth)]
        m = rows[0]
        for r in rows[1:]:
            m = jnp.maximum(m, r)
        es = [jnp.exp(r - m) for r in rows]
        tot = es[0]
        for e in es[1:]:
            tot = tot + e
        acc = es[0] / tot
        for e in es[1:layer + 1]:
            acc = acc + e / tot
        return acc - es[0] / tot

    lbs = [[lower_bound(d, slice(hh * W, (hh + 1) * W)) for hh in range(nh)] for d in (0, 1)]
    for d in (0, 1):
        for hh in range(nh):
            st_scr[d, hh] = s0_ref[d, hh].T

    def step(c, d, hh):
        rev = d == 1
        rows = pl.ds(pl.multiple_of(c * C, C), C)
        cols = slice(hh * W, (hh + 1) * W)
        lb = lbs[d][hh]
        q = _silu(q_ref[rows, cols])
        uf = f_refs[d][rows, cols]
        v = v_ref[rows, cols]
        f = lb + (1.0 - lb) * _sigmoid(uf)
        g = jnp.log(jnp.maximum(f, LOG_FLOOR))
        kin = (1.0 - lb) * _sigmoid(-uf)
        b = _cumsum_rows(g, rev)
        st = st_scr[d, hh]
        o = _bdot_nt(q * jnp.exp(b), st)
        intra = jnp.zeros((C, W), F32)
        for t in range(C):
            mask = (ridx >= t) if rev else (ridx <= t)
            pair = _masked_exp(mask, b[t:t + 1, :] - b)
            a = jnp.sum(pair * (q[t:t + 1, :] * kin), axis=1, keepdims=True)
            row = jnp.sum(a * v, axis=0, keepdims=True)
            intra = jnp.where(ridx == t, row, intra)
        b_end = b[0:1, :] if rev else b[C - 1:C, :]
        st_scr[d, hh] = st * jnp.exp(b_end) + _bdot_tn(v, kin * jnp.exp(b_end - b))
        o_scrs[d][rows, cols] = o + intra

    def body(ci, carry):
        for d in (0, 1):
            for hh in range(nh):
                step((nc - 1 - ci) if d == 1 else ci, d, hh)
        return carry

    lax.fori_loop(0, nc, body, 0)
    for d in (0, 1):
        for hh in range(nh):
            sf_ref[d, hh] = st_scr[d, hh].T

    tr = min(L, 256)

    def fin(i, carry):
        rows = pl.ds(pl.multiple_of(i * tr, tr), tr)
        for hh in range(nh):
            cols = slice(hh * W, (hh + 1) * W)
            o = of_scr[rows, cols] + ob_scr[rows, cols]
            on = o * lax.rsqrt(jnp.mean(o * o, axis=-1, keepdims=True) + EPS) * nrm_ref[...]
            y_ref[rows, cols] = on * _silu(g_ref[rows, cols])
        return carry

    lax.fori_loop(0, L // tr, fin, 0)


def _hgrn_scan(proj3, hg_lb, hg_norm, s0, *, layer):
    B, L = proj3.shape[0], proj3.shape[1]
    depth = hg_lb.shape[1]
    nh = 2
    W = nh * HG_DK
    col = lambda base: (lambda b, h: (b, 0, base // W + h))
    st_spec = pl.BlockSpec((None, 2, nh, HG_DK, HG_DK), lambda b, h: (b, 0, h, 0, 0))
    return pl.pallas_call(
        functools.partial(_hgrn_kernel, L=L, layer=layer, depth=depth),
        grid=(B, HG_HEADS // nh),
        in_specs=[pl.BlockSpec((None, L, W), col(P_HGQ)),
                  pl.BlockSpec((None, L, W), col(P_HGFF)),
                  pl.BlockSpec((None, L, W), col(P_HGFB)),
                  pl.BlockSpec((None, L, W), col(P_HGV)),
                  pl.BlockSpec((None, L, W), col(P_HGG)),
                  pl.BlockSpec((2, depth, W), lambda b, h: (0, 0, h)),
                  pl.BlockSpec((1, HG_DK), lambda b, h: (0, 0)),
                  st_spec],
        out_specs=[pl.BlockSpec((None, L, W), lambda b, h: (b, 0, h)), st_spec],
        out_shape=[jax.ShapeDtypeStruct((B, L, D_BRANCH), F32),
                   jax.ShapeDtypeStruct(s0.shape, F32)],
        scratch_shapes=[pltpu.VMEM((L, W), F32), pltpu.VMEM((L, W), F32),
                        pltpu.VMEM((2, nh, HG_DK, HG_DK), F32)],
        compiler_params=_cparams(("arbitrary", "arbitrary")),
        name="hgrn2_scan",
    )(proj3, proj3, proj3, proj3, proj3, hg_lb, hg_norm.reshape(1, HG_DK), s0)


def _ssd_kernel(x_ref, bm_ref, cm_ref, z_ref, sm_ref, bias_ref, alog_ref, dskip_ref, s0_ref,
                y_ref, sf_ref, of_scr, ob_scr, st_scr, *, L):
    C = SSD_CHUNK
    W = LANES
    nc = L // C
    npairs = x_ref.shape[1] // W
    grp = pl.program_id(1)
    lane_lo = _iota((C, W), 1) < SSD_P
    row_lo = _iota((W, SSD_N), 0) < SSD_P
    pick2 = lambda a0, a1: jnp.where(lane_lo, a0, a1)
    o_scrs = (of_scr, ob_scr)
    incls = (_tri_mask(C, False), _tri_mask(C, True))
    for d in (0, 1):
        for pp in range(npairs):
            st_scr[d, pp] = s0_ref[d, pp]

    heads = [(d, pp, hh) for d in (0, 1) for pp in range(npairs) for hh in (0, 1)]
    pairs = [(d, pp) for d in (0, 1) for pp in range(npairs)]

    def body(ci, carry):
        rows_d, bm, cm, dt_all, cum_all, gram = [], [], [], [], [], []
        for d in (0, 1):
            c = (nc - 1 - ci) if d == 1 else ci
            rows = pl.ds(pl.multiple_of(c * C, C), C)
            rows_d.append(rows)
            bm.append(_silu(bm_ref[rows, :]))
            cm.append(_silu(cm_ref[rows, :]))
            dt_all.append(_softplus(sm_ref[rows, :] + bias_ref[...]))
            cum_all.append(_cumsum_rows(-jnp.exp(alog_ref[...]) * dt_all[d], d == 1))
        lane_of = lambda u: S_DT + u[0] * SSD_HEADS + 2 * (npairs * grp + u[1]) + u[2]
        cumr = {u: _row_pick(cum_all[u[0]], lane_of(u)) for u in heads}
        for d in (0, 1):
            gram.append(_bdot_nt(cm[d], bm[d]))
        dtc = {u: _lane_pick(dt_all[u[0]], lane_of(u)) for u in heads}
        cumc = {u: _lane_pick(cum_all[u[0]], lane_of(u)) for u in heads}
        end = {u: (cumc[u][0:1, :] if u[0] == 1 else cumc[u][C - 1:C, :]) for u in heads}
        xs = {pr: _silu(x_ref[rows_d[pr[0]], slice(pr[1] * W, (pr[1] + 1) * W)]) for pr in pairs}
        st = {pr: st_scr[pr[0], pr[1]] for pr in pairs}
        y_state = {u: _bdot_nt(cm[u[0]] * jnp.exp(cumc[u]), st[u[:2]]) for u in heads}
        scores = {u: gram[u[0]] * _masked_exp(incls[u[0]], cumc[u] - cumr[u]) for u in heads}
        y_intra = {}
        for u in heads:
            head_x = jnp.where(lane_lo if u[2] == 0 else ~lane_lo, xs[u[:2]], 0.0) * dtc[u]
            y_intra[u] = _bdot(scores[u], head_x)
        for pr in pairs:
            u0, u1 = pr + (0,), pr + (1,)
            xdt_e = xs[pr] * pick2(dtc[u0] * jnp.exp(end[u0] - cumc[u0]), dtc[u1] * jnp.exp(end[u1] - cumc[u1]))
            st_scr[pr[0], pr[1]] = (st[pr] * jnp.where(row_lo, jnp.exp(end[u0]), jnp.exp(end[u1]))
                                    + _bdot_tn(xdt_e, bm[pr[0]]))
            o_scrs[pr[0]][rows_d[pr[0]], slice(pr[1] * W, (pr[1] + 1) * W)] = pick2(
                y_intra[u0] + y_state[u0], y_intra[u1] + y_state[u1])
        return carry

    lax.fori_loop(0, nc, body, 0)
    for d in (0, 1):
        for pp in range(npairs):
            sf_ref[d, pp] = st_scr[d, pp]

    tr = min(L, 256)

    def fin(i, carry):
        rows = pl.ds(pl.multiple_of(i * tr, tr), tr)
        y = of_scr[rows, :] + ob_scr[rows, :] + _silu(x_ref[rows, :]) * dskip_ref[...]
        y_ref[rows, :] = y * _silu(z_ref[rows, :])
        return carry

    lax.fori_loop(0, L // tr, fin, 0)


def _ssd_scan(proj3, bias_row, alog_row, ssd_d, s0):
    B, L = proj3.shape[0], proj3.shape[1]
    n_groups = 2
    n_pairs = SSD_HEADS // 2
    ppg = n_pairs // n_groups
    W = ppg * LANES
    s0p = s0.reshape(B, 2, n_pairs, 2 * SSD_P, SSD_N)
    dskip = jnp.repeat(ssd_d, SSD_P).reshape(n_groups, 1, W)
    col = lambda base: (lambda b, g: (b, 0, base // W + g))
    grp = lambda base: (lambda b, g: (b, 0, base // LANES + g))
    st_spec = pl.BlockSpec((None, 2, ppg, LANES, SSD_N), lambda b, g: (b, 0, g, 0, 0))
    y, sf = pl.pallas_call(
        functools.partial(_ssd_kernel, L=L),
        grid=(B, n_groups),
        in_specs=[pl.BlockSpec((None, L, W), col(C_SSDX)),
                  pl.BlockSpec((None, L, LANES), grp(C_SSDB)),
                  pl.BlockSpec((None, L, LANES), grp(C_SSDC)),
                  pl.BlockSpec((None, L, W), col(P_SSDZ)),
                  pl.BlockSpec((None, L, LANES), lambda b, g: (b, 0, P_SMALL // LANES)),
                  pl.BlockSpec((1, LANES), lambda b, g: (0, 0)),
                  pl.BlockSpec((1, LANES), lambda b, g: (0, 0)),
                  pl.BlockSpec((None, 1, W), lambda b, g: (g, 0, 0)),
                  st_spec],
        out_specs=[pl.BlockSpec((None, L, W), lambda b, g: (b, 0, g)), st_spec],
        out_shape=[jax.ShapeDtypeStruct((B, L, D_BRANCH), F32),
                   jax.ShapeDtypeStruct(s0p.shape, F32)],
        scratch_shapes=[pltpu.VMEM((L, W), F32), pltpu.VMEM((L, W), F32),
                        pltpu.VMEM((2, ppg, LANES, SSD_N), F32)],
        compiler_params=_cparams(("arbitrary", "arbitrary")),
        name="ssd_scan",
    )(proj3, proj3, proj3, proj3, proj3, bias_row, alog_row, dskip, s0p)
    return y, sf.reshape(s0.shape)


def _l2norm(a):
    return a * lax.rsqrt(jnp.sum(a * a, axis=-1, keepdims=True) + EPS)


def _gdn_kernel(q_ref, k_ref, v_ref, g_ref, sm_ref, bias_ref, alog_ref, nrm_ref, s0_ref,
                y_ref, sf_ref, of_scr, ob_scr, st_scr, *, L):
    C = GDN_CHUNK
    W = GDN_DK
    nc = L // C
    nh = q_ref.shape[1] // W
    hblk = pl.program_id(1)
    o_scrs = (of_scr, ob_scr)
    incls = (_tri_mask(C, False), _tri_mask(C, True))
    stricts = (_tri_mask(C, False, strict=True), _tri_mask(C, True, strict=True))
    for d in (0, 1):
        for hh in range(nh):
            st_scr[d, hh] = s0_ref[d, hh]

    units = [(d, hh) for d in (0, 1) for hh in range(nh)]
    eye = (_iota((C, C), 0) == _iota((C, C), 1)).astype(F32)

    def body(ci, carry):
        rows_d, cum_d, beta_d = [], [], []
        for d in (0, 1):
            c = (nc - 1 - ci) if d == 1 else ci
            rows = pl.ds(pl.multiple_of(c * C, C), C)
            raw = sm_ref[rows, :]
            rows_d.append(rows)
            cum_d.append(_cumsum_rows(-jnp.exp(alog_ref[...]) * _softplus(raw + bias_ref[...]), d == 1))
            beta_d.append(_sigmoid(raw))
        q, k, v, gc, grow, beta = {}, {}, {}, {}, {}, {}
        for u in units:
            d, hh = u
            cols = slice(hh * W, (hh + 1) * W)
            h = nh * hblk + hh
            q[u] = _l2norm(_silu(q_ref[rows_d[d], cols])) * (GDN_DK ** -0.5)
            k[u] = _l2norm(_silu(k_ref[rows_d[d], cols]))
            v[u] = _silu(v_ref[rows_d[d], cols])
            gc[u] = _lane_pick(cum_d[d], S_GA + d * GDN_HEADS + h)
            beta[u] = _lane_pick(beta_d[d], S_GB + d * GDN_HEADS + h)
        for u in units:
            grow[u] = _row_pick(cum_d[u[0]], S_GA + u[0] * GDN_HEADS + nh * hblk + u[1])
        decay = {u: _masked_exp(incls[u[0]], gc[u] - grow[u]) for u in units}
        kb = {u: k[u] * beta[u] for u in units}
        m = {u: -jnp.where(stricts[u[0]], _bdot_nt(kb[u], k[u]) * decay[u], 0.0) for u in units}
        aqk = {u: _bdot_nt(q[u], k[u]) * decay[u] for u in units}
        p = {u: eye + m[u] for u in units}
        m = {u: _dot3(m[u], m[u]) for u in units}
        sh = 4
        while sh < C:
            both = {u: _dot3(jnp.concatenate([m[u], p[u]], axis=0), m[u]) for u in units}
            p = {u: p[u] + both[u][C:2 * C] for u in units}
            m = {u: both[u][0:C] for u in units}
            sh *= 2
        p = {u: p[u] + _dot3(p[u], m[u]) for u in units}
        rhs = {u: jnp.concatenate([v[u] * beta[u], kb[u] * jnp.exp(gc[u])], axis=1) for u in units}
        uw = {u: _dot3(p[u], rhs[u]) for u in units}
        uu = {u: uw[u][:, 0:W] for u in units}
        ww = {u: uw[u][:, W:2 * W] for u in units}
        st = {u: st_scr[u[0], u[1]] for u in units}
        v_new = {u: uu[u] - _bdot(ww[u], st[u]) for u in units}
        o = {u: _bdot(q[u] * jnp.exp(gc[u]), st[u]) + _bdot(aqk[u], v_new[u]) for u in units}
        for u in units:
            d, hh = u
            g_end = gc[u][0:1, :] if d == 1 else gc[u][C - 1:C, :]
            st_scr[d, hh] = st[u] * jnp.exp(g_end) + _bdot_tn(k[u] * jnp.exp(g_end - gc[u]), v_new[u])
            o_scrs[d][rows_d[d], slice(hh * W, (hh + 1) * W)] = o[u]
        return carry

    lax.fori_loop(0, nc, body, 0)
    for d in (0, 1):
        for hh in range(nh):
            sf_ref[d, hh] = st_scr[d, hh]

    tr = min(L, 256)

    def fin(i, carry):
        rows = pl.ds(pl.multiple_of(i * tr, tr), tr)
        for hh in range(nh):
            cols = slice(hh * W, (hh + 1) * W)
            o = of_scr[rows, cols] + ob_scr[rows, cols]
            on = o * lax.rsqrt(jnp.mean(o * o, axis=-1, keepdims=True) + EPS) * nrm_ref[...]
            y_ref[rows, cols] = on * _silu(g_ref[rows, cols])
        return carry

    lax.fori_loop(0, L // tr, fin, 0)


def _gdn_scan(proj3, bias_row, alog_row, gdn_norm, s0):
    B, L = proj3.shape[0], proj3.shape[1]
    nh = GDN_HEADS
    W = nh * GDN_DK
    col = lambda base: (lambda b, h: (b, 0, base // W + h))
    st_spec = pl.BlockSpec((None, 2, nh, GDN_DK, GDN_DK), lambda b, h: (b, 0, h, 0, 0))
    once = pl.Buffered(1)
    return pl.pallas_call(
        functools.partial(_gdn_kernel, L=L),
        grid=(B, GDN_HEADS // nh),
        in_specs=[pl.BlockSpec((None, L, W), col(C_GQ), pipeline_mode=once),
                  pl.BlockSpec((None, L, W), col(C_GK), pipeline_mode=once),
                  pl.BlockSpec((None, L, W), col(C_GV), pipeline_mode=once),
                  pl.BlockSpec((None, L, W), col(P_GDNG), pipeline_mode=once),
                  pl.BlockSpec((None, L, LANES), lambda b, h: (b, 0, P_SMALL // LANES), pipeline_mode=once),
                  pl.BlockSpec((1, LANES), lambda b, h: (0, 0)),
                  pl.BlockSpec((1, LANES), lambda b, h: (0, 0)),
                  pl.BlockSpec((1, GDN_DK), lambda b, h: (0, 0)),
                  st_spec],
        out_specs=[pl.BlockSpec((None, L, W), lambda b, h: (b, 0, h)), st_spec],
        out_shape=[jax.ShapeDtypeStruct((B, L, D_BRANCH), F32),
                   jax.ShapeDtypeStruct(s0.shape, F32)],
        scratch_shapes=[pltpu.VMEM((L, W), F32), pltpu.VMEM((L, W), F32),
                        pltpu.VMEM((2, nh, GDN_DK, GDN_DK), F32)],
        compiler_params=_cparams(("arbitrary", "arbitrary")),
        name="gdn_scan",
    )(proj3, proj3, proj3, proj3, proj3, bias_row, alog_row, gdn_norm.reshape(1, GDN_DK), s0)


def _merge_kernel(ya_ref, yb_ref, yc_ref, yd_ref, mg_ref, x_ref, mod_ref, sn_ref, n2_ref, wb_ref, wo_ref,
                  xo_ref, h_ref):
    yc = yc_ref[...]
    half = D_BRANCH // 2
    parts = []
    for gidx in range(2):
        seg = yc[:, gidx * half:(gidx + 1) * half]
        parts.append(seg * lax.rsqrt(jnp.mean(seg * seg, axis=-1, keepdims=True) + EPS)
                     * sn_ref[:, gidx * half:(gidx + 1) * half])
    branches = (ya_ref[...], yb_ref[...], None, yd_ref[...])
    mixed = None
    for n in range(N_BRANCH):
        gate = _sigmoid(mg_ref[:, n * D_MODEL:(n + 1) * D_MODEL])
        if n == 2:
            lifted = (_dot(parts[0].astype(BF16), wb_ref[n, 0:half, :])
                      + _dot(parts[1].astype(BF16), wb_ref[n, half:D_BRANCH, :]))
        else:
            lifted = _dot(branches[n].astype(BF16), wb_ref[n])
        mixed = gate * lifted if mixed is None else mixed + gate * lifted
    m = mod_ref[...]
    x = x_ref[...] + m[2:3] * _dot(mixed.astype(BF16), wo_ref[...])
    xo_ref[...] = x
    xn = x * lax.rsqrt(jnp.mean(x * x, axis=-1, keepdims=True) + EPS) * n2_ref[...]
    h_ref[...] = xn * (1.0 + m[4:5]) + m[3:4]


def _merge(ya, yb, yc, yd, proj, x, mod, ssd_norm, norm2_g, wb_bf16, wo_bf16, *, tokens_per_mod):
    T = x.shape[0]
    tm = min(512, tokens_per_mod)
    tiles_per_mod = tokens_per_mod // tm
    yspec = pl.BlockSpec((tm, D_BRANCH), lambda i: (i, 0))
    xspec = pl.BlockSpec((tm, D_MODEL), lambda i: (i, 0))
    return pl.pallas_call(
        _merge_kernel,
        grid=(T // tm,),
        in_specs=[yspec, yspec, yspec, yspec,
                  pl.BlockSpec((tm, N_BRANCH * D_MODEL), lambda i: (i, P_MERGE // (N_BRANCH * D_MODEL))),
                  xspec,
                  pl.BlockSpec((None, 6, D_MODEL), lambda i: (i // tiles_per_mod, 0, 0)),
                  pl.BlockSpec((1, D_BRANCH), lambda i: (0, 0)),
                  pl.BlockSpec((1, D_MODEL), lambda i: (0, 0)),
                  pl.BlockSpec((N_BRANCH, D_BRANCH, D_MODEL), lambda i: (0, 0, 0)),
                  pl.BlockSpec((D_MODEL, D_MODEL), lambda i: (0, 0))],
        out_specs=[xspec, xspec],
        out_shape=[jax.ShapeDtypeStruct((T, D_MODEL), F32)] * 2,
        compiler_params=_cparams(("arbitrary",)),
        name="merge_outproj",
    )(ya, yb, yc, yd, proj, x, mod, ssd_norm.reshape(1, D_BRANCH), norm2_g.reshape(1, D_MODEL), wb_bf16, wo_bf16)


def _router_kernel(h_ref, rw_ref, rb_ref, wt_ref, lpos_ref, tlen_ref, tstart_ref, tcar_ref, carry):
    i = pl.program_id(0)
    tm = h_ref.shape[0]
    E = N_EXPERTS
    gsz = E // N_EXPERT_GROUPS
    neg_inf = -jnp.inf

    @pl.when(i == 0)
    def _():
        carry[...] = jnp.zeros_like(carry)

    scores = _sigmoid(_dot_nt(rw_ref[...], h_ref[...], precision=HIGHEST))
    biased = scores + rb_ref[...]
    eidx = _iota((E, tm), 0)
    ridx = _iota((gsz, tm), 0)

    slabs = [biased[g * gsz:(g + 1) * gsz, :] for g in range(N_EXPERT_GROUPS)]
    gs = []
    for v in slabs:
        m1 = jnp.max(v, axis=0, keepdims=True)
        i1 = jnp.min(jnp.where(v == m1, ridx, gsz), axis=0, keepdims=True)
        m2 = jnp.max(jnp.where(ridx == i1, neg_inf, v), axis=0, keepdims=True)
        gs.append(m1 + m2)
    masked = []
    for g in range(N_EXPERT_GROUPS):
        rank = jnp.zeros((1, tm), I32)
        for g2 in range(N_EXPERT_GROUPS):
            if g2 == g:
                continue
            ahead = (gs[g2] >= gs[g]) if g2 < g else (gs[g2] > gs[g])
            rank = rank + ahead.astype(I32)
        masked.append(jnp.where(rank < TOPK_GROUPS, slabs[g], MASK_NEG))
    cur = jnp.concatenate(masked, axis=0)

    krow = _iota((TOP_K, tm), 0)
    onehot = jnp.zeros((E, tm), F32)
    picks = []
    wsum = jnp.zeros((1, tm), F32)
    for kk in range(TOP_K):
        m = jnp.max(cur, axis=0, keepdims=True)
        ik = jnp.min(jnp.where(cur == m, eidx, E), axis=0, keepdims=True)
        hit = eidx == ik
        wk = jnp.sum(jnp.where(hit, scores, 0.0), axis=0, keepdims=True)
        cur = jnp.where(hit, neg_inf, cur)
        onehot = onehot + hit.astype(F32)
        picks.append((ik, wk))
        wsum = wsum + wk
    wt_out = jnp.zeros((TOP_K, tm), F32)
    for kk, (ik, wk) in enumerate(picks):
        wt_out = jnp.where(krow == kk, wk / wsum * ROUTED_SCALE, wt_out)

    earlier = (_iota((tm, tm), 0) < _iota((tm, tm), 1)).astype(BF16)
    before = _dot(onehot.astype(BF16), earlier)
    pad = lambda c: (c.astype(I32) + (RUN_ROWS - 1)) // RUN_ROWS * RUN_ROWS
    run_len = pad(jnp.sum(onehot, axis=1, keepdims=True) + jnp.zeros((E, LANES), F32)).astype(F32)
    run_start = _cumsum_rows(run_len, False) - run_len
    local = before + run_start[:, 0:1]
    lpos_out = jnp.zeros((TOP_K, tm), I32)
    for kk, (ik, wk) in enumerate(picks):
        lp = jnp.sum(jnp.where(eidx == ik, local, 0.0), axis=0, keepdims=True)
        lpos_out = jnp.where(krow == kk, lp.astype(I32), lpos_out)
    wt_ref[...] = wt_out
    lpos_ref[...] = lpos_out

    onehot_l = jnp.concatenate([onehot, jnp.zeros((LANES - E, tm), F32)], axis=0).astype(BF16)
    len_row = pad(_dot_nt(jnp.ones((8, tm), BF16), onehot_l)).astype(F32)
    lanes_before = (_iota((LANES, LANES), 0) < _iota((LANES, LANES), 1)).astype(BF16)
    tlen_ref[...] = len_row.astype(I32)
    tstart_ref[...] = _dot(len_row.astype(BF16), lanes_before).astype(I32)
    tcar_ref[...] = carry[...].astype(I32)
    carry[...] = carry[...] + len_row


def _router(h2, router_w, router_b, *, tm):
    T = h2.shape[0]
    nt = T // tm
    ospec = pl.BlockSpec((TOP_K, tm), lambda i: (0, i))
    tspec = pl.BlockSpec((None, 8, LANES), lambda i: (i, 0, 0))
    tshape = jax.ShapeDtypeStruct((nt, 8, LANES), I32)
    wt, lpos, tlen, tstart, tcar = pl.pallas_call(
        _router_kernel,
        grid=(nt,),
        in_specs=[pl.BlockSpec((tm, D_MODEL), lambda i: (i, 0)),
                  pl.BlockSpec((N_EXPERTS, D_MODEL), lambda i: (0, 0)),
                  pl.BlockSpec((N_EXPERTS, 1), lambda i: (0, 0))],
        out_specs=[ospec, ospec, tspec, tspec, tspec],
        out_shape=[jax.ShapeDtypeStruct((TOP_K, T), F32), jax.ShapeDtypeStruct((TOP_K, T), I32),
                   tshape, tshape, tshape],
        scratch_shapes=[pltpu.VMEM((8, LANES), F32)],
        compiler_params=_cparams(("arbitrary",)),
        name="moe_router",
    )(h2, router_w.T, router_b.reshape(N_EXPERTS, 1))
    table = lambda a: a[:, 0, :N_EXPERTS]
    return wt, lpos, table(tlen), table(tstart), table(tcar)


def _dispatch_kernel(len_ref, lst_ref, car_ref, seg_ref, lpos_ref, h_ref, xs_ref, xloc, sem):
    i = pl.program_id(0)
    tm = h_ref.shape[0]
    nloc = xloc.shape[0]
    R = RUN_ROWS
    lpos = lpos_ref[...]
    hb = h_ref[...].astype(BF16)
    rb = 256
    off = _iota((rb, tm), 0).astype(F32).astype(BF16)
    one = jnp.ones((rb, tm), BF16)
    for blk in range(nloc // rb):
        cand = jnp.where(lpos // rb == blk, lpos % rb, -1).astype(F32).astype(BF16)
        owner = jnp.zeros((rb, tm), BF16)
        for kk in range(TOP_K):
            owner = jnp.where(off == cand[kk:kk + 1, :], one, owner)
        xloc[blk * rb:(blk + 1) * rb, :] = _dot(owner, hb)

    def piece(src_row, dst_row):
        return pltpu.make_async_copy(xloc.at[pl.ds(pl.multiple_of(src_row, R), R), :],
                                     xs_ref.at[pl.ds(pl.multiple_of(dst_row, R), R), :], sem)

    def per_expert(e, issued):
        pieces = len_ref[i, e] // R
        ls = lst_ref[i, e]
        gd = seg_ref[e] + car_ref[i, e]

        def issue(j, c):
            piece(ls + j * R, gd + j * R).start()
            return c

        lax.fori_loop(0, pieces, issue, 0)
        return issued + pieces

    total = lax.fori_loop(0, N_EXPERTS, per_expert, 0)

    def wait(j, c):
        piece(0, 0).wait()
        return c

    lax.fori_loop(0, total, wait, 0)


def _dispatch(h2, lpos, tile_len, tile_lstart, tile_car, seg_start, *, tm):
    T = h2.shape[0]
    nt = T // tm
    n_rows = T * TOP_K + N_EXPERTS * nt * RUN_ROWS
    nloc = tm * TOP_K + N_EXPERTS * RUN_ROWS
    return pl.pallas_call(
        _dispatch_kernel,
        grid_spec=pltpu.PrefetchScalarGridSpec(
            num_scalar_prefetch=4,
            grid=(nt,),
            in_specs=[pl.BlockSpec((TOP_K, tm), lambda i, *_: (0, i)),
                      pl.BlockSpec((tm, D_MODEL), lambda i, *_: (i, 0))],
            out_specs=pl.BlockSpec(memory_space=pl.ANY),
            scratch_shapes=[pltpu.VMEM((nloc, D_MODEL), F32), pltpu.SemaphoreType.DMA(())]),
        out_shape=jax.ShapeDtypeStruct((n_rows, D_MODEL), F32),
        compiler_params=_cparams(("arbitrary",)),
        name="moe_dispatch",
    )(tile_len, tile_lstart, tile_car, seg_start, lpos, h2)


def _expert_kernel(blk_ref, exp_ref, lo_ref, hi_ref, first_ref, x_ref, w1_ref, w3_ref, w2_ref, o_ref):
    w = pl.program_id(0)
    x = x_ref[...].astype(BF16)
    a = _dot(x, w1_ref[...].astype(BF16))
    b = _dot(x, w3_ref[...].astype(BF16))
    y = _dot((_silu(a) * b).astype(BF16), w2_ref[...].astype(BF16))
    r = _iota(y.shape, 0)
    y = jnp.where((r >= lo_ref[w]) & (r < hi_ref[w]), y, 0.0)

    @pl.when(first_ref[w] == 1)
    def _():
        o_ref[...] = y

    @pl.when(first_ref[w] == 0)
    def _():
        o_ref[...] = o_ref[...] + y


def _experts(xs, work, w1, w3, w2, *, layer, bm):
    A = xs.shape[0]
    n_work = work[0].shape[0]
    xmap = lambda w, blk, ex, lo, hi, first: (blk[w], 0)
    wmap = lambda w, blk, ex, lo, hi, first: (layer, ex[w], 0, 0)
    return pl.pallas_call(
        _expert_kernel,
        grid_spec=pltpu.PrefetchScalarGridSpec(
            num_scalar_prefetch=5,
            grid=(n_work,),
            in_specs=[pl.BlockSpec((bm, D_MODEL), xmap),
                      pl.BlockSpec((None, None, D_MODEL, D_EXPERT), wmap),
                      pl.BlockSpec((None, None, D_MODEL, D_EXPERT), wmap),
                      pl.BlockSpec((None, None, D_EXPERT, D_MODEL), wmap)],
            out_specs=pl.BlockSpec((bm, D_MODEL), xmap)),
        out_shape=jax.ShapeDtypeStruct((A, D_MODEL), F32),
        compiler_params=_cparams(("arbitrary",)),
        name="moe_experts",
    )(*work, xs, w1, w3, w2)


def _combine_kernel(len_ref, lst_ref, car_ref, seg_ref, lpos_ref, wt_ref, ys_ref, h_ref, x_ref, mod_ref,
                    s1_ref, s3_ref, s2_ref, fg_ref, o_ref, yloc, sem, *, final_norm):
    i = pl.program_id(0)
    tm = h_ref.shape[0]
    nloc = yloc.shape[0]
    R = RUN_ROWS

    @pl.when(i == 0)
    def _():
        yloc[...] = jnp.zeros_like(yloc)

    def piece(src_row, dst_row):
        return pltpu.make_async_copy(ys_ref.at[pl.ds(pl.multiple_of(src_row, R), R), :],
                                     yloc.at[pl.ds(pl.multiple_of(dst_row, R), R), :], sem)

    def per_expert(e, issued):
        pieces = len_ref[i, e] // R
        ls = lst_ref[i, e]
        gd = seg_ref[e] + car_ref[i, e]

        def issue(j, c):
            piece(gd + j * R, ls + j * R).start()
            return c

        lax.fori_loop(0, pieces, issue, 0)
        return issued + pieces

    total = lax.fori_loop(0, N_EXPERTS, per_expert, 0)

    hb = h_ref[...].astype(BF16)
    mid = _silu(_dot(hb, s1_ref[...])) * _dot(hb, s3_ref[...])
    y = _dot(mid.astype(BF16), s2_ref[...])

    def wait(j, c):
        piece(0, 0).wait()
        return c

    lax.fori_loop(0, total, wait, 0)

    lpos = lpos_ref[...]
    wt = wt_ref[...]
    used_rows = lst_ref[i, N_EXPERTS - 1] + len_ref[i, N_EXPERTS - 1]
    rb = 256

    def block_sum(blk):
        r = _iota((rb, tm), 0) + blk * rb
        pw = jnp.zeros((rb, tm), F32)
        for kk in range(TOP_K):
            pw = jnp.where(r == lpos[kk:kk + 1, :], wt[kk:kk + 1, :], pw)
        ysc = yloc[blk * rb:(blk + 1) * rb, :] * jnp.sum(pw, axis=1, keepdims=True)
        hi = ysc.astype(BF16)
        lo = (ysc - hi.astype(F32)).astype(BF16)
        owner = (pw != 0.0).astype(BF16)
        return _dot_tn(owner, hi) + _dot_tn(owner, lo)

    for blk in range(nloc // rb):
        if (blk + 1) * rb <= tm * TOP_K:
            y = y + block_sum(blk)
        else:
            y = y + lax.cond(used_rows > blk * rb, functools.partial(block_sum, blk),
                             lambda: jnp.zeros((tm, D_MODEL), F32))
    x = x_ref[...] + mod_ref[5:6, :] * y
    if final_norm:
        x = x * lax.rsqrt(jnp.mean(x * x, axis=-1, keepdims=True) + EPS) * fg_ref[...]
    o_ref[...] = x


def _combine(ys, lpos, wt, tile_len, tile_lstart, tile_car, seg_start, h2, x, mod, s1, s3, s2, final_g,
             *, tm, tokens_per_mod, final_norm):
    T = x.shape[0]
    tiles_per_mod = tokens_per_mod // tm
    nloc = tm * TOP_K + N_EXPERTS * RUN_ROWS
    xspec = pl.BlockSpec((tm, D_MODEL), lambda i, *_: (i, 0))
    kspec = pl.BlockSpec((TOP_K, tm), lambda i, *_: (0, i))
    const = lambda i, *_: (0, 0)
    return pl.pallas_call(
        functools.partial(_combine_kernel, final_norm=final_norm),
        grid_spec=pltpu.PrefetchScalarGridSpec(
            num_scalar_prefetch=4,
            grid=(T // tm,),
            in_specs=[kspec, kspec,
                      pl.BlockSpec(memory_space=pl.ANY),
                      xspec, xspec,
                      pl.BlockSpec((None, 6, D_MODEL), lambda i, *_: (i // tiles_per_mod, 0, 0)),
                      pl.BlockSpec((D_MODEL, D_SHARED), const),
                      pl.BlockSpec((D_MODEL, D_SHARED), const),
                      pl.BlockSpec((D_SHARED, D_MODEL), const),
                      pl.BlockSpec((1, D_MODEL), const)],
            out_specs=xspec,
            scratch_shapes=[pltpu.VMEM((nloc, D_MODEL), F32), pltpu.SemaphoreType.DMA(())]),
        out_shape=jax.ShapeDtypeStruct((T, D_MODEL), F32),
        compiler_params=_cparams(("arbitrary",)),
        name="moe_combine",
    )(tile_len, tile_lstart, tile_car, seg_start, lpos, wt, ys, h2, x, mod, s1, s3, s2, final_g.reshape(1, D_MODEL))


def _work_list(counts, starts, n_rows, bm):
    E = N_EXPERTS
    n_blocks = n_rows // bm
    n_work = n_blocks + E - 1
    ends = starts + counts
    first_blk = starts // bm
    last_blk = jnp.where(counts > 0, (ends - 1) // bm, first_blk)
    n_items = jnp.where(counts > 0, last_blk - first_blk + 1, 0)
    item_end = jnp.cumsum(n_items)
    item_start = item_end - n_items
    w = jnp.arange(n_work, dtype=I32)
    ex = jnp.minimum(jnp.sum((item_end[None, :] <= w[:, None]).astype(I32), axis=1), E - 1)
    valid = w < item_end[-1]
    blk = first_blk[ex] + (w - item_start[ex])
    blk = jnp.where(valid, blk, blk[jnp.maximum(item_end[-1] - 1, 0)]).astype(I32)
    lo = jnp.clip(starts[ex] - blk * bm, 0, bm)
    hi = jnp.clip(ends[ex] - blk * bm, 0, bm)
    lo = jnp.where(valid, lo, 0).astype(I32)
    hi = jnp.where(valid, hi, 0).astype(I32)
    ex = jnp.where(valid, ex, ex[jnp.maximum(item_end[-1] - 1, 0)])
    prev_blk = jnp.concatenate([jnp.full((1,), -1, I32), blk[:-1]])
    first = (blk != prev_blk).astype(I32)
    return blk, ex.astype(I32), lo, hi, first


def _moe(h2, x, mod, p, final_g, *, layer, tokens_per_mod, final_norm):
    T = x.shape[0]
    tm = 256
    wt, lpos, tile_len, tile_lstart, tile_car = _router(h2, p["router_w"], p["router_b"], tm=tm)
    seg_len = tile_car[-1] + tile_len[-1]
    seg_start = jnp.cumsum(seg_len) - seg_len
    xs = _dispatch(h2, lpos, tile_len, tile_lstart, tile_car, seg_start, tm=tm)
    bm = 512
    work = _work_list(seg_len, seg_start, xs.shape[0], bm)
    ys = _experts(xs, work, p["exp_w1"], p["exp_w3"], p["exp_w2"], layer=layer, bm=bm)
    return _combine(ys, lpos, wt, tile_len, tile_lstart, tile_car, seg_start, h2, x, mod,
                    p["sh_w1"], p["sh_w3"], p["sh_w2"], final_g,
                    tm=tm, tokens_per_mod=tokens_per_mod, final_norm=final_norm)


def _reorder_w_in(w_in):
    plain = D_CONV
    pieces = [w_in[..., :plain],
              w_in[..., plain + 3616:plain + 3616 + 4096],
              w_in[..., plain:plain + 3072],
              w_in[..., plain + 3088:plain + 3600],
              w_in[..., plain + 3072:plain + 3088],
              w_in[..., plain + 3600:plain + 3616]]
    out = jnp.concatenate(pieces, axis=-1)
    pad = D_PROJ - out.shape[-1]
    return jnp.pad(out, ((0, 0), (0, 0), (0, pad))).astype(BF16)


def _small_rows(ssd_vals, gdn_vals):
    row = jnp.zeros((LANES,), F32)
    row = row.at[S_DT:S_DT + 2 * SSD_HEADS].set(ssd_vals.reshape(-1))
    row = row.at[S_GA:S_GA + 2 * GDN_HEADS].set(gdn_vals.reshape(-1))
    return row.reshape(1, LANES)


def _layer_pass(x, mod, lp, hy, st_hg, st_ssd, st_gdn, final_g, *, B, L, tokens_per_mod, seg, layer, final_norm):
    T = B * L
    proj = _in_projection(x, mod, lp["norm1_g"], lp["w_in"], lp["conv_w"], lp["conv_b"],
                          layer=layer, tokens_per_mod=tokens_per_mod, seg=seg)
    proj3 = proj.reshape(B, L, D_PROJ)
    fmat, gmat, hspec = hy
    cb = lambda c: c // D_BRANCH
    z = _spectral_conv(fmat, gmat, hspec, 0, lp["hy_bias"][0], proj3, cb(C_HYV), proj3, cb(C_HYX1))
    ya = _spectral_conv(fmat, gmat, hspec, 1, lp["hy_bias"][1], z, 0, proj3, cb(C_HYX2))
    yb, s_hg = _hgrn_scan(proj3, lp["hg_lb"], lp["hg_norm"], st_hg, layer=layer)
    yc, s_ssd = _ssd_scan(proj3, lp["bias_row"], lp["alog_row"], lp["ssd_d"], st_ssd)
    yd, s_gdn = _gdn_scan(proj3, lp["bias_row"], lp["alog_row"], lp["gdn_norm"], st_gdn)
    flat = lambda a: a.reshape(T, D_BRANCH)
    x, h2 = _merge(flat(ya), flat(yb), flat(yc), flat(yd), proj, x, mod, lp["ssd_norm"], lp["norm2_g"],
                   lp["w_branch"], lp["w_out"], tokens_per_mod=tokens_per_mod)
    x = _moe(h2, x, mod, lp, final_g, layer=layer, tokens_per_mod=tokens_per_mod, final_norm=final_norm)
    return x, s_hg, s_ssd, s_gdn


def kernel(x_prompt, x_sample, state_hgrn, state_ssd, state_gdn, c, c_ctx, norm1_g, norm2_g, ada_w, ada_b, w_in, conv_w, conv_b, hy_w1, hy_b1, hy_w2, hy_b2, hy_w3, hy_bias, hg_lb, hg_norm, ssd_a_log, ssd_dt_bias, ssd_d, ssd_norm, gdn_a_log, gdn_dt_bias, gdn_norm, w_branch, w_out, router_w, router_bias, exp_w1, exp_w3, exp_w2, sh_w1, sh_w3, sh_w2, final_g):
    depth = w_in.shape[0]
    bp, lp_len = x_prompt.shape[0], x_prompt.shape[1]
    bs, ls_len = x_sample.shape[0], x_sample.shape[1]
    D = D_MODEL

    w_in_r = _reorder_w_in(w_in)
    hy_w1p = jnp.pad(hy_w1, ((0, 0), (0, LANES - hy_w1.shape[1]), (0, 0)))
    layers = []
    for l in range(depth):
        layers.append(dict(
            norm1_g=norm1_g[l], norm2_g=norm2_g[l], w_in=w_in_r, conv_w=conv_w[l], conv_b=conv_b[l],
            hy_bias=hy_bias[l], hg_lb=hg_lb, hg_norm=hg_norm[l],
            bias_row=_small_rows(ssd_dt_bias[l], gdn_dt_bias[l]),
            alog_row=_small_rows(ssd_a_log[l], gdn_a_log[l]),
            ssd_d=ssd_d[l], ssd_norm=ssd_norm[l], gdn_norm=gdn_norm[l],
            w_branch=w_branch[l].astype(BF16), w_out=w_out[l].astype(BF16),
            router_w=router_w[l], router_b=router_bias[l],
            exp_w1=exp_w1, exp_w3=exp_w3, exp_w2=exp_w2,
            sh_w1=sh_w1[l].astype(BF16), sh_w3=sh_w3[l].astype(BF16), sh_w2=sh_w2[l].astype(BF16)))

    def hyena_setup(L):
        fmat, fs = _dft_matrices(L)
        gmat = fs.T
        specs = []
        for l in range(depth):
            filt = _hyena_filters(L, hy_w1p[l], hy_b1[l], hy_w2[l], hy_b2[l], hy_w3[l])
            specs.append(_filter_spectrum(fmat, filt))
        return fmat, gmat, specs

    cond = jnp.concatenate([c_ctx.reshape(1, D), c], axis=0)
    rows = cond.shape[0]
    rows8 = (rows + 7) // 8 * 8
    cond8 = jnp.pad(cond, ((0, rows8 - rows), (0, 0)))
    mods = [_modulation(cond8, ada_w, ada_b[l], layer=l).reshape(rows8, 6, D) for l in range(depth)]

    fmat, gmat, specs = hyena_setup(lp_len)
    x = x_prompt.reshape(bp * lp_len, D)
    z_hg = jnp.zeros((bp, 2, HG_HEADS, HG_DK, HG_DK), F32)
    z_ssd = jnp.zeros((bp, 2, SSD_HEADS, SSD_P, SSD_N), F32)
    z_gdn = jnp.zeros((bp, 2, GDN_HEADS, GDN_DK, GDN_DK), F32)
    hg_states, ssd_states, gdn_states = [], [], []
    for l in range(depth):
        x, s_hg, s_ssd, s_gdn = _layer_pass(
            x, mods[l][0:1], layers[l], (fmat, gmat, specs[l]), z_hg, z_ssd, z_gdn, final_g,
            B=bp, L=lp_len, tokens_per_mod=bp * lp_len, seg=lp_len, layer=l, final_norm=(l == depth - 1))
        hg_states.append(s_hg)
        ssd_states.append(s_ssd)
        gdn_states.append(s_gdn)
    y_prompt = x.reshape(bp, lp_len, D)
    new_hg = jnp.stack(hg_states, axis=1)
    new_ssd = jnp.stack(ssd_states, axis=1)
    new_gdn = jnp.stack(gdn_states, axis=1)

    fmat, gmat, specs = hyena_setup(ls_len)
    x = x_sample.reshape(bs * ls_len, D)
    for l in range(depth):
        x, _, _, _ = _layer_pass(
            x, mods[l][1:1 + bs], layers[l], (fmat, gmat, specs[l]),
            state_hgrn[:, l], state_ssd[:, l], state_gdn[:, l], final_g,
            B=bs, L=ls_len, tokens_per_mod=ls_len, seg=GRID_W, layer=l, final_norm=(l == depth - 1))
    y_sample = x.reshape(bs, ls_len, D)
    return (y_prompt, y_sample, new_hg, new_ssd, new_gdn)
```

```python
import functools
import math

import numpy as np
import jax
import jax.numpy as jnp
from jax import lax
from jax.experimental import pallas as pl
from jax.experimental.pallas import tpu as pltpu

F32 = jnp.float32
BF16 = jnp.bfloat16
I32 = jnp.int32
HIGHEST = lax.Precision.HIGHEST

D_MODEL = 1024
GRID_W = 64
EPS = 1e-6
LOG_FLOOR = 1e-30
MASK_NEG = -1e30
N_BRANCH = 4
D_BRANCH = 512
HY_POS_FREQS = 16
HY_FILTER_HIDDEN = 64
HY_FAST_DECAY = 0.3
HY_SLOW_DECAY = 1.5
HY_DECAY_TARGET = 1e-2
HG_HEADS = 4
HG_DK = 128
HG_CHUNK = 16
SSD_HEADS = 8
SSD_P = 64
SSD_N = 128
SSD_CHUNK = 64
GDN_HEADS = 4
GDN_DK = 128
GDN_CHUNK = 64
N_EXPERTS = 64
TOP_K = 8
N_EXPERT_GROUPS = 8
TOPK_GROUPS = 4
D_EXPERT = 256
D_SHARED = 256
ROUTED_SCALE = 2.5

LANES = 128
D_XPACK = D_MODEL // 2
RUN_ROWS = 8

D_CONV = 4096
C_HYV, C_HYX1, C_HYX2 = 0, 512, 1024
C_SSDX, C_SSDB, C_SSDC = 1536, 2048, 2304
C_GQ, C_GK, C_GV = 2560, 3072, 3584
P_MERGE = 4096
P_HGQ, P_HGFF, P_HGFB, P_HGV, P_HGG = 8192, 8704, 9216, 9728, 10240
P_SSDZ, P_GDNG, P_SMALL = 10752, 11264, 11776
D_PROJ = 12288
S_DT, S_GA, S_GB = 0, 16, 24

VMEM_LIMIT = 56 * 1024 * 1024


def _cparams(sem):
    return pltpu.CompilerParams(dimension_semantics=sem, vmem_limit_bytes=VMEM_LIMIT)


def _sigmoid(x):
    return 1.0 / (1.0 + jnp.exp(-x))


def _silu(x):
    return x * _sigmoid(x)


def _softplus(x):
    return jnp.maximum(x, 0.0) + jnp.log(1.0 + jnp.exp(-jnp.abs(x)))


def _dot(a, b, precision=None):
    return jnp.dot(a, b, preferred_element_type=F32, precision=precision)


def _dot_nt(a, b, precision=None):
    return lax.dot_general(a, b, (((1,), (1,)), ((), ())), preferred_element_type=F32, precision=precision)


def _dot_tn(a, b, precision=None):
    return lax.dot_general(a, b, (((0,), (0,)), ((), ())), preferred_element_type=F32, precision=precision)


def _bdot(a, b):
    return _dot(a.astype(BF16), b.astype(BF16))


def _bdot_nt(a, b):
    return _dot_nt(a.astype(BF16), b.astype(BF16))


def _bdot_tn(a, b):
    return _dot_tn(a.astype(BF16), b.astype(BF16))


def _iota(shape, dim):
    return lax.broadcasted_iota(I32, shape, dim)


def _cumsum_rows(g, reverse):
    n = g.shape[0]
    row = _iota(g.shape, 0)
    sh = 1
    while sh < n:
        if reverse:
            g = g + jnp.where(row < n - sh, pltpu.roll(g, n - sh, 0), 0.0)
        else:
            g = g + jnp.where(row >= sh, pltpu.roll(g, sh, 0), 0.0)
        sh *= 2
    return g


def _lane_pick(a, j):
    return jnp.sum(jnp.where(_iota(a.shape, 1) == j, a, 0.0), axis=1, keepdims=True)


def _split3(a):
    a1 = a.astype(BF16)
    r1 = a - a1.astype(F32)
    a2 = r1.astype(BF16)
    a3 = (r1 - a2.astype(F32)).astype(BF16)
    return a1, a2, a3


def _dot3(a, b):
    a1, a2, _ = _split3(a)
    b1, b2, _ = _split3(b)
    return _dot(a1, b1) + (_dot(a1, b2) + _dot(a2, b1))


def _row_pick(a, j):
    sel = (_iota((8, a.shape[1]), 1) == j).astype(BF16)
    a1, a2, a3 = _split3(a)
    return (_dot_nt(sel, a1) + (_dot_nt(sel, a2) + _dot_nt(sel, a3)))[0:1, :]


def _tri_mask(n, reverse, strict=False):
    t = _iota((n, n), 0)
    s = _iota((n, n), 1)
    if reverse:
        return (s > t) if strict else (s >= t)
    return (s < t) if strict else (s <= t)


def _masked_exp(mask, diff):
    return jnp.where(mask, jnp.exp(jnp.where(mask, diff, 0.0)), 0.0)


def _mod_kernel(c_ref, w_ref, b_ref, o_ref):
    o_ref[...] = _dot(_silu(c_ref[...]), w_ref[...], precision=HIGHEST) + b_ref[...]


def _modulation(cond8, ada_w, ada_b, *, layer):
    rows = cond8.shape[0]
    tn = 1536
    n = ada_w.shape[2]
    return pl.pallas_call(
        _mod_kernel,
        grid=(n // tn,),
        in_specs=[pl.BlockSpec((rows, D_MODEL), lambda j: (0, 0)),
                  pl.BlockSpec((None, D_MODEL, tn), lambda j: (layer, 0, j)),
                  pl.BlockSpec((1, tn), lambda j: (0, j))],
        out_specs=pl.BlockSpec((rows, tn), lambda j: (0, j)),
        out_shape=jax.ShapeDtypeStruct((rows, n), F32),
        compiler_params=_cparams(("arbitrary",)),
        name="adaln_mod",
    )(cond8, ada_w, ada_b.reshape(1, n))


def _inproj_kernel(x_ref, mod_ref, g_ref, w_ref, cw_ref, cb_ref, o_ref, h_scr, *, seg, n_conv_tiles):
    j = pl.program_id(1)

    @pl.when(j == 0)
    def _():
        x = x_ref[...]
        xn = x * lax.rsqrt(jnp.mean(x * x, axis=-1, keepdims=True) + EPS) * g_ref[...]
        m = mod_ref[...]
        h_scr[...] = (xn * (1.0 + m[1:2]) + m[0:1]).astype(BF16)

    y = _dot(h_scr[...], w_ref[...])

    @pl.when(j < n_conv_tiles)
    def _():
        tm = y.shape[0]
        pos = _iota(y.shape, 0) & (seg - 1)
        prev = jnp.where(pos == 0, 0.0, pltpu.roll(y, 1, 0))
        nxt = jnp.where(pos == seg - 1, 0.0, pltpu.roll(y, tm - 1, 0))
        cw = cw_ref[...]
        o_ref[...] = cb_ref[...] + prev * cw[0:1] + y * cw[1:2] + nxt * cw[2:3]

    @pl.when(j >= n_conv_tiles)
    def _():
        o_ref[...] = y


def _in_projection(x, mod, norm_g, w_bf16, conv_w, conv_b, *, layer, tokens_per_mod, seg):
    T = x.shape[0]
    tm = min(2048, tokens_per_mod)
    tn = 1024
    n_conv_tiles = D_CONV // tn
    tiles_per_mod = tokens_per_mod // tm
    kern = functools.partial(_inproj_kernel, seg=seg, n_conv_tiles=n_conv_tiles)
    cmap = lambda i, j: (0, jnp.minimum(j, n_conv_tiles - 1))
    return pl.pallas_call(
        kern,
        grid=(T // tm, D_PROJ // tn),
        in_specs=[pl.BlockSpec((tm, D_MODEL), lambda i, j: (i, 0)),
                  pl.BlockSpec((None, 6, D_MODEL), lambda i, j: (i // tiles_per_mod, 0, 0)),
                  pl.BlockSpec((1, D_MODEL), lambda i, j: (0, 0)),
                  pl.BlockSpec((None, D_MODEL, tn), lambda i, j: (layer, 0, j)),
                  pl.BlockSpec((3, tn), cmap),
                  pl.BlockSpec((1, tn), cmap)],
        out_specs=pl.BlockSpec((tm, tn), lambda i, j: (i, j)),
        out_shape=jax.ShapeDtypeStruct((T, D_PROJ), F32),
        scratch_shapes=[pltpu.VMEM((tm, D_MODEL), BF16)],
        compiler_params=_cparams(("arbitrary", "arbitrary")),
        name="in_proj",
    )(x, mod, norm_g.reshape(1, D_MODEL), w_bf16, conv_w, conv_b.reshape(1, D_CONV))


def _hyfilt_kernel(w1_ref, b1_ref, w2_ref, b2_ref, w3_ref, o_ref, *, L):
    i = pl.program_id(1)
    tl = o_ref.shape[0]
    t = (_iota((tl, LANES), 0) + i * tl).astype(F32) / L
    lane = _iota((tl, LANES), 1)
    band = jnp.where(lane <= HY_POS_FREQS, lane, lane - HY_POS_FREQS).astype(F32)
    ang = 2.0 * math.pi * t * band
    feats = jnp.where(lane == 0, t,
                      jnp.where(lane <= HY_POS_FREQS, jnp.sin(ang),
                                jnp.where(lane <= 2 * HY_POS_FREQS, jnp.cos(ang), 0.0)))
    hdn = jnp.sin(_dot(feats, w1_ref[...], precision=HIGHEST) + b1_ref[...])
    hdn = jnp.sin(_dot(hdn, w2_ref[...], precision=HIGHEST) + b2_ref[...])
    filt = _dot(hdn, w3_ref[...], precision=HIGHEST)
    max_decay = math.log(HY_DECAY_TARGET) / HY_FAST_DECAY
    min_decay = math.log(HY_DECAY_TARGET) / HY_SLOW_DECAY
    n = filt.shape[1]
    ch = (_iota((tl, n), 1) & (D_BRANCH - 1)).astype(F32)
    delta = min_decay + ch * ((max_decay - min_decay) / (D_BRANCH - 1))
    tt = (_iota((tl, n), 0) + i * tl).astype(F32) / L
    o_ref[...] = filt * jnp.exp(-tt * jnp.abs(delta))


def _hyena_filters(L, w1p, b1, w2, b2, w3):
    tl = min(L, 256)
    n = 2 * D_BRANCH
    return pl.pallas_call(
        functools.partial(_hyfilt_kernel, L=L),
        grid=(2, L // tl),
        in_specs=[pl.BlockSpec((LANES, HY_FILTER_HIDDEN), lambda d, i: (0, 0)),
                  pl.BlockSpec((1, HY_FILTER_HIDDEN), lambda d, i: (0, 0)),
                  pl.BlockSpec((HY_FILTER_HIDDEN, HY_FILTER_HIDDEN), lambda d, i: (0, 0)),
                  pl.BlockSpec((1, HY_FILTER_HIDDEN), lambda d, i: (0, 0)),
                  pl.BlockSpec((HY_FILTER_HIDDEN, n), lambda d, i: (0, d))],
        out_specs=pl.BlockSpec((None, tl, n), lambda d, i: (d, i, 0)),
        out_shape=jax.ShapeDtypeStruct((2, L, n), F32),
        compiler_params=_cparams(("arbitrary", "arbitrary")),
        name="hyena_filters",
    )(w1p, b1.reshape(1, -1), w2, b2.reshape(1, -1), w3)


def _dftgen_kernel(f_ref, fs_ref, *, L, tk):
    i = pl.program_id(0)
    N = 2 * L
    k = _iota((tk, LANES), 0) + i * tk
    lane = _iota((tk, LANES), 1)
    w = 2.0 * math.pi / N
    a0 = ((k * lane) & (N - 1)).astype(F32) * w
    c0, s0 = jnp.cos(a0), jnp.sin(a0)
    a1 = ((k * lane * LANES) & (N - 1)).astype(F32) * w
    c1, s1 = jnp.cos(a1), jnp.sin(a1)
    alt = jnp.where((lane & 1) == 0, 1.0, -1.0)
    coef = jnp.where(k == 0, 1.0 / N, 2.0 / N)
    for n1 in range(L // LANES):
        c1b = c1[:, n1:n1 + 1]
        s1b = s1[:, n1:n1 + 1]
        cosb = c1b * c0 - s1b * s0
        sinb = jnp.where(k == 0, alt, -(s1b * c0 + c1b * s0))
        cols = slice(n1 * LANES, (n1 + 1) * LANES)
        f_ref[0:tk, cols] = cosb.astype(BF16)
        f_ref[tk:2 * tk, cols] = sinb.astype(BF16)
        fs_ref[0:tk, cols] = (coef * cosb).astype(BF16)
        fs_ref[tk:2 * tk, cols] = (coef * sinb).astype(BF16)


def _dft_matrices(L):
    tk = min(L, 256)
    spec = pl.BlockSpec((2 * tk, L), lambda i: (i, 0))
    return pl.pallas_call(
        functools.partial(_dftgen_kernel, L=L, tk=tk),
        grid=(L // tk,),
        in_specs=[],
        out_specs=[spec, spec],
        out_shape=[jax.ShapeDtypeStruct((2 * L, L), BF16)] * 2,
        compiler_params=_cparams(("arbitrary",)),
        name="dft_matrices",
    )()


def _dfth_kernel(f_ref, h_ref, o_ref, hp_scr, *, tk):
    i = pl.program_id(1)
    n = o_ref.shape[1]

    @pl.when(i == 0)
    def _():
        hc = h_ref[0]
        ha = jnp.where(_iota(hc.shape, 0) == 0, 0.0, h_ref[1])
        hp_scr[:, 0:n] = (hc + ha).astype(BF16)
        hp_scr[:, n:2 * n] = (hc - ha).astype(BF16)

    u = _dot(f_ref[...], hp_scr[...])
    r = _iota((2 * tk, n), 0)
    from_sum = (r < tk) | ((r == tk) & (i == 0))
    o_ref[...] = jnp.where(from_sum, u[:, 0:n], u[:, n:2 * n])


def _filter_spectrum(fmat, filt):
    L = fmat.shape[1]
    tk = min(L, 256)
    C = filt.shape[2]
    tc = 256
    return pl.pallas_call(
        functools.partial(_dfth_kernel, tk=tk),
        grid=(C // tc, L // tk),
        in_specs=[pl.BlockSpec((2 * tk, L), lambda c, i: (i, 0)),
                  pl.BlockSpec((2, L, tc), lambda c, i: (0, 0, c))],
        out_specs=pl.BlockSpec((2 * tk, tc), lambda c, i: (i, c)),
        out_shape=jax.ShapeDtypeStruct((2 * L, C), F32),
        scratch_shapes=[pltpu.VMEM((L, 2 * tc), BF16)],
        compiler_params=_cparams(("arbitrary", "arbitrary")),
        name="filter_spectrum",
    )(fmat, filt)


def _dfta_kernel(f_ref, u_ref, h_ref, y_ref, u_scr, *, tk):
    i = pl.program_id(1)

    @pl.when(i == 0)
    def _():
        u_scr[...] = u_ref[...].astype(BF16)

    uf = _dot(f_ref[...], u_scr[...])
    ur, ui = uf[0:tk], uf[tk:2 * tk]
    hr, hi = h_ref[0:tk, :], h_ref[tk:2 * tk, :]
    dc = (_iota(ur.shape, 0) == 0) & (i == 0)
    y_ref[0:tk, :] = jnp.where(dc, ur * hr, ur * hr - ui * hi).astype(BF16)
    y_ref[tk:2 * tk, :] = jnp.where(dc, ui * hi, ur * hi + ui * hr).astype(BF16)


def _dftb_kernel(g_ref, y_ref, u_ref, x_ref, b_ref, o_ref):
    y = _dot(g_ref[...], y_ref[...])
    o_ref[...] = x_ref[...] * (y + u_ref[...] * b_ref[...])


def _spectral_conv(fmat, gmat, hspec, h_col, bias, u_arr, u_col, gate_arr, gate_col):
    B, L = u_arr.shape[0], u_arr.shape[1]
    C = D_BRANCH
    tk = min(L, 256)
    y = pl.pallas_call(
        functools.partial(_dfta_kernel, tk=tk),
        grid=(B, L // tk),
        in_specs=[pl.BlockSpec((2 * tk, L), lambda b, i: (i, 0)),
                  pl.BlockSpec((None, L, C), lambda b, i: (b, 0, u_col)),
                  pl.BlockSpec((2 * tk, C), lambda b, i: (i, h_col))],
        out_specs=pl.BlockSpec((None, 2 * tk, C), lambda b, i: (b, i, 0)),
        out_shape=jax.ShapeDtypeStruct((B, 2 * L, C), BF16),
        scratch_shapes=[pltpu.VMEM((L, C), BF16)],
        compiler_params=_cparams(("arbitrary", "arbitrary")),
        name="hyena_dft_fwd",
    )(fmat, u_arr, hspec)
    tr = min(L, 256)
    return pl.pallas_call(
        _dftb_kernel,
        grid=(B, L // tr),
        in_specs=[pl.BlockSpec((tr, 2 * L), lambda b, i: (i, 0)),
                  pl.BlockSpec((None, 2 * L, C), lambda b, i: (b, 0, 0)),
                  pl.BlockSpec((None, tr, C), lambda b, i: (b, i, u_col)),
                  pl.BlockSpec((None, tr, C), lambda b, i: (b, i, gate_col)),
                  pl.BlockSpec((1, C), lambda b, i: (0, 0))],
        out_specs=pl.BlockSpec((None, tr, C), lambda b, i: (b, i, 0)),
        out_shape=jax.ShapeDtypeStruct((B, L, C), F32),
        compiler_params=_cparams(("arbitrary", "arbitrary")),
        name="hyena_dft_inv",
    )(gmat, y, u_arr, gate_arr, bias.reshape(1, C))


def _hgrn_kernel(q_ref, ff_ref, fb_ref, v_ref, g_ref, lb_ref, nrm_ref, s0_ref, y_ref, sf_ref,
                 of_scr, ob_scr, st_scr, *, L, layer, depth):
    C = HG_CHUNK
    W = HG_DK
    nc = L // C
    nh = q_ref.shape[1] // W
    ridx = _iota((C, W), 0)
    o_scrs = (of_scr, ob_scr)
    f_refs = (ff_ref, fb_ref)

    def lower_bound(d, cols):
        rows = [lb_ref[d, l:l + 1, cols] for l in range(depth)]
        m = rows[0]
        for r in rows[1:]:
            m = jnp.maximum(m, r)
        es = [jnp.exp(r - m) for r in rows]
        tot = es[0]
        for e in es[1:]:
            tot = tot + e
        acc = es[0] / tot
        for e in es[1:layer + 1]:
            acc = acc + e / tot
        return acc - es[0] / tot

    lbs = [[lower_bound(d, slice(hh * W, (hh + 1) * W)) for hh in range(nh)] for d in (0, 1)]
    for d in (0, 1):
        for hh in range(nh):
            st_scr[d, hh] = s0_ref[d, hh].T

    def step(c, d, hh):
        rev = d == 1
        rows = pl.ds(pl.multiple_of(c * C, C), C)
        cols = slice(hh * W, (hh + 1) * W)
        lb = lbs[d][hh]
        q = _silu(q_ref[rows, cols])
        uf = f_refs[d][rows, cols]
        v = v_ref[rows, cols]
        f = lb + (1.0 - lb) * _sigmoid(uf)
        g = jnp.log(jnp.maximum(f, LOG_FLOOR))
        kin = (1.0 - lb) * _sigmoid(-uf)
        b = _cumsum_rows(g, rev)
        st = st_scr[d, hh]
        o = _bdot_nt(q * jnp.exp(b), st)
        intra = jnp.zeros((C, W), F32)
        for t in range(C):
            mask = (ridx >= t) if rev else (ridx <= t)
            pair = _masked_exp(mask, b[t:t + 1, :] - b)
            a = jnp.sum(pair * (q[t:t + 1, :] * kin), axis=1, keepdims=True)
            row = jnp.sum(a * v, axis=0, keepdims=True)
            intra = jnp.where(ridx == t, row, intra)
        b_end = b[0:1, :] if rev else b[C - 1:C, :]
        st_scr[d, hh] = st * jnp.exp(b_end) + _bdot_tn(v, kin * jnp.exp(b_end - b))
        o_scrs[d][rows, cols] = o + intra

    def body(ci, carry):
        for d in (0, 1):
            for hh in range(nh):
                step((nc - 1 - ci) if d == 1 else ci, d, hh)
        return carry

    lax.fori_loop(0, nc, body, 0)
    for d in (0, 1):
        for hh in range(nh):
            sf_ref[d, hh] = st_scr[d, hh].T

    tr = min(L, 256)

    def fin(i, carry):
        rows = pl.ds(pl.multiple_of(i * tr, tr), tr)
        for hh in range(nh):
            cols = slice(hh * W, (hh + 1) * W)
            o = of_scr[rows, cols] + ob_scr[rows, cols]
            on = o * lax.rsqrt(jnp.mean(o * o, axis=-1, keepdims=True) + EPS) * nrm_ref[...]
            y_ref[rows, cols] = on * _silu(g_ref[rows, cols])
        return carry

    lax.fori_loop(0, L // tr, fin, 0)


def _hgrn_scan(proj3, hg_lb, hg_norm, s0, *, layer):
    B, L = proj3.shape[0], proj3.shape[1]
    depth = hg_lb.shape[1]
    nh = 2
    W = nh * HG_DK
    col = lambda base: (lambda b, h: (b, 0, base // W + h))
    st_spec = pl.BlockSpec((None, 2, nh, HG_DK, HG_DK), lambda b, h: (b, 0, h, 0, 0))
    return pl.pallas_call(
        functools.partial(_hgrn_kernel, L=L, layer=layer, depth=depth),
        grid=(B, HG_HEADS // nh),
        in_specs=[pl.BlockSpec((None, L, W), col(P_HGQ)),
                  pl.BlockSpec((None, L, W), col(P_HGFF)),
                  pl.BlockSpec((None, L, W), col(P_HGFB)),
                  pl.BlockSpec((None, L, W), col(P_HGV)),
                  pl.BlockSpec((None, L, W), col(P_HGG)),
                  pl.BlockSpec((2, depth, W), lambda b, h: (0, 0, h)),
                  pl.BlockSpec((1, HG_DK), lambda b, h: (0, 0)),
                  st_spec],
        out_specs=[pl.BlockSpec((None, L, W), lambda b, h: (b, 0, h)), st_spec],
        out_shape=[jax.ShapeDtypeStruct((B, L, D_BRANCH), F32),
                   jax.ShapeDtypeStruct(s0.shape, F32)],
        scratch_shapes=[pltpu.VMEM((L, W), F32), pltpu.VMEM((L, W), F32),
                        pltpu.VMEM((2, nh, HG_DK, HG_DK), F32)],
        compiler_params=_cparams(("arbitrary", "arbitrary")),
        name="hgrn2_scan",
    )(proj3, proj3, proj3, proj3, proj3, hg_lb, hg_norm.reshape(1, HG_DK), s0)


def _ssd_kernel(x_ref, bm_ref, cm_ref, z_ref, sm_ref, bias_ref, alog_ref, dskip_ref, s0_ref,
                y_ref, sf_ref, of_scr, ob_scr, st_scr, *, L):
    C = SSD_CHUNK
    W = LANES
    nc = L // C
    npairs = x_ref.shape[1] // W
    grp = pl.program_id(1)
    lane_lo = _iota((C, W), 1) < SSD_P
    row_lo = _iota((W, SSD_N), 0) < SSD_P
    pick2 = lambda a0, a1: jnp.where(lane_lo, a0, a1)
    o_scrs = (of_scr, ob_scr)
    incls = (_tri_mask(C, False), _tri_mask(C, True))
    for d in (0, 1):
        for pp in range(npairs):
            st_scr[d, pp] = s0_ref[d, pp]

    heads = [(d, pp, hh) for d in (0, 1) for pp in range(npairs) for hh in (0, 1)]
    pairs = [(d, pp) for d in (0, 1) for pp in range(npairs)]

    def body(ci, carry):
        rows_d, bm, cm, dt_all, cum_all, gram = [], [], [], [], [], []
        for d in (0, 1):
            c = (nc - 1 - ci) if d == 1 else ci
            rows = pl.ds(pl.multiple_of(c * C, C), C)
            rows_d.append(rows)
            bm.append(_silu(bm_ref[rows, :]))
            cm.append(_silu(cm_ref[rows, :]))
            dt_all.append(_softplus(sm_ref[rows, :] + bias_ref[...]))
            cum_all.append(_cumsum_rows(-jnp.exp(alog_ref[...]) * dt_all[d], d == 1))
        lane_of = lambda u: S_DT + u[0] * SSD_HEADS + 2 * (npairs * grp + u[1]) + u[2]
        cumr = {u: _row_pick(cum_all[u[0]], lane_of(u)) for u in heads}
        for d in (0, 1):
            gram.append(_bdot_nt(cm[d], bm[d]))
        dtc = {u: _lane_pick(dt_all[u[0]], lane_of(u)) for u in heads}
        cumc = {u: _lane_pick(cum_all[u[0]], lane_of(u)) for u in heads}
        end = {u: (cumc[u][0:1, :] if u[0] == 1 else cumc[u][C - 1:C, :]) for u in heads}
        xs = {pr: _silu(x_ref[rows_d[pr[0]], slice(pr[1] * W, (pr[1] + 1) * W)]) for pr in pairs}
        st = {pr: st_scr[pr[0], pr[1]] for pr in pairs}
        y_state = {u: _bdot_nt(cm[u[0]] * jnp.exp(cumc[u]), st[u[:2]]) for u in heads}
        scores = {u: gram[u[0]] * _masked_exp(incls[u[0]], cumc[u] - cumr[u]) for u in heads}
        y_intra = {}
        for u in heads:
            head_x = jnp.where(lane_lo if u[2] == 0 else ~lane_lo, xs[u[:2]], 0.0) * dtc[u]
            y_intra[u] = _bdot(scores[u], head_x)
        for pr in pairs:
            u0, u1 = pr + (0,), pr + (1,)
            xdt_e = xs[pr] * pick2(dtc[u0] * jnp.exp(end[u0] - cumc[u0]), dtc[u1] * jnp.exp(end[u1] - cumc[u1]))
            st_scr[pr[0], pr[1]] = (st[pr] * jnp.where(row_lo, jnp.exp(end[u0]), jnp.exp(end[u1]))
                                    + _bdot_tn(xdt_e, bm[pr[0]]))
            o_scrs[pr[0]][rows_d[pr[0]], slice(pr[1] * W, (pr[1] + 1) * W)] = pick2(
                y_intra[u0] + y_state[u0], y_intra[u1] + y_state[u1])
        return carry

    lax.fori_loop(0, nc, body, 0)
    for d in (0, 1):
        for pp in range(npairs):
            sf_ref[d, pp] = st_scr[d, pp]

    tr = min(L, 256)

    def fin(i, carry):
        rows = pl.ds(pl.multiple_of(i * tr, tr), tr)
        y = of_scr[rows, :] + ob_scr[rows, :] + _silu(x_ref[rows, :]) * dskip_ref[...]
        y_ref[rows, :] = y * _silu(z_ref[rows, :])
        return carry

    lax.fori_loop(0, L // tr, fin, 0)


def _ssd_scan(proj3, bias_row, alog_row, ssd_d, s0):
    B, L = proj3.shape[0], proj3.shape[1]
    n_groups = 2
    n_pairs = SSD_HEADS // 2
    ppg = n_pairs // n_groups
    W = ppg * LANES
    s0p = s0.reshape(B, 2, n_pairs, 2 * SSD_P, SSD_N)
    dskip = jnp.repeat(ssd_d, SSD_P).reshape(n_groups, 1, W)
    col = lambda base: (lambda b, g: (b, 0, base // W + g))
    grp = lambda base: (lambda b, g: (b, 0, base // LANES + g))
    st_spec = pl.BlockSpec((None, 2, ppg, LANES, SSD_N), lambda b, g: (b, 0, g, 0, 0))
    y, sf = pl.pallas_call(
        functools.partial(_ssd_kernel, L=L),
        grid=(B, n_groups),
        in_specs=[pl.BlockSpec((None, L, W), col(C_SSDX)),
                  pl.BlockSpec((None, L, LANES), grp(C_SSDB)),
                  pl.BlockSpec((None, L, LANES), grp(C_SSDC)),
                  pl.BlockSpec((None, L, W), col(P_SSDZ)),
                  pl.BlockSpec((None, L, LANES), lambda b, g: (b, 0, P_SMALL // LANES)),
                  pl.BlockSpec((1, LANES), lambda b, g: (0, 0)),
                  pl.BlockSpec((1, LANES), lambda b, g: (0, 0)),
                  pl.BlockSpec((None, 1, W), lambda b, g: (g, 0, 0)),
                  st_spec],
        out_specs=[pl.BlockSpec((None, L, W), lambda b, g: (b, 0, g)), st_spec],
        out_shape=[jax.ShapeDtypeStruct((B, L, D_BRANCH), F32),
                   jax.ShapeDtypeStruct(s0p.shape, F32)],
        scratch_shapes=[pltpu.VMEM((L, W), F32), pltpu.VMEM((L, W), F32),
                        pltpu.VMEM((2, ppg, LANES, SSD_N), F32)],
        compiler_params=_cparams(("arbitrary", "arbitrary")),
        name="ssd_scan",
    )(proj3, proj3, proj3, proj3, proj3, bias_row, alog_row, dskip, s0p)
    return y, sf.reshape(s0.shape)


def _l2norm(a):
    return a * lax.rsqrt(jnp.sum(a * a, axis=-1, keepdims=True) + EPS)


def _gdn_kernel(q_ref, k_ref, v_ref, g_ref, sm_ref, bias_ref, alog_ref, nrm_ref, s0_ref,
                y_ref, sf_ref, of_scr, ob_scr, st_scr, *, L):
    C = GDN_CHUNK
    W = GDN_DK
    nc = L // C
    nh = q_ref.shape[1] // W
    hblk = pl.program_id(1)
    o_scrs = (of_scr, ob_scr)
    incls = (_tri_mask(C, False), _tri_mask(C, True))
    stricts = (_tri_mask(C, False, strict=True), _tri_mask(C, True, strict=True))
    for d in (0, 1):
        for hh in range(nh):
            st_scr[d, hh] = s0_ref[d, hh]

    units = [(d, hh) for d in (0, 1) for hh in range(nh)]
    eye = (_iota((C, C), 0) == _iota((C, C), 1)).astype(F32)

    def body(ci, carry):
        rows_d, cum_d, beta_d = [], [], []
        for d in (0, 1):
            c = (nc - 1 - ci) if d == 1 else ci
            rows = pl.ds(pl.multiple_of(c * C, C), C)
            raw = sm_ref[rows, :]
            rows_d.append(rows)
            cum_d.append(_cumsum_rows(-jnp.exp(alog_ref[...]) * _softplus(raw + bias_ref[...]), d == 1))
            beta_d.append(_sigmoid(raw))
        q, k, v, gc, grow, beta = {}, {}, {}, {}, {}, {}
        for u in units:
            d, hh = u
            cols = slice(hh * W, (hh + 1) * W)
            h = nh * hblk + hh
            q[u] = _l2norm(_silu(q_ref[rows_d[d], cols])) * (GDN_DK ** -0.5)
            k[u] = _l2norm(_silu(k_ref[rows_d[d], cols]))
            v[u] = _silu(v_ref[rows_d[d], cols])
            gc[u] = _lane_pick(cum_d[d], S_GA + d * GDN_HEADS + h)
            beta[u] = _lane_pick(beta_d[d], S_GB + d * GDN_HEADS + h)
        for u in units:
            grow[u] = _row_pick(cum_d[u[0]], S_GA + u[0] * GDN_HEADS + nh * hblk + u[1])
        decay = {u: _masked_exp(incls[u[0]], gc[u] - grow[u]) for u in units}
        kb = {u: k[u] * beta[u] for u in units}
        m = {u: -jnp.where(stricts[u[0]], _bdot_nt(kb[u], k[u]) * decay[u], 0.0) for u in units}
        aqk = {u: _bdot_nt(q[u], k[u]) * decay[u] for u in units}
        p = {u: eye + m[u] for u in units}
        m = {u: _dot3(m[u], m[u]) for u in units}
        sh = 4
        while sh < C:
            both = {u: _dot3(jnp.concatenate([m[u], p[u]], axis=0), m[u]) for u in units}
            p = {u: p[u] + both[u][C:2 * C] for u in units}
            m = {u: both[u][0:C] for u in units}
            sh *= 2
        p = {u: p[u] + _dot3(p[u], m[u]) for u in units}
        rhs = {u: jnp.concatenate([v[u] * beta[u], kb[u] * jnp.exp(gc[u])], axis=1) for u in units}
        uw = {u: _dot3(p[u], rhs[u]) for u in units}
        uu = {u: uw[u][:, 0:W] for u in units}
        ww = {u: uw[u][:, W:2 * W] for u in units}
        st = {u: st_scr[u[0], u[1]] for u in units}
        v_new = {u: uu[u] - _bdot(ww[u], st[u]) for u in units}
        o = {u: _bdot(q[u] * jnp.exp(gc[u]), st[u]) + _bdot(aqk[u], v_new[u]) for u in units}
        for u in units:
            d, hh = u
            g_end = gc[u][0:1, :] if d == 1 else gc[u][C - 1:C, :]
            st_scr[d, hh] = st[u] * jnp.exp(g_end) + _bdot_tn(k[u] * jnp.exp(g_end - gc[u]), v_new[u])
            o_scrs[d][rows_d[d], slice(hh * W, (hh + 1) * W)] = o[u]
        return carry

    lax.fori_loop(0, nc, body, 0)
    for d in (0, 1):
        for hh in range(nh):
            sf_ref[d, hh] = st_scr[d, hh]

    tr = min(L, 256)

    def fin(i, carry):
        rows = pl.ds(pl.multiple_of(i * tr, tr), tr)
        for hh in range(nh):
            cols = slice(hh * W, (hh + 1) * W)
            o = of_scr[rows, cols] + ob_scr[rows, cols]
            on = o * lax.rsqrt(jnp.mean(o * o, axis=-1, keepdims=True) + EPS) * nrm_ref[...]
            y_ref[rows, cols] = on * _silu(g_ref[rows, cols])
        return carry

    lax.fori_loop(0, L // tr, fin, 0)


def _gdn_scan(proj3, bias_row, alog_row, gdn_norm, s0):
    B, L = proj3.shape[0], proj3.shape[1]
    nh = GDN_HEADS
    W = nh * GDN_DK
    col = lambda base: (lambda b, h: (b, 0, base // W + h))
    st_spec = pl.BlockSpec((None, 2, nh, GDN_DK, GDN_DK), lambda b, h: (b, 0, h, 0, 0))
    once = pl.Buffered(1)
    return pl.pallas_call(
        functools.partial(_gdn_kernel, L=L),
        grid=(B, GDN_HEADS // nh),
        in_specs=[pl.BlockSpec((None, L, W), col(C_GQ), pipeline_mode=once),
                  pl.BlockSpec((None, L, W), col(C_GK), pipeline_mode=once),
                  pl.BlockSpec((None, L, W), col(C_GV), pipeline_mode=once),
                  pl.BlockSpec((None, L, W), col(P_GDNG), pipeline_mode=once),
                  pl.BlockSpec((None, L, LANES), lambda b, h: (b, 0, P_SMALL // LANES), pipeline_mode=once),
                  pl.BlockSpec((1, LANES), lambda b, h: (0, 0)),
                  pl.BlockSpec((1, LANES), lambda b, h: (0, 0)),
                  pl.BlockSpec((1, GDN_DK), lambda b, h: (0, 0)),
                  st_spec],
        out_specs=[pl.BlockSpec((None, L, W), lambda b, h: (b, 0, h)), st_spec],
        out_shape=[jax.ShapeDtypeStruct((B, L, D_BRANCH), F32),
                   jax.ShapeDtypeStruct(s0.shape, F32)],
        scratch_shapes=[pltpu.VMEM((L, W), F32), pltpu.VMEM((L, W), F32),
                        pltpu.VMEM((2, nh, GDN_DK, GDN_DK), F32)],
        compiler_params=_cparams(("arbitrary", "arbitrary")),
        name="gdn_scan",
    )(proj3, proj3, proj3, proj3, proj3, bias_row, alog_row, gdn_norm.reshape(1, GDN_DK), s0)


def _merge_kernel(ya_ref, yb_ref, yc_ref, yd_ref, mg_ref, x_ref, mod_ref, sn_ref, n2_ref, wb_ref, wo_ref,
                  xo_ref, h_ref):
    yc = yc_ref[...]
    half = D_BRANCH // 2
    parts = []
    for gidx in range(2):
        seg = yc[:, gidx * half:(gidx + 1) * half]
        parts.append(seg * lax.rsqrt(jnp.mean(seg * seg, axis=-1, keepdims=True) + EPS)
                     * sn_ref[:, gidx * half:(gidx + 1) * half])
    branches = (ya_ref[...], yb_ref[...], None, yd_ref[...])
    mixed = None
    for n in range(N_BRANCH):
        gate = _sigmoid(mg_ref[:, n * D_MODEL:(n + 1) * D_MODEL])
        if n == 2:
            lifted = (_dot(parts[0].astype(BF16), wb_ref[n, 0:half, :])
                      + _dot(parts[1].astype(BF16), wb_ref[n, half:D_BRANCH, :]))
        else:
            lifted = _dot(branches[n].astype(BF16), wb_ref[n])
        mixed = gate * lifted if mixed is None else mixed + gate * lifted
    m = mod_ref[...]
    x = x_ref[...] + m[2:3] * _dot(mixed.astype(BF16), wo_ref[...])
    xo_ref[...] = x
    xn = x * lax.rsqrt(jnp.mean(x * x, axis=-1, keepdims=True) + EPS) * n2_ref[...]
    h_ref[...] = xn * (1.0 + m[4:5]) + m[3:4]


def _merge(ya, yb, yc, yd, proj, x, mod, ssd_norm, norm2_g, wb_bf16, wo_bf16, *, tokens_per_mod):
    T = x.shape[0]
    tm = min(512, tokens_per_mod)
    tiles_per_mod = tokens_per_mod // tm
    yspec = pl.BlockSpec((tm, D_BRANCH), lambda i: (i, 0))
    xspec = pl.BlockSpec((tm, D_MODEL), lambda i: (i, 0))
    return pl.pallas_call(
        _merge_kernel,
        grid=(T // tm,),
        in_specs=[yspec, yspec, yspec, yspec,
                  pl.BlockSpec((tm, N_BRANCH * D_MODEL), lambda i: (i, P_MERGE // (N_BRANCH * D_MODEL))),
                  xspec,
                  pl.BlockSpec((None, 6, D_MODEL), lambda i: (i // tiles_per_mod, 0, 0)),
                  pl.BlockSpec((1, D_BRANCH), lambda i: (0, 0)),
                  pl.BlockSpec((1, D_MODEL), lambda i: (0, 0)),
                  pl.BlockSpec((N_BRANCH, D_BRANCH, D_MODEL), lambda i: (0, 0, 0)),
                  pl.BlockSpec((D_MODEL, D_MODEL), lambda i: (0, 0))],
        out_specs=[xspec, xspec],
        out_shape=[jax.ShapeDtypeStruct((T, D_MODEL), F32)] * 2,
        compiler_params=_cparams(("arbitrary",)),
        name="merge_outproj",
    )(ya, yb, yc, yd, proj, x, mod, ssd_norm.reshape(1, D_BRANCH), norm2_g.reshape(1, D_MODEL), wb_bf16, wo_bf16)


def _router_kernel(h_ref, rw_ref, rb_ref, wt_ref, lpos_ref, tlen_ref, tstart_ref, tcar_ref, carry):
    i = pl.program_id(0)
    tm = h_ref.shape[0]
    E = N_EXPERTS
    gsz = E // N_EXPERT_GROUPS
    neg_inf = -jnp.inf

    @pl.when(i == 0)
    def _():
        carry[...] = jnp.zeros_like(carry)

    scores = _sigmoid(_dot_nt(rw_ref[...], h_ref[...], precision=HIGHEST))
    biased = scores + rb_ref[...]
    eidx = _iota((E, tm), 0)
    ridx = _iota((gsz, tm), 0)

    slabs = [biased[g * gsz:(g + 1) * gsz, :] for g in range(N_EXPERT_GROUPS)]
    gs = []
    for v in slabs:
        m1 = jnp.max(v, axis=0, keepdims=True)
        i1 = jnp.min(jnp.where(v == m1, ridx, gsz), axis=0, keepdims=True)
        m2 = jnp.max(jnp.where(ridx == i1, neg_inf, v), axis=0, keepdims=True)
        gs.append(m1 + m2)
    masked = []
    for g in range(N_EXPERT_GROUPS):
        rank = jnp.zeros((1, tm), I32)
        for g2 in range(N_EXPERT_GROUPS):
            if g2 == g:
                continue
            ahead = (gs[g2] >= gs[g]) if g2 < g else (gs[g2] > gs[g])
            rank = rank + ahead.astype(I32)
        masked.append(jnp.where(rank < TOPK_GROUPS, slabs[g], MASK_NEG))
    cur = jnp.concatenate(masked, axis=0)

    krow = _iota((TOP_K, tm), 0)
    onehot = jnp.zeros((E, tm), F32)
    picks = []
    wsum = jnp.zeros((1, tm), F32)
    for kk in range(TOP_K):
        m = jnp.max(cur, axis=0, keepdims=True)
        ik = jnp.min(jnp.where(cur == m, eidx, E), axis=0, keepdims=True)
        hit = eidx == ik
        wk = jnp.sum(jnp.where(hit, scores, 0.0), axis=0, keepdims=True)
        cur = jnp.where(hit, neg_inf, cur)
        onehot = onehot + hit.astype(F32)
        picks.append((ik, wk))
        wsum = wsum + wk
    wt_out = jnp.zeros((TOP_K, tm), F32)
    for kk, (ik, wk) in enumerate(picks):
        wt_out = jnp.where(krow == kk, wk / wsum * ROUTED_SCALE, wt_out)

    earlier = (_iota((tm, tm), 0) < _iota((tm, tm), 1)).astype(BF16)
    before = _dot(onehot.astype(BF16), earlier)
    pad = lambda c: (c.astype(I32) + (RUN_ROWS - 1)) // RUN_ROWS * RUN_ROWS
    run_len = pad(jnp.sum(onehot, axis=1, keepdims=True) + jnp.zeros((E, LANES), F32)).astype(F32)
    run_start = _cumsum_rows(run_len, False) - run_len
    local = before + run_start[:, 0:1]
    lpos_out = jnp.zeros((TOP_K, tm), I32)
    for kk, (ik, wk) in enumerate(picks):
        lp = jnp.sum(jnp.where(eidx == ik, local, 0.0), axis=0, keepdims=True)
        lpos_out = jnp.where(krow == kk, lp.astype(I32), lpos_out)
    wt_ref[...] = wt_out
    lpos_ref[...] = lpos_out

    onehot_l = jnp.concatenate([onehot, jnp.zeros((LANES - E, tm), F32)], axis=0).astype(BF16)
    len_row = pad(_dot_nt(jnp.ones((8, tm), BF16), onehot_l)).astype(F32)
    lanes_before = (_iota((LANES, LANES), 0) < _iota((LANES, LANES), 1)).astype(BF16)
    tlen_ref[...] = len_row.astype(I32)
    tstart_ref[...] = _dot(len_row.astype(BF16), lanes_before).astype(I32)
    tcar_ref[...] = carry[...].astype(I32)
    carry[...] = carry[...] + len_row


def _router(h2, router_w, router_b, *, tm):
    T = h2.shape[0]
    nt = T // tm
    ospec = pl.BlockSpec((TOP_K, tm), lambda i: (0, i))
    tspec = pl.BlockSpec((None, 8, LANES), lambda i: (i, 0, 0))
    tshape = jax.ShapeDtypeStruct((nt, 8, LANES), I32)
    wt, lpos, tlen, tstart, tcar = pl.pallas_call(
        _router_kernel,
        grid=(nt,),
        in_specs=[pl.BlockSpec((tm, D_MODEL), lambda i: (i, 0)),
                  pl.BlockSpec((N_EXPERTS, D_MODEL), lambda i: (0, 0)),
                  pl.BlockSpec((N_EXPERTS, 1), lambda i: (0, 0))],
        out_specs=[ospec, ospec, tspec, tspec, tspec],
        out_shape=[jax.ShapeDtypeStruct((TOP_K, T), F32), jax.ShapeDtypeStruct((TOP_K, T), I32),
                   tshape, tshape, tshape],
        scratch_shapes=[pltpu.VMEM((8, LANES), F32)],
        compiler_params=_cparams(("arbitrary",)),
        name="moe_router",
    )(h2, router_w.T, router_b.reshape(N_EXPERTS, 1))
    table = lambda a: a[:, 0, :N_EXPERTS]
    return wt, lpos, table(tlen), table(tstart), table(tcar)


def _dispatch_kernel(len_ref, lst_ref, car_ref, seg_ref, lpos_ref, h_ref, xs_ref, xloc, sem):
    i = pl.program_id(0)
    tm = h_ref.shape[0]
    nloc = xloc.shape[0]
    R = RUN_ROWS
    lpos = lpos_ref[...]
    hb = h_ref[...].astype(BF16)
    rb = 256
    off = _iota((rb, tm), 0).astype(F32).astype(BF16)
    one = jnp.ones((rb, tm), BF16)
    for blk in range(nloc // rb):
        cand = jnp.where(lpos // rb == blk, lpos % rb, -1).astype(F32).astype(BF16)
        owner = jnp.zeros((rb, tm), BF16)
        for kk in range(TOP_K):
            owner = jnp.where(off == cand[kk:kk + 1, :], one, owner)
        rows = _dot(owner, hb)
        bits = lax.bitcast_convert_type(rows, jnp.uint32)
        xloc[blk * rb:(blk + 1) * rb, :] = bits[:, 0:D_XPACK] | (bits[:, D_XPACK:D_MODEL] >> 16)

    def piece(src_row, dst_row):
        return pltpu.make_async_copy(xloc.at[pl.ds(pl.multiple_of(src_row, R), R), :],
                                     xs_ref.at[pl.ds(pl.multiple_of(dst_row, R), R), :], sem)

    def per_expert(e, issued):
        pieces = len_ref[i, e] // R
        ls = lst_ref[i, e]
        gd = seg_ref[e] + car_ref[i, e]

        def issue(j, c):
            piece(ls + j * R, gd + j * R).start()
            return c

        lax.fori_loop(0, pieces, issue, 0)
        return issued + pieces

    total = lax.fori_loop(0, N_EXPERTS, per_expert, 0)

    def wait(j, c):
        piece(0, 0).wait()
        return c

    lax.fori_loop(0, total, wait, 0)


def _dispatch(h2, lpos, tile_len, tile_lstart, tile_car, seg_start, *, tm):
    T = h2.shape[0]
    nt = T // tm
    n_rows = T * TOP_K + N_EXPERTS * nt * RUN_ROWS
    nloc = tm * TOP_K + N_EXPERTS * RUN_ROWS
    return pl.pallas_call(
        _dispatch_kernel,
        grid_spec=pltpu.PrefetchScalarGridSpec(
            num_scalar_prefetch=4,
            grid=(nt,),
            in_specs=[pl.BlockSpec((TOP_K, tm), lambda i, *_: (0, i)),
                      pl.BlockSpec((tm, D_MODEL), lambda i, *_: (i, 0))],
            out_specs=pl.BlockSpec(memory_space=pl.ANY),
            scratch_shapes=[pltpu.VMEM((nloc, D_XPACK), jnp.uint32), pltpu.SemaphoreType.DMA(())]),
        out_shape=jax.ShapeDtypeStruct((n_rows, D_XPACK), jnp.uint32),
        compiler_params=_cparams(("arbitrary",)),
        name="moe_dispatch",
    )(tile_len, tile_lstart, tile_car, seg_start, lpos, h2)


def _expert_kernel(blk_ref, exp_ref, lo_ref, hi_ref, first_ref, x_ref, w1_ref, w3_ref, w2_ref, o_ref):
    w = pl.program_id(0)
    packed = x_ref[...]
    x_a = lax.bitcast_convert_type(packed & jnp.uint32(0xFFFF0000), F32).astype(BF16)
    x_b = lax.bitcast_convert_type(packed << 16, F32).astype(BF16)
    w1 = w1_ref[...].astype(BF16)
    w3 = w3_ref[...].astype(BF16)
    a = _dot(x_a, w1[0:D_XPACK]) + _dot(x_b, w1[D_XPACK:D_MODEL])
    b = _dot(x_a, w3[0:D_XPACK]) + _dot(x_b, w3[D_XPACK:D_MODEL])
    y = _dot((_silu(a) * b).astype(BF16), w2_ref[...].astype(BF16))
    r = _iota(y.shape, 0)
    y = jnp.where((r >= lo_ref[w]) & (r < hi_ref[w]), y, 0.0)

    @pl.when(first_ref[w] == 1)
    def _():
        o_ref[...] = y

    @pl.when(first_ref[w] == 0)
    def _():
        o_ref[...] = o_ref[...] + y


def _experts(xs, work, w1, w3, w2, *, layer, bm):
    A = xs.shape[0]
    n_work = work[0].shape[0]
    xmap = lambda w, blk, ex, lo, hi, first: (blk[w], 0)
    wmap = lambda w, blk, ex, lo, hi, first: (layer, ex[w], 0, 0)
    return pl.pallas_call(
        _expert_kernel,
        grid_spec=pltpu.PrefetchScalarGridSpec(
            num_scalar_prefetch=5,
            grid=(n_work,),
            in_specs=[pl.BlockSpec((bm, D_XPACK), xmap),
                      pl.BlockSpec((None, None, D_MODEL, D_EXPERT), wmap),
                      pl.BlockSpec((None, None, D_MODEL, D_EXPERT), wmap),
                      pl.BlockSpec((None, None, D_EXPERT, D_MODEL), wmap)],
            out_specs=pl.BlockSpec((bm, D_MODEL), xmap)),
        out_shape=jax.ShapeDtypeStruct((A, D_MODEL), F32),
        compiler_params=_cparams(("arbitrary",)),
        name="moe_experts",
    )(*work, xs, w1, w3, w2)


def _combine_kernel(len_ref, lst_ref, car_ref, seg_ref, lpos_ref, wt_ref, ys_ref, h_ref, x_ref, mod_ref,
                    s1_ref, s3_ref, s2_ref, fg_ref, o_ref, yloc, sem, *, final_norm):
    i = pl.program_id(0)
    tm = h_ref.shape[0]
    nloc = yloc.shape[0]
    R = RUN_ROWS

    @pl.when(i == 0)
    def _():
        yloc[...] = jnp.zeros_like(yloc)

    def piece(src_row, dst_row):
        return pltpu.make_async_copy(ys_ref.at[pl.ds(pl.multiple_of(src_row, R), R), :],
                                     yloc.at[pl.ds(pl.multiple_of(dst_row, R), R), :], sem)

    def per_expert(e, issued):
        pieces = len_ref[i, e] // R
        ls = lst_ref[i, e]
        gd = seg_ref[e] + car_ref[i, e]

        def issue(j, c):
            piece(gd + j * R, ls + j * R).start()
            return c

        lax.fori_loop(0, pieces, issue, 0)
        return issued + pieces

    total = lax.fori_loop(0, N_EXPERTS, per_expert, 0)

    hb = h_ref[...].astype(BF16)
    mid = _silu(_dot(hb, s1_ref[...])) * _dot(hb, s3_ref[...])
    y = _dot(mid.astype(BF16), s2_ref[...])

    def wait(j, c):
        piece(0, 0).wait()
        return c

    lax.fori_loop(0, total, wait, 0)

    lpos = lpos_ref[...]
    wt = wt_ref[...]
    used_rows = lst_ref[i, N_EXPERTS - 1] + len_ref[i, N_EXPERTS - 1]
    rb = 256

    def block_sum(blk):
        r = _iota((rb, tm), 0) + blk * rb
        pw = jnp.zeros((rb, tm), F32)
        for kk in range(TOP_K):
            pw = jnp.where(r == lpos[kk:kk + 1, :], wt[kk:kk + 1, :], pw)
        ysc = yloc[blk * rb:(blk + 1) * rb, :] * jnp.sum(pw, axis=1, keepdims=True)
        hi = ysc.astype(BF16)
        lo = (ysc - hi.astype(F32)).astype(BF16)
        owner = (pw != 0.0).astype(BF16)
        return _dot_tn(owner, hi) + _dot_tn(owner, lo)

    for blk in range(nloc // rb):
        if (blk + 1) * rb <= tm * TOP_K:
            y = y + block_sum(blk)
        else:
            y = y + lax.cond(used_rows > blk * rb, functools.partial(block_sum, blk),
                             lambda: jnp.zeros((tm, D_MODEL), F32))
    x = x_ref[...] + mod_ref[5:6, :] * y
    if final_norm:
        x = x * lax.rsqrt(jnp.mean(x * x, axis=-1, keepdims=True) + EPS) * fg_ref[...]
    o_ref[...] = x


def _combine(ys, lpos, wt, tile_len, tile_lstart, tile_car, seg_start, h2, x, mod, s1, s3, s2, final_g,
             *, tm, tokens_per_mod, final_norm):
    T = x.shape[0]
    tiles_per_mod = tokens_per_mod // tm
    nloc = tm * TOP_K + N_EXPERTS * RUN_ROWS
    xspec = pl.BlockSpec((tm, D_MODEL), lambda i, *_: (i, 0))
    kspec = pl.BlockSpec((TOP_K, tm), lambda i, *_: (0, i))
    const = lambda i, *_: (0, 0)
    return pl.pallas_call(
        functools.partial(_combine_kernel, final_norm=final_norm),
        grid_spec=pltpu.PrefetchScalarGridSpec(
            num_scalar_prefetch=4,
            grid=(T // tm,),
            in_specs=[kspec, kspec,
                      pl.BlockSpec(memory_space=pl.ANY),
                      xspec, xspec,
                      pl.BlockSpec((None, 6, D_MODEL), lambda i, *_: (i // tiles_per_mod, 0, 0)),
                      pl.BlockSpec((D_MODEL, D_SHARED), const),
                      pl.BlockSpec((D_MODEL, D_SHARED), const),
                      pl.BlockSpec((D_SHARED, D_MODEL), const),
                      pl.BlockSpec((1, D_MODEL), const)],
            out_specs=xspec,
            scratch_shapes=[pltpu.VMEM((nloc, D_MODEL), F32), pltpu.SemaphoreType.DMA(())]),
        out_shape=jax.ShapeDtypeStruct((T, D_MODEL), F32),
        compiler_params=_cparams(("arbitrary",)),
        name="moe_combine",
    )(tile_len, tile_lstart, tile_car, seg_start, lpos, wt, ys, h2, x, mod, s1, s3, s2, final_g.reshape(1, D_MODEL))


def _work_list(counts, starts, n_rows, bm):
    E = N_EXPERTS
    n_blocks = n_rows // bm
    n_work = n_blocks + E - 1
    ends = starts + counts
    first_blk = starts // bm
    last_blk = jnp.where(counts > 0, (ends - 1) // bm, first_blk)
    n_items = jnp.where(counts > 0, last_blk - first_blk + 1, 0)
    item_end = jnp.cumsum(n_items)
    item_start = item_end - n_items
    w = jnp.arange(n_work, dtype=I32)
    ex = jnp.minimum(jnp.sum((item_end[None, :] <= w[:, None]).astype(I32), axis=1), E - 1)
    valid = w < item_end[-1]
    blk = first_blk[ex] + (w - item_start[ex])
    blk = jnp.where(valid, blk, blk[jnp.maximum(item_end[-1] - 1, 0)]).astype(I32)
    lo = jnp.clip(starts[ex] - blk * bm, 0, bm)
    hi = jnp.clip(ends[ex] - blk * bm, 0, bm)
    lo = jnp.where(valid, lo, 0).astype(I32)
    hi = jnp.where(valid, hi, 0).astype(I32)
    ex = jnp.where(valid, ex, ex[jnp.maximum(item_end[-1] - 1, 0)])
    prev_blk = jnp.concatenate([jnp.full((1,), -1, I32), blk[:-1]])
    first = (blk != prev_blk).astype(I32)
    return blk, ex.astype(I32), lo, hi, first


def _moe(h2, x, mod, p, final_g, *, layer, tokens_per_mod, final_norm):
    T = x.shape[0]
    tm = 256
    wt, lpos, tile_len, tile_lstart, tile_car = _router(h2, p["router_w"], p["router_b"], tm=tm)
    seg_len = tile_car[-1] + tile_len[-1]
    seg_start = jnp.cumsum(seg_len) - seg_len
    xs = _dispatch(h2, lpos, tile_len, tile_lstart, tile_car, seg_start, tm=tm)
    bm = 512
    work = _work_list(seg_len, seg_start, xs.shape[0], bm)
    ys = _experts(xs, work, p["exp_w1"], p["exp_w3"], p["exp_w2"], layer=layer, bm=bm)
    return _combine(ys, lpos, wt, tile_len, tile_lstart, tile_car, seg_start, h2, x, mod,
                    p["sh_w1"], p["sh_w3"], p["sh_w2"], final_g,
                    tm=tm, tokens_per_mod=tokens_per_mod, final_norm=final_norm)


def _reorder_w_in(w_in):
    plain = D_CONV
    pieces = [w_in[..., :plain],
              w_in[..., plain + 3616:plain + 3616 + 4096],
              w_in[..., plain:plain + 3072],
              w_in[..., plain + 3088:plain + 3600],
              w_in[..., plain + 3072:plain + 3088],
              w_in[..., plain + 3600:plain + 3616]]
    out = jnp.concatenate(pieces, axis=-1)
    pad = D_PROJ - out.shape[-1]
    return jnp.pad(out, ((0, 0), (0, 0), (0, pad))).astype(BF16)


def _small_rows(ssd_vals, gdn_vals):
    row = jnp.zeros((LANES,), F32)
    row = row.at[S_DT:S_DT + 2 * SSD_HEADS].set(ssd_vals.reshape(-1))
    row = row.at[S_GA:S_GA + 2 * GDN_HEADS].set(gdn_vals.reshape(-1))
    return row.reshape(1, LANES)


def _layer_pass(x, mod, lp, hy, st_hg, st_ssd, st_gdn, final_g, *, B, L, tokens_per_mod, seg, layer, final_norm):
    T = B * L
    proj = _in_projection(x, mod, lp["norm1_g"], lp["w_in"], lp["conv_w"], lp["conv_b"],
                          layer=layer, tokens_per_mod=tokens_per_mod, seg=seg)
    proj3 = proj.reshape(B, L, D_PROJ)
    fmat, gmat, hspec = hy
    cb = lambda c: c // D_BRANCH
    z = _spectral_conv(fmat, gmat, hspec, 0, lp["hy_bias"][0], proj3, cb(C_HYV), proj3, cb(C_HYX1))
    ya = _spectral_conv(fmat, gmat, hspec, 1, lp["hy_bias"][1], z, 0, proj3, cb(C_HYX2))
    yb, s_hg = _hgrn_scan(proj3, lp["hg_lb"], lp["hg_norm"], st_hg, layer=layer)
    yc, s_ssd = _ssd_scan(proj3, lp["bias_row"], lp["alog_row"], lp["ssd_d"], st_ssd)
    yd, s_gdn = _gdn_scan(proj3, lp["bias_row"], lp["alog_row"], lp["gdn_norm"], st_gdn)
    flat = lambda a: a.reshape(T, D_BRANCH)
    x, h2 = _merge(flat(ya), flat(yb), flat(yc), flat(yd), proj, x, mod, lp["ssd_norm"], lp["norm2_g"],
                   lp["w_branch"], lp["w_out"], tokens_per_mod=tokens_per_mod)
    x = _moe(h2, x, mod, lp, final_g, layer=layer, tokens_per_mod=tokens_per_mod, final_norm=final_norm)
    return x, s_hg, s_ssd, s_gdn


def kernel(x_prompt, x_sample, state_hgrn, state_ssd, state_gdn, c, c_ctx, norm1_g, norm2_g, ada_w, ada_b, w_in, conv_w, conv_b, hy_w1, hy_b1, hy_w2, hy_b2, hy_w3, hy_bias, hg_lb, hg_norm, ssd_a_log, ssd_dt_bias, ssd_d, ssd_norm, gdn_a_log, gdn_dt_bias, gdn_norm, w_branch, w_out, router_w, router_bias, exp_w1, exp_w3, exp_w2, sh_w1, sh_w3, sh_w2, final_g):
    depth = w_in.shape[0]
    bp, lp_len = x_prompt.shape[0], x_prompt.shape[1]
    bs, ls_len = x_sample.shape[0], x_sample.shape[1]
    D = D_MODEL

    w_in_r = _reorder_w_in(w_in)
    hy_w1p = jnp.pad(hy_w1, ((0, 0), (0, LANES - hy_w1.shape[1]), (0, 0)))
    layers = []
    for l in range(depth):
        layers.append(dict(
            norm1_g=norm1_g[l], norm2_g=norm2_g[l], w_in=w_in_r, conv_w=conv_w[l], conv_b=conv_b[l],
            hy_bias=hy_bias[l], hg_lb=hg_lb, hg_norm=hg_norm[l],
            bias_row=_small_rows(ssd_dt_bias[l], gdn_dt_bias[l]),
            alog_row=_small_rows(ssd_a_log[l], gdn_a_log[l]),
            ssd_d=ssd_d[l], ssd_norm=ssd_norm[l], gdn_norm=gdn_norm[l],
            w_branch=w_branch[l].astype(BF16), w_out=w_out[l].astype(BF16),
            router_w=router_w[l], router_b=router_bias[l],
            exp_w1=exp_w1, exp_w3=exp_w3, exp_w2=exp_w2,
            sh_w1=sh_w1[l].astype(BF16), sh_w3=sh_w3[l].astype(BF16), sh_w2=sh_w2[l].astype(BF16)))

    def hyena_setup(L):
        fmat, fs = _dft_matrices(L)
        gmat = fs.T
        specs = []
        for l in range(depth):
            filt = _hyena_filters(L, hy_w1p[l], hy_b1[l], hy_w2[l], hy_b2[l], hy_w3[l])
            specs.append(_filter_spectrum(fmat, filt))
        return fmat, gmat, specs

    cond = jnp.concatenate([c_ctx.reshape(1, D), c], axis=0)
    rows = cond.shape[0]
    rows8 = (rows + 7) // 8 * 8
    cond8 = jnp.pad(cond, ((0, rows8 - rows), (0, 0)))
    mods = [_modulation(cond8, ada_w, ada_b[l], layer=l).reshape(rows8, 6, D) for l in range(depth)]

    fmat, gmat, specs = hyena_setup(lp_len)
    x = x_prompt.reshape(bp * lp_len, D)
    z_hg = jnp.zeros((bp, 2, HG_HEADS, HG_DK, HG_DK), F32)
    z_ssd = jnp.zeros((bp, 2, SSD_HEADS, SSD_P, SSD_N), F32)
    z_gdn = jnp.zeros((bp, 2, GDN_HEADS, GDN_DK, GDN_DK), F32)
    hg_states, ssd_states, gdn_states = [], [], []
    for l in range(depth):
        x, s_hg, s_ssd, s_gdn = _layer_pass(
            x, mods[l][0:1], layers[l], (fmat, gmat, specs[l]), z_hg, z_ssd, z_gdn, final_g,
            B=bp, L=lp_len, tokens_per_mod=bp * lp_len, seg=lp_len, layer=l, final_norm=(l == depth - 1))
        hg_states.append(s_hg)
        ssd_states.append(s_ssd)
        gdn_states.append(s_gdn)
    y_prompt = x.reshape(bp, lp_len, D)
    new_hg = jnp.stack(hg_states, axis=1)
    new_ssd = jnp.stack(ssd_states, axis=1)
    new_gdn = jnp.stack(gdn_states, axis=1)

    fmat, gmat, specs = hyena_setup(ls_len)
    x = x_sample.reshape(bs * ls_len, D)
    for l in range(depth):
        x, _, _, _ = _layer_pass(
            x, mods[l][1:1 + bs], layers[l], (fmat, gmat, specs[l]),
            state_hgrn[:, l], state_ssd[:, l], state_gdn[:, l], final_g,
            B=bs, L=ls_len, tokens_per_mod=ls_len, seg=GRID_W, layer=l, final_norm=(l == depth - 1))
    y_sample = x.reshape(bs, ls_len, D)
    return (y_prompt, y_sample, new_hg, new_ssd, new_gdn)
```

```python
import functools
import math

import numpy as np
import jax
import jax.numpy as jnp
from jax import lax
from jax.experimental import pallas as pl
from jax.experimental.pallas import tpu as pltpu

F32 = jnp.float32
BF16 = jnp.bfloat16
I32 = jnp.int32
HIGHEST = lax.Precision.HIGHEST

D_MODEL = 1024
GRID_W = 64
EPS = 1e-6
LOG_FLOOR = 1e-30
MASK_NEG = -1e30
N_BRANCH = 4
D_BRANCH = 512
HY_POS_FREQS = 16
HY_FILTER_HIDDEN = 64
HY_FAST_DECAY = 0.3
HY_SLOW_DECAY = 1.5
HY_DECAY_TARGET = 1e-2
HG_HEADS = 4
HG_DK = 128
HG_CHUNK = 16
SSD_HEADS = 8
SSD_P = 64
SSD_N = 128
SSD_CHUNK = 64
GDN_HEADS = 4
GDN_DK = 128
GDN_CHUNK = 64
N_EXPERTS = 64
TOP_K = 8
N_EXPERT_GROUPS = 8
TOPK_GROUPS = 4
D_EXPERT = 256
D_SHARED = 256
ROUTED_SCALE = 2.5

LANES = 128
D_XPACK = D_MODEL // 2
RUN_ROWS = 8

D_CONV = 4096
C_HYV, C_HYX1, C_HYX2 = 0, 512, 1024
C_SSDX, C_SSDB, C_SSDC = 1536, 2048, 2304
C_GQ, C_GK, C_GV = 2560, 3072, 3584
P_MERGE = 4096
P_HGQ, P_HGFF, P_HGFB, P_HGV, P_HGG = 8192, 8704, 9216, 9728, 10240
P_SSDZ, P_GDNG, P_SMALL = 10752, 11264, 11776
D_PROJ = 12288
S_DT, S_GA, S_GB = 0, 16, 24

VMEM_LIMIT = 56 * 1024 * 1024


def _cparams(sem):
    return pltpu.CompilerParams(dimension_semantics=sem, vmem_limit_bytes=VMEM_LIMIT)


def _sigmoid(x):
    return 1.0 / (1.0 + jnp.exp(-x))


def _silu(x):
    return x * _sigmoid(x)


def _softplus(x):
    return jnp.maximum(x, 0.0) + jnp.log(1.0 + jnp.exp(-jnp.abs(x)))


def _dot(a, b, precision=None):
    return jnp.dot(a, b, preferred_element_type=F32, precision=precision)


def _dot_nt(a, b, precision=None):
    return lax.dot_general(a, b, (((1,), (1,)), ((), ())), preferred_element_type=F32, precision=precision)


def _dot_tn(a, b, precision=None):
    return lax.dot_general(a, b, (((0,), (0,)), ((), ())), preferred_element_type=F32, precision=precision)


def _bdot(a, b):
    return _dot(a.astype(BF16), b.astype(BF16))


def _bdot_nt(a, b):
    return _dot_nt(a.astype(BF16), b.astype(BF16))


def _bdot_tn(a, b):
    return _dot_tn(a.astype(BF16), b.astype(BF16))


def _iota(shape, dim):
    return lax.broadcasted_iota(I32, shape, dim)


def _cumsum_rows(g, reverse):
    n = g.shape[0]
    row = _iota(g.shape, 0)
    sh = 1
    while sh < n:
        if reverse:
            g = g + jnp.where(row < n - sh, pltpu.roll(g, n - sh, 0), 0.0)
        else:
            g = g + jnp.where(row >= sh, pltpu.roll(g, sh, 0), 0.0)
        sh *= 2
    return g


def _lane_pick(a, j):
    return jnp.sum(jnp.where(_iota(a.shape, 1) == j, a, 0.0), axis=1, keepdims=True)


def _split3(a):
    a1 = a.astype(BF16)
    r1 = a - a1.astype(F32)
    a2 = r1.astype(BF16)
    a3 = (r1 - a2.astype(F32)).astype(BF16)
    return a1, a2, a3


def _dot3(a, b):
    a1, a2, _ = _split3(a)
    b1, b2, _ = _split3(b)
    return _dot(a1, b1) + (_dot(a1, b2) + _dot(a2, b1))


def _row_pick(a, j):
    sel = (_iota((8, a.shape[1]), 1) == j).astype(BF16)
    a1, a2, a3 = _split3(a)
    return (_dot_nt(sel, a1) + (_dot_nt(sel, a2) + _dot_nt(sel, a3)))[0:1, :]


def _tri_mask(n, reverse, strict=False):
    t = _iota((n, n), 0)
    s = _iota((n, n), 1)
    if reverse:
        return (s > t) if strict else (s >= t)
    return (s < t) if strict else (s <= t)


def _masked_exp(mask, diff):
    return jnp.where(mask, jnp.exp(jnp.where(mask, diff, 0.0)), 0.0)


def _mod_kernel(c_ref, w_ref, b_ref, o_ref):
    o_ref[...] = _dot(_silu(c_ref[...]), w_ref[...], precision=HIGHEST) + b_ref[...]


def _modulation(cond8, ada_w, ada_b, *, layer):
    rows = cond8.shape[0]
    tn = 1536
    n = ada_w.shape[2]
    return pl.pallas_call(
        _mod_kernel,
        grid=(n // tn,),
        in_specs=[pl.BlockSpec((rows, D_MODEL), lambda j: (0, 0)),
                  pl.BlockSpec((None, D_MODEL, tn), lambda j: (layer, 0, j)),
                  pl.BlockSpec((1, tn), lambda j: (0, j))],
        out_specs=pl.BlockSpec((rows, tn), lambda j: (0, j)),
        out_shape=jax.ShapeDtypeStruct((rows, n), F32),
        compiler_params=_cparams(("arbitrary",)),
        name="adaln_mod",
    )(cond8, ada_w, ada_b.reshape(1, n))


def _inproj_kernel(x_ref, mod_ref, g_ref, w_ref, cw_ref, cb_ref, o_ref, h_scr, *, seg, n_conv_tiles):
    j = pl.program_id(1)

    @pl.when(j == 0)
    def _():
        x = x_ref[...]
        xn = x * lax.rsqrt(jnp.mean(x * x, axis=-1, keepdims=True) + EPS) * g_ref[...]
        m = mod_ref[...]
        h_scr[...] = (xn * (1.0 + m[1:2]) + m[0:1]).astype(BF16)

    y = _dot(h_scr[...], w_ref[...])

    @pl.when(j < n_conv_tiles)
    def _():
        tm = y.shape[0]
        pos = _iota(y.shape, 0) & (seg - 1)
        prev = jnp.where(pos == 0, 0.0, pltpu.roll(y, 1, 0))
        nxt = jnp.where(pos == seg - 1, 0.0, pltpu.roll(y, tm - 1, 0))
        cw = cw_ref[...]
        o_ref[...] = cb_ref[...] + prev * cw[0:1] + y * cw[1:2] + nxt * cw[2:3]

    @pl.when(j >= n_conv_tiles)
    def _():
        o_ref[...] = y


def _in_projection(x, mod, norm_g, w_bf16, conv_w, conv_b, *, layer, tokens_per_mod, seg):
    T = x.shape[0]
    tm = min(2048, tokens_per_mod)
    tn = 1024
    n_conv_tiles = D_CONV // tn
    tiles_per_mod = tokens_per_mod // tm
    kern = functools.partial(_inproj_kernel, seg=seg, n_conv_tiles=n_conv_tiles)
    cmap = lambda i, j: (0, jnp.minimum(j, n_conv_tiles - 1))
    return pl.pallas_call(
        kern,
        grid=(T // tm, D_PROJ // tn),
        in_specs=[pl.BlockSpec((tm, D_MODEL), lambda i, j: (i, 0)),
                  pl.BlockSpec((None, 6, D_MODEL), lambda i, j: (i // tiles_per_mod, 0, 0)),
                  pl.BlockSpec((1, D_MODEL), lambda i, j: (0, 0)),
                  pl.BlockSpec((None, D_MODEL, tn), lambda i, j: (layer, 0, j)),
                  pl.BlockSpec((3, tn), cmap),
                  pl.BlockSpec((1, tn), cmap)],
        out_specs=pl.BlockSpec((tm, tn), lambda i, j: (i, j)),
        out_shape=jax.ShapeDtypeStruct((T, D_PROJ), F32),
        scratch_shapes=[pltpu.VMEM((tm, D_MODEL), BF16)],
        compiler_params=_cparams(("arbitrary", "arbitrary")),
        name="in_proj",
    )(x, mod, norm_g.reshape(1, D_MODEL), w_bf16, conv_w, conv_b.reshape(1, D_CONV))


def _hyfilt_kernel(w1_ref, b1_ref, w2_ref, b2_ref, w3_ref, o_ref, *, L):
    i = pl.program_id(1)
    tl = o_ref.shape[0]
    t = (_iota((tl, LANES), 0) + i * tl).astype(F32) / L
    lane = _iota((tl, LANES), 1)
    band = jnp.where(lane <= HY_POS_FREQS, lane, lane - HY_POS_FREQS).astype(F32)
    ang = 2.0 * math.pi * t * band
    feats = jnp.where(lane == 0, t,
                      jnp.where(lane <= HY_POS_FREQS, jnp.sin(ang),
                                jnp.where(lane <= 2 * HY_POS_FREQS, jnp.cos(ang), 0.0)))
    hdn = jnp.sin(_dot(feats, w1_ref[...], precision=HIGHEST) + b1_ref[...])
    hdn = jnp.sin(_dot(hdn, w2_ref[...], precision=HIGHEST) + b2_ref[...])
    filt = _dot(hdn, w3_ref[...], precision=HIGHEST)
    max_decay = math.log(HY_DECAY_TARGET) / HY_FAST_DECAY
    min_decay = math.log(HY_DECAY_TARGET) / HY_SLOW_DECAY
    n = filt.shape[1]
    ch = (_iota((tl, n), 1) & (D_BRANCH - 1)).astype(F32)
    delta = min_decay + ch * ((max_decay - min_decay) / (D_BRANCH - 1))
    tt = (_iota((tl, n), 0) + i * tl).astype(F32) / L
    o_ref[...] = filt * jnp.exp(-tt * jnp.abs(delta))


def _hyena_filters(L, w1p, b1, w2, b2, w3):
    tl = min(L, 256)
    n = 2 * D_BRANCH
    return pl.pallas_call(
        functools.partial(_hyfilt_kernel, L=L),
        grid=(2, L // tl),
        in_specs=[pl.BlockSpec((LANES, HY_FILTER_HIDDEN), lambda d, i: (0, 0)),
                  pl.BlockSpec((1, HY_FILTER_HIDDEN), lambda d, i: (0, 0)),
                  pl.BlockSpec((HY_FILTER_HIDDEN, HY_FILTER_HIDDEN), lambda d, i: (0, 0)),
                  pl.BlockSpec((1, HY_FILTER_HIDDEN), lambda d, i: (0, 0)),
                  pl.BlockSpec((HY_FILTER_HIDDEN, n), lambda d, i: (0, d))],
        out_specs=pl.BlockSpec((None, tl, n), lambda d, i: (d, i, 0)),
        out_shape=jax.ShapeDtypeStruct((2, L, n), F32),
        compiler_params=_cparams(("arbitrary", "arbitrary")),
        name="hyena_filters",
    )(w1p, b1.reshape(1, -1), w2, b2.reshape(1, -1), w3)


def _dftgen_kernel(f_ref, fs_ref, *, L, tk):
    i = pl.program_id(0)
    N = 2 * L
    k = _iota((tk, LANES), 0) + i * tk
    lane = _iota((tk, LANES), 1)
    w = 2.0 * math.pi / N
    a0 = ((k * lane) & (N - 1)).astype(F32) * w
    c0, s0 = jnp.cos(a0), jnp.sin(a0)
    a1 = ((k * lane * LANES) & (N - 1)).astype(F32) * w
    c1, s1 = jnp.cos(a1), jnp.sin(a1)
    alt = jnp.where((lane & 1) == 0, 1.0, -1.0)
    coef = jnp.where(k == 0, 1.0 / N, 2.0 / N)
    for n1 in range(L // LANES):
        c1b = c1[:, n1:n1 + 1]
        s1b = s1[:, n1:n1 + 1]
        cosb = c1b * c0 - s1b * s0
        sinb = jnp.where(k == 0, alt, -(s1b * c0 + c1b * s0))
        cols = slice(n1 * LANES, (n1 + 1) * LANES)
        f_ref[0:tk, cols] = cosb.astype(BF16)
        f_ref[tk:2 * tk, cols] = sinb.astype(BF16)
        fs_ref[0:tk, cols] = (coef * cosb).astype(BF16)
        fs_ref[tk:2 * tk, cols] = (coef * sinb).astype(BF16)


def _dft_matrices(L):
    tk = min(L, 256)
    spec = pl.BlockSpec((2 * tk, L), lambda i: (i, 0))
    return pl.pallas_call(
        functools.partial(_dftgen_kernel, L=L, tk=tk),
        grid=(L // tk,),
        in_specs=[],
        out_specs=[spec, spec],
        out_shape=[jax.ShapeDtypeStruct((2 * L, L), BF16)] * 2,
        compiler_params=_cparams(("arbitrary",)),
        name="dft_matrices",
    )()


def _dfth_kernel(f_ref, h_ref, o_ref, hp_scr, *, tk):
    i = pl.program_id(1)
    n = o_ref.shape[1]

    @pl.when(i == 0)
    def _():
        hc = h_ref[0]
        ha = jnp.where(_iota(hc.shape, 0) == 0, 0.0, h_ref[1])
        hp_scr[:, 0:n] = (hc + ha).astype(BF16)
        hp_scr[:, n:2 * n] = (hc - ha).astype(BF16)

    u = _dot(f_ref[...], hp_scr[...])
    r = _iota((2 * tk, n), 0)
    from_sum = (r < tk) | ((r == tk) & (i == 0))
    o_ref[...] = jnp.where(from_sum, u[:, 0:n], u[:, n:2 * n])


def _filter_spectrum(fmat, filt):
    L = fmat.shape[1]
    tk = min(L, 256)
    C = filt.shape[2]
    tc = 256
    return pl.pallas_call(
        functools.partial(_dfth_kernel, tk=tk),
        grid=(C // tc, L // tk),
        in_specs=[pl.BlockSpec((2 * tk, L), lambda c, i: (i, 0)),
                  pl.BlockSpec((2, L, tc), lambda c, i: (0, 0, c))],
        out_specs=pl.BlockSpec((2 * tk, tc), lambda c, i: (i, c)),
        out_shape=jax.ShapeDtypeStruct((2 * L, C), F32),
        scratch_shapes=[pltpu.VMEM((L, 2 * tc), BF16)],
        compiler_params=_cparams(("arbitrary", "arbitrary")),
        name="filter_spectrum",
    )(fmat, filt)


def _dfta_kernel(f_ref, u_ref, h_ref, y_ref, u_scr, *, tk):
    i = pl.program_id(1)

    @pl.when(i == 0)
    def _():
        u_scr[...] = u_ref[...].astype(BF16)

    uf = _dot(f_ref[...], u_scr[...])
    ur, ui = uf[0:tk], uf[tk:2 * tk]
    hr, hi = h_ref[0:tk, :], h_ref[tk:2 * tk, :]
    dc = (_iota(ur.shape, 0) == 0) & (i == 0)
    y_ref[0:tk, :] = jnp.where(dc, ur * hr, ur * hr - ui * hi).astype(BF16)
    y_ref[tk:2 * tk, :] = jnp.where(dc, ui * hi, ur * hi + ui * hr).astype(BF16)


def _dftb_kernel(g_ref, y_ref, u_ref, x_ref, b_ref, o_ref):
    y = _dot(g_ref[...], y_ref[...])
    o_ref[...] = x_ref[...] * (y + u_ref[...] * b_ref[...])


def _spectral_conv(fmat, gmat, hspec, h_col, bias, u_arr, u_col, gate_arr, gate_col):
    B, L = u_arr.shape[0], u_arr.shape[1]
    C = D_BRANCH
    tk = min(L, 256)
    y = pl.pallas_call(
        functools.partial(_dfta_kernel, tk=tk),
        grid=(B, L // tk),
        in_specs=[pl.BlockSpec((2 * tk, L), lambda b, i: (i, 0)),
                  pl.BlockSpec((None, L, C), lambda b, i: (b, 0, u_col)),
                  pl.BlockSpec((2 * tk, C), lambda b, i: (i, h_col))],
        out_specs=pl.BlockSpec((None, 2 * tk, C), lambda b, i: (b, i, 0)),
        out_shape=jax.ShapeDtypeStruct((B, 2 * L, C), BF16),
        scratch_shapes=[pltpu.VMEM((L, C), BF16)],
        compiler_params=_cparams(("arbitrary", "arbitrary")),
        name="hyena_dft_fwd",
    )(fmat, u_arr, hspec)
    tr = min(L, 256)
    return pl.pallas_call(
        _dftb_kernel,
        grid=(B, L // tr),
        in_specs=[pl.BlockSpec((tr, 2 * L), lambda b, i: (i, 0)),
                  pl.BlockSpec((None, 2 * L, C), lambda b, i: (b, 0, 0)),
                  pl.BlockSpec((None, tr, C), lambda b, i: (b, i, u_col)),
                  pl.BlockSpec((None, tr, C), lambda b, i: (b, i, gate_col)),
                  pl.BlockSpec((1, C), lambda b, i: (0, 0))],
        out_specs=pl.BlockSpec((None, tr, C), lambda b, i: (b, i, 0)),
        out_shape=jax.ShapeDtypeStruct((B, L, C), F32),
        compiler_params=_cparams(("arbitrary", "arbitrary")),
        name="hyena_dft_inv",
    )(gmat, y, u_arr, gate_arr, bias.reshape(1, C))


def _hgrn_kernel(q_ref, ff_ref, fb_ref, v_ref, g_ref, lb_ref, nrm_ref, s0_ref, y_ref, sf_ref,
                 of_scr, ob_scr, st_scr, *, L, layer, depth):
    C = HG_CHUNK
    W = HG_DK
    nc = L // C
    nh = q_ref.shape[1] // W
    ridx = _iota((C, W), 0)
    o_scrs = (of_scr, ob_scr)
    f_refs = (ff_ref, fb_ref)

    def lower_bound(d, cols):
        rows = [lb_ref[d, l:l + 1, cols] for l in range(depth)]
        m = rows[0]
        for r in rows[1:]:
            m = jnp.maximum(m, r)
        es = [jnp.exp(r - m) for r in rows]
        tot = es[0]
        for e in es[1:]:
            tot = tot + e
        acc = es[0] / tot
        for e in es[1:layer + 1]:
            acc = acc + e / tot
        return acc - es[0] / tot

    lbs = [[lower_bound(d, slice(hh * W, (hh + 1) * W)) for hh in range(nh)] for d in (0, 1)]
    for d in (0, 1):
        for hh in range(nh):
            st_scr[d, hh] = s0_ref[d, hh].T

    def step(c, d, hh):
        rev = d == 1
        rows = pl.ds(pl.multiple_of(c * C, C), C)
        cols = slice(hh * W, (hh + 1) * W)
        lb = lbs[d][hh]
        q = _silu(q_ref[rows, cols])
        uf = f_refs[d][rows, cols]
        v = v_ref[rows, cols]
        f = lb + (1.0 - lb) * _sigmoid(uf)
        g = jnp.log(jnp.maximum(f, LOG_FLOOR))
        kin = (1.0 - lb) * _sigmoid(-uf)
        b = _cumsum_rows(g, rev)
        st = st_scr[d, hh]
        o = _bdot_nt(q * jnp.exp(b), st)
        intra = jnp.zeros((C, W), F32)
        for t in range(C):
            mask = (ridx >= t) if rev else (ridx <= t)
            pair = _masked_exp(mask, b[t:t + 1, :] - b)
            a = jnp.sum(pair * (q[t:t + 1, :] * kin), axis=1, keepdims=True)
            row = jnp.sum(a * v, axis=0, keepdims=True)
            intra = jnp.where(ridx == t, row, intra)
        b_end = b[0:1, :] if rev else b[C - 1:C, :]
        st_scr[d, hh] = st * jnp.exp(b_end) + _bdot_tn(v, kin * jnp.exp(b_end - b))
        o_scrs[d][rows, cols] = o + intra

    def body(ci, carry):
        for d in (0, 1):
            for hh in range(nh):
                step((nc - 1 - ci) if d == 1 else ci, d, hh)
        return carry

    lax.fori_loop(0, nc, body, 0)
    for d in (0, 1):
        for hh in range(nh):
            sf_ref[d, hh] = st_scr[d, hh].T

    tr = min(L, 256)

    def fin(i, carry):
        rows = pl.ds(pl.multiple_of(i * tr, tr), tr)
        for hh in range(nh):
            cols = slice(hh * W, (hh + 1) * W)
            o = of_scr[rows, cols] + ob_scr[rows, cols]
            on = o * lax.rsqrt(jnp.mean(o * o, axis=-1, keepdims=True) + EPS) * nrm_ref[...]
            y_ref[rows, cols] = on * _silu(g_ref[rows, cols])
        return carry

    lax.fori_loop(0, L // tr, fin, 0)


def _hgrn_scan(proj3, hg_lb, hg_norm, s0, *, layer):
    B, L = proj3.shape[0], proj3.shape[1]
    depth = hg_lb.shape[1]
    nh = 2
    W = nh * HG_DK
    col = lambda base: (lambda b, h: (b, 0, base // W + h))
    st_spec = pl.BlockSpec((None, 2, nh, HG_DK, HG_DK), lambda b, h: (b, 0, h, 0, 0))
    return pl.pallas_call(
        functools.partial(_hgrn_kernel, L=L, layer=layer, depth=depth),
        grid=(B, HG_HEADS // nh),
        in_specs=[pl.BlockSpec((None, L, W), col(P_HGQ)),
                  pl.BlockSpec((None, L, W), col(P_HGFF)),
                  pl.BlockSpec((None, L, W), col(P_HGFB)),
                  pl.BlockSpec((None, L, W), col(P_HGV)),
                  pl.BlockSpec((None, L, W), col(P_HGG)),
                  pl.BlockSpec((2, depth, W), lambda b, h: (0, 0, h)),
                  pl.BlockSpec((1, HG_DK), lambda b, h: (0, 0)),
                  st_spec],
        out_specs=[pl.BlockSpec((None, L, W), lambda b, h: (b, 0, h)), st_spec],
        out_shape=[jax.ShapeDtypeStruct((B, L, D_BRANCH), F32),
                   jax.ShapeDtypeStruct(s0.shape, F32)],
        scratch_shapes=[pltpu.VMEM((L, W), F32), pltpu.VMEM((L, W), F32),
                        pltpu.VMEM((2, nh, HG_DK, HG_DK), F32)],
        compiler_params=_cparams(("arbitrary", "arbitrary")),
        name="hgrn2_scan",
    )(proj3, proj3, proj3, proj3, proj3, hg_lb, hg_norm.reshape(1, HG_DK), s0)


def _ssd_kernel(x_ref, bm_ref, cm_ref, z_ref, sm_ref, bias_ref, alog_ref, dskip_ref, s0_ref,
                y_ref, sf_ref, of_scr, ob_scr, st_scr, *, L):
    C = SSD_CHUNK
    W = LANES
    nc = L // C
    npairs = x_ref.shape[1] // W
    ngroups = bm_ref.shape[1] // W
    grp = pl.program_id(1)
    lane_lo = _iota((C, W), 1) < SSD_P
    row_lo = _iota((W, SSD_N), 0) < SSD_P
    pick2 = lambda a0, a1: jnp.where(lane_lo, a0, a1)
    o_scrs = (of_scr, ob_scr)
    incls = (_tri_mask(C, False), _tri_mask(C, True))
    for d in (0, 1):
        for pp in range(npairs):
            st_scr[d, pp] = s0_ref[d, pp]

    heads = [(d, pp, hh) for d in (0, 1) for pp in range(npairs) for hh in (0, 1)]
    pairs = [(d, pp) for d in (0, 1) for pp in range(npairs)]

    def body(ci, carry):
        rows_d, bm, cm, dt_all, cum_all, gram = [], {}, {}, [], [], {}
        for d in (0, 1):
            c = (nc - 1 - ci) if d == 1 else ci
            rows = pl.ds(pl.multiple_of(c * C, C), C)
            rows_d.append(rows)
            for g in range(ngroups):
                bm[d, g] = _silu(bm_ref[rows, g * W:(g + 1) * W])
                cm[d, g] = _silu(cm_ref[rows, g * W:(g + 1) * W])
            dt_all.append(_softplus(sm_ref[rows, :] + bias_ref[...]))
            cum_all.append(_cumsum_rows(-jnp.exp(alog_ref[...]) * dt_all[d], d == 1))
        lane_of = lambda u: S_DT + u[0] * SSD_HEADS + 2 * (npairs * grp + u[1]) + u[2]
        grp_of = lambda u: (u[0], u[1] // (npairs // ngroups))
        cumr = {u: _row_pick(cum_all[u[0]], lane_of(u)) for u in heads}
        for key in bm:
            gram[key] = _bdot_nt(cm[key], bm[key])
        dtc = {u: _lane_pick(dt_all[u[0]], lane_of(u)) for u in heads}
        cumc = {u: _lane_pick(cum_all[u[0]], lane_of(u)) for u in heads}
        end = {u: (cumc[u][0:1, :] if u[0] == 1 else cumc[u][C - 1:C, :]) for u in heads}
        xs = {pr: _silu(x_ref[rows_d[pr[0]], slice(pr[1] * W, (pr[1] + 1) * W)]) for pr in pairs}
        st = {pr: st_scr[pr[0], pr[1]] for pr in pairs}
        y_state = {u: _bdot_nt(cm[grp_of(u)] * jnp.exp(cumc[u]), st[u[:2]]) for u in heads}
        scores = {u: gram[grp_of(u)] * _masked_exp(incls[u[0]], cumc[u] - cumr[u]) for u in heads}
        y_intra = {}
        for u in heads:
            head_x = jnp.where(lane_lo if u[2] == 0 else ~lane_lo, xs[u[:2]], 0.0) * dtc[u]
            y_intra[u] = _bdot(scores[u], head_x)
        for pr in pairs:
            u0, u1 = pr + (0,), pr + (1,)
            xdt_e = xs[pr] * pick2(dtc[u0] * jnp.exp(end[u0] - cumc[u0]), dtc[u1] * jnp.exp(end[u1] - cumc[u1]))
            st_scr[pr[0], pr[1]] = (st[pr] * jnp.where(row_lo, jnp.exp(end[u0]), jnp.exp(end[u1]))
                                    + _bdot_tn(xdt_e, bm[grp_of(pr)]))
            o_scrs[pr[0]][rows_d[pr[0]], slice(pr[1] * W, (pr[1] + 1) * W)] = pick2(
                y_intra[u0] + y_state[u0], y_intra[u1] + y_state[u1])
        return carry

    lax.fori_loop(0, nc, body, 0)
    for d in (0, 1):
        for pp in range(npairs):
            sf_ref[d, pp] = st_scr[d, pp]

    tr = min(L, 256)

    def fin(i, carry):
        rows = pl.ds(pl.multiple_of(i * tr, tr), tr)
        y = of_scr[rows, :] + ob_scr[rows, :] + _silu(x_ref[rows, :]) * dskip_ref[...]
        y_ref[rows, :] = y * _silu(z_ref[rows, :])
        return carry

    lax.fori_loop(0, L // tr, fin, 0)


def _ssd_scan(proj3, bias_row, alog_row, ssd_d, s0):
    B, L = proj3.shape[0], proj3.shape[1]
    n_groups = 2
    n_pairs = SSD_HEADS // 2
    W = n_pairs * LANES
    s0p = s0.reshape(B, 2, n_pairs, 2 * SSD_P, SSD_N)
    dskip = jnp.repeat(ssd_d, SSD_P).reshape(1, 1, W)
    col = lambda base: (lambda b, g: (b, 0, base // W + g))
    st_spec = pl.BlockSpec((None, 2, n_pairs, LANES, SSD_N), lambda b, g: (b, 0, g, 0, 0))
    once = pl.Buffered(1)
    y, sf = pl.pallas_call(
        functools.partial(_ssd_kernel, L=L),
        grid=(B, 1),
        in_specs=[pl.BlockSpec((None, L, W), col(C_SSDX), pipeline_mode=once),
                  pl.BlockSpec((None, L, n_groups * LANES), lambda b, g: (b, 0, C_SSDB // (n_groups * LANES)),
                               pipeline_mode=once),
                  pl.BlockSpec((None, L, n_groups * LANES), lambda b, g: (b, 0, C_SSDC // (n_groups * LANES)),
                               pipeline_mode=once),
                  pl.BlockSpec((None, L, W), col(P_SSDZ), pipeline_mode=once),
                  pl.BlockSpec((None, L, LANES), lambda b, g: (b, 0, P_SMALL // LANES), pipeline_mode=once),
                  pl.BlockSpec((1, LANES), lambda b, g: (0, 0)),
                  pl.BlockSpec((1, LANES), lambda b, g: (0, 0)),
                  pl.BlockSpec((None, 1, W), lambda b, g: (g, 0, 0)),
                  st_spec],
        out_specs=[pl.BlockSpec((None, L, W), lambda b, g: (b, 0, g)), st_spec],
        out_shape=[jax.ShapeDtypeStruct((B, L, D_BRANCH), F32),
                   jax.ShapeDtypeStruct(s0p.shape, F32)],
        scratch_shapes=[pltpu.VMEM((L, W), F32), pltpu.VMEM((L, W), F32),
                        pltpu.VMEM((2, n_pairs, LANES, SSD_N), F32)],
        compiler_params=_cparams(("arbitrary", "arbitrary")),
        name="ssd_scan",
    )(proj3, proj3, proj3, proj3, proj3, bias_row, alog_row, dskip, s0p)
    return y, sf.reshape(s0.shape)


def _l2norm(a):
    return a * lax.rsqrt(jnp.sum(a * a, axis=-1, keepdims=True) + EPS)


def _gdn_kernel(q_ref, k_ref, v_ref, g_ref, sm_ref, bias_ref, alog_ref, nrm_ref, s0_ref,
                y_ref, sf_ref, of_scr, ob_scr, st_scr, *, L):
    C = GDN_CHUNK
    W = GDN_DK
    nc = L // C
    nh = q_ref.shape[1] // W
    hblk = pl.program_id(1)
    o_scrs = (of_scr, ob_scr)
    incls = (_tri_mask(C, False), _tri_mask(C, True))
    stricts = (_tri_mask(C, False, strict=True), _tri_mask(C, True, strict=True))
    for d in (0, 1):
        for hh in range(nh):
            st_scr[d, hh] = s0_ref[d, hh]

    units = [(d, hh) for d in (0, 1) for hh in range(nh)]
    eye = (_iota((C, C), 0) == _iota((C, C), 1)).astype(F32)

    def body(ci, carry):
        rows_d, cum_d, beta_d = [], [], []
        for d in (0, 1):
            c = (nc - 1 - ci) if d == 1 else ci
            rows = pl.ds(pl.multiple_of(c * C, C), C)
            raw = sm_ref[rows, :]
            rows_d.append(rows)
            cum_d.append(_cumsum_rows(-jnp.exp(alog_ref[...]) * _softplus(raw + bias_ref[...]), d == 1))
            beta_d.append(_sigmoid(raw))
        q, k, v, gc, grow, beta = {}, {}, {}, {}, {}, {}
        for u in units:
            d, hh = u
            cols = slice(hh * W, (hh + 1) * W)
            h = nh * hblk + hh
            q[u] = _l2norm(_silu(q_ref[rows_d[d], cols])) * (GDN_DK ** -0.5)
            k[u] = _l2norm(_silu(k_ref[rows_d[d], cols]))
            v[u] = _silu(v_ref[rows_d[d], cols])
            gc[u] = _lane_pick(cum_d[d], S_GA + d * GDN_HEADS + h)
            beta[u] = _lane_pick(beta_d[d], S_GB + d * GDN_HEADS + h)
        for u in units:
            grow[u] = _row_pick(cum_d[u[0]], S_GA + u[0] * GDN_HEADS + nh * hblk + u[1])
        decay = {u: _masked_exp(incls[u[0]], gc[u] - grow[u]) for u in units}
        kb = {u: k[u] * beta[u] for u in units}
        m = {u: -jnp.where(stricts[u[0]], _bdot_nt(kb[u], k[u]) * decay[u], 0.0) for u in units}
        aqk = {u: _bdot_nt(q[u], k[u]) * decay[u] for u in units}
        p = {u: eye + m[u] for u in units}
        m = {u: _dot3(m[u], m[u]) for u in units}
        sh = 4
        while sh < C:
            both = {u: _dot3(jnp.concatenate([m[u], p[u]], axis=0), m[u]) for u in units}
            p = {u: p[u] + both[u][C:2 * C] for u in units}
            m = {u: both[u][0:C] for u in units}
            sh *= 2
        p = {u: p[u] + _dot3(p[u], m[u]) for u in units}
        rhs = {u: jnp.concatenate([v[u] * beta[u], kb[u] * jnp.exp(gc[u])], axis=1) for u in units}
        uw = {u: _dot3(p[u], rhs[u]) for u in units}
        uu = {u: uw[u][:, 0:W] for u in units}
        ww = {u: uw[u][:, W:2 * W] for u in units}
        st = {u: st_scr[u[0], u[1]] for u in units}
        v_new = {u: uu[u] - _bdot(ww[u], st[u]) for u in units}
        o = {u: _bdot(q[u] * jnp.exp(gc[u]), st[u]) + _bdot(aqk[u], v_new[u]) for u in units}
        for u in units:
            d, hh = u
            g_end = gc[u][0:1, :] if d == 1 else gc[u][C - 1:C, :]
            st_scr[d, hh] = st[u] * jnp.exp(g_end) + _bdot_tn(k[u] * jnp.exp(g_end - gc[u]), v_new[u])
            o_scrs[d][rows_d[d], slice(hh * W, (hh + 1) * W)] = o[u]
        return carry

    lax.fori_loop(0, nc, body, 0)
    for d in (0, 1):
        for hh in range(nh):
            sf_ref[d, hh] = st_scr[d, hh]

    tr = min(L, 256)

    def fin(i, carry):
        rows = pl.ds(pl.multiple_of(i * tr, tr), tr)
        for hh in range(nh):
            cols = slice(hh * W, (hh + 1) * W)
            o = of_scr[rows, cols] + ob_scr[rows, cols]
            on = o * lax.rsqrt(jnp.mean(o * o, axis=-1, keepdims=True) + EPS) * nrm_ref[...]
            y_ref[rows, cols] = on * _silu(g_ref[rows, cols])
        return carry

    lax.fori_loop(0, L // tr, fin, 0)


def _gdn_scan(proj3, bias_row, alog_row, gdn_norm, s0):
    B, L = proj3.shape[0], proj3.shape[1]
    nh = GDN_HEADS
    W = nh * GDN_DK
    col = lambda base: (lambda b, h: (b, 0, base // W + h))
    st_spec = pl.BlockSpec((None, 2, nh, GDN_DK, GDN_DK), lambda b, h: (b, 0, h, 0, 0))
    once = pl.Buffered(1)
    return pl.pallas_call(
        functools.partial(_gdn_kernel, L=L),
        grid=(B, GDN_HEADS // nh),
        in_specs=[pl.BlockSpec((None, L, W), col(C_GQ), pipeline_mode=once),
                  pl.BlockSpec((None, L, W), col(C_GK), pipeline_mode=once),
                  pl.BlockSpec((None, L, W), col(C_GV), pipeline_mode=once),
                  pl.BlockSpec((None, L, W), col(P_GDNG), pipeline_mode=once),
                  pl.BlockSpec((None, L, LANES), lambda b, h: (b, 0, P_SMALL // LANES), pipeline_mode=once),
                  pl.BlockSpec((1, LANES), lambda b, h: (0, 0)),
                  pl.BlockSpec((1, LANES), lambda b, h: (0, 0)),
                  pl.BlockSpec((1, GDN_DK), lambda b, h: (0, 0)),
                  st_spec],
        out_specs=[pl.BlockSpec((None, L, W), lambda b, h: (b, 0, h)), st_spec],
        out_shape=[jax.ShapeDtypeStruct((B, L, D_BRANCH), F32),
                   jax.ShapeDtypeStruct(s0.shape, F32)],
        scratch_shapes=[pltpu.VMEM((L, W), F32), pltpu.VMEM((L, W), F32),
                        pltpu.VMEM((2, nh, GDN_DK, GDN_DK), F32)],
        compiler_params=_cparams(("arbitrary", "arbitrary")),
        name="gdn_scan",
    )(proj3, proj3, proj3, proj3, proj3, bias_row, alog_row, gdn_norm.reshape(1, GDN_DK), s0)


def _merge_kernel(ya_ref, yb_ref, yc_ref, yd_ref, mg_ref, x_ref, mod_ref, sn_ref, n2_ref, wb_ref, wo_ref,
                  xo_ref, h_ref):
    yc = yc_ref[...]
    half = D_BRANCH // 2
    parts = []
    for gidx in range(2):
        seg = yc[:, gidx * half:(gidx + 1) * half]
        parts.append(seg * lax.rsqrt(jnp.mean(seg * seg, axis=-1, keepdims=True) + EPS)
                     * sn_ref[:, gidx * half:(gidx + 1) * half])
    branches = (ya_ref[...], yb_ref[...], None, yd_ref[...])
    mixed = None
    for n in range(N_BRANCH):
        gate = _sigmoid(mg_ref[:, n * D_MODEL:(n + 1) * D_MODEL])
        if n == 2:
            lifted = (_dot(parts[0].astype(BF16), wb_ref[n, 0:half, :])
                      + _dot(parts[1].astype(BF16), wb_ref[n, half:D_BRANCH, :]))
        else:
            lifted = _dot(branches[n].astype(BF16), wb_ref[n])
        mixed = gate * lifted if mixed is None else mixed + gate * lifted
    m = mod_ref[...]
    x = x_ref[...] + m[2:3] * _dot(mixed.astype(BF16), wo_ref[...])
    xo_ref[...] = x
    xn = x * lax.rsqrt(jnp.mean(x * x, axis=-1, keepdims=True) + EPS) * n2_ref[...]
    h_ref[...] = xn * (1.0 + m[4:5]) + m[3:4]


def _merge(ya, yb, yc, yd, proj, x, mod, ssd_norm, norm2_g, wb_bf16, wo_bf16, *, tokens_per_mod):
    T = x.shape[0]
    tm = min(512, tokens_per_mod)
    tiles_per_mod = tokens_per_mod // tm
    yspec = pl.BlockSpec((tm, D_BRANCH), lambda i: (i, 0))
    xspec = pl.BlockSpec((tm, D_MODEL), lambda i: (i, 0))
    return pl.pallas_call(
        _merge_kernel,
        grid=(T // tm,),
        in_specs=[yspec, yspec, yspec, yspec,
                  pl.BlockSpec((tm, N_BRANCH * D_MODEL), lambda i: (i, P_MERGE // (N_BRANCH * D_MODEL))),
                  xspec,
                  pl.BlockSpec((None, 6, D_MODEL), lambda i: (i // tiles_per_mod, 0, 0)),
                  pl.BlockSpec((1, D_BRANCH), lambda i: (0, 0)),
                  pl.BlockSpec((1, D_MODEL), lambda i: (0, 0)),
                  pl.BlockSpec((N_BRANCH, D_BRANCH, D_MODEL), lambda i: (0, 0, 0)),
                  pl.BlockSpec((D_MODEL, D_MODEL), lambda i: (0, 0))],
        out_specs=[xspec, xspec],
        out_shape=[jax.ShapeDtypeStruct((T, D_MODEL), F32)] * 2,
        compiler_params=_cparams(("arbitrary",)),
        name="merge_outproj",
    )(ya, yb, yc, yd, proj, x, mod, ssd_norm.reshape(1, D_BRANCH), norm2_g.reshape(1, D_MODEL), wb_bf16, wo_bf16)


def _router_kernel(h_ref, rw_ref, rb_ref, wt_ref, lpos_ref, tlen_ref, tstart_ref, tcar_ref, carry):
    i = pl.program_id(0)
    tm = h_ref.shape[0]
    E = N_EXPERTS
    gsz = E // N_EXPERT_GROUPS
    neg_inf = -jnp.inf

    @pl.when(i == 0)
    def _():
        carry[...] = jnp.zeros_like(carry)

    scores = _sigmoid(_dot_nt(rw_ref[...], h_ref[...], precision=HIGHEST))
    biased = scores + rb_ref[...]
    eidx = _iota((E, tm), 0)
    ridx = _iota((gsz, tm), 0)

    slabs = [biased[g * gsz:(g + 1) * gsz, :] for g in range(N_EXPERT_GROUPS)]
    gs = []
    for v in slabs:
        m1 = jnp.max(v, axis=0, keepdims=True)
        i1 = jnp.min(jnp.where(v == m1, ridx, gsz), axis=0, keepdims=True)
        m2 = jnp.max(jnp.where(ridx == i1, neg_inf, v), axis=0, keepdims=True)
        gs.append(m1 + m2)
    masked = []
    for g in range(N_EXPERT_GROUPS):
        rank = jnp.zeros((1, tm), I32)
        for g2 in range(N_EXPERT_GROUPS):
            if g2 == g:
                continue
            ahead = (gs[g2] >= gs[g]) if g2 < g else (gs[g2] > gs[g])
            rank = rank + ahead.astype(I32)
        masked.append(jnp.where(rank < TOPK_GROUPS, slabs[g], MASK_NEG))
    cur = jnp.concatenate(masked, axis=0)

    krow = _iota((TOP_K, tm), 0)
    onehot = jnp.zeros((E, tm), F32)
    picks = []
    wsum = jnp.zeros((1, tm), F32)
    for kk in range(TOP_K):
        m = jnp.max(cur, axis=0, keepdims=True)
        ik = jnp.min(jnp.where(cur == m, eidx, E), axis=0, keepdims=True)
        hit = eidx == ik
        wk = jnp.sum(jnp.where(hit, scores, 0.0), axis=0, keepdims=True)
        cur = jnp.where(hit, neg_inf, cur)
        onehot = onehot + hit.astype(F32)
        picks.append((ik, wk))
        wsum = wsum + wk
    wt_out = jnp.zeros((TOP_K, tm), F32)
    for kk, (ik, wk) in enumerate(picks):
        wt_out = jnp.where(krow == kk, wk / wsum * ROUTED_SCALE, wt_out)

    earlier = (_iota((tm, tm), 0) < _iota((tm, tm), 1)).astype(BF16)
    before = _dot(onehot.astype(BF16), earlier)
    pad = lambda c: (c.astype(I32) + (RUN_ROWS - 1)) // RUN_ROWS * RUN_ROWS
    run_len = pad(jnp.sum(onehot, axis=1, keepdims=True) + jnp.zeros((E, LANES), F32)).astype(F32)
    run_start = _cumsum_rows(run_len, False) - run_len
    local = before + run_start[:, 0:1]
    lpos_out = jnp.zeros((TOP_K, tm), I32)
    for kk, (ik, wk) in enumerate(picks):
        lp = jnp.sum(jnp.where(eidx == ik, local, 0.0), axis=0, keepdims=True)
        lpos_out = jnp.where(krow == kk, lp.astype(I32), lpos_out)
    wt_ref[...] = wt_out
    lpos_ref[...] = lpos_out

    onehot_l = jnp.concatenate([onehot, jnp.zeros((LANES - E, tm), F32)], axis=0).astype(BF16)
    len_row = pad(_dot_nt(jnp.ones((8, tm), BF16), onehot_l)).astype(F32)
    lanes_before = (_iota((LANES, LANES), 0) < _iota((LANES, LANES), 1)).astype(BF16)
    tlen_ref[...] = len_row.astype(I32)
    tstart_ref[...] = _dot(len_row.astype(BF16), lanes_before).astype(I32)
    tcar_ref[...] = carry[...].astype(I32)
    carry[...] = carry[...] + len_row


def _router(h2, router_w, router_b, *, tm):
    T = h2.shape[0]
    nt = T // tm
    ospec = pl.BlockSpec((TOP_K, tm), lambda i: (0, i))
    tspec = pl.BlockSpec((None, 8, LANES), lambda i: (i, 0, 0))
    tshape = jax.ShapeDtypeStruct((nt, 8, LANES), I32)
    wt, lpos, tlen, tstart, tcar = pl.pallas_call(
        _router_kernel,
        grid=(nt,),
        in_specs=[pl.BlockSpec((tm, D_MODEL), lambda i: (i, 0)),
                  pl.BlockSpec((N_EXPERTS, D_MODEL), lambda i: (0, 0)),
                  pl.BlockSpec((N_EXPERTS, 1), lambda i: (0, 0))],
        out_specs=[ospec, ospec, tspec, tspec, tspec],
        out_shape=[jax.ShapeDtypeStruct((TOP_K, T), F32), jax.ShapeDtypeStruct((TOP_K, T), I32),
                   tshape, tshape, tshape],
        scratch_shapes=[pltpu.VMEM((8, LANES), F32)],
        compiler_params=_cparams(("arbitrary",)),
        name="moe_router",
    )(h2, router_w.T, router_b.reshape(N_EXPERTS, 1))
    table = lambda a: a[:, 0, :N_EXPERTS]
    return wt, lpos, table(tlen), table(tstart), table(tcar)


def _dispatch_kernel(len_ref, lst_ref, car_ref, seg_ref, lpos_ref, h_ref, xs_ref, xloc, sem):
    i = pl.program_id(0)
    tm = h_ref.shape[0]
    nloc = xloc.shape[0]
    R = RUN_ROWS
    lpos = lpos_ref[...]
    hb = h_ref[...].astype(BF16)
    rb = 256
    off = _iota((rb, tm), 0).astype(F32).astype(BF16)
    one = jnp.ones((rb, tm), BF16)
    for blk in range(nloc // rb):
        cand = jnp.where(lpos // rb == blk, lpos % rb, -1).astype(F32).astype(BF16)
        owner = jnp.zeros((rb, tm), BF16)
        for kk in range(TOP_K):
            owner = jnp.where(off == cand[kk:kk + 1, :], one, owner)
        rows = _dot(owner, hb)
        bits = lax.bitcast_convert_type(rows, jnp.uint32)
        xloc[blk * rb:(blk + 1) * rb, :] = bits[:, 0:D_XPACK] | (bits[:, D_XPACK:D_MODEL] >> 16)

    def piece(src_row, dst_row):
        return pltpu.make_async_copy(xloc.at[pl.ds(pl.multiple_of(src_row, R), R), :],
                                     xs_ref.at[pl.ds(pl.multiple_of(dst_row, R), R), :], sem)

    def per_expert(e, issued):
        pieces = len_ref[i, e] // R
        ls = lst_ref[i, e]
        gd = seg_ref[e] + car_ref[i, e]

        def issue(j, c):
            piece(ls + j * R, gd + j * R).start()
            return c

        lax.fori_loop(0, pieces, issue, 0)
        return issued + pieces

    total = lax.fori_loop(0, N_EXPERTS, per_expert, 0)

    def wait(j, c):
        piece(0, 0).wait()
        return c

    lax.fori_loop(0, total, wait, 0)


def _dispatch(h2, lpos, tile_len, tile_lstart, tile_car, seg_start, *, tm):
    T = h2.shape[0]
    nt = T // tm
    n_rows = T * TOP_K + N_EXPERTS * nt * RUN_ROWS
    nloc = tm * TOP_K + N_EXPERTS * RUN_ROWS
    return pl.pallas_call(
        _dispatch_kernel,
        grid_spec=pltpu.PrefetchScalarGridSpec(
            num_scalar_prefetch=4,
            grid=(nt,),
            in_specs=[pl.BlockSpec((TOP_K, tm), lambda i, *_: (0, i)),
                      pl.BlockSpec((tm, D_MODEL), lambda i, *_: (i, 0))],
            out_specs=pl.BlockSpec(memory_space=pl.ANY),
            scratch_shapes=[pltpu.VMEM((nloc, D_XPACK), jnp.uint32), pltpu.SemaphoreType.DMA(())]),
        out_shape=jax.ShapeDtypeStruct((n_rows, D_XPACK), jnp.uint32),
        compiler_params=_cparams(("arbitrary",)),
        name="moe_dispatch",
    )(tile_len, tile_lstart, tile_car, seg_start, lpos, h2)


def _expert_kernel(blk_ref, exp_ref, lo_ref, hi_ref, first_ref, x_ref, w1_ref, w3_ref, w2_ref, o_ref):
    w = pl.program_id(0)
    packed = x_ref[...]
    x_a = lax.bitcast_convert_type(packed & jnp.uint32(0xFFFF0000), F32).astype(BF16)
    x_b = lax.bitcast_convert_type(packed << 16, F32).astype(BF16)
    w1 = w1_ref[...].astype(BF16)
    w3 = w3_ref[...].astype(BF16)
    a = _dot(x_a, w1[0:D_XPACK]) + _dot(x_b, w1[D_XPACK:D_MODEL])
    b = _dot(x_a, w3[0:D_XPACK]) + _dot(x_b, w3[D_XPACK:D_MODEL])
    y = _dot((_silu(a) * b).astype(BF16), w2_ref[...].astype(BF16))
    r = _iota(y.shape, 0)
    y = jnp.where((r >= lo_ref[w]) & (r < hi_ref[w]), y, 0.0)

    @pl.when(first_ref[w] == 1)
    def _():
        o_ref[...] = y

    @pl.when(first_ref[w] == 0)
    def _():
        o_ref[...] = o_ref[...] + y


def _experts(xs, work, w1, w3, w2, *, layer, bm):
    A = xs.shape[0]
    n_work = work[0].shape[0]
    xmap = lambda w, blk, ex, lo, hi, first: (blk[w], 0)
    wmap = lambda w, blk, ex, lo, hi, first: (layer, ex[w], 0, 0)
    return pl.pallas_call(
        _expert_kernel,
        grid_spec=pltpu.PrefetchScalarGridSpec(
            num_scalar_prefetch=5,
            grid=(n_work,),
            in_specs=[pl.BlockSpec((bm, D_XPACK), xmap),
                      pl.BlockSpec((None, None, D_MODEL, D_EXPERT), wmap),
                      pl.BlockSpec((None, None, D_MODEL, D_EXPERT), wmap),
                      pl.BlockSpec((None, None, D_EXPERT, D_MODEL), wmap)],
            out_specs=pl.BlockSpec((bm, D_MODEL), xmap)),
        out_shape=jax.ShapeDtypeStruct((A, D_MODEL), F32),
        compiler_params=_cparams(("arbitrary",)),
        name="moe_experts",
    )(*work, xs, w1, w3, w2)


def _combine_kernel(len_ref, lst_ref, car_ref, seg_ref, lpos_ref, wt_ref, ys_ref, h_ref, x_ref, mod_ref,
                    s1_ref, s3_ref, s2_ref, fg_ref, o_ref, yloc, sem, *, final_norm):
    i = pl.program_id(0)
    tm = h_ref.shape[0]
    nloc = yloc.shape[0]
    R = RUN_ROWS

    @pl.when(i == 0)
    def _():
        yloc[...] = jnp.zeros_like(yloc)

    def piece(src_row, dst_row):
        return pltpu.make_async_copy(ys_ref.at[pl.ds(pl.multiple_of(src_row, R), R), :],
                                     yloc.at[pl.ds(pl.multiple_of(dst_row, R), R), :], sem)

    def per_expert(e, issued):
        pieces = len_ref[i, e] // R
        ls = lst_ref[i, e]
        gd = seg_ref[e] + car_ref[i, e]

        def issue(j, c):
            piece(gd + j * R, ls + j * R).start()
            return c

        lax.fori_loop(0, pieces, issue, 0)
        return issued + pieces

    total = lax.fori_loop(0, N_EXPERTS, per_expert, 0)

    hb = h_ref[...].astype(BF16)
    mid = _silu(_dot(hb, s1_ref[...])) * _dot(hb, s3_ref[...])
    y = _dot(mid.astype(BF16), s2_ref[...])

    def wait(j, c):
        piece(0, 0).wait()
        return c

    lax.fori_loop(0, total, wait, 0)

    lpos = lpos_ref[...]
    wt = wt_ref[...]
    used_rows = lst_ref[i, N_EXPERTS - 1] + len_ref[i, N_EXPERTS - 1]
    rb = 256

    def block_sum(blk):
        r = _iota((rb, tm), 0) + blk * rb
        pw = jnp.zeros((rb, tm), F32)
        for kk in range(TOP_K):
            pw = jnp.where(r == lpos[kk:kk + 1, :], wt[kk:kk + 1, :], pw)
        ysc = yloc[blk * rb:(blk + 1) * rb, :] * jnp.sum(pw, axis=1, keepdims=True)
        hi = ysc.astype(BF16)
        lo = (ysc - hi.astype(F32)).astype(BF16)
        owner = (pw != 0.0).astype(BF16)
        return _dot_tn(owner, hi) + _dot_tn(owner, lo)

    for blk in range(nloc // rb):
        if (blk + 1) * rb <= tm * TOP_K:
            y = y + block_sum(blk)
        else:
            y = y + lax.cond(used_rows > blk * rb, functools.partial(block_sum, blk),
                             lambda: jnp.zeros((tm, D_MODEL), F32))
    x = x_ref[...] + mod_ref[5:6, :] * y
    if final_norm:
        x = x * lax.rsqrt(jnp.mean(x * x, axis=-1, keepdims=True) + EPS) * fg_ref[...]
    o_ref[...] = x


def _combine(ys, lpos, wt, tile_len, tile_lstart, tile_car, seg_start, h2, x, mod, s1, s3, s2, final_g,
             *, tm, tokens_per_mod, final_norm):
    T = x.shape[0]
    tiles_per_mod = tokens_per_mod // tm
    nloc = tm * TOP_K + N_EXPERTS * RUN_ROWS
    xspec = pl.BlockSpec((tm, D_MODEL), lambda i, *_: (i, 0))
    kspec = pl.BlockSpec((TOP_K, tm), lambda i, *_: (0, i))
    const = lambda i, *_: (0, 0)
    return pl.pallas_call(
        functools.partial(_combine_kernel, final_norm=final_norm),
        grid_spec=pltpu.PrefetchScalarGridSpec(
            num_scalar_prefetch=4,
            grid=(T // tm,),
            in_specs=[kspec, kspec,
                      pl.BlockSpec(memory_space=pl.ANY),
                      xspec, xspec,
                      pl.BlockSpec((None, 6, D_MODEL), lambda i, *_: (i // tiles_per_mod, 0, 0)),
                      pl.BlockSpec((D_MODEL, D_SHARED), const),
                      pl.BlockSpec((D_MODEL, D_SHARED), const),
                      pl.BlockSpec((D_SHARED, D_MODEL), const),
                      pl.BlockSpec((1, D_MODEL), const)],
            out_specs=xspec,
            scratch_shapes=[pltpu.VMEM((nloc, D_MODEL), F32), pltpu.SemaphoreType.DMA(())]),
        out_shape=jax.ShapeDtypeStruct((T, D_MODEL), F32),
        compiler_params=_cparams(("arbitrary",)),
        name="moe_combine",
    )(tile_len, tile_lstart, tile_car, seg_start, lpos, wt, ys, h2, x, mod, s1, s3, s2, final_g.reshape(1, D_MODEL))


def _work_list(counts, starts, n_rows, bm):
    E = N_EXPERTS
    n_blocks = n_rows // bm
    n_work = n_blocks + E - 1
    ends = starts + counts
    first_blk = starts // bm
    last_blk = jnp.where(counts > 0, (ends - 1) // bm, first_blk)
    n_items = jnp.where(counts > 0, last_blk - first_blk + 1, 0)
    item_end = jnp.cumsum(n_items)
    item_start = item_end - n_items
    w = jnp.arange(n_work, dtype=I32)
    ex = jnp.minimum(jnp.sum((item_end[None, :] <= w[:, None]).astype(I32), axis=1), E - 1)
    valid = w < item_end[-1]
    blk = first_blk[ex] + (w - item_start[ex])
    blk = jnp.where(valid, blk, blk[jnp.maximum(item_end[-1] - 1, 0)]).astype(I32)
    lo = jnp.clip(starts[ex] - blk * bm, 0, bm)
    hi = jnp.clip(ends[ex] - blk * bm, 0, bm)
    lo = jnp.where(valid, lo, 0).astype(I32)
    hi = jnp.where(valid, hi, 0).astype(I32)
    ex = jnp.where(valid, ex, ex[jnp.maximum(item_end[-1] - 1, 0)])
    prev_blk = jnp.concatenate([jnp.full((1,), -1, I32), blk[:-1]])
    first = (blk != prev_blk).astype(I32)
    return blk, ex.astype(I32), lo, hi, first


def _moe(h2, x, mod, p, final_g, *, layer, tokens_per_mod, final_norm):
    T = x.shape[0]
    tm = 256
    wt, lpos, tile_len, tile_lstart, tile_car = _router(h2, p["router_w"], p["router_b"], tm=tm)
    seg_len = tile_car[-1] + tile_len[-1]
    seg_start = jnp.cumsum(seg_len) - seg_len
    xs = _dispatch(h2, lpos, tile_len, tile_lstart, tile_car, seg_start, tm=tm)
    bm = 512
    work = _work_list(seg_len, seg_start, xs.shape[0], bm)
    ys = _experts(xs, work, p["exp_w1"], p["exp_w3"], p["exp_w2"], layer=layer, bm=bm)
    return _combine(ys, lpos, wt, tile_len, tile_lstart, tile_car, seg_start, h2, x, mod,
                    p["sh_w1"], p["sh_w3"], p["sh_w2"], final_g,
                    tm=tm, tokens_per_mod=tokens_per_mod, final_norm=final_norm)


def _reorder_w_in(w_in):
    plain = D_CONV
    pieces = [w_in[..., :plain],
              w_in[..., plain + 3616:plain + 3616 + 4096],
              w_in[..., plain:plain + 3072],
              w_in[..., plain + 3088:plain + 3600],
              w_in[..., plain + 3072:plain + 3088],
              w_in[..., plain + 3600:plain + 3616]]
    out = jnp.concatenate(pieces, axis=-1)
    pad = D_PROJ - out.shape[-1]
    return jnp.pad(out, ((0, 0), (0, 0), (0, pad))).astype(BF16)


def _small_rows(ssd_vals, gdn_vals):
    row = jnp.zeros((LANES,), F32)
    row = row.at[S_DT:S_DT + 2 * SSD_HEADS].set(ssd_vals.reshape(-1))
    row = row.at[S_GA:S_GA + 2 * GDN_HEADS].set(gdn_vals.reshape(-1))
    return row.reshape(1, LANES)


def _layer_pass(x, mod, lp, hy, st_hg, st_ssd, st_gdn, final_g, *, B, L, tokens_per_mod, seg, layer, final_norm):
    T = B * L
    proj = _in_projection(x, mod, lp["norm1_g"], lp["w_in"], lp["conv_w"], lp["conv_b"],
                          layer=layer, tokens_per_mod=tokens_per_mod, seg=seg)
    proj3 = proj.reshape(B, L, D_PROJ)
    fmat, gmat, hspec = hy
    cb = lambda c: c // D_BRANCH
    z = _spectral_conv(fmat, gmat, hspec, 0, lp["hy_bias"][0], proj3, cb(C_HYV), proj3, cb(C_HYX1))
    ya = _spectral_conv(fmat, gmat, hspec, 1, lp["hy_bias"][1], z, 0, proj3, cb(C_HYX2))
    yb, s_hg = _hgrn_scan(proj3, lp["hg_lb"], lp["hg_norm"], st_hg, layer=layer)
    yc, s_ssd = _ssd_scan(proj3, lp["bias_row"], lp["alog_row"], lp["ssd_d"], st_ssd)
    yd, s_gdn = _gdn_scan(proj3, lp["bias_row"], lp["alog_row"], lp["gdn_norm"], st_gdn)
    flat = lambda a: a.reshape(T, D_BRANCH)
    x, h2 = _merge(flat(ya), flat(yb), flat(yc), flat(yd), proj, x, mod, lp["ssd_norm"], lp["norm2_g"],
                   lp["w_branch"], lp["w_out"], tokens_per_mod=tokens_per_mod)
    x = _moe(h2, x, mod, lp, final_g, layer=layer, tokens_per_mod=tokens_per_mod, final_norm=final_norm)
    return x, s_hg, s_ssd, s_gdn


def kernel(x_prompt, x_sample, state_hgrn, state_ssd, state_gdn, c, c_ctx, norm1_g, norm2_g, ada_w, ada_b, w_in, conv_w, conv_b, hy_w1, hy_b1, hy_w2, hy_b2, hy_w3, hy_bias, hg_lb, hg_norm, ssd_a_log, ssd_dt_bias, ssd_d, ssd_norm, gdn_a_log, gdn_dt_bias, gdn_norm, w_branch, w_out, router_w, router_bias, exp_w1, exp_w3, exp_w2, sh_w1, sh_w3, sh_w2, final_g):
    depth = w_in.shape[0]
    bp, lp_len = x_prompt.shape[0], x_prompt.shape[1]
    bs, ls_len = x_sample.shape[0], x_sample.shape[1]
    D = D_MODEL

    w_in_r = _reorder_w_in(w_in)
    hy_w1p = jnp.pad(hy_w1, ((0, 0), (0, LANES - hy_w1.shape[1]), (0, 0)))
    layers = []
    for l in range(depth):
        layers.append(dict(
            norm1_g=norm1_g[l], norm2_g=norm2_g[l], w_in=w_in_r, conv_w=conv_w[l], conv_b=conv_b[l],
            hy_bias=hy_bias[l], hg_lb=hg_lb, hg_norm=hg_norm[l],
            bias_row=_small_rows(ssd_dt_bias[l], gdn_dt_bias[l]),
            alog_row=_small_rows(ssd_a_log[l], gdn_a_log[l]),
            ssd_d=ssd_d[l], ssd_norm=ssd_norm[l], gdn_norm=gdn_norm[l],
            w_branch=w_branch[l].astype(BF16), w_out=w_out[l].astype(BF16),
            router_w=router_w[l], router_b=router_bias[l],
            exp_w1=exp_w1, exp_w3=exp_w3, exp_w2=exp_w2,
            sh_w1=sh_w1[l].astype(BF16), sh_w3=sh_w3[l].astype(BF16), sh_w2=sh_w2[l].astype(BF16)))

    def hyena_setup(L):
        fmat, fs = _dft_matrices(L)
        gmat = fs.T
        specs = []
        for l in range(depth):
            filt = _hyena_filters(L, hy_w1p[l], hy_b1[l], hy_w2[l], hy_b2[l], hy_w3[l])
            specs.append(_filter_spectrum(fmat, filt))
        return fmat, gmat, specs

    cond = jnp.concatenate([c_ctx.reshape(1, D), c], axis=0)
    rows = cond.shape[0]
    rows8 = (rows + 7) // 8 * 8
    cond8 = jnp.pad(cond, ((0, rows8 - rows), (0, 0)))
    mods = [_modulation(cond8, ada_w, ada_b[l], layer=l).reshape(rows8, 6, D) for l in range(depth)]

    fmat, gmat, specs = hyena_setup(lp_len)
    x = x_prompt.reshape(bp * lp_len, D)
    z_hg = jnp.zeros((bp, 2, HG_HEADS, HG_DK, HG_DK), F32)
    z_ssd = jnp.zeros((bp, 2, SSD_HEADS, SSD_P, SSD_N), F32)
    z_gdn = jnp.zeros((bp, 2, GDN_HEADS, GDN_DK, GDN_DK), F32)
    hg_states, ssd_states, gdn_states = [], [], []
    for l in range(depth):
        x, s_hg, s_ssd, s_gdn = _layer_pass(
            x, mods[l][0:1], layers[l], (fmat, gmat, specs[l]), z_hg, z_ssd, z_gdn, final_g,
            B=bp, L=lp_len, tokens_per_mod=bp * lp_len, seg=lp_len, layer=l, final_norm=(l == depth - 1))
        hg_states.append(s_hg)
        ssd_states.append(s_ssd)
        gdn_states.append(s_gdn)
    y_prompt = x.reshape(bp, lp_len, D)
    new_hg = jnp.stack(hg_states, axis=1)
    new_ssd = jnp.stack(ssd_states, axis=1)
    new_gdn = jnp.stack(gdn_states, axis=1)

    fmat, gmat, specs = hyena_setup(ls_len)
    x = x_sample.reshape(bs * ls_len, D)
    for l in range(depth):
        x, _, _, _ = _layer_pass(
            x, mods[l][1:1 + bs], layers[l], (fmat, gmat, specs[l]),
            state_hgrn[:, l], state_ssd[:, l], state_gdn[:, l], final_g,
            B=bs, L=ls_len, tokens_per_mod=ls_len, seg=GRID_W, layer=l, final_norm=(l == depth - 1))
    y_sample = x.reshape(bs, ls_len, D)
    return (y_prompt, y_sample, new_hg, new_ssd, new_gdn)
```

```python
import functools
import math

import jax
import jax.numpy as jnp
from jax import lax
from jax.experimental import pallas as pl
from jax.experimental.pallas import tpu as pltpu

F32 = jnp.float32
BF16 = jnp.bfloat16
I32 = jnp.int32
HIGHEST = lax.Precision.HIGHEST

D_MODEL = 1024
GRID_W = 64
EPS = 1e-6
LOG_FLOOR = 1e-30
MASK_NEG = -1e30
N_BRANCH = 4
D_BRANCH = 512
HY_POS_FREQS = 16
HY_FILTER_HIDDEN = 64
HY_FAST_DECAY = 0.3
HY_SLOW_DECAY = 1.5
HY_DECAY_TARGET = 1e-2
HG_HEADS = 4
HG_DK = 128
HG_CHUNK = 16
SSD_HEADS = 8
SSD_P = 64
SSD_N = 128
SSD_CHUNK = 64
GDN_HEADS = 4
GDN_DK = 128
GDN_CHUNK = 64
N_EXPERTS = 64
TOP_K = 8
N_EXPERT_GROUPS = 8
TOPK_GROUPS = 4
D_EXPERT = 256
D_SHARED = 256
ROUTED_SCALE = 2.5

LANES = 128
D_XPACK = D_MODEL // 2
RUN_ROWS = 8

D_CONV = 4096
C_HYV, C_HYX1, C_HYX2 = 0, 512, 1024
C_SSDX, C_SSDB, C_SSDC = 1536, 2048, 2304
C_GQ, C_GK, C_GV = 2560, 3072, 3584
P_MERGE = 4096
P_HGQ, P_HGFF, P_HGFB, P_HGV, P_HGG = 8192, 8704, 9216, 9728, 10240
P_SSDZ, P_GDNG, P_SMALL = 10752, 11264, 11776
D_PROJ = 12288
S_DT, S_GA, S_GB = 0, 16, 24

VMEM_LIMIT = 56 * 1024 * 1024


def _cparams(sem):
    return pltpu.CompilerParams(dimension_semantics=sem, vmem_limit_bytes=VMEM_LIMIT)


def _sigmoid(x):
    return 1.0 / (1.0 + jnp.exp(-x))


def _silu(x):
    return x * _sigmoid(x)


def _softplus(x):
    return jnp.maximum(x, 0.0) + jnp.log(1.0 + jnp.exp(-jnp.abs(x)))


def _dot(a, b, precision=None):
    return jnp.dot(a, b, preferred_element_type=F32, precision=precision)


def _dot_nt(a, b, precision=None):
    return lax.dot_general(a, b, (((1,), (1,)), ((), ())), preferred_element_type=F32, precision=precision)


def _dot_tn(a, b, precision=None):
    return lax.dot_general(a, b, (((0,), (0,)), ((), ())), preferred_element_type=F32, precision=precision)


def _bdot(a, b):
    return _dot(a.astype(BF16), b.astype(BF16))


def _bdot_nt(a, b):
    return _dot_nt(a.astype(BF16), b.astype(BF16))


def _bdot_tn(a, b):
    return _dot_tn(a.astype(BF16), b.astype(BF16))


def _iota(shape, dim):
    return lax.broadcasted_iota(I32, shape, dim)


def _cumsum_rows(g, reverse):
    n = g.shape[0]
    row = _iota(g.shape, 0)
    sh = 1
    while sh < n:
        if reverse:
            g = g + jnp.where(row < n - sh, pltpu.roll(g, n - sh, 0), 0.0)
        else:
            g = g + jnp.where(row >= sh, pltpu.roll(g, sh, 0), 0.0)
        sh *= 2
    return g


def _lane_pick(a, j):
    return jnp.sum(jnp.where(_iota(a.shape, 1) == j, a, 0.0), axis=1, keepdims=True)


def _split3(a):
    a1 = a.astype(BF16)
    r1 = a - a1.astype(F32)
    a2 = r1.astype(BF16)
    a3 = (r1 - a2.astype(F32)).astype(BF16)
    return a1, a2, a3


def _dot3(a, b):
    a1, a2, _ = _split3(a)
    b1, b2, _ = _split3(b)
    return _dot(a1, b1) + (_dot(a1, b2) + _dot(a2, b1))


def _row_pick(a, j):
    sel = (_iota((8, a.shape[1]), 1) == j).astype(BF16)
    a1, a2, a3 = _split3(a)
    return (_dot_nt(sel, a1) + (_dot_nt(sel, a2) + _dot_nt(sel, a3)))[0:1, :]


def _tri_mask(n, reverse, strict=False):
    t = _iota((n, n), 0)
    s = _iota((n, n), 1)
    if reverse:
        return (s > t) if strict else (s >= t)
    return (s < t) if strict else (s <= t)


def _masked_exp(mask, diff):
    return jnp.where(mask, jnp.exp(jnp.where(mask, diff, 0.0)), 0.0)


def _mod_kernel(c_ref, w_ref, b_ref, o_ref):
    o_ref[...] = _dot(_silu(c_ref[...]), w_ref[...], precision=HIGHEST) + b_ref[...]


def _modulation(cond8, ada_w, ada_b, *, layer):
    rows = cond8.shape[0]
    tn = 1536
    n = ada_w.shape[2]
    return pl.pallas_call(
        _mod_kernel,
        grid=(n // tn,),
        in_specs=[pl.BlockSpec((rows, D_MODEL), lambda j: (0, 0)),
                  pl.BlockSpec((None, D_MODEL, tn), lambda j: (layer, 0, j)),
                  pl.BlockSpec((1, tn), lambda j: (0, j))],
        out_specs=pl.BlockSpec((rows, tn), lambda j: (0, j)),
        out_shape=jax.ShapeDtypeStruct((rows, n), F32),
        compiler_params=_cparams(("arbitrary",)),
        name="adaln_mod",
    )(cond8, ada_w, ada_b.reshape(1, n))


def _inproj_kernel(x_ref, mod_ref, g_ref, w_ref, cw_ref, cb_ref, o_ref, h_scr, *, seg, n_conv_tiles):
    j = pl.program_id(1)

    @pl.when(j == 0)
    def _():
        x = x_ref[...]
        xn = x * lax.rsqrt(jnp.mean(x * x, axis=-1, keepdims=True) + EPS) * g_ref[...]
        m = mod_ref[...]
        h_scr[...] = (xn * (1.0 + m[1:2]) + m[0:1]).astype(BF16)

    y = _dot(h_scr[...], w_ref[...])

    @pl.when(j < n_conv_tiles)
    def _():
        tm = y.shape[0]
        pos = _iota(y.shape, 0) & (seg - 1)
        prev = jnp.where(pos == 0, 0.0, pltpu.roll(y, 1, 0))
        nxt = jnp.where(pos == seg - 1, 0.0, pltpu.roll(y, tm - 1, 0))
        cw = cw_ref[...]
        o_ref[...] = cb_ref[...] + prev * cw[0:1] + y * cw[1:2] + nxt * cw[2:3]

    @pl.when(j >= n_conv_tiles)
    def _():
        o_ref[...] = y


def _in_projection(x, mod, norm_g, w_bf16, conv_w, conv_b, *, layer, tokens_per_mod, seg):
    T = x.shape[0]
    tm = min(2048, tokens_per_mod)
    tn = 1024
    n_conv_tiles = D_CONV // tn
    tiles_per_mod = tokens_per_mod // tm
    kern = functools.partial(_inproj_kernel, seg=seg, n_conv_tiles=n_conv_tiles)
    cmap = lambda i, j: (0, jnp.minimum(j, n_conv_tiles - 1))
    return pl.pallas_call(
        kern,
        grid=(T // tm, D_PROJ // tn),
        in_specs=[pl.BlockSpec((tm, D_MODEL), lambda i, j: (i, 0)),
                  pl.BlockSpec((None, 6, D_MODEL), lambda i, j: (i // tiles_per_mod, 0, 0)),
                  pl.BlockSpec((1, D_MODEL), lambda i, j: (0, 0)),
                  pl.BlockSpec((None, D_MODEL, tn), lambda i, j: (layer, 0, j)),
                  pl.BlockSpec((3, tn), cmap),
                  pl.BlockSpec((1, tn), cmap)],
        out_specs=pl.BlockSpec((tm, tn), lambda i, j: (i, j)),
        out_shape=jax.ShapeDtypeStruct((T, D_PROJ), F32),
        scratch_shapes=[pltpu.VMEM((tm, D_MODEL), BF16)],
        compiler_params=_cparams(("arbitrary", "arbitrary")),
        name="in_proj",
    )(x, mod, norm_g.reshape(1, D_MODEL), w_bf16, conv_w, conv_b.reshape(1, D_CONV))


def _hyfilt_kernel(w1_ref, b1_ref, w2_ref, b2_ref, w3_ref, o_ref, *, L):
    i = pl.program_id(1)
    tl = o_ref.shape[0]
    t = (_iota((tl, LANES), 0) + i * tl).astype(F32) / L
    lane = _iota((tl, LANES), 1)
    band = jnp.where(lane <= HY_POS_FREQS, lane, lane - HY_POS_FREQS).astype(F32)
    ang = 2.0 * math.pi * t * band
    feats = jnp.where(lane == 0, t,
                      jnp.where(lane <= HY_POS_FREQS, jnp.sin(ang),
                                jnp.where(lane <= 2 * HY_POS_FREQS, jnp.cos(ang), 0.0)))
    hdn = jnp.sin(_dot(feats, w1_ref[...], precision=HIGHEST) + b1_ref[...])
    hdn = jnp.sin(_dot(hdn, w2_ref[...], precision=HIGHEST) + b2_ref[...])
    filt = _dot(hdn, w3_ref[...], precision=HIGHEST)
    max_decay = math.log(HY_DECAY_TARGET) / HY_FAST_DECAY
    min_decay = math.log(HY_DECAY_TARGET) / HY_SLOW_DECAY
    n = filt.shape[1]
    ch = (_iota((tl, n), 1) & (D_BRANCH - 1)).astype(F32)
    delta = min_decay + ch * ((max_decay - min_decay) / (D_BRANCH - 1))
    tt = (_iota((tl, n), 0) + i * tl).astype(F32) / L
    o_ref[...] = filt * jnp.exp(-tt * jnp.abs(delta))


def _hyena_filters(L, w1p, b1, w2, b2, w3):
    tl = min(L, 256)
    n = 2 * D_BRANCH
    return pl.pallas_call(
        functools.partial(_hyfilt_kernel, L=L),
        grid=(2, L // tl),
        in_specs=[pl.BlockSpec((LANES, HY_FILTER_HIDDEN), lambda d, i: (0, 0)),
                  pl.BlockSpec((1, HY_FILTER_HIDDEN), lambda d, i: (0, 0)),
                  pl.BlockSpec((HY_FILTER_HIDDEN, HY_FILTER_HIDDEN), lambda d, i: (0, 0)),
                  pl.BlockSpec((1, HY_FILTER_HIDDEN), lambda d, i: (0, 0)),
                  pl.BlockSpec((HY_FILTER_HIDDEN, n), lambda d, i: (0, d))],
        out_specs=pl.BlockSpec((None, tl, n), lambda d, i: (d, i, 0)),
        out_shape=jax.ShapeDtypeStruct((2, L, n), F32),
        compiler_params=_cparams(("arbitrary", "arbitrary")),
        name="hyena_filters",
    )(w1p, b1.reshape(1, -1), w2, b2.reshape(1, -1), w3)


def _dftgen_kernel(f_ref, fs_ref, *, L, tk):
    i = pl.program_id(0)
    N = 2 * L
    k = _iota((tk, LANES), 0) + i * tk
    lane = _iota((tk, LANES), 1)
    w = 2.0 * math.pi / N
    a0 = ((k * lane) & (N - 1)).astype(F32) * w
    c0, s0 = jnp.cos(a0), jnp.sin(a0)
    a1 = ((k * lane * LANES) & (N - 1)).astype(F32) * w
    c1, s1 = jnp.cos(a1), jnp.sin(a1)
    alt = jnp.where((lane & 1) == 0, 1.0, -1.0)
    coef = jnp.where(k == 0, 1.0 / N, 2.0 / N)
    for n1 in range(L // LANES):
        c1b = c1[:, n1:n1 + 1]
        s1b = s1[:, n1:n1 + 1]
        cosb = c1b * c0 - s1b * s0
        sinb = jnp.where(k == 0, alt, -(s1b * c0 + c1b * s0))
        cols = slice(n1 * LANES, (n1 + 1) * LANES)
        f_ref[0:tk, cols] = cosb.astype(BF16)
        f_ref[tk:2 * tk, cols] = sinb.astype(BF16)
        fs_ref[0:tk, cols] = (coef * cosb).astype(BF16)
        fs_ref[tk:2 * tk, cols] = (coef * sinb).astype(BF16)


def _dft_matrices(L):
    tk = min(L, 256)
    spec = pl.BlockSpec((2 * tk, L), lambda i: (i, 0))
    return pl.pallas_call(
        functools.partial(_dftgen_kernel, L=L, tk=tk),
        grid=(L // tk,),
        in_specs=[],
        out_specs=[spec, spec],
        out_shape=[jax.ShapeDtypeStruct((2 * L, L), BF16)] * 2,
        compiler_params=_cparams(("arbitrary",)),
        name="dft_matrices",
    )()


def _dfth_kernel(f_ref, h_ref, o_ref, hp_scr, *, tk):
    i = pl.program_id(1)
    n = o_ref.shape[1]

    @pl.when(i == 0)
    def _():
        hc = h_ref[0]
        ha = jnp.where(_iota(hc.shape, 0) == 0, 0.0, h_ref[1])
        hp_scr[:, 0:n] = (hc + ha).astype(BF16)
        hp_scr[:, n:2 * n] = (hc - ha).astype(BF16)

    u = _dot(f_ref[...], hp_scr[...])
    r = _iota((2 * tk, n), 0)
    from_sum = (r < tk) | ((r == tk) & (i == 0))
    o_ref[...] = jnp.where(from_sum, u[:, 0:n], u[:, n:2 * n])


def _filter_spectrum(fmat, filt):
    L = fmat.shape[1]
    tk = min(L, 256)
    C = filt.shape[2]
    tc = 256
    return pl.pallas_call(
        functools.partial(_dfth_kernel, tk=tk),
        grid=(C // tc, L // tk),
        in_specs=[pl.BlockSpec((2 * tk, L), lambda c, i: (i, 0)),
                  pl.BlockSpec((2, L, tc), lambda c, i: (0, 0, c))],
        out_specs=pl.BlockSpec((2 * tk, tc), lambda c, i: (i, c)),
        out_shape=jax.ShapeDtypeStruct((2 * L, C), F32),
        scratch_shapes=[pltpu.VMEM((L, 2 * tc), BF16)],
        compiler_params=_cparams(("arbitrary", "arbitrary")),
        name="filter_spectrum",
    )(fmat, filt)


def _dfta_kernel(f_ref, u_ref, h_ref, y_ref, u_scr, *, tk):
    i = pl.program_id(1)

    @pl.when(i == 0)
    def _():
        u_scr[...] = u_ref[...].astype(BF16)

    uf = _dot(f_ref[...], u_scr[...])
    ur, ui = uf[0:tk], uf[tk:2 * tk]
    hr, hi = h_ref[0:tk, :], h_ref[tk:2 * tk, :]
    dc = (_iota(ur.shape, 0) == 0) & (i == 0)
    y_ref[0:tk, :] = jnp.where(dc, ur * hr, ur * hr - ui * hi).astype(BF16)
    y_ref[tk:2 * tk, :] = jnp.where(dc, ui * hi, ur * hi + ui * hr).astype(BF16)


def _dftb_kernel(g_ref, y_ref, u_ref, x_ref, b_ref, o_ref):
    y = _dot(g_ref[...], y_ref[...])
    o_ref[...] = x_ref[...] * (y + u_ref[...] * b_ref[...])


def _spectral_conv(fmat, gmat, hspec, h_col, bias, u_arr, u_col, gate_arr, gate_col):
    B, L = u_arr.shape[0], u_arr.shape[1]
    C = D_BRANCH
    tk = min(L, 256)
    y = pl.pallas_call(
        functools.partial(_dfta_kernel, tk=tk),
        grid=(B, L // tk),
        in_specs=[pl.BlockSpec((2 * tk, L), lambda b, i: (i, 0)),
                  pl.BlockSpec((None, L, C), lambda b, i: (b, 0, u_col)),
                  pl.BlockSpec((2 * tk, C), lambda b, i: (i, h_col))],
        out_specs=pl.BlockSpec((None, 2 * tk, C), lambda b, i: (b, i, 0)),
        out_shape=jax.ShapeDtypeStruct((B, 2 * L, C), BF16),
        scratch_shapes=[pltpu.VMEM((L, C), BF16)],
        compiler_params=_cparams(("arbitrary", "arbitrary")),
        name="hyena_dft_fwd",
    )(fmat, u_arr, hspec)
    tr = min(L, 256)
    return pl.pallas_call(
        _dftb_kernel,
        grid=(B, L // tr),
        in_specs=[pl.BlockSpec((tr, 2 * L), lambda b, i: (i, 0)),
                  pl.BlockSpec((None, 2 * L, C), lambda b, i: (b, 0, 0)),
                  pl.BlockSpec((None, tr, C), lambda b, i: (b, i, u_col)),
                  pl.BlockSpec((None, tr, C), lambda b, i: (b, i, gate_col)),
                  pl.BlockSpec((1, C), lambda b, i: (0, 0))],
        out_specs=pl.BlockSpec((None, tr, C), lambda b, i: (b, i, 0)),
        out_shape=jax.ShapeDtypeStruct((B, L, C), F32),
        compiler_params=_cparams(("arbitrary", "arbitrary")),
        name="hyena_dft_inv",
    )(gmat, y, u_arr, gate_arr, bias.reshape(1, C))


def _hgrn_kernel(q_ref, ff_ref, fb_ref, v_ref, g_ref, lb_ref, nrm_ref, s0_ref, y_ref, sf_ref,
                 of_scr, ob_scr, st_scr, *, L, layer, depth):
    C = HG_CHUNK
    W = HG_DK
    nc = L // C
    nh = q_ref.shape[1] // W
    ridx = _iota((C, W), 0)
    o_scrs = (of_scr, ob_scr)
    f_refs = (ff_ref, fb_ref)

    def lower_bound(d, cols):
        rows = [lb_ref[d, l:l + 1, cols] for l in range(depth)]
        m = rows[0]
        for r in rows[1:]:
            m = jnp.maximum(m, r)
        es = [jnp.exp(r - m) for r in rows]
        tot = es[0]
        for e in es[1:]:
            tot = tot + e
        acc = es[0] / tot
        for e in es[1:layer + 1]:
            acc = acc + e / tot
        return acc - es[0] / tot

    lbs = [[lower_bound(d, slice(hh * W, (hh + 1) * W)) for hh in range(nh)] for d in (0, 1)]
    for d in (0, 1):
        for hh in range(nh):
            st_scr[d, hh] = s0_ref[d, hh].T

    def step(c, d, hh):
        rev = d == 1
        rows = pl.ds(pl.multiple_of(c * C, C), C)
        cols = slice(hh * W, (hh + 1) * W)
        lb = lbs[d][hh]
        q = _silu(q_ref[rows, cols])
        uf = f_refs[d][rows, cols]
        v = v_ref[rows, cols]
        f = lb + (1.0 - lb) * _sigmoid(uf)
        g = jnp.log(jnp.maximum(f, LOG_FLOOR))
        kin = (1.0 - lb) * _sigmoid(-uf)
        b = _cumsum_rows(g, rev)
        st = st_scr[d, hh]
        o = _bdot_nt(q * jnp.exp(b), st)
        intra = jnp.zeros((C, W), F32)
        for t in range(C):
            mask = (ridx >= t) if rev else (ridx <= t)
            pair = _masked_exp(mask, b[t:t + 1, :] - b)
            a = jnp.sum(pair * (q[t:t + 1, :] * kin), axis=1, keepdims=True)
            row = jnp.sum(a * v, axis=0, keepdims=True)
            intra = jnp.where(ridx == t, row, intra)
        b_end = b[0:1, :] if rev else b[C - 1:C, :]
        st_scr[d, hh] = st * jnp.exp(b_end) + _bdot_tn(v, kin * jnp.exp(b_end - b))
        o_scrs[d][rows, cols] = o + intra

    def body(ci, carry):
        for d in (0, 1):
            for hh in range(nh):
                step((nc - 1 - ci) if d == 1 else ci, d, hh)
        return carry

    lax.fori_loop(0, nc, body, 0)
    for d in (0, 1):
        for hh in range(nh):
            sf_ref[d, hh] = st_scr[d, hh].T

    tr = min(L, 256)

    def fin(i, carry):
        rows = pl.ds(pl.multiple_of(i * tr, tr), tr)
        for hh in range(nh):
            cols = slice(hh * W, (hh + 1) * W)
            o = of_scr[rows, cols] + ob_scr[rows, cols]
            on = o * lax.rsqrt(jnp.mean(o * o, axis=-1, keepdims=True) + EPS) * nrm_ref[...]
            y_ref[rows, cols] = on * _silu(g_ref[rows, cols])
        return carry

    lax.fori_loop(0, L // tr, fin, 0)


def _hgrn_scan(proj3, hg_lb, hg_norm, s0, *, layer):
    B, L = proj3.shape[0], proj3.shape[1]
    depth = hg_lb.shape[1]
    nh = 2
    W = nh * HG_DK
    col = lambda base: (lambda b, h: (b, 0, base // W + h))
    st_spec = pl.BlockSpec((None, 2, nh, HG_DK, HG_DK), lambda b, h: (b, 0, h, 0, 0))
    return pl.pallas_call(
        functools.partial(_hgrn_kernel, L=L, layer=layer, depth=depth),
        grid=(B, HG_HEADS // nh),
        in_specs=[pl.BlockSpec((None, L, W), col(P_HGQ)),
                  pl.BlockSpec((None, L, W), col(P_HGFF)),
                  pl.BlockSpec((None, L, W), col(P_HGFB)),
                  pl.BlockSpec((None, L, W), col(P_HGV)),
                  pl.BlockSpec((None, L, W), col(P_HGG)),
                  pl.BlockSpec((2, depth, W), lambda b, h: (0, 0, h)),
                  pl.BlockSpec((1, HG_DK), lambda b, h: (0, 0)),
                  st_spec],
        out_specs=[pl.BlockSpec((None, L, W), lambda b, h: (b, 0, h)), st_spec],
        out_shape=[jax.ShapeDtypeStruct((B, L, D_BRANCH), F32),
                   jax.ShapeDtypeStruct(s0.shape, F32)],
        scratch_shapes=[pltpu.VMEM((L, W), F32), pltpu.VMEM((L, W), F32),
                        pltpu.VMEM((2, nh, HG_DK, HG_DK), F32)],
        compiler_params=_cparams(("arbitrary", "arbitrary")),
        name="hgrn2_scan",
    )(proj3, proj3, proj3, proj3, proj3, hg_lb, hg_norm.reshape(1, HG_DK), s0)


def _ssd_kernel(x_ref, bm_ref, cm_ref, z_ref, sm_ref, bias_ref, alog_ref, dskip_ref, s0_ref,
                y_ref, sf_ref, of_scr, ob_scr, st_scr, *, L):
    C = SSD_CHUNK
    W = LANES
    nc = L // C
    npairs = x_ref.shape[1] // W
    grp = pl.program_id(1)
    lane_lo = _iota((C, W), 1) < SSD_P
    row_lo = _iota((W, SSD_N), 0) < SSD_P
    pick2 = lambda a0, a1: jnp.where(lane_lo, a0, a1)
    o_scrs = (of_scr, ob_scr)
    incls = (_tri_mask(C, False), _tri_mask(C, True))
    for d in (0, 1):
        for pp in range(npairs):
            st_scr[d, pp] = s0_ref[d, pp]

    heads = [(d, pp, hh) for d in (0, 1) for pp in range(npairs) for hh in (0, 1)]
    pairs = [(d, pp) for d in (0, 1) for pp in range(npairs)]

    def body(ci, carry):
        rows_d, bm, cm, dt_all, cum_all, gram = [], [], [], [], [], []
        for d in (0, 1):
            c = (nc - 1 - ci) if d == 1 else ci
            rows = pl.ds(pl.multiple_of(c * C, C), C)
            rows_d.append(rows)
            bm.append(_silu(bm_ref[rows, :]))
            cm.append(_silu(cm_ref[rows, :]))
            dt_all.append(_softplus(sm_ref[rows, :] + bias_ref[...]))
            cum_all.append(_cumsum_rows(-jnp.exp(alog_ref[...]) * dt_all[d], d == 1))
        lane_of = lambda u: S_DT + u[0] * SSD_HEADS + 2 * (npairs * grp + u[1]) + u[2]
        cumr = {u: _row_pick(cum_all[u[0]], lane_of(u)) for u in heads}
        for d in (0, 1):
            gram.append(_bdot_nt(cm[d], bm[d]))
        dtc = {u: _lane_pick(dt_all[u[0]], lane_of(u)) for u in heads}
        cumc = {u: _lane_pick(cum_all[u[0]], lane_of(u)) for u in heads}
        end = {u: (cumc[u][0:1, :] if u[0] == 1 else cumc[u][C - 1:C, :]) for u in heads}
        xs = {pr: _silu(x_ref[rows_d[pr[0]], slice(pr[1] * W, (pr[1] + 1) * W)]) for pr in pairs}
        st = {pr: st_scr[pr[0], pr[1]] for pr in pairs}
        y_state = {u: _bdot_nt(cm[u[0]] * jnp.exp(cumc[u]), st[u[:2]]) for u in heads}
        scores = {u: gram[u[0]] * _masked_exp(incls[u[0]], cumc[u] - cumr[u]) for u in heads}
        y_intra = {}
        for u in heads:
            head_x = jnp.where(lane_lo if u[2] == 0 else ~lane_lo, xs[u[:2]], 0.0) * dtc[u]
            y_intra[u] = _bdot(scores[u], head_x)
        for pr in pairs:
            u0, u1 = pr + (0,), pr + (1,)
            xdt_e = xs[pr] * pick2(dtc[u0] * jnp.exp(end[u0] - cumc[u0]), dtc[u1] * jnp.exp(end[u1] - cumc[u1]))
            st_scr[pr[0], pr[1]] = (st[pr] * jnp.where(row_lo, jnp.exp(end[u0]), jnp.exp(end[u1]))
                                    + _bdot_tn(xdt_e, bm[pr[0]]))
            o_scrs[pr[0]][rows_d[pr[0]], slice(pr[1] * W, (pr[1] + 1) * W)] = pick2(
                y_intra[u0] + y_state[u0], y_intra[u1] + y_state[u1])
        return carry

    lax.fori_loop(0, nc, body, 0)
    for d in (0, 1):
        for pp in range(npairs):
            sf_ref[d, pp] = st_scr[d, pp]

    tr = min(L, 256)

    def fin(i, carry):
        rows = pl.ds(pl.multiple_of(i * tr, tr), tr)
        y = of_scr[rows, :] + ob_scr[rows, :] + _silu(x_ref[rows, :]) * dskip_ref[...]
        y_ref[rows, :] = y * _silu(z_ref[rows, :])
        return carry

    lax.fori_loop(0, L // tr, fin, 0)


def _ssd_scan(proj3, bias_row, alog_row, ssd_d, s0):
    B, L = proj3.shape[0], proj3.shape[1]
    n_groups = 2
    n_pairs = SSD_HEADS // 2
    ppg = n_pairs // n_groups
    W = ppg * LANES
    s0p = s0.reshape(B, 2, n_pairs, 2 * SSD_P, SSD_N)
    dskip = jnp.repeat(ssd_d, SSD_P).reshape(n_groups, 1, W)
    col = lambda base: (lambda b, g: (b, 0, base // W + g))
    grp = lambda base: (lambda b, g: (b, 0, base // LANES + g))
    st_spec = pl.BlockSpec((None, 2, ppg, LANES, SSD_N), lambda b, g: (b, 0, g, 0, 0))
    y, sf = pl.pallas_call(
        functools.partial(_ssd_kernel, L=L),
        grid=(B, n_groups),
        in_specs=[pl.BlockSpec((None, L, W), col(C_SSDX)),
                  pl.BlockSpec((None, L, LANES), grp(C_SSDB)),
                  pl.BlockSpec((None, L, LANES), grp(C_SSDC)),
                  pl.BlockSpec((None, L, W), col(P_SSDZ)),
                  pl.BlockSpec((None, L, LANES), lambda b, g: (b, 0, P_SMALL // LANES)),
                  pl.BlockSpec((1, LANES), lambda b, g: (0, 0)),
                  pl.BlockSpec((1, LANES), lambda b, g: (0, 0)),
                  pl.BlockSpec((None, 1, W), lambda b, g: (g, 0, 0)),
                  st_spec],
        out_specs=[pl.BlockSpec((None, L, W), lambda b, g: (b, 0, g)), st_spec],
        out_shape=[jax.ShapeDtypeStruct((B, L, D_BRANCH), F32),
                   jax.ShapeDtypeStruct(s0p.shape, F32)],
        scratch_shapes=[pltpu.VMEM((L, W), F32), pltpu.VMEM((L, W), F32),
                        pltpu.VMEM((2, ppg, LANES, SSD_N), F32)],
        compiler_params=_cparams(("arbitrary", "arbitrary")),
        name="ssd_scan",
    )(proj3, proj3, proj3, proj3, proj3, bias_row, alog_row, dskip, s0p)
    return y, sf.reshape(s0.shape)


def _l2norm(a):
    return a * lax.rsqrt(jnp.sum(a * a, axis=-1, keepdims=True) + EPS)


def _gdn_kernel(q_ref, k_ref, v_ref, g_ref, sm_ref, bias_ref, alog_ref, nrm_ref, s0_ref,
                y_ref, sf_ref, of_scr, ob_scr, st_scr, *, L):
    C = GDN_CHUNK
    W = GDN_DK
    nc = L // C
    nh = q_ref.shape[1] // W
    hblk = pl.program_id(1)
    o_scrs = (of_scr, ob_scr)
    incls = (_tri_mask(C, False), _tri_mask(C, True))
    stricts = (_tri_mask(C, False, strict=True), _tri_mask(C, True, strict=True))
    for d in (0, 1):
        for hh in range(nh):
            st_scr[d, hh] = s0_ref[d, hh]

    units = [(d, hh) for d in (0, 1) for hh in range(nh)]
    eye = (_iota((C, C), 0) == _iota((C, C), 1)).astype(F32)

    def body(ci, carry):
        rows_d, cum_d, beta_d = [], [], []
        for d in (0, 1):
            c = (nc - 1 - ci) if d == 1 else ci
            rows = pl.ds(pl.multiple_of(c * C, C), C)
            raw = sm_ref[rows, :]
            rows_d.append(rows)
            cum_d.append(_cumsum_rows(-jnp.exp(alog_ref[...]) * _softplus(raw + bias_ref[...]), d == 1))
            beta_d.append(_sigmoid(raw))
        q, k, v, gc, grow, beta = {}, {}, {}, {}, {}, {}
        for u in units:
            d, hh = u
            cols = slice(hh * W, (hh + 1) * W)
            h = nh * hblk + hh
            q[u] = _l2norm(_silu(q_ref[rows_d[d], cols])) * (GDN_DK ** -0.5)
            k[u] = _l2norm(_silu(k_ref[rows_d[d], cols]))
            v[u] = _silu(v_ref[rows_d[d], cols])
            gc[u] = _lane_pick(cum_d[d], S_GA + d * GDN_HEADS + h)
            beta[u] = _lane_pick(beta_d[d], S_GB + d * GDN_HEADS + h)
        for u in units:
            grow[u] = _row_pick(cum_d[u[0]], S_GA + u[0] * GDN_HEADS + nh * hblk + u[1])
        decay = {u: _masked_exp(incls[u[0]], gc[u] - grow[u]) for u in units}
        kb = {u: k[u] * beta[u] for u in units}
        m = {u: -jnp.where(stricts[u[0]], _bdot_nt(kb[u], k[u]) * decay[u], 0.0) for u in units}
        aqk = {u: _bdot_nt(q[u], k[u]) * decay[u] for u in units}
        p = {u: eye + m[u] for u in units}
        m = {u: _dot3(m[u], m[u]) for u in units}
        sh = 4
        while sh < C:
            both = {u: _dot3(jnp.concatenate([m[u], p[u]], axis=0), m[u]) for u in units}
            p = {u: p[u] + both[u][C:2 * C] for u in units}
            m = {u: both[u][0:C] for u in units}
            sh *= 2
        p = {u: p[u] + _dot3(p[u], m[u]) for u in units}
        rhs = {u: jnp.concatenate([v[u] * beta[u], kb[u] * jnp.exp(gc[u])], axis=1) for u in units}
        uw = {u: _dot3(p[u], rhs[u]) for u in units}
        uu = {u: uw[u][:, 0:W] for u in units}
        ww = {u: uw[u][:, W:2 * W] for u in units}
        st = {u: st_scr[u[0], u[1]] for u in units}
        v_new = {u: uu[u] - _bdot(ww[u], st[u]) for u in units}
        o = {u: _bdot(q[u] * jnp.exp(gc[u]), st[u]) + _bdot(aqk[u], v_new[u]) for u in units}
        for u in units:
            d, hh = u
            g_end = gc[u][0:1, :] if d == 1 else gc[u][C - 1:C, :]
            st_scr[d, hh] = st[u] * jnp.exp(g_end) + _bdot_tn(k[u] * jnp.exp(g_end - gc[u]), v_new[u])
            o_scrs[d][rows_d[d], slice(hh * W, (hh + 1) * W)] = o[u]
        return carry

    lax.fori_loop(0, nc, body, 0)
    for d in (0, 1):
        for hh in range(nh):
            sf_ref[d, hh] = st_scr[d, hh]

    tr = min(L, 256)

    def fin(i, carry):
        rows = pl.ds(pl.multiple_of(i * tr, tr), tr)
        for hh in range(nh):
            cols = slice(hh * W, (hh + 1) * W)
            o = of_scr[rows, cols] + ob_scr[rows, cols]
            on = o * lax.rsqrt(jnp.mean(o * o, axis=-1, keepdims=True) + EPS) * nrm_ref[...]
            y_ref[rows, cols] = on * _silu(g_ref[rows, cols])
        return carry

    lax.fori_loop(0, L // tr, fin, 0)


def _gdn_scan(proj3, bias_row, alog_row, gdn_norm, s0):
    B, L = proj3.shape[0], proj3.shape[1]
    nh = GDN_HEADS
    W = nh * GDN_DK
    col = lambda base: (lambda b, h: (b, 0, base // W + h))
    st_spec = pl.BlockSpec((None, 2, nh, GDN_DK, GDN_DK), lambda b, h: (b, 0, h, 0, 0))
    return pl.pallas_call(
        functools.partial(_gdn_kernel, L=L),
        grid=(B, GDN_HEADS // nh),
        in_specs=[pl.BlockSpec((None, L, W), col(C_GQ)),
                  pl.BlockSpec((None, L, W), col(C_GK)),
                  pl.BlockSpec((None, L, W), col(C_GV)),
                  pl.BlockSpec((None, L, W), col(P_GDNG)),
                  pl.BlockSpec((None, L, LANES), lambda b, h: (b, 0, P_SMALL // LANES)),
                  pl.BlockSpec((1, LANES), lambda b, h: (0, 0)),
                  pl.BlockSpec((1, LANES), lambda b, h: (0, 0)),
                  pl.BlockSpec((1, GDN_DK), lambda b, h: (0, 0)),
                  st_spec],
        out_specs=[pl.BlockSpec((None, L, W), lambda b, h: (b, 0, h)), st_spec],
        out_shape=[jax.ShapeDtypeStruct((B, L, D_BRANCH), F32),
                   jax.ShapeDtypeStruct(s0.shape, F32)],
        scratch_shapes=[pltpu.VMEM((L, W), F32), pltpu.VMEM((L, W), F32),
                        pltpu.VMEM((2, nh, GDN_DK, GDN_DK), F32)],
        compiler_params=_cparams(("arbitrary", "arbitrary")),
        name="gdn_scan",
    )(proj3, proj3, proj3, proj3, proj3, bias_row, alog_row, gdn_norm.reshape(1, GDN_DK), s0)


def _merge_kernel(ya_ref, yb_ref, yc_ref, yd_ref, mg_ref, x_ref, mod_ref, sn_ref, n2_ref, wb_ref, wo_ref,
                  xo_ref, h_ref):
    yc = yc_ref[...]
    half = D_BRANCH // 2
    parts = []
    for gidx in range(2):
        seg = yc[:, gidx * half:(gidx + 1) * half]
        parts.append(seg * lax.rsqrt(jnp.mean(seg * seg, axis=-1, keepdims=True) + EPS)
                     * sn_ref[:, gidx * half:(gidx + 1) * half])
    branches = (ya_ref[...], yb_ref[...], None, yd_ref[...])
    mixed = None
    for n in range(N_BRANCH):
        gate = _sigmoid(mg_ref[:, n * D_MODEL:(n + 1) * D_MODEL])
        if n == 2:
            lifted = (_dot(parts[0].astype(BF16), wb_ref[n, 0:half, :])
                      + _dot(parts[1].astype(BF16), wb_ref[n, half:D_BRANCH, :]))
        else:
            lifted = _dot(branches[n].astype(BF16), wb_ref[n])
        mixed = gate * lifted if mixed is None else mixed + gate * lifted
    m = mod_ref[...]
    x = x_ref[...] + m[2:3] * _dot(mixed.astype(BF16), wo_ref[...])
    xo_ref[...] = x
    xn = x * lax.rsqrt(jnp.mean(x * x, axis=-1, keepdims=True) + EPS) * n2_ref[...]
    h_ref[...] = xn * (1.0 + m[4:5]) + m[3:4]


def _merge(ya, yb, yc, yd, proj, x, mod, ssd_norm, norm2_g, wb_bf16, wo_bf16, *, tokens_per_mod):
    T = x.shape[0]
    tm = min(512, tokens_per_mod)
    tiles_per_mod = tokens_per_mod // tm
    yspec = pl.BlockSpec((tm, D_BRANCH), lambda i: (i, 0))
    xspec = pl.BlockSpec((tm, D_MODEL), lambda i: (i, 0))
    return pl.pallas_call(
        _merge_kernel,
        grid=(T // tm,),
        in_specs=[yspec, yspec, yspec, yspec,
                  pl.BlockSpec((tm, N_BRANCH * D_MODEL), lambda i: (i, P_MERGE // (N_BRANCH * D_MODEL))),
                  xspec,
                  pl.BlockSpec((None, 6, D_MODEL), lambda i: (i // tiles_per_mod, 0, 0)),
                  pl.BlockSpec((1, D_BRANCH), lambda i: (0, 0)),
                  pl.BlockSpec((1, D_MODEL), lambda i: (0, 0)),
                  pl.BlockSpec((N_BRANCH, D_BRANCH, D_MODEL), lambda i: (0, 0, 0)),
                  pl.BlockSpec((D_MODEL, D_MODEL), lambda i: (0, 0))],
        out_specs=[xspec, xspec],
        out_shape=[jax.ShapeDtypeStruct((T, D_MODEL), F32)] * 2,
        compiler_params=_cparams(("arbitrary",)),
        name="merge_outproj",
    )(ya, yb, yc, yd, proj, x, mod, ssd_norm.reshape(1, D_BRANCH), norm2_g.reshape(1, D_MODEL), wb_bf16, wo_bf16)


def _router_kernel(h_ref, rw_ref, rb_ref, wt_ref, lpos_ref, tlen_ref, tstart_ref, tcar_ref, carry):
    i = pl.program_id(0)
    tm = h_ref.shape[0]
    E = N_EXPERTS
    gsz = E // N_EXPERT_GROUPS
    neg_inf = -jnp.inf

    @pl.when(i == 0)
    def _():
        carry[...] = jnp.zeros_like(carry)

    scores = _sigmoid(_dot_nt(rw_ref[...], h_ref[...], precision=HIGHEST))
    biased = scores + rb_ref[...]
    eidx = _iota((E, tm), 0)
    ridx = _iota((gsz, tm), 0)

    slabs = [biased[g * gsz:(g + 1) * gsz, :] for g in range(N_EXPERT_GROUPS)]
    gs = []
    for v in slabs:
        m1 = jnp.max(v, axis=0, keepdims=True)
        i1 = jnp.min(jnp.where(v == m1, ridx, gsz), axis=0, keepdims=True)
        m2 = jnp.max(jnp.where(ridx == i1, neg_inf, v), axis=0, keepdims=True)
        gs.append(m1 + m2)
    masked = []
    for g in range(N_EXPERT_GROUPS):
        rank = jnp.zeros((1, tm), I32)
        for g2 in range(N_EXPERT_GROUPS):
            if g2 == g:
                continue
            ahead = (gs[g2] >= gs[g]) if g2 < g else (gs[g2] > gs[g])
            rank = rank + ahead.astype(I32)
        masked.append(jnp.where(rank < TOPK_GROUPS, slabs[g], MASK_NEG))
    cur = jnp.concatenate(masked, axis=0)

    krow = _iota((TOP_K, tm), 0)
    onehot = jnp.zeros((E, tm), F32)
    picks = []
    wsum = jnp.zeros((1, tm), F32)
    for kk in range(TOP_K):
        m = jnp.max(cur, axis=0, keepdims=True)
        ik = jnp.min(jnp.where(cur == m, eidx, E), axis=0, keepdims=True)
        hit = eidx == ik
        wk = jnp.sum(jnp.where(hit, scores, 0.0), axis=0, keepdims=True)
        cur = jnp.where(hit, neg_inf, cur)
        onehot = onehot + hit.astype(F32)
        picks.append((ik, wk))
        wsum = wsum + wk
    wt_out = jnp.zeros((TOP_K, tm), F32)
    for kk, (ik, wk) in enumerate(picks):
        wt_out = jnp.where(krow == kk, wk / wsum * ROUTED_SCALE, wt_out)

    earlier = (_iota((tm, tm), 0) < _iota((tm, tm), 1)).astype(BF16)
    before = _dot(onehot.astype(BF16), earlier)
    pad = lambda c: (c.astype(I32) + (RUN_ROWS - 1)) // RUN_ROWS * RUN_ROWS
    run_len = pad(jnp.sum(onehot, axis=1, keepdims=True) + jnp.zeros((E, LANES), F32)).astype(F32)
    run_start = _cumsum_rows(run_len, False) - run_len
    local = before + run_start[:, 0:1]
    lpos_out = jnp.zeros((TOP_K, tm), I32)
    for kk, (ik, wk) in enumerate(picks):
        lp = jnp.sum(jnp.where(eidx == ik, local, 0.0), axis=0, keepdims=True)
        lpos_out = jnp.where(krow == kk, lp.astype(I32), lpos_out)
    wt_ref[...] = wt_out
    lpos_ref[...] = lpos_out

    onehot_l = jnp.concatenate([onehot, jnp.zeros((LANES - E, tm), F32)], axis=0).astype(BF16)
    len_row = pad(_dot_nt(jnp.ones((8, tm), BF16), onehot_l)).astype(F32)
    lanes_before = (_iota((LANES, LANES), 0) < _iota((LANES, LANES), 1)).astype(BF16)
    tlen_ref[...] = len_row.astype(I32)
    tstart_ref[...] = _dot(len_row.astype(BF16), lanes_before).astype(I32)
    tcar_ref[...] = carry[...].astype(I32)
    carry[...] = carry[...] + len_row


def _router(h2, router_w, router_b, *, tm):
    T = h2.shape[0]
    nt = T // tm
    ospec = pl.BlockSpec((TOP_K, tm), lambda i: (0, i))
    tspec = pl.BlockSpec((None, 8, LANES), lambda i: (i, 0, 0))
    tshape = jax.ShapeDtypeStruct((nt, 8, LANES), I32)
    wt, lpos, tlen, tstart, tcar = pl.pallas_call(
        _router_kernel,
        grid=(nt,),
        in_specs=[pl.BlockSpec((tm, D_MODEL), lambda i: (i, 0)),
                  pl.BlockSpec((N_EXPERTS, D_MODEL), lambda i: (0, 0)),
                  pl.BlockSpec((N_EXPERTS, 1), lambda i: (0, 0))],
        out_specs=[ospec, ospec, tspec, tspec, tspec],
        out_shape=[jax.ShapeDtypeStruct((TOP_K, T), F32), jax.ShapeDtypeStruct((TOP_K, T), I32),
                   tshape, tshape, tshape],
        scratch_shapes=[pltpu.VMEM((8, LANES), F32)],
        compiler_params=_cparams(("arbitrary",)),
        name="moe_router",
    )(h2, router_w.T, router_b.reshape(N_EXPERTS, 1))
    table = lambda a: a[:, 0, :N_EXPERTS]
    return wt, lpos, table(tlen), table(tstart), table(tcar)


def _dispatch_kernel(len_ref, lst_ref, car_ref, seg_ref, lpos_ref, h_ref, xs_ref, xloc, sem):
    i = pl.program_id(0)
    tm = h_ref.shape[0]
    nloc = xloc.shape[0]
    R = RUN_ROWS
    lpos = lpos_ref[...]
    hb = h_ref[...].astype(BF16)
    rb = 256
    off = _iota((rb, tm), 0).astype(F32).astype(BF16)
    one = jnp.ones((rb, tm), BF16)
    for blk in range(nloc // rb):
        cand = jnp.where(lpos // rb == blk, lpos % rb, -1).astype(F32).astype(BF16)
        owner = jnp.zeros((rb, tm), BF16)
        for kk in range(TOP_K):
            owner = jnp.where(off == cand[kk:kk + 1, :], one, owner)
        rows = _dot(owner, hb)
        bits = lax.bitcast_convert_type(rows, jnp.uint32)
        xloc[blk * rb:(blk + 1) * rb, :] = bits[:, 0:D_XPACK] | (bits[:, D_XPACK:D_MODEL] >> 16)

    def piece(src_row, dst_row):
        return pltpu.make_async_copy(xloc.at[pl.ds(pl.multiple_of(src_row, R), R), :],
                                     xs_ref.at[pl.ds(pl.multiple_of(dst_row, R), R), :], sem)

    def per_expert(e, issued):
        pieces = len_ref[i, e] // R
        ls = lst_ref[i, e]
        gd = seg_ref[e] + car_ref[i, e]

        def issue(j, c):
            piece(ls + j * R, gd + j * R).start()
            return c

        lax.fori_loop(0, pieces, issue, 0)
        return issued + pieces

    total = lax.fori_loop(0, N_EXPERTS, per_expert, 0)

    def wait(j, c):
        piece(0, 0).wait()
        return c

    lax.fori_loop(0, total, wait, 0)


def _dispatch(h2, lpos, tile_len, tile_lstart, tile_car, seg_start, *, tm):
    T = h2.shape[0]
    nt = T // tm
    n_rows = T * TOP_K + N_EXPERTS * nt * RUN_ROWS
    nloc = tm * TOP_K + N_EXPERTS * RUN_ROWS
    return pl.pallas_call(
        _dispatch_kernel,
        grid_spec=pltpu.PrefetchScalarGridSpec(
            num_scalar_prefetch=4,
            grid=(nt,),
            in_specs=[pl.BlockSpec((TOP_K, tm), lambda i, *_: (0, i)),
                      pl.BlockSpec((tm, D_MODEL), lambda i, *_: (i, 0))],
            out_specs=pl.BlockSpec(memory_space=pl.ANY),
            scratch_shapes=[pltpu.VMEM((nloc, D_XPACK), jnp.uint32), pltpu.SemaphoreType.DMA(())]),
        out_shape=jax.ShapeDtypeStruct((n_rows, D_XPACK), jnp.uint32),
        compiler_params=_cparams(("arbitrary",)),
        name="moe_dispatch",
    )(tile_len, tile_lstart, tile_car, seg_start, lpos, h2)


def _expert_kernel(blk_ref, exp_ref, lo_ref, hi_ref, first_ref, x_ref, w1_ref, w3_ref, w2_ref, o_ref):
    w = pl.program_id(0)
    packed = x_ref[...]
    x_a = lax.bitcast_convert_type(packed & jnp.uint32(0xFFFF0000), F32).astype(BF16)
    x_b = lax.bitcast_convert_type(packed << 16, F32).astype(BF16)
    w1 = w1_ref[...].astype(BF16)
    w3 = w3_ref[...].astype(BF16)
    a = _dot(x_a, w1[0:D_XPACK]) + _dot(x_b, w1[D_XPACK:D_MODEL])
    b = _dot(x_a, w3[0:D_XPACK]) + _dot(x_b, w3[D_XPACK:D_MODEL])
    y = _dot((_silu(a) * b).astype(BF16), w2_ref[...].astype(BF16))
    r = _iota(y.shape, 0)
    y = jnp.where((r >= lo_ref[w]) & (r < hi_ref[w]), y, 0.0)

    @pl.when(first_ref[w] == 1)
    def _():
        o_ref[...] = y

    @pl.when(first_ref[w] == 0)
    def _():
        o_ref[...] = o_ref[...] + y


def _experts(xs, work, w1, w3, w2, *, layer, bm):
    A = xs.shape[0]
    n_work = work[0].shape[0]
    xmap = lambda w, blk, ex, lo, hi, first: (blk[w], 0)
    wmap = lambda w, blk, ex, lo, hi, first: (layer, ex[w], 0, 0)
    return pl.pallas_call(
        _expert_kernel,
        grid_spec=pltpu.PrefetchScalarGridSpec(
            num_scalar_prefetch=5,
            grid=(n_work,),
            in_specs=[pl.BlockSpec((bm, D_XPACK), xmap),
                      pl.BlockSpec((None, None, D_MODEL, D_EXPERT), wmap),
                      pl.BlockSpec((None, None, D_MODEL, D_EXPERT), wmap),
                      pl.BlockSpec((None, None, D_EXPERT, D_MODEL), wmap)],
            out_specs=pl.BlockSpec((bm, D_MODEL), xmap)),
        out_shape=jax.ShapeDtypeStruct((A, D_MODEL), F32),
        compiler_params=_cparams(("arbitrary",)),
        name="moe_experts",
    )(*work, xs, w1, w3, w2)


def _combine_kernel(len_ref, lst_ref, car_ref, seg_ref, lpos_ref, wt_ref, ys_ref, h_ref, x_ref, mod_ref,
                    s1_ref, s3_ref, s2_ref, fg_ref, o_ref, yloc, sem, *, final_norm):
    i = pl.program_id(0)
    tm = h_ref.shape[0]
    nloc = yloc.shape[0]
    R = RUN_ROWS

    @pl.when(i == 0)
    def _():
        yloc[...] = jnp.zeros_like(yloc)

    def piece(src_row, dst_row):
        return pltpu.make_async_copy(ys_ref.at[pl.ds(pl.multiple_of(src_row, R), R), :],
                                     yloc.at[pl.ds(pl.multiple_of(dst_row, R), R), :], sem)

    def per_expert(e, issued):
        pieces = len_ref[i, e] // R
        ls = lst_ref[i, e]
        gd = seg_ref[e] + car_ref[i, e]

        def issue(j, c):
            piece(gd + j * R, ls + j * R).start()
            return c

        lax.fori_loop(0, pieces, issue, 0)
        return issued + pieces

    total = lax.fori_loop(0, N_EXPERTS, per_expert, 0)

    hb = h_ref[...].astype(BF16)
    mid = _silu(_dot(hb, s1_ref[...])) * _dot(hb, s3_ref[...])
    y = _dot(mid.astype(BF16), s2_ref[...])

    def wait(j, c):
        piece(0, 0).wait()
        return c

    lax.fori_loop(0, total, wait, 0)

    lpos = lpos_ref[...]
    wt = wt_ref[...]
    used_rows = lst_ref[i, N_EXPERTS - 1] + len_ref[i, N_EXPERTS - 1]
    rb = 256

    def block_sum(blk):
        r = _iota((rb, tm), 0) + blk * rb
        pw = jnp.zeros((rb, tm), F32)
        for kk in range(TOP_K):
            pw = jnp.where(r == lpos[kk:kk + 1, :], wt[kk:kk + 1, :], pw)
        ysc = yloc[blk * rb:(blk + 1) * rb, :] * jnp.sum(pw, axis=1, keepdims=True)
        hi = ysc.astype(BF16)
        lo = (ysc - hi.astype(F32)).astype(BF16)
        owner = (pw != 0.0).astype(BF16)
        return _dot_tn(owner, hi) + _dot_tn(owner, lo)

    for blk in range(nloc // rb):
        if (blk + 1) * rb <= tm * TOP_K:
            y = y + block_sum(blk)
        else:
            y = y + lax.cond(used_rows > blk * rb, functools.partial(block_sum, blk),
                             lambda: jnp.zeros((tm, D_MODEL), F32))
    x = x_ref[...] + mod_ref[5:6, :] * y
    if final_norm:
        x = x * lax.rsqrt(jnp.mean(x * x, axis=-1, keepdims=True) + EPS) * fg_ref[...]
    o_ref[...] = x


def _combine(ys, lpos, wt, tile_len, tile_lstart, tile_car, seg_start, h2, x, mod, s1, s3, s2, final_g,
             *, tm, tokens_per_mod, final_norm):
    T = x.shape[0]
    tiles_per_mod = tokens_per_mod // tm
    nloc = tm * TOP_K + N_EXPERTS * RUN_ROWS
    xspec = pl.BlockSpec((tm, D_MODEL), lambda i, *_: (i, 0))
    kspec = pl.BlockSpec((TOP_K, tm), lambda i, *_: (0, i))
    const = lambda i, *_: (0, 0)
    return pl.pallas_call(
        functools.partial(_combine_kernel, final_norm=final_norm),
        grid_spec=pltpu.PrefetchScalarGridSpec(
            num_scalar_prefetch=4,
            grid=(T // tm,),
            in_specs=[kspec, kspec,
                      pl.BlockSpec(memory_space=pl.ANY),
                      xspec, xspec,
                      pl.BlockSpec((None, 6, D_MODEL), lambda i, *_: (i // tiles_per_mod, 0, 0)),
                      pl.BlockSpec((D_MODEL, D_SHARED), const),
                      pl.BlockSpec((D_MODEL, D_SHARED), const),
                      pl.BlockSpec((D_SHARED, D_MODEL), const),
                      pl.BlockSpec((1, D_MODEL), const)],
            out_specs=xspec,
            scratch_shapes=[pltpu.VMEM((nloc, D_MODEL), F32), pltpu.SemaphoreType.DMA(())]),
        out_shape=jax.ShapeDtypeStruct((T, D_MODEL), F32),
        compiler_params=_cparams(("arbitrary",)),
        name="moe_combine",
    )(tile_len, tile_lstart, tile_car, seg_start, lpos, wt, ys, h2, x, mod, s1, s3, s2, final_g.reshape(1, D_MODEL))


def _work_list(counts, starts, n_rows, bm):
    E = N_EXPERTS
    n_blocks = n_rows // bm
    n_work = n_blocks + E - 1
    ends = starts + counts
    first_blk = starts // bm
    last_blk = jnp.where(counts > 0, (ends - 1) // bm, first_blk)
    n_items = jnp.where(counts > 0, last_blk - first_blk + 1, 0)
    item_end = jnp.cumsum(n_items)
    item_start = item_end - n_items
    w = jnp.arange(n_work, dtype=I32)
    ex = jnp.minimum(jnp.sum((item_end[None, :] <= w[:, None]).astype(I32), axis=1), E - 1)
    valid = w < item_end[-1]
    blk = first_blk[ex] + (w - item_start[ex])
    blk = jnp.where(valid, blk, blk[jnp.maximum(item_end[-1] - 1, 0)]).astype(I32)
    lo = jnp.clip(starts[ex] - blk * bm, 0, bm)
    hi = jnp.clip(ends[ex] - blk * bm, 0, bm)
    lo = jnp.where(valid, lo, 0).astype(I32)
    hi = jnp.where(valid, hi, 0).astype(I32)
    ex = jnp.where(valid, ex, ex[jnp.maximum(item_end[-1] - 1, 0)])
    prev_blk = jnp.concatenate([jnp.full((1,), -1, I32), blk[:-1]])
    first = (blk != prev_blk).astype(I32)
    return blk, ex.astype(I32), lo, hi, first


def _moe(h2, x, mod, p, final_g, *, layer, tokens_per_mod, final_norm):
    T = x.shape[0]
    tm = 256
    wt, lpos, tile_len, tile_lstart, tile_car = _router(h2, p["router_w"], p["router_b"], tm=tm)
    seg_len = tile_car[-1] + tile_len[-1]
    seg_start = jnp.cumsum(seg_len) - seg_len
    xs = _dispatch(h2, lpos, tile_len, tile_lstart, tile_car, seg_start, tm=tm)
    bm = 512
    work = _work_list(seg_len, seg_start, xs.shape[0], bm)
    ys = _experts(xs, work, p["exp_w1"], p["exp_w3"], p["exp_w2"], layer=layer, bm=bm)
    return _combine(ys, lpos, wt, tile_len, tile_lstart, tile_car, seg_start, h2, x, mod,
                    p["sh_w1"], p["sh_w3"], p["sh_w2"], final_g,
                    tm=tm, tokens_per_mod=tokens_per_mod, final_norm=final_norm)


def _reorder_w_in(w_in):
    plain = D_CONV
    pieces = [w_in[..., :plain],
              w_in[..., plain + 3616:plain + 3616 + 4096],
              w_in[..., plain:plain + 3072],
              w_in[..., plain + 3088:plain + 3600],
              w_in[..., plain + 3072:plain + 3088],
              w_in[..., plain + 3600:plain + 3616]]
    out = jnp.concatenate(pieces, axis=-1)
    pad = D_PROJ - out.shape[-1]
    return jnp.pad(out, ((0, 0), (0, 0), (0, pad))).astype(BF16)


def _small_rows(ssd_vals, gdn_vals):
    row = jnp.zeros((LANES,), F32)
    row = row.at[S_DT:S_DT + 2 * SSD_HEADS].set(ssd_vals.reshape(-1))
    row = row.at[S_GA:S_GA + 2 * GDN_HEADS].set(gdn_vals.reshape(-1))
    return row.reshape(1, LANES)


def _layer_pass(x, mod, lp, hy, st_hg, st_ssd, st_gdn, final_g, *, B, L, tokens_per_mod, seg, layer, final_norm):
    T = B * L
    proj = _in_projection(x, mod, lp["norm1_g"], lp["w_in"], lp["conv_w"], lp["conv_b"],
                          layer=layer, tokens_per_mod=tokens_per_mod, seg=seg)
    proj3 = proj.reshape(B, L, D_PROJ)
    fmat, gmat, hspec = hy
    cb = lambda c: c // D_BRANCH
    z = _spectral_conv(fmat, gmat, hspec, 0, lp["hy_bias"][0], proj3, cb(C_HYV), proj3, cb(C_HYX1))
    ya = _spectral_conv(fmat, gmat, hspec, 1, lp["hy_bias"][1], z, 0, proj3, cb(C_HYX2))
    yb, s_hg = _hgrn_scan(proj3, lp["hg_lb"], lp["hg_norm"], st_hg, layer=layer)
    yc, s_ssd = _ssd_scan(proj3, lp["bias_row"], lp["alog_row"], lp["ssd_d"], st_ssd)
    yd, s_gdn = _gdn_scan(proj3, lp["bias_row"], lp["alog_row"], lp["gdn_norm"], st_gdn)
    flat = lambda a: a.reshape(T, D_BRANCH)
    x, h2 = _merge(flat(ya), flat(yb), flat(yc), flat(yd), proj, x, mod, lp["ssd_norm"], lp["norm2_g"],
                   lp["w_branch"], lp["w_out"], tokens_per_mod=tokens_per_mod)
    x = _moe(h2, x, mod, lp, final_g, layer=layer, tokens_per_mod=tokens_per_mod, final_norm=final_norm)
    return x, s_hg, s_ssd, s_gdn


def kernel(x_prompt, x_sample, state_hgrn, state_ssd, state_gdn, c, c_ctx, norm1_g, norm2_g, ada_w, ada_b, w_in, conv_w, conv_b, hy_w1, hy_b1, hy_w2, hy_b2, hy_w3, hy_bias, hg_lb, hg_norm, ssd_a_log, ssd_dt_bias, ssd_d, ssd_norm, gdn_a_log, gdn_dt_bias, gdn_norm, w_branch, w_out, router_w, router_bias, exp_w1, exp_w3, exp_w2, sh_w1, sh_w3, sh_w2, final_g):
    depth = w_in.shape[0]
    bp, lp_len = x_prompt.shape[0], x_prompt.shape[1]
    bs, ls_len = x_sample.shape[0], x_sample.shape[1]
    D = D_MODEL

    w_in_r = _reorder_w_in(w_in)
    hy_w1p = jnp.pad(hy_w1, ((0, 0), (0, LANES - hy_w1.shape[1]), (0, 0)))
    layers = []
    for l in range(depth):
        layers.append(dict(
            norm1_g=norm1_g[l], norm2_g=norm2_g[l], w_in=w_in_r, conv_w=conv_w[l], conv_b=conv_b[l],
            hy_bias=hy_bias[l], hg_lb=hg_lb, hg_norm=hg_norm[l],
            bias_row=_small_rows(ssd_dt_bias[l], gdn_dt_bias[l]),
            alog_row=_small_rows(ssd_a_log[l], gdn_a_log[l]),
            ssd_d=ssd_d[l], ssd_norm=ssd_norm[l], gdn_norm=gdn_norm[l],
            w_branch=w_branch[l].astype(BF16), w_out=w_out[l].astype(BF16),
            router_w=router_w[l], router_b=router_bias[l],
            exp_w1=exp_w1, exp_w3=exp_w3, exp_w2=exp_w2,
            sh_w1=sh_w1[l].astype(BF16), sh_w3=sh_w3[l].astype(BF16), sh_w2=sh_w2[l].astype(BF16)))

    def hyena_setup(L):
        fmat, fs = _dft_matrices(L)
        gmat = fs.T
        specs = []
        for l in range(depth):
            filt = _hyena_filters(L, hy_w1p[l], hy_b1[l], hy_w2[l], hy_b2[l], hy_w3[l])
            specs.append(_filter_spectrum(fmat, filt))
        return fmat, gmat, specs

    cond = jnp.concatenate([c_ctx.reshape(1, D), c], axis=0)
    rows = cond.shape[0]
    rows8 = (rows + 7) // 8 * 8
    cond8 = jnp.pad(cond, ((0, rows8 - rows), (0, 0)))
    mods = [_modulation(cond8, ada_w, ada_b[l], layer=l).reshape(rows8, 6, D) for l in range(depth)]

    fmat, gmat, specs = hyena_setup(lp_len)
    x = x_prompt.reshape(bp * lp_len, D)
    z_hg = jnp.zeros((bp, 2, HG_HEADS, HG_DK, HG_DK), F32)
    z_ssd = jnp.zeros((bp, 2, SSD_HEADS, SSD_P, SSD_N), F32)
    z_gdn = jnp.zeros((bp, 2, GDN_HEADS, GDN_DK, GDN_DK), F32)
    hg_states, ssd_states, gdn_states = [], [], []
    for l in range(depth):
        x, s_hg, s_ssd, s_gdn = _layer_pass(
            x, mods[l][0:1], layers[l], (fmat, gmat, specs[l]), z_hg, z_ssd, z_gdn, final_g,
            B=bp, L=lp_len, tokens_per_mod=bp * lp_len, seg=lp_len, layer=l, final_norm=(l == depth - 1))
        hg_states.append(s_hg)
        ssd_states.append(s_ssd)
        gdn_states.append(s_gdn)
    y_prompt = x.reshape(bp, lp_len, D)
    new_hg = jnp.stack(hg_states, axis=1)
    new_ssd = jnp.stack(ssd_states, axis=1)
    new_gdn = jnp.stack(gdn_states, axis=1)

    fmat, gmat, specs = hyena_setup(ls_len)
    x = x_sample.reshape(bs * ls_len, D)
    for l in range(depth):
        x, _, _, _ = _layer_pass(
            x, mods[l][1:1 + bs], layers[l], (fmat, gmat, specs[l]),
            state_hgrn[:, l], state_ssd[:, l], state_gdn[:, l], final_g,
            B=bs, L=ls_len, tokens_per_mod=ls_len, seg=GRID_W, layer=l, final_norm=(l == depth - 1))
    y_sample = x.reshape(bs, ls_len, D)
    return (y_prompt, y_sample, new_hg, new_ssd, new_gdn)
```

```python
import functools
import math

import jax
import jax.numpy as jnp
from jax import lax
from jax.experimental import pallas as pl
from jax.experimental.pallas import tpu as pltpu

F32 = jnp.float32
BF16 = jnp.bfloat16
I32 = jnp.int32
HIGHEST = lax.Precision.HIGHEST

D_MODEL = 1024
GRID_W = 64
EPS = 1e-6
LOG_FLOOR = 1e-30
MASK_NEG = -1e30
N_BRANCH = 4
D_BRANCH = 512
HY_POS_FREQS = 16
HY_FILTER_HIDDEN = 64
HY_FAST_DECAY = 0.3
HY_SLOW_DECAY = 1.5
HY_DECAY_TARGET = 1e-2
HG_HEADS = 4
HG_DK = 128
HG_CHUNK = 16
SSD_HEADS = 8
SSD_P = 64
SSD_N = 128
SSD_CHUNK = 64
GDN_HEADS = 4
GDN_DK = 128
GDN_CHUNK = 64
N_EXPERTS = 64
TOP_K = 8
N_EXPERT_GROUPS = 8
TOPK_GROUPS = 4
D_EXPERT = 256
D_SHARED = 256
ROUTED_SCALE = 2.5

LANES = 128
D_XPACK = D_MODEL // 2
RUN_ROWS = 8

D_CONV = 4096
C_HYV, C_HYX1, C_HYX2 = 0, 512, 1024
C_SSDX, C_SSDB, C_SSDC = 1536, 2048, 2304
C_GQ, C_GK, C_GV = 2560, 3072, 3584
P_MERGE = 4096
P_HGQ, P_HGFF, P_HGFB, P_HGV, P_HGG = 8192, 8704, 9216, 9728, 10240
P_SSDZ, P_GDNG, P_SMALL = 10752, 11264, 11776
D_PROJ = 12288
S_DT, S_GA, S_GB = 0, 16, 24

VMEM_LIMIT = 56 * 1024 * 1024


def _cparams(sem):
    return pltpu.CompilerParams(dimension_semantics=sem, vmem_limit_bytes=VMEM_LIMIT)


def _sigmoid(x):
    return 1.0 / (1.0 + jnp.exp(-x))


def _silu(x):
    return x * _sigmoid(x)


def _softplus(x):
    return jnp.maximum(x, 0.0) + jnp.log(1.0 + jnp.exp(-jnp.abs(x)))


def _dot(a, b, precision=None):
    return jnp.dot(a, b, preferred_element_type=F32, precision=precision)


def _dot_nt(a, b, precision=None):
    return lax.dot_general(a, b, (((1,), (1,)), ((), ())), preferred_element_type=F32, precision=precision)


def _dot_tn(a, b, precision=None):
    return lax.dot_general(a, b, (((0,), (0,)), ((), ())), preferred_element_type=F32, precision=precision)


def _bdot(a, b):
    return _dot(a.astype(BF16), b.astype(BF16))


def _bdot_nt(a, b):
    return _dot_nt(a.astype(BF16), b.astype(BF16))


def _bdot_tn(a, b):
    return _dot_tn(a.astype(BF16), b.astype(BF16))


def _iota(shape, dim):
    return lax.broadcasted_iota(I32, shape, dim)


def _cumsum_rows(g, reverse):
    n = g.shape[0]
    row = _iota(g.shape, 0)
    sh = 1
    while sh < n:
        if reverse:
            g = g + jnp.where(row < n - sh, pltpu.roll(g, n - sh, 0), 0.0)
        else:
            g = g + jnp.where(row >= sh, pltpu.roll(g, sh, 0), 0.0)
        sh *= 2
    return g


def _lane_pick(a, j):
    return jnp.sum(jnp.where(_iota(a.shape, 1) == j, a, 0.0), axis=1, keepdims=True)


def _split3(a):
    a1 = a.astype(BF16)
    r1 = a - a1.astype(F32)
    a2 = r1.astype(BF16)
    a3 = (r1 - a2.astype(F32)).astype(BF16)
    return a1, a2, a3


def _dot3(a, b):
    a1, a2, _ = _split3(a)
    b1, b2, _ = _split3(b)
    return _dot(a1, b1) + (_dot(a1, b2) + _dot(a2, b1))


def _row_pick(a, j):
    sel = (_iota((8, a.shape[1]), 1) == j).astype(BF16)
    a1, a2, a3 = _split3(a)
    return (_dot_nt(sel, a1) + (_dot_nt(sel, a2) + _dot_nt(sel, a3)))[0:1, :]


def _tri_mask(n, reverse, strict=False):
    t = _iota((n, n), 0)
    s = _iota((n, n), 1)
    if reverse:
        return (s > t) if strict else (s >= t)
    return (s < t) if strict else (s <= t)


def _masked_exp(mask, diff):
    return jnp.where(mask, jnp.exp(jnp.where(mask, diff, 0.0)), 0.0)


def _mod_kernel(c_ref, w_ref, b_ref, o_ref):
    o_ref[...] = _dot(_silu(c_ref[...]), w_ref[...], precision=HIGHEST) + b_ref[...]


def _modulation(cond8, ada_w, ada_b, *, layer):
    rows = cond8.shape[0]
    tn = 1536
    n = ada_w.shape[2]
    return pl.pallas_call(
        _mod_kernel,
        grid=(n // tn,),
        in_specs=[pl.BlockSpec((rows, D_MODEL), lambda j: (0, 0)),
                  pl.BlockSpec((None, D_MODEL, tn), lambda j: (layer, 0, j)),
                  pl.BlockSpec((1, tn), lambda j: (0, j))],
        out_specs=pl.BlockSpec((rows, tn), lambda j: (0, j)),
        out_shape=jax.ShapeDtypeStruct((rows, n), F32),
        compiler_params=_cparams(("arbitrary",)),
        name="adaln_mod",
    )(cond8, ada_w, ada_b.reshape(1, n))


def _inproj_kernel(x_ref, mod_ref, g_ref, w_ref, cw_ref, cb_ref, o_ref, h_scr, *, seg, n_conv_tiles):
    j = pl.program_id(1)

    @pl.when(j == 0)
    def _():
        x = x_ref[...]
        xn = x * lax.rsqrt(jnp.mean(x * x, axis=-1, keepdims=True) + EPS) * g_ref[...]
        m = mod_ref[...]
        h_scr[...] = (xn * (1.0 + m[1:2]) + m[0:1]).astype(BF16)

    y = _dot(h_scr[...], w_ref[...])

    @pl.when(j < n_conv_tiles)
    def _():
        tm = y.shape[0]
        pos = _iota(y.shape, 0) & (seg - 1)
        prev = jnp.where(pos == 0, 0.0, pltpu.roll(y, 1, 0))
        nxt = jnp.where(pos == seg - 1, 0.0, pltpu.roll(y, tm - 1, 0))
        cw = cw_ref[...]
        o_ref[...] = cb_ref[...] + prev * cw[0:1] + y * cw[1:2] + nxt * cw[2:3]

    @pl.when(j >= n_conv_tiles)
    def _():
        o_ref[...] = y


def _in_projection(x, mod, norm_g, w_bf16, conv_w, conv_b, *, layer, tokens_per_mod, seg):
    T = x.shape[0]
    tm = min(2048, tokens_per_mod)
    tn = 1024
    n_conv_tiles = D_CONV // tn
    tiles_per_mod = tokens_per_mod // tm
    kern = functools.partial(_inproj_kernel, seg=seg, n_conv_tiles=n_conv_tiles)
    cmap = lambda i, j: (0, jnp.minimum(j, n_conv_tiles - 1))
    return pl.pallas_call(
        kern,
        grid=(T // tm, D_PROJ // tn),
        in_specs=[pl.BlockSpec((tm, D_MODEL), lambda i, j: (i, 0)),
                  pl.BlockSpec((None, 6, D_MODEL), lambda i, j: (i // tiles_per_mod, 0, 0)),
                  pl.BlockSpec((1, D_MODEL), lambda i, j: (0, 0)),
                  pl.BlockSpec((None, D_MODEL, tn), lambda i, j: (layer, 0, j)),
                  pl.BlockSpec((3, tn), cmap),
                  pl.BlockSpec((1, tn), cmap)],
        out_specs=pl.BlockSpec((tm, tn), lambda i, j: (i, j)),
        out_shape=jax.ShapeDtypeStruct((T, D_PROJ), F32),
        scratch_shapes=[pltpu.VMEM((tm, D_MODEL), BF16)],
        compiler_params=_cparams(("arbitrary", "arbitrary")),
        name="in_proj",
    )(x, mod, norm_g.reshape(1, D_MODEL), w_bf16, conv_w, conv_b.reshape(1, D_CONV))


def _hyfilt_kernel(w1_ref, b1_ref, w2_ref, b2_ref, w3_ref, o_ref, *, L):
    i = pl.program_id(1)
    tl = o_ref.shape[0]
    t = (_iota((tl, LANES), 0) + i * tl).astype(F32) / L
    lane = _iota((tl, LANES), 1)
    band = jnp.where(lane <= HY_POS_FREQS, lane, lane - HY_POS_FREQS).astype(F32)
    ang = 2.0 * math.pi * t * band
    feats = jnp.where(lane == 0, t,
                      jnp.where(lane <= HY_POS_FREQS, jnp.sin(ang),
                                jnp.where(lane <= 2 * HY_POS_FREQS, jnp.cos(ang), 0.0)))
    hdn = jnp.sin(_dot(feats, w1_ref[...], precision=HIGHEST) + b1_ref[...])
    hdn = jnp.sin(_dot(hdn, w2_ref[...], precision=HIGHEST) + b2_ref[...])
    filt = _dot(hdn, w3_ref[...], precision=HIGHEST)
    max_decay = math.log(HY_DECAY_TARGET) / HY_FAST_DECAY
    min_decay = math.log(HY_DECAY_TARGET) / HY_SLOW_DECAY
    n = filt.shape[1]
    ch = (_iota((tl, n), 1) & (D_BRANCH - 1)).astype(F32)
    delta = min_decay + ch * ((max_decay - min_decay) / (D_BRANCH - 1))
    tt = (_iota((tl, n), 0) + i * tl).astype(F32) / L
    o_ref[...] = filt * jnp.exp(-tt * jnp.abs(delta))


def _hyena_filters(L, w1p, b1, w2, b2, w3):
    tl = min(L, 256)
    n = 2 * D_BRANCH
    return pl.pallas_call(
        functools.partial(_hyfilt_kernel, L=L),
        grid=(2, L // tl),
        in_specs=[pl.BlockSpec((LANES, HY_FILTER_HIDDEN), lambda d, i: (0, 0)),
                  pl.BlockSpec((1, HY_FILTER_HIDDEN), lambda d, i: (0, 0)),
                  pl.BlockSpec((HY_FILTER_HIDDEN, HY_FILTER_HIDDEN), lambda d, i: (0, 0)),
                  pl.BlockSpec((1, HY_FILTER_HIDDEN), lambda d, i: (0, 0)),
                  pl.BlockSpec((HY_FILTER_HIDDEN, n), lambda d, i: (0, d))],
        out_specs=pl.BlockSpec((None, tl, n), lambda d, i: (d, i, 0)),
        out_shape=jax.ShapeDtypeStruct((2, L, n), F32),
        compiler_params=_cparams(("arbitrary", "arbitrary")),
        name="hyena_filters",
    )(w1p, b1.reshape(1, -1), w2, b2.reshape(1, -1), w3)


def _dftgen_kernel(f_ref, fs_ref, *, L, tk):
    i = pl.program_id(0)
    N = 2 * L
    k = _iota((tk, LANES), 0) + i * tk
    lane = _iota((tk, LANES), 1)
    w = 2.0 * math.pi / N
    a0 = ((k * lane) & (N - 1)).astype(F32) * w
    c0, s0 = jnp.cos(a0), jnp.sin(a0)
    a1 = ((k * lane * LANES) & (N - 1)).astype(F32) * w
    c1, s1 = jnp.cos(a1), jnp.sin(a1)
    alt = jnp.where((lane & 1) == 0, 1.0, -1.0)
    coef = jnp.where(k == 0, 1.0 / N, 2.0 / N)
    for n1 in range(L // LANES):
        c1b = c1[:, n1:n1 + 1]
        s1b = s1[:, n1:n1 + 1]
        cosb = c1b * c0 - s1b * s0
        sinb = jnp.where(k == 0, alt, -(s1b * c0 + c1b * s0))
        cols = slice(n1 * LANES, (n1 + 1) * LANES)
        f_ref[0:tk, cols] = cosb.astype(BF16)
        f_ref[tk:2 * tk, cols] = sinb.astype(BF16)
        fs_ref[0:tk, cols] = (coef * cosb).astype(BF16)
        fs_ref[tk:2 * tk, cols] = (coef * sinb).astype(BF16)


def _dft_matrices(L):
    tk = min(L, 256)
    spec = pl.BlockSpec((2 * tk, L), lambda i: (i, 0))
    return pl.pallas_call(
        functools.partial(_dftgen_kernel, L=L, tk=tk),
        grid=(L // tk,),
        in_specs=[],
        out_specs=[spec, spec],
        out_shape=[jax.ShapeDtypeStruct((2 * L, L), BF16)] * 2,
        compiler_params=_cparams(("arbitrary",)),
        name="dft_matrices",
    )()


def _dfth_kernel(f_ref, h_ref, o_ref, hp_scr, *, tk):
    i = pl.program_id(1)
    n = o_ref.shape[1]

    @pl.when(i == 0)
    def _():
        hc = h_ref[0]
        ha = jnp.where(_iota(hc.shape, 0) == 0, 0.0, h_ref[1])
        hp_scr[:, 0:n] = (hc + ha).astype(BF16)
        hp_scr[:, n:2 * n] = (hc - ha).astype(BF16)

    u = _dot(f_ref[...], hp_scr[...])
    r = _iota((2 * tk, n), 0)
    from_sum = (r < tk) | ((r == tk) & (i == 0))
    o_ref[...] = jnp.where(from_sum, u[:, 0:n], u[:, n:2 * n])


def _filter_spectrum(fmat, filt):
    L = fmat.shape[1]
    tk = min(L, 256)
    C = filt.shape[2]
    tc = 256
    return pl.pallas_call(
        functools.partial(_dfth_kernel, tk=tk),
        grid=(C // tc, L // tk),
        in_specs=[pl.BlockSpec((2 * tk, L), lambda c, i: (i, 0)),
                  pl.BlockSpec((2, L, tc), lambda c, i: (0, 0, c))],
        out_specs=pl.BlockSpec((2 * tk, tc), lambda c, i: (i, c)),
        out_shape=jax.ShapeDtypeStruct((2 * L, C), F32),
        scratch_shapes=[pltpu.VMEM((L, 2 * tc), BF16)],
        compiler_params=_cparams(("arbitrary", "arbitrary")),
        name="filter_spectrum",
    )(fmat, filt)


def _dfta_kernel(f_ref, u_ref, h_ref, y_ref, u_scr, *, tk):
    i = pl.program_id(1)

    @pl.when(i == 0)
    def _():
        u_scr[...] = u_ref[...].astype(BF16)

    uf = _dot(f_ref[...], u_scr[...])
    ur, ui = uf[0:tk], uf[tk:2 * tk]
    hr, hi = h_ref[0:tk, :], h_ref[tk:2 * tk, :]
    dc = (_iota(ur.shape, 0) == 0) & (i == 0)
    y_ref[0:tk, :] = jnp.where(dc, ur * hr, ur * hr - ui * hi).astype(BF16)
    y_ref[tk:2 * tk, :] = jnp.where(dc, ui * hi, ur * hi + ui * hr).astype(BF16)


def _dftb_kernel(g_ref, y_ref, u_ref, x_ref, b_ref, o_ref):
    y = _dot(g_ref[...], y_ref[...])
    o_ref[...] = x_ref[...] * (y + u_ref[...] * b_ref[...])


def _spectral_conv(fmat, gmat, hspec, h_col, bias, u_arr, u_col, gate_arr, gate_col):
    B, L = u_arr.shape[0], u_arr.shape[1]
    C = D_BRANCH
    tk = min(L, 256)
    y = pl.pallas_call(
        functools.partial(_dfta_kernel, tk=tk),
        grid=(B, L // tk),
        in_specs=[pl.BlockSpec((2 * tk, L), lambda b, i: (i, 0)),
                  pl.BlockSpec((None, L, C), lambda b, i: (b, 0, u_col)),
                  pl.BlockSpec((2 * tk, C), lambda b, i: (i, h_col))],
        out_specs=pl.BlockSpec((None, 2 * tk, C), lambda b, i: (b, i, 0)),
        out_shape=jax.ShapeDtypeStruct((B, 2 * L, C), BF16),
        scratch_shapes=[pltpu.VMEM((L, C), BF16)],
        compiler_params=_cparams(("arbitrary", "arbitrary")),
        name="hyena_dft_fwd",
    )(fmat, u_arr, hspec)
    tr = min(L, 256)
    return pl.pallas_call(
        _dftb_kernel,
        grid=(B, L // tr),
        in_specs=[pl.BlockSpec((tr, 2 * L), lambda b, i: (i, 0)),
                  pl.BlockSpec((None, 2 * L, C), lambda b, i: (b, 0, 0)),
                  pl.BlockSpec((None, tr, C), lambda b, i: (b, i, u_col)),
                  pl.BlockSpec((None, tr, C), lambda b, i: (b, i, gate_col)),
                  pl.BlockSpec((1, C), lambda b, i: (0, 0))],
        out_specs=pl.BlockSpec((None, tr, C), lambda b, i: (b, i, 0)),
        out_shape=jax.ShapeDtypeStruct((B, L, C), F32),
        compiler_params=_cparams(("arbitrary", "arbitrary")),
        name="hyena_dft_inv",
    )(gmat, y, u_arr, gate_arr, bias.reshape(1, C))


def _hgrn_kernel(q_ref, ff_ref, fb_ref, v_ref, g_ref, lb_ref, nrm_ref, s0_ref, y_ref, sf_ref,
                 of_scr, ob_scr, st_scr, *, L, layer, depth):
    C = HG_CHUNK
    W = HG_DK
    nc = L // C
    nh = q_ref.shape[1] // W
    ridx = _iota((C, W), 0)
    o_scrs = (of_scr, ob_scr)
    f_refs = (ff_ref, fb_ref)

    def lower_bound(d, cols):
        rows = [lb_ref[d, l:l + 1, cols] for l in range(depth)]
        m = rows[0]
        for r in rows[1:]:
            m = jnp.maximum(m, r)
        es = [jnp.exp(r - m) for r in rows]
        tot = es[0]
        for e in es[1:]:
            tot = tot + e
        acc = es[0] / tot
        for e in es[1:layer + 1]:
            acc = acc + e / tot
        return acc - es[0] / tot

    lbs = [[lower_bound(d, slice(hh * W, (hh + 1) * W)) for hh in range(nh)] for d in (0, 1)]
    for d in (0, 1):
        for hh in range(nh):
            st_scr[d, hh] = s0_ref[d, hh].T

    def step(c, d, hh):
        rev = d == 1
        rows = pl.ds(pl.multiple_of(c * C, C), C)
        cols = slice(hh * W, (hh + 1) * W)
        lb = lbs[d][hh]
        q = _silu(q_ref[rows, cols])
        uf = f_refs[d][rows, cols]
        v = v_ref[rows, cols]
        f = lb + (1.0 - lb) * _sigmoid(uf)
        g = jnp.log(jnp.maximum(f, LOG_FLOOR))
        kin = (1.0 - lb) * _sigmoid(-uf)
        b = _cumsum_rows(g, rev)
        st = st_scr[d, hh]
        o = _bdot_nt(q * jnp.exp(b), st)
        intra = jnp.zeros((C, W), F32)
        for t in range(C):
            mask = (ridx >= t) if rev else (ridx <= t)
            pair = _masked_exp(mask, b[t:t + 1, :] - b)
            a = jnp.sum(pair * (q[t:t + 1, :] * kin), axis=1, keepdims=True)
            row = jnp.sum(a * v, axis=0, keepdims=True)
            intra = jnp.where(ridx == t, row, intra)
        b_end = b[0:1, :] if rev else b[C - 1:C, :]
        st_scr[d, hh] = st * jnp.exp(b_end) + _bdot_tn(v, kin * jnp.exp(b_end - b))
        o_scrs[d][rows, cols] = o + intra

    def body(ci, carry):
        for d in (0, 1):
            for hh in range(nh):
                step((nc - 1 - ci) if d == 1 else ci, d, hh)
        return carry

    lax.fori_loop(0, nc, body, 0)
    for d in (0, 1):
        for hh in range(nh):
            sf_ref[d, hh] = st_scr[d, hh].T

    tr = min(L, 256)

    def fin(i, carry):
        rows = pl.ds(pl.multiple_of(i * tr, tr), tr)
        for hh in range(nh):
            cols = slice(hh * W, (hh + 1) * W)
            o = of_scr[rows, cols] + ob_scr[rows, cols]
            on = o * lax.rsqrt(jnp.mean(o * o, axis=-1, keepdims=True) + EPS) * nrm_ref[...]
            y_ref[rows, cols] = on * _silu(g_ref[rows, cols])
        return carry

    lax.fori_loop(0, L // tr, fin, 0)


def _hgrn_scan(proj3, hg_lb, hg_norm, s0, *, layer):
    B, L = proj3.shape[0], proj3.shape[1]
    depth = hg_lb.shape[1]
    nh = 2
    W = nh * HG_DK
    col = lambda base: (lambda b, h: (b, 0, base // W + h))
    st_spec = pl.BlockSpec((None, 2, nh, HG_DK, HG_DK), lambda b, h: (b, 0, h, 0, 0))
    return pl.pallas_call(
        functools.partial(_hgrn_kernel, L=L, layer=layer, depth=depth),
        grid=(B, HG_HEADS // nh),
        in_specs=[pl.BlockSpec((None, L, W), col(P_HGQ)),
                  pl.BlockSpec((None, L, W), col(P_HGFF)),
                  pl.BlockSpec((None, L, W), col(P_HGFB)),
                  pl.BlockSpec((None, L, W), col(P_HGV)),
                  pl.BlockSpec((None, L, W), col(P_HGG)),
                  pl.BlockSpec((2, depth, W), lambda b, h: (0, 0, h)),
                  pl.BlockSpec((1, HG_DK), lambda b, h: (0, 0)),
                  st_spec],
        out_specs=[pl.BlockSpec((None, L, W), lambda b, h: (b, 0, h)), st_spec],
        out_shape=[jax.ShapeDtypeStruct((B, L, D_BRANCH), F32),
                   jax.ShapeDtypeStruct(s0.shape, F32)],
        scratch_shapes=[pltpu.VMEM((L, W), F32), pltpu.VMEM((L, W), F32),
                        pltpu.VMEM((2, nh, HG_DK, HG_DK), F32)],
        compiler_params=_cparams(("arbitrary", "arbitrary")),
        name="hgrn2_scan",
    )(proj3, proj3, proj3, proj3, proj3, hg_lb, hg_norm.reshape(1, HG_DK), s0)


def _ssd_kernel(x_ref, bm_ref, cm_ref, z_ref, sm_ref, bias_ref, alog_ref, dskip_ref, s0_ref,
                y_ref, sf_ref, of_scr, ob_scr, st_scr, *, L):
    C = SSD_CHUNK
    W = LANES
    nc = L // C
    npairs = x_ref.shape[1] // W
    ngroups = bm_ref.shape[1] // W
    grp = pl.program_id(1)
    lane_lo = _iota((C, W), 1) < SSD_P
    row_lo = _iota((W, SSD_N), 0) < SSD_P
    pick2 = lambda a0, a1: jnp.where(lane_lo, a0, a1)
    o_scrs = (of_scr, ob_scr)
    incls = (_tri_mask(C, False), _tri_mask(C, True))
    for d in (0, 1):
        for pp in range(npairs):
            st_scr[d, pp] = s0_ref[d, pp]

    heads = [(d, pp, hh) for d in (0, 1) for pp in range(npairs) for hh in (0, 1)]
    pairs = [(d, pp) for d in (0, 1) for pp in range(npairs)]

    def body(ci, carry):
        rows_d, bm, cm, dt_all, cum_all, gram = [], {}, {}, [], [], {}
        for d in (0, 1):
            c = (nc - 1 - ci) if d == 1 else ci
            rows = pl.ds(pl.multiple_of(c * C, C), C)
            rows_d.append(rows)
            for g in range(ngroups):
                bm[d, g] = _silu(bm_ref[rows, g * W:(g + 1) * W])
                cm[d, g] = _silu(cm_ref[rows, g * W:(g + 1) * W])
            dt_all.append(_softplus(sm_ref[rows, :] + bias_ref[...]))
            cum_all.append(_cumsum_rows(-jnp.exp(alog_ref[...]) * dt_all[d], d == 1))
        lane_of = lambda u: S_DT + u[0] * SSD_HEADS + 2 * (npairs * grp + u[1]) + u[2]
        grp_of = lambda u: (u[0], u[1] // (npairs // ngroups))
        cumr = {u: _row_pick(cum_all[u[0]], lane_of(u)) for u in heads}
        for key in bm:
            gram[key] = _bdot_nt(cm[key], bm[key])
        dtc = {u: _lane_pick(dt_all[u[0]], lane_of(u)) for u in heads}
        cumc = {u: _lane_pick(cum_all[u[0]], lane_of(u)) for u in heads}
        end = {u: (cumc[u][0:1, :] if u[0] == 1 else cumc[u][C - 1:C, :]) for u in heads}
        xs = {pr: _silu(x_ref[rows_d[pr[0]], slice(pr[1] * W, (pr[1] + 1) * W)]) for pr in pairs}
        st = {pr: st_scr[pr[0], pr[1]] for pr in pairs}
        y_state = {u: _bdot_nt(cm[grp_of(u)] * jnp.exp(cumc[u]), st[u[:2]]) for u in heads}
        scores = {u: gram[grp_of(u)] * _masked_exp(incls[u[0]], cumc[u] - cumr[u]) for u in heads}
        y_intra = {}
        for u in heads:
            head_x = jnp.where(lane_lo if u[2] == 0 else ~lane_lo, xs[u[:2]], 0.0) * dtc[u]
            y_intra[u] = _bdot(scores[u], head_x)
        for pr in pairs:
            u0, u1 = pr + (0,), pr + (1,)
            xdt_e = xs[pr] * pick2(dtc[u0] * jnp.exp(end[u0] - cumc[u0]), dtc[u1] * jnp.exp(end[u1] - cumc[u1]))
            st_scr[pr[0], pr[1]] = (st[pr] * jnp.where(row_lo, jnp.exp(end[u0]), jnp.exp(end[u1]))
                                    + _bdot_tn(xdt_e, bm[grp_of(pr)]))
            o_scrs[pr[0]][rows_d[pr[0]], slice(pr[1] * W, (pr[1] + 1) * W)] = pick2(
                y_intra[u0] + y_state[u0], y_intra[u1] + y_state[u1])
        return carry

    lax.fori_loop(0, nc, body, 0)
    for d in (0, 1):
        for pp in range(npairs):
            sf_ref[d, pp] = st_scr[d, pp]

    tr = min(L, 256)

    def fin(i, carry):
        rows = pl.ds(pl.multiple_of(i * tr, tr), tr)
        y = of_scr[rows, :] + ob_scr[rows, :] + _silu(x_ref[rows, :]) * dskip_ref[...]
        y_ref[rows, :] = y * _silu(z_ref[rows, :])
        return carry

    lax.fori_loop(0, L // tr, fin, 0)


def _ssd_scan(proj3, bias_row, alog_row, ssd_d, s0):
    B, L = proj3.shape[0], proj3.shape[1]
    n_groups = 2
    n_pairs = SSD_HEADS // 2
    W = n_pairs * LANES
    s0p = s0.reshape(B, 2, n_pairs, 2 * SSD_P, SSD_N)
    dskip = jnp.repeat(ssd_d, SSD_P).reshape(1, 1, W)
    col = lambda base: (lambda b, g: (b, 0, base // W + g))
    st_spec = pl.BlockSpec((None, 2, n_pairs, LANES, SSD_N), lambda b, g: (b, 0, g, 0, 0))
    y, sf = pl.pallas_call(
        functools.partial(_ssd_kernel, L=L),
        grid=(B, 1),
        in_specs=[pl.BlockSpec((None, L, W), col(C_SSDX)),
                  pl.BlockSpec((None, L, n_groups * LANES), lambda b, g: (b, 0, C_SSDB // (n_groups * LANES))),
                  pl.BlockSpec((None, L, n_groups * LANES), lambda b, g: (b, 0, C_SSDC // (n_groups * LANES))),
                  pl.BlockSpec((None, L, W), col(P_SSDZ)),
                  pl.BlockSpec((None, L, LANES), lambda b, g: (b, 0, P_SMALL // LANES)),
                  pl.BlockSpec((1, LANES), lambda b, g: (0, 0)),
                  pl.BlockSpec((1, LANES), lambda b, g: (0, 0)),
                  pl.BlockSpec((None, 1, W), lambda b, g: (g, 0, 0)),
                  st_spec],
        out_specs=[pl.BlockSpec((None, L, W), lambda b, g: (b, 0, g)), st_spec],
        out_shape=[jax.ShapeDtypeStruct((B, L, D_BRANCH), F32),
                   jax.ShapeDtypeStruct(s0p.shape, F32)],
        scratch_shapes=[pltpu.VMEM((L, W), F32), pltpu.VMEM((L, W), F32),
                        pltpu.VMEM((2, n_pairs, LANES, SSD_N), F32)],
        compiler_params=_cparams(("arbitrary", "arbitrary")),
        name="ssd_scan",
    )(proj3, proj3, proj3, proj3, proj3, bias_row, alog_row, dskip, s0p)
    return y, sf.reshape(s0.shape)


def _l2norm(a):
    return a * lax.rsqrt(jnp.sum(a * a, axis=-1, keepdims=True) + EPS)


def _gdn_kernel(q_ref, k_ref, v_ref, g_ref, sm_ref, bias_ref, alog_ref, nrm_ref, s0_ref,
                y_ref, sf_ref, of_scr, ob_scr, st_scr, *, L):
    C = GDN_CHUNK
    W = GDN_DK
    nc = L // C
    nh = q_ref.shape[1] // W
    hblk = pl.program_id(1)
    o_scrs = (of_scr, ob_scr)
    incls = (_tri_mask(C, False), _tri_mask(C, True))
    stricts = (_tri_mask(C, False, strict=True), _tri_mask(C, True, strict=True))
    for d in (0, 1):
        for hh in range(nh):
            st_scr[d, hh] = s0_ref[d, hh]

    units = [(d, hh) for d in (0, 1) for hh in range(nh)]
    eye = (_iota((C, C), 0) == _iota((C, C), 1)).astype(F32)

    def body(ci, carry):
        rows_d, cum_d, beta_d = [], [], []
        for d in (0, 1):
            c = (nc - 1 - ci) if d == 1 else ci
            rows = pl.ds(pl.multiple_of(c * C, C), C)
            raw = sm_ref[rows, :]
            rows_d.append(rows)
            cum_d.append(_cumsum_rows(-jnp.exp(alog_ref[...]) * _softplus(raw + bias_ref[...]), d == 1))
            beta_d.append(_sigmoid(raw))
        q, k, v, gc, grow, beta = {}, {}, {}, {}, {}, {}
        for u in units:
            d, hh = u
            cols = slice(hh * W, (hh + 1) * W)
            h = nh * hblk + hh
            q[u] = _l2norm(_silu(q_ref[rows_d[d], cols])) * (GDN_DK ** -0.5)
            k[u] = _l2norm(_silu(k_ref[rows_d[d], cols]))
            v[u] = _silu(v_ref[rows_d[d], cols])
            gc[u] = _lane_pick(cum_d[d], S_GA + d * GDN_HEADS + h)
            beta[u] = _lane_pick(beta_d[d], S_GB + d * GDN_HEADS + h)
        for u in units:
            grow[u] = _row_pick(cum_d[u[0]], S_GA + u[0] * GDN_HEADS + nh * hblk + u[1])
        decay = {u: _masked_exp(incls[u[0]], gc[u] - grow[u]) for u in units}
        kb = {u: k[u] * beta[u] for u in units}
        m = {u: -jnp.where(stricts[u[0]], _bdot_nt(kb[u], k[u]) * decay[u], 0.0) for u in units}
        aqk = {u: _bdot_nt(q[u], k[u]) * decay[u] for u in units}
        p = {u: eye + m[u] for u in units}
        m = {u: _dot3(m[u], m[u]) for u in units}
        sh = 4
        while sh < C:
            both = {u: _dot3(jnp.concatenate([m[u], p[u]], axis=0), m[u]) for u in units}
            p = {u: p[u] + both[u][C:2 * C] for u in units}
            m = {u: both[u][0:C] for u in units}
            sh *= 2
        p = {u: p[u] + _dot3(p[u], m[u]) for u in units}
        rhs = {u: jnp.concatenate([v[u] * beta[u], kb[u] * jnp.exp(gc[u])], axis=1) for u in units}
        uw = {u: _dot3(p[u], rhs[u]) for u in units}
        uu = {u: uw[u][:, 0:W] for u in units}
        ww = {u: uw[u][:, W:2 * W] for u in units}
        st = {u: st_scr[u[0], u[1]] for u in units}
        v_new = {u: uu[u] - _bdot(ww[u], st[u]) for u in units}
        o = {u: _bdot(q[u] * jnp.exp(gc[u]), st[u]) + _bdot(aqk[u], v_new[u]) for u in units}
        for u in units:
            d, hh = u
            g_end = gc[u][0:1, :] if d == 1 else gc[u][C - 1:C, :]
            st_scr[d, hh] = st[u] * jnp.exp(g_end) + _bdot_tn(k[u] * jnp.exp(g_end - gc[u]), v_new[u])
            o_scrs[d][rows_d[d], slice(hh * W, (hh + 1) * W)] = o[u]
        return carry

    lax.fori_loop(0, nc, body, 0)
    for d in (0, 1):
        for hh in range(nh):
            sf_ref[d, hh] = st_scr[d, hh]

    tr = min(L, 256)

    def fin(i, carry):
        rows = pl.ds(pl.multiple_of(i * tr, tr), tr)
        for hh in range(nh):
            cols = slice(hh * W, (hh + 1) * W)
            o = of_scr[rows, cols] + ob_scr[rows, cols]
            on = o * lax.rsqrt(jnp.mean(o * o, axis=-1, keepdims=True) + EPS) * nrm_ref[...]
            y_ref[rows, cols] = on * _silu(g_ref[rows, cols])
        return carry

    lax.fori_loop(0, L // tr, fin, 0)


def _gdn_scan(proj3, bias_row, alog_row, gdn_norm, s0):
    B, L = proj3.shape[0], proj3.shape[1]
    nh = GDN_HEADS
    W = nh * GDN_DK
    col = lambda base: (lambda b, h: (b, 0, base // W + h))
    st_spec = pl.BlockSpec((None, 2, nh, GDN_DK, GDN_DK), lambda b, h: (b, 0, h, 0, 0))
    return pl.pallas_call(
        functools.partial(_gdn_kernel, L=L),
        grid=(B, GDN_HEADS // nh),
        in_specs=[pl.BlockSpec((None, L, W), col(C_GQ)),
                  pl.BlockSpec((None, L, W), col(C_GK)),
                  pl.BlockSpec((None, L, W), col(C_GV)),
                  pl.BlockSpec((None, L, W), col(P_GDNG)),
                  pl.BlockSpec((None, L, LANES), lambda b, h: (b, 0, P_SMALL // LANES)),
                  pl.BlockSpec((1, LANES), lambda b, h: (0, 0)),
                  pl.BlockSpec((1, LANES), lambda b, h: (0, 0)),
                  pl.BlockSpec((1, GDN_DK), lambda b, h: (0, 0)),
                  st_spec],
        out_specs=[pl.BlockSpec((None, L, W), lambda b, h: (b, 0, h)), st_spec],
        out_shape=[jax.ShapeDtypeStruct((B, L, D_BRANCH), F32),
                   jax.ShapeDtypeStruct(s0.shape, F32)],
        scratch_shapes=[pltpu.VMEM((L, W), F32), pltpu.VMEM((L, W), F32),
                        pltpu.VMEM((2, nh, GDN_DK, GDN_DK), F32)],
        compiler_params=_cparams(("arbitrary", "arbitrary")),
        name="gdn_scan",
    )(proj3, proj3, proj3, proj3, proj3, bias_row, alog_row, gdn_norm.reshape(1, GDN_DK), s0)


def _merge_kernel(ya_ref, yb_ref, yc_ref, yd_ref, mg_ref, x_ref, mod_ref, sn_ref, n2_ref, wb_ref, wo_ref,
                  xo_ref, h_ref):
    yc = yc_ref[...]
    half = D_BRANCH // 2
    parts = []
    for gidx in range(2):
        seg = yc[:, gidx * half:(gidx + 1) * half]
        parts.append(seg * lax.rsqrt(jnp.mean(seg * seg, axis=-1, keepdims=True) + EPS)
                     * sn_ref[:, gidx * half:(gidx + 1) * half])
    branches = (ya_ref[...], yb_ref[...], None, yd_ref[...])
    mixed = None
    for n in range(N_BRANCH):
        gate = _sigmoid(mg_ref[:, n * D_MODEL:(n + 1) * D_MODEL])
        if n == 2:
            lifted = (_dot(parts[0].astype(BF16), wb_ref[n, 0:half, :])
                      + _dot(parts[1].astype(BF16), wb_ref[n, half:D_BRANCH, :]))
        else:
            lifted = _dot(branches[n].astype(BF16), wb_ref[n])
        mixed = gate * lifted if mixed is None else mixed + gate * lifted
    m = mod_ref[...]
    x = x_ref[...] + m[2:3] * _dot(mixed.astype(BF16), wo_ref[...])
    xo_ref[...] = x
    xn = x * lax.rsqrt(jnp.mean(x * x, axis=-1, keepdims=True) + EPS) * n2_ref[...]
    h_ref[...] = xn * (1.0 + m[4:5]) + m[3:4]


def _merge(ya, yb, yc, yd, proj, x, mod, ssd_norm, norm2_g, wb_bf16, wo_bf16, *, tokens_per_mod):
    T = x.shape[0]
    tm = min(512, tokens_per_mod)
    tiles_per_mod = tokens_per_mod // tm
    yspec = pl.BlockSpec((tm, D_BRANCH), lambda i: (i, 0))
    xspec = pl.BlockSpec((tm, D_MODEL), lambda i: (i, 0))
    return pl.pallas_call(
        _merge_kernel,
        grid=(T // tm,),
        in_specs=[yspec, yspec, yspec, yspec,
                  pl.BlockSpec((tm, N_BRANCH * D_MODEL), lambda i: (i, P_MERGE // (N_BRANCH * D_MODEL))),
                  xspec,
                  pl.BlockSpec((None, 6, D_MODEL), lambda i: (i // tiles_per_mod, 0, 0)),
                  pl.BlockSpec((1, D_BRANCH), lambda i: (0, 0)),
                  pl.BlockSpec((1, D_MODEL), lambda i: (0, 0)),
                  pl.BlockSpec((N_BRANCH, D_BRANCH, D_MODEL), lambda i: (0, 0, 0)),
                  pl.BlockSpec((D_MODEL, D_MODEL), lambda i: (0, 0))],
        out_specs=[xspec, xspec],
        out_shape=[jax.ShapeDtypeStruct((T, D_MODEL), F32)] * 2,
        compiler_params=_cparams(("arbitrary",)),
        name="merge_outproj",
    )(ya, yb, yc, yd, proj, x, mod, ssd_norm.reshape(1, D_BRANCH), norm2_g.reshape(1, D_MODEL), wb_bf16, wo_bf16)


def _router_kernel(h_ref, rw_ref, rb_ref, wt_ref, lpos_ref, tlen_ref, tstart_ref, tcar_ref, carry):
    i = pl.program_id(0)
    tm = h_ref.shape[0]
    E = N_EXPERTS
    gsz = E // N_EXPERT_GROUPS
    neg_inf = -jnp.inf

    @pl.when(i == 0)
    def _():
        carry[...] = jnp.zeros_like(carry)

    scores = _sigmoid(_dot_nt(rw_ref[...], h_ref[...], precision=HIGHEST))
    biased = scores + rb_ref[...]
    eidx = _iota((E, tm), 0)
    ridx = _iota((gsz, tm), 0)

    slabs = [biased[g * gsz:(g + 1) * gsz, :] for g in range(N_EXPERT_GROUPS)]
    gs = []
    for v in slabs:
        m1 = jnp.max(v, axis=0, keepdims=True)
        i1 = jnp.min(jnp.where(v == m1, ridx, gsz), axis=0, keepdims=True)
        m2 = jnp.max(jnp.where(ridx == i1, neg_inf, v), axis=0, keepdims=True)
        gs.append(m1 + m2)
    masked = []
    for g in range(N_EXPERT_GROUPS):
        rank = jnp.zeros((1, tm), I32)
        for g2 in range(N_EXPERT_GROUPS):
            if g2 == g:
                continue
            ahead = (gs[g2] >= gs[g]) if g2 < g else (gs[g2] > gs[g])
            rank = rank + ahead.astype(I32)
        masked.append(jnp.where(rank < TOPK_GROUPS, slabs[g], MASK_NEG))
    cur = jnp.concatenate(masked, axis=0)

    krow = _iota((TOP_K, tm), 0)
    onehot = jnp.zeros((E, tm), F32)
    picks = []
    wsum = jnp.zeros((1, tm), F32)
    for kk in range(TOP_K):
        m = jnp.max(cur, axis=0, keepdims=True)
        ik = jnp.min(jnp.where(cur == m, eidx, E), axis=0, keepdims=True)
        hit = eidx == ik
        wk = jnp.sum(jnp.where(hit, scores, 0.0), axis=0, keepdims=True)
        cur = jnp.where(hit, neg_inf, cur)
        onehot = onehot + hit.astype(F32)
        picks.append((ik, wk))
        wsum = wsum + wk
    wt_out = jnp.zeros((TOP_K, tm), F32)
    for kk, (ik, wk) in enumerate(picks):
        wt_out = jnp.where(krow == kk, wk / wsum * ROUTED_SCALE, wt_out)

    earlier = (_iota((tm, tm), 0) < _iota((tm, tm), 1)).astype(BF16)
    before = _dot(onehot.astype(BF16), earlier)
    pad = lambda c: (c.astype(I32) + (RUN_ROWS - 1)) // RUN_ROWS * RUN_ROWS
    run_len = pad(jnp.sum(onehot, axis=1, keepdims=True) + jnp.zeros((E, LANES), F32)).astype(F32)
    run_start = _cumsum_rows(run_len, False) - run_len
    local = before + run_start[:, 0:1]
    lpos_out = jnp.zeros((TOP_K, tm), I32)
    for kk, (ik, wk) in enumerate(picks):
        lp = jnp.sum(jnp.where(eidx == ik, local, 0.0), axis=0, keepdims=True)
        lpos_out = jnp.where(krow == kk, lp.astype(I32), lpos_out)
    wt_ref[...] = wt_out
    lpos_ref[...] = lpos_out

    onehot_l = jnp.concatenate([onehot, jnp.zeros((LANES - E, tm), F32)], axis=0).astype(BF16)
    len_row = pad(_dot_nt(jnp.ones((8, tm), BF16), onehot_l)).astype(F32)
    lanes_before = (_iota((LANES, LANES), 0) < _iota((LANES, LANES), 1)).astype(BF16)
    tlen_ref[...] = len_row.astype(I32)
    tstart_ref[...] = _dot(len_row.astype(BF16), lanes_before).astype(I32)
    tcar_ref[...] = carry[...].astype(I32)
    carry[...] = carry[...] + len_row


def _router(h2, router_w, router_b, *, tm):
    T = h2.shape[0]
    nt = T // tm
    ospec = pl.BlockSpec((TOP_K, tm), lambda i: (0, i))
    tspec = pl.BlockSpec((None, 8, LANES), lambda i: (i, 0, 0))
    tshape = jax.ShapeDtypeStruct((nt, 8, LANES), I32)
    wt, lpos, tlen, tstart, tcar = pl.pallas_call(
        _router_kernel,
        grid=(nt,),
        in_specs=[pl.BlockSpec((tm, D_MODEL), lambda i: (i, 0)),
                  pl.BlockSpec((N_EXPERTS, D_MODEL), lambda i: (0, 0)),
                  pl.BlockSpec((N_EXPERTS, 1), lambda i: (0, 0))],
        out_specs=[ospec, ospec, tspec, tspec, tspec],
        out_shape=[jax.ShapeDtypeStruct((TOP_K, T), F32), jax.ShapeDtypeStruct((TOP_K, T), I32),
                   tshape, tshape, tshape],
        scratch_shapes=[pltpu.VMEM((8, LANES), F32)],
        compiler_params=_cparams(("arbitrary",)),
        name="moe_router",
    )(h2, router_w.T, router_b.reshape(N_EXPERTS, 1))
    table = lambda a: a[:, 0, :N_EXPERTS]
    return wt, lpos, table(tlen), table(tstart), table(tcar)


def _dispatch_kernel(len_ref, lst_ref, car_ref, seg_ref, lpos_ref, h_ref, xs_ref, xloc, sem):
    i = pl.program_id(0)
    tm = h_ref.shape[0]
    nloc = xloc.shape[0]
    R = RUN_ROWS
    lpos = lpos_ref[...]
    hb = h_ref[...].astype(BF16)
    rb = 256
    off = _iota((rb, tm), 0).astype(F32).astype(BF16)
    one = jnp.ones((rb, tm), BF16)
    for blk in range(nloc // rb):
        cand = jnp.where(lpos // rb == blk, lpos % rb, -1).astype(F32).astype(BF16)
        owner = jnp.zeros((rb, tm), BF16)
        for kk in range(TOP_K):
            owner = jnp.where(off == cand[kk:kk + 1, :], one, owner)
        rows = _dot(owner, hb)
        bits = lax.bitcast_convert_type(rows, jnp.uint32)
        xloc[blk * rb:(blk + 1) * rb, :] = bits[:, 0:D_XPACK] | (bits[:, D_XPACK:D_MODEL] >> 16)

    def piece(src_row, dst_row):
        return pltpu.make_async_copy(xloc.at[pl.ds(pl.multiple_of(src_row, R), R), :],
                                     xs_ref.at[pl.ds(pl.multiple_of(dst_row, R), R), :], sem)

    def per_expert(e, issued):
        pieces = len_ref[i, e] // R
        ls = lst_ref[i, e]
        gd = seg_ref[e] + car_ref[i, e]

        def issue(j, c):
            piece(ls + j * R, gd + j * R).start()
            return c

        lax.fori_loop(0, pieces, issue, 0)
        return issued + pieces

    total = lax.fori_loop(0, N_EXPERTS, per_expert, 0)

    def wait(j, c):
        piece(0, 0).wait()
        return c

    lax.fori_loop(0, total, wait, 0)


def _dispatch(h2, lpos, tile_len, tile_lstart, tile_car, seg_start, *, tm):
    T = h2.shape[0]
    nt = T // tm
    n_rows = T * TOP_K + N_EXPERTS * nt * RUN_ROWS
    nloc = tm * TOP_K + N_EXPERTS * RUN_ROWS
    return pl.pallas_call(
        _dispatch_kernel,
        grid_spec=pltpu.PrefetchScalarGridSpec(
            num_scalar_prefetch=4,
            grid=(nt,),
            in_specs=[pl.BlockSpec((TOP_K, tm), lambda i, *_: (0, i)),
                      pl.BlockSpec((tm, D_MODEL), lambda i, *_: (i, 0))],
            out_specs=pl.BlockSpec(memory_space=pl.ANY),
            scratch_shapes=[pltpu.VMEM((nloc, D_XPACK), jnp.uint32), pltpu.SemaphoreType.DMA(())]),
        out_shape=jax.ShapeDtypeStruct((n_rows, D_XPACK), jnp.uint32),
        compiler_params=_cparams(("arbitrary",)),
        name="moe_dispatch",
    )(tile_len, tile_lstart, tile_car, seg_start, lpos, h2)


def _expert_kernel(blk_ref, exp_ref, lo_ref, hi_ref, first_ref, x_ref, w1_ref, w3_ref, w2_ref, o_ref):
    w = pl.program_id(0)
    packed = x_ref[...]
    x_a = lax.bitcast_convert_type(packed & jnp.uint32(0xFFFF0000), F32).astype(BF16)
    x_b = lax.bitcast_convert_type(packed << 16, F32).astype(BF16)
    w1 = w1_ref[...].astype(BF16)
    w3 = w3_ref[...].astype(BF16)
    a = _dot(x_a, w1[0:D_XPACK]) + _dot(x_b, w1[D_XPACK:D_MODEL])
    b = _dot(x_a, w3[0:D_XPACK]) + _dot(x_b, w3[D_XPACK:D_MODEL])
    y = _dot((_silu(a) * b).astype(BF16), w2_ref[...].astype(BF16))
    r = _iota(y.shape, 0)
    y = jnp.where((r >= lo_ref[w]) & (r < hi_ref[w]), y, 0.0)

    @pl.when(first_ref[w] == 1)
    def _():
        o_ref[...] = y

    @pl.when(first_ref[w] == 0)
    def _():
        o_ref[...] = o_ref[...] + y


def _experts(xs, work, w1, w3, w2, *, layer, bm):
    A = xs.shape[0]
    n_work = work[0].shape[0]
    xmap = lambda w, blk, ex, lo, hi, first: (blk[w], 0)
    wmap = lambda w, blk, ex, lo, hi, first: (layer, ex[w], 0, 0)
    return pl.pallas_call(
        _expert_kernel,
        grid_spec=pltpu.PrefetchScalarGridSpec(
            num_scalar_prefetch=5,
            grid=(n_work,),
            in_specs=[pl.BlockSpec((bm, D_XPACK), xmap),
                      pl.BlockSpec((None, None, D_MODEL, D_EXPERT), wmap),
                      pl.BlockSpec((None, None, D_MODEL, D_EXPERT), wmap),
                      pl.BlockSpec((None, None, D_EXPERT, D_MODEL), wmap)],
            out_specs=pl.BlockSpec((bm, D_MODEL), xmap)),
        out_shape=jax.ShapeDtypeStruct((A, D_MODEL), F32),
        compiler_params=_cparams(("arbitrary",)),
        name="moe_experts",
    )(*work, xs, w1, w3, w2)


def _combine_kernel(len_ref, lst_ref, car_ref, seg_ref, lpos_ref, wt_ref, ys_ref, h_ref, x_ref, mod_ref,
                    s1_ref, s3_ref, s2_ref, fg_ref, o_ref, yloc, sem, *, final_norm):
    i = pl.program_id(0)
    tm = h_ref.shape[0]
    nloc = yloc.shape[0]
    R = RUN_ROWS

    @pl.when(i == 0)
    def _():
        yloc[...] = jnp.zeros_like(yloc)

    def piece(src_row, dst_row):
        return pltpu.make_async_copy(ys_ref.at[pl.ds(pl.multiple_of(src_row, R), R), :],
                                     yloc.at[pl.ds(pl.multiple_of(dst_row, R), R), :], sem)

    def per_expert(e, issued):
        pieces = len_ref[i, e] // R
        ls = lst_ref[i, e]
        gd = seg_ref[e] + car_ref[i, e]

        def issue(j, c):
            piece(gd + j * R, ls + j * R).start()
            return c

        lax.fori_loop(0, pieces, issue, 0)
        return issued + pieces

    total = lax.fori_loop(0, N_EXPERTS, per_expert, 0)

    hb = h_ref[...].astype(BF16)
    mid = _silu(_dot(hb, s1_ref[...])) * _dot(hb, s3_ref[...])
    y = _dot(mid.astype(BF16), s2_ref[...])

    def wait(j, c):
        piece(0, 0).wait()
        return c

    lax.fori_loop(0, total, wait, 0)

    lpos = lpos_ref[...]
    wt = wt_ref[...]
    used_rows = lst_ref[i, N_EXPERTS - 1] + len_ref[i, N_EXPERTS - 1]
    rb = 256

    def block_sum(blk):
        r = _iota((rb, tm), 0) + blk * rb
        pw = jnp.zeros((rb, tm), F32)
        for kk in range(TOP_K):
            pw = jnp.where(r == lpos[kk:kk + 1, :], wt[kk:kk + 1, :], pw)
        ysc = yloc[blk * rb:(blk + 1) * rb, :] * jnp.sum(pw, axis=1, keepdims=True)
        hi = ysc.astype(BF16)
        lo = (ysc - hi.astype(F32)).astype(BF16)
        owner = (pw != 0.0).astype(BF16)
        return _dot_tn(owner, hi) + _dot_tn(owner, lo)

    for blk in range(nloc // rb):
        if (blk + 1) * rb <= tm * TOP_K:
            y = y + block_sum(blk)
        else:
            y = y + lax.cond(used_rows > blk * rb, functools.partial(block_sum, blk),
                             lambda: jnp.zeros((tm, D_MODEL), F32))
    x = x_ref[...] + mod_ref[5:6, :] * y
    if final_norm:
        x = x * lax.rsqrt(jnp.mean(x * x, axis=-1, keepdims=True) + EPS) * fg_ref[...]
    o_ref[...] = x


def _combine(ys, lpos, wt, tile_len, tile_lstart, tile_car, seg_start, h2, x, mod, s1, s3, s2, final_g,
             *, tm, tokens_per_mod, final_norm):
    T = x.shape[0]
    tiles_per_mod = tokens_per_mod // tm
    nloc = tm * TOP_K + N_EXPERTS * RUN_ROWS
    xspec = pl.BlockSpec((tm, D_MODEL), lambda i, *_: (i, 0))
    kspec = pl.BlockSpec((TOP_K, tm), lambda i, *_: (0, i))
    const = lambda i, *_: (0, 0)
    return pl.pallas_call(
        functools.partial(_combine_kernel, final_norm=final_norm),
        grid_spec=pltpu.PrefetchScalarGridSpec(
            num_scalar_prefetch=4,
            grid=(T // tm,),
            in_specs=[kspec, kspec,
                      pl.BlockSpec(memory_space=pl.ANY),
                      xspec, xspec,
                      pl.BlockSpec((None, 6, D_MODEL), lambda i, *_: (i // tiles_per_mod, 0, 0)),
                      pl.BlockSpec((D_MODEL, D_SHARED), const),
                      pl.BlockSpec((D_MODEL, D_SHARED), const),
                      pl.BlockSpec((D_SHARED, D_MODEL), const),
                      pl.BlockSpec((1, D_MODEL), const)],
            out_specs=xspec,
            scratch_shapes=[pltpu.VMEM((nloc, D_MODEL), F32), pltpu.SemaphoreType.DMA(())]),
        out_shape=jax.ShapeDtypeStruct((T, D_MODEL), F32),
        compiler_params=_cparams(("arbitrary",)),
        name="moe_combine",
    )(tile_len, tile_lstart, tile_car, seg_start, lpos, wt, ys, h2, x, mod, s1, s3, s2, final_g.reshape(1, D_MODEL))


def _work_list(counts, starts, n_rows, bm):
    E = N_EXPERTS
    n_blocks = n_rows // bm
    n_work = n_blocks + E - 1
    ends = starts + counts
    first_blk = starts // bm
    last_blk = jnp.where(counts > 0, (ends - 1) // bm, first_blk)
    n_items = jnp.where(counts > 0, last_blk - first_blk + 1, 0)
    item_end = jnp.cumsum(n_items)
    item_start = item_end - n_items
    w = jnp.arange(n_work, dtype=I32)
    ex = jnp.minimum(jnp.sum((item_end[None, :] <= w[:, None]).astype(I32), axis=1), E - 1)
    valid = w < item_end[-1]
    blk = first_blk[ex] + (w - item_start[ex])
    blk = jnp.where(valid, blk, blk[jnp.maximum(item_end[-1] - 1, 0)]).astype(I32)
    lo = jnp.clip(starts[ex] - blk * bm, 0, bm)
    hi = jnp.clip(ends[ex] - blk * bm, 0, bm)
    lo = jnp.where(valid, lo, 0).astype(I32)
    hi = jnp.where(valid, hi, 0).astype(I32)
    ex = jnp.where(valid, ex, ex[jnp.maximum(item_end[-1] - 1, 0)])
    prev_blk = jnp.concatenate([jnp.full((1,), -1, I32), blk[:-1]])
    first = (blk != prev_blk).astype(I32)
    return blk, ex.astype(I32), lo, hi, first


def _moe(h2, x, mod, p, final_g, *, layer, tokens_per_mod, final_norm):
    T = x.shape[0]
    tm = 256
    wt, lpos, tile_len, tile_lstart, tile_car = _router(h2, p["router_w"], p["router_b"], tm=tm)
    seg_len = tile_car[-1] + tile_len[-1]
    seg_start = jnp.cumsum(seg_len) - seg_len
    xs = _dispatch(h2, lpos, tile_len, tile_lstart, tile_car, seg_start, tm=tm)
    bm = 512
    work = _work_list(seg_len, seg_start, xs.shape[0], bm)
    ys = _experts(xs, work, p["exp_w1"], p["exp_w3"], p["exp_w2"], layer=layer, bm=bm)
    return _combine(ys, lpos, wt, tile_len, tile_lstart, tile_car, seg_start, h2, x, mod,
                    p["sh_w1"], p["sh_w3"], p["sh_w2"], final_g,
                    tm=tm, tokens_per_mod=tokens_per_mod, final_norm=final_norm)


def _reorder_w_in(w_in):
    plain = D_CONV
    pieces = [w_in[..., :plain],
              w_in[..., plain + 3616:plain + 3616 + 4096],
              w_in[..., plain:plain + 3072],
              w_in[..., plain + 3088:plain + 3600],
              w_in[..., plain + 3072:plain + 3088],
              w_in[..., plain + 3600:plain + 3616]]
    out = jnp.concatenate(pieces, axis=-1)
    pad = D_PROJ - out.shape[-1]
    return jnp.pad(out, ((0, 0), (0, 0), (0, pad))).astype(BF16)


def _small_rows(ssd_vals, gdn_vals):
    row = jnp.zeros((LANES,), F32)
    row = row.at[S_DT:S_DT + 2 * SSD_HEADS].set(ssd_vals.reshape(-1))
    row = row.at[S_GA:S_GA + 2 * GDN_HEADS].set(gdn_vals.reshape(-1))
    return row.reshape(1, LANES)


def _layer_pass(x, mod, lp, hy, st_hg, st_ssd, st_gdn, final_g, *, B, L, tokens_per_mod, seg, layer, final_norm):
    T = B * L
    proj = _in_projection(x, mod, lp["norm1_g"], lp["w_in"], lp["conv_w"], lp["conv_b"],
                          layer=layer, tokens_per_mod=tokens_per_mod, seg=seg)
    proj3 = proj.reshape(B, L, D_PROJ)
    fmat, gmat, hspec = hy
    cb = lambda c: c // D_BRANCH
    z = _spectral_conv(fmat, gmat, hspec, 0, lp["hy_bias"][0], proj3, cb(C_HYV), proj3, cb(C_HYX1))
    ya = _spectral_conv(fmat, gmat, hspec, 1, lp["hy_bias"][1], z, 0, proj3, cb(C_HYX2))
    yb, s_hg = _hgrn_scan(proj3, lp["hg_lb"], lp["hg_norm"], st_hg, layer=layer)
    yc, s_ssd = _ssd_scan(proj3, lp["bias_row"], lp["alog_row"], lp["ssd_d"], st_ssd)
    yd, s_gdn = _gdn_scan(proj3, lp["bias_row"], lp["alog_row"], lp["gdn_norm"], st_gdn)
    flat = lambda a: a.reshape(T, D_BRANCH)
    x, h2 = _merge(flat(ya), flat(yb), flat(yc), flat(yd), proj, x, mod, lp["ssd_norm"], lp["norm2_g"],
                   lp["w_branch"], lp["w_out"], tokens_per_mod=tokens_per_mod)
    x = _moe(h2, x, mod, lp, final_g, layer=layer, tokens_per_mod=tokens_per_mod, final_norm=final_norm)
    return x, s_hg, s_ssd, s_gdn


def kernel(x_prompt, x_sample, state_hgrn, state_ssd, state_gdn, c, c_ctx, norm1_g, norm2_g, ada_w, ada_b, w_in, conv_w, conv_b, hy_w1, hy_b1, hy_w2, hy_b2, hy_w3, hy_bias, hg_lb, hg_norm, ssd_a_log, ssd_dt_bias, ssd_d, ssd_norm, gdn_a_log, gdn_dt_bias, gdn_norm, w_branch, w_out, router_w, router_bias, exp_w1, exp_w3, exp_w2, sh_w1, sh_w3, sh_w2, final_g):
    depth = w_in.shape[0]
    bp, lp_len = x_prompt.shape[0], x_prompt.shape[1]
    bs, ls_len = x_sample.shape[0], x_sample.shape[1]
    D = D_MODEL

    w_in_r = _reorder_w_in(w_in)
    hy_w1p = jnp.pad(hy_w1, ((0, 0), (0, LANES - hy_w1.shape[1]), (0, 0)))
    layers = []
    for l in range(depth):
        layers.append(dict(
            norm1_g=norm1_g[l], norm2_g=norm2_g[l], w_in=w_in_r, conv_w=conv_w[l], conv_b=conv_b[l],
            hy_bias=hy_bias[l], hg_lb=hg_lb, hg_norm=hg_norm[l],
            bias_row=_small_rows(ssd_dt_bias[l], gdn_dt_bias[l]),
            alog_row=_small_rows(ssd_a_log[l], gdn_a_log[l]),
            ssd_d=ssd_d[l], ssd_norm=ssd_norm[l], gdn_norm=gdn_norm[l],
            w_branch=w_branch[l].astype(BF16), w_out=w_out[l].astype(BF16),
            router_w=router_w[l], router_b=router_bias[l],
            exp_w1=exp_w1, exp_w3=exp_w3, exp_w2=exp_w2,
            sh_w1=sh_w1[l].astype(BF16), sh_w3=sh_w3[l].astype(BF16), sh_w2=sh_w2[l].astype(BF16)))

    def hyena_setup(L):
        fmat, fs = _dft_matrices(L)
        gmat = fs.T
        specs = []
        for l in range(depth):
            filt = _hyena_filters(L, hy_w1p[l], hy_b1[l], hy_w2[l], hy_b2[l], hy_w3[l])
            specs.append(_filter_spectrum(fmat, filt))
        return fmat, gmat, specs

    cond = jnp.concatenate([c_ctx.reshape(1, D), c], axis=0)
    rows = cond.shape[0]
    rows8 = (rows + 7) // 8 * 8
    cond8 = jnp.pad(cond, ((0, rows8 - rows), (0, 0)))
    mods = [_modulation(cond8, ada_w, ada_b[l], layer=l).reshape(rows8, 6, D) for l in range(depth)]

    fmat, gmat, specs = hyena_setup(lp_len)
    x = x_prompt.reshape(bp * lp_len, D)
    z_hg = jnp.zeros((bp, 2, HG_HEADS, HG_DK, HG_DK), F32)
    z_ssd = jnp.zeros((bp, 2, SSD_HEADS, SSD_P, SSD_N), F32)
    z_gdn = jnp.zeros((bp, 2, GDN_HEADS, GDN_DK, GDN_DK), F32)
    hg_states, ssd_states, gdn_states = [], [], []
    for l in range(depth):
        x, s_hg, s_ssd, s_gdn = _layer_pass(
            x, mods[l][0:1], layers[l], (fmat, gmat, specs[l]), z_hg, z_ssd, z_gdn, final_g,
            B=bp, L=lp_len, tokens_per_mod=bp * lp_len, seg=lp_len, layer=l, final_norm=(l == depth - 1))
        hg_states.append(s_hg)
        ssd_states.append(s_ssd)
        gdn_states.append(s_gdn)
    y_prompt = x.reshape(bp, lp_len, D)
    new_hg = jnp.stack(hg_states, axis=1)
    new_ssd = jnp.stack(ssd_states, axis=1)
    new_gdn = jnp.stack(gdn_states, axis=1)

    fmat, gmat, specs = hyena_setup(ls_len)
    x = x_sample.reshape(bs * ls_len, D)
    for l in range(depth):
        x, _, _, _ = _layer_pass(
            x, mods[l][1:1 + bs], layers[l], (fmat, gmat, specs[l]),
            state_hgrn[:, l], state_ssd[:, l], state_gdn[:, l], final_g,
            B=bs, L=ls_len, tokens_per_mod=ls_len, seg=GRID_W, layer=l, final_norm=(l == depth - 1))
    y_sample = x.reshape(bs, ls_len, D)
    return (y_prompt, y_sample, new_hg, new_ssd, new_gdn)
```

```python
import functools
import math

import jax
import jax.numpy as jnp
from jax import lax
from jax.experimental import pallas as pl
from jax.experimental.pallas import tpu as pltpu

F32 = jnp.float32
BF16 = jnp.bfloat16
I32 = jnp.int32
HIGHEST = lax.Precision.HIGHEST

D_MODEL = 1024
GRID_W = 64
EPS = 1e-6
LOG_FLOOR = 1e-30
MASK_NEG = -1e30
N_BRANCH = 4
D_BRANCH = 512
HY_POS_FREQS = 16
HY_FILTER_HIDDEN = 64
HY_FAST_DECAY = 0.3
HY_SLOW_DECAY = 1.5
HY_DECAY_TARGET = 1e-2
HG_HEADS = 4
HG_DK = 128
HG_CHUNK = 16
SSD_HEADS = 8
SSD_P = 64
SSD_N = 128
SSD_CHUNK = 64
GDN_HEADS = 4
GDN_DK = 128
GDN_CHUNK = 64
N_EXPERTS = 64
TOP_K = 8
N_EXPERT_GROUPS = 8
TOPK_GROUPS = 4
D_EXPERT = 256
D_SHARED = 256
ROUTED_SCALE = 2.5

LANES = 128
D_XPACK = D_MODEL // 2
RUN_ROWS = 8

D_CONV = 4096
C_HYV, C_HYX1, C_HYX2 = 0, 512, 1024
C_SSDX, C_SSDB, C_SSDC = 1536, 2048, 2304
C_GQ, C_GK, C_GV = 2560, 3072, 3584
P_MERGE = 4096
P_HGQ, P_HGFF, P_HGFB, P_HGV, P_HGG = 8192, 8704, 9216, 9728, 10240
P_SSDZ, P_GDNG, P_SMALL = 10752, 11264, 11776
D_PROJ = 12288
S_DT, S_GA, S_GB = 0, 16, 24

VMEM_LIMIT = 56 * 1024 * 1024


def _cparams(sem):
    return pltpu.CompilerParams(dimension_semantics=sem, vmem_limit_bytes=VMEM_LIMIT)


def _sigmoid(x):
    return 1.0 / (1.0 + jnp.exp(-x))


def _silu(x):
    return x * _sigmoid(x)


def _softplus(x):
    return jnp.maximum(x, 0.0) + jnp.log(1.0 + jnp.exp(-jnp.abs(x)))


def _dot(a, b, precision=None):
    return jnp.dot(a, b, preferred_element_type=F32, precision=precision)


def _dot_nt(a, b, precision=None):
    return lax.dot_general(a, b, (((1,), (1,)), ((), ())), preferred_element_type=F32, precision=precision)


def _dot_tn(a, b, precision=None):
    return lax.dot_general(a, b, (((0,), (0,)), ((), ())), preferred_element_type=F32, precision=precision)


def _bdot(a, b):
    return _dot(a.astype(BF16), b.astype(BF16))


def _bdot_nt(a, b):
    return _dot_nt(a.astype(BF16), b.astype(BF16))


def _bdot_tn(a, b):
    return _dot_tn(a.astype(BF16), b.astype(BF16))


def _iota(shape, dim):
    return lax.broadcasted_iota(I32, shape, dim)


def _cumsum_rows(g, reverse):
    n = g.shape[0]
    row = _iota(g.shape, 0)
    sh = 1
    while sh < n:
        if reverse:
            g = g + jnp.where(row < n - sh, pltpu.roll(g, n - sh, 0), 0.0)
        else:
            g = g + jnp.where(row >= sh, pltpu.roll(g, sh, 0), 0.0)
        sh *= 2
    return g


def _lane_pick(a, j):
    return jnp.sum(jnp.where(_iota(a.shape, 1) == j, a, 0.0), axis=1, keepdims=True)


def _split3(a):
    a1 = a.astype(BF16)
    r1 = a - a1.astype(F32)
    a2 = r1.astype(BF16)
    a3 = (r1 - a2.astype(F32)).astype(BF16)
    return a1, a2, a3


def _dot3(a, b):
    a1, a2, _ = _split3(a)
    b1, b2, _ = _split3(b)
    return _dot(a1, b1) + (_dot(a1, b2) + _dot(a2, b1))


def _row_pick(a, j):
    sel = (_iota((8, a.shape[1]), 1) == j).astype(BF16)
    a1, a2, a3 = _split3(a)
    return (_dot_nt(sel, a1) + (_dot_nt(sel, a2) + _dot_nt(sel, a3)))[0:1, :]


def _tri_mask(n, reverse, strict=False):
    t = _iota((n, n), 0)
    s = _iota((n, n), 1)
    if reverse:
        return (s > t) if strict else (s >= t)
    return (s < t) if strict else (s <= t)


def _masked_exp(mask, diff):
    return jnp.where(mask, jnp.exp(jnp.where(mask, diff, 0.0)), 0.0)


def _mod_kernel(c_ref, w_ref, b_ref, o_ref):
    o_ref[...] = _dot(_silu(c_ref[...]), w_ref[...], precision=HIGHEST) + b_ref[...]


def _modulation(cond8, ada_w, ada_b, *, layer):
    rows = cond8.shape[0]
    tn = 1536
    n = ada_w.shape[2]
    return pl.pallas_call(
        _mod_kernel,
        grid=(n // tn,),
        in_specs=[pl.BlockSpec((rows, D_MODEL), lambda j: (0, 0)),
                  pl.BlockSpec((None, D_MODEL, tn), lambda j: (layer, 0, j)),
                  pl.BlockSpec((1, tn), lambda j: (0, j))],
        out_specs=pl.BlockSpec((rows, tn), lambda j: (0, j)),
        out_shape=jax.ShapeDtypeStruct((rows, n), F32),
        compiler_params=_cparams(("arbitrary",)),
        name="adaln_mod",
    )(cond8, ada_w, ada_b.reshape(1, n))


def _inproj_kernel(x_ref, mod_ref, g_ref, w_ref, cw_ref, cb_ref, o_ref, h_scr, *, seg, n_conv_tiles):
    j = pl.program_id(1)

    @pl.when(j == 0)
    def _():
        x = x_ref[...]
        xn = x * lax.rsqrt(jnp.mean(x * x, axis=-1, keepdims=True) + EPS) * g_ref[...]
        m = mod_ref[...]
        h_scr[...] = (xn * (1.0 + m[1:2]) + m[0:1]).astype(BF16)

    y = _dot(h_scr[...], w_ref[...])

    @pl.when(j < n_conv_tiles)
    def _():
        tm = y.shape[0]
        pos = _iota(y.shape, 0) & (seg - 1)
        prev = jnp.where(pos == 0, 0.0, pltpu.roll(y, 1, 0))
        nxt = jnp.where(pos == seg - 1, 0.0, pltpu.roll(y, tm - 1, 0))
        cw = cw_ref[...]
        o_ref[...] = cb_ref[...] + prev * cw[0:1] + y * cw[1:2] + nxt * cw[2:3]

    @pl.when(j >= n_conv_tiles)
    def _():
        o_ref[...] = y


def _in_projection(x, mod, norm_g, w_bf16, conv_w, conv_b, *, layer, tokens_per_mod, seg):
    T = x.shape[0]
    tm = min(2048, tokens_per_mod)
    tn = 1024
    n_conv_tiles = D_CONV // tn
    tiles_per_mod = tokens_per_mod // tm
    kern = functools.partial(_inproj_kernel, seg=seg, n_conv_tiles=n_conv_tiles)
    cmap = lambda i, j: (0, jnp.minimum(j, n_conv_tiles - 1))
    return pl.pallas_call(
        kern,
        grid=(T // tm, D_PROJ // tn),
        in_specs=[pl.BlockSpec((tm, D_MODEL), lambda i, j: (i, 0)),
                  pl.BlockSpec((None, 6, D_MODEL), lambda i, j: (i // tiles_per_mod, 0, 0)),
                  pl.BlockSpec((1, D_MODEL), lambda i, j: (0, 0)),
                  pl.BlockSpec((None, D_MODEL, tn), lambda i, j: (layer, 0, j)),
                  pl.BlockSpec((3, tn), cmap),
                  pl.BlockSpec((1, tn), cmap)],
        out_specs=pl.BlockSpec((tm, tn), lambda i, j: (i, j)),
        out_shape=jax.ShapeDtypeStruct((T, D_PROJ), F32),
        scratch_shapes=[pltpu.VMEM((tm, D_MODEL), BF16)],
        compiler_params=_cparams(("arbitrary", "arbitrary")),
        name="in_proj",
    )(x, mod, norm_g.reshape(1, D_MODEL), w_bf16, conv_w, conv_b.reshape(1, D_CONV))


def _hyfilt_kernel(w1_ref, b1_ref, w2_ref, b2_ref, w3_ref, o_ref, *, L):
    i = pl.program_id(1)
    tl = o_ref.shape[0]
    t = (_iota((tl, LANES), 0) + i * tl).astype(F32) / L
    lane = _iota((tl, LANES), 1)
    band = jnp.where(lane <= HY_POS_FREQS, lane, lane - HY_POS_FREQS).astype(F32)
    ang = 2.0 * math.pi * t * band
    feats = jnp.where(lane == 0, t,
                      jnp.where(lane <= HY_POS_FREQS, jnp.sin(ang),
                                jnp.where(lane <= 2 * HY_POS_FREQS, jnp.cos(ang), 0.0)))
    hdn = jnp.sin(_dot(feats, w1_ref[...], precision=HIGHEST) + b1_ref[...])
    hdn = jnp.sin(_dot(hdn, w2_ref[...], precision=HIGHEST) + b2_ref[...])
    filt = _dot(hdn, w3_ref[...], precision=HIGHEST)
    max_decay = math.log(HY_DECAY_TARGET) / HY_FAST_DECAY
    min_decay = math.log(HY_DECAY_TARGET) / HY_SLOW_DECAY
    n = filt.shape[1]
    ch = (_iota((tl, n), 1) & (D_BRANCH - 1)).astype(F32)
    delta = min_decay + ch * ((max_decay - min_decay) / (D_BRANCH - 1))
    tt = (_iota((tl, n), 0) + i * tl).astype(F32) / L
    o_ref[...] = filt * jnp.exp(-tt * jnp.abs(delta))


def _hyena_filters(L, w1p, b1, w2, b2, w3):
    tl = min(L, 256)
    n = 2 * D_BRANCH
    return pl.pallas_call(
        functools.partial(_hyfilt_kernel, L=L),
        grid=(2, L // tl),
        in_specs=[pl.BlockSpec((LANES, HY_FILTER_HIDDEN), lambda d, i: (0, 0)),
                  pl.BlockSpec((1, HY_FILTER_HIDDEN), lambda d, i: (0, 0)),
                  pl.BlockSpec((HY_FILTER_HIDDEN, HY_FILTER_HIDDEN), lambda d, i: (0, 0)),
                  pl.BlockSpec((1, HY_FILTER_HIDDEN), lambda d, i: (0, 0)),
                  pl.BlockSpec((HY_FILTER_HIDDEN, n), lambda d, i: (0, d))],
        out_specs=pl.BlockSpec((None, tl, n), lambda d, i: (d, i, 0)),
        out_shape=jax.ShapeDtypeStruct((2, L, n), F32),
        compiler_params=_cparams(("arbitrary", "arbitrary")),
        name="hyena_filters",
    )(w1p, b1.reshape(1, -1), w2, b2.reshape(1, -1), w3)


def _dftgen_kernel(f_ref, fs_ref, *, L, tk):
    i = pl.program_id(0)
    N = 2 * L
    k = _iota((tk, LANES), 0) + i * tk
    lane = _iota((tk, LANES), 1)
    w = 2.0 * math.pi / N
    a0 = ((k * lane) & (N - 1)).astype(F32) * w
    c0, s0 = jnp.cos(a0), jnp.sin(a0)
    a1 = ((k * lane * LANES) & (N - 1)).astype(F32) * w
    c1, s1 = jnp.cos(a1), jnp.sin(a1)
    alt = jnp.where((lane & 1) == 0, 1.0, -1.0)
    coef = jnp.where(k == 0, 1.0 / N, 2.0 / N)
    for n1 in range(L // LANES):
        c1b = c1[:, n1:n1 + 1]
        s1b = s1[:, n1:n1 + 1]
        cosb = c1b * c0 - s1b * s0
        sinb = jnp.where(k == 0, alt, -(s1b * c0 + c1b * s0))
        cols = slice(n1 * LANES, (n1 + 1) * LANES)
        f_ref[0:tk, cols] = cosb.astype(BF16)
        f_ref[tk:2 * tk, cols] = sinb.astype(BF16)
        fs_ref[0:tk, cols] = (coef * cosb).astype(BF16)
        fs_ref[tk:2 * tk, cols] = (coef * sinb).astype(BF16)


def _dft_matrices(L):
    tk = min(L, 256)
    spec = pl.BlockSpec((2 * tk, L), lambda i: (i, 0))
    return pl.pallas_call(
        functools.partial(_dftgen_kernel, L=L, tk=tk),
        grid=(L // tk,),
        in_specs=[],
        out_specs=[spec, spec],
        out_shape=[jax.ShapeDtypeStruct((2 * L, L), BF16)] * 2,
        compiler_params=_cparams(("arbitrary",)),
        name="dft_matrices",
    )()


def _dfth_kernel(f_ref, h_ref, o_ref, hp_scr, *, tk):
    i = pl.program_id(1)
    n = o_ref.shape[1]

    @pl.when(i == 0)
    def _():
        hc = h_ref[0]
        ha = jnp.where(_iota(hc.shape, 0) == 0, 0.0, h_ref[1])
        hp_scr[:, 0:n] = (hc + ha).astype(BF16)
        hp_scr[:, n:2 * n] = (hc - ha).astype(BF16)

    u = _dot(f_ref[...], hp_scr[...])
    r = _iota((2 * tk, n), 0)
    from_sum = (r < tk) | ((r == tk) & (i == 0))
    o_ref[...] = jnp.where(from_sum, u[:, 0:n], u[:, n:2 * n])


def _filter_spectrum(fmat, filt):
    L = fmat.shape[1]
    tk = min(L, 256)
    C = filt.shape[2]
    tc = 256
    return pl.pallas_call(
        functools.partial(_dfth_kernel, tk=tk),
        grid=(C // tc, L // tk),
        in_specs=[pl.BlockSpec((2 * tk, L), lambda c, i: (i, 0)),
                  pl.BlockSpec((2, L, tc), lambda c, i: (0, 0, c))],
        out_specs=pl.BlockSpec((2 * tk, tc), lambda c, i: (i, c)),
        out_shape=jax.ShapeDtypeStruct((2 * L, C), F32),
        scratch_shapes=[pltpu.VMEM((L, 2 * tc), BF16)],
        compiler_params=_cparams(("arbitrary", "arbitrary")),
        name="filter_spectrum",
    )(fmat, filt)


def _dfta_kernel(f_ref, u_ref, h_ref, y_ref, u_scr, *, tk):
    i = pl.program_id(1)

    @pl.when(i == 0)
    def _():
        u_scr[...] = u_ref[...].astype(BF16)

    uf = _dot(f_ref[...], u_scr[...])
    ur, ui = uf[0:tk], uf[tk:2 * tk]
    hr, hi = h_ref[0:tk, :], h_ref[tk:2 * tk, :]
    dc = (_iota(ur.shape, 0) == 0) & (i == 0)
    y_ref[0:tk, :] = jnp.where(dc, ur * hr, ur * hr - ui * hi).astype(BF16)
    y_ref[tk:2 * tk, :] = jnp.where(dc, ui * hi, ur * hi + ui * hr).astype(BF16)


def _dftb_kernel(g_ref, y_ref, u_ref, x_ref, b_ref, o_ref):
    y = _dot(g_ref[...], y_ref[...])
    o_ref[...] = x_ref[...] * (y + u_ref[...] * b_ref[...])


def _spectral_conv(fmat, gmat, hspec, h_col, bias, u_arr, u_col, gate_arr, gate_col):
    B, L = u_arr.shape[0], u_arr.shape[1]
    C = D_BRANCH
    tk = min(L, 256)
    y = pl.pallas_call(
        functools.partial(_dfta_kernel, tk=tk),
        grid=(B, L // tk),
        in_specs=[pl.BlockSpec((2 * tk, L), lambda b, i: (i, 0)),
                  pl.BlockSpec((None, L, C), lambda b, i: (b, 0, u_col)),
                  pl.BlockSpec((2 * tk, C), lambda b, i: (i, h_col))],
        out_specs=pl.BlockSpec((None, 2 * tk, C), lambda b, i: (b, i, 0)),
        out_shape=jax.ShapeDtypeStruct((B, 2 * L, C), BF16),
        scratch_shapes=[pltpu.VMEM((L, C), BF16)],
        compiler_params=_cparams(("arbitrary", "arbitrary")),
        name="hyena_dft_fwd",
    )(fmat, u_arr, hspec)
    tr = min(L, 256)
    return pl.pallas_call(
        _dftb_kernel,
        grid=(B, L // tr),
        in_specs=[pl.BlockSpec((tr, 2 * L), lambda b, i: (i, 0)),
                  pl.BlockSpec((None, 2 * L, C), lambda b, i: (b, 0, 0)),
                  pl.BlockSpec((None, tr, C), lambda b, i: (b, i, u_col)),
                  pl.BlockSpec((None, tr, C), lambda b, i: (b, i, gate_col)),
                  pl.BlockSpec((1, C), lambda b, i: (0, 0))],
        out_specs=pl.BlockSpec((None, tr, C), lambda b, i: (b, i, 0)),
        out_shape=jax.ShapeDtypeStruct((B, L, C), F32),
        compiler_params=_cparams(("arbitrary", "arbitrary")),
        name="hyena_dft_inv",
    )(gmat, y, u_arr, gate_arr, bias.reshape(1, C))


def _hgrn_kernel(q_ref, ff_ref, fb_ref, v_ref, g_ref, lb_ref, nrm_ref, s0_ref, y_ref, sf_ref,
                 of_scr, ob_scr, st_scr, *, L, layer, depth):
    C = HG_CHUNK
    W = HG_DK
    nc = L // C
    nh = q_ref.shape[1] // W
    ridx = _iota((C, W), 0)
    o_scrs = (of_scr, ob_scr)
    f_refs = (ff_ref, fb_ref)

    def lower_bound(d, cols):
        rows = [lb_ref[d, l:l + 1, cols] for l in range(depth)]
        m = rows[0]
        for r in rows[1:]:
            m = jnp.maximum(m, r)
        es = [jnp.exp(r - m) for r in rows]
        tot = es[0]
        for e in es[1:]:
            tot = tot + e
        acc = es[0] / tot
        for e in es[1:layer + 1]:
            acc = acc + e / tot
        return acc - es[0] / tot

    lbs = [[lower_bound(d, slice(hh * W, (hh + 1) * W)) for hh in range(nh)] for d in (0, 1)]
    for d in (0, 1):
        for hh in range(nh):
            st_scr[d, hh] = s0_ref[d, hh].T

    def step(c, d, hh):
        rev = d == 1
        rows = pl.ds(pl.multiple_of(c * C, C), C)
        cols = slice(hh * W, (hh + 1) * W)
        lb = lbs[d][hh]
        q = _silu(q_ref[rows, cols])
        uf = f_refs[d][rows, cols]
        v = v_ref[rows, cols]
        f = lb + (1.0 - lb) * _sigmoid(uf)
        g = jnp.log(jnp.maximum(f, LOG_FLOOR))
        kin = (1.0 - lb) * _sigmoid(-uf)
        b = _cumsum_rows(g, rev)
        st = st_scr[d, hh]
        o = _bdot_nt(q * jnp.exp(b), st)
        intra = jnp.zeros((C, W), F32)
        for t in range(C):
            mask = (ridx >= t) if rev else (ridx <= t)
            pair = _masked_exp(mask, b[t:t + 1, :] - b)
            a = jnp.sum(pair * (q[t:t + 1, :] * kin), axis=1, keepdims=True)
            row = jnp.sum(a * v, axis=0, keepdims=True)
            intra = jnp.where(ridx == t, row, intra)
        b_end = b[0:1, :] if rev else b[C - 1:C, :]
        st_scr[d, hh] = st * jnp.exp(b_end) + _bdot_tn(v, kin * jnp.exp(b_end - b))
        o_scrs[d][rows, cols] = o + intra

    def body(ci, carry):
        for d in (0, 1):
            for hh in range(nh):
                step((nc - 1 - ci) if d == 1 else ci, d, hh)
        return carry

    lax.fori_loop(0, nc, body, 0)
    for d in (0, 1):
        for hh in range(nh):
            sf_ref[d, hh] = st_scr[d, hh].T

    tr = min(L, 256)

    def fin(i, carry):
        rows = pl.ds(pl.multiple_of(i * tr, tr), tr)
        for hh in range(nh):
            cols = slice(hh * W, (hh + 1) * W)
            o = of_scr[rows, cols] + ob_scr[rows, cols]
            on = o * lax.rsqrt(jnp.mean(o * o, axis=-1, keepdims=True) + EPS) * nrm_ref[...]
            y_ref[rows, cols] = on * _silu(g_ref[rows, cols])
        return carry

    lax.fori_loop(0, L // tr, fin, 0)


def _hgrn_scan(proj3, hg_lb, hg_norm, s0, *, layer):
    B, L = proj3.shape[0], proj3.shape[1]
    depth = hg_lb.shape[1]
    nh = 2
    W = nh * HG_DK
    col = lambda base: (lambda b, h: (b, 0, base // W + h))
    st_spec = pl.BlockSpec((None, 2, nh, HG_DK, HG_DK), lambda b, h: (b, 0, h, 0, 0))
    return pl.pallas_call(
        functools.partial(_hgrn_kernel, L=L, layer=layer, depth=depth),
        grid=(B, HG_HEADS // nh),
        in_specs=[pl.BlockSpec((None, L, W), col(P_HGQ)),
                  pl.BlockSpec((None, L, W), col(P_HGFF)),
                  pl.BlockSpec((None, L, W), col(P_HGFB)),
                  pl.BlockSpec((None, L, W), col(P_HGV)),
                  pl.BlockSpec((None, L, W), col(P_HGG)),
                  pl.BlockSpec((2, depth, W), lambda b, h: (0, 0, h)),
                  pl.BlockSpec((1, HG_DK), lambda b, h: (0, 0)),
                  st_spec],
        out_specs=[pl.BlockSpec((None, L, W), lambda b, h: (b, 0, h)), st_spec],
        out_shape=[jax.ShapeDtypeStruct((B, L, D_BRANCH), F32),
                   jax.ShapeDtypeStruct(s0.shape, F32)],
        scratch_shapes=[pltpu.VMEM((L, W), F32), pltpu.VMEM((L, W), F32),
                        pltpu.VMEM((2, nh, HG_DK, HG_DK), F32)],
        compiler_params=_cparams(("arbitrary", "arbitrary")),
        name="hgrn2_scan",
    )(proj3, proj3, proj3, proj3, proj3, hg_lb, hg_norm.reshape(1, HG_DK), s0)


def _ssd_kernel(x_ref, bm_ref, cm_ref, z_ref, sm_ref, bias_ref, alog_ref, dskip_ref, s0_ref,
                y_ref, sf_ref, of_scr, ob_scr, st_scr, *, L):
    C = SSD_CHUNK
    W = LANES
    nc = L // C
    npairs = x_ref.shape[1] // W
    ngroups = bm_ref.shape[1] // W
    grp = pl.program_id(1)
    lane_lo = _iota((C, W), 1) < SSD_P
    row_lo = _iota((W, SSD_N), 0) < SSD_P
    pick2 = lambda a0, a1: jnp.where(lane_lo, a0, a1)
    o_scrs = (of_scr, ob_scr)
    incls = (_tri_mask(C, False), _tri_mask(C, True))
    for d in (0, 1):
        for pp in range(npairs):
            st_scr[d, pp] = s0_ref[d, pp]

    heads = [(d, pp, hh) for d in (0, 1) for pp in range(npairs) for hh in (0, 1)]
    pairs = [(d, pp) for d in (0, 1) for pp in range(npairs)]

    def body(ci, carry):
        rows_d, bm, cm, dt_all, cum_all, gram = [], {}, {}, [], [], {}
        for d in (0, 1):
            c = (nc - 1 - ci) if d == 1 else ci
            rows = pl.ds(pl.multiple_of(c * C, C), C)
            rows_d.append(rows)
            for g in range(ngroups):
                bm[d, g] = _silu(bm_ref[rows, g * W:(g + 1) * W])
                cm[d, g] = _silu(cm_ref[rows, g * W:(g + 1) * W])
            dt_all.append(_softplus(sm_ref[rows, :] + bias_ref[...]))
            cum_all.append(_cumsum_rows(-jnp.exp(alog_ref[...]) * dt_all[d], d == 1))
        lane_of = lambda u: S_DT + u[0] * SSD_HEADS + 2 * (npairs * grp + u[1]) + u[2]
        grp_of = lambda u: (u[0], u[1] // (npairs // ngroups))
        cumr = {u: _row_pick(cum_all[u[0]], lane_of(u)) for u in heads}
        for key in bm:
            gram[key] = _bdot_nt(cm[key], bm[key])
        dtc = {u: _lane_pick(dt_all[u[0]], lane_of(u)) for u in heads}
        cumc = {u: _lane_pick(cum_all[u[0]], lane_of(u)) for u in heads}
        end = {u: (cumc[u][0:1, :] if u[0] == 1 else cumc[u][C - 1:C, :]) for u in heads}
        xs = {pr: _silu(x_ref[rows_d[pr[0]], slice(pr[1] * W, (pr[1] + 1) * W)]) for pr in pairs}
        st = {pr: st_scr[pr[0], pr[1]] for pr in pairs}
        y_state = {u: _bdot_nt(cm[grp_of(u)] * jnp.exp(cumc[u]), st[u[:2]]) for u in heads}
        scores = {u: gram[grp_of(u)] * _masked_exp(incls[u[0]], cumc[u] - cumr[u]) for u in heads}
        y_intra = {}
        for u in heads:
            head_x = jnp.where(lane_lo if u[2] == 0 else ~lane_lo, xs[u[:2]], 0.0) * dtc[u]
            y_intra[u] = _bdot(scores[u], head_x)
        for pr in pairs:
            u0, u1 = pr + (0,), pr + (1,)
            xdt_e = xs[pr] * pick2(dtc[u0] * jnp.exp(end[u0] - cumc[u0]), dtc[u1] * jnp.exp(end[u1] - cumc[u1]))
            st_scr[pr[0], pr[1]] = (st[pr] * jnp.where(row_lo, jnp.exp(end[u0]), jnp.exp(end[u1]))
                                    + _bdot_tn(xdt_e, bm[grp_of(pr)]))
            o_scrs[pr[0]][rows_d[pr[0]], slice(pr[1] * W, (pr[1] + 1) * W)] = pick2(
                y_intra[u0] + y_state[u0], y_intra[u1] + y_state[u1])
        return carry

    lax.fori_loop(0, nc, body, 0)
    for d in (0, 1):
        for pp in range(npairs):
            sf_ref[d, pp] = st_scr[d, pp]

    tr = min(L, 256)

    def fin(i, carry):
        rows = pl.ds(pl.multiple_of(i * tr, tr), tr)
        y = of_scr[rows, :] + ob_scr[rows, :] + _silu(x_ref[rows, :]) * dskip_ref[...]
        y_ref[rows, :] = y * _silu(z_ref[rows, :])
        return carry

    lax.fori_loop(0, L // tr, fin, 0)


def _ssd_scan(proj3, bias_row, alog_row, ssd_d, s0):
    B, L = proj3.shape[0], proj3.shape[1]
    n_groups = 2
    n_pairs = SSD_HEADS // 2
    W = n_pairs * LANES
    s0p = s0.reshape(B, 2, n_pairs, 2 * SSD_P, SSD_N)
    dskip = jnp.repeat(ssd_d, SSD_P).reshape(1, 1, W)
    col = lambda base: (lambda b, g: (b, 0, base // W + g))
    st_spec = pl.BlockSpec((None, 2, n_pairs, LANES, SSD_N), lambda b, g: (b, 0, g, 0, 0))
    y, sf = pl.pallas_call(
        functools.partial(_ssd_kernel, L=L),
        grid=(B, 1),
        in_specs=[pl.BlockSpec((None, L, W), col(C_SSDX)),
                  pl.BlockSpec((None, L, n_groups * LANES), lambda b, g: (b, 0, C_SSDB // (n_groups * LANES))),
                  pl.BlockSpec((None, L, n_groups * LANES), lambda b, g: (b, 0, C_SSDC // (n_groups * LANES))),
                  pl.BlockSpec((None, L, W), col(P_SSDZ)),
                  pl.BlockSpec((None, L, LANES), lambda b, g: (b, 0, P_SMALL // LANES)),
                  pl.BlockSpec((1, LANES), lambda b, g: (0, 0)),
                  pl.BlockSpec((1, LANES), lambda b, g: (0, 0)),
                  pl.BlockSpec((None, 1, W), lambda b, g: (g, 0, 0)),
                  st_spec],
        out_specs=[pl.BlockSpec((None, L, W), lambda b, g: (b, 0, g)), st_spec],
        out_shape=[jax.ShapeDtypeStruct((B, L, D_BRANCH), F32),
                   jax.ShapeDtypeStruct(s0p.shape, F32)],
        scratch_shapes=[pltpu.VMEM((L, W), F32), pltpu.VMEM((L, W), F32),
                        pltpu.VMEM((2, n_pairs, LANES, SSD_N), F32)],
        compiler_params=_cparams(("arbitrary", "arbitrary")),
        name="ssd_scan",
    )(proj3, proj3, proj3, proj3, proj3, bias_row, alog_row, dskip, s0p)
    return y, sf.reshape(s0.shape)


def _l2norm(a):
    return a * lax.rsqrt(jnp.sum(a * a, axis=-1, keepdims=True) + EPS)


def _gdn_kernel(q_ref, k_ref, v_ref, g_ref, sm_ref, bias_ref, alog_ref, nrm_ref, s0_ref,
                y_ref, sf_ref, of_scr, ob_scr, st_scr, *, L):
    C = GDN_CHUNK
    W = GDN_DK
    nc = L // C
    nh = q_ref.shape[1] // W
    hblk = pl.program_id(1)
    o_scrs = (of_scr, ob_scr)
    incls = (_tri_mask(C, False), _tri_mask(C, True))
    stricts = (_tri_mask(C, False, strict=True), _tri_mask(C, True, strict=True))
    for d in (0, 1):
        for hh in range(nh):
            st_scr[d, hh] = s0_ref[d, hh]

    units = [(d, hh) for d in (0, 1) for hh in range(nh)]
    eye = (_iota((C, C), 0) == _iota((C, C), 1)).astype(F32)

    def body(ci, carry):
        rows_d, cum_d, beta_d = [], [], []
        for d in (0, 1):
            c = (nc - 1 - ci) if d == 1 else ci
            rows = pl.ds(pl.multiple_of(c * C, C), C)
            raw = sm_ref[rows, :]
            rows_d.append(rows)
            cum_d.append(_cumsum_rows(-jnp.exp(alog_ref[...]) * _softplus(raw + bias_ref[...]), d == 1))
            beta_d.append(_sigmoid(raw))
        q, k, v, gc, grow, beta = {}, {}, {}, {}, {}, {}
        for u in units:
            d, hh = u
            cols = slice(hh * W, (hh + 1) * W)
            h = nh * hblk + hh
            q[u] = _l2norm(_silu(q_ref[rows_d[d], cols])) * (GDN_DK ** -0.5)
            k[u] = _l2norm(_silu(k_ref[rows_d[d], cols]))
            v[u] = _silu(v_ref[rows_d[d], cols])
            gc[u] = _lane_pick(cum_d[d], S_GA + d * GDN_HEADS + h)
            beta[u] = _lane_pick(beta_d[d], S_GB + d * GDN_HEADS + h)
        for u in units:
            grow[u] = _row_pick(cum_d[u[0]], S_GA + u[0] * GDN_HEADS + nh * hblk + u[1])
        decay = {u: _masked_exp(incls[u[0]], gc[u] - grow[u]) for u in units}
        kb = {u: k[u] * beta[u] for u in units}
        m = {u: -jnp.where(stricts[u[0]], _bdot_nt(kb[u], k[u]) * decay[u], 0.0) for u in units}
        aqk = {u: _bdot_nt(q[u], k[u]) * decay[u] for u in units}
        p = {u: eye + m[u] for u in units}
        m = {u: _dot3(m[u], m[u]) for u in units}
        sh = 4
        while sh < C:
            both = {u: _dot3(jnp.concatenate([m[u], p[u]], axis=0), m[u]) for u in units}
            p = {u: p[u] + both[u][C:2 * C] for u in units}
            m = {u: both[u][0:C] for u in units}
            sh *= 2
        p = {u: p[u] + _dot3(p[u], m[u]) for u in units}
        rhs = {u: jnp.concatenate([v[u] * beta[u], kb[u] * jnp.exp(gc[u])], axis=1) for u in units}
        uw = {u: _dot3(p[u], rhs[u]) for u in units}
        uu = {u: uw[u][:, 0:W] for u in units}
        ww = {u: uw[u][:, W:2 * W] for u in units}
        st = {u: st_scr[u[0], u[1]] for u in units}
        v_new = {u: uu[u] - _bdot(ww[u], st[u]) for u in units}
        o = {u: _bdot(q[u] * jnp.exp(gc[u]), st[u]) + _bdot(aqk[u], v_new[u]) for u in units}
        for u in units:
            d, hh = u
            g_end = gc[u][0:1, :] if d == 1 else gc[u][C - 1:C, :]
            st_scr[d, hh] = st[u] * jnp.exp(g_end) + _bdot_tn(k[u] * jnp.exp(g_end - gc[u]), v_new[u])
            o_scrs[d][rows_d[d], slice(hh * W, (hh + 1) * W)] = o[u]
        return carry

    lax.fori_loop(0, nc, body, 0)
    for d in (0, 1):
        for hh in range(nh):
            sf_ref[d, hh] = st_scr[d, hh]

    tr = min(L, 256)

    def fin(i, carry):
        rows = pl.ds(pl.multiple_of(i * tr, tr), tr)
        for hh in range(nh):
            cols = slice(hh * W, (hh + 1) * W)
            o = of_scr[rows, cols] + ob_scr[rows, cols]
            on = o * lax.rsqrt(jnp.mean(o * o, axis=-1, keepdims=True) + EPS) * nrm_ref[...]
            y_ref[rows, cols] = on * _silu(g_ref[rows, cols])
        return carry

    lax.fori_loop(0, L // tr, fin, 0)


def _gdn_scan(proj3, bias_row, alog_row, gdn_norm, s0):
    B, L = proj3.shape[0], proj3.shape[1]
    nh = GDN_HEADS
    W = nh * GDN_DK
    col = lambda base: (lambda b, h: (b, 0, base // W + h))
    st_spec = pl.BlockSpec((None, 2, nh, GDN_DK, GDN_DK), lambda b, h: (b, 0, h, 0, 0))
    return pl.pallas_call(
        functools.partial(_gdn_kernel, L=L),
        grid=(B, GDN_HEADS // nh),
        in_specs=[pl.BlockSpec((None, L, W), col(C_GQ)),
                  pl.BlockSpec((None, L, W), col(C_GK)),
                  pl.BlockSpec((None, L, W), col(C_GV)),
                  pl.BlockSpec((None, L, W), col(P_GDNG)),
                  pl.BlockSpec((None, L, LANES), lambda b, h: (b, 0, P_SMALL // LANES)),
                  pl.BlockSpec((1, LANES), lambda b, h: (0, 0)),
                  pl.BlockSpec((1, LANES), lambda b, h: (0, 0)),
                  pl.BlockSpec((1, GDN_DK), lambda b, h: (0, 0)),
                  st_spec],
        out_specs=[pl.BlockSpec((None, L, W), lambda b, h: (b, 0, h)), st_spec],
        out_shape=[jax.ShapeDtypeStruct((B, L, D_BRANCH), F32),
                   jax.ShapeDtypeStruct(s0.shape, F32)],
        scratch_shapes=[pltpu.VMEM((L, W), F32), pltpu.VMEM((L, W), F32),
                        pltpu.VMEM((2, nh, GDN_DK, GDN_DK), F32)],
        compiler_params=_cparams(("arbitrary", "arbitrary")),
        name="gdn_scan",
    )(proj3, proj3, proj3, proj3, proj3, bias_row, alog_row, gdn_norm.reshape(1, GDN_DK), s0)


def _merge_kernel(ya_ref, yb_ref, yc_ref, yd_ref, mg_ref, x_ref, mod_ref, sn_ref, n2_ref, wb_ref, wo_ref,
                  xo_ref, h_ref):
    yc = yc_ref[...]
    half = D_BRANCH // 2
    parts = []
    for gidx in range(2):
        seg = yc[:, gidx * half:(gidx + 1) * half]
        parts.append(seg * lax.rsqrt(jnp.mean(seg * seg, axis=-1, keepdims=True) + EPS)
                     * sn_ref[:, gidx * half:(gidx + 1) * half])
    branches = (ya_ref[...], yb_ref[...], None, yd_ref[...])
    mixed = None
    for n in range(N_BRANCH):
        gate = _sigmoid(mg_ref[:, n * D_MODEL:(n + 1) * D_MODEL])
        if n == 2:
            lifted = (_dot(parts[0].astype(BF16), wb_ref[n, 0:half, :])
                      + _dot(parts[1].astype(BF16), wb_ref[n, half:D_BRANCH, :]))
        else:
            lifted = _dot(branches[n].astype(BF16), wb_ref[n])
        mixed = gate * lifted if mixed is None else mixed + gate * lifted
    m = mod_ref[...]
    x = x_ref[...] + m[2:3] * _dot(mixed.astype(BF16), wo_ref[...])
    xo_ref[...] = x
    xn = x * lax.rsqrt(jnp.mean(x * x, axis=-1, keepdims=True) + EPS) * n2_ref[...]
    h_ref[...] = xn * (1.0 + m[4:5]) + m[3:4]


def _merge(ya, yb, yc, yd, proj, x, mod, ssd_norm, norm2_g, wb_bf16, wo_bf16, *, tokens_per_mod):
    T = x.shape[0]
    tm = min(512, tokens_per_mod)
    tiles_per_mod = tokens_per_mod // tm
    yspec = pl.BlockSpec((tm, D_BRANCH), lambda i: (i, 0))
    xspec = pl.BlockSpec((tm, D_MODEL), lambda i: (i, 0))
    return pl.pallas_call(
        _merge_kernel,
        grid=(T // tm,),
        in_specs=[yspec, yspec, yspec, yspec,
                  pl.BlockSpec((tm, N_BRANCH * D_MODEL), lambda i: (i, P_MERGE // (N_BRANCH * D_MODEL))),
                  xspec,
                  pl.BlockSpec((None, 6, D_MODEL), lambda i: (i // tiles_per_mod, 0, 0)),
                  pl.BlockSpec((1, D_BRANCH), lambda i: (0, 0)),
                  pl.BlockSpec((1, D_MODEL), lambda i: (0, 0)),
                  pl.BlockSpec((N_BRANCH, D_BRANCH, D_MODEL), lambda i: (0, 0, 0)),
                  pl.BlockSpec((D_MODEL, D_MODEL), lambda i: (0, 0))],
        out_specs=[xspec, xspec],
        out_shape=[jax.ShapeDtypeStruct((T, D_MODEL), F32)] * 2,
        compiler_params=_cparams(("arbitrary",)),
        name="merge_outproj",
    )(ya, yb, yc, yd, proj, x, mod, ssd_norm.reshape(1, D_BRANCH), norm2_g.reshape(1, D_MODEL), wb_bf16, wo_bf16)


def _router_kernel(h_ref, rw_ref, rb_ref, wt_ref, lpos_ref, tlen_ref, tstart_ref, tcar_ref, carry):
    i = pl.program_id(0)
    tm = h_ref.shape[0]
    E = N_EXPERTS
    gsz = E // N_EXPERT_GROUPS
    neg_inf = -jnp.inf

    @pl.when(i == 0)
    def _():
        carry[...] = jnp.zeros_like(carry)

    scores = _sigmoid(_dot_nt(rw_ref[...], h_ref[...], precision=HIGHEST))
    biased = scores + rb_ref[...]
    eidx = _iota((E, tm), 0)
    ridx = _iota((gsz, tm), 0)

    slabs = [biased[g * gsz:(g + 1) * gsz, :] for g in range(N_EXPERT_GROUPS)]
    gs = []
    for v in slabs:
        m1 = jnp.max(v, axis=0, keepdims=True)
        i1 = jnp.min(jnp.where(v == m1, ridx, gsz), axis=0, keepdims=True)
        m2 = jnp.max(jnp.where(ridx == i1, neg_inf, v), axis=0, keepdims=True)
        gs.append(m1 + m2)
    masked = []
    for g in range(N_EXPERT_GROUPS):
        rank = jnp.zeros((1, tm), I32)
        for g2 in range(N_EXPERT_GROUPS):
            if g2 == g:
                continue
            ahead = (gs[g2] >= gs[g]) if g2 < g else (gs[g2] > gs[g])
            rank = rank + ahead.astype(I32)
        masked.append(jnp.where(rank < TOPK_GROUPS, slabs[g], MASK_NEG))
    cur = jnp.concatenate(masked, axis=0)

    krow = _iota((TOP_K, tm), 0)
    onehot = jnp.zeros((E, tm), F32)
    picks = []
    wsum = jnp.zeros((1, tm), F32)
    for kk in range(TOP_K):
        m = jnp.max(cur, axis=0, keepdims=True)
        ik = jnp.min(jnp.where(cur == m, eidx, E), axis=0, keepdims=True)
        hit = eidx == ik
        wk = jnp.sum(jnp.where(hit, scores, 0.0), axis=0, keepdims=True)
        cur = jnp.where(hit, neg_inf, cur)
        onehot = onehot + hit.astype(F32)
        picks.append((ik, wk))
        wsum = wsum + wk
    wt_out = jnp.zeros((TOP_K, tm), F32)
    for kk, (ik, wk) in enumerate(picks):
        wt_out = jnp.where(krow == kk, wk / wsum * ROUTED_SCALE, wt_out)

    earlier = (_iota((tm, tm), 0) < _iota((tm, tm), 1)).astype(BF16)
    before = _dot(onehot.astype(BF16), earlier)
    pad = lambda c: (c.astype(I32) + (RUN_ROWS - 1)) // RUN_ROWS * RUN_ROWS
    run_len = pad(jnp.sum(onehot, axis=1, keepdims=True) + jnp.zeros((E, LANES), F32)).astype(F32)
    run_start = _cumsum_rows(run_len, False) - run_len
    local = before + run_start[:, 0:1]
    lpos_out = jnp.zeros((TOP_K, tm), I32)
    for kk, (ik, wk) in enumerate(picks):
        lp = jnp.sum(jnp.where(eidx == ik, local, 0.0), axis=0, keepdims=True)
        lpos_out = jnp.where(krow == kk, lp.astype(I32), lpos_out)
    wt_ref[...] = wt_out
    lpos_ref[...] = lpos_out

    onehot_l = jnp.concatenate([onehot, jnp.zeros((LANES - E, tm), F32)], axis=0).astype(BF16)
    len_row = pad(_dot_nt(jnp.ones((8, tm), BF16), onehot_l)).astype(F32)
    lanes_before = (_iota((LANES, LANES), 0) < _iota((LANES, LANES), 1)).astype(BF16)
    tlen_ref[...] = len_row.astype(I32)
    tstart_ref[...] = _dot(len_row.astype(BF16), lanes_before).astype(I32)
    tcar_ref[...] = carry[...].astype(I32)
    carry[...] = carry[...] + len_row


def _router(h2, router_w, router_b, *, tm):
    T = h2.shape[0]
    nt = T // tm
    ospec = pl.BlockSpec((TOP_K, tm), lambda i: (0, i))
    tspec = pl.BlockSpec((None, 8, LANES), lambda i: (i, 0, 0))
    tshape = jax.ShapeDtypeStruct((nt, 8, LANES), I32)
    wt, lpos, tlen, tstart, tcar = pl.pallas_call(
        _router_kernel,
        grid=(nt,),
        in_specs=[pl.BlockSpec((tm, D_MODEL), lambda i: (i, 0)),
                  pl.BlockSpec((N_EXPERTS, D_MODEL), lambda i: (0, 0)),
                  pl.BlockSpec((N_EXPERTS, 1), lambda i: (0, 0))],
        out_specs=[ospec, ospec, tspec, tspec, tspec],
        out_shape=[jax.ShapeDtypeStruct((TOP_K, T), F32), jax.ShapeDtypeStruct((TOP_K, T), I32),
                   tshape, tshape, tshape],
        scratch_shapes=[pltpu.VMEM((8, LANES), F32)],
        compiler_params=_cparams(("arbitrary",)),
        name="moe_router",
    )(h2, router_w.T, router_b.reshape(N_EXPERTS, 1))
    table = lambda a: a[:, 0, :N_EXPERTS]
    return wt, lpos, table(tlen), table(tstart), table(tcar)


def _dispatch_kernel(len_ref, lst_ref, car_ref, seg_ref, lpos_ref, h_ref, xs_ref, xloc, sem):
    i = pl.program_id(0)
    tm = h_ref.shape[0]
    nloc = xloc.shape[0]
    R = RUN_ROWS
    lpos = lpos_ref[...]
    hb = h_ref[...].astype(BF16)
    rb = 256
    off = _iota((rb, tm), 0).astype(F32).astype(BF16)
    one = jnp.ones((rb, tm), BF16)
    for blk in range(nloc // rb):
        cand = jnp.where(lpos // rb == blk, lpos % rb, -1).astype(F32).astype(BF16)
        owner = jnp.zeros((rb, tm), BF16)
        for kk in range(TOP_K):
            owner = jnp.where(off == cand[kk:kk + 1, :], one, owner)
        rows = _dot(owner, hb)
        bits = lax.bitcast_convert_type(rows, jnp.uint32)
        xloc[blk * rb:(blk + 1) * rb, :] = bits[:, 0:D_XPACK] | (bits[:, D_XPACK:D_MODEL] >> 16)

    def piece(src_row, dst_row):
        return pltpu.make_async_copy(xloc.at[pl.ds(pl.multiple_of(src_row, R), R), :],
                                     xs_ref.at[pl.ds(pl.multiple_of(dst_row, R), R), :], sem)

    def per_expert_pair(ep, issued):
        for prio in (0, 1):
            e = 2 * ep + prio
            pieces = len_ref[i, e] // R
            ls = lst_ref[i, e]
            gd = seg_ref[e] + car_ref[i, e]

            def issue(j, c, ls=ls, gd=gd, prio=prio):
                piece(ls + j * R, gd + j * R).start(priority=prio)
                return c

            lax.fori_loop(0, pieces, issue, 0)
            issued = issued + pieces
        return issued

    total = lax.fori_loop(0, N_EXPERTS // 2, per_expert_pair, 0)

    def wait(j, c):
        piece(0, 0).wait()
        return c

    lax.fori_loop(0, total, wait, 0)


def _dispatch(h2, lpos, tile_len, tile_lstart, tile_car, seg_start, *, tm):
    T = h2.shape[0]
    nt = T // tm
    n_rows = T * TOP_K + N_EXPERTS * nt * RUN_ROWS
    nloc = tm * TOP_K + N_EXPERTS * RUN_ROWS
    return pl.pallas_call(
        _dispatch_kernel,
        grid_spec=pltpu.PrefetchScalarGridSpec(
            num_scalar_prefetch=4,
            grid=(nt,),
            in_specs=[pl.BlockSpec((TOP_K, tm), lambda i, *_: (0, i)),
                      pl.BlockSpec((tm, D_MODEL), lambda i, *_: (i, 0))],
            out_specs=pl.BlockSpec(memory_space=pl.ANY),
            scratch_shapes=[pltpu.VMEM((nloc, D_XPACK), jnp.uint32), pltpu.SemaphoreType.DMA(())]),
        out_shape=jax.ShapeDtypeStruct((n_rows, D_XPACK), jnp.uint32),
        compiler_params=_cparams(("arbitrary",)),
        name="moe_dispatch",
    )(tile_len, tile_lstart, tile_car, seg_start, lpos, h2)


def _expert_kernel(blk_ref, exp_ref, lo_ref, hi_ref, first_ref, x_ref, w1_ref, w3_ref, w2_ref, o_ref):
    w = pl.program_id(0)
    packed = x_ref[...]
    x_a = lax.bitcast_convert_type(packed & jnp.uint32(0xFFFF0000), F32).astype(BF16)
    x_b = lax.bitcast_convert_type(packed << 16, F32).astype(BF16)
    w1 = w1_ref[...].astype(BF16)
    w3 = w3_ref[...].astype(BF16)
    a = _dot(x_a, w1[0:D_XPACK]) + _dot(x_b, w1[D_XPACK:D_MODEL])
    b = _dot(x_a, w3[0:D_XPACK]) + _dot(x_b, w3[D_XPACK:D_MODEL])
    y = _dot((_silu(a) * b).astype(BF16), w2_ref[...].astype(BF16))
    r = _iota(y.shape, 0)
    y = jnp.where((r >= lo_ref[w]) & (r < hi_ref[w]), y, 0.0)

    @pl.when(first_ref[w] == 1)
    def _():
        o_ref[...] = y

    @pl.when(first_ref[w] == 0)
    def _():
        o_ref[...] = o_ref[...] + y


def _experts(xs, work, w1, w3, w2, *, layer, bm):
    A = xs.shape[0]
    n_work = work[0].shape[0]
    xmap = lambda w, blk, ex, lo, hi, first: (blk[w], 0)
    wmap = lambda w, blk, ex, lo, hi, first: (layer, ex[w], 0, 0)
    return pl.pallas_call(
        _expert_kernel,
        grid_spec=pltpu.PrefetchScalarGridSpec(
            num_scalar_prefetch=5,
            grid=(n_work,),
            in_specs=[pl.BlockSpec((bm, D_XPACK), xmap),
                      pl.BlockSpec((None, None, D_MODEL, D_EXPERT), wmap),
                      pl.BlockSpec((None, None, D_MODEL, D_EXPERT), wmap),
                      pl.BlockSpec((None, None, D_EXPERT, D_MODEL), wmap)],
            out_specs=pl.BlockSpec((bm, D_MODEL), xmap)),
        out_shape=jax.ShapeDtypeStruct((A, D_MODEL), F32),
        compiler_params=_cparams(("arbitrary",)),
        name="moe_experts",
    )(*work, xs, w1, w3, w2)


def _combine_kernel(len_ref, lst_ref, car_ref, seg_ref, lpos_ref, wt_ref, ys_ref, h_ref, x_ref, mod_ref,
                    s1_ref, s3_ref, s2_ref, fg_ref, o_ref, yloc, sem, *, final_norm):
    i = pl.program_id(0)
    tm = h_ref.shape[0]
    nloc = yloc.shape[0]
    R = RUN_ROWS

    @pl.when(i == 0)
    def _():
        yloc[...] = jnp.zeros_like(yloc)

    def piece(src_row, dst_row):
        return pltpu.make_async_copy(ys_ref.at[pl.ds(pl.multiple_of(src_row, R), R), :],
                                     yloc.at[pl.ds(pl.multiple_of(dst_row, R), R), :], sem)

    def per_expert_pair(ep, issued):
        for prio in (0, 1):
            e = 2 * ep + prio
            pieces = len_ref[i, e] // R
            ls = lst_ref[i, e]
            gd = seg_ref[e] + car_ref[i, e]

            def issue(j, c, ls=ls, gd=gd, prio=prio):
                piece(gd + j * R, ls + j * R).start(priority=prio)
                return c

            lax.fori_loop(0, pieces, issue, 0)
            issued = issued + pieces
        return issued

    total = lax.fori_loop(0, N_EXPERTS // 2, per_expert_pair, 0)

    hb = h_ref[...].astype(BF16)
    mid = _silu(_dot(hb, s1_ref[...])) * _dot(hb, s3_ref[...])
    y = _dot(mid.astype(BF16), s2_ref[...])

    def wait(j, c):
        piece(0, 0).wait()
        return c

    lax.fori_loop(0, total, wait, 0)

    lpos = lpos_ref[...]
    wt = wt_ref[...]
    used_rows = lst_ref[i, N_EXPERTS - 1] + len_ref[i, N_EXPERTS - 1]
    rb = 256

    def block_sum(blk):
        r = _iota((rb, tm), 0) + blk * rb
        pw = jnp.zeros((rb, tm), F32)
        for kk in range(TOP_K):
            pw = jnp.where(r == lpos[kk:kk + 1, :], wt[kk:kk + 1, :], pw)
        ysc = yloc[blk * rb:(blk + 1) * rb, :] * jnp.sum(pw, axis=1, keepdims=True)
        hi = ysc.astype(BF16)
        lo = (ysc - hi.astype(F32)).astype(BF16)
        owner = (pw != 0.0).astype(BF16)
        return _dot_tn(owner, hi) + _dot_tn(owner, lo)

    for blk in range(nloc // rb):
        if (blk + 1) * rb <= tm * TOP_K:
            y = y + block_sum(blk)
        else:
            y = y + lax.cond(used_rows > blk * rb, functools.partial(block_sum, blk),
                             lambda: jnp.zeros((tm, D_MODEL), F32))
    x = x_ref[...] + mod_ref[5:6, :] * y
    if final_norm:
        x = x * lax.rsqrt(jnp.mean(x * x, axis=-1, keepdims=True) + EPS) * fg_ref[...]
    o_ref[...] = x


def _combine(ys, lpos, wt, tile_len, tile_lstart, tile_car, seg_start, h2, x, mod, s1, s3, s2, final_g,
             *, tm, tokens_per_mod, final_norm):
    T = x.shape[0]
    tiles_per_mod = tokens_per_mod // tm
    nloc = tm * TOP_K + N_EXPERTS * RUN_ROWS
    xspec = pl.BlockSpec((tm, D_MODEL), lambda i, *_: (i, 0))
    kspec = pl.BlockSpec((TOP_K, tm), lambda i, *_: (0, i))
    const = lambda i, *_: (0, 0)
    return pl.pallas_call(
        functools.partial(_combine_kernel, final_norm=final_norm),
        grid_spec=pltpu.PrefetchScalarGridSpec(
            num_scalar_prefetch=4,
            grid=(T // tm,),
            in_specs=[kspec, kspec,
                      pl.BlockSpec(memory_space=pl.ANY),
                      xspec, xspec,
                      pl.BlockSpec((None, 6, D_MODEL), lambda i, *_: (i // tiles_per_mod, 0, 0)),
                      pl.BlockSpec((D_MODEL, D_SHARED), const),
                      pl.BlockSpec((D_MODEL, D_SHARED), const),
                      pl.BlockSpec((D_SHARED, D_MODEL), const),
                      pl.BlockSpec((1, D_MODEL), const)],
            out_specs=xspec,
            scratch_shapes=[pltpu.VMEM((nloc, D_MODEL), F32), pltpu.SemaphoreType.DMA(())]),
        out_shape=jax.ShapeDtypeStruct((T, D_MODEL), F32),
        compiler_params=_cparams(("arbitrary",)),
        name="moe_combine",
    )(tile_len, tile_lstart, tile_car, seg_start, lpos, wt, ys, h2, x, mod, s1, s3, s2, final_g.reshape(1, D_MODEL))


def _work_list(counts, starts, n_rows, bm):
    E = N_EXPERTS
    n_blocks = n_rows // bm
    n_work = n_blocks + E - 1
    ends = starts + counts
    first_blk = starts // bm
    last_blk = jnp.where(counts > 0, (ends - 1) // bm, first_blk)
    n_items = jnp.where(counts > 0, last_blk - first_blk + 1, 0)
    item_end = jnp.cumsum(n_items)
    item_start = item_end - n_items
    w = jnp.arange(n_work, dtype=I32)
    ex = jnp.minimum(jnp.sum((item_end[None, :] <= w[:, None]).astype(I32), axis=1), E - 1)
    valid = w < item_end[-1]
    blk = first_blk[ex] + (w - item_start[ex])
    blk = jnp.where(valid, blk, blk[jnp.maximum(item_end[-1] - 1, 0)]).astype(I32)
    lo = jnp.clip(starts[ex] - blk * bm, 0, bm)
    hi = jnp.clip(ends[ex] - blk * bm, 0, bm)
    lo = jnp.where(valid, lo, 0).astype(I32)
    hi = jnp.where(valid, hi, 0).astype(I32)
    ex = jnp.where(valid, ex, ex[jnp.maximum(item_end[-1] - 1, 0)])
    prev_blk = jnp.concatenate([jnp.full((1,), -1, I32), blk[:-1]])
    first = (blk != prev_blk).astype(I32)
    return blk, ex.astype(I32), lo, hi, first


def _moe(h2, x, mod, p, final_g, *, layer, tokens_per_mod, final_norm):
    T = x.shape[0]
    tm = 256
    wt, lpos, tile_len, tile_lstart, tile_car = _router(h2, p["router_w"], p["router_b"], tm=tm)
    seg_len = tile_car[-1] + tile_len[-1]
    seg_start = jnp.cumsum(seg_len) - seg_len
    xs = _dispatch(h2, lpos, tile_len, tile_lstart, tile_car, seg_start, tm=tm)
    bm = 512
    work = _work_list(seg_len, seg_start, xs.shape[0], bm)
    ys = _experts(xs, work, p["exp_w1"], p["exp_w3"], p["exp_w2"], layer=layer, bm=bm)
    return _combine(ys, lpos, wt, tile_len, tile_lstart, tile_car, seg_start, h2, x, mod,
                    p["sh_w1"], p["sh_w3"], p["sh_w2"], final_g,
                    tm=tm, tokens_per_mod=tokens_per_mod, final_norm=final_norm)


def _reorder_w_in(w_in):
    plain = D_CONV
    pieces = [w_in[..., :plain],
              w_in[..., plain + 3616:plain + 3616 + 4096],
              w_in[..., plain:plain + 3072],
              w_in[..., plain + 3088:plain + 3600],
              w_in[..., plain + 3072:plain + 3088],
              w_in[..., plain + 3600:plain + 3616]]
    out = jnp.concatenate(pieces, axis=-1)
    pad = D_PROJ - out.shape[-1]
    return jnp.pad(out, ((0, 0), (0, 0), (0, pad))).astype(BF16)


def _small_rows(ssd_vals, gdn_vals):
    row = jnp.zeros((LANES,), F32)
    row = row.at[S_DT:S_DT + 2 * SSD_HEADS].set(ssd_vals.reshape(-1))
    row = row.at[S_GA:S_GA + 2 * GDN_HEADS].set(gdn_vals.reshape(-1))
    return row.reshape(1, LANES)


def _layer_pass(x, mod, lp, hy, st_hg, st_ssd, st_gdn, final_g, *, B, L, tokens_per_mod, seg, layer, final_norm):
    T = B * L
    proj = _in_projection(x, mod, lp["norm1_g"], lp["w_in"], lp["conv_w"], lp["conv_b"],
                          layer=layer, tokens_per_mod=tokens_per_mod, seg=seg)
    proj3 = proj.reshape(B, L, D_PROJ)
    fmat, gmat, hspec = hy
    cb = lambda c: c // D_BRANCH
    z = _spectral_conv(fmat, gmat, hspec, 0, lp["hy_bias"][0], proj3, cb(C_HYV), proj3, cb(C_HYX1))
    ya = _spectral_conv(fmat, gmat, hspec, 1, lp["hy_bias"][1], z, 0, proj3, cb(C_HYX2))
    yb, s_hg = _hgrn_scan(proj3, lp["hg_lb"], lp["hg_norm"], st_hg, layer=layer)
    yc, s_ssd = _ssd_scan(proj3, lp["bias_row"], lp["alog_row"], lp["ssd_d"], st_ssd)
    yd, s_gdn = _gdn_scan(proj3, lp["bias_row"], lp["alog_row"], lp["gdn_norm"], st_gdn)
    flat = lambda a: a.reshape(T, D_BRANCH)
    x, h2 = _merge(flat(ya), flat(yb), flat(yc), flat(yd), proj, x, mod, lp["ssd_norm"], lp["norm2_g"],
                   lp["w_branch"], lp["w_out"], tokens_per_mod=tokens_per_mod)
    x = _moe(h2, x, mod, lp, final_g, layer=layer, tokens_per_mod=tokens_per_mod, final_norm=final_norm)
    return x, s_hg, s_ssd, s_gdn


def kernel(x_prompt, x_sample, state_hgrn, state_ssd, state_gdn, c, c_ctx, norm1_g, norm2_g, ada_w, ada_b, w_in, conv_w, conv_b, hy_w1, hy_b1, hy_w2, hy_b2, hy_w3, hy_bias, hg_lb, hg_norm, ssd_a_log, ssd_dt_bias, ssd_d, ssd_norm, gdn_a_log, gdn_dt_bias, gdn_norm, w_branch, w_out, router_w, router_bias, exp_w1, exp_w3, exp_w2, sh_w1, sh_w3, sh_w2, final_g):
    depth = w_in.shape[0]
    bp, lp_len = x_prompt.shape[0], x_prompt.shape[1]
    bs, ls_len = x_sample.shape[0], x_sample.shape[1]
    D = D_MODEL

    w_in_r = _reorder_w_in(w_in)
    hy_w1p = jnp.pad(hy_w1, ((0, 0), (0, LANES - hy_w1.shape[1]), (0, 0)))
    layers = []
    for l in range(depth):
        layers.append(dict(
            norm1_g=norm1_g[l], norm2_g=norm2_g[l], w_in=w_in_r, conv_w=conv_w[l], conv_b=conv_b[l],
            hy_bias=hy_bias[l], hg_lb=hg_lb, hg_norm=hg_norm[l],
            bias_row=_small_rows(ssd_dt_bias[l], gdn_dt_bias[l]),
            alog_row=_small_rows(ssd_a_log[l], gdn_a_log[l]),
            ssd_d=ssd_d[l], ssd_norm=ssd_norm[l], gdn_norm=gdn_norm[l],
            w_branch=w_branch[l].astype(BF16), w_out=w_out[l].astype(BF16),
            router_w=router_w[l], router_b=router_bias[l],
            exp_w1=exp_w1, exp_w3=exp_w3, exp_w2=exp_w2,
            sh_w1=sh_w1[l].astype(BF16), sh_w3=sh_w3[l].astype(BF16), sh_w2=sh_w2[l].astype(BF16)))

    def hyena_setup(L):
        fmat, fs = _dft_matrices(L)
        gmat = fs.T
        specs = []
        for l in range(depth):
            filt = _hyena_filters(L, hy_w1p[l], hy_b1[l], hy_w2[l], hy_b2[l], hy_w3[l])
            specs.append(_filter_spectrum(fmat, filt))
        return fmat, gmat, specs

    cond = jnp.concatenate([c_ctx.reshape(1, D), c], axis=0)
    rows = cond.shape[0]
    rows8 = (rows + 7) // 8 * 8
    cond8 = jnp.pad(cond, ((0, rows8 - rows), (0, 0)))
    mods = [_modulation(cond8, ada_w, ada_b[l], layer=l).reshape(rows8, 6, D) for l in range(depth)]

    fmat, gmat, specs = hyena_setup(lp_len)
    x = x_prompt.reshape(bp * lp_len, D)
    z_hg = jnp.zeros((bp, 2, HG_HEADS, HG_DK, HG_DK), F32)
    z_ssd = jnp.zeros((bp, 2, SSD_HEADS, SSD_P, SSD_N), F32)
    z_gdn = jnp.zeros((bp, 2, GDN_HEADS, GDN_DK, GDN_DK), F32)
    hg_states, ssd_states, gdn_states = [], [], []
    for l in range(depth):
        x, s_hg, s_ssd, s_gdn = _layer_pass(
            x, mods[l][0:1], layers[l], (fmat, gmat, specs[l]), z_hg, z_ssd, z_gdn, final_g,
            B=bp, L=lp_len, tokens_per_mod=bp * lp_len, seg=lp_len, layer=l, final_norm=(l == depth - 1))
        hg_states.append(s_hg)
        ssd_states.append(s_ssd)
        gdn_states.append(s_gdn)
    y_prompt = x.reshape(bp, lp_len, D)
    new_hg = jnp.stack(hg_states, axis=1)
    new_ssd = jnp.stack(ssd_states, axis=1)
    new_gdn = jnp.stack(gdn_states, axis=1)

    fmat, gmat, specs = hyena_setup(ls_len)
    x = x_sample.reshape(bs * ls_len, D)
    for l in range(depth):
        x, _, _, _ = _layer_pass(
            x, mods[l][1:1 + bs], layers[l], (fmat, gmat, specs[l]),
            state_hgrn[:, l], state_ssd[:, l], state_gdn[:, l], final_g,
            B=bs, L=ls_len, tokens_per_mod=ls_len, seg=GRID_W, layer=l, final_norm=(l == depth - 1))
    y_sample = x.reshape(bs, ls_len, D)
    return (y_prompt, y_sample, new_hg, new_ssd, new_gdn)
```
